```python
import jax, jax.numpy as jnp
from jax import lax
import numpy as np

D_MODEL = 1024
BATCH = 16
SEQ = 2048
DEPTH = 4

CHUNK = 64
Q_BLOCK = 128
N_MEM = 256
FOX_HEADS = 8
FOX_HEAD_DIM = 64
MLA_HEADS = 8
MLA_NOPE_DIM = 64
MLA_ROPE_DIM = 32
MLA_V_DIM = 64
MLA_Q_RANK = 384
MLA_KV_RANK = 256
ROPE_BASE = 10000.0
MEM_HEADS = 4
MEM_HEAD_DIM = 128
N_BRANCHES = 3
BRANCH_WIDTH = 512
D_FF = 4 * D_MODEL
DEEPNORM_ALPHA = (2 * DEPTH) ** 0.25
DEEPNORM_BETA = (8 * DEPTH) ** -0.25
LN_EPS = 1e-5
RMS_EPS = 1e-6
NEG_INF = -1e30

IN_SPLITS = (
    3 * FOX_HEADS * FOX_HEAD_DIM,
    FOX_HEADS,
    MLA_Q_RANK,
    MLA_KV_RANK,
    MLA_ROPE_DIM,
    MEM_HEADS * MEM_HEAD_DIM,
    N_BRANCHES * D_MODEL,
)
D_IN = sum(IN_SPLITS)
SPLIT_POINTS = tuple(int(p) for p in np.cumsum(IN_SPLITS)[:-1])

kernel_name = 'hybrid_fox_mla_memory_deepnorm_trunk'


def _layer_norm(x, g, b):
    xf = x.astype(jnp.float32)
    mu = jnp.mean(xf, axis=-1, keepdims=True)
    var = jnp.mean(jnp.square(xf - mu), axis=-1, keepdims=True)
    y = (xf - mu) * lax.rsqrt(var + LN_EPS)
    return (y * g.astype(jnp.float32) + b.astype(jnp.float32)).astype(x.dtype)


def _rms_norm(x, g):
    xf = x.astype(jnp.float32)
    y = xf * lax.rsqrt(jnp.mean(jnp.square(xf), axis=-1, keepdims=True) + RMS_EPS)
    return (y * g.astype(jnp.float32)).astype(x.dtype)


def _rope_tables(positions):
    inv_freq = ROPE_BASE ** (-jnp.arange(0, MLA_ROPE_DIM, 2, dtype=jnp.float32) / MLA_ROPE_DIM)
    ang = positions.astype(jnp.float32)[..., None] * inv_freq
    return jnp.cos(ang), jnp.sin(ang)


def _rope(x, cos, sin):
    half = x.shape[-1] // 2
    x1 = x[..., :half].astype(jnp.float32)
    x2 = x[..., half:].astype(jnp.float32)
    return jnp.concatenate([x1 * cos - x2 * sin, x2 * cos + x1 * sin], axis=-1).astype(x.dtype)


def _swept_attention(q, k, v, bias_fn):
    seq = q.shape[2]
    scale = q.shape[-1] ** -0.5
    outs = []
    for i in range(seq // Q_BLOCK):
        q0 = i * Q_BLOCK
        k_len = q0 + Q_BLOCK
        logits = jnp.einsum('bhqd,bhkd->bhqk', q[:, :, q0:k_len], k[:, :, :k_len]).astype(jnp.float32)
        logits = logits * scale + bias_fn(q0, k_len)
        p = jax.nn.softmax(logits, axis=-1).astype(v.dtype)
        outs.append(jnp.einsum('bhqk,bhkd->bhqd', p, v[:, :, :k_len]))
    return jnp.concatenate(outs, axis=2)


def _chunk_causal_bias(q0, k_len):
    t_chunk = (q0 + jnp.arange(Q_BLOCK)) // CHUNK
    s_chunk = jnp.arange(k_len) // CHUNK
    allowed = s_chunk[None, :] <= t_chunk[:, None]
    return jnp.where(allowed, jnp.float32(0.0), jnp.float32(NEG_INF))[None, None]


def _mixer(h, mem, cos, sin, w_in, b_forget, w_uq, g_cq, w_ukv, g_ckv, w_mem_kv, w_br, w_out):
    B, S, _ = h.shape
    proj = h @ w_in
    fox_qkv, f_logit, c_q, c_kv, k_rope, q_mem, gate_logit = jnp.split(proj, SPLIT_POINTS, axis=-1)

    qkv = fox_qkv.reshape(B, S, 3, FOX_HEADS, FOX_HEAD_DIM).transpose(2, 0, 3, 1, 4)
    log_f = jax.nn.log_sigmoid(f_logit.astype(jnp.float32) + b_forget.astype(jnp.float32))
    cum_f = jnp.cumsum(log_f, axis=1).transpose(0, 2, 1)

    def fox_bias(q0, k_len):
        t = q0 + jnp.arange(Q_BLOCK)
        s = jnp.arange(k_len)
        decay = cum_f[:, :, q0:q0 + Q_BLOCK, None] - cum_f[:, :, None, :k_len]
        return jnp.where(s[None, :] <= t[:, None], decay, jnp.float32(NEG_INF))

    o_a = _swept_attention(qkv[0], qkv[1], qkv[2], fox_bias)
    o_a = o_a.transpose(0, 2, 1, 3).reshape(B, S, BRANCH_WIDTH)

    q_b = (_rms_norm(c_q, g_cq) @ w_uq).reshape(B, S, MLA_HEADS, MLA_NOPE_DIM + MLA_ROPE_DIM)
    q_b = q_b.transpose(0, 2, 1, 3)
    q_nope, q_pe = q_b[..., :MLA_NOPE_DIM], q_b[..., MLA_NOPE_DIM:]
    q_pe = _rope(q_pe, cos[:, None], sin[:, None])
    kv_b = (_rms_norm(c_kv, g_ckv) @ w_ukv).reshape(B, S, MLA_HEADS, MLA_NOPE_DIM + MLA_V_DIM)
    kv_b = kv_b.transpose(0, 2, 1, 3)
    k_nope, v_b = kv_b[..., :MLA_NOPE_DIM], kv_b[..., MLA_NOPE_DIM:]
    k_pe = _rope(k_rope, cos, sin)[:, None]
    q_full = jnp.concatenate([q_nope, q_pe], axis=-1)
    k_full = jnp.concatenate([k_nope, jnp.broadcast_to(k_pe, (B, MLA_HEADS, S, MLA_ROPE_DIM))], axis=-1)
    o_b = _swept_attention(q_full, k_full, v_b, _chunk_causal_bias)
    o_b = o_b.transpose(0, 2, 1, 3).reshape(B, S, BRANCH_WIDTH)

    mkv = (mem @ w_mem_kv).reshape(B, mem.shape[1], 2, MEM_HEADS, MEM_HEAD_DIM).transpose(2, 0, 3, 1, 4)
    qm = q_mem.reshape(B, S, MEM_HEADS, MEM_HEAD_DIM).transpose(0, 2, 1, 3)
    logits_m = jnp.einsum('bhqd,bhmd->bhqm', qm, mkv[0]).astype(jnp.float32) * (MEM_HEAD_DIM ** -0.5)
    p_m = jax.nn.softmax(logits_m, axis=-1).astype(mkv.dtype)
    o_c = jnp.einsum('bhqm,bhmd->bhqd', p_m, mkv[1]).transpose(0, 2, 1, 3).reshape(B, S, BRANCH_WIDTH)

    branches = jnp.stack([o_a, o_b, o_c], axis=2)
    branch_proj = jnp.einsum('bsnc,ncd->bsnd', branches, w_br)
    gates = jax.nn.sigmoid(gate_logit.reshape(B, S, N_BRANCHES, D_MODEL))
    merged = jnp.sum(gates * branch_proj, axis=2)
    return merged @ w_out


def _fwd_setup_inputs(seed: int = 0) -> dict:
    key = jax.random.key(seed)
    ks = jax.random.split(key, 24)
    L = DEPTH

    def w(k, shape, fan_in, scale=1.0):
        return jax.random.normal(k, shape, jnp.float32) * (fan_in ** -0.5) * scale

    def gain(k, shape):
        return 1.0 + 0.02 * jax.random.normal(k, shape, jnp.float32)

    def bias(k, shape):
        return 0.02 * jax.random.normal(k, shape, jnp.float32)

    x = jax.random.normal(ks[0], (BATCH, SEQ, D_MODEL), jnp.float32)
    mem = jax.random.normal(ks[1], (BATCH, N_MEM, D_MODEL), jnp.float32)
    offsets = jax.random.randint(ks[2], (BATCH, 1), 0, 8192, dtype=jnp.int32)
    positions = offsets + jnp.arange(SEQ, dtype=jnp.int32)[None, :]
    return {
        'x': x,
        'mem': mem,
        'positions': positions,
        'ln_in_g': gain(ks[3], (D_MODEL,)),
        'ln_in_b': bias(ks[4], (D_MODEL,)),
        'w_in': w(ks[5], (L, D_MODEL, D_IN), D_MODEL),
        'b_forget': jax.random.uniform(ks[6], (L, FOX_HEADS), jnp.float32, 1.0, 6.0),
        'w_uq': w(ks[7], (L, MLA_Q_RANK, MLA_HEADS * (MLA_NOPE_DIM + MLA_ROPE_DIM)), MLA_Q_RANK),
        'g_cq': gain(ks[8], (L, MLA_Q_RANK)),
        'w_ukv': w(ks[9], (L, MLA_KV_RANK, MLA_HEADS * (MLA_NOPE_DIM + MLA_V_DIM)), MLA_KV_RANK),
        'g_ckv': gain(ks[10], (L, MLA_KV_RANK)),
        'w_mem_kv': w(ks[11], (L, D_MODEL, 2 * MEM_HEADS * MEM_HEAD_DIM), D_MODEL),
        'w_br': w(ks[12], (L, N_BRANCHES, BRANCH_WIDTH, D_MODEL), BRANCH_WIDTH, DEEPNORM_BETA),
        'w_out': w(ks[13], (L, D_MODEL, D_MODEL), D_MODEL, DEEPNORM_BETA),
        'ln1_g': gain(ks[14], (L, D_MODEL)),
        'ln1_b': bias(ks[15], (L, D_MODEL)),
        'w_ff1': w(ks[16], (L, D_MODEL, D_FF), D_MODEL),
        'w_ff2': w(ks[17], (L, D_FF, D_MODEL), D_FF, DEEPNORM_BETA),
        'ln2_g': gain(ks[18], (L, D_MODEL)),
        'ln2_b': bias(ks[19], (L, D_MODEL)),
    }


def _fwd_reference(x, mem, positions, ln_in_g, ln_in_b, w_in, b_forget, w_uq, g_cq, w_ukv, g_ckv,
              w_mem_kv, w_br, w_out, ln1_g, ln1_b, w_ff1, w_ff2, ln2_g, ln2_b):
    cos, sin = _rope_tables(positions)
    h = _layer_norm(x, ln_in_g, ln_in_b)
    for l in range(DEPTH):
        y = _mixer(h, mem, cos, sin, w_in[l], b_forget[l], w_uq[l], g_cq[l], w_ukv[l], g_ckv[l],
                   w_mem_kv[l], w_br[l], w_out[l])
        h = _layer_norm(DEEPNORM_ALPHA * h + y, ln1_g[l], ln1_b[l])
        ff = jnp.square(jax.nn.relu(h @ w_ff1[l])) @ w_ff2[l]
        h = _layer_norm(DEEPNORM_ALPHA * h + ff, ln2_g[l], ln2_b[l])
    return h


import jax as _jax
import jax.numpy as _jnp

TWIN_FORMAT = 'train_step'
FWD_PARAMS = ['x', 'mem', 'positions', 'ln_in_g', 'ln_in_b', 'w_in', 'b_forget', 'w_uq', 'g_cq', 'w_ukv', 'g_ckv', 'w_mem_kv', 'w_br', 'w_out', 'ln1_g', 'ln1_b', 'w_ff1', 'w_ff2', 'ln2_g', 'ln2_b']
TWIN_WEIGHTS = ['ln_in_g', 'ln_in_b', 'w_in', 'b_forget', 'w_uq', 'g_cq', 'w_ukv', 'g_ckv', 'w_mem_kv', 'w_br', 'w_out', 'ln1_g', 'ln1_b', 'w_ff1', 'w_ff2', 'ln2_g', 'ln2_b']
TWIN_DIFF_INPUT = 'x'
TWIN_INPUTS = ['x', 'mem', 'positions', 'ln_in_g', 'ln_in_b', 'w_in', 'b_forget', 'w_uq', 'g_cq', 'w_ukv', 'g_ckv', 'w_mem_kv', 'w_br', 'w_out', 'ln1_g', 'ln1_b', 'w_ff1', 'w_ff2', 'ln2_g', 'ln2_b', 'loss_target', 'm_ln_in_g', 'm_ln_in_b', 'm_w_in', 'm_b_forget', 'm_w_uq', 'm_g_cq', 'm_w_ukv', 'm_g_ckv', 'm_w_mem_kv', 'm_w_br', 'm_w_out', 'm_ln1_g', 'm_ln1_b', 'm_w_ff1', 'm_w_ff2', 'm_ln2_g', 'm_ln2_b', 'v_ln_in_g', 'v_ln_in_b', 'v_w_in', 'v_b_forget', 'v_w_uq', 'v_g_cq', 'v_w_ukv', 'v_g_ckv', 'v_w_mem_kv', 'v_w_br', 'v_w_out', 'v_ln1_g', 'v_ln1_b', 'v_w_ff1', 'v_w_ff2', 'v_ln2_g', 'v_ln2_b']
TWIN_OUTPUTS = ['loss', 'grad_x', 'grad_ln_in_g', 'grad_ln_in_b', 'grad_w_in', 'grad_b_forget', 'grad_w_uq', 'grad_g_cq', 'grad_w_ukv', 'grad_g_ckv', 'grad_w_mem_kv', 'grad_w_br', 'grad_w_out', 'grad_ln1_g', 'grad_ln1_b', 'grad_w_ff1', 'grad_w_ff2', 'grad_ln2_g', 'grad_ln2_b', 'delta_ln_in_g', 'delta_ln_in_b', 'delta_w_in', 'delta_b_forget', 'delta_w_uq', 'delta_g_cq', 'delta_w_ukv', 'delta_g_ckv', 'delta_w_mem_kv', 'delta_w_br', 'delta_w_out', 'delta_ln1_g', 'delta_ln1_b', 'delta_w_ff1', 'delta_w_ff2', 'delta_ln2_g', 'delta_ln2_b', 'new_m_ln_in_g', 'new_m_ln_in_b', 'new_m_w_in', 'new_m_b_forget', 'new_m_w_uq', 'new_m_g_cq', 'new_m_w_ukv', 'new_m_g_ckv', 'new_m_w_mem_kv', 'new_m_w_br', 'new_m_w_out', 'new_m_ln1_g', 'new_m_ln1_b', 'new_m_w_ff1', 'new_m_w_ff2', 'new_m_ln2_g', 'new_m_ln2_b', 'new_v_ln_in_g', 'new_v_ln_in_b', 'new_v_w_in', 'new_v_b_forget', 'new_v_w_uq', 'new_v_g_cq', 'new_v_w_ukv', 'new_v_g_ckv', 'new_v_w_mem_kv', 'new_v_w_br', 'new_v_w_out', 'new_v_ln1_g', 'new_v_ln1_b', 'new_v_w_ff1', 'new_v_w_ff2', 'new_v_ln2_g', 'new_v_ln2_b']
TWIN_LEAF_KINDS = {'loss': 'loss', 'grad_x': 'grad_x', 'grad_ln_in_g': 'grad_w', 'grad_ln_in_b': 'grad_w', 'grad_w_in': 'grad_w', 'grad_b_forget': 'grad_w', 'grad_w_uq': 'grad_w', 'grad_g_cq': 'grad_w', 'grad_w_ukv': 'grad_w', 'grad_g_ckv': 'grad_w', 'grad_w_mem_kv': 'grad_w', 'grad_w_br': 'grad_w', 'grad_w_out': 'grad_w', 'grad_ln1_g': 'grad_w', 'grad_ln1_b': 'grad_w', 'grad_w_ff1': 'grad_w', 'grad_w_ff2': 'grad_w', 'grad_ln2_g': 'grad_w', 'grad_ln2_b': 'grad_w', 'delta_ln_in_g': 'delta_w', 'delta_ln_in_b': 'delta_w', 'delta_w_in': 'delta_w', 'delta_b_forget': 'delta_w', 'delta_w_uq': 'delta_w', 'delta_g_cq': 'delta_w', 'delta_w_ukv': 'delta_w', 'delta_g_ckv': 'delta_w', 'delta_w_mem_kv': 'delta_w', 'delta_w_br': 'delta_w', 'delta_w_out': 'delta_w', 'delta_ln1_g': 'delta_w', 'delta_ln1_b': 'delta_w', 'delta_w_ff1': 'delta_w', 'delta_w_ff2': 'delta_w', 'delta_ln2_g': 'delta_w', 'delta_ln2_b': 'delta_w', 'new_m_ln_in_g': 'new_m', 'new_m_ln_in_b': 'new_m', 'new_m_w_in': 'new_m', 'new_m_b_forget': 'new_m', 'new_m_w_uq': 'new_m', 'new_m_g_cq': 'new_m', 'new_m_w_ukv': 'new_m', 'new_m_g_ckv': 'new_m', 'new_m_w_mem_kv': 'new_m', 'new_m_w_br': 'new_m', 'new_m_w_out': 'new_m', 'new_m_ln1_g': 'new_m', 'new_m_ln1_b': 'new_m', 'new_m_w_ff1': 'new_m', 'new_m_w_ff2': 'new_m', 'new_m_ln2_g': 'new_m', 'new_m_ln2_b': 'new_m', 'new_v_ln_in_g': 'new_v', 'new_v_ln_in_b': 'new_v', 'new_v_w_in': 'new_v', 'new_v_b_forget': 'new_v', 'new_v_w_uq': 'new_v', 'new_v_g_cq': 'new_v', 'new_v_w_ukv': 'new_v', 'new_v_g_ckv': 'new_v', 'new_v_w_mem_kv': 'new_v', 'new_v_w_br': 'new_v', 'new_v_w_out': 'new_v', 'new_v_ln1_g': 'new_v', 'new_v_ln1_b': 'new_v', 'new_v_w_ff1': 'new_v', 'new_v_w_ff2': 'new_v', 'new_v_ln2_g': 'new_v', 'new_v_ln2_b': 'new_v'}


def _forward(args):
    return _fwd_reference(*[args[k] for k in FWD_PARAMS])


def _output_shape():
    out = _jax.eval_shape(lambda: _forward(_fwd_setup_inputs(0)))
    return out.shape, out.dtype

N_MICROBATCH = 1
ADAM_LR = 0.001
ADAM_B1 = 0.9
ADAM_B2 = 0.999
ADAM_EPS = 1e-08
ADAM_WD = 0.01
ADAM_STEP = 10
PER_EXAMPLE_BATCH_AXIS = {'x': 0, 'mem': 0, 'positions': 0, 'loss_target': 0}
SHARED_INPUTS = []
_WEIGHT_DTYPES = {'ln_in_g': _jnp.float32, 'ln_in_b': _jnp.float32, 'w_in': _jnp.float32, 'b_forget': _jnp.float32, 'w_uq': _jnp.float32, 'g_cq': _jnp.float32, 'w_ukv': _jnp.float32, 'g_ckv': _jnp.float32, 'w_mem_kv': _jnp.float32, 'w_br': _jnp.float32, 'w_out': _jnp.float32, 'ln1_g': _jnp.float32, 'ln1_b': _jnp.float32, 'w_ff1': _jnp.float32, 'w_ff2': _jnp.float32, 'ln2_g': _jnp.float32, 'ln2_b': _jnp.float32}
MOMENT_SCALE = {'ln_in_g': 7.573688e-01, 'ln_in_b': 4.815938e-01, 'w_in': 2.743500e-03, 'b_forget': 2.411872e-02, 'w_uq': 1.752998e-03, 'g_cq': 2.459544e-03, 'w_ukv': 3.043308e-03, 'g_ckv': 6.402722e-03, 'w_mem_kv': 1.671234e-03, 'w_br': 7.378953e-03, 'w_out': 1.255758e-02, 'ln1_g': 8.916814e-01, 'ln1_b': 4.991950e-01, 'w_ff1': 3.120007e-02, 'w_ff2': 1.720255e-01, 'ln2_g': 1.609933e+01, 'ln2_b': 3.669060e+00}


def _to_microbatches(a, axis):
    t = _jnp.moveaxis(a, axis, 0)
    t = t.reshape((N_MICROBATCH, t.shape[0] // N_MICROBATCH) + t.shape[1:])
    return _jnp.moveaxis(t, 1, axis + 1)


def setup_inputs(seed: int = 0) -> dict:
    inp = _fwd_setup_inputs(seed)
    key = _jax.random.fold_in(_jax.random.key(seed), 7919)
    shape, _ = _output_shape()
    out = dict(inp)
    out["loss_target"] = _jax.random.normal(_jax.random.fold_in(key, 0), shape, _jnp.float32)
    for i, name in enumerate(TWIN_WEIGHTS):
        w = inp[name].astype(_jnp.float32)
        if MOMENT_SCALE is None:
            s = _jnp.sqrt(_jnp.mean(_jnp.square(w)) + 1e-30)
        else:
            s = MOMENT_SCALE[name]
        km, kv = _jax.random.split(_jax.random.fold_in(key, i + 1))
        out[name] = w
        out["m_" + name] = s * _jax.random.normal(km, w.shape, _jnp.float32)
        out["v_" + name] = (s * s) * _jax.random.uniform(kv, w.shape, _jnp.float32, 0.5, 1.5)
    if N_MICROBATCH > 1:
        for name, axis in PER_EXAMPLE_BATCH_AXIS.items():
            out[name] = _to_microbatches(out[name], axis)
    return {'x': out['x'], 'mem': out['mem'], 'positions': out['positions'], 'ln_in_g': out['ln_in_g'], 'ln_in_b': out['ln_in_b'], 'w_in': out['w_in'], 'b_forget': out['b_forget'], 'w_uq': out['w_uq'], 'g_cq': out['g_cq'], 'w_ukv': out['w_ukv'], 'g_ckv': out['g_ckv'], 'w_mem_kv': out['w_mem_kv'], 'w_br': out['w_br'], 'w_out': out['w_out'], 'ln1_g': out['ln1_g'], 'ln1_b': out['ln1_b'], 'w_ff1': out['w_ff1'], 'w_ff2': out['w_ff2'], 'ln2_g': out['ln2_g'], 'ln2_b': out['ln2_b'], 'loss_target': out['loss_target'], 'm_ln_in_g': out['m_ln_in_g'], 'm_ln_in_b': out['m_ln_in_b'], 'm_w_in': out['m_w_in'], 'm_b_forget': out['m_b_forget'], 'm_w_uq': out['m_w_uq'], 'm_g_cq': out['m_g_cq'], 'm_w_ukv': out['m_w_ukv'], 'm_g_ckv': out['m_g_ckv'], 'm_w_mem_kv': out['m_w_mem_kv'], 'm_w_br': out['m_w_br'], 'm_w_out': out['m_w_out'], 'm_ln1_g': out['m_ln1_g'], 'm_ln1_b': out['m_ln1_b'], 'm_w_ff1': out['m_w_ff1'], 'm_w_ff2': out['m_w_ff2'], 'm_ln2_g': out['m_ln2_g'], 'm_ln2_b': out['m_ln2_b'], 'v_ln_in_g': out['v_ln_in_g'], 'v_ln_in_b': out['v_ln_in_b'], 'v_w_in': out['v_w_in'], 'v_b_forget': out['v_b_forget'], 'v_w_uq': out['v_w_uq'], 'v_g_cq': out['v_g_cq'], 'v_w_ukv': out['v_w_ukv'], 'v_g_ckv': out['v_g_ckv'], 'v_w_mem_kv': out['v_w_mem_kv'], 'v_w_br': out['v_w_br'], 'v_w_out': out['v_w_out'], 'v_ln1_g': out['v_ln1_g'], 'v_ln1_b': out['v_ln1_b'], 'v_w_ff1': out['v_w_ff1'], 'v_w_ff2': out['v_w_ff2'], 'v_ln2_g': out['v_ln2_g'], 'v_ln2_b': out['v_ln2_b']}


def _loss(weights, diff, rest, loss_target):
    with _jax.named_scope("forward"):
        args = {**rest, TWIN_DIFF_INPUT: diff, **{k: w.astype(_WEIGHT_DTYPES[k]) for k, w in weights.items()}}
        y = _forward(args)
    with _jax.named_scope("loss_head"):
        err = _jnp.square(y.astype(_jnp.float32) - loss_target)
        return 0.5 * _jnp.sum(_jnp.mean(err, axis=-1)) if err.ndim else 0.5 * err


def _adamw(w, g, m, v):
    m = ADAM_B1 * m + (1.0 - ADAM_B1) * g
    v = ADAM_B2 * v + (1.0 - ADAM_B2) * _jnp.square(g)
    m_hat = m / (1.0 - ADAM_B1 ** ADAM_STEP)
    v_hat = v / (1.0 - ADAM_B2 ** ADAM_STEP)
    delta = -ADAM_LR * (m_hat / (_jnp.sqrt(v_hat) + ADAM_EPS) + ADAM_WD * w)
    return delta, m, v


def reference(x, mem, positions, ln_in_g, ln_in_b, w_in, b_forget, w_uq, g_cq, w_ukv, g_ckv, w_mem_kv, w_br, w_out, ln1_g, ln1_b, w_ff1, w_ff2, ln2_g, ln2_b, loss_target, m_ln_in_g, m_ln_in_b, m_w_in, m_b_forget, m_w_uq, m_g_cq, m_w_ukv, m_g_ckv, m_w_mem_kv, m_w_br, m_w_out, m_ln1_g, m_ln1_b, m_w_ff1, m_w_ff2, m_ln2_g, m_ln2_b, v_ln_in_g, v_ln_in_b, v_w_in, v_b_forget, v_w_uq, v_g_cq, v_w_ukv, v_g_ckv, v_w_mem_kv, v_w_br, v_w_out, v_ln1_g, v_ln1_b, v_w_ff1, v_w_ff2, v_ln2_g, v_ln2_b):
    given = dict(x=x, mem=mem, positions=positions, ln_in_g=ln_in_g, ln_in_b=ln_in_b, w_in=w_in, b_forget=b_forget, w_uq=w_uq, g_cq=g_cq, w_ukv=w_ukv, g_ckv=g_ckv, w_mem_kv=w_mem_kv, w_br=w_br, w_out=w_out, ln1_g=ln1_g, ln1_b=ln1_b, w_ff1=w_ff1, w_ff2=w_ff2, ln2_g=ln2_g, ln2_b=ln2_b, loss_target=loss_target, m_ln_in_g=m_ln_in_g, m_ln_in_b=m_ln_in_b, m_w_in=m_w_in, m_b_forget=m_b_forget, m_w_uq=m_w_uq, m_g_cq=m_g_cq, m_w_ukv=m_w_ukv, m_g_ckv=m_g_ckv, m_w_mem_kv=m_w_mem_kv, m_w_br=m_w_br, m_w_out=m_w_out, m_ln1_g=m_ln1_g, m_ln1_b=m_ln1_b, m_w_ff1=m_w_ff1, m_w_ff2=m_w_ff2, m_ln2_g=m_ln2_g, m_ln2_b=m_ln2_b, v_ln_in_g=v_ln_in_g, v_ln_in_b=v_ln_in_b, v_w_in=v_w_in, v_b_forget=v_b_forget, v_w_uq=v_w_uq, v_g_cq=v_g_cq, v_w_ukv=v_w_ukv, v_g_ckv=v_g_ckv, v_w_mem_kv=v_w_mem_kv, v_w_br=v_w_br, v_w_out=v_w_out, v_ln1_g=v_ln1_g, v_ln1_b=v_ln1_b, v_w_ff1=v_w_ff1, v_w_ff2=v_w_ff2, v_ln2_g=v_ln2_g, v_ln2_b=v_ln2_b)
    weights = {n: given[n] for n in TWIN_WEIGHTS}
    shared = {n: given[n] for n in SHARED_INPUTS}
    per_example = {n: given[n] for n in ['x', 'mem', 'positions']}
    grad_fn = _jax.value_and_grad(_loss, argnums=(0, 1))

    def one_microbatch(ex, loss_target):
        ex = dict(ex)
        diff = ex.pop(TWIN_DIFF_INPUT)
        return grad_fn(weights, diff, {**shared, **ex}, loss_target)

    if N_MICROBATCH == 1:
        loss, (grad_w, grad_x) = one_microbatch(per_example, given["loss_target"])
    else:
        def body(carry, xs):
            loss_sum, grad_sum = carry
            l_k, (gw_k, gx_k) = one_microbatch(xs[0], xs[1])
            with _jax.named_scope("update"):
                return (loss_sum + l_k, _jax.tree.map(_jnp.add, grad_sum, gw_k)), gx_k

        init = (_jnp.zeros((), _jnp.float32), _jax.tree.map(_jnp.zeros_like, weights))
        (loss, grad_w), grad_x = _jax.lax.scan(body, init, (per_example, given["loss_target"]))
    with _jax.named_scope("update"):
        delta_w, new_m, new_v = {}, {}, {}
        for n in TWIN_WEIGHTS:
            delta_w[n], new_m[n], new_v[n] = _adamw(weights[n], grad_w[n], given["m_" + n], given["v_" + n])
    return (loss, grad_x, *[grad_w[n] for n in TWIN_WEIGHTS], *[delta_w[n] for n in TWIN_WEIGHTS],
            *[new_m[n] for n in TWIN_WEIGHTS], *[new_v[n] for n in TWIN_WEIGHTS])
```

```python
import functools
from typing import NamedTuple

import jax
import jax.numpy as jnp
from jax import lax
from jax.experimental import pallas as pl
from jax.experimental.pallas import tpu as pltpu

F32 = jnp.float32
BF16 = jnp.bfloat16
MESH = pl.DeviceIdType.MESH

LANES = 128
SUBLANES = 8
VMEM_BYTES = 64 * 1024 * 1024
N_CHIPS = 4
N_DEV = 8

FOX_DH = 64
MLA_NOPE = 64
MLA_ROPE = 32
MLA_V = 64
MEM_DH = 128
ROPE_BASE = 10000.0
LN_EPS = 1e-5
RMS_EPS = 1e-6
NEG_INF = -1e30

ADAM_LR = 0.001
ADAM_B1 = 0.9
ADAM_B2 = 0.999
ADAM_EPS = 1e-08
ADAM_WD = 0.01
ADAM_STEP = 10


class Cfg(NamedTuple):
    d: int = 1024
    depth: int = 4
    seq: int = 2048
    chunk: int = 64
    n_mem: int = 256
    fox_h: int = 8
    mla_h: int = 8
    q_rank: int = 384
    kv_rank: int = 256
    mem_h: int = 4
    d_ff: int = 4096

    @property
    def width(self):
        return self.fox_h * FOX_DH

    @property
    def alpha(self):
        return (2 * self.depth) ** 0.25

    @property
    def small_w(self):
        return LANES + self.q_rank + self.kv_rank + LANES

    @property
    def in_splits(self):
        return (3 * self.width, self.fox_h, self.q_rank, self.kv_rank, MLA_ROPE, self.width, 3 * self.d)


def _pcall(body, **kw):
    return pl.pallas_call(body, **kw)


def _nbytes(shape, dtype):
    n = 1
    for s in shape:
        n *= s
    return n * jnp.dtype(dtype).itemsize


def _tile(dim, target):
    if dim <= target:
        return dim
    t = target - target % LANES
    while t >= LANES:
        if dim % t == 0:
            return t
        t -= LANES
    return dim


def _params(block_bytes, scratch_bytes=0):
    est = 2 * block_bytes + scratch_bytes + 24 * 1024 * 1024
    return pltpu.CompilerParams(vmem_limit_bytes=int(min(max(est, 32 * 1024 * 1024), VMEM_BYTES - 4 * 1024 * 1024)))


_DIMS = {"nn": (((1,), (0,)), ((), ())), "nt": (((1,), (1,)), ((), ())), "tn": (((0,), (0,)), ((), ()))}


def _mm(name, mode, pairs, out_dtypes, tm=512, tn=512, epi=None, row_extras=(), bc_extras=()):
    a0, b0 = pairs[0]
    m = a0.shape[1] if mode == "tn" else a0.shape[0]
    n = b0.shape[0] if mode == "nt" else b0.shape[1]
    tm, tn = _tile(m, tm), _tile(n, tn)
    in_specs, ops, blk = [], [], 0
    for a, b in pairs:
        if mode == "tn":
            k = a.shape[0]
            sa, sha = pl.BlockSpec((k, tm), lambda i, j: (0, i)), (k, tm)
        else:
            k = a.shape[1]
            sa, sha = pl.BlockSpec((tm, k), lambda i, j: (i, 0)), (tm, k)
        if mode == "nt":
            sb, shb = pl.BlockSpec((tn, k), lambda i, j: (j, 0)), (tn, k)
        else:
            sb, shb = pl.BlockSpec((k, tn), lambda i, j: (0, j)), (k, tn)
        in_specs += [sa, sb]
        ops += [a, b]
        blk += _nbytes(sha, a.dtype) + _nbytes(shb, b.dtype)
    for e in row_extras:
        w = e.shape[1]
        if w == n:
            in_specs.append(pl.BlockSpec((tm, tn), lambda i, j: (i, j)))
            blk += _nbytes((tm, tn), e.dtype)
        else:
            in_specs.append(pl.BlockSpec((tm, w), lambda i, j: (i, 0)))
            blk += _nbytes((tm, w), e.dtype)
        ops.append(e)
    for e in bc_extras:
        r, w = e.shape
        if w == n:
            in_specs.append(pl.BlockSpec((r, tn), lambda i, j: (0, j)))
        else:
            in_specs.append(pl.BlockSpec((r, w), lambda i, j: (0, 0)))
        blk += _nbytes((r, w), e.dtype)
        ops.append(e)
    npairs, nrow, nbc, nout = len(pairs), len(row_extras), len(bc_extras), len(out_dtypes)
    dims = _DIMS[mode]

    def body(*refs):
        acc = None
        for p in range(npairs):
            a = refs[2 * p][...].astype(BF16)
            b = refs[2 * p + 1][...].astype(BF16)
            d = lax.dot_general(a, b, dims, preferred_element_type=F32)
            acc = d if acc is None else acc + d
        ex = [r[...] for r in refs[2 * npairs:2 * npairs + nrow + nbc]]
        outs = (acc,) if epi is None else epi(acc, *ex)
        for o_ref, o in zip(refs[2 * npairs + nrow + nbc:], outs):
            o_ref[...] = o.astype(o_ref.dtype)

    blk += sum(_nbytes((tm, tn), dt) for dt in out_dtypes) + 2 * _nbytes((tm, tn), F32)
    res = _pcall(
        body,
        name=name,
        grid=(m // tm, n // tn),
        in_specs=in_specs,
        out_specs=[pl.BlockSpec((tm, tn), lambda i, j: (i, j)) for _ in range(nout)],
        out_shape=[jax.ShapeDtypeStruct((m, n), dt) for dt in out_dtypes],
        compiler_params=_params(blk),
    )(*ops)
    return res[0] if nout == 1 else res


def _rowwise(name, fn, row_ins, bc_ins, outs, accs=(), tm=256):
    rows = row_ins[0].shape[0]
    tm = min(tm, rows)
    assert rows % tm == 0
    nrow, nbc, nout, nacc = len(row_ins), len(bc_ins), len(outs), len(accs)
    in_specs = [pl.BlockSpec((tm, a.shape[1]), lambda i: (i, 0)) for a in row_ins]
    in_specs += [pl.BlockSpec(a.shape, lambda i: (0, 0)) for a in bc_ins]
    out_specs = [pl.BlockSpec((tm, w), lambda i: (i, 0)) for w, _ in outs]
    out_specs += [pl.BlockSpec(s, lambda i: (0, 0)) for s in accs]
    out_shape = [jax.ShapeDtypeStruct((rows, w), dt) for w, dt in outs]
    out_shape += [jax.ShapeDtypeStruct(s, F32) for s in accs]

    def body(*refs):
        vals = fn(*[r[...] for r in refs[:nrow + nbc]])
        o_refs = refs[nrow + nbc:]
        for r, v in zip(o_refs[:nout], vals[:nout]):
            r[...] = v.astype(r.dtype)
        if nacc:
            @pl.when(pl.program_id(0) == 0)
            def _():
                for r in o_refs[nout:]:
                    r[...] = jnp.zeros(r.shape, F32)

            for r, v in zip(o_refs[nout:], vals[nout:]):
                r[...] += v

    blk = sum(_nbytes((tm, a.shape[1]), a.dtype) for a in row_ins) + sum(_nbytes(a.shape, a.dtype) for a in bc_ins)
    blk += sum(_nbytes((tm, w), dt) for w, dt in outs) + sum(_nbytes(s, F32) for s in accs)
    res = _pcall(
        body,
        name=name,
        grid=(rows // tm,),
        in_specs=in_specs,
        out_specs=out_specs,
        out_shape=out_shape,
        compiler_params=_params(2 * blk),
    )(*row_ins, *bc_ins)
    return res


def _ln(z, g, b):
    mu = jnp.mean(z, axis=-1, keepdims=True)
    zc = z - mu
    var = jnp.mean(zc * zc, axis=-1, keepdims=True)
    return zc * lax.rsqrt(var + LN_EPS) * g + b


def _rms(x, g):
    return x * lax.rsqrt(jnp.mean(x * x, axis=-1, keepdims=True) + RMS_EPS) * g


def _colsum(v):
    return jnp.sum(v, axis=0, keepdims=True)


def _rope_swap(x):
    w = x.shape[1]
    lane = lax.broadcasted_iota(jnp.int32, (1, w), 1) % LANES
    from_left = pltpu.roll(x, 16, 1)
    from_right = pltpu.roll(x, w - 16, 1)
    lo = (lane >= MLA_NOPE) & (lane < MLA_NOPE + 16)
    hi = (lane >= MLA_NOPE + 16) & (lane < MLA_NOPE + 32)
    return jnp.where(hi, from_left, jnp.where(lo, from_right, 0.0))


def _rope(x, cos_t, sin_t):
    nh = x.shape[1] // LANES
    ct, st = jnp.tile(cos_t, (1, nh)), jnp.tile(sin_t, (1, nh))
    return x * ct + _rope_swap(x) * st


def _rope_t(dy, cos_t, sin_t):
    nh = dy.shape[1] // LANES
    ct, st = jnp.tile(cos_t, (1, nh)), jnp.tile(sin_t, (1, nh))
    return dy * ct + _rope_swap(dy * st)


def _seq_cumsum(name, x, batch, seq, reverse, tb=256):
    tb = min(tb, seq)
    nb = seq // tb

    def body(x_ref, o_ref, carry):
        @pl.when(pl.program_id(1) == 0)
        def _():
            carry[...] = jnp.zeros(carry.shape, F32)

        r = lax.broadcasted_iota(jnp.int32, (tb, tb), 0)
        c = lax.broadcasted_iota(jnp.int32, (tb, tb), 1)
        tri = jnp.where((c >= r) if reverse else (c <= r), 1.0, 0.0).astype(BF16)
        v = x_ref[...]
        hi = v.astype(BF16)
        r1 = v - hi.astype(F32)
        mid = r1.astype(BF16)
        lo = (r1 - mid.astype(F32)).astype(BF16)
        out = carry[...] + sum(jnp.dot(tri, p, preferred_element_type=F32) for p in (hi, mid, lo))
        o_ref[...] = out
        carry[...] = out[0:1, :] if reverse else out[tb - 1:tb, :]

    if reverse:
        idx = lambda b, i: (b * nb + nb - 1 - i, 0)
    else:
        idx = lambda b, i: (b * nb + i, 0)
    return _pcall(
        body,
        name=name,
        grid=(batch, nb),
        in_specs=[pl.BlockSpec((tb, LANES), idx)],
        out_specs=pl.BlockSpec((tb, LANES), idx),
        out_shape=jax.ShapeDtypeStruct(x.shape, F32),
        scratch_shapes=[pltpu.VMEM((1, LANES), F32)],
        compiler_params=_params(4 * tb * LANES * 4),
    )(x)


class Attn(NamedTuple):
    batch: int
    sq: int
    sk: int
    groups: int
    hq: int
    hv: int
    mode: str
    scale: float
    chunk: int
    tq: int
    tk: int

    @property
    def hg(self):
        return self.hv

    @property
    def qw(self):
        return LANES * self.hg // self.hq

    @property
    def dv(self):
        return LANES // self.hv


def _head_lanes(j, dv):
    lane = lax.broadcasted_iota(jnp.int32, (1, LANES), 1)
    return (lane >= j * dv) & (lane < (j + 1) * dv)


def _allowed(sp, rows, cols):
    if sp.mode == "fox":
        return cols <= rows
    shift = sp.chunk.bit_length() - 1
    return jnp.right_shift(cols, shift) <= jnp.right_shift(rows, shift)


def _attn_fwd(name, sp, q, k, v, cfq=None, cfk=None):
    (qa, qo), (ka, ko), (va, vo) = q, k, v
    tq, tk, hg, qw, dv = min(sp.tq, sp.sq), min(sp.tk, sp.sk), sp.hg, sp.qw, sp.dv
    nqb, nkc = sp.sq // tq, sp.sk // tk
    fox = sp.mode == "fox"

    def body(*refs):
        if fox:
            q_ref, k_ref, v_ref, cfq_ref, cfk_ref, o_ref, lse_ref, m_scr, l_scr, acc_scr = refs
        else:
            q_ref, k_ref, v_ref, o_ref, lse_ref, m_scr, l_scr, acc_scr = refs
        q0 = pl.program_id(2) * tq
        q_blk = q_ref[...]
        if sp.hq == 2:
            qs = [jnp.where(_head_lanes(j, FOX_DH), q_blk, jnp.zeros_like(q_blk)) for j in range(hg)]
        else:
            qs = [q_blk[:, LANES * j:LANES * (j + 1)] for j in range(hg)]
        m_scr[...] = jnp.full(m_scr.shape, NEG_INF, F32)
        l_scr[...] = jnp.zeros(l_scr.shape, F32)
        acc_scr[...] = jnp.zeros(acc_scr.shape, F32)
        rows = q0 + lax.broadcasted_iota(jnp.int32, (tq, 1), 0)
        sel = _head_lanes(0, dv)

        def chunk(kc, carry):
            k0 = pl.multiple_of(kc * tk, tk)
            k_c = k_ref[pl.ds(k0, tk), :]
            v_c = v_ref[pl.ds(k0, tk), :]
            cols = k0 + lax.broadcasted_iota(jnp.int32, (1, tk), 1)
            pvs, alphas = [], []
            for j in range(hg):
                k_j = k_c if sp.hq == 2 else k_c[:, LANES * j:LANES * (j + 1)]
                s = lax.dot_general(qs[j], k_j, _DIMS["nt"], preferred_element_type=F32) * sp.scale
                if fox:
                    s = s + (cfq_ref[:, dv * j:dv * j + 1] - cfk_ref[kc][j:j + 1, :])
                if sp.mode != "none":
                    s = jnp.where(_allowed(sp, rows, cols), s, NEG_INF)
                m_prev = m_scr[j]
                m_new = jnp.maximum(m_prev, jnp.max(s, axis=1, keepdims=True))
                alpha = jnp.exp(m_prev - m_new)
                p = jnp.exp(s - m_new)
                l_scr[j] = alpha * l_scr[j] + jnp.sum(p, axis=1, keepdims=True)
                m_scr[j] = m_new
                pvs.append(jnp.dot(p.astype(BF16), v_c, preferred_element_type=F32))
                alphas.append(alpha)
            if hg == 2:
                acc_scr[...] = acc_scr[...] * jnp.where(sel, alphas[0], alphas[1]) + jnp.where(sel, pvs[0], pvs[1])
            else:
                acc_scr[...] = acc_scr[...] * alphas[0] + pvs[0]
            return carry

        n_chunks = nkc if sp.mode == "none" else (q0 + tq + tk - 1) // tk
        lax.fori_loop(0, n_chunks, chunk, 0)
        if hg == 2:
            l_all = jnp.where(sel, l_scr[0], l_scr[1])
            m_all = jnp.where(sel, m_scr[0], m_scr[1])
        else:
            l_all = jnp.broadcast_to(l_scr[0], (tq, LANES))
            m_all = jnp.broadcast_to(m_scr[0], (tq, LANES))
        o_ref[...] = (acc_scr[...] / l_all).astype(o_ref.dtype)
        lse_ref[...] = m_all + jnp.log(l_all)

    in_specs = [
        pl.BlockSpec((tq, qw), lambda b, g, i: (b * nqb + i, qo + g)),
        pl.BlockSpec((sp.sk, qw), lambda b, g, i: (b, ko + g)),
        pl.BlockSpec((sp.sk, LANES), lambda b, g, i: (b, vo + g)),
    ]
    ops = [qa, ka, va]
    if fox:
        in_specs += [
            pl.BlockSpec((tq, LANES), lambda b, g, i: (b * nqb + i, g)),
            pl.BlockSpec((None, None, nkc, hg, tk), lambda b, g, i: (b, g, 0, 0, 0)),
        ]
        ops += [cfq, cfk]
    rows_total = sp.batch * sp.sq
    blk = _nbytes((tq, qw), BF16) + _nbytes((sp.sk, qw + LANES), BF16) + 3 * _nbytes((tq, LANES), F32)
    blk += _nbytes((sp.sk, 8), F32) + 4 * _nbytes((tq, tk), F32)
    return _pcall(
        body,
        name=name,
        grid=(sp.batch, sp.groups, nqb),
        in_specs=in_specs,
        out_specs=[pl.BlockSpec((tq, LANES), lambda b, g, i: (b * nqb + i, g))] * 2,
        out_shape=[
            jax.ShapeDtypeStruct((rows_total, sp.groups * LANES), F32),
            jax.ShapeDtypeStruct((rows_total, sp.groups * LANES), F32),
        ],
        scratch_shapes=[pltpu.VMEM((hg, tq, 1), F32), pltpu.VMEM((hg, tq, 1), F32), pltpu.VMEM((tq, LANES), F32)],
        compiler_params=_params(blk, 3 * tq * LANES * 4 * 2),
    )(*ops)


def _attn_bwd(name, sp, q, k, v, o, lse, do, cfq=None, cfk=None):
    (qa, qo), (ka, ko), (va, vo) = q, k, v
    tq, tk, hg, qw, dv = min(sp.tq, sp.sq), min(sp.tk, sp.sk), sp.hg, sp.qw, sp.dv
    nqb, nkb = sp.sq // tq, sp.sk // tk
    fox = sp.mode == "fox"

    def body(*refs):
        if fox:
            (q_ref, k_ref, v_ref, lse_ref, do_ref, cfq_ref, cfk_ref, kall_ref, vall_ref, cfkall_ref,
             dq_ref, dk_ref, dv_ref, dcf_ref, delta_scr, dk_scr, dv_scr, dcf_scr) = refs
        else:
            (q_ref, k_ref, v_ref, o_ref, lse_ref, do_ref,
             dq_ref, dk_ref, dv_ref, delta_scr, dk_scr, dv_scr) = refs
        kb = pl.program_id(2)
        k0 = kb * tk
        heads = [_head_lanes(j, dv) for j in range(hg)]

        def head_qk(j, q_c, k_c):
            if sp.hq == 2:
                return jnp.where(_head_lanes(j, FOX_DH), q_c, jnp.zeros_like(q_c)), k_c
            return q_c[:, LANES * j:LANES * (j + 1)], k_c[:, LANES * j:LANES * (j + 1)]

        def probs(j, q_j, k_j, r0, lse_c, cf_keys, rows, cols):
            s = lax.dot_general(q_j, k_j, _DIMS["nt"], preferred_element_type=F32) * sp.scale
            if fox:
                s = s + (cfq_ref[pl.ds(r0, tq), :][:, dv * j:dv * j + 1] - cf_keys[j:j + 1, :])
            if sp.mode != "none":
                s = jnp.where(_allowed(sp, rows, cols), s, NEG_INF)
            return jnp.exp(s - lse_c[:, dv * j:dv * j + 1])

        def head_do(j, do_c):
            return jnp.where(heads[j], do_c, jnp.zeros_like(do_c)) if hg == 2 else do_c

        @pl.when(kb == 0)
        def _():
            dq_ref[...] = jnp.zeros(dq_ref.shape, F32)

            def fill(i, carry):
                r0 = pl.multiple_of(i * tq, tq)
                do_c = do_ref[pl.ds(r0, tq), :]
                if fox:
                    q_c = q_ref[pl.ds(r0, tq), :]
                    lse_c = lse_ref[pl.ds(r0, tq), :]
                    rows = r0 + lax.broadcasted_iota(jnp.int32, (tq, 1), 0)

                    def keys(kc, acc):
                        c0 = pl.multiple_of(kc * tk, tk)
                        k_c = kall_ref[pl.ds(c0, tk), :]
                        v_c = vall_ref[pl.ds(c0, tk), :]
                        cols = c0 + lax.broadcasted_iota(jnp.int32, (1, tk), 1)
                        out = []
                        for j in range(hg):
                            q_j, k_j = head_qk(j, q_c, k_c)
                            p = probs(j, q_j, k_j, r0, lse_c, cfkall_ref[kc], rows, cols)
                            dp = lax.dot_general(head_do(j, do_c), v_c, _DIMS["nt"], preferred_element_type=F32)
                            out.append(acc[j] + jnp.sum(p * dp, axis=1, keepdims=True))
                        return tuple(out)

                    d = lax.fori_loop(0, (r0 + tq + tk - 1) // tk, keys,
                                      tuple(jnp.zeros((tq, 1), F32) for _ in range(hg)))
                    delta_scr[pl.ds(r0, tq), :] = jnp.where(heads[0], d[0], d[1])
                else:
                    prod = do_c.astype(F32) * o_ref[pl.ds(r0, tq), :]
                    if hg == 2:
                        d0 = jnp.sum(jnp.where(heads[0], prod, 0.0), axis=1, keepdims=True)
                        d1 = jnp.sum(jnp.where(heads[1], prod, 0.0), axis=1, keepdims=True)
                        delta_scr[pl.ds(r0, tq), :] = jnp.where(heads[0], d0, d1)
                    else:
                        delta_scr[pl.ds(r0, tq), :] = jnp.broadcast_to(jnp.sum(prod, axis=1, keepdims=True), (tq, LANES))
                return carry

            lax.fori_loop(0, nqb, fill, 0)

        k_blk = k_ref[...]
        v_blk = v_ref[...]
        dk_scr[...] = jnp.zeros(dk_scr.shape, F32)
        dv_scr[...] = jnp.zeros(dv_scr.shape, F32)
        if fox:
            dcf_scr[...] = jnp.zeros(dcf_scr.shape, F32)
        cols = k0 + lax.broadcasted_iota(jnp.int32, (1, tk), 1)

        def qblock(i, carry):
            r0 = pl.multiple_of(i * tq, tq)
            q_c = q_ref[pl.ds(r0, tq), :]
            do_c = do_ref[pl.ds(r0, tq), :]
            lse_c = lse_ref[pl.ds(r0, tq), :]
            delta_c = delta_scr[pl.ds(r0, tq), :]
            rows = r0 + lax.broadcasted_iota(jnp.int32, (tq, 1), 0)
            dqs = []
            for j in range(hg):
                q_j, k_j = head_qk(j, q_c, k_blk)
                p = probs(j, q_j, k_j, r0, lse_c, cfk_ref[...] if fox else None, rows, cols)
                dp = lax.dot_general(head_do(j, do_c), v_blk, _DIMS["nt"], preferred_element_type=F32)
                ds = p * (dp - delta_c[:, dv * j:dv * j + 1])
                if fox:
                    dcf_scr[j:j + 1, :] -= jnp.sum(ds, axis=0, keepdims=True)
                ds_b = (ds * sp.scale).astype(BF16)
                dv_scr[j] += lax.dot_general(p.astype(BF16), do_c, _DIMS["tn"], preferred_element_type=F32)
                dk_scr[j] += lax.dot_general(ds_b, q_c if sp.hq == 2 else q_j, _DIMS["tn"], preferred_element_type=F32)
                dqs.append(jnp.dot(ds_b, k_j, preferred_element_type=F32))
            if sp.hq == 2:
                dq_c = jnp.where(_head_lanes(0, FOX_DH), dqs[0], dqs[1])
            elif hg == 2:
                dq_c = jnp.concatenate(dqs, axis=1)
            else:
                dq_c = dqs[0]
            dq_ref[pl.ds(r0, tq), :] += dq_c
            return carry

        first = 0 if sp.mode == "none" else k0 // tq
        lax.fori_loop(first, nqb, qblock, 0)
        if hg == 2:
            dv_ref[...] = jnp.where(heads[0], dv_scr[0], dv_scr[1]).astype(dv_ref.dtype)
        else:
            dv_ref[...] = dv_scr[0].astype(dv_ref.dtype)
        if sp.hq == 2:
            dk_ref[...] = jnp.where(_head_lanes(0, FOX_DH), dk_scr[0], dk_scr[1]).astype(dk_ref.dtype)
        elif hg == 2:
            dk_ref[...] = jnp.concatenate([dk_scr[0], dk_scr[1]], axis=1).astype(dk_ref.dtype)
        else:
            dk_ref[...] = dk_scr[0].astype(dk_ref.dtype)
        if fox:
            dcf_ref[...] = dcf_scr[...]

    seq_lanes = lambda b, g, kb: (b, g)
    key_blk = lambda b, g, kb: (b * nkb + kb, g)
    in_specs = [
        pl.BlockSpec((sp.sq, qw), lambda b, g, kb: (b, qo + g)),
        pl.BlockSpec((tk, qw), lambda b, g, kb: (b * nkb + kb, ko + g)),
        pl.BlockSpec((tk, LANES), lambda b, g, kb: (b * nkb + kb, vo + g)),
    ]
    ops = [qa, ka, va]
    if not fox:
        in_specs.append(pl.BlockSpec((sp.sq, LANES), seq_lanes))
        ops.append(o)
    in_specs += [pl.BlockSpec((sp.sq, LANES), seq_lanes)] * 2
    ops += [lse, do]
    out_specs = [pl.BlockSpec((sp.sq, qw), seq_lanes), pl.BlockSpec((tk, qw), key_blk), pl.BlockSpec((tk, LANES), key_blk)]
    out_shape = [
        jax.ShapeDtypeStruct((sp.batch * sp.sq, sp.groups * qw), F32),
        jax.ShapeDtypeStruct((sp.batch * sp.sk, sp.groups * qw), BF16),
        jax.ShapeDtypeStruct((sp.batch * sp.sk, sp.groups * LANES), BF16),
    ]
    scratch = [pltpu.VMEM((sp.sq, LANES), F32), pltpu.VMEM((hg, tk, LANES), F32), pltpu.VMEM((hg, tk, LANES), F32)]
    if fox:
        cf_blk = pl.BlockSpec((None, None, None, hg, tk), lambda b, g, kb: (b, g, kb, 0, 0))
        in_specs += [
            pl.BlockSpec((sp.sq, LANES), seq_lanes),
            cf_blk,
            pl.BlockSpec((sp.sk, qw), lambda b, g, kb: (b, ko + g)),
            pl.BlockSpec((sp.sk, LANES), lambda b, g, kb: (b, vo + g)),
            pl.BlockSpec((None, None, nkb, hg, tk), lambda b, g, kb: (b, g, 0, 0, 0)),
        ]
        ops += [cfq, cfk, ka, va, cfk]
        out_specs.append(cf_blk)
        out_shape.append(jax.ShapeDtypeStruct((sp.batch, sp.groups, nkb, hg, tk), F32))
        scratch.append(pltpu.VMEM((hg, tk), F32))
    blk = _nbytes((sp.sq, qw), BF16) + 2 * _nbytes((sp.sq, LANES), BF16) + 2 * _nbytes((sp.sq, LANES), F32)
    blk += _nbytes((sp.sq, qw), F32) + 4 * _nbytes((tk, qw), BF16) + 6 * _nbytes((tq, tk), F32)
    blk += _nbytes((sp.sk, qw + LANES), BF16) if fox else 0
    return _pcall(
        body,
        name=name,
        grid=(sp.batch, sp.groups, nkb),
        in_specs=in_specs,
        out_specs=out_specs,
        out_shape=out_shape,
        scratch_shapes=scratch,
        compiler_params=_params(blk, _nbytes((sp.sq, LANES), F32) + 4 * _nbytes((tk, LANES), F32)),
    )(*ops)


def _slabwise(name, fn, ins, out_dtypes, rows_per_step=128):
    n = max(a.shape[0] for a in ins)
    rows, cols = ins[0].shape[1:]
    tr = min(rows_per_step, rows)
    while rows % tr:
        tr //= 2
    assert tr % 16 == 0 or tr == rows, (name, rows, tr)

    def spec(a):
        if a.shape[0] == 1:
            return pl.BlockSpec((None, tr, cols), lambda s, i: (0, i, 0))
        return pl.BlockSpec((None, tr, cols), lambda s, i: (s, i, 0))

    def body(*refs):
        vals = fn(*[r[...] for r in refs[:len(ins)]])
        for r, v in zip(refs[len(ins):], vals):
            r[...] = v.astype(r.dtype)

    blk = (len(ins) + len(out_dtypes)) * _nbytes((tr, cols + LANES), F32)
    res = _pcall(
        body,
        name=name,
        grid=(n, rows // tr),
        in_specs=[spec(a) for a in ins],
        out_specs=[pl.BlockSpec((None, tr, cols), lambda s, i: (s, i, 0)) for _ in out_dtypes],
        out_shape=[jax.ShapeDtypeStruct((n, rows, cols), dt) for dt in out_dtypes],
        compiler_params=_params(2 * blk),
    )(*ins)
    return res


def _adamw_math(w, g, m, v):
    m = ADAM_B1 * m + (1.0 - ADAM_B1) * g
    v = ADAM_B2 * v + (1.0 - ADAM_B2) * jnp.square(g)
    m_hat = m / (1.0 - ADAM_B1 ** ADAM_STEP)
    v_hat = v / (1.0 - ADAM_B2 ** ADAM_STEP)
    delta = -ADAM_LR * (m_hat / (jnp.sqrt(v_hat) + ADAM_EPS) + ADAM_WD * w)
    return delta, m, v


def _peer(rel):
    x, y, c = lax.axis_index("x"), lax.axis_index("y"), lax.axis_index("c")
    if rel == "c":
        return (x, y, 1 - c)
    return ((1 - x) if rel in (1, 3) else x, (1 - y) if rel in (2, 3) else y, c)


def _exchange(name, ins, out_shapes, plan, n_copies, aliases=None):
    n_in, n_out = len(ins), len(out_shapes)

    def body(*refs):
        in_refs, out_refs = refs[:n_in], refs[n_in:n_in + n_out]
        send_sems, recv_sems = refs[n_in + n_out:]
        copies = plan(in_refs, out_refs)
        assert len(copies) == n_copies, (name, len(copies))
        started = []
        for i, (src, dst, rel) in enumerate(copies):
            if rel is None:
                cp = pltpu.make_async_copy(src, dst, send_sems.at[i])
            else:
                cp = pltpu.make_async_remote_copy(src_ref=src, dst_ref=dst, send_sem=send_sems.at[i],
                                                  recv_sem=recv_sems.at[i], device_id=_peer(rel), device_id_type=MESH)
            cp.start()
            started.append(cp)
        for cp in started:
            cp.wait()

    any_spec = pl.BlockSpec(memory_space=pl.ANY)
    res = _pcall(
        body,
        name=name,
        in_specs=[any_spec] * n_in,
        out_specs=[any_spec] * n_out,
        out_shape=out_shapes,
        scratch_shapes=[pltpu.SemaphoreType.DMA((n_copies,)), pltpu.SemaphoreType.DMA((n_copies,))],
        input_output_aliases=aliases or {},
    )(*ins)
    return res


def _gather_weights(shards):
    n = len(shards)
    shapes = [jax.ShapeDtypeStruct((N_CHIPS,) + s.shape, s.dtype) for s in shards]

    def plan1(in_refs, out_refs):
        c = lax.axis_index("c")
        copies = []
        for s, g in zip(in_refs, out_refs):
            copies.append((s.at[c], g.at[0, 0], None))
            copies.append((s.at[1 - c], g.at[0, 1], None))
            for rel in (1, 2, 3):
                copies.append((s.at[c], g.at[rel, 0], rel))
        return copies

    got = _exchange("gather_weights_ici", shards, shapes, plan1, 5 * n)

    def plan2(in_refs, out_refs):
        return [(g_in.at[pl.ds(1, 3), 0], g_out.at[pl.ds(1, 3), 1], "c") for g_in, g_out in zip(in_refs, out_refs)]

    return _exchange("gather_weights_d2d", got, shapes, plan2, n, aliases={i: i for i in range(n)})


def _to_absolute(rel_arr):
    x, y, c = lax.axis_index("x"), lax.axis_index("y"), lax.axis_index("c")
    tail = rel_arr.shape[2:]
    zeros = (0,) * len(tail)
    parts = []
    for sx in (0, 1):
        for sy in (0, 1):
            for h in (0, 1):
                rel = (sx ^ x) + 2 * (sy ^ y)
                parts.append(lax.dynamic_slice(rel_arr, (rel, h ^ c) + zeros, (1, 1) + tail))
    return jnp.concatenate(parts, axis=0).reshape((N_CHIPS, 2) + tail)


def _to_relative(abs_arr):
    x, y, c = lax.axis_index("x"), lax.axis_index("y"), lax.axis_index("c")
    tail = abs_arr.shape[2:]
    zeros = (0,) * len(tail)
    parts = []
    for hr in (0, 1):
        for rel in (0, 1, 2, 3):
            chip = 2 * (x ^ (rel & 1)) + (y ^ (rel >> 1))
            parts.append(lax.dynamic_slice(abs_arr, (chip, c ^ hr) + zeros, (1, 1) + tail))
    return jnp.concatenate(parts, axis=0).reshape((2, N_CHIPS) + tail)


def _reduce_grads(grads_rel):
    n = len(grads_rel)
    half_shapes = [jax.ShapeDtypeStruct(g.shape[1:], g.dtype) for g in grads_rel]

    def plan1(in_refs, out_refs):
        return [(g.at[1], r, "c") for g, r in zip(in_refs, out_refs)]

    from_sibling = _exchange("reduce_grads_pair", grads_rel, half_shapes, plan1, n)
    pair = [
        _slabwise(f"reduce_grads_pair_sum_{i}", lambda a, b: (a.astype(F32) + b.astype(F32),),
                  [g[0], r], [BF16])[0]
        for i, (g, r) in enumerate(zip(grads_rel, from_sibling))
    ]
    chip_shapes = [jax.ShapeDtypeStruct((3,) + p.shape[1:], p.dtype) for p in pair]

    def plan2(in_refs, out_refs):
        return [(p.at[rel], r.at[rel - 1], rel) for p, r in zip(in_refs, out_refs) for rel in (1, 2, 3)]

    from_chips = _exchange("reduce_grads_chips", pair, chip_shapes, plan2, 3 * n)
    total = [
        _slabwise(f"reduce_grads_chip_sum_{i}",
                  lambda a, b, c_, d: (a.astype(F32) + b.astype(F32) + c_.astype(F32) + d.astype(F32),),
                  [p[0:1], r[0:1], r[1:2], r[2:3]], [F32])[0]
        for i, (p, r) in enumerate(zip(pair, from_chips))
    ]
    full_shapes = [jax.ShapeDtypeStruct((2,) + t.shape[1:], F32) for t in total]

    def plan3(in_refs, out_refs):
        c = lax.axis_index("c")
        copies = []
        for t, f in zip(in_refs, out_refs):
            copies.append((t.at[0], f.at[c], None))
            copies.append((t.at[0], f.at[c], "c"))
        return copies

    return _exchange("reduce_grads_share", total, full_shapes, plan3, 2 * n)


def _allreduce_small(v):
    rows = v.shape[0]

    def body(v_ref, sum_ref, all_ref, send_sems, recv_sems, local_sem):
        x, y, c = lax.axis_index("x"), lax.axis_index("y"), lax.axis_index("c")
        sibling = (x, y, 1 - c)
        chips = [(1 - x, y), (x, 1 - y), (1 - x, 1 - y)]

        def slab(px, py, pc):
            return all_ref.at[pl.ds((4 * px + 2 * py + pc) * rows, rows), :]

        def copy(k, block, to, src=None):
            return pltpu.make_async_remote_copy(
                src_ref=slab(*block) if src is None else src, dst_ref=slab(*block), send_sem=send_sems.at[k],
                recv_sem=recv_sems.at[k], device_id=to, device_id_type=MESH)

        mine = pltpu.make_async_copy(v_ref, slab(x, y, c), local_sem)
        mine.start()
        first = [copy(0, (x, y, c), sibling, src=v_ref)]
        first += [copy(1 + j, (x, y, c), (*chip, c), src=v_ref) for j, chip in enumerate(chips)]
        for cp in first:
            cp.start()
        passed = [copy(4 + j, (*chip, c), sibling) for j, chip in enumerate(chips)]
        for j, chip in enumerate(chips):
            copy(1 + j, (*chip, c), (x, y, c)).wait_recv()
            passed[j].start()
        copy(0, (x, y, 1 - c), (x, y, c)).wait_recv()
        for j, chip in enumerate(chips):
            copy(4 + j, (*chip, 1 - c), (x, y, c)).wait_recv()
        for cp in first + passed:
            cp.wait_send()
        mine.wait()
        total = all_ref[pl.ds(0, rows), :]
        for d in range(1, N_DEV):
            total = total + all_ref[pl.ds(d * rows, rows), :]
        sum_ref[...] = total

    vm = pl.BlockSpec(memory_space=pltpu.VMEM)
    return _pcall(
        body,
        name="allreduce_small",
        in_specs=[vm],
        out_specs=vm,
        out_shape=jax.ShapeDtypeStruct((rows, LANES), F32),
        scratch_shapes=[pltpu.VMEM((N_DEV * rows, LANES), F32), pltpu.SemaphoreType.DMA((7,)),
                        pltpu.SemaphoreType.DMA((7,)), pltpu.SemaphoreType.DMA],
    )(v)


def _pad_cols(a, before, total):
    return jnp.pad(a, ((0, 0), (before, total - before - a.shape[1])))


def _layer_weights(cfg, w_in, w_uq, w_ukv):
    w = cfg.width
    qkv, f, cq, ckv, kr, qm, gates = jnp.split(w_in, list(_cumsum(cfg.in_splits))[:-1], axis=1)
    wa = jnp.concatenate([qkv, qm], axis=1)
    ws = jnp.concatenate([_pad_cols(f, 0, LANES), cq, ckv, _pad_cols(kr, MLA_NOPE, LANES)], axis=1)
    wq = jnp.pad(w_uq.reshape(cfg.q_rank, cfg.mla_h, MLA_NOPE + MLA_ROPE), ((0, 0), (0, 0), (0, LANES - MLA_NOPE - MLA_ROPE)))
    wq = wq.reshape(cfg.q_rank, cfg.mla_h * LANES)
    kv = w_ukv.reshape(cfg.kv_rank, cfg.mla_h, MLA_NOPE + MLA_V)
    wk = jnp.pad(kv[:, :, :MLA_NOPE], ((0, 0), (0, 0), (0, LANES - MLA_NOPE))).reshape(cfg.kv_rank, cfg.mla_h * LANES)
    wv = kv[:, :, MLA_NOPE:].reshape(cfg.kv_rank, cfg.mla_h * MLA_V)
    del w
    return wa, gates, ws, wq, wk, wv


def _cumsum(xs):
    out, t = [], 0
    for v in xs:
        t += v
        out.append(t)
    return out


def _layer_weight_grads(cfg, dwa, dwg, dws, dwq, dwk, dwv):
    w, qr, kvr = cfg.width, cfg.q_rank, cfg.kv_rank
    off_kr = LANES + qr + kvr + MLA_NOPE
    dw_in = jnp.concatenate([
        dwa[:, :3 * w], dws[:, :cfg.fox_h], dws[:, LANES:LANES + qr], dws[:, LANES + qr:LANES + qr + kvr],
        dws[:, off_kr:off_kr + MLA_ROPE], dwa[:, 3 * w:], dwg], axis=1)
    dw_uq = dwq.reshape(qr, cfg.mla_h, LANES)[:, :, :MLA_NOPE + MLA_ROPE].reshape(qr, cfg.mla_h * (MLA_NOPE + MLA_ROPE))
    dw_ukv = jnp.concatenate([dwk.reshape(kvr, cfg.mla_h, LANES)[:, :, :MLA_NOPE], dwv.reshape(kvr, cfg.mla_h, MLA_V)],
                             axis=2).reshape(kvr, cfg.mla_h * (MLA_NOPE + MLA_V))
    return dw_in, dw_uq, dw_ukv


def _attn_specs(cfg, batch):
    common = dict(batch=batch, sq=cfg.seq, chunk=cfg.chunk, tq=256, tk=256)
    fox = Attn(sk=cfg.seq, groups=cfg.fox_h // 2, hq=2, hv=2, mode="fox", scale=FOX_DH ** -0.5, **common)
    mla = Attn(sk=cfg.seq, groups=cfg.mla_h // 2, hq=1, hv=2, mode="chunk",
               scale=(MLA_NOPE + MLA_ROPE) ** -0.5, **common)
    mem = Attn(sk=cfg.n_mem, groups=cfg.mem_h, hq=1, hv=1, mode="none", scale=MEM_DH ** -0.5, **common)
    return fox, mla, mem


def _small_core(cfg, ps, bf, gq, gkv):
    qr, kvr = cfg.q_rank, cfg.kv_rank
    z = ps[:, :LANES] + bf
    logf = jnp.minimum(z, 0.0) - jnp.log1p(jnp.exp(-jnp.abs(z)))
    nq = _rms(ps[:, LANES:LANES + qr], gq)
    nkv = _rms(ps[:, LANES + qr:LANES + qr + kvr], gkv)
    return logf, nq, nkv


def _cf_layouts(cfg, sp, cf, batch):
    h = cfg.fox_h
    tk = min(sp.tk, sp.sk)
    cfq = jnp.repeat(cf[:, :h], FOX_DH, axis=1)
    cfk = cf[:, :h].reshape(batch, cfg.seq // tk, tk, h // 2, 2).transpose(0, 3, 1, 4, 2)
    return cfq, cfk


def _layer_fwd(cfg, l, batch, h, hb, mem_b, rope_c, rope_s, lw, bf_pad, g_cq, g_ckv, ln1, ln2):
    wa, wg, ws, wq, wk, wv, wmkv, wbr, wout, wff1, wff2 = lw
    w, d = cfg.width, cfg.d
    fox, mla, mem = _attn_specs(cfg, batch)
    nw = w // LANES
    pa = _mm(f"proj_a_{l}", "nn", [(hb, wa)], [BF16])
    gl = _mm(f"proj_gates_{l}", "nn", [(hb, wg)], [F32])
    ps = _mm(f"proj_small_{l}", "nn", [(hb, ws)], [F32], tn=cfg.small_w)

    def small_fwd(ps_, c_, s_, bf_, gq_, gkv_):
        logf, nq, nkv = _small_core(cfg, ps_, bf_, gq_, gkv_)
        kpe = _rope(ps_[:, cfg.small_w - LANES:], c_, s_)
        return logf, nq, nkv, kpe

    logf, nq, nkv, kpe = _rowwise(
        f"small_fwd_{l}", small_fwd, [ps, rope_c, rope_s], [bf_pad, g_cq, g_ckv],
        [(LANES, F32), (cfg.q_rank, BF16), (cfg.kv_rank, BF16), (LANES, F32)])
    cf = _seq_cumsum(f"cum_forget_{l}", logf, batch, cfg.seq, reverse=False)
    cfq, cfk = _cf_layouts(cfg, fox, cf, batch)

    qf = _mm(f"mla_q_{l}", "nn", [(nq, wq)], [BF16], tn=wq.shape[1],
             epi=lambda acc, c_, s_: (_rope(acc, c_, s_),), row_extras=[rope_c, rope_s])
    kf = _mm(f"mla_k_{l}", "nn", [(nkv, wk)], [BF16], tn=wk.shape[1],
             epi=lambda acc, kp: (acc + jnp.tile(kp, (1, cfg.mla_h)),), row_extras=[kpe])
    vb = _mm(f"mla_v_{l}", "nn", [(nkv, wv)], [BF16])
    mkv = _mm(f"mem_kv_{l}", "nn", [(mem_b, wmkv)], [BF16])

    o_a, lse_a = _attn_fwd(f"fox_fwd_{l}", fox, (pa, 0), (pa, nw), (pa, 2 * nw), cfq, cfk)
    o_b, lse_b = _attn_fwd(f"mla_fwd_{l}", mla, (qf, 0), (kf, 0), (vb, 0))
    o_c, lse_c = _attn_fwd(f"mem_fwd_{l}", mem, (pa, 3 * nw), (mkv, 0), (mkv, nw))
    bps = [_mm(f"branch_{n}_{l}", "nn", [(o, wbr[n])], [F32]) for n, o in enumerate((o_a, o_b, o_c))]

    def merge(gl_, b0, b1, b2):
        g = jax.nn.sigmoid(gl_)
        return (g[:, :d] * b0 + g[:, d:2 * d] * b1 + g[:, 2 * d:] * b2,)

    (merged,) = _rowwise(f"merge_{l}", merge, [gl] + bps, [], [(d, BF16)])

    def post_ln(acc, res, g_, b_):
        z = cfg.alpha * res + acc
        y = _ln(z, g_, b_)
        return z, y, y

    z1, h1, h1b = _mm(f"out_ln1_{l}", "nn", [(merged, wout)], [F32, F32, BF16], tm=256, tn=d,
                      epi=post_ln, row_extras=[h], bc_extras=list(ln1))
    u, a = _mm(f"ff1_{l}", "nn", [(h1b, wff1)], [BF16, BF16],
               epi=lambda acc: (acc, jnp.square(jnp.maximum(acc, 0.0))))
    z2, h2, h2b = _mm(f"ff2_ln2_{l}", "nn", [(a, wff2)], [F32, F32, BF16], tm=256, tn=d,
                      epi=post_ln, row_extras=[h1], bc_extras=list(ln2))
    saved = dict(hb=hb, pa=pa, gl=gl, ps=ps, nq=nq, nkv=nkv, cfq=cfq, cfk=cfk, qf=qf, kf=kf, vb=vb, mkv=mkv,
                 o=(o_a, o_b, o_c), lse=(lse_a, lse_b, lse_c), bps=bps, merged=merged, z1=z1, h1b=h1b, u=u, a=a, z2=z2)
    return h2, h2b, saved


def _ln_bwd(name, cfg, ga, gb, z, g, b):
    d = cfg.d

    def fn(*vals):
        if gb is None:
            ga_, z_, g_, b_ = vals
            dy = ga_
        else:
            ga_, gb_, z_, g_, b_ = vals
            dy = ga_ + cfg.alpha * gb_
        _, vjp = jax.vjp(_ln, z_, g_, b_)
        dz, dg, db = vjp(dy)
        return dz, dz, dg, db

    rows = [ga, z] if gb is None else [ga, gb, z]
    return _rowwise(name, fn, rows, [g, b], [(d, F32), (d, BF16)], accs=[(1, d), (1, d)])


def _layer_bwd(cfg, l, batch, ga, gb, sv, mem_b, rope_c, rope_s, lw, bf_pad, g_cq, g_ckv, ln1, ln2):
    wa, wg, ws, wq, wk, wv, wmkv, wbr, wout, wff1, wff2 = lw
    w, d = cfg.width, cfg.d
    fox, mla, mem = _attn_specs(cfg, batch)
    nw = w // LANES
    gdt = BF16

    dz2, dz2b, dg2, db2 = _ln_bwd(f"ln2_bwd_{l}", cfg, ga, gb, sv["z2"], *ln2)
    du = _mm(f"ff2_bwd_x_{l}", "nt", [(dz2b, wff2)], [BF16],
             epi=lambda acc, u_: (acc * (2.0 * jnp.maximum(u_.astype(F32), 0.0)),), row_extras=[sv["u"]])
    dwff2 = _mm(f"ff2_bwd_w_{l}", "tn", [(sv["a"], dz2b)], [gdt])
    dwff1 = _mm(f"ff1_bwd_w_{l}", "tn", [(sv["h1b"], du)], [gdt])
    dh1 = _mm(f"ff1_bwd_x_{l}", "nt", [(du, wff1)], [F32])
    dz1, dz1b, dg1, db1 = _ln_bwd(f"ln1_bwd_{l}", cfg, dh1, dz2, sv["z1"], *ln1)
    dmerged = _mm(f"out_bwd_x_{l}", "nt", [(dz1b, wout)], [F32])
    dwout = _mm(f"out_bwd_w_{l}", "tn", [(sv["merged"], dz1b)], [gdt])

    def merge_bwd(dm, gl_, b0, b1, b2):
        def f(gl__, b0_, b1_, b2_):
            g = jax.nn.sigmoid(gl__)
            return g[:, :d] * b0_ + g[:, d:2 * d] * b1_ + g[:, 2 * d:] * b2_

        _, vjp = jax.vjp(f, gl_, b0, b1, b2)
        return vjp(dm)

    dgl, db0, db1_, db2_ = _rowwise(f"merge_bwd_{l}", merge_bwd, [dmerged, sv["gl"]] + sv["bps"], [],
                                    [(3 * d, BF16), (d, BF16), (d, BF16), (d, BF16)])
    dbps = (db0, db1_, db2_)
    dos = [_mm(f"branch_bwd_x_{n}_{l}", "nt", [(dbps[n], wbr[n])], [BF16]) for n in range(3)]
    dwbr = [_mm(f"branch_bwd_w_{n}_{l}", "tn", [(sv["o"][n], dbps[n])], [gdt]) for n in range(3)]

    pa = sv["pa"]
    dq_a, dk_a, dv_a, dcfk = _attn_bwd(f"fox_bwd_{l}", fox, (pa, 0), (pa, nw), (pa, 2 * nw), sv["o"][0], sv["lse"][0],
                                       dos[0], sv["cfq"], sv["cfk"])
    dqf, dkf, dvb = _attn_bwd(f"mla_bwd_{l}", mla, (sv["qf"], 0), (sv["kf"], 0), (sv["vb"], 0), sv["o"][1],
                              sv["lse"][1], dos[1])
    dqm, dmk, dmv = _attn_bwd(f"mem_bwd_{l}", mem, (pa, 3 * nw), (sv["mkv"], 0), (sv["mkv"], nw), sv["o"][2],
                              sv["lse"][2], dos[2])
    dwmkv = _mm(f"mem_kv_bwd_w_{l}", "tn", [(mem_b, jnp.concatenate([dmk, dmv], axis=1))], [gdt])

    (dq_raw,) = _rowwise(f"mla_q_rope_bwd_{l}", lambda dy, c_, s_: (_rope_t(dy, c_, s_),), [dqf, rope_c, rope_s], [],
                         [(wq.shape[1], BF16)])
    dwq = _mm(f"mla_q_bwd_w_{l}", "tn", [(sv["nq"], dq_raw)], [gdt])
    dnq = _mm(f"mla_q_bwd_x_{l}", "nt", [(dq_raw, wq)], [F32])
    dwk = _mm(f"mla_k_bwd_w_{l}", "tn", [(sv["nkv"], dkf)], [gdt])
    dwv = _mm(f"mla_v_bwd_w_{l}", "tn", [(sv["nkv"], dvb)], [gdt])
    dnkv = _mm(f"mla_kv_bwd_x_{l}", "nt", [(dkf, wk), (dvb, wv)], [F32])

    tk = min(fox.tk, fox.sk)
    dcf = dcfk.transpose(0, 2, 4, 1, 3).reshape(batch * cfg.seq, cfg.fox_h)
    dcf = jnp.pad(dcf, ((0, 0), (0, LANES - cfg.fox_h)))
    del tk
    dlogf = _seq_cumsum(f"cum_forget_bwd_{l}", dcf, batch, cfg.seq, reverse=True)

    def small_bwd(ps_, dlogf_, dnq_, dnkv_, dkf_, c_, s_, bf_, gq_, gkv_):
        _, vjp = jax.vjp(functools.partial(_small_core, cfg), ps_, bf_, gq_, gkv_)
        dps, dbf, dgq, dgkv = vjp((dlogf_, dnq_, dnkv_))
        dkpe = dkf_[:, :LANES].astype(F32)
        for hh in range(1, cfg.mla_h):
            dkpe = dkpe + dkf_[:, hh * LANES:(hh + 1) * LANES].astype(F32)
        lane = lax.broadcasted_iota(jnp.int32, (1, LANES), 1)
        dkpe = jnp.where((lane >= MLA_NOPE) & (lane < MLA_NOPE + MLA_ROPE), dkpe, 0.0)
        dkr = _rope_t(dkpe, c_, s_)
        dps = jnp.concatenate([dps[:, :cfg.small_w - LANES], dkr], axis=1)
        return dps, dbf, dgq, dgkv

    dps, dbf, dgq, dgkv = _rowwise(
        f"small_bwd_{l}", small_bwd, [sv["ps"], dlogf, dnq, dnkv, dkf, rope_c, rope_s], [bf_pad, g_cq, g_ckv],
        [(cfg.small_w, BF16)], accs=[(1, LANES), (1, cfg.q_rank), (1, cfg.kv_rank)])

    dpa = jnp.concatenate([dq_a.astype(BF16), dk_a, dv_a, dqm.astype(BF16)], axis=1)
    hb = sv["hb"]
    dh = _mm(f"proj_bwd_x_{l}", "nt", [(dpa, wa), (dgl, wg), (dps, ws)], [F32], tm=256)
    dwa = _mm(f"proj_a_bwd_w_{l}", "tn", [(hb, dpa)], [gdt])
    dwg = _mm(f"proj_gates_bwd_w_{l}", "tn", [(hb, dgl)], [gdt])
    dws = _mm(f"proj_small_bwd_w_{l}", "tn", [(hb, dps)], [gdt], tn=cfg.small_w)
    dw_in, dw_uq, dw_ukv = _layer_weight_grads(cfg, dwa, dwg, dws, dwq, dwk, dwv)
    big = dict(w_in=dw_in, w_uq=dw_uq, w_ukv=dw_ukv, w_mem_kv=dwmkv, w_br=jnp.stack(dwbr), w_out=dwout,
               w_ff1=dwff1, w_ff2=dwff2)
    small = dict(b_forget=dbf[0, :cfg.fox_h], g_cq=dgq[0], g_ckv=dgkv[0], ln1_g=dg1[0], ln1_b=db1[0],
                 ln2_g=dg2[0], ln2_b=db2[0])
    return dh, dz1, big, small


def _rope_tables(positions):
    inv_freq = ROPE_BASE ** (-jnp.arange(0, MLA_ROPE, 2, dtype=F32) / MLA_ROPE)
    ang = positions.astype(F32).reshape(-1)[:, None] * inv_freq
    cos, sin = jnp.cos(ang), jnp.sin(ang)
    t = ang.shape[0]
    rope_c = jnp.concatenate([jnp.ones((t, MLA_NOPE), F32), cos, cos, jnp.zeros((t, LANES - MLA_NOPE - MLA_ROPE), F32)], axis=1)
    rope_s = jnp.concatenate([jnp.zeros((t, MLA_NOPE), F32), -sin, sin, jnp.zeros((t, LANES - MLA_NOPE - MLA_ROPE), F32)], axis=1)
    return rope_c, rope_s


def _local_step(cfg, x, mem, positions, target, small_w, big_w):
    batch = x.shape[0]
    d, depth = cfg.d, cfg.depth
    t = batch * cfg.seq
    x2, tgt = x.reshape(t, d), target.reshape(t, d)
    mem_b = mem.reshape(batch * cfg.n_mem, d).astype(BF16)
    rope_c, rope_s = _rope_tables(positions)
    row = lambda v: v.reshape(1, -1)
    ln_in = (row(small_w["ln_in_g"]), row(small_w["ln_in_b"]))

    h, hb = _rowwise("ln_in", lambda x_, g_, b_: (_ln(x_, g_, b_),) * 2, [x2], list(ln_in), [(d, F32), (d, BF16)])
    layers, saves = [], []
    for l in range(depth):
        lw = _layer_weights(cfg, big_w["w_in"][l], big_w["w_uq"][l], big_w["w_ukv"][l]) + (
            big_w["w_mem_kv"][l], big_w["w_br"][l], big_w["w_out"][l], big_w["w_ff1"][l], big_w["w_ff2"][l])
        par = dict(
            lw=lw, bf_pad=jnp.pad(row(small_w["b_forget"][l]), ((0, 0), (0, LANES - cfg.fox_h))),
            g_cq=row(small_w["g_cq"][l]), g_ckv=row(small_w["g_ckv"][l]),
            ln1=(row(small_w["ln1_g"][l]), row(small_w["ln1_b"][l])),
            ln2=(row(small_w["ln2_g"][l]), row(small_w["ln2_b"][l])))
        layers.append(par)
        h, hb, sv = _layer_fwd(cfg, l, batch, h, hb, mem_b, rope_c, rope_s, **par)
        saves.append(sv)

    def loss_fn(y, tg):
        err = y - tg
        part = 0.5 * jnp.sum(jnp.mean(err * err, axis=-1, keepdims=True), axis=0, keepdims=True)
        return err * (1.0 / d), jnp.broadcast_to(part, (1, LANES))

    ga, loss_acc = _rowwise("loss", loss_fn, [h, tgt], [], [(d, F32)], accs=[(1, LANES)])
    gb = None
    big_g = {k: [None] * depth for k in ("w_in", "w_uq", "w_ukv", "w_mem_kv", "w_br", "w_out", "w_ff1", "w_ff2")}
    small_g = {k: [None] * depth for k in ("b_forget", "g_cq", "g_ckv", "ln1_g", "ln1_b", "ln2_g", "ln2_b")}
    for l in reversed(range(depth)):
        ga, gb, big, small = _layer_bwd(cfg, l, batch, ga, gb, saves[l], mem_b, rope_c, rope_s, **layers[l])
        for k, v in big.items():
            big_g[k][l] = v
        for k, v in small.items():
            small_g[k][l] = v
    dx, _, dg_in, db_in = _ln_bwd("ln_in_bwd", cfg, ga, gb, x2, *ln_in)
    big_g = {k: jnp.stack(v) for k, v in big_g.items()}
    small_g = {k: jnp.stack(v) for k, v in small_g.items()}
    small_g["ln_in_g"], small_g["ln_in_b"] = dg_in[0], db_in[0]
    return loss_acc[0, 0], dx.reshape(x.shape), big_g, small_g


BIG = ("w_in", "w_uq", "w_ukv", "w_mem_kv", "w_br", "w_out", "w_ff1", "w_ff2")
SMALL = ("ln_in_g", "ln_in_b", "b_forget", "g_cq", "g_ckv", "ln1_g", "ln1_b", "ln2_g", "ln2_b")
ROW_CUT = ("w_mem_kv", "w_out", "w_ff2")


def _shard_2d(a):
    cols = a.shape[-1]
    rows = a.size // cols
    return a.reshape(2, rows // 2, cols)


def _full_from_slots(name, slots, shard_shape):
    parts = slots.reshape((N_CHIPS,) + shard_shape)
    axis = len(shard_shape) - (2 if name in ROW_CUT else 1)
    return jnp.concatenate([parts[i] for i in range(N_CHIPS)], axis=axis)


def _slots_from_full(name, full, shard_shape):
    axis = len(shard_shape) - (2 if name in ROW_CUT else 1)
    parts = jnp.stack(jnp.split(full, N_CHIPS, axis=axis))
    cols = shard_shape[-1]
    return parts.reshape(N_CHIPS, 2, -1, cols)


def _pack_small(cfg, vals):
    flat = jnp.concatenate([vals[k].reshape(-1).astype(F32) for k in SMALL])
    pad = (-flat.shape[0]) % (LANES * LANES)
    return jnp.pad(flat, (0, pad)).reshape(-1, LANES)


def _unpack_small(packed, like):
    flat, out, off = packed.reshape(-1), {}, 0
    for k in SMALL:
        n = like[k].size
        out[k] = flat[off:off + n].reshape(like[k].shape)
        off += n
    return out


def _step(cfg, x, mem, positions, target, w, m, v):
    shards = [_shard_2d(w[k].astype(BF16)) for k in BIG]
    gathered = _gather_weights(shards)
    big_w = {k: _full_from_slots(k, _to_absolute(g), w[k].shape) for k, g in zip(BIG, gathered)}
    small_w = {k: w[k] for k in SMALL}
    loss_local, dx, big_g, small_g = _local_step(cfg, x, mem, positions, target, small_w, big_w)
    loss = lax.psum(loss_local, ("x", "y", "c"))

    grads_rel = [_to_relative(_slots_from_full(k, big_g[k], w[k].shape)) for k in BIG]
    reduced = _reduce_grads(grads_rel)
    outs_big = {}
    for k, g in zip(BIG, reduced):
        shape3 = (1,) + g.shape[0:1] + g.shape[1:]
        shape3 = (1, g.shape[0] * g.shape[1], g.shape[2])
        args = [a.reshape(shape3) for a in (w[k], g, m[k], v[k])]
        dl, nm, nv = _slabwise(f"adamw_{k}", _adamw_math, args, [F32, F32, F32])
        outs_big[k] = tuple(a.reshape(w[k].shape) for a in (g, dl, nm, nv))

    g_small = _allreduce_small(_pack_small(cfg, small_g))
    packs = [_pack_small(cfg, {k: d_[k] for k in SMALL}) for d_ in (w, m, v)]
    dl, nm, nv = _slabwise("adamw_small", _adamw_math, [a[None] for a in (packs[0], g_small, packs[1], packs[2])],
                           [F32, F32, F32])
    outs_small = [_unpack_small(a[0] if a.ndim == 3 else a, w) for a in (g_small, dl, nm, nv)]

    names = SMALL[:2] + ("w_in", "b_forget", "w_uq", "g_cq", "w_ukv", "g_ckv", "w_mem_kv", "w_br", "w_out",
                         "ln1_g", "ln1_b", "w_ff1", "w_ff2", "ln2_g", "ln2_b")
    result = [loss, dx]
    for part in range(4):
        for k in names:
            result.append(outs_big[k][part] if k in outs_big else outs_small[part][k])
    return tuple(result)


def kernel(x, mem, positions, ln_in_g, ln_in_b, w_in, b_forget, w_uq, g_cq, w_ukv, g_ckv, w_mem_kv, w_br, w_out, ln1_g, ln1_b, w_ff1, w_ff2, ln2_g, ln2_b, loss_target, m_ln_in_g, m_ln_in_b, m_w_in, m_b_forget, m_w_uq, m_g_cq, m_w_ukv, m_g_ckv, m_w_mem_kv, m_w_br, m_w_out, m_ln1_g, m_ln1_b, m_w_ff1, m_w_ff2, m_ln2_g, m_ln2_b, v_ln_in_g, v_ln_in_b, v_w_in, v_b_forget, v_w_uq, v_g_cq, v_w_ukv, v_g_ckv, v_w_mem_kv, v_w_br, v_w_out, v_ln1_g, v_ln1_b, v_w_ff1, v_w_ff2, v_ln2_g, v_ln2_b):
    w = dict(ln_in_g=ln_in_g, ln_in_b=ln_in_b, w_in=w_in, b_forget=b_forget, w_uq=w_uq, g_cq=g_cq, w_ukv=w_ukv,
             g_ckv=g_ckv, w_mem_kv=w_mem_kv, w_br=w_br, w_out=w_out, ln1_g=ln1_g, ln1_b=ln1_b, w_ff1=w_ff1,
             w_ff2=w_ff2, ln2_g=ln2_g, ln2_b=ln2_b)
    m = dict(ln_in_g=m_ln_in_g, ln_in_b=m_ln_in_b, w_in=m_w_in, b_forget=m_b_forget, w_uq=m_w_uq, g_cq=m_g_cq,
             w_ukv=m_w_ukv, g_ckv=m_g_ckv, w_mem_kv=m_w_mem_kv, w_br=m_w_br, w_out=m_w_out, ln1_g=m_ln1_g,
             ln1_b=m_ln1_b, w_ff1=m_w_ff1, w_ff2=m_w_ff2, ln2_g=m_ln2_g, ln2_b=m_ln2_b)
    v = dict(ln_in_g=v_ln_in_g, ln_in_b=v_ln_in_b, w_in=v_w_in, b_forget=v_b_forget, w_uq=v_w_uq, g_cq=v_g_cq,
             w_ukv=v_w_ukv, g_ckv=v_g_ckv, w_mem_kv=v_w_mem_kv, w_br=v_w_br, w_out=v_w_out, ln1_g=v_ln1_g,
             ln1_b=v_ln1_b, w_ff1=v_w_ff1, w_ff2=v_w_ff2, ln2_g=v_ln2_g, ln2_b=v_ln2_b)
    return _step(Cfg(), x, mem, positions, loss_target, w, m, v)
```

```python
import functools
from typing import NamedTuple

import jax
import jax.numpy as jnp
from jax import lax
from jax.experimental import pallas as pl
from jax.experimental.pallas import tpu as pltpu

F32 = jnp.float32
BF16 = jnp.bfloat16
MESH = pl.DeviceIdType.MESH

LANES = 128
SUBLANES = 8
VMEM_BYTES = 64 * 1024 * 1024
N_CHIPS = 4
N_DEV = 8

FOX_DH = 64
MLA_NOPE = 64
MLA_ROPE = 32
MLA_V = 64
MEM_DH = 128
ROPE_BASE = 10000.0
LN_EPS = 1e-5
RMS_EPS = 1e-6
NEG_INF = -1e30
ATTN_TILE = 256

ADAM_LR = 0.001
ADAM_B1 = 0.9
ADAM_B2 = 0.999
ADAM_EPS = 1e-08
ADAM_WD = 0.01
ADAM_STEP = 10


class Cfg(NamedTuple):
    d: int = 1024
    depth: int = 4
    seq: int = 2048
    chunk: int = 64
    n_mem: int = 256
    fox_h: int = 8
    mla_h: int = 8
    q_rank: int = 384
    kv_rank: int = 256
    mem_h: int = 4
    d_ff: int = 4096

    @property
    def width(self):
        return self.fox_h * FOX_DH

    @property
    def alpha(self):
        return (2 * self.depth) ** 0.25

    @property
    def small_w(self):
        return LANES + self.q_rank + self.kv_rank + LANES

    @property
    def in_splits(self):
        return (3 * self.width, self.fox_h, self.q_rank, self.kv_rank, MLA_ROPE, self.width, 3 * self.d)


def _pcall(body, **kw):
    return pl.pallas_call(body, **kw)


def _nbytes(shape, dtype):
    n = 1
    for s in shape:
        n *= s
    return n * jnp.dtype(dtype).itemsize


def _tile(dim, target):
    if dim <= target:
        return dim
    t = target - target % LANES
    while t >= LANES:
        if dim % t == 0:
            return t
        t -= LANES
    return dim


def _params(block_bytes, scratch_bytes=0):
    est = 2 * block_bytes + scratch_bytes + 24 * 1024 * 1024
    return pltpu.CompilerParams(vmem_limit_bytes=int(min(max(est, 32 * 1024 * 1024), VMEM_BYTES - 4 * 1024 * 1024)))


_DIMS = {"nn": (((1,), (0,)), ((), ())), "nt": (((1,), (1,)), ((), ())), "tn": (((0,), (0,)), ((), ()))}


def _mm(name, mode, pairs, out_dtypes, tm=512, tn=512, epi=None, row_extras=(), bc_extras=()):
    a0, b0 = pairs[0]
    m = a0.shape[1] if mode == "tn" else a0.shape[0]
    n = b0.shape[0] if mode == "nt" else b0.shape[1]
    tm, tn = _tile(m, tm), _tile(n, tn)
    in_specs, ops, blk = [], [], 0
    for a, b in pairs:
        if mode == "tn":
            k = a.shape[0]
            sa, sha = pl.BlockSpec((k, tm), lambda i, j: (0, i)), (k, tm)
        else:
            k = a.shape[1]
            sa, sha = pl.BlockSpec((tm, k), lambda i, j: (i, 0)), (tm, k)
        if mode == "nt":
            sb, shb = pl.BlockSpec((tn, k), lambda i, j: (j, 0)), (tn, k)
        else:
            sb, shb = pl.BlockSpec((k, tn), lambda i, j: (0, j)), (k, tn)
        in_specs += [sa, sb]
        ops += [a, b]
        blk += _nbytes(sha, a.dtype) + _nbytes(shb, b.dtype)
    for e in row_extras:
        w = e.shape[1]
        if w == n:
            in_specs.append(pl.BlockSpec((tm, tn), lambda i, j: (i, j)))
            blk += _nbytes((tm, tn), e.dtype)
        else:
            in_specs.append(pl.BlockSpec((tm, w), lambda i, j: (i, 0)))
            blk += _nbytes((tm, w), e.dtype)
        ops.append(e)
    for e in bc_extras:
        r, w = e.shape
        if w == n:
            in_specs.append(pl.BlockSpec((r, tn), lambda i, j: (0, j)))
        else:
            in_specs.append(pl.BlockSpec((r, w), lambda i, j: (0, 0)))
        blk += _nbytes((r, w), e.dtype)
        ops.append(e)
    npairs, nrow, nbc, nout = len(pairs), len(row_extras), len(bc_extras), len(out_dtypes)
    dims = _DIMS[mode]

    def body(*refs):
        acc = None
        for p in range(npairs):
            a = refs[2 * p][...].astype(BF16)
            b = refs[2 * p + 1][...].astype(BF16)
            d = lax.dot_general(a, b, dims, preferred_element_type=F32)
            acc = d if acc is None else acc + d
        ex = [r[...] for r in refs[2 * npairs:2 * npairs + nrow + nbc]]
        outs = (acc,) if epi is None else epi(acc, *ex)
        for o_ref, o in zip(refs[2 * npairs + nrow + nbc:], outs):
            o_ref[...] = o.astype(o_ref.dtype)

    blk += sum(_nbytes((tm, tn), dt) for dt in out_dtypes) + 2 * _nbytes((tm, tn), F32)
    res = _pcall(
        body,
        name=name,
        grid=(m // tm, n // tn),
        in_specs=in_specs,
        out_specs=[pl.BlockSpec((tm, tn), lambda i, j: (i, j)) for _ in range(nout)],
        out_shape=[jax.ShapeDtypeStruct((m, n), dt) for dt in out_dtypes],
        compiler_params=_params(blk),
    )(*ops)
    return res[0] if nout == 1 else res


def _rowwise(name, fn, row_ins, bc_ins, outs, accs=(), tm=256):
    rows = row_ins[0].shape[0]
    tm = min(tm, rows)
    assert rows % tm == 0
    nrow, nbc, nout, nacc = len(row_ins), len(bc_ins), len(outs), len(accs)
    in_specs = [pl.BlockSpec((tm, a.shape[1]), lambda i: (i, 0)) for a in row_ins]
    in_specs += [pl.BlockSpec(a.shape, lambda i: (0, 0)) for a in bc_ins]
    out_specs = [pl.BlockSpec((tm, w), lambda i: (i, 0)) for w, _ in outs]
    out_specs += [pl.BlockSpec(s, lambda i: (0, 0)) for s in accs]
    out_shape = [jax.ShapeDtypeStruct((rows, w), dt) for w, dt in outs]
    out_shape += [jax.ShapeDtypeStruct(s, F32) for s in accs]

    def body(*refs):
        vals = fn(*[r[...] for r in refs[:nrow + nbc]])
        o_refs = refs[nrow + nbc:]
        for r, v in zip(o_refs[:nout], vals[:nout]):
            r[...] = v.astype(r.dtype)
        if nacc:
            @pl.when(pl.program_id(0) == 0)
            def _():
                for r in o_refs[nout:]:
                    r[...] = jnp.zeros(r.shape, F32)

            for r, v in zip(o_refs[nout:], vals[nout:]):
                r[...] += v

    blk = sum(_nbytes((tm, a.shape[1]), a.dtype) for a in row_ins) + sum(_nbytes(a.shape, a.dtype) for a in bc_ins)
    blk += sum(_nbytes((tm, w), dt) for w, dt in outs) + sum(_nbytes(s, F32) for s in accs)
    res = _pcall(
        body,
        name=name,
        grid=(rows // tm,),
        in_specs=in_specs,
        out_specs=out_specs,
        out_shape=out_shape,
        compiler_params=_params(2 * blk),
    )(*row_ins, *bc_ins)
    return res


def _ln(z, g, b):
    mu = jnp.mean(z, axis=-1, keepdims=True)
    zc = z - mu
    var = jnp.mean(zc * zc, axis=-1, keepdims=True)
    return zc * lax.rsqrt(var + LN_EPS) * g + b


def _rms(x, g):
    return x * lax.rsqrt(jnp.mean(x * x, axis=-1, keepdims=True) + RMS_EPS) * g


def _colsum(v):
    return jnp.sum(v, axis=0, keepdims=True)


def _rope_swap(x):
    w = x.shape[1]
    lane = lax.broadcasted_iota(jnp.int32, (1, w), 1) % LANES
    from_left = pltpu.roll(x, 16, 1)
    from_right = pltpu.roll(x, w - 16, 1)
    lo = (lane >= MLA_NOPE) & (lane < MLA_NOPE + 16)
    hi = (lane >= MLA_NOPE + 16) & (lane < MLA_NOPE + 32)
    return jnp.where(hi, from_left, jnp.where(lo, from_right, 0.0))


def _rope(x, cos_t, sin_t):
    nh = x.shape[1] // LANES
    ct, st = jnp.tile(cos_t, (1, nh)), jnp.tile(sin_t, (1, nh))
    return x * ct + _rope_swap(x) * st


def _rope_t(dy, cos_t, sin_t):
    nh = dy.shape[1] // LANES
    ct, st = jnp.tile(cos_t, (1, nh)), jnp.tile(sin_t, (1, nh))
    return dy * ct + _rope_swap(dy * st)


def _seq_cumsum(name, x, batch, seq, reverse, tb=256):
    tb = min(tb, seq)
    nb = seq // tb

    def body(x_ref, o_ref, carry):
        @pl.when(pl.program_id(1) == 0)
        def _():
            carry[...] = jnp.zeros(carry.shape, F32)

        r = lax.broadcasted_iota(jnp.int32, (tb, tb), 0)
        c = lax.broadcasted_iota(jnp.int32, (tb, tb), 1)
        tri = jnp.where((c >= r) if reverse else (c <= r), 1.0, 0.0).astype(BF16)
        v = x_ref[...]
        hi = v.astype(BF16)
        r1 = v - hi.astype(F32)
        mid = r1.astype(BF16)
        lo = (r1 - mid.astype(F32)).astype(BF16)
        out = carry[...] + sum(jnp.dot(tri, p, preferred_element_type=F32) for p in (hi, mid, lo))
        o_ref[...] = out
        carry[...] = out[0:1, :] if reverse else out[tb - 1:tb, :]

    if reverse:
        idx = lambda b, i: (b * nb + nb - 1 - i, 0)
    else:
        idx = lambda b, i: (b * nb + i, 0)
    return _pcall(
        body,
        name=name,
        grid=(batch, nb),
        in_specs=[pl.BlockSpec((tb, LANES), idx)],
        out_specs=pl.BlockSpec((tb, LANES), idx),
        out_shape=jax.ShapeDtypeStruct(x.shape, F32),
        scratch_shapes=[pltpu.VMEM((1, LANES), F32)],
        compiler_params=_params(4 * tb * LANES * 4),
    )(x)


class Attn(NamedTuple):
    batch: int
    sq: int
    sk: int
    groups: int
    hq: int
    hv: int
    mode: str
    scale: float
    chunk: int
    tq: int
    tk: int

    @property
    def hg(self):
        return self.hv

    @property
    def qw(self):
        return LANES * self.hg // self.hq

    @property
    def dv(self):
        return LANES // self.hv


def _head_lanes(j, dv):
    lane = lax.broadcasted_iota(jnp.int32, (1, LANES), 1)
    return (lane >= j * dv) & (lane < (j + 1) * dv)


def _head_q(sp, j, q_blk):
    if sp.hq == 2:
        return jnp.where(_head_lanes(j, FOX_DH), q_blk, jnp.zeros_like(q_blk))
    return q_blk[:, LANES * j:LANES * (j + 1)]


def _head_rows(sp, j):
    return slice(j * sp.dv, (j + 1) * sp.dv) if sp.hg == 2 else slice(None)


def _scores_t(sp, j, k_c, q_j, cfq_rows, cfk_rep, k0, q0, masked):
    tk, tq = k_c.shape[0], q_j.shape[0]
    k_j = k_c if sp.hq == 2 else k_c[:, LANES * j:LANES * (j + 1)]
    st = lax.dot_general(k_j, q_j, _DIMS["nt"], preferred_element_type=F32) * sp.scale
    if sp.mode == "fox":
        st = st + (cfq_rows[j:j + 1, :] - jnp.tile(cfk_rep[:, LANES * j:LANES * (j + 1)], (1, tq // LANES)))
    if masked:
        kidx = k0 + lax.broadcasted_iota(jnp.int32, (tk, tq), 0)
        qidx = q0 + lax.broadcasted_iota(jnp.int32, (tk, tq), 1)
        if sp.mode == "chunk":
            shift = sp.chunk.bit_length() - 1
            kidx, qidx = jnp.right_shift(kidx, shift), jnp.right_shift(qidx, shift)
        st = jnp.where(kidx <= qidx, st, NEG_INF)
    return st


def _attn_fwd(name, sp, q, k, v, cfq=None, cfk=None):
    (qa, qo), (ka, ko), (va, vo) = q, k, v
    tq, tk, hg, qw = sp.tq, sp.tk, sp.hg, sp.qw
    nqb, nkc = sp.sq // tq, sp.sk // tk
    fox, causal = sp.mode == "fox", sp.mode != "none"
    assert sp.sq % tq == 0 and sp.sk % tk == 0 and (not causal or (tq == tk and sp.sq == sp.sk))

    def body(*refs):
        if fox:
            q_ref, k_ref, v_ref, cfq_ref, cfk_ref, o_ref, lse_ref, acc_scr = refs
        else:
            q_ref, k_ref, v_ref, o_ref, lse_ref, acc_scr = refs
        i = pl.program_id(2)
        q0 = i * tq
        q_blk = q_ref[...]
        qs = [_head_q(sp, j, q_blk) for j in range(hg)]
        acc_scr[...] = jnp.zeros(acc_scr.shape, F32)

        def chunk(kc, carry, masked):
            ms, ls = carry
            k0 = pl.multiple_of(kc * tk, tk)
            k_c = k_ref[pl.ds(k0, tk), :]
            v_c = v_ref[pl.ds(k0, tk), :]
            new_m, new_l = [], []
            for j in range(hg):
                st = _scores_t(sp, j, k_c, qs[j], cfq_ref[...] if fox else None,
                               cfk_ref[pl.ds(k0, tk), :] if fox else None, k0, q0, masked)
                m_new = jnp.maximum(ms[j], jnp.max(st, axis=0, keepdims=True))
                alpha = jnp.exp(ms[j] - m_new)
                pt = jnp.exp(st - m_new)
                new_m.append(m_new)
                new_l.append(alpha * ls[j] + jnp.sum(pt, axis=0, keepdims=True))
                pv = lax.dot_general(v_c, pt.astype(BF16), _DIMS["tn"], preferred_element_type=F32)
                r = _head_rows(sp, j)
                acc_scr[r, :] = acc_scr[r, :] * alpha + pv[r, :]
            return tuple(new_m), tuple(new_l)

        carry = (tuple(jnp.full((1, tq), NEG_INF, F32) for _ in range(hg)),
                 tuple(jnp.zeros((1, tq), F32) for _ in range(hg)))
        if causal:
            carry = lax.fori_loop(0, i, functools.partial(chunk, masked=False), carry)
            ms, ls = chunk(i, carry, True)
        else:
            ms, ls = lax.fori_loop(0, nkc, functools.partial(chunk, masked=False), carry)
        for j in range(hg):
            r = _head_rows(sp, j)
            acc_scr[r, :] = acc_scr[r, :] / ls[j]
            lse_ref[j:j + 1, :] = ms[j] + jnp.log(ls[j])
        o_ref[...] = acc_scr[...].T

    in_specs = [
        pl.BlockSpec((tq, qw), lambda b, g, i: (b * nqb + i, qo + g)),
        pl.BlockSpec((sp.sk, qw), lambda b, g, i: (b, ko + g)),
        pl.BlockSpec((sp.sk, LANES), lambda b, g, i: (b, vo + g)),
    ]
    ops = [qa, ka, va]
    stat_blk = pl.BlockSpec((None, None, None, hg, tq), lambda b, g, i: (b, g, i, 0, 0))
    if fox:
        in_specs += [stat_blk, pl.BlockSpec((sp.sk, hg * LANES), lambda b, g, i: (b, g))]
        ops += [cfq, cfk]
    blk = _nbytes((tq, qw), BF16) + _nbytes((sp.sk, qw + LANES), BF16) + 2 * _nbytes((tq, LANES), F32)
    blk += _nbytes((sp.sk, hg * LANES), F32) + 6 * _nbytes((tk, tq), F32)
    return _pcall(
        body,
        name=name,
        grid=(sp.batch, sp.groups, nqb),
        in_specs=in_specs,
        out_specs=[pl.BlockSpec((tq, LANES), lambda b, g, i: (b * nqb + i, g)), stat_blk],
        out_shape=[
            jax.ShapeDtypeStruct((sp.batch * sp.sq, sp.groups * LANES), F32),
            jax.ShapeDtypeStruct((sp.batch, sp.groups, nqb, hg, tq), F32),
        ],
        scratch_shapes=[pltpu.VMEM((LANES, tq), F32)],
        compiler_params=_params(blk, tq * LANES * 4),
    )(*ops)


def _attn_bwd(name, sp, q, k, v, o, lse, do, cfq=None, cfk=None):
    (qa, qo), (ka, ko), (va, vo) = q, k, v
    tq, tk, hg, qw, dv = sp.tq, sp.tk, sp.hg, sp.qw, sp.dv
    nqb, nkb = sp.sq // tq, sp.sk // tk
    fox, causal = sp.mode == "fox", sp.mode != "none"
    assert sp.sq % tq == 0 and sp.sk % tk == 0 and (not causal or (tq == tk and sp.sq == sp.sk))

    def body(*refs):
        if fox:
            (q_ref, k_ref, v_ref, lse_ref, do_ref, cfq_ref, cfk_ref, kall_ref, vall_ref, cfkall_ref,
             dqt_ref, dk_ref, dv_ref, dcf_ref, delta_scr, dk_scr, dv_scr, dcf_scr) = refs
        else:
            (q_ref, k_ref, v_ref, o_ref, lse_ref, do_ref,
             dqt_ref, dk_ref, dv_ref, delta_scr, dk_scr, dv_scr) = refs
        kb = pl.program_id(2)
        k0 = kb * tk
        heads = [_head_lanes(j, dv) for j in range(hg)]

        def head_do(j, do_c):
            return jnp.where(heads[j], do_c, jnp.zeros_like(do_c)) if hg == 2 else do_c

        def probs_t(j, k_c, v_c, q_c, do_c, i, cf_keys, c0, masked):
            st = _scores_t(sp, j, k_c, _head_q(sp, j, q_c), cfq_ref[i] if fox else None, cf_keys, c0, i * tq, masked)
            pt = jnp.exp(st - lse_ref[i][j:j + 1, :])
            dpt = lax.dot_general(v_c, head_do(j, do_c), _DIMS["nt"], preferred_element_type=F32)
            return pt, dpt

        @pl.when(kb == 0)
        def _():
            dqt_ref[...] = jnp.zeros(dqt_ref.shape, F32)

            def fill(i, carry):
                r0 = pl.multiple_of(i * tq, tq)
                do_c = do_ref[pl.ds(r0, tq), :]
                if fox:
                    q_c = q_ref[pl.ds(r0, tq), :]

                    def keys(kc, acc, masked):
                        c0 = pl.multiple_of(kc * tk, tk)
                        out = []
                        for j in range(hg):
                            pt, dpt = probs_t(j, kall_ref[pl.ds(c0, tk), :], vall_ref[pl.ds(c0, tk), :], q_c, do_c, i,
                                              cfkall_ref[pl.ds(c0, tk), :], c0, masked)
                            out.append(acc[j] + jnp.sum(pt * dpt, axis=0, keepdims=True))
                        return tuple(out)

                    d = lax.fori_loop(0, i, functools.partial(keys, masked=False),
                                      tuple(jnp.zeros((1, tq), F32) for _ in range(hg)))
                    d = keys(i, d, True)
                    for j in range(hg):
                        delta_scr[i, j:j + 1, :] = d[j]
                else:
                    prod_t = (do_c.astype(F32) * o_ref[pl.ds(r0, tq), :]).T
                    for j in range(hg):
                        delta_scr[i, j:j + 1, :] = jnp.sum(prod_t[_head_rows(sp, j), :], axis=0, keepdims=True)
                return carry

            lax.fori_loop(0, nqb, fill, 0)

        k_blk = k_ref[...]
        v_blk = v_ref[...]
        k_t = k_blk.astype(F32).T.astype(BF16)
        dk_scr[...] = jnp.zeros(dk_scr.shape, F32)
        dv_scr[...] = jnp.zeros(dv_scr.shape, F32)
        if fox:
            dcf_scr[...] = jnp.zeros(dcf_scr.shape, F32)

        def qblock(i, carry, masked):
            r0 = pl.multiple_of(i * tq, tq)
            q_c = q_ref[pl.ds(r0, tq), :]
            do_c = do_ref[pl.ds(r0, tq), :]
            for j in range(hg):
                pt, dpt = probs_t(j, k_blk, v_blk, q_c, do_c, i, cfk_ref[...] if fox else None, k0, masked)
                dst = pt * (dpt - delta_scr[i][j:j + 1, :])
                if fox:
                    part = dst[:, :LANES]
                    for t in range(1, tq // LANES):
                        part = part + dst[:, t * LANES:(t + 1) * LANES]
                    dcf_scr[j] += part
                ds_b = (dst * sp.scale).astype(BF16)
                dv_scr[j] += jnp.dot(pt.astype(BF16), do_c, preferred_element_type=F32)
                dk_scr[j] += jnp.dot(ds_b, q_c if sp.hq == 2 else _head_q(sp, j, q_c), preferred_element_type=F32)
                if sp.hq == 2:
                    r = pl.ds(j * FOX_DH, FOX_DH)
                    dqt_ref[i, r, :] += jnp.dot(k_t[j * FOX_DH:(j + 1) * FOX_DH, :], ds_b, preferred_element_type=F32)
                else:
                    r = pl.ds(j * LANES, LANES)
                    dqt_ref[i, r, :] += jnp.dot(k_t[j * LANES:(j + 1) * LANES, :], ds_b, preferred_element_type=F32)
            return carry

        if causal:
            qblock(kb, 0, True)
            lax.fori_loop(kb + 1, nqb, functools.partial(qblock, masked=False), 0)
        else:
            lax.fori_loop(0, nqb, functools.partial(qblock, masked=False), 0)
        if hg == 2:
            dv_ref[...] = jnp.where(heads[0], dv_scr[0], dv_scr[1]).astype(dv_ref.dtype)
        else:
            dv_ref[...] = dv_scr[0].astype(dv_ref.dtype)
        if sp.hq == 2:
            dk_ref[...] = jnp.where(_head_lanes(0, FOX_DH), dk_scr[0], dk_scr[1]).astype(dk_ref.dtype)
        elif hg == 2:
            dk_ref[...] = jnp.concatenate([dk_scr[0], dk_scr[1]], axis=1).astype(dk_ref.dtype)
        else:
            dk_ref[...] = dk_scr[0].astype(dk_ref.dtype)
        if fox:
            lane = lax.broadcasted_iota(jnp.int32, (1, LANES), 1)
            sums = [jnp.sum(dcf_scr[j], axis=1, keepdims=True) for j in range(hg)]
            dcf_ref[...] = jnp.where(lane == 0, -sums[0], jnp.where(lane == 1, -sums[1], 0.0))

    seq_lanes = lambda b, g, kb: (b, g)
    key_blk = lambda b, g, kb: (b * nkb + kb, g)
    stats = pl.BlockSpec((None, None, nqb, hg, tq), lambda b, g, kb: (b, g, 0, 0, 0))
    in_specs = [
        pl.BlockSpec((sp.sq, qw), lambda b, g, kb: (b, qo + g)),
        pl.BlockSpec((tk, qw), lambda b, g, kb: (b * nkb + kb, ko + g)),
        pl.BlockSpec((tk, LANES), lambda b, g, kb: (b * nkb + kb, vo + g)),
    ]
    ops = [qa, ka, va]
    if not fox:
        in_specs.append(pl.BlockSpec((sp.sq, LANES), seq_lanes))
        ops.append(o)
    in_specs += [stats, pl.BlockSpec((sp.sq, LANES), seq_lanes)]
    ops += [lse, do]
    out_specs = [pl.BlockSpec((None, None, nqb, qw, tq), lambda b, g, kb: (b, g, 0, 0, 0)),
                 pl.BlockSpec((tk, qw), key_blk), pl.BlockSpec((tk, LANES), key_blk)]
    out_shape = [
        jax.ShapeDtypeStruct((sp.batch, sp.groups, nqb, qw, tq), F32),
        jax.ShapeDtypeStruct((sp.batch * sp.sk, sp.groups * qw), BF16),
        jax.ShapeDtypeStruct((sp.batch * sp.sk, sp.groups * LANES), BF16),
    ]
    scratch = [pltpu.VMEM((nqb, hg, tq), F32), pltpu.VMEM((hg, tk, LANES), F32), pltpu.VMEM((hg, tk, LANES), F32)]
    if fox:
        in_specs += [
            stats,
            pl.BlockSpec((tk, hg * LANES), key_blk),
            pl.BlockSpec((sp.sk, qw), lambda b, g, kb: (b, ko + g)),
            pl.BlockSpec((sp.sk, LANES), lambda b, g, kb: (b, vo + g)),
            pl.BlockSpec((sp.sk, hg * LANES), seq_lanes),
        ]
        ops += [cfq, cfk, ka, va, cfk]
        out_specs.append(pl.BlockSpec((tk, LANES), key_blk))
        out_shape.append(jax.ShapeDtypeStruct((sp.batch * sp.sk, sp.groups * LANES), F32))
        scratch.append(pltpu.VMEM((hg, tk, LANES), F32))
    blk = _nbytes((sp.sq, qw), BF16) + _nbytes((sp.sq, LANES), BF16) + 2 * _nbytes((sp.sq, LANES), F32)
    blk += _nbytes((sp.sq, qw), F32) + 4 * _nbytes((tk, qw), BF16) + 8 * _nbytes((tq, tk), F32)
    blk += (_nbytes((sp.sk, qw + LANES), BF16) + _nbytes((sp.sk, hg * LANES), F32)) if fox else 0
    return _pcall(
        body,
        name=name,
        grid=(sp.batch, sp.groups, nkb),
        in_specs=in_specs,
        out_specs=out_specs,
        out_shape=out_shape,
        scratch_shapes=scratch,
        compiler_params=_params(blk, _nbytes((sp.sq, LANES), F32) + 4 * _nbytes((tk, LANES), F32)),
    )(*ops)


def _slabwise(name, fn, ins, out_dtypes, rows_per_step=128):
    n = max(a.shape[0] for a in ins)
    rows, cols = ins[0].shape[1:]
    tr = min(rows_per_step, rows)
    while rows % tr:
        tr //= 2
    assert tr % 16 == 0 or tr == rows, (name, rows, tr)

    def spec(a):
        if a.shape[0] == 1:
            return pl.BlockSpec((None, tr, cols), lambda s, i: (0, i, 0))
        return pl.BlockSpec((None, tr, cols), lambda s, i: (s, i, 0))

    def body(*refs):
        vals = fn(*[r[...] for r in refs[:len(ins)]])
        for r, v in zip(refs[len(ins):], vals):
            r[...] = v.astype(r.dtype)

    blk = (len(ins) + len(out_dtypes)) * _nbytes((tr, cols + LANES), F32)
    res = _pcall(
        body,
        name=name,
        grid=(n, rows // tr),
        in_specs=[spec(a) for a in ins],
        out_specs=[pl.BlockSpec((None, tr, cols), lambda s, i: (s, i, 0)) for _ in out_dtypes],
        out_shape=[jax.ShapeDtypeStruct((n, rows, cols), dt) for dt in out_dtypes],
        compiler_params=_params(2 * blk),
    )(*ins)
    return res


def _adamw_math(w, g, m, v):
    m = ADAM_B1 * m + (1.0 - ADAM_B1) * g
    v = ADAM_B2 * v + (1.0 - ADAM_B2) * jnp.square(g)
    m_hat = m / (1.0 - ADAM_B1 ** ADAM_STEP)
    v_hat = v / (1.0 - ADAM_B2 ** ADAM_STEP)
    delta = -ADAM_LR * (m_hat / (jnp.sqrt(v_hat) + ADAM_EPS) + ADAM_WD * w)
    return delta, m, v


def _peer(rel):
    x, y, c = lax.axis_index("x"), lax.axis_index("y"), lax.axis_index("c")
    if rel == "c":
        return (x, y, 1 - c)
    return ((1 - x) if rel in (1, 3) else x, (1 - y) if rel in (2, 3) else y, c)


def _exchange(name, ins, out_shapes, plan, n_copies, aliases=None):
    n_in, n_out = len(ins), len(out_shapes)

    def body(*refs):
        in_refs, out_refs = refs[:n_in], refs[n_in:n_in + n_out]
        send_sems, recv_sems = refs[n_in + n_out:]
        copies = plan(in_refs, out_refs)
        assert len(copies) == n_copies, (name, len(copies))
        started = []
        for i, (src, dst, rel) in enumerate(copies):
            if rel is None:
                cp = pltpu.make_async_copy(src, dst, send_sems.at[i])
            else:
                cp = pltpu.make_async_remote_copy(src_ref=src, dst_ref=dst, send_sem=send_sems.at[i],
                                                  recv_sem=recv_sems.at[i], device_id=_peer(rel), device_id_type=MESH)
            cp.start()
            started.append(cp)
        for cp in started:
            cp.wait()

    any_spec = pl.BlockSpec(memory_space=pl.ANY)
    res = _pcall(
        body,
        name=name,
        in_specs=[any_spec] * n_in,
        out_specs=[any_spec] * n_out,
        out_shape=out_shapes,
        scratch_shapes=[pltpu.SemaphoreType.DMA((n_copies,)), pltpu.SemaphoreType.DMA((n_copies,))],
        input_output_aliases=aliases or {},
    )(*ins)
    return res


def _gather_weights(shards):
    n = len(shards)
    shapes = [jax.ShapeDtypeStruct((N_CHIPS,) + s.shape, s.dtype) for s in shards]

    def plan1(in_refs, out_refs):
        c = lax.axis_index("c")
        return [(s.at[c], g.at[rel, 0], rel) for s, g in zip(in_refs, out_refs) for rel in (1, 2, 3)]

    got = _exchange("gather_weights_ici", shards, shapes, plan1, 3 * n)

    def plan2(in_refs, out_refs):
        return [(g_in.at[pl.ds(1, 3), 0], g_out.at[pl.ds(1, 3), 1], "c") for g_in, g_out in zip(in_refs, out_refs)]

    return _exchange("gather_weights_d2d", got, shapes, plan2, n, aliases={i: i for i in range(n)})


def _to_absolute(rel_arr, own):
    x, y, c = lax.axis_index("x"), lax.axis_index("y"), lax.axis_index("c")
    tail = rel_arr.shape[2:]
    zeros = (0,) * len(tail)
    own_rel = jnp.concatenate([lax.dynamic_slice(own, (h ^ c,) + zeros, (1,) + tail) for h in (0, 1)], axis=0)
    rel_arr = jnp.concatenate([own_rel[None], rel_arr[1:]], axis=0)
    parts = []
    for sx in (0, 1):
        for sy in (0, 1):
            for h in (0, 1):
                rel = (sx ^ x) + 2 * (sy ^ y)
                parts.append(lax.dynamic_slice(rel_arr, (rel, h ^ c) + zeros, (1, 1) + tail))
    return jnp.concatenate(parts, axis=0).reshape((N_CHIPS, 2) + tail)


def _to_relative(abs_arr):
    x, y, c = lax.axis_index("x"), lax.axis_index("y"), lax.axis_index("c")
    tail = abs_arr.shape[2:]
    zeros = (0,) * len(tail)
    parts = []
    for hr in (0, 1):
        for rel in (0, 1, 2, 3):
            chip = 2 * (x ^ (rel & 1)) + (y ^ (rel >> 1))
            parts.append(lax.dynamic_slice(abs_arr, (chip, c ^ hr) + zeros, (1, 1) + tail))
    return jnp.concatenate(parts, axis=0).reshape((2, N_CHIPS) + tail)


def _reduce_grads(grads_rel):
    n = len(grads_rel)
    half_shapes = [jax.ShapeDtypeStruct(g.shape[1:], g.dtype) for g in grads_rel]

    def plan1(in_refs, out_refs):
        return [(g.at[1], r, "c") for g, r in zip(in_refs, out_refs)]

    from_sibling = _exchange("reduce_grads_pair", grads_rel, half_shapes, plan1, n)
    pair = [
        _slabwise(f"reduce_grads_pair_sum_{i}", lambda a, b: (a.astype(F32) + b.astype(F32),),
                  [g[0], r], [BF16])[0]
        for i, (g, r) in enumerate(zip(grads_rel, from_sibling))
    ]
    chip_shapes = [jax.ShapeDtypeStruct((3,) + p.shape[1:], p.dtype) for p in pair]

    def plan2(in_refs, out_refs):
        return [(p.at[rel], r.at[rel - 1], rel) for p, r in zip(in_refs, out_refs) for rel in (1, 2, 3)]

    from_chips = _exchange("reduce_grads_chips", pair, chip_shapes, plan2, 3 * n)
    total = [
        _slabwise(f"reduce_grads_chip_sum_{i}",
                  lambda a, b, c_, d: (a.astype(F32) + b.astype(F32) + c_.astype(F32) + d.astype(F32),),
                  [p[0:1], r[0:1], r[1:2], r[2:3]], [F32])[0]
        for i, (p, r) in enumerate(zip(pair, from_chips))
    ]
    def plan3(in_refs, out_refs):
        return [(t, r, "c") for t, r in zip(in_refs, out_refs)]

    theirs = _exchange("reduce_grads_share", total, [jax.ShapeDtypeStruct(t.shape, F32) for t in total], plan3, n)
    return list(zip(total, theirs))


def _adamw_shard(name, w, m, v, mine, theirs, rows_per_step=128):
    _, rows, cols = w.shape
    tr = min(rows_per_step, rows)
    while rows % tr:
        tr //= 2

    def body(w_ref, m_ref, v_ref, mine_ref, theirs_ref, g_ref, d_ref, nm_ref, nv_ref):
        g = jnp.where(pl.program_id(0) == lax.axis_index("c"), mine_ref[...], theirs_ref[...])
        d, nm, nv = _adamw_math(w_ref[...], g, m_ref[...], v_ref[...])
        g_ref[...], d_ref[...], nm_ref[...], nv_ref[...] = g, d, nm, nv

    half = pl.BlockSpec((None, tr, cols), lambda h, i: (h, i, 0))
    one = pl.BlockSpec((None, tr, cols), lambda h, i: (0, i, 0))
    return _pcall(
        body,
        name=name,
        grid=(2, rows // tr),
        in_specs=[half, half, half, one, one],
        out_specs=[half] * 4,
        out_shape=[jax.ShapeDtypeStruct(w.shape, F32)] * 4,
        compiler_params=_params(2 * 9 * _nbytes((tr, cols + LANES), F32)),
    )(w, m, v, mine, theirs)


def _allreduce_small(v):
    rows = v.shape[0]

    def body(v_ref, sum_ref, all_ref, send_sems, recv_sems, local_sem):
        x, y, c = lax.axis_index("x"), lax.axis_index("y"), lax.axis_index("c")
        sibling = (x, y, 1 - c)
        chips = [(1 - x, y), (x, 1 - y), (1 - x, 1 - y)]

        def slab(px, py, pc):
            return all_ref.at[pl.ds((4 * px + 2 * py + pc) * rows, rows), :]

        def copy(k, block, to, src=None):
            return pltpu.make_async_remote_copy(
                src_ref=slab(*block) if src is None else src, dst_ref=slab(*block), send_sem=send_sems.at[k],
                recv_sem=recv_sems.at[k], device_id=to, device_id_type=MESH)

        mine = pltpu.make_async_copy(v_ref, slab(x, y, c), local_sem)
        mine.start()
        first = [copy(0, (x, y, c), sibling, src=v_ref)]
        first += [copy(1 + j, (x, y, c), (*chip, c), src=v_ref) for j, chip in enumerate(chips)]
        for cp in first:
            cp.start()
        passed = [copy(4 + j, (*chip, c), sibling) for j, chip in enumerate(chips)]
        for j, chip in enumerate(chips):
            copy(1 + j, (*chip, c), (x, y, c)).wait_recv()
            passed[j].start()
        copy(0, (x, y, 1 - c), (x, y, c)).wait_recv()
        for j, chip in enumerate(chips):
            copy(4 + j, (*chip, 1 - c), (x, y, c)).wait_recv()
        for cp in first + passed:
            cp.wait_send()
        mine.wait()
        total = all_ref[pl.ds(0, rows), :]
        for d in range(1, N_DEV):
            total = total + all_ref[pl.ds(d * rows, rows), :]
        sum_ref[...] = total

    vm = pl.BlockSpec(memory_space=pltpu.VMEM)
    return _pcall(
        body,
        name="allreduce_small",
        in_specs=[vm],
        out_specs=vm,
        out_shape=jax.ShapeDtypeStruct((rows, LANES), F32),
        scratch_shapes=[pltpu.VMEM((N_DEV * rows, LANES), F32), pltpu.SemaphoreType.DMA((7,)),
                        pltpu.SemaphoreType.DMA((7,)), pltpu.SemaphoreType.DMA],
    )(v)


def _pad_cols(a, before, total):
    return jnp.pad(a, ((0, 0), (before, total - before - a.shape[1])))


def _layer_weights(cfg, w_in, w_uq, w_ukv):
    w = cfg.width
    qkv, f, cq, ckv, kr, qm, gates = jnp.split(w_in, list(_cumsum(cfg.in_splits))[:-1], axis=1)
    wa = jnp.concatenate([qkv, qm], axis=1)
    ws = jnp.concatenate([_pad_cols(f, 0, LANES), cq, ckv, _pad_cols(kr, MLA_NOPE, LANES)], axis=1)
    wq = jnp.pad(w_uq.reshape(cfg.q_rank, cfg.mla_h, MLA_NOPE + MLA_ROPE), ((0, 0), (0, 0), (0, LANES - MLA_NOPE - MLA_ROPE)))
    wq = wq.reshape(cfg.q_rank, cfg.mla_h * LANES)
    kv = w_ukv.reshape(cfg.kv_rank, cfg.mla_h, MLA_NOPE + MLA_V)
    wk = jnp.pad(kv[:, :, :MLA_NOPE], ((0, 0), (0, 0), (0, LANES - MLA_NOPE))).reshape(cfg.kv_rank, cfg.mla_h * LANES)
    wv = kv[:, :, MLA_NOPE:].reshape(cfg.kv_rank, cfg.mla_h * MLA_V)
    del w
    return wa, gates, ws, wq, wk, wv


def _cumsum(xs):
    out, t = [], 0
    for v in xs:
        t += v
        out.append(t)
    return out


def _layer_weight_grads(cfg, dwa, dwg, dws, dwq, dwk, dwv):
    w, qr, kvr = cfg.width, cfg.q_rank, cfg.kv_rank
    off_kr = LANES + qr + kvr + MLA_NOPE
    dw_in = jnp.concatenate([
        dwa[:, :3 * w], dws[:, :cfg.fox_h], dws[:, LANES:LANES + qr], dws[:, LANES + qr:LANES + qr + kvr],
        dws[:, off_kr:off_kr + MLA_ROPE], dwa[:, 3 * w:], dwg], axis=1)
    dw_uq = dwq.reshape(qr, cfg.mla_h, LANES)[:, :, :MLA_NOPE + MLA_ROPE].reshape(qr, cfg.mla_h * (MLA_NOPE + MLA_ROPE))
    dw_ukv = jnp.concatenate([dwk.reshape(kvr, cfg.mla_h, LANES)[:, :, :MLA_NOPE], dwv.reshape(kvr, cfg.mla_h, MLA_V)],
                             axis=2).reshape(kvr, cfg.mla_h * (MLA_NOPE + MLA_V))
    return dw_in, dw_uq, dw_ukv


def _attn_specs(cfg, batch):
    t = min(ATTN_TILE, cfg.seq)
    common = dict(batch=batch, sq=cfg.seq, chunk=cfg.chunk, tq=t)
    fox = Attn(sk=cfg.seq, groups=cfg.fox_h // 2, hq=2, hv=2, mode="fox", scale=FOX_DH ** -0.5, tk=t, **common)
    mla = Attn(sk=cfg.seq, groups=cfg.mla_h // 2, hq=1, hv=2, mode="chunk",
               scale=(MLA_NOPE + MLA_ROPE) ** -0.5, tk=t, **common)
    mem = Attn(sk=cfg.n_mem, groups=cfg.mem_h, hq=1, hv=1, mode="none", scale=MEM_DH ** -0.5, tk=cfg.n_mem, **common)
    return fox, mla, mem


def _untranspose_dq(dqt):
    b, g, nqb, qw, tq = dqt.shape
    return dqt.transpose(0, 2, 4, 1, 3).reshape(b * nqb * tq, g * qw)


def _small_core(cfg, ps, bf, gq, gkv):
    qr, kvr = cfg.q_rank, cfg.kv_rank
    z = ps[:, :LANES] + bf
    logf = jnp.minimum(z, 0.0) - jnp.log1p(jnp.exp(-jnp.abs(z)))
    nq = _rms(ps[:, LANES:LANES + qr], gq)
    nkv = _rms(ps[:, LANES + qr:LANES + qr + kvr], gkv)
    return logf, nq, nkv


def _cf_layouts(cfg, sp, cf, batch):
    h = cfg.fox_h
    cfq = cf[:, :h].reshape(batch, cfg.seq // sp.tq, sp.tq, h // 2, 2).transpose(0, 3, 1, 4, 2)
    cfk = jnp.repeat(cf[:, :h], LANES, axis=1)
    return cfq, cfk


def _layer_fwd(cfg, l, batch, h, hb, mem_b, rope_c, rope_s, lw, bf_pad, g_cq, g_ckv, ln1, ln2):
    wa, wg, ws, wq, wk, wv, wmkv, wbr, wout, wff1, wff2 = lw
    w, d = cfg.width, cfg.d
    fox, mla, mem = _attn_specs(cfg, batch)
    nw = w // LANES
    pa = _mm(f"proj_a_{l}", "nn", [(hb, wa)], [BF16])
    gl = _mm(f"proj_gates_{l}", "nn", [(hb, wg)], [F32])
    ps = _mm(f"proj_small_{l}", "nn", [(hb, ws)], [F32], tn=cfg.small_w)

    def small_fwd(ps_, c_, s_, bf_, gq_, gkv_):
        logf, nq, nkv = _small_core(cfg, ps_, bf_, gq_, gkv_)
        kpe = _rope(ps_[:, cfg.small_w - LANES:], c_, s_)
        return logf, nq, nkv, kpe

    logf, nq, nkv, kpe = _rowwise(
        f"small_fwd_{l}", small_fwd, [ps, rope_c, rope_s], [bf_pad, g_cq, g_ckv],
        [(LANES, F32), (cfg.q_rank, BF16), (cfg.kv_rank, BF16), (LANES, F32)])
    cf = _seq_cumsum(f"cum_forget_{l}", logf, batch, cfg.seq, reverse=False)
    cfq, cfk = _cf_layouts(cfg, fox, cf, batch)

    qf = _mm(f"mla_q_{l}", "nn", [(nq, wq)], [BF16], tn=wq.shape[1],
             epi=lambda acc, c_, s_: (_rope(acc, c_, s_),), row_extras=[rope_c, rope_s])
    kf = _mm(f"mla_k_{l}", "nn", [(nkv, wk)], [BF16], tn=wk.shape[1],
             epi=lambda acc, kp: (acc + jnp.tile(kp, (1, cfg.mla_h)),), row_extras=[kpe])
    vb = _mm(f"mla_v_{l}", "nn", [(nkv, wv)], [BF16])
    mkv = _mm(f"mem_kv_{l}", "nn", [(mem_b, wmkv)], [BF16])

    o_a, lse_a = _attn_fwd(f"fox_fwd_{l}", fox, (pa, 0), (pa, nw), (pa, 2 * nw), cfq, cfk)
    o_b, lse_b = _attn_fwd(f"mla_fwd_{l}", mla, (qf, 0), (kf, 0), (vb, 0))
    o_c, lse_c = _attn_fwd(f"mem_fwd_{l}", mem, (pa, 3 * nw), (mkv, 0), (mkv, nw))
    bps = [_mm(f"branch_{n}_{l}", "nn", [(o, wbr[n])], [F32]) for n, o in enumerate((o_a, o_b, o_c))]

    def merge(gl_, b0, b1, b2):
        g = jax.nn.sigmoid(gl_)
        return (g[:, :d] * b0 + g[:, d:2 * d] * b1 + g[:, 2 * d:] * b2,)

    (merged,) = _rowwise(f"merge_{l}", merge, [gl] + bps, [], [(d, BF16)])

    def post_ln(acc, res, g_, b_):
        z = cfg.alpha * res + acc
        y = _ln(z, g_, b_)
        return z, y, y

    z1, h1, h1b = _mm(f"out_ln1_{l}", "nn", [(merged, wout)], [F32, F32, BF16], tm=256, tn=d,
                      epi=post_ln, row_extras=[h], bc_extras=list(ln1))
    u, a = _mm(f"ff1_{l}", "nn", [(h1b, wff1)], [BF16, BF16],
               epi=lambda acc: (acc, jnp.square(jnp.maximum(acc, 0.0))))
    z2, h2, h2b = _mm(f"ff2_ln2_{l}", "nn", [(a, wff2)], [F32, F32, BF16], tm=256, tn=d,
                      epi=post_ln, row_extras=[h1], bc_extras=list(ln2))
    saved = dict(hb=hb, pa=pa, gl=gl, ps=ps, nq=nq, nkv=nkv, cfq=cfq, cfk=cfk, qf=qf, kf=kf, vb=vb, mkv=mkv,
                 o=(o_a, o_b, o_c), lse=(lse_a, lse_b, lse_c), bps=bps, merged=merged, z1=z1, h1b=h1b, u=u, a=a, z2=z2)
    return h2, h2b, saved


def _ln_bwd(name, cfg, ga, gb, z, g, b):
    d = cfg.d

    def fn(*vals):
        if gb is None:
            ga_, z_, g_, b_ = vals
            dy = ga_
        else:
            ga_, gb_, z_, g_, b_ = vals
            dy = ga_ + cfg.alpha * gb_
        _, vjp = jax.vjp(_ln, z_, g_, b_)
        dz, dg, db = vjp(dy)
        return dz, dz, dg, db

    rows = [ga, z] if gb is None else [ga, gb, z]
    return _rowwise(name, fn, rows, [g, b], [(d, F32), (d, BF16)], accs=[(1, d), (1, d)])


def _layer_bwd(cfg, l, batch, ga, gb, sv, mem_b, rope_c, rope_s, lw, bf_pad, g_cq, g_ckv, ln1, ln2):
    wa, wg, ws, wq, wk, wv, wmkv, wbr, wout, wff1, wff2 = lw
    w, d = cfg.width, cfg.d
    fox, mla, mem = _attn_specs(cfg, batch)
    nw = w // LANES
    gdt = BF16

    dz2, dz2b, dg2, db2 = _ln_bwd(f"ln2_bwd_{l}", cfg, ga, gb, sv["z2"], *ln2)
    du = _mm(f"ff2_bwd_x_{l}", "nt", [(dz2b, wff2)], [BF16],
             epi=lambda acc, u_: (acc * (2.0 * jnp.maximum(u_.astype(F32), 0.0)),), row_extras=[sv["u"]])
    dwff2 = _mm(f"ff2_bwd_w_{l}", "tn", [(sv["a"], dz2b)], [gdt])
    dwff1 = _mm(f"ff1_bwd_w_{l}", "tn", [(sv["h1b"], du)], [gdt])
    dh1 = _mm(f"ff1_bwd_x_{l}", "nt", [(du, wff1)], [F32])
    dz1, dz1b, dg1, db1 = _ln_bwd(f"ln1_bwd_{l}", cfg, dh1, dz2, sv["z1"], *ln1)
    dmerged = _mm(f"out_bwd_x_{l}", "nt", [(dz1b, wout)], [F32])
    dwout = _mm(f"out_bwd_w_{l}", "tn", [(sv["merged"], dz1b)], [gdt])

    def merge_bwd(dm, gl_, b0, b1, b2):
        def f(gl__, b0_, b1_, b2_):
            g = jax.nn.sigmoid(gl__)
            return g[:, :d] * b0_ + g[:, d:2 * d] * b1_ + g[:, 2 * d:] * b2_

        _, vjp = jax.vjp(f, gl_, b0, b1, b2)
        return vjp(dm)

    dgl, db0, db1_, db2_ = _rowwise(f"merge_bwd_{l}", merge_bwd, [dmerged, sv["gl"]] + sv["bps"], [],
                                    [(3 * d, BF16), (d, BF16), (d, BF16), (d, BF16)])
    dbps = (db0, db1_, db2_)
    dos = [_mm(f"branch_bwd_x_{n}_{l}", "nt", [(dbps[n], wbr[n])], [BF16]) for n in range(3)]
    dwbr = [_mm(f"branch_bwd_w_{n}_{l}", "tn", [(sv["o"][n], dbps[n])], [gdt]) for n in range(3)]

    pa = sv["pa"]
    dq_a, dk_a, dv_a, dcfk = _attn_bwd(f"fox_bwd_{l}", fox, (pa, 0), (pa, nw), (pa, 2 * nw), sv["o"][0], sv["lse"][0],
                                       dos[0], sv["cfq"], sv["cfk"])
    dqf, dkf, dvb = _attn_bwd(f"mla_bwd_{l}", mla, (sv["qf"], 0), (sv["kf"], 0), (sv["vb"], 0), sv["o"][1],
                              sv["lse"][1], dos[1])
    dqm, dmk, dmv = _attn_bwd(f"mem_bwd_{l}", mem, (pa, 3 * nw), (sv["mkv"], 0), (sv["mkv"], nw), sv["o"][2],
                              sv["lse"][2], dos[2])
    dq_a, dqf, dqm = (_untranspose_dq(a) for a in (dq_a, dqf, dqm))
    dwmkv = _mm(f"mem_kv_bwd_w_{l}", "tn", [(mem_b, jnp.concatenate([dmk, dmv], axis=1))], [gdt])

    (dq_raw,) = _rowwise(f"mla_q_rope_bwd_{l}", lambda dy, c_, s_: (_rope_t(dy, c_, s_),), [dqf, rope_c, rope_s], [],
                         [(wq.shape[1], BF16)])
    dwq = _mm(f"mla_q_bwd_w_{l}", "tn", [(sv["nq"], dq_raw)], [gdt])
    dnq = _mm(f"mla_q_bwd_x_{l}", "nt", [(dq_raw, wq)], [F32])
    dwk = _mm(f"mla_k_bwd_w_{l}", "tn", [(sv["nkv"], dkf)], [gdt])
    dwv = _mm(f"mla_v_bwd_w_{l}", "tn", [(sv["nkv"], dvb)], [gdt])
    dnkv = _mm(f"mla_kv_bwd_x_{l}", "nt", [(dkf, wk), (dvb, wv)], [F32])

    dcf = dcfk.reshape(batch * cfg.seq, cfg.fox_h // 2, LANES)[:, :, :2].reshape(batch * cfg.seq, cfg.fox_h)
    dcf = jnp.pad(dcf, ((0, 0), (0, LANES - cfg.fox_h)))
    dlogf = _seq_cumsum(f"cum_forget_bwd_{l}", dcf, batch, cfg.seq, reverse=True)

    def small_bwd(ps_, dlogf_, dnq_, dnkv_, dkf_, c_, s_, bf_, gq_, gkv_):
        _, vjp = jax.vjp(functools.partial(_small_core, cfg), ps_, bf_, gq_, gkv_)
        dps, dbf, dgq, dgkv = vjp((dlogf_, dnq_, dnkv_))
        dkpe = dkf_[:, :LANES].astype(F32)
        for hh in range(1, cfg.mla_h):
            dkpe = dkpe + dkf_[:, hh * LANES:(hh + 1) * LANES].astype(F32)
        lane = lax.broadcasted_iota(jnp.int32, (1, LANES), 1)
        dkpe = jnp.where((lane >= MLA_NOPE) & (lane < MLA_NOPE + MLA_ROPE), dkpe, 0.0)
        dkr = _rope_t(dkpe, c_, s_)
        dps = jnp.concatenate([dps[:, :cfg.small_w - LANES], dkr], axis=1)
        return dps, dbf, dgq, dgkv

    dps, dbf, dgq, dgkv = _rowwise(
        f"small_bwd_{l}", small_bwd, [sv["ps"], dlogf, dnq, dnkv, dkf, rope_c, rope_s], [bf_pad, g_cq, g_ckv],
        [(cfg.small_w, BF16)], accs=[(1, LANES), (1, cfg.q_rank), (1, cfg.kv_rank)])

    dpa = jnp.concatenate([dq_a.astype(BF16), dk_a, dv_a, dqm.astype(BF16)], axis=1)
    hb = sv["hb"]
    dh = _mm(f"proj_bwd_x_{l}", "nt", [(dpa, wa), (dgl, wg), (dps, ws)], [F32], tm=256)
    dwa = _mm(f"proj_a_bwd_w_{l}", "tn", [(hb, dpa)], [gdt])
    dwg = _mm(f"proj_gates_bwd_w_{l}", "tn", [(hb, dgl)], [gdt])
    dws = _mm(f"proj_small_bwd_w_{l}", "tn", [(hb, dps)], [gdt], tn=cfg.small_w)
    dw_in, dw_uq, dw_ukv = _layer_weight_grads(cfg, dwa, dwg, dws, dwq, dwk, dwv)
    big = dict(w_in=dw_in, w_uq=dw_uq, w_ukv=dw_ukv, w_mem_kv=dwmkv, w_br=jnp.stack(dwbr), w_out=dwout,
               w_ff1=dwff1, w_ff2=dwff2)
    small = dict(b_forget=dbf[0, :cfg.fox_h], g_cq=dgq[0], g_ckv=dgkv[0], ln1_g=dg1[0], ln1_b=db1[0],
                 ln2_g=dg2[0], ln2_b=db2[0])
    return dh, dz1, big, small


def _rope_tables(positions):
    inv_freq = ROPE_BASE ** (-jnp.arange(0, MLA_ROPE, 2, dtype=F32) / MLA_ROPE)
    ang = positions.astype(F32).reshape(-1)[:, None] * inv_freq
    cos, sin = jnp.cos(ang), jnp.sin(ang)
    t = ang.shape[0]
    rope_c = jnp.concatenate([jnp.ones((t, MLA_NOPE), F32), cos, cos, jnp.zeros((t, LANES - MLA_NOPE - MLA_ROPE), F32)], axis=1)
    rope_s = jnp.concatenate([jnp.zeros((t, MLA_NOPE), F32), -sin, sin, jnp.zeros((t, LANES - MLA_NOPE - MLA_ROPE), F32)], axis=1)
    return rope_c, rope_s


def _local_step(cfg, x, mem, positions, target, small_w, big_w):
    batch = x.shape[0]
    d, depth = cfg.d, cfg.depth
    t = batch * cfg.seq
    x2, tgt = x.reshape(t, d), target.reshape(t, d)
    mem_b = mem.reshape(batch * cfg.n_mem, d).astype(BF16)
    rope_c, rope_s = _rope_tables(positions)
    row = lambda v: v.reshape(1, -1)
    ln_in = (row(small_w["ln_in_g"]), row(small_w["ln_in_b"]))

    h, hb = _rowwise("ln_in", lambda x_, g_, b_: (_ln(x_, g_, b_),) * 2, [x2], list(ln_in), [(d, F32), (d, BF16)])
    layers, saves = [], []
    for l in range(depth):
        lw = _layer_weights(cfg, big_w["w_in"][l], big_w["w_uq"][l], big_w["w_ukv"][l]) + (
            big_w["w_mem_kv"][l], big_w["w_br"][l], big_w["w_out"][l], big_w["w_ff1"][l], big_w["w_ff2"][l])
        par = dict(
            lw=lw, bf_pad=jnp.pad(row(small_w["b_forget"][l]), ((0, 0), (0, LANES - cfg.fox_h))),
            g_cq=row(small_w["g_cq"][l]), g_ckv=row(small_w["g_ckv"][l]),
            ln1=(row(small_w["ln1_g"][l]), row(small_w["ln1_b"][l])),
            ln2=(row(small_w["ln2_g"][l]), row(small_w["ln2_b"][l])))
        layers.append(par)
        h, hb, sv = _layer_fwd(cfg, l, batch, h, hb, mem_b, rope_c, rope_s, **par)
        saves.append(sv)

    def loss_fn(y, tg):
        err = y - tg
        part = 0.5 * jnp.sum(jnp.mean(err * err, axis=-1, keepdims=True), axis=0, keepdims=True)
        return err * (1.0 / d), jnp.broadcast_to(part, (1, LANES))

    ga, loss_acc = _rowwise("loss", loss_fn, [h, tgt], [], [(d, F32)], accs=[(1, LANES)])
    gb = None
    big_g = {k: [None] * depth for k in ("w_in", "w_uq", "w_ukv", "w_mem_kv", "w_br", "w_out", "w_ff1", "w_ff2")}
    small_g = {k: [None] * depth for k in ("b_forget", "g_cq", "g_ckv", "ln1_g", "ln1_b", "ln2_g", "ln2_b")}
    for l in reversed(range(depth)):
        ga, gb, big, small = _layer_bwd(cfg, l, batch, ga, gb, saves[l], mem_b, rope_c, rope_s, **layers[l])
        for k, v in big.items():
            big_g[k][l] = v
        for k, v in small.items():
            small_g[k][l] = v
    dx, _, dg_in, db_in = _ln_bwd("ln_in_bwd", cfg, ga, gb, x2, *ln_in)
    big_g = {k: jnp.stack(v) for k, v in big_g.items()}
    small_g = {k: jnp.stack(v) for k, v in small_g.items()}
    small_g["ln_in_g"], small_g["ln_in_b"] = dg_in[0], db_in[0]
    return loss_acc[0, 0], dx.reshape(x.shape), big_g, small_g


BIG = ("w_in", "w_uq", "w_ukv", "w_mem_kv", "w_br", "w_out", "w_ff1", "w_ff2")
SMALL = ("ln_in_g", "ln_in_b", "b_forget", "g_cq", "g_ckv", "ln1_g", "ln1_b", "ln2_g", "ln2_b")
ROW_CUT = ("w_mem_kv", "w_out", "w_ff2")


def _shard_2d(a):
    cols = a.shape[-1]
    rows = a.size // cols
    return a.reshape(2, rows // 2, cols)


def _full_from_slots(name, slots, shard_shape):
    parts = slots.reshape((N_CHIPS,) + shard_shape)
    axis = len(shard_shape) - (2 if name in ROW_CUT else 1)
    return jnp.concatenate([parts[i] for i in range(N_CHIPS)], axis=axis)


def _slots_from_full(name, full, shard_shape):
    axis = len(shard_shape) - (2 if name in ROW_CUT else 1)
    parts = jnp.stack(jnp.split(full, N_CHIPS, axis=axis))
    cols = shard_shape[-1]
    return parts.reshape(N_CHIPS, 2, -1, cols)


def _pack_small(cfg, vals):
    flat = jnp.concatenate([vals[k].reshape(-1).astype(F32) for k in SMALL])
    pad = (-flat.shape[0]) % (LANES * LANES)
    return jnp.pad(flat, (0, pad)).reshape(-1, LANES)


def _unpack_small(packed, like):
    flat, out, off = packed.reshape(-1), {}, 0
    for k in SMALL:
        n = like[k].size
        out[k] = flat[off:off + n].reshape(like[k].shape)
        off += n
    return out


def _step(cfg, x, mem, positions, target, w, m, v):
    shards = [_shard_2d(w[k].astype(BF16)) for k in BIG]
    gathered = _gather_weights(shards)
    big_w = {k: _full_from_slots(k, _to_absolute(g, s), w[k].shape) for k, g, s in zip(BIG, gathered, shards)}
    small_w = {k: w[k] for k in SMALL}
    loss_local, dx, big_g, small_g = _local_step(cfg, x, mem, positions, target, small_w, big_w)
    loss = lax.psum(loss_local, ("x", "y", "c"))

    grads_rel = [_to_relative(_slots_from_full(k, big_g[k], w[k].shape)) for k in BIG]
    reduced = _reduce_grads(grads_rel)
    outs_big = {}
    for k, (mine, theirs) in zip(BIG, reduced):
        halves = (2,) + mine.shape[1:]
        res = _adamw_shard(f"adamw_{k}", *[a.reshape(halves) for a in (w[k], m[k], v[k])], mine, theirs)
        outs_big[k] = tuple(a.reshape(w[k].shape) for a in res)

    g_small = _allreduce_small(_pack_small(cfg, small_g))
    packs = [_pack_small(cfg, {k: d_[k] for k in SMALL}) for d_ in (w, m, v)]
    dl, nm, nv = _slabwise("adamw_small", _adamw_math, [a[None] for a in (packs[0], g_small, packs[1], packs[2])],
                           [F32, F32, F32])
    outs_small = [_unpack_small(a[0] if a.ndim == 3 else a, w) for a in (g_small, dl, nm, nv)]

    names = SMALL[:2] + ("w_in", "b_forget", "w_uq", "g_cq", "w_ukv", "g_ckv", "w_mem_kv", "w_br", "w_out",
                         "ln1_g", "ln1_b", "w_ff1", "w_ff2", "ln2_g", "ln2_b")
    result = [loss, dx]
    for part in range(4):
        for k in names:
            result.append(outs_big[k][part] if k in outs_big else outs_small[part][k])
    return tuple(result)


def kernel(x, mem, positions, ln_in_g, ln_in_b, w_in, b_forget, w_uq, g_cq, w_ukv, g_ckv, w_mem_kv, w_br, w_out, ln1_g, ln1_b, w_ff1, w_ff2, ln2_g, ln2_b, loss_target, m_ln_in_g, m_ln_in_b, m_w_in, m_b_forget, m_w_uq, m_g_cq, m_w_ukv, m_g_ckv, m_w_mem_kv, m_w_br, m_w_out, m_ln1_g, m_ln1_b, m_w_ff1, m_w_ff2, m_ln2_g, m_ln2_b, v_ln_in_g, v_ln_in_b, v_w_in, v_b_forget, v_w_uq, v_g_cq, v_w_ukv, v_g_ckv, v_w_mem_kv, v_w_br, v_w_out, v_ln1_g, v_ln1_b, v_w_ff1, v_w_ff2, v_ln2_g, v_ln2_b):
    w = dict(ln_in_g=ln_in_g, ln_in_b=ln_in_b, w_in=w_in, b_forget=b_forget, w_uq=w_uq, g_cq=g_cq, w_ukv=w_ukv,
             g_ckv=g_ckv, w_mem_kv=w_mem_kv, w_br=w_br, w_out=w_out, ln1_g=ln1_g, ln1_b=ln1_b, w_ff1=w_ff1,
             w_ff2=w_ff2, ln2_g=ln2_g, ln2_b=ln2_b)
    m = dict(ln_in_g=m_ln_in_g, ln_in_b=m_ln_in_b, w_in=m_w_in, b_forget=m_b_forget, w_uq=m_w_uq, g_cq=m_g_cq,
             w_ukv=m_w_ukv, g_ckv=m_g_ckv, w_mem_kv=m_w_mem_kv, w_br=m_w_br, w_out=m_w_out, ln1_g=m_ln1_g,
             ln1_b=m_ln1_b, w_ff1=m_w_ff1, w_ff2=m_w_ff2, ln2_g=m_ln2_g, ln2_b=m_ln2_b)
    v = dict(ln_in_g=v_ln_in_g, ln_in_b=v_ln_in_b, w_in=v_w_in, b_forget=v_b_forget, w_uq=v_w_uq, g_cq=v_g_cq,
             w_ukv=v_w_ukv, g_ckv=v_g_ckv, w_mem_kv=v_w_mem_kv, w_br=v_w_br, w_out=v_w_out, ln1_g=v_ln1_g,
             ln1_b=v_ln1_b, w_ff1=v_w_ff1, w_ff2=v_w_ff2, ln2_g=v_ln2_g, ln2_b=v_ln2_b)
    return _step(Cfg(), x, mem, positions, loss_target, w, m, v)
```

```python
import functools
from typing import NamedTuple

import jax
import jax.numpy as jnp
from jax import lax
from jax.experimental import pallas as pl
from jax.experimental.pallas import tpu as pltpu

F32 = jnp.float32
BF16 = jnp.bfloat16
MESH = pl.DeviceIdType.MESH

LANES = 128
SUBLANES = 8
VMEM_BYTES = 64 * 1024 * 1024
N_CHIPS = 4
N_DEV = 8

FOX_DH = 64
MLA_NOPE = 64
MLA_ROPE = 32
MLA_V = 64
MEM_DH = 128
ROPE_BASE = 10000.0
LN_EPS = 1e-5
RMS_EPS = 1e-6
NEG_INF = -1e30
ATTN_TILE = 512

ADAM_LR = 0.001
ADAM_B1 = 0.9
ADAM_B2 = 0.999
ADAM_EPS = 1e-08
ADAM_WD = 0.01
ADAM_STEP = 10


class Cfg(NamedTuple):
    d: int = 1024
    depth: int = 4
    seq: int = 2048
    chunk: int = 64
    n_mem: int = 256
    fox_h: int = 8
    mla_h: int = 8
    q_rank: int = 384
    kv_rank: int = 256
    mem_h: int = 4
    d_ff: int = 4096

    @property
    def width(self):
        return self.fox_h * FOX_DH

    @property
    def alpha(self):
        return (2 * self.depth) ** 0.25

    @property
    def small_w(self):
        return LANES + self.q_rank + self.kv_rank + LANES

    @property
    def in_splits(self):
        return (3 * self.width, self.fox_h, self.q_rank, self.kv_rank, MLA_ROPE, self.width, 3 * self.d)


def _pcall(body, **kw):
    call = pl.pallas_call(body, **kw)
    return lambda *ops: call(*[pltpu.with_memory_space_constraint(o, pltpu.HBM) for o in ops])


def _nbytes(shape, dtype):
    n = 1
    for s in shape:
        n *= s
    return n * jnp.dtype(dtype).itemsize


def _tile(dim, target):
    if dim <= target:
        return dim
    t = target - target % LANES
    while t >= LANES:
        if dim % t == 0:
            return t
        t -= LANES
    return dim


def _params(block_bytes, scratch_bytes=0):
    est = 2 * block_bytes + scratch_bytes + 24 * 1024 * 1024
    return pltpu.CompilerParams(vmem_limit_bytes=int(min(max(est, 32 * 1024 * 1024), VMEM_BYTES - 4 * 1024 * 1024)))


_DIMS = {"nn": (((1,), (0,)), ((), ())), "nt": (((1,), (1,)), ((), ())), "tn": (((0,), (0,)), ((), ()))}


def _mm(name, mode, pairs, out_dtypes, tm=512, tn=512, epi=None, row_extras=(), bc_extras=()):
    a0, b0 = pairs[0]
    m = a0.shape[1] if mode == "tn" else a0.shape[0]
    n = b0.shape[0] if mode == "nt" else b0.shape[1]
    tm, tn = _tile(m, tm), _tile(n, tn)
    in_specs, ops, blk = [], [], 0
    for a, b in pairs:
        if mode == "tn":
            k = a.shape[0]
            sa, sha = pl.BlockSpec((k, tm), lambda i, j: (0, i)), (k, tm)
        else:
            k = a.shape[1]
            sa, sha = pl.BlockSpec((tm, k), lambda i, j: (i, 0)), (tm, k)
        if mode == "nt":
            sb, shb = pl.BlockSpec((tn, k), lambda i, j: (j, 0)), (tn, k)
        else:
            sb, shb = pl.BlockSpec((k, tn), lambda i, j: (0, j)), (k, tn)
        in_specs += [sa, sb]
        ops += [a, b]
        blk += _nbytes(sha, a.dtype) + _nbytes(shb, b.dtype)
    for e in row_extras:
        w = e.shape[1]
        if w == n:
            in_specs.append(pl.BlockSpec((tm, tn), lambda i, j: (i, j)))
            blk += _nbytes((tm, tn), e.dtype)
        else:
            in_specs.append(pl.BlockSpec((tm, w), lambda i, j: (i, 0)))
            blk += _nbytes((tm, w), e.dtype)
        ops.append(e)
    for e in bc_extras:
        r, w = e.shape
        if w == n:
            in_specs.append(pl.BlockSpec((r, tn), lambda i, j: (0, j)))
        else:
            in_specs.append(pl.BlockSpec((r, w), lambda i, j: (0, 0)))
        blk += _nbytes((r, w), e.dtype)
        ops.append(e)
    npairs, nrow, nbc, nout = len(pairs), len(row_extras), len(bc_extras), len(out_dtypes)
    dims = _DIMS[mode]

    def body(*refs):
        acc = None
        for p in range(npairs):
            a = refs[2 * p][...].astype(BF16)
            b = refs[2 * p + 1][...].astype(BF16)
            d = lax.dot_general(a, b, dims, preferred_element_type=F32)
            acc = d if acc is None else acc + d
        ex = [r[...] for r in refs[2 * npairs:2 * npairs + nrow + nbc]]
        outs = (acc,) if epi is None else epi(acc, *ex)
        for o_ref, o in zip(refs[2 * npairs + nrow + nbc:], outs):
            o_ref[...] = o.astype(o_ref.dtype)

    blk += sum(_nbytes((tm, tn), dt) for dt in out_dtypes) + 2 * _nbytes((tm, tn), F32)
    res = _pcall(
        body,
        name=name,
        grid=(m // tm, n // tn),
        in_specs=in_specs,
        out_specs=[pl.BlockSpec((tm, tn), lambda i, j: (i, j)) for _ in range(nout)],
        out_shape=[jax.ShapeDtypeStruct((m, n), dt) for dt in out_dtypes],
        compiler_params=_params(blk),
    )(*ops)
    return res[0] if nout == 1 else res


def _rowwise(name, fn, row_ins, bc_ins, outs, accs=(), tm=256):
    rows = row_ins[0].shape[0]
    tm = min(tm, rows)
    assert rows % tm == 0
    nrow, nbc, nout, nacc = len(row_ins), len(bc_ins), len(outs), len(accs)
    in_specs = [pl.BlockSpec((tm, a.shape[1]), lambda i: (i, 0)) for a in row_ins]
    in_specs += [pl.BlockSpec(a.shape, lambda i: (0, 0)) for a in bc_ins]
    out_specs = [pl.BlockSpec((tm, w), lambda i: (i, 0)) for w, _ in outs]
    out_specs += [pl.BlockSpec(s, lambda i: (0, 0)) for s in accs]
    out_shape = [jax.ShapeDtypeStruct((rows, w), dt) for w, dt in outs]
    out_shape += [jax.ShapeDtypeStruct(s, F32) for s in accs]

    def body(*refs):
        vals = fn(*[r[...] for r in refs[:nrow + nbc]])
        o_refs = refs[nrow + nbc:]
        for r, v in zip(o_refs[:nout], vals[:nout]):
            r[...] = v.astype(r.dtype)
        if nacc:
            @pl.when(pl.program_id(0) == 0)
            def _():
                for r in o_refs[nout:]:
                    r[...] = jnp.zeros(r.shape, F32)

            for r, v in zip(o_refs[nout:], vals[nout:]):
                r[...] += v

    blk = sum(_nbytes((tm, a.shape[1]), a.dtype) for a in row_ins) + sum(_nbytes(a.shape, a.dtype) for a in bc_ins)
    blk += sum(_nbytes((tm, w), dt) for w, dt in outs) + sum(_nbytes(s, F32) for s in accs)
    res = _pcall(
        body,
        name=name,
        grid=(rows // tm,),
        in_specs=in_specs,
        out_specs=out_specs,
        out_shape=out_shape,
        compiler_params=_params(2 * blk),
    )(*row_ins, *bc_ins)
    return res


def _ln(z, g, b):
    mu = jnp.mean(z, axis=-1, keepdims=True)
    zc = z - mu
    var = jnp.mean(zc * zc, axis=-1, keepdims=True)
    return zc * lax.rsqrt(var + LN_EPS) * g + b


def _rms(x, g):
    return x * lax.rsqrt(jnp.mean(x * x, axis=-1, keepdims=True) + RMS_EPS) * g


def _colsum(v):
    return jnp.sum(v, axis=0, keepdims=True)


def _rope_swap(x):
    w = x.shape[1]
    lane = lax.broadcasted_iota(jnp.int32, (1, w), 1) % LANES
    from_left = pltpu.roll(x, 16, 1)
    from_right = pltpu.roll(x, w - 16, 1)
    lo = (lane >= MLA_NOPE) & (lane < MLA_NOPE + 16)
    hi = (lane >= MLA_NOPE + 16) & (lane < MLA_NOPE + 32)
    return jnp.where(hi, from_left, jnp.where(lo, from_right, 0.0))


def _rope(x, cos_t, sin_t):
    nh = x.shape[1] // LANES
    ct, st = jnp.tile(cos_t, (1, nh)), jnp.tile(sin_t, (1, nh))
    return x * ct + _rope_swap(x) * st


def _rope_t(dy, cos_t, sin_t):
    nh = dy.shape[1] // LANES
    ct, st = jnp.tile(cos_t, (1, nh)), jnp.tile(sin_t, (1, nh))
    return dy * ct + _rope_swap(dy * st)


def _block_cumsum(v, carry, reverse):
    tb = v.shape[0]
    r = lax.broadcasted_iota(jnp.int32, (tb, tb), 0)
    c = lax.broadcasted_iota(jnp.int32, (tb, tb), 1)
    tri = jnp.where((c >= r) if reverse else (c <= r), 1.0, 0.0).astype(BF16)
    hi = v.astype(BF16)
    r1 = v - hi.astype(F32)
    mid = r1.astype(BF16)
    lo = (r1 - mid.astype(F32)).astype(BF16)
    out = carry + sum(jnp.dot(tri, p, preferred_element_type=F32) for p in (hi, mid, lo))
    return out, (out[0:1, :] if reverse else out[tb - 1:tb, :])


def _forget_cumsum(name, logf, batch, seq, heads, tb):
    nb = seq // tb

    def body(x_ref, keys_ref, rows_ref, carry):
        @pl.when(pl.program_id(1) == 0)
        def _():
            carry[...] = jnp.zeros(carry.shape, F32)

        out, carry[...] = _block_cumsum(x_ref[...], carry[...], False)
        keys_ref[...] = jnp.concatenate([jnp.broadcast_to(out[:, h:h + 1], (tb, LANES)) for h in range(heads)], axis=1)
        out_t = out.T
        for g in range(heads // 2):
            rows_ref[g] = out_t[2 * g:2 * g + 2, :]

    return _pcall(
        body,
        name=name,
        grid=(batch, nb),
        in_specs=[pl.BlockSpec((tb, LANES), lambda b, i: (b * nb + i, 0))],
        out_specs=[pl.BlockSpec((tb, heads * LANES), lambda b, i: (b * nb + i, 0)),
                   pl.BlockSpec((None, None, heads // 2, 2, tb), lambda b, i: (b, i, 0, 0, 0))],
        out_shape=[jax.ShapeDtypeStruct((batch * seq, heads * LANES), F32),
                   jax.ShapeDtypeStruct((batch, nb, heads // 2, 2, tb), F32)],
        scratch_shapes=[pltpu.VMEM((1, LANES), F32)],
        compiler_params=_params(4 * tb * (heads + 2) * LANES * 4),
    )(logf)


def _forget_cumsum_bwd(name, dcf, batch, seq, heads, tb):
    nb = seq // tb

    def body(x_ref, o_ref, carry):
        @pl.when(pl.program_id(1) == 0)
        def _():
            carry[...] = jnp.zeros(carry.shape, F32)

        lane = lax.broadcasted_iota(jnp.int32, (1, LANES), 1)
        v = jnp.zeros((tb, LANES), F32)
        for g in range(heads // 2):
            blk = x_ref[:, g * LANES:(g + 1) * LANES]
            moved = pltpu.roll(blk, 2 * g, 1) if g else blk
            v = v + jnp.where((lane >= 2 * g) & (lane < 2 * g + 2), moved, 0.0)
        o_ref[...], carry[...] = _block_cumsum(v, carry[...], True)

    return _pcall(
        body,
        name=name,
        grid=(batch, nb),
        in_specs=[pl.BlockSpec((tb, (heads // 2) * LANES), lambda b, i: (b * nb + nb - 1 - i, 0))],
        out_specs=pl.BlockSpec((tb, LANES), lambda b, i: (b * nb + nb - 1 - i, 0)),
        out_shape=jax.ShapeDtypeStruct((batch * seq, LANES), F32),
        scratch_shapes=[pltpu.VMEM((1, LANES), F32)],
        compiler_params=_params(4 * tb * (heads // 2 + 1) * LANES * 4),
    )(dcf)


class Attn(NamedTuple):
    batch: int
    sq: int
    sk: int
    groups: int
    hq: int
    hv: int
    mode: str
    scale: float
    chunk: int
    tq: int
    tk: int

    @property
    def hg(self):
        return self.hv

    @property
    def qw(self):
        return LANES * self.hg // self.hq

    @property
    def dv(self):
        return LANES // self.hv


def _head_lanes(j, dv):
    lane = lax.broadcasted_iota(jnp.int32, (1, LANES), 1)
    return (lane >= j * dv) & (lane < (j + 1) * dv)


def _head_q(sp, j, q_blk):
    if sp.hq == 2:
        return jnp.where(_head_lanes(j, FOX_DH), q_blk, jnp.zeros_like(q_blk))
    return q_blk[:, LANES * j:LANES * (j + 1)]


def _head_rows(sp, j):
    return slice(j * sp.dv, (j + 1) * sp.dv) if sp.hg == 2 else slice(None)


def _scores_t(sp, j, k_c, q_j, cfq_rows, cfk_rep, k0, q0, masked):
    tk, tq = k_c.shape[0], q_j.shape[0]
    k_j = k_c if sp.hq == 2 else k_c[:, LANES * j:LANES * (j + 1)]
    st = lax.dot_general(k_j, q_j, _DIMS["nt"], preferred_element_type=F32) * sp.scale
    if sp.mode == "fox":
        st = st + (cfq_rows[j:j + 1, :] - jnp.tile(cfk_rep[:, LANES * j:LANES * (j + 1)], (1, tq // LANES)))
    if masked:
        kidx = k0 + lax.broadcasted_iota(jnp.int32, (tk, tq), 0)
        qidx = q0 + lax.broadcasted_iota(jnp.int32, (tk, tq), 1)
        if sp.mode == "chunk":
            shift = sp.chunk.bit_length() - 1
            kidx, qidx = jnp.right_shift(kidx, shift), jnp.right_shift(qidx, shift)
        st = jnp.where(kidx <= qidx, st, NEG_INF)
    return st


def _attn_fwd(name, sp, q, k, v, cfq=None, cfk=None):
    (qa, qo), (ka, ko), (va, vo) = q, k, v
    tq, tk, hg, qw = sp.tq, sp.tk, sp.hg, sp.qw
    nqb, nkc = sp.sq // tq, sp.sk // tk
    fox, causal = sp.mode == "fox", sp.mode != "none"
    assert sp.sq % tq == 0 and sp.sk % tk == 0 and (not causal or (tq == tk and sp.sq == sp.sk))

    def body(*refs):
        if fox:
            q_ref, k_ref, v_ref, cfq_ref, cfk_ref, o_ref, lse_ref, acc_scr = refs
        else:
            q_ref, k_ref, v_ref, o_ref, lse_ref, acc_scr = refs
        i = pl.program_id(2)
        q0 = i * tq
        q_blk = q_ref[...]
        qs = [_head_q(sp, j, q_blk) for j in range(hg)]
        acc_scr[...] = jnp.zeros(acc_scr.shape, F32)

        def chunk(kc, carry, masked):
            ms, ls = carry
            k0 = pl.multiple_of(kc * tk, tk)
            k_c = k_ref[pl.ds(k0, tk), :]
            v_c = v_ref[pl.ds(k0, tk), :]
            new_m, new_l = [], []
            for j in range(hg):
                st = _scores_t(sp, j, k_c, qs[j], cfq_ref[...] if fox else None,
                               cfk_ref[pl.ds(k0, tk), :] if fox else None, k0, q0, masked)
                m_new = jnp.maximum(ms[j], jnp.max(st, axis=0, keepdims=True))
                alpha = jnp.exp(ms[j] - m_new)
                pt = jnp.exp(st - m_new)
                new_m.append(m_new)
                new_l.append(alpha * ls[j] + jnp.sum(pt, axis=0, keepdims=True))
                pv = lax.dot_general(v_c, pt.astype(BF16), _DIMS["tn"], preferred_element_type=F32)
                r = _head_rows(sp, j)
                acc_scr[r, :] = acc_scr[r, :] * alpha + pv[r, :]
            return tuple(new_m), tuple(new_l)

        carry = (tuple(jnp.full((1, tq), NEG_INF, F32) for _ in range(hg)),
                 tuple(jnp.zeros((1, tq), F32) for _ in range(hg)))
        if causal:
            carry = lax.fori_loop(0, i, functools.partial(chunk, masked=False), carry)
            ms, ls = chunk(i, carry, True)
        else:
            ms, ls = lax.fori_loop(0, nkc, functools.partial(chunk, masked=False), carry)
        for j in range(hg):
            r = _head_rows(sp, j)
            acc_scr[r, :] = acc_scr[r, :] / ls[j]
            lse_ref[j:j + 1, :] = ms[j] + jnp.log(ls[j])
        o_ref[...] = acc_scr[...].T

    in_specs = [
        pl.BlockSpec((tq, qw), lambda b, g, i: (b * nqb + i, qo + g)),
        pl.BlockSpec((sp.sk, qw), lambda b, g, i: (b, ko + g)),
        pl.BlockSpec((sp.sk, LANES), lambda b, g, i: (b, vo + g)),
    ]
    ops = [qa, ka, va]
    stat_blk = pl.BlockSpec((None, None, None, hg, tq), lambda b, g, i: (b, g, i, 0, 0))
    if fox:
        in_specs += [pl.BlockSpec((None, None, None, hg, tq), lambda b, g, i: (b, i, g, 0, 0)),
                     pl.BlockSpec((sp.sk, hg * LANES), lambda b, g, i: (b, g))]
        ops += [cfq, cfk]
    blk = _nbytes((tq, qw), BF16) + _nbytes((sp.sk, qw + LANES), BF16) + 2 * _nbytes((tq, LANES), F32)
    blk += _nbytes((sp.sk, hg * LANES), F32) + 6 * _nbytes((tk, tq), F32)
    return _pcall(
        body,
        name=name,
        grid=(sp.batch, sp.groups, nqb),
        in_specs=in_specs,
        out_specs=[pl.BlockSpec((tq, LANES), lambda b, g, i: (b * nqb + i, g)), stat_blk],
        out_shape=[
            jax.ShapeDtypeStruct((sp.batch * sp.sq, sp.groups * LANES), F32),
            jax.ShapeDtypeStruct((sp.batch, sp.groups, nqb, hg, tq), F32),
        ],
        scratch_shapes=[pltpu.VMEM((LANES, tq), F32)],
        compiler_params=_params(blk, tq * LANES * 4),
    )(*ops)


def _attn_bwd(name, sp, q, k, v, o, lse, do, cfq=None, cfk=None):
    (qa, qo), (ka, ko), (va, vo) = q, k, v
    tq, tk, hg, qw, dv = sp.tq, sp.tk, sp.hg, sp.qw, sp.dv
    nqb, nkb = sp.sq // tq, sp.sk // tk
    fox, causal = sp.mode == "fox", sp.mode != "none"
    assert sp.sq % tq == 0 and sp.sk % tk == 0 and (not causal or (tq == tk and sp.sq == sp.sk))

    def body(*refs):
        if fox:
            (q_ref, k_ref, v_ref, lse_ref, do_ref, cfq_ref, cfk_ref, kall_ref, vall_ref, cfkall_ref,
             dq_ref, dk_ref, dv_ref, dcf_ref, delta_scr, dk_scr, dv_scr, dqt_scr, dcf_scr) = refs
        else:
            (q_ref, k_ref, v_ref, o_ref, lse_ref, do_ref,
             dq_ref, dk_ref, dv_ref, delta_scr, dk_scr, dv_scr, dqt_scr) = refs
        kb = pl.program_id(2)
        k0 = kb * tk
        heads = [_head_lanes(j, dv) for j in range(hg)]

        def head_do(j, do_c):
            return jnp.where(heads[j], do_c, jnp.zeros_like(do_c)) if hg == 2 else do_c

        def probs_t(j, k_c, v_c, q_c, do_c, i, cf_keys, c0, masked):
            st = _scores_t(sp, j, k_c, _head_q(sp, j, q_c), cfq_ref[i] if fox else None, cf_keys, c0, i * tq, masked)
            pt = jnp.exp(st - lse_ref[i][j:j + 1, :])
            dpt = lax.dot_general(v_c, head_do(j, do_c), _DIMS["nt"], preferred_element_type=F32)
            return pt, dpt

        @pl.when(kb == 0)
        def _():
            dqt_scr[...] = jnp.zeros(dqt_scr.shape, F32)

            def fill(i, carry):
                r0 = pl.multiple_of(i * tq, tq)
                do_c = do_ref[pl.ds(r0, tq), :]
                if fox:
                    q_c = q_ref[pl.ds(r0, tq), :]

                    def keys(kc, acc, masked):
                        c0 = pl.multiple_of(kc * tk, tk)
                        out = []
                        for j in range(hg):
                            pt, dpt = probs_t(j, kall_ref[pl.ds(c0, tk), :], vall_ref[pl.ds(c0, tk), :], q_c, do_c, i,
                                              cfkall_ref[pl.ds(c0, tk), :], c0, masked)
                            out.append(acc[j] + jnp.sum(pt * dpt, axis=0, keepdims=True))
                        return tuple(out)

                    d = lax.fori_loop(0, i, functools.partial(keys, masked=False),
                                      tuple(jnp.zeros((1, tq), F32) for _ in range(hg)))
                    d = keys(i, d, True)
                    for j in range(hg):
                        delta_scr[i, j:j + 1, :] = d[j]
                else:
                    prod_t = (do_c.astype(F32) * o_ref[pl.ds(r0, tq), :]).T
                    for j in range(hg):
                        delta_scr[i, j:j + 1, :] = jnp.sum(prod_t[_head_rows(sp, j), :], axis=0, keepdims=True)
                return carry

            lax.fori_loop(0, nqb, fill, 0)

        k_blk = k_ref[...]
        v_blk = v_ref[...]
        k_t = k_blk.astype(F32).T.astype(BF16)
        dk_scr[...] = jnp.zeros(dk_scr.shape, F32)
        dv_scr[...] = jnp.zeros(dv_scr.shape, F32)
        if fox:
            dcf_scr[...] = jnp.zeros(dcf_scr.shape, F32)

        def qblock(i, carry, masked):
            r0 = pl.multiple_of(i * tq, tq)
            q_c = q_ref[pl.ds(r0, tq), :]
            do_c = do_ref[pl.ds(r0, tq), :]
            for j in range(hg):
                pt, dpt = probs_t(j, k_blk, v_blk, q_c, do_c, i, cfk_ref[...] if fox else None, k0, masked)
                dst = pt * (dpt - delta_scr[i][j:j + 1, :])
                if fox:
                    part = dst[:, :LANES]
                    for t in range(1, tq // LANES):
                        part = part + dst[:, t * LANES:(t + 1) * LANES]
                    dcf_scr[j] += part
                ds_b = (dst * sp.scale).astype(BF16)
                dv_scr[j] += jnp.dot(pt.astype(BF16), do_c, preferred_element_type=F32)
                dk_scr[j] += jnp.dot(ds_b, q_c if sp.hq == 2 else _head_q(sp, j, q_c), preferred_element_type=F32)
                if sp.hq == 2:
                    r = pl.ds(j * FOX_DH, FOX_DH)
                    dqt_scr[i, r, :] += jnp.dot(k_t[j * FOX_DH:(j + 1) * FOX_DH, :], ds_b, preferred_element_type=F32)
                else:
                    r = pl.ds(j * LANES, LANES)
                    dqt_scr[i, r, :] += jnp.dot(k_t[j * LANES:(j + 1) * LANES, :], ds_b, preferred_element_type=F32)
            return carry

        if causal:
            qblock(kb, 0, True)
            lax.fori_loop(kb + 1, nqb, functools.partial(qblock, masked=False), 0)
        else:
            lax.fori_loop(0, nqb, functools.partial(qblock, masked=False), 0)

        @pl.when(kb == nkb - 1)
        def _():
            def untranspose(i, carry):
                dq_ref[pl.ds(pl.multiple_of(i * tq, tq), tq), :] = dqt_scr[i].T
                return carry

            lax.fori_loop(0, nqb, untranspose, 0)

        if hg == 2:
            dv_ref[...] = jnp.where(heads[0], dv_scr[0], dv_scr[1]).astype(dv_ref.dtype)
        else:
            dv_ref[...] = dv_scr[0].astype(dv_ref.dtype)
        if sp.hq == 2:
            dk_ref[...] = jnp.where(_head_lanes(0, FOX_DH), dk_scr[0], dk_scr[1]).astype(dk_ref.dtype)
        elif hg == 2:
            dk_ref[...] = jnp.concatenate([dk_scr[0], dk_scr[1]], axis=1).astype(dk_ref.dtype)
        else:
            dk_ref[...] = dk_scr[0].astype(dk_ref.dtype)
        if fox:
            lane = lax.broadcasted_iota(jnp.int32, (1, LANES), 1)
            sums = [jnp.sum(dcf_scr[j], axis=1, keepdims=True) for j in range(hg)]
            dcf_ref[...] = jnp.where(lane == 0, -sums[0], jnp.where(lane == 1, -sums[1], 0.0))

    seq_lanes = lambda b, g, kb: (b, g)
    key_blk = lambda b, g, kb: (b * nkb + kb, g)
    stats = pl.BlockSpec((None, None, nqb, hg, tq), lambda b, g, kb: (b, g, 0, 0, 0))
    in_specs = [
        pl.BlockSpec((sp.sq, qw), lambda b, g, kb: (b, qo + g)),
        pl.BlockSpec((tk, qw), lambda b, g, kb: (b * nkb + kb, ko + g)),
        pl.BlockSpec((tk, LANES), lambda b, g, kb: (b * nkb + kb, vo + g)),
    ]
    ops = [qa, ka, va]
    if not fox:
        in_specs.append(pl.BlockSpec((sp.sq, LANES), seq_lanes))
        ops.append(o)
    in_specs += [stats, pl.BlockSpec((sp.sq, LANES), seq_lanes)]
    ops += [lse, do]
    out_specs = [pl.BlockSpec((sp.sq, qw), seq_lanes), pl.BlockSpec((tk, qw), key_blk), pl.BlockSpec((tk, LANES), key_blk)]
    out_shape = [
        jax.ShapeDtypeStruct((sp.batch * sp.sq, sp.groups * qw), F32),
        jax.ShapeDtypeStruct((sp.batch * sp.sk, sp.groups * qw), BF16),
        jax.ShapeDtypeStruct((sp.batch * sp.sk, sp.groups * LANES), BF16),
    ]
    scratch = [pltpu.VMEM((nqb, hg, tq), F32), pltpu.VMEM((hg, tk, LANES), F32), pltpu.VMEM((hg, tk, LANES), F32),
               pltpu.VMEM((nqb, qw, tq), F32)]
    if fox:
        in_specs += [
            pl.BlockSpec((None, nqb, None, hg, tq), lambda b, g, kb: (b, 0, g, 0, 0)),
            pl.BlockSpec((tk, hg * LANES), key_blk),
            pl.BlockSpec((sp.sk, qw), lambda b, g, kb: (b, ko + g)),
            pl.BlockSpec((sp.sk, LANES), lambda b, g, kb: (b, vo + g)),
            pl.BlockSpec((sp.sk, hg * LANES), seq_lanes),
        ]
        ops += [cfq, cfk, ka, va, cfk]
        out_specs.append(pl.BlockSpec((tk, LANES), key_blk))
        out_shape.append(jax.ShapeDtypeStruct((sp.batch * sp.sk, sp.groups * LANES), F32))
        scratch.append(pltpu.VMEM((hg, tk, LANES), F32))
    blk = _nbytes((sp.sq, qw), BF16) + _nbytes((sp.sq, LANES), BF16) + 2 * _nbytes((sp.sq, LANES), F32)
    blk += _nbytes((sp.sq, qw), F32) + 4 * _nbytes((tk, qw), BF16) + 8 * _nbytes((tq, tk), F32)
    blk += (_nbytes((sp.sk, qw + LANES), BF16) + _nbytes((sp.sk, hg * LANES), F32)) if fox else 0
    return _pcall(
        body,
        name=name,
        grid=(sp.batch, sp.groups, nkb),
        in_specs=in_specs,
        out_specs=out_specs,
        out_shape=out_shape,
        scratch_shapes=scratch,
        compiler_params=_params(blk, _nbytes((sp.sq, LANES), F32) + 4 * _nbytes((tk, LANES), F32)),
    )(*ops)


def _slabwise(name, fn, ins, out_dtypes, rows_per_step=128):
    n = max(a.shape[0] for a in ins)
    rows, cols = ins[0].shape[1:]
    tr = min(rows_per_step, rows)
    while rows % tr:
        tr //= 2
    assert tr % 16 == 0 or tr == rows, (name, rows, tr)

    def spec(a):
        if a.shape[0] == 1:
            return pl.BlockSpec((None, tr, cols), lambda s, i: (0, i, 0))
        return pl.BlockSpec((None, tr, cols), lambda s, i: (s, i, 0))

    def body(*refs):
        vals = fn(*[r[...] for r in refs[:len(ins)]])
        for r, v in zip(refs[len(ins):], vals):
            r[...] = v.astype(r.dtype)

    blk = (len(ins) + len(out_dtypes)) * _nbytes((tr, cols + LANES), F32)
    res = _pcall(
        body,
        name=name,
        grid=(n, rows // tr),
        in_specs=[spec(a) for a in ins],
        out_specs=[pl.BlockSpec((None, tr, cols), lambda s, i: (s, i, 0)) for _ in out_dtypes],
        out_shape=[jax.ShapeDtypeStruct((n, rows, cols), dt) for dt in out_dtypes],
        compiler_params=_params(2 * blk),
    )(*ins)
    return res


def _adamw_math(w, g, m, v):
    m = ADAM_B1 * m + (1.0 - ADAM_B1) * g
    v = ADAM_B2 * v + (1.0 - ADAM_B2) * jnp.square(g)
    m_hat = m / (1.0 - ADAM_B1 ** ADAM_STEP)
    v_hat = v / (1.0 - ADAM_B2 ** ADAM_STEP)
    delta = -ADAM_LR * (m_hat / (jnp.sqrt(v_hat) + ADAM_EPS) + ADAM_WD * w)
    return delta, m, v


def _peer(rel):
    x, y, c = lax.axis_index("x"), lax.axis_index("y"), lax.axis_index("c")
    if rel == "c":
        return (x, y, 1 - c)
    return ((1 - x) if rel in (1, 3) else x, (1 - y) if rel in (2, 3) else y, c)


def _exchange(name, ins, out_shapes, plan, n_copies, aliases=None):
    n_in, n_out = len(ins), len(out_shapes)

    def body(*refs):
        in_refs, out_refs = refs[:n_in], refs[n_in:n_in + n_out]
        send_sems, recv_sems = refs[n_in + n_out:]
        copies = plan(in_refs, out_refs)
        assert len(copies) == n_copies, (name, len(copies))
        started = []
        for i, (src, dst, rel) in enumerate(copies):
            if rel is None:
                cp = pltpu.make_async_copy(src, dst, send_sems.at[i])
            else:
                cp = pltpu.make_async_remote_copy(src_ref=src, dst_ref=dst, send_sem=send_sems.at[i],
                                                  recv_sem=recv_sems.at[i], device_id=_peer(rel), device_id_type=MESH)
            cp.start()
            started.append(cp)
        for cp in started:
            cp.wait()

    any_spec = pl.BlockSpec(memory_space=pl.ANY)
    res = _pcall(
        body,
        name=name,
        in_specs=[any_spec] * n_in,
        out_specs=[any_spec] * n_out,
        out_shape=out_shapes,
        scratch_shapes=[pltpu.SemaphoreType.DMA((n_copies,)), pltpu.SemaphoreType.DMA((n_copies,))],
        input_output_aliases=aliases or {},
    )(*ins)
    return res


def _gather_weights(shards):
    n = len(shards)
    shapes = [jax.ShapeDtypeStruct((N_CHIPS,) + s.shape, s.dtype) for s in shards]

    def plan1(in_refs, out_refs):
        c = lax.axis_index("c")
        return [(s.at[c], g.at[rel, 0], rel) for s, g in zip(in_refs, out_refs) for rel in (1, 2, 3)]

    got = _exchange("gather_weights_ici", shards, shapes, plan1, 3 * n)

    def plan2(in_refs, out_refs):
        return [(g_in.at[pl.ds(1, 3), 0], g_out.at[pl.ds(1, 3), 1], "c") for g_in, g_out in zip(in_refs, out_refs)]

    return _exchange("gather_weights_d2d", got, shapes, plan2, n, aliases={i: i for i in range(n)})


def _to_absolute(rel_arr, own):
    x, y, c = lax.axis_index("x"), lax.axis_index("y"), lax.axis_index("c")
    tail = rel_arr.shape[2:]
    zeros = (0,) * len(tail)
    own_rel = jnp.concatenate([lax.dynamic_slice(own, (h ^ c,) + zeros, (1,) + tail) for h in (0, 1)], axis=0)
    rel_arr = jnp.concatenate([own_rel[None], rel_arr[1:]], axis=0)
    parts = []
    for sx in (0, 1):
        for sy in (0, 1):
            for h in (0, 1):
                rel = (sx ^ x) + 2 * (sy ^ y)
                parts.append(lax.dynamic_slice(rel_arr, (rel, h ^ c) + zeros, (1, 1) + tail))
    return jnp.concatenate(parts, axis=0).reshape((N_CHIPS, 2) + tail)


def _to_relative(abs_arr):
    x, y, c = lax.axis_index("x"), lax.axis_index("y"), lax.axis_index("c")
    tail = abs_arr.shape[2:]
    zeros = (0,) * len(tail)
    parts = []
    for hr in (0, 1):
        for rel in (0, 1, 2, 3):
            chip = 2 * (x ^ (rel & 1)) + (y ^ (rel >> 1))
            parts.append(lax.dynamic_slice(abs_arr, (chip, c ^ hr) + zeros, (1, 1) + tail))
    return jnp.concatenate(parts, axis=0).reshape((2, N_CHIPS) + tail)


def _reduce_grads(grads_rel):
    n = len(grads_rel)
    half_shapes = [jax.ShapeDtypeStruct(g.shape[1:], g.dtype) for g in grads_rel]

    def plan1(in_refs, out_refs):
        return [(g.at[1], r, "c") for g, r in zip(in_refs, out_refs)]

    from_sibling = _exchange("reduce_grads_pair", grads_rel, half_shapes, plan1, n)
    pair = [
        _slabwise(f"reduce_grads_pair_sum_{i}", lambda a, b: (a.astype(F32) + b.astype(F32),),
                  [g[0], r], [BF16])[0]
        for i, (g, r) in enumerate(zip(grads_rel, from_sibling))
    ]
    chip_shapes = [jax.ShapeDtypeStruct((3,) + p.shape[1:], p.dtype) for p in pair]

    def plan2(in_refs, out_refs):
        return [(p.at[rel], r.at[rel - 1], rel) for p, r in zip(in_refs, out_refs) for rel in (1, 2, 3)]

    from_chips = _exchange("reduce_grads_chips", pair, chip_shapes, plan2, 3 * n)
    total = [
        _slabwise(f"reduce_grads_chip_sum_{i}",
                  lambda a, b, c_, d: (a.astype(F32) + b.astype(F32) + c_.astype(F32) + d.astype(F32),),
                  [p[0:1], r[0:1], r[1:2], r[2:3]], [F32])[0]
        for i, (p, r) in enumerate(zip(pair, from_chips))
    ]
    def plan3(in_refs, out_refs):
        return [(t, r, "c") for t, r in zip(in_refs, out_refs)]

    theirs = _exchange("reduce_grads_share", total, [jax.ShapeDtypeStruct(t.shape, F32) for t in total], plan3, n)
    return list(zip(total, theirs))


def _adamw_shard(name, w, m, v, mine, theirs, rows_per_step=128):
    _, rows, cols = w.shape
    tr = min(rows_per_step, rows)
    while rows % tr:
        tr //= 2

    def body(w_ref, m_ref, v_ref, mine_ref, theirs_ref, g_ref, d_ref, nm_ref, nv_ref):
        g = jnp.where(pl.program_id(0) == lax.axis_index("c"), mine_ref[...], theirs_ref[...])
        d, nm, nv = _adamw_math(w_ref[...], g, m_ref[...], v_ref[...])
        g_ref[...], d_ref[...], nm_ref[...], nv_ref[...] = g, d, nm, nv

    half = pl.BlockSpec((None, tr, cols), lambda h, i: (h, i, 0))
    one = pl.BlockSpec((None, tr, cols), lambda h, i: (0, i, 0))
    return _pcall(
        body,
        name=name,
        grid=(2, rows // tr),
        in_specs=[half, half, half, one, one],
        out_specs=[half] * 4,
        out_shape=[jax.ShapeDtypeStruct(w.shape, F32)] * 4,
        compiler_params=_params(2 * 9 * _nbytes((tr, cols + LANES), F32)),
    )(w, m, v, mine, theirs)


def _allreduce_small(v):
    rows = v.shape[0]

    def body(v_ref, sum_ref, all_ref, send_sems, recv_sems, local_sem):
        x, y, c = lax.axis_index("x"), lax.axis_index("y"), lax.axis_index("c")
        sibling = (x, y, 1 - c)
        chips = [(1 - x, y), (x, 1 - y), (1 - x, 1 - y)]

        def slab(px, py, pc):
            return all_ref.at[pl.ds((4 * px + 2 * py + pc) * rows, rows), :]

        def copy(k, block, to, src=None):
            return pltpu.make_async_remote_copy(
                src_ref=slab(*block) if src is None else src, dst_ref=slab(*block), send_sem=send_sems.at[k],
                recv_sem=recv_sems.at[k], device_id=to, device_id_type=MESH)

        mine = pltpu.make_async_copy(v_ref, slab(x, y, c), local_sem)
        mine.start()
        first = [copy(0, (x, y, c), sibling, src=v_ref)]
        first += [copy(1 + j, (x, y, c), (*chip, c), src=v_ref) for j, chip in enumerate(chips)]
        for cp in first:
            cp.start()
        passed = [copy(4 + j, (*chip, c), sibling) for j, chip in enumerate(chips)]
        for j, chip in enumerate(chips):
            copy(1 + j, (*chip, c), (x, y, c)).wait_recv()
            passed[j].start()
        copy(0, (x, y, 1 - c), (x, y, c)).wait_recv()
        for j, chip in enumerate(chips):
            copy(4 + j, (*chip, 1 - c), (x, y, c)).wait_recv()
        for cp in first + passed:
            cp.wait_send()
        mine.wait()
        total = all_ref[pl.ds(0, rows), :]
        for d in range(1, N_DEV):
            total = total + all_ref[pl.ds(d * rows, rows), :]
        sum_ref[...] = total

    vm = pl.BlockSpec(memory_space=pltpu.VMEM)
    return _pcall(
        body,
        name="allreduce_small",
        in_specs=[vm],
        out_specs=vm,
        out_shape=jax.ShapeDtypeStruct((rows, LANES), F32),
        scratch_shapes=[pltpu.VMEM((N_DEV * rows, LANES), F32), pltpu.SemaphoreType.DMA((7,)),
                        pltpu.SemaphoreType.DMA((7,)), pltpu.SemaphoreType.DMA],
    )(v)


def _pad_cols(a, before, total):
    return jnp.pad(a, ((0, 0), (before, total - before - a.shape[1])))


def _layer_weights(cfg, w_in, w_uq, w_ukv):
    w = cfg.width
    qkv, f, cq, ckv, kr, qm, gates = jnp.split(w_in, list(_cumsum(cfg.in_splits))[:-1], axis=1)
    wa = jnp.concatenate([qkv, qm], axis=1)
    ws = jnp.concatenate([_pad_cols(f, 0, LANES), cq, ckv, _pad_cols(kr, MLA_NOPE, LANES)], axis=1)
    wq = jnp.pad(w_uq.reshape(cfg.q_rank, cfg.mla_h, MLA_NOPE + MLA_ROPE), ((0, 0), (0, 0), (0, LANES - MLA_NOPE - MLA_ROPE)))
    wq = wq.reshape(cfg.q_rank, cfg.mla_h * LANES)
    kv = w_ukv.reshape(cfg.kv_rank, cfg.mla_h, MLA_NOPE + MLA_V)
    wk = jnp.pad(kv[:, :, :MLA_NOPE], ((0, 0), (0, 0), (0, LANES - MLA_NOPE))).reshape(cfg.kv_rank, cfg.mla_h * LANES)
    wv = kv[:, :, MLA_NOPE:].reshape(cfg.kv_rank, cfg.mla_h * MLA_V)
    del w
    return wa, gates, ws, wq, wk, wv


def _cumsum(xs):
    out, t = [], 0
    for v in xs:
        t += v
        out.append(t)
    return out


def _layer_weight_grads(cfg, dwa, dwg, dws, dwq, dwk, dwv):
    w, qr, kvr = cfg.width, cfg.q_rank, cfg.kv_rank
    off_kr = LANES + qr + kvr + MLA_NOPE
    dw_in = jnp.concatenate([
        dwa[:, :3 * w], dws[:, :cfg.fox_h], dws[:, LANES:LANES + qr], dws[:, LANES + qr:LANES + qr + kvr],
        dws[:, off_kr:off_kr + MLA_ROPE], dwa[:, 3 * w:], dwg], axis=1)
    dw_uq = dwq.reshape(qr, cfg.mla_h, LANES)[:, :, :MLA_NOPE + MLA_ROPE].reshape(qr, cfg.mla_h * (MLA_NOPE + MLA_ROPE))
    dw_ukv = jnp.concatenate([dwk.reshape(kvr, cfg.mla_h, LANES)[:, :, :MLA_NOPE], dwv.reshape(kvr, cfg.mla_h, MLA_V)],
                             axis=2).reshape(kvr, cfg.mla_h * (MLA_NOPE + MLA_V))
    return dw_in, dw_uq, dw_ukv


def _attn_specs(cfg, batch):
    t = min(ATTN_TILE, cfg.seq)
    common = dict(batch=batch, sq=cfg.seq, chunk=cfg.chunk, tq=t)
    fox = Attn(sk=cfg.seq, groups=cfg.fox_h // 2, hq=2, hv=2, mode="fox", scale=FOX_DH ** -0.5, tk=t, **common)
    mla = Attn(sk=cfg.seq, groups=cfg.mla_h // 2, hq=1, hv=2, mode="chunk",
               scale=(MLA_NOPE + MLA_ROPE) ** -0.5, tk=t, **common)
    mem = Attn(sk=cfg.n_mem, groups=cfg.mem_h, hq=1, hv=1, mode="none", scale=MEM_DH ** -0.5, tk=cfg.n_mem, **common)
    return fox, mla, mem


def _small_core(cfg, ps, bf, gq, gkv):
    qr, kvr = cfg.q_rank, cfg.kv_rank
    z = ps[:, :LANES] + bf
    logf = jnp.minimum(z, 0.0) - jnp.log1p(jnp.exp(-jnp.abs(z)))
    nq = _rms(ps[:, LANES:LANES + qr], gq)
    nkv = _rms(ps[:, LANES + qr:LANES + qr + kvr], gkv)
    return logf, nq, nkv


def _layer_fwd(cfg, l, batch, h, hb, mem_b, rope_c, rope_s, lw, bf_pad, g_cq, g_ckv, ln1, ln2):
    wa, wg, ws, wq, wk, wv, wmkv, wbr, wout, wff1, wff2 = lw
    w, d = cfg.width, cfg.d
    fox, mla, mem = _attn_specs(cfg, batch)
    nw = w // LANES
    pa = _mm(f"proj_a_{l}", "nn", [(hb, wa)], [BF16])
    gl = _mm(f"proj_gates_{l}", "nn", [(hb, wg)], [F32])
    ps = _mm(f"proj_small_{l}", "nn", [(hb, ws)], [F32], tn=cfg.small_w)

    def small_fwd(ps_, c_, s_, bf_, gq_, gkv_):
        logf, nq, nkv = _small_core(cfg, ps_, bf_, gq_, gkv_)
        kpe = _rope(ps_[:, cfg.small_w - LANES:], c_, s_)
        return logf, nq, nkv, kpe

    logf, nq, nkv, kpe = _rowwise(
        f"small_fwd_{l}", small_fwd, [ps, rope_c, rope_s], [bf_pad, g_cq, g_ckv],
        [(LANES, F32), (cfg.q_rank, BF16), (cfg.kv_rank, BF16), (LANES, F32)])
    cfk, cfq = _forget_cumsum(f"cum_forget_{l}", logf, batch, cfg.seq, cfg.fox_h, fox.tq)

    qf = _mm(f"mla_q_{l}", "nn", [(nq, wq)], [BF16], tn=wq.shape[1],
             epi=lambda acc, c_, s_: (_rope(acc, c_, s_),), row_extras=[rope_c, rope_s])
    kf = _mm(f"mla_k_{l}", "nn", [(nkv, wk)], [BF16], tn=wk.shape[1],
             epi=lambda acc, kp: (acc + jnp.tile(kp, (1, cfg.mla_h)),), row_extras=[kpe])
    vb = _mm(f"mla_v_{l}", "nn", [(nkv, wv)], [BF16])
    mkv = _mm(f"mem_kv_{l}", "nn", [(mem_b, wmkv)], [BF16])

    o_a, lse_a = _attn_fwd(f"fox_fwd_{l}", fox, (pa, 0), (pa, nw), (pa, 2 * nw), cfq, cfk)
    o_b, lse_b = _attn_fwd(f"mla_fwd_{l}", mla, (qf, 0), (kf, 0), (vb, 0))
    o_c, lse_c = _attn_fwd(f"mem_fwd_{l}", mem, (pa, 3 * nw), (mkv, 0), (mkv, nw))
    bps = [_mm(f"branch_{n}_{l}", "nn", [(o, wbr[n])], [F32]) for n, o in enumerate((o_a, o_b, o_c))]

    def merge(gl_, b0, b1, b2):
        g = jax.nn.sigmoid(gl_)
        return (g[:, :d] * b0 + g[:, d:2 * d] * b1 + g[:, 2 * d:] * b2,)

    (merged,) = _rowwise(f"merge_{l}", merge, [gl] + bps, [], [(d, BF16)])

    def post_ln(acc, res, g_, b_):
        z = cfg.alpha * res + acc
        y = _ln(z, g_, b_)
        return z, y, y

    z1, h1, h1b = _mm(f"out_ln1_{l}", "nn", [(merged, wout)], [F32, F32, BF16], tm=256, tn=d,
                      epi=post_ln, row_extras=[h], bc_extras=list(ln1))
    u, a = _mm(f"ff1_{l}", "nn", [(h1b, wff1)], [BF16, BF16],
               epi=lambda acc: (acc, jnp.square(jnp.maximum(acc, 0.0))))
    z2, h2, h2b = _mm(f"ff2_ln2_{l}", "nn", [(a, wff2)], [F32, F32, BF16], tm=256, tn=d,
                      epi=post_ln, row_extras=[h1], bc_extras=list(ln2))
    saved = dict(hb=hb, pa=pa, gl=gl, ps=ps, nq=nq, nkv=nkv, cfq=cfq, cfk=cfk, qf=qf, kf=kf, vb=vb, mkv=mkv,
                 o=(o_a, o_b, o_c), lse=(lse_a, lse_b, lse_c), bps=bps, merged=merged, z1=z1, h1b=h1b, u=u, a=a, z2=z2)
    return h2, h2b, saved


def _ln_bwd(name, cfg, ga, gb, z, g, b):
    d = cfg.d

    def fn(*vals):
        if gb is None:
            ga_, z_, g_, b_ = vals
            dy = ga_
        else:
            ga_, gb_, z_, g_, b_ = vals
            dy = ga_ + cfg.alpha * gb_
        _, vjp = jax.vjp(_ln, z_, g_, b_)
        dz, dg, db = vjp(dy)
        return dz, dz, dg, db

    rows = [ga, z] if gb is None else [ga, gb, z]
    return _rowwise(name, fn, rows, [g, b], [(d, F32), (d, BF16)], accs=[(1, d), (1, d)])


def _layer_bwd(cfg, l, batch, ga, gb, sv, mem_b, rope_c, rope_s, lw, bf_pad, g_cq, g_ckv, ln1, ln2):
    wa, wg, ws, wq, wk, wv, wmkv, wbr, wout, wff1, wff2 = lw
    w, d = cfg.width, cfg.d
    fox, mla, mem = _attn_specs(cfg, batch)
    nw = w // LANES
    gdt = BF16

    dz2, dz2b, dg2, db2 = _ln_bwd(f"ln2_bwd_{l}", cfg, ga, gb, sv["z2"], *ln2)
    du = _mm(f"ff2_bwd_x_{l}", "nt", [(dz2b, wff2)], [BF16],
             epi=lambda acc, u_: (acc * (2.0 * jnp.maximum(u_.astype(F32), 0.0)),), row_extras=[sv["u"]])
    dwff2 = _mm(f"ff2_bwd_w_{l}", "tn", [(sv["a"], dz2b)], [gdt])
    dwff1 = _mm(f"ff1_bwd_w_{l}", "tn", [(sv["h1b"], du)], [gdt])
    dh1 = _mm(f"ff1_bwd_x_{l}", "nt", [(du, wff1)], [F32])
    dz1, dz1b, dg1, db1 = _ln_bwd(f"ln1_bwd_{l}", cfg, dh1, dz2, sv["z1"], *ln1)
    dmerged = _mm(f"out_bwd_x_{l}", "nt", [(dz1b, wout)], [F32])
    dwout = _mm(f"out_bwd_w_{l}", "tn", [(sv["merged"], dz1b)], [gdt])

    def merge_bwd(dm, gl_, b0, b1, b2):
        def f(gl__, b0_, b1_, b2_):
            g = jax.nn.sigmoid(gl__)
            return g[:, :d] * b0_ + g[:, d:2 * d] * b1_ + g[:, 2 * d:] * b2_

        _, vjp = jax.vjp(f, gl_, b0, b1, b2)
        return vjp(dm)

    dgl, db0, db1_, db2_ = _rowwise(f"merge_bwd_{l}", merge_bwd, [dmerged, sv["gl"]] + sv["bps"], [],
                                    [(3 * d, BF16), (d, BF16), (d, BF16), (d, BF16)])
    dbps = (db0, db1_, db2_)
    dos = [_mm(f"branch_bwd_x_{n}_{l}", "nt", [(dbps[n], wbr[n])], [BF16]) for n in range(3)]
    dwbr = [_mm(f"branch_bwd_w_{n}_{l}", "tn", [(sv["o"][n], dbps[n])], [gdt]) for n in range(3)]

    pa = sv["pa"]
    dq_a, dk_a, dv_a, dcfk = _attn_bwd(f"fox_bwd_{l}", fox, (pa, 0), (pa, nw), (pa, 2 * nw), sv["o"][0], sv["lse"][0],
                                       dos[0], sv["cfq"], sv["cfk"])
    dqf, dkf, dvb = _attn_bwd(f"mla_bwd_{l}", mla, (sv["qf"], 0), (sv["kf"], 0), (sv["vb"], 0), sv["o"][1],
                              sv["lse"][1], dos[1])
    dqm, dmk, dmv = _attn_bwd(f"mem_bwd_{l}", mem, (pa, 3 * nw), (sv["mkv"], 0), (sv["mkv"], nw), sv["o"][2],
                              sv["lse"][2], dos[2])
    dwmkv = _mm(f"mem_kv_bwd_w_{l}", "tn", [(mem_b, jnp.concatenate([dmk, dmv], axis=1))], [gdt])

    (dq_raw,) = _rowwise(f"mla_q_rope_bwd_{l}", lambda dy, c_, s_: (_rope_t(dy, c_, s_),), [dqf, rope_c, rope_s], [],
                         [(wq.shape[1], BF16)])
    dwq = _mm(f"mla_q_bwd_w_{l}", "tn", [(sv["nq"], dq_raw)], [gdt])
    dnq = _mm(f"mla_q_bwd_x_{l}", "nt", [(dq_raw, wq)], [F32])
    dwk = _mm(f"mla_k_bwd_w_{l}", "tn", [(sv["nkv"], dkf)], [gdt])
    dwv = _mm(f"mla_v_bwd_w_{l}", "tn", [(sv["nkv"], dvb)], [gdt])
    dnkv = _mm(f"mla_kv_bwd_x_{l}", "nt", [(dkf, wk), (dvb, wv)], [F32])

    dlogf = _forget_cumsum_bwd(f"cum_forget_bwd_{l}", dcfk, batch, cfg.seq, cfg.fox_h, fox.tq)

    def small_bwd(ps_, dlogf_, dnq_, dnkv_, dkf_, c_, s_, bf_, gq_, gkv_):
        _, vjp = jax.vjp(functools.partial(_small_core, cfg), ps_, bf_, gq_, gkv_)
        dps, dbf, dgq, dgkv = vjp((dlogf_, dnq_, dnkv_))
        dkpe = dkf_[:, :LANES].astype(F32)
        for hh in range(1, cfg.mla_h):
            dkpe = dkpe + dkf_[:, hh * LANES:(hh + 1) * LANES].astype(F32)
        lane = lax.broadcasted_iota(jnp.int32, (1, LANES), 1)
        dkpe = jnp.where((lane >= MLA_NOPE) & (lane < MLA_NOPE + MLA_ROPE), dkpe, 0.0)
        dkr = _rope_t(dkpe, c_, s_)
        dps = jnp.concatenate([dps[:, :cfg.small_w - LANES], dkr], axis=1)
        return dps, dbf, dgq, dgkv

    dps, dbf, dgq, dgkv = _rowwise(
        f"small_bwd_{l}", small_bwd, [sv["ps"], dlogf, dnq, dnkv, dkf, rope_c, rope_s], [bf_pad, g_cq, g_ckv],
        [(cfg.small_w, BF16)], accs=[(1, LANES), (1, cfg.q_rank), (1, cfg.kv_rank)])

    dpa = jnp.concatenate([dq_a.astype(BF16), dk_a, dv_a, dqm.astype(BF16)], axis=1)
    hb = sv["hb"]
    dh = _mm(f"proj_bwd_x_{l}", "nt", [(dpa, wa), (dgl, wg), (dps, ws)], [F32], tm=256)
    dwa = _mm(f"proj_a_bwd_w_{l}", "tn", [(hb, dpa)], [gdt])
    dwg = _mm(f"proj_gates_bwd_w_{l}", "tn", [(hb, dgl)], [gdt])
    dws = _mm(f"proj_small_bwd_w_{l}", "tn", [(hb, dps)], [gdt], tn=cfg.small_w)
    dw_in, dw_uq, dw_ukv = _layer_weight_grads(cfg, dwa, dwg, dws, dwq, dwk, dwv)
    big = dict(w_in=dw_in, w_uq=dw_uq, w_ukv=dw_ukv, w_mem_kv=dwmkv, w_br=jnp.stack(dwbr), w_out=dwout,
               w_ff1=dwff1, w_ff2=dwff2)
    small = dict(b_forget=dbf[0, :cfg.fox_h], g_cq=dgq[0], g_ckv=dgkv[0], ln1_g=dg1[0], ln1_b=db1[0],
                 ln2_g=dg2[0], ln2_b=db2[0])
    return dh, dz1, big, small


def _rope_tables(positions):
    inv_freq = ROPE_BASE ** (-jnp.arange(0, MLA_ROPE, 2, dtype=F32) / MLA_ROPE)
    ang = positions.astype(F32).reshape(-1)[:, None] * inv_freq
    cos, sin = jnp.cos(ang), jnp.sin(ang)
    t = ang.shape[0]
    rope_c = jnp.concatenate([jnp.ones((t, MLA_NOPE), F32), cos, cos, jnp.zeros((t, LANES - MLA_NOPE - MLA_ROPE), F32)], axis=1)
    rope_s = jnp.concatenate([jnp.zeros((t, MLA_NOPE), F32), -sin, sin, jnp.zeros((t, LANES - MLA_NOPE - MLA_ROPE), F32)], axis=1)
    return rope_c, rope_s


def _local_step(cfg, x, mem, positions, target, small_w, big_w):
    batch = x.shape[0]
    d, depth = cfg.d, cfg.depth
    t = batch * cfg.seq
    x2, tgt = x.reshape(t, d), target.reshape(t, d)
    mem_b = mem.reshape(batch * cfg.n_mem, d).astype(BF16)
    rope_c, rope_s = _rope_tables(positions)
    row = lambda v: v.reshape(1, -1)
    ln_in = (row(small_w["ln_in_g"]), row(small_w["ln_in_b"]))

    h, hb = _rowwise("ln_in", lambda x_, g_, b_: (_ln(x_, g_, b_),) * 2, [x2], list(ln_in), [(d, F32), (d, BF16)])
    layers, saves = [], []
    for l in range(depth):
        lw = _layer_weights(cfg, big_w["w_in"][l], big_w["w_uq"][l], big_w["w_ukv"][l]) + (
            big_w["w_mem_kv"][l], big_w["w_br"][l], big_w["w_out"][l], big_w["w_ff1"][l], big_w["w_ff2"][l])
        par = dict(
            lw=lw, bf_pad=jnp.pad(row(small_w["b_forget"][l]), ((0, 0), (0, LANES - cfg.fox_h))),
            g_cq=row(small_w["g_cq"][l]), g_ckv=row(small_w["g_ckv"][l]),
            ln1=(row(small_w["ln1_g"][l]), row(small_w["ln1_b"][l])),
            ln2=(row(small_w["ln2_g"][l]), row(small_w["ln2_b"][l])))
        layers.append(par)
        h, hb, sv = _layer_fwd(cfg, l, batch, h, hb, mem_b, rope_c, rope_s, **par)
        saves.append(sv)

    def loss_fn(y, tg):
        err = y - tg
        part = 0.5 * jnp.sum(jnp.mean(err * err, axis=-1, keepdims=True), axis=0, keepdims=True)
        return err * (1.0 / d), jnp.broadcast_to(part, (1, LANES))

    ga, loss_acc = _rowwise("loss", loss_fn, [h, tgt], [], [(d, F32)], accs=[(1, LANES)])
    gb = None
    big_g = {k: [None] * depth for k in ("w_in", "w_uq", "w_ukv", "w_mem_kv", "w_br", "w_out", "w_ff1", "w_ff2")}
    small_g = {k: [None] * depth for k in ("b_forget", "g_cq", "g_ckv", "ln1_g", "ln1_b", "ln2_g", "ln2_b")}
    for l in reversed(range(depth)):
        ga, gb, big, small = _layer_bwd(cfg, l, batch, ga, gb, saves[l], mem_b, rope_c, rope_s, **layers[l])
        for k, v in big.items():
            big_g[k][l] = v
        for k, v in small.items():
            small_g[k][l] = v
    dx, _, dg_in, db_in = _ln_bwd("ln_in_bwd", cfg, ga, gb, x2, *ln_in)
    big_g = {k: jnp.stack(v) for k, v in big_g.items()}
    small_g = {k: jnp.stack(v) for k, v in small_g.items()}
    small_g["ln_in_g"], small_g["ln_in_b"] = dg_in[0], db_in[0]
    return loss_acc[0, 0], dx.reshape(x.shape), big_g, small_g


BIG = ("w_in", "w_uq", "w_ukv", "w_mem_kv", "w_br", "w_out", "w_ff1", "w_ff2")
SMALL = ("ln_in_g", "ln_in_b", "b_forget", "g_cq", "g_ckv", "ln1_g", "ln1_b", "ln2_g", "ln2_b")
ROW_CUT = ("w_mem_kv", "w_out", "w_ff2")


def _shard_2d(a):
    cols = a.shape[-1]
    rows = a.size // cols
    return a.reshape(2, rows // 2, cols)


def _full_from_slots(name, slots, shard_shape):
    parts = slots.reshape((N_CHIPS,) + shard_shape)
    axis = len(shard_shape) - (2 if name in ROW_CUT else 1)
    return jnp.concatenate([parts[i] for i in range(N_CHIPS)], axis=axis)


def _slots_from_full(name, full, shard_shape):
    axis = len(shard_shape) - (2 if name in ROW_CUT else 1)
    parts = jnp.stack(jnp.split(full, N_CHIPS, axis=axis))
    cols = shard_shape[-1]
    return parts.reshape(N_CHIPS, 2, -1, cols)


def _pack_small(cfg, vals):
    flat = jnp.concatenate([vals[k].reshape(-1).astype(F32) for k in SMALL])
    pad = (-flat.shape[0]) % (LANES * LANES)
    return jnp.pad(flat, (0, pad)).reshape(-1, LANES)


def _unpack_small(packed, like):
    flat, out, off = packed.reshape(-1), {}, 0
    for k in SMALL:
        n = like[k].size
        out[k] = flat[off:off + n].reshape(like[k].shape)
        off += n
    return out


def _step(cfg, x, mem, positions, target, w, m, v):
    shards = [_shard_2d(w[k].astype(BF16)) for k in BIG]
    gathered = _gather_weights(shards)
    big_w = {k: _full_from_slots(k, _to_absolute(g, s), w[k].shape) for k, g, s in zip(BIG, gathered, shards)}
    small_w = {k: w[k] for k in SMALL}
    loss_local, dx, big_g, small_g = _local_step(cfg, x, mem, positions, target, small_w, big_w)
    loss = lax.psum(loss_local, ("x", "y", "c"))

    grads_rel = [_to_relative(_slots_from_full(k, big_g[k], w[k].shape)) for k in BIG]
    reduced = _reduce_grads(grads_rel)
    outs_big = {}
    for k, (mine, theirs) in zip(BIG, reduced):
        halves = (2,) + mine.shape[1:]
        res = _adamw_shard(f"adamw_{k}", *[a.reshape(halves) for a in (w[k], m[k], v[k])], mine, theirs)
        outs_big[k] = tuple(a.reshape(w[k].shape) for a in res)

    g_small = _allreduce_small(_pack_small(cfg, small_g))
    packs = [_pack_small(cfg, {k: d_[k] for k in SMALL}) for d_ in (w, m, v)]
    dl, nm, nv = _slabwise("adamw_small", _adamw_math, [a[None] for a in (packs[0], g_small, packs[1], packs[2])],
                           [F32, F32, F32])
    outs_small = [_unpack_small(a[0] if a.ndim == 3 else a, w) for a in (g_small, dl, nm, nv)]

    names = SMALL[:2] + ("w_in", "b_forget", "w_uq", "g_cq", "w_ukv", "g_ckv", "w_mem_kv", "w_br", "w_out",
                         "ln1_g", "ln1_b", "w_ff1", "w_ff2", "ln2_g", "ln2_b")
    result = [loss, dx]
    for part in range(4):
        for k in names:
            result.append(outs_big[k][part] if k in outs_big else outs_small[part][k])
    return tuple(result)


def kernel(x, mem, positions, ln_in_g, ln_in_b, w_in, b_forget, w_uq, g_cq, w_ukv, g_ckv, w_mem_kv, w_br, w_out, ln1_g, ln1_b, w_ff1, w_ff2, ln2_g, ln2_b, loss_target, m_ln_in_g, m_ln_in_b, m_w_in, m_b_forget, m_w_uq, m_g_cq, m_w_ukv, m_g_ckv, m_w_mem_kv, m_w_br, m_w_out, m_ln1_g, m_ln1_b, m_w_ff1, m_w_ff2, m_ln2_g, m_ln2_b, v_ln_in_g, v_ln_in_b, v_w_in, v_b_forget, v_w_uq, v_g_cq, v_w_ukv, v_g_ckv, v_w_mem_kv, v_w_br, v_w_out, v_ln1_g, v_ln1_b, v_w_ff1, v_w_ff2, v_ln2_g, v_ln2_b):
    w = dict(ln_in_g=ln_in_g, ln_in_b=ln_in_b, w_in=w_in, b_forget=b_forget, w_uq=w_uq, g_cq=g_cq, w_ukv=w_ukv,
             g_ckv=g_ckv, w_mem_kv=w_mem_kv, w_br=w_br, w_out=w_out, ln1_g=ln1_g, ln1_b=ln1_b, w_ff1=w_ff1,
             w_ff2=w_ff2, ln2_g=ln2_g, ln2_b=ln2_b)
    m = dict(ln_in_g=m_ln_in_g, ln_in_b=m_ln_in_b, w_in=m_w_in, b_forget=m_b_forget, w_uq=m_w_uq, g_cq=m_g_cq,
             w_ukv=m_w_ukv, g_ckv=m_g_ckv, w_mem_kv=m_w_mem_kv, w_br=m_w_br, w_out=m_w_out, ln1_g=m_ln1_g,
             ln1_b=m_ln1_b, w_ff1=m_w_ff1, w_ff2=m_w_ff2, ln2_g=m_ln2_g, ln2_b=m_ln2_b)
    v = dict(ln_in_g=v_ln_in_g, ln_in_b=v_ln_in_b, w_in=v_w_in, b_forget=v_b_forget, w_uq=v_w_uq, g_cq=v_g_cq,
             w_ukv=v_w_ukv, g_ckv=v_g_ckv, w_mem_kv=v_w_mem_kv, w_br=v_w_br, w_out=v_w_out, ln1_g=v_ln1_g,
             ln1_b=v_ln1_b, w_ff1=v_w_ff1, w_ff2=v_w_ff2, ln2_g=v_ln2_g, ln2_b=v_ln2_b)
    return _step(Cfg(), x, mem, positions, loss_target, w, m, v)
```

```python
import functools
from typing import NamedTuple

import jax
import jax.numpy as jnp
from jax import lax
from jax.experimental import pallas as pl
from jax.experimental.pallas import tpu as pltpu

F32 = jnp.float32
BF16 = jnp.bfloat16
MESH = pl.DeviceIdType.MESH

LANES = 128
SUBLANES = 8
VMEM_BYTES = 64 * 1024 * 1024
N_CHIPS = 4
N_DEV = 8

FOX_DH = 64
MLA_NOPE = 64
MLA_ROPE = 32
MLA_V = 64
MEM_DH = 128
ROPE_BASE = 10000.0
LN_EPS = 1e-5
RMS_EPS = 1e-6
NEG_INF = -1e30
ATTN_TILE = 512

ADAM_LR = 0.001
ADAM_B1 = 0.9
ADAM_B2 = 0.999
ADAM_EPS = 1e-08
ADAM_WD = 0.01
ADAM_STEP = 10


class Cfg(NamedTuple):
    d: int = 1024
    depth: int = 4
    seq: int = 2048
    chunk: int = 64
    n_mem: int = 256
    fox_h: int = 8
    mla_h: int = 8
    q_rank: int = 384
    kv_rank: int = 256
    mem_h: int = 4
    d_ff: int = 4096

    @property
    def width(self):
        return self.fox_h * FOX_DH

    @property
    def alpha(self):
        return (2 * self.depth) ** 0.25

    @property
    def small_w(self):
        return LANES + self.q_rank + self.kv_rank + LANES

    @property
    def in_splits(self):
        return (3 * self.width, self.fox_h, self.q_rank, self.kv_rank, MLA_ROPE, self.width, 3 * self.d)


def _pcall(body, **kw):
    call = pl.pallas_call(body, **kw)
    return lambda *ops: call(*[pltpu.with_memory_space_constraint(o, pltpu.HBM) for o in ops])


def _nbytes(shape, dtype):
    n = 1
    for s in shape:
        n *= s
    return n * jnp.dtype(dtype).itemsize


def _tile(dim, target):
    if dim <= target:
        return dim
    t = target - target % LANES
    while t >= LANES:
        if dim % t == 0:
            return t
        t -= LANES
    return dim


def _params(block_bytes, scratch_bytes=0):
    est = 2 * block_bytes + scratch_bytes + 24 * 1024 * 1024
    return pltpu.CompilerParams(vmem_limit_bytes=int(min(max(est, 32 * 1024 * 1024), VMEM_BYTES - 4 * 1024 * 1024)))


_DIMS = {"nn": (((1,), (0,)), ((), ())), "nt": (((1,), (1,)), ((), ())), "tn": (((0,), (0,)), ((), ()))}


MM_TILE = 1024
MM_BLOCK_BYTES = 16 * 1024 * 1024


def _mm_tiles(mode, pairs, out_dtypes, m, n, tm, tn):
    fixed_m, fixed_n = tm is not None, tn is not None
    tm, tn = _tile(m, tm or MM_TILE), _tile(n, tn or MM_TILE)

    def block_bytes(tm_, tn_):
        total = sum(_nbytes((tm_, tn_), dt) for dt in out_dtypes)
        for a, b in pairs:
            k = a.shape[0] if mode == "tn" else a.shape[1]
            total += _nbytes((k, tm_), a.dtype) + _nbytes((k, tn_), b.dtype)
        return total

    while block_bytes(tm, tn) > MM_BLOCK_BYTES:
        if not fixed_m and tm >= tn and tm > 2 * LANES:
            tm = _tile(m, tm // 2)
        elif not fixed_n and tn > 2 * LANES:
            tn = _tile(n, tn // 2)
        elif not fixed_m and tm > 2 * LANES:
            tm = _tile(m, tm // 2)
        else:
            break
    return tm, tn


def _mm(name, mode, pairs, out_dtypes, tm=None, tn=None, epi=None, row_extras=(), bc_extras=()):
    a0, b0 = pairs[0]
    m = a0.shape[1] if mode == "tn" else a0.shape[0]
    n = b0.shape[0] if mode == "nt" else b0.shape[1]
    tm, tn = _mm_tiles(mode, pairs, out_dtypes, m, n, tm, tn)
    in_specs, ops, blk = [], [], 0
    for a, b in pairs:
        if mode == "tn":
            k = a.shape[0]
            sa, sha = pl.BlockSpec((k, tm), lambda i, j: (0, i)), (k, tm)
        else:
            k = a.shape[1]
            sa, sha = pl.BlockSpec((tm, k), lambda i, j: (i, 0)), (tm, k)
        if mode == "nt":
            sb, shb = pl.BlockSpec((tn, k), lambda i, j: (j, 0)), (tn, k)
        else:
            sb, shb = pl.BlockSpec((k, tn), lambda i, j: (0, j)), (k, tn)
        in_specs += [sa, sb]
        ops += [a, b]
        blk += _nbytes(sha, a.dtype) + _nbytes(shb, b.dtype)
    for e in row_extras:
        w = e.shape[1]
        if w == n:
            in_specs.append(pl.BlockSpec((tm, tn), lambda i, j: (i, j)))
            blk += _nbytes((tm, tn), e.dtype)
        else:
            in_specs.append(pl.BlockSpec((tm, w), lambda i, j: (i, 0)))
            blk += _nbytes((tm, w), e.dtype)
        ops.append(e)
    for e in bc_extras:
        r, w = e.shape
        if w == n:
            in_specs.append(pl.BlockSpec((r, tn), lambda i, j: (0, j)))
        else:
            in_specs.append(pl.BlockSpec((r, w), lambda i, j: (0, 0)))
        blk += _nbytes((r, w), e.dtype)
        ops.append(e)
    npairs, nrow, nbc, nout = len(pairs), len(row_extras), len(bc_extras), len(out_dtypes)
    dims = _DIMS[mode]

    def body(*refs):
        acc = None
        for p in range(npairs):
            a = refs[2 * p][...].astype(BF16)
            b = refs[2 * p + 1][...].astype(BF16)
            d = lax.dot_general(a, b, dims, preferred_element_type=F32)
            acc = d if acc is None else acc + d
        ex = [r[...] for r in refs[2 * npairs:2 * npairs + nrow + nbc]]
        outs = (acc,) if epi is None else epi(acc, *ex)
        for o_ref, o in zip(refs[2 * npairs + nrow + nbc:], outs):
            o_ref[...] = o.astype(o_ref.dtype)

    blk += sum(_nbytes((tm, tn), dt) for dt in out_dtypes) + 2 * _nbytes((tm, tn), F32)
    res = _pcall(
        body,
        name=name,
        grid=(m // tm, n // tn),
        in_specs=in_specs,
        out_specs=[pl.BlockSpec((tm, tn), lambda i, j: (i, j)) for _ in range(nout)],
        out_shape=[jax.ShapeDtypeStruct((m, n), dt) for dt in out_dtypes],
        compiler_params=_params(blk),
    )(*ops)
    return res[0] if nout == 1 else res


def _rowwise(name, fn, row_ins, bc_ins, outs, accs=(), tm=256):
    rows = row_ins[0].shape[0]
    tm = min(tm, rows)
    assert rows % tm == 0
    nrow, nbc, nout, nacc = len(row_ins), len(bc_ins), len(outs), len(accs)
    in_specs = [pl.BlockSpec((tm, a.shape[1]), lambda i: (i, 0)) for a in row_ins]
    in_specs += [pl.BlockSpec(a.shape, lambda i: (0, 0)) for a in bc_ins]
    out_specs = [pl.BlockSpec((tm, w), lambda i: (i, 0)) for w, _ in outs]
    out_specs += [pl.BlockSpec(s, lambda i: (0, 0)) for s in accs]
    out_shape = [jax.ShapeDtypeStruct((rows, w), dt) for w, dt in outs]
    out_shape += [jax.ShapeDtypeStruct(s, F32) for s in accs]

    def body(*refs):
        vals = fn(*[r[...] for r in refs[:nrow + nbc]])
        o_refs = refs[nrow + nbc:]
        for r, v in zip(o_refs[:nout], vals[:nout]):
            r[...] = v.astype(r.dtype)
        if nacc:
            @pl.when(pl.program_id(0) == 0)
            def _():
                for r in o_refs[nout:]:
                    r[...] = jnp.zeros(r.shape, F32)

            for r, v in zip(o_refs[nout:], vals[nout:]):
                r[...] += v

    blk = sum(_nbytes((tm, a.shape[1]), a.dtype) for a in row_ins) + sum(_nbytes(a.shape, a.dtype) for a in bc_ins)
    blk += sum(_nbytes((tm, w), dt) for w, dt in outs) + sum(_nbytes(s, F32) for s in accs)
    res = _pcall(
        body,
        name=name,
        grid=(rows // tm,),
        in_specs=in_specs,
        out_specs=out_specs,
        out_shape=out_shape,
        compiler_params=_params(2 * blk),
    )(*row_ins, *bc_ins)
    return res


def _ln(z, g, b):
    mu = jnp.mean(z, axis=-1, keepdims=True)
    zc = z - mu
    var = jnp.mean(zc * zc, axis=-1, keepdims=True)
    return zc * lax.rsqrt(var + LN_EPS) * g + b


def _rms(x, g):
    return x * lax.rsqrt(jnp.mean(x * x, axis=-1, keepdims=True) + RMS_EPS) * g


def _colsum(v):
    return jnp.sum(v, axis=0, keepdims=True)


def _rope_swap(x):
    w = x.shape[1]
    lane = lax.broadcasted_iota(jnp.int32, (1, w), 1) % LANES
    from_left = pltpu.roll(x, 16, 1)
    from_right = pltpu.roll(x, w - 16, 1)
    lo = (lane >= MLA_NOPE) & (lane < MLA_NOPE + 16)
    hi = (lane >= MLA_NOPE + 16) & (lane < MLA_NOPE + 32)
    return jnp.where(hi, from_left, jnp.where(lo, from_right, 0.0))


def _rope(x, cos_t, sin_t):
    nh = x.shape[1] // LANES
    ct, st = jnp.tile(cos_t, (1, nh)), jnp.tile(sin_t, (1, nh))
    return x * ct + _rope_swap(x) * st


def _rope_t(dy, cos_t, sin_t):
    nh = dy.shape[1] // LANES
    ct, st = jnp.tile(cos_t, (1, nh)), jnp.tile(sin_t, (1, nh))
    return dy * ct + _rope_swap(dy * st)


def _block_cumsum(v, carry, reverse):
    tb = v.shape[0]
    r = lax.broadcasted_iota(jnp.int32, (tb, tb), 0)
    c = lax.broadcasted_iota(jnp.int32, (tb, tb), 1)
    tri = jnp.where((c >= r) if reverse else (c <= r), 1.0, 0.0).astype(BF16)
    hi = v.astype(BF16)
    r1 = v - hi.astype(F32)
    mid = r1.astype(BF16)
    lo = (r1 - mid.astype(F32)).astype(BF16)
    out = carry + sum(jnp.dot(tri, p, preferred_element_type=F32) for p in (hi, mid, lo))
    return out, (out[0:1, :] if reverse else out[tb - 1:tb, :])


def _forget_cumsum(name, logf, batch, seq, heads, tb):
    nb = seq // tb

    def body(x_ref, keys_ref, rows_ref, carry):
        @pl.when(pl.program_id(1) == 0)
        def _():
            carry[...] = jnp.zeros(carry.shape, F32)

        out, carry[...] = _block_cumsum(x_ref[...], carry[...], False)
        keys_ref[...] = jnp.concatenate([jnp.broadcast_to(out[:, h:h + 1], (tb, LANES)) for h in range(heads)], axis=1)
        out_t = out.T
        for g in range(heads // 2):
            rows_ref[g] = out_t[2 * g:2 * g + 2, :]

    return _pcall(
        body,
        name=name,
        grid=(batch, nb),
        in_specs=[pl.BlockSpec((tb, LANES), lambda b, i: (b * nb + i, 0))],
        out_specs=[pl.BlockSpec((tb, heads * LANES), lambda b, i: (b * nb + i, 0)),
                   pl.BlockSpec((None, None, heads // 2, 2, tb), lambda b, i: (b, i, 0, 0, 0))],
        out_shape=[jax.ShapeDtypeStruct((batch * seq, heads * LANES), F32),
                   jax.ShapeDtypeStruct((batch, nb, heads // 2, 2, tb), F32)],
        scratch_shapes=[pltpu.VMEM((1, LANES), F32)],
        compiler_params=_params(4 * tb * (heads + 2) * LANES * 4),
    )(logf)


def _forget_cumsum_bwd(name, dcf, batch, seq, heads, tb):
    nb = seq // tb

    def body(x_ref, o_ref, carry):
        @pl.when(pl.program_id(1) == 0)
        def _():
            carry[...] = jnp.zeros(carry.shape, F32)

        lane = lax.broadcasted_iota(jnp.int32, (1, LANES), 1)
        v = jnp.zeros((tb, LANES), F32)
        for g in range(heads // 2):
            blk = x_ref[:, g * LANES:(g + 1) * LANES]
            moved = pltpu.roll(blk, 2 * g, 1) if g else blk
            v = v + jnp.where((lane >= 2 * g) & (lane < 2 * g + 2), moved, 0.0)
        o_ref[...], carry[...] = _block_cumsum(v, carry[...], True)

    return _pcall(
        body,
        name=name,
        grid=(batch, nb),
        in_specs=[pl.BlockSpec((tb, (heads // 2) * LANES), lambda b, i: (b * nb + nb - 1 - i, 0))],
        out_specs=pl.BlockSpec((tb, LANES), lambda b, i: (b * nb + nb - 1 - i, 0)),
        out_shape=jax.ShapeDtypeStruct((batch * seq, LANES), F32),
        scratch_shapes=[pltpu.VMEM((1, LANES), F32)],
        compiler_params=_params(4 * tb * (heads // 2 + 1) * LANES * 4),
    )(dcf)


class Attn(NamedTuple):
    batch: int
    sq: int
    sk: int
    groups: int
    hq: int
    hv: int
    mode: str
    scale: float
    chunk: int
    tq: int
    tk: int

    @property
    def hg(self):
        return self.hv

    @property
    def qw(self):
        return LANES * self.hg // self.hq

    @property
    def dv(self):
        return LANES // self.hv


def _head_lanes(j, dv):
    lane = lax.broadcasted_iota(jnp.int32, (1, LANES), 1)
    return (lane >= j * dv) & (lane < (j + 1) * dv)


def _head_q(sp, j, q_blk):
    if sp.hq == 2:
        return jnp.where(_head_lanes(j, FOX_DH), q_blk, jnp.zeros_like(q_blk))
    return q_blk[:, LANES * j:LANES * (j + 1)]


def _head_rows(sp, j):
    return slice(j * sp.dv, (j + 1) * sp.dv) if sp.hg == 2 else slice(None)


def _scores_t(sp, j, k_c, q_j, cfq_rows, cfk_rep, k0, q0, masked):
    tk, tq = k_c.shape[0], q_j.shape[0]
    k_j = k_c if sp.hq == 2 else k_c[:, LANES * j:LANES * (j + 1)]
    st = lax.dot_general(k_j, q_j, _DIMS["nt"], preferred_element_type=F32) * sp.scale
    if sp.mode == "fox":
        st = st + (cfq_rows[j:j + 1, :] - jnp.tile(cfk_rep[:, LANES * j:LANES * (j + 1)], (1, tq // LANES)))
    if masked:
        kidx = k0 + lax.broadcasted_iota(jnp.int32, (tk, tq), 0)
        qidx = q0 + lax.broadcasted_iota(jnp.int32, (tk, tq), 1)
        if sp.mode == "chunk":
            shift = sp.chunk.bit_length() - 1
            kidx, qidx = jnp.right_shift(kidx, shift), jnp.right_shift(qidx, shift)
        st = jnp.where(kidx <= qidx, st, NEG_INF)
    return st


def _attn_fwd(name, sp, q, k, v, cfq=None, cfk=None):
    (qa, qo), (ka, ko), (va, vo) = q, k, v
    tq, tk, hg, qw = sp.tq, sp.tk, sp.hg, sp.qw
    nqb, nkc = sp.sq // tq, sp.sk // tk
    fox, causal = sp.mode == "fox", sp.mode != "none"
    assert sp.sq % tq == 0 and sp.sk % tk == 0 and (not causal or (tq == tk and sp.sq == sp.sk))

    def body(*refs):
        if fox:
            q_ref, k_ref, v_ref, cfq_ref, cfk_ref, o_ref, lse_ref, acc_scr = refs
        else:
            q_ref, k_ref, v_ref, o_ref, lse_ref, acc_scr = refs
        i = pl.program_id(2)
        q0 = i * tq
        q_blk = q_ref[...]
        qs = [_head_q(sp, j, q_blk) for j in range(hg)]
        acc_scr[...] = jnp.zeros(acc_scr.shape, F32)

        def chunk(kc, carry, masked):
            ms, ls = carry
            k0 = pl.multiple_of(kc * tk, tk)
            k_c = k_ref[pl.ds(k0, tk), :]
            v_c = v_ref[pl.ds(k0, tk), :]
            new_m, new_l = [], []
            for j in range(hg):
                st = _scores_t(sp, j, k_c, qs[j], cfq_ref[...] if fox else None,
                               cfk_ref[pl.ds(k0, tk), :] if fox else None, k0, q0, masked)
                m_new = jnp.maximum(ms[j], jnp.max(st, axis=0, keepdims=True))
                alpha = jnp.exp(ms[j] - m_new)
                pt = jnp.exp(st - m_new)
                new_m.append(m_new)
                new_l.append(alpha * ls[j] + jnp.sum(pt, axis=0, keepdims=True))
                pv = lax.dot_general(v_c, pt.astype(BF16), _DIMS["tn"], preferred_element_type=F32)
                r = _head_rows(sp, j)
                acc_scr[r, :] = acc_scr[r, :] * alpha + pv[r, :]
            return tuple(new_m), tuple(new_l)

        carry = (tuple(jnp.full((1, tq), NEG_INF, F32) for _ in range(hg)),
                 tuple(jnp.zeros((1, tq), F32) for _ in range(hg)))
        if causal:
            carry = lax.fori_loop(0, i, functools.partial(chunk, masked=False), carry)
            ms, ls = chunk(i, carry, True)
        else:
            ms, ls = lax.fori_loop(0, nkc, functools.partial(chunk, masked=False), carry)
        for j in range(hg):
            r = _head_rows(sp, j)
            acc_scr[r, :] = acc_scr[r, :] / ls[j]
            lse_ref[j:j + 1, :] = ms[j] + jnp.log(ls[j])
        o_ref[...] = acc_scr[...].T

    in_specs = [
        pl.BlockSpec((tq, qw), lambda b, g, i: (b * nqb + i, qo + g)),
        pl.BlockSpec((sp.sk, qw), lambda b, g, i: (b, ko + g)),
        pl.BlockSpec((sp.sk, LANES), lambda b, g, i: (b, vo + g)),
    ]
    ops = [qa, ka, va]
    stat_blk = pl.BlockSpec((None, None, None, hg, tq), lambda b, g, i: (b, g, i, 0, 0))
    if fox:
        in_specs += [pl.BlockSpec((None, None, None, hg, tq), lambda b, g, i: (b, i, g, 0, 0)),
                     pl.BlockSpec((sp.sk, hg * LANES), lambda b, g, i: (b, g))]
        ops += [cfq, cfk]
    blk = _nbytes((tq, qw), BF16) + _nbytes((sp.sk, qw + LANES), BF16) + 2 * _nbytes((tq, LANES), F32)
    blk += _nbytes((sp.sk, hg * LANES), F32) + 6 * _nbytes((tk, tq), F32)
    return _pcall(
        body,
        name=name,
        grid=(sp.batch, sp.groups, nqb),
        in_specs=in_specs,
        out_specs=[pl.BlockSpec((tq, LANES), lambda b, g, i: (b * nqb + i, g)), stat_blk],
        out_shape=[
            jax.ShapeDtypeStruct((sp.batch * sp.sq, sp.groups * LANES), F32),
            jax.ShapeDtypeStruct((sp.batch, sp.groups, nqb, hg, tq), F32),
        ],
        scratch_shapes=[pltpu.VMEM((LANES, tq), F32)],
        compiler_params=_params(blk, tq * LANES * 4),
    )(*ops)


def _attn_bwd(name, sp, q, k, v, o, lse, do, cfq=None, cfk=None):
    (qa, qo), (ka, ko), (va, vo) = q, k, v
    tq, tk, hg, qw, dv = sp.tq, sp.tk, sp.hg, sp.qw, sp.dv
    nqb, nkb = sp.sq // tq, sp.sk // tk
    fox, causal = sp.mode == "fox", sp.mode != "none"
    assert sp.sq % tq == 0 and sp.sk % tk == 0 and (not causal or (tq == tk and sp.sq == sp.sk))

    def body(*refs):
        if fox:
            (q_ref, k_ref, v_ref, lse_ref, do_ref, cfq_ref, cfk_ref, kall_ref, vall_ref, cfkall_ref,
             dq_ref, dk_ref, dv_ref, dcf_ref, delta_scr, dk_scr, dv_scr, dqt_scr, dcf_scr) = refs
        else:
            (q_ref, k_ref, v_ref, o_ref, lse_ref, do_ref,
             dq_ref, dk_ref, dv_ref, delta_scr, dk_scr, dv_scr, dqt_scr) = refs
        kb = pl.program_id(2)
        k0 = kb * tk
        heads = [_head_lanes(j, dv) for j in range(hg)]

        def head_do(j, do_c):
            return jnp.where(heads[j], do_c, jnp.zeros_like(do_c)) if hg == 2 else do_c

        def probs_t(j, k_c, v_c, q_c, do_c, i, cf_keys, c0, masked):
            st = _scores_t(sp, j, k_c, _head_q(sp, j, q_c), cfq_ref[i] if fox else None, cf_keys, c0, i * tq, masked)
            pt = jnp.exp(st - lse_ref[i][j:j + 1, :])
            dpt = lax.dot_general(v_c, head_do(j, do_c), _DIMS["nt"], preferred_element_type=F32)
            return pt, dpt

        @pl.when(kb == 0)
        def _():
            dqt_scr[...] = jnp.zeros(dqt_scr.shape, F32)

            def fill(i, carry):
                r0 = pl.multiple_of(i * tq, tq)
                do_c = do_ref[pl.ds(r0, tq), :]
                if fox:
                    q_c = q_ref[pl.ds(r0, tq), :]

                    def keys(kc, acc, masked):
                        c0 = pl.multiple_of(kc * tk, tk)
                        out = []
                        for j in range(hg):
                            pt, dpt = probs_t(j, kall_ref[pl.ds(c0, tk), :], vall_ref[pl.ds(c0, tk), :], q_c, do_c, i,
                                              cfkall_ref[pl.ds(c0, tk), :], c0, masked)
                            out.append(acc[j] + jnp.sum(pt * dpt, axis=0, keepdims=True))
                        return tuple(out)

                    d = lax.fori_loop(0, i, functools.partial(keys, masked=False),
                                      tuple(jnp.zeros((1, tq), F32) for _ in range(hg)))
                    d = keys(i, d, True)
                    for j in range(hg):
                        delta_scr[i, j:j + 1, :] = d[j]
                else:
                    prod_t = (do_c.astype(F32) * o_ref[pl.ds(r0, tq), :]).T
                    for j in range(hg):
                        delta_scr[i, j:j + 1, :] = jnp.sum(prod_t[_head_rows(sp, j), :], axis=0, keepdims=True)
                return carry

            lax.fori_loop(0, nqb, fill, 0)

        k_blk = k_ref[...]
        v_blk = v_ref[...]
        k_t = k_blk.astype(F32).T.astype(BF16)
        dk_scr[...] = jnp.zeros(dk_scr.shape, F32)
        dv_scr[...] = jnp.zeros(dv_scr.shape, F32)
        if fox:
            dcf_scr[...] = jnp.zeros(dcf_scr.shape, F32)

        def qblock(i, carry, masked):
            r0 = pl.multiple_of(i * tq, tq)
            q_c = q_ref[pl.ds(r0, tq), :]
            do_c = do_ref[pl.ds(r0, tq), :]
            for j in range(hg):
                pt, dpt = probs_t(j, k_blk, v_blk, q_c, do_c, i, cfk_ref[...] if fox else None, k0, masked)
                dst = pt * (dpt - delta_scr[i][j:j + 1, :])
                if fox:
                    part = dst[:, :LANES]
                    for t in range(1, tq // LANES):
                        part = part + dst[:, t * LANES:(t + 1) * LANES]
                    dcf_scr[j] += part
                ds_b = (dst * sp.scale).astype(BF16)
                dv_scr[j] += jnp.dot(pt.astype(BF16), do_c, preferred_element_type=F32)
                dk_scr[j] += jnp.dot(ds_b, q_c if sp.hq == 2 else _head_q(sp, j, q_c), preferred_element_type=F32)
                if sp.hq == 2:
                    r = pl.ds(j * FOX_DH, FOX_DH)
                    dqt_scr[i, r, :] += jnp.dot(k_t[j * FOX_DH:(j + 1) * FOX_DH, :], ds_b, preferred_element_type=F32)
                else:
                    r = pl.ds(j * LANES, LANES)
                    dqt_scr[i, r, :] += jnp.dot(k_t[j * LANES:(j + 1) * LANES, :], ds_b, preferred_element_type=F32)
            return carry

        if causal:
            qblock(kb, 0, True)
            lax.fori_loop(kb + 1, nqb, functools.partial(qblock, masked=False), 0)
        else:
            lax.fori_loop(0, nqb, functools.partial(qblock, masked=False), 0)

        @pl.when(kb == nkb - 1)
        def _():
            def untranspose(i, carry):
                dq_ref[pl.ds(pl.multiple_of(i * tq, tq), tq), :] = dqt_scr[i].T
                return carry

            lax.fori_loop(0, nqb, untranspose, 0)

        if hg == 2:
            dv_ref[...] = jnp.where(heads[0], dv_scr[0], dv_scr[1]).astype(dv_ref.dtype)
        else:
            dv_ref[...] = dv_scr[0].astype(dv_ref.dtype)
        if sp.hq == 2:
            dk_ref[...] = jnp.where(_head_lanes(0, FOX_DH), dk_scr[0], dk_scr[1]).astype(dk_ref.dtype)
        elif hg == 2:
            dk_ref[...] = jnp.concatenate([dk_scr[0], dk_scr[1]], axis=1).astype(dk_ref.dtype)
        else:
            dk_ref[...] = dk_scr[0].astype(dk_ref.dtype)
        if fox:
            lane = lax.broadcasted_iota(jnp.int32, (1, LANES), 1)
            sums = [jnp.sum(dcf_scr[j], axis=1, keepdims=True) for j in range(hg)]
            dcf_ref[...] = jnp.where(lane == 0, -sums[0], jnp.where(lane == 1, -sums[1], 0.0))

    seq_lanes = lambda b, g, kb: (b, g)
    key_blk = lambda b, g, kb: (b * nkb + kb, g)
    stats = pl.BlockSpec((None, None, nqb, hg, tq), lambda b, g, kb: (b, g, 0, 0, 0))
    in_specs = [
        pl.BlockSpec((sp.sq, qw), lambda b, g, kb: (b, qo + g)),
        pl.BlockSpec((tk, qw), lambda b, g, kb: (b * nkb + kb, ko + g)),
        pl.BlockSpec((tk, LANES), lambda b, g, kb: (b * nkb + kb, vo + g)),
    ]
    ops = [qa, ka, va]
    if not fox:
        in_specs.append(pl.BlockSpec((sp.sq, LANES), seq_lanes))
        ops.append(o)
    in_specs += [stats, pl.BlockSpec((sp.sq, LANES), seq_lanes)]
    ops += [lse, do]
    out_specs = [pl.BlockSpec((sp.sq, qw), seq_lanes), pl.BlockSpec((tk, qw), key_blk), pl.BlockSpec((tk, LANES), key_blk)]
    out_shape = [
        jax.ShapeDtypeStruct((sp.batch * sp.sq, sp.groups * qw), F32),
        jax.ShapeDtypeStruct((sp.batch * sp.sk, sp.groups * qw), BF16),
        jax.ShapeDtypeStruct((sp.batch * sp.sk, sp.groups * LANES), BF16),
    ]
    scratch = [pltpu.VMEM((nqb, hg, tq), F32), pltpu.VMEM((hg, tk, LANES), F32), pltpu.VMEM((hg, tk, LANES), F32),
               pltpu.VMEM((nqb, qw, tq), F32)]
    if fox:
        in_specs += [
            pl.BlockSpec((None, nqb, None, hg, tq), lambda b, g, kb: (b, 0, g, 0, 0)),
            pl.BlockSpec((tk, hg * LANES), key_blk),
            pl.BlockSpec((sp.sk, qw), lambda b, g, kb: (b, ko + g)),
            pl.BlockSpec((sp.sk, LANES), lambda b, g, kb: (b, vo + g)),
            pl.BlockSpec((sp.sk, hg * LANES), seq_lanes),
        ]
        ops += [cfq, cfk, ka, va, cfk]
        out_specs.append(pl.BlockSpec((tk, LANES), key_blk))
        out_shape.append(jax.ShapeDtypeStruct((sp.batch * sp.sk, sp.groups * LANES), F32))
        scratch.append(pltpu.VMEM((hg, tk, LANES), F32))
    blk = _nbytes((sp.sq, qw), BF16) + _nbytes((sp.sq, LANES), BF16) + 2 * _nbytes((sp.sq, LANES), F32)
    blk += _nbytes((sp.sq, qw), F32) + 4 * _nbytes((tk, qw), BF16) + 8 * _nbytes((tq, tk), F32)
    blk += (_nbytes((sp.sk, qw + LANES), BF16) + _nbytes((sp.sk, hg * LANES), F32)) if fox else 0
    return _pcall(
        body,
        name=name,
        grid=(sp.batch, sp.groups, nkb),
        in_specs=in_specs,
        out_specs=out_specs,
        out_shape=out_shape,
        scratch_shapes=scratch,
        compiler_params=_params(blk, _nbytes((sp.sq, LANES), F32) + 4 * _nbytes((tk, LANES), F32)),
    )(*ops)


def _slabwise(name, fn, ins, out_dtypes, rows_per_step=512):
    n = max(a.shape[0] for a in ins)
    rows, cols = ins[0].shape[1:]
    tr = min(rows_per_step, rows)
    while rows % tr:
        tr //= 2
    assert tr % 16 == 0 or tr == rows, (name, rows, tr)

    def spec(a):
        if a.shape[0] == 1:
            return pl.BlockSpec((None, tr, cols), lambda s, i: (0, i, 0))
        return pl.BlockSpec((None, tr, cols), lambda s, i: (s, i, 0))

    def body(*refs):
        vals = fn(*[r[...] for r in refs[:len(ins)]])
        for r, v in zip(refs[len(ins):], vals):
            r[...] = v.astype(r.dtype)

    blk = (len(ins) + len(out_dtypes)) * _nbytes((tr, cols + LANES), F32)
    res = _pcall(
        body,
        name=name,
        grid=(n, rows // tr),
        in_specs=[spec(a) for a in ins],
        out_specs=[pl.BlockSpec((None, tr, cols), lambda s, i: (s, i, 0)) for _ in out_dtypes],
        out_shape=[jax.ShapeDtypeStruct((n, rows, cols), dt) for dt in out_dtypes],
        compiler_params=_params(2 * blk),
    )(*ins)
    return res


def _adamw_math(w, g, m, v):
    m = ADAM_B1 * m + (1.0 - ADAM_B1) * g
    v = ADAM_B2 * v + (1.0 - ADAM_B2) * jnp.square(g)
    m_hat = m / (1.0 - ADAM_B1 ** ADAM_STEP)
    v_hat = v / (1.0 - ADAM_B2 ** ADAM_STEP)
    delta = -ADAM_LR * (m_hat / (jnp.sqrt(v_hat) + ADAM_EPS) + ADAM_WD * w)
    return delta, m, v


def _peer(rel):
    x, y, c = lax.axis_index("x"), lax.axis_index("y"), lax.axis_index("c")
    if rel == "c":
        return (x, y, 1 - c)
    return ((1 - x) if rel in (1, 3) else x, (1 - y) if rel in (2, 3) else y, c)


def _exchange(name, ins, out_shapes, plan, n_copies, aliases=None):
    n_in, n_out = len(ins), len(out_shapes)

    def body(*refs):
        in_refs, out_refs = refs[:n_in], refs[n_in:n_in + n_out]
        send_sems, recv_sems = refs[n_in + n_out:]
        copies = plan(in_refs, out_refs)
        assert len(copies) == n_copies, (name, len(copies))
        started = []
        for i, (src, dst, rel) in enumerate(copies):
            if rel is None:
                cp = pltpu.make_async_copy(src, dst, send_sems.at[i])
            else:
                cp = pltpu.make_async_remote_copy(src_ref=src, dst_ref=dst, send_sem=send_sems.at[i],
                                                  recv_sem=recv_sems.at[i], device_id=_peer(rel), device_id_type=MESH)
            cp.start()
            started.append(cp)
        for cp in started:
            cp.wait()

    any_spec = pl.BlockSpec(memory_space=pl.ANY)
    res = _pcall(
        body,
        name=name,
        in_specs=[any_spec] * n_in,
        out_specs=[any_spec] * n_out,
        out_shape=out_shapes,
        scratch_shapes=[pltpu.SemaphoreType.DMA((n_copies,)), pltpu.SemaphoreType.DMA((n_copies,))],
        input_output_aliases=aliases or {},
    )(*ins)
    return res


def _gather_weights(shards):
    n = len(shards)
    shapes = [jax.ShapeDtypeStruct((N_CHIPS,) + s.shape, s.dtype) for s in shards]

    def plan1(in_refs, out_refs):
        c = lax.axis_index("c")
        return [(s.at[c], g.at[rel, 0], rel) for s, g in zip(in_refs, out_refs) for rel in (1, 2, 3)]

    got = _exchange("gather_weights_ici", shards, shapes, plan1, 3 * n)

    def plan2(in_refs, out_refs):
        return [(g_in.at[pl.ds(1, 3), 0], g_out.at[pl.ds(1, 3), 1], "c") for g_in, g_out in zip(in_refs, out_refs)]

    return _exchange("gather_weights_d2d", got, shapes, plan2, n, aliases={i: i for i in range(n)})


def _to_absolute(rel_arr, own):
    x, y, c = lax.axis_index("x"), lax.axis_index("y"), lax.axis_index("c")
    tail = rel_arr.shape[2:]
    zeros = (0,) * len(tail)
    own_rel = jnp.concatenate([lax.dynamic_slice(own, (h ^ c,) + zeros, (1,) + tail) for h in (0, 1)], axis=0)
    rel_arr = jnp.concatenate([own_rel[None], rel_arr[1:]], axis=0)
    parts = []
    for sx in (0, 1):
        for sy in (0, 1):
            for h in (0, 1):
                rel = (sx ^ x) + 2 * (sy ^ y)
                parts.append(lax.dynamic_slice(rel_arr, (rel, h ^ c) + zeros, (1, 1) + tail))
    return jnp.concatenate(parts, axis=0).reshape((N_CHIPS, 2) + tail)


def _to_relative(abs_arr):
    x, y, c = lax.axis_index("x"), lax.axis_index("y"), lax.axis_index("c")
    tail = abs_arr.shape[2:]
    zeros = (0,) * len(tail)
    parts = []
    for hr in (0, 1):
        for rel in (0, 1, 2, 3):
            chip = 2 * (x ^ (rel & 1)) + (y ^ (rel >> 1))
            parts.append(lax.dynamic_slice(abs_arr, (chip, c ^ hr) + zeros, (1, 1) + tail))
    return jnp.concatenate(parts, axis=0).reshape((2, N_CHIPS) + tail)


def _reduce_grads(grads_rel):
    n = len(grads_rel)
    half_shapes = [jax.ShapeDtypeStruct(g.shape[1:], g.dtype) for g in grads_rel]

    def plan1(in_refs, out_refs):
        return [(g.at[1], r, "c") for g, r in zip(in_refs, out_refs)]

    from_sibling = _exchange("reduce_grads_pair", grads_rel, half_shapes, plan1, n)
    pair = [
        _slabwise(f"reduce_grads_pair_sum_{i}", lambda a, b: (a.astype(F32) + b.astype(F32),),
                  [g[0], r], [BF16])[0]
        for i, (g, r) in enumerate(zip(grads_rel, from_sibling))
    ]
    chip_shapes = [jax.ShapeDtypeStruct((3,) + p.shape[1:], p.dtype) for p in pair]

    def plan2(in_refs, out_refs):
        return [(p.at[rel], r.at[rel - 1], rel) for p, r in zip(in_refs, out_refs) for rel in (1, 2, 3)]

    from_chips = _exchange("reduce_grads_chips", pair, chip_shapes, plan2, 3 * n)
    total = [
        _slabwise(f"reduce_grads_chip_sum_{i}",
                  lambda a, b, c_, d: (a.astype(F32) + b.astype(F32) + c_.astype(F32) + d.astype(F32),),
                  [p[0:1], r[0:1], r[1:2], r[2:3]], [F32])[0]
        for i, (p, r) in enumerate(zip(pair, from_chips))
    ]
    def plan3(in_refs, out_refs):
        return [(t, r, "c") for t, r in zip(in_refs, out_refs)]

    theirs = _exchange("reduce_grads_share", total, [jax.ShapeDtypeStruct(t.shape, F32) for t in total], plan3, n)
    return list(zip(total, theirs))


def _adamw_shard(name, w, m, v, mine, theirs, rows_per_step=256):
    _, rows, cols = w.shape
    tr = min(rows_per_step, rows)
    while rows % tr:
        tr //= 2

    def body(w_ref, m_ref, v_ref, mine_ref, theirs_ref, g_ref, d_ref, nm_ref, nv_ref):
        g = jnp.where(pl.program_id(0) == lax.axis_index("c"), mine_ref[...], theirs_ref[...])
        d, nm, nv = _adamw_math(w_ref[...], g, m_ref[...], v_ref[...])
        g_ref[...], d_ref[...], nm_ref[...], nv_ref[...] = g, d, nm, nv

    half = pl.BlockSpec((None, tr, cols), lambda h, i: (h, i, 0))
    one = pl.BlockSpec((None, tr, cols), lambda h, i: (0, i, 0))
    return _pcall(
        body,
        name=name,
        grid=(2, rows // tr),
        in_specs=[half, half, half, one, one],
        out_specs=[half] * 4,
        out_shape=[jax.ShapeDtypeStruct(w.shape, F32)] * 4,
        compiler_params=_params(2 * 9 * _nbytes((tr, cols + LANES), F32)),
    )(w, m, v, mine, theirs)


def _allreduce_small(v):
    rows = v.shape[0]

    def body(v_ref, sum_ref, all_ref, send_sems, recv_sems, local_sem):
        x, y, c = lax.axis_index("x"), lax.axis_index("y"), lax.axis_index("c")
        sibling = (x, y, 1 - c)
        chips = [(1 - x, y), (x, 1 - y), (1 - x, 1 - y)]

        def slab(px, py, pc):
            return all_ref.at[pl.ds((4 * px + 2 * py + pc) * rows, rows), :]

        def copy(k, block, to, src=None):
            return pltpu.make_async_remote_copy(
                src_ref=slab(*block) if src is None else src, dst_ref=slab(*block), send_sem=send_sems.at[k],
                recv_sem=recv_sems.at[k], device_id=to, device_id_type=MESH)

        mine = pltpu.make_async_copy(v_ref, slab(x, y, c), local_sem)
        mine.start()
        first = [copy(0, (x, y, c), sibling, src=v_ref)]
        first += [copy(1 + j, (x, y, c), (*chip, c), src=v_ref) for j, chip in enumerate(chips)]
        for cp in first:
            cp.start()
        passed = [copy(4 + j, (*chip, c), sibling) for j, chip in enumerate(chips)]
        for j, chip in enumerate(chips):
            copy(1 + j, (*chip, c), (x, y, c)).wait_recv()
            passed[j].start()
        copy(0, (x, y, 1 - c), (x, y, c)).wait_recv()
        for j, chip in enumerate(chips):
            copy(4 + j, (*chip, 1 - c), (x, y, c)).wait_recv()
        for cp in first + passed:
            cp.wait_send()
        mine.wait()
        total = all_ref[pl.ds(0, rows), :]
        for d in range(1, N_DEV):
            total = total + all_ref[pl.ds(d * rows, rows), :]
        sum_ref[...] = total

    vm = pl.BlockSpec(memory_space=pltpu.VMEM)
    return _pcall(
        body,
        name="allreduce_small",
        in_specs=[vm],
        out_specs=vm,
        out_shape=jax.ShapeDtypeStruct((rows, LANES), F32),
        scratch_shapes=[pltpu.VMEM((N_DEV * rows, LANES), F32), pltpu.SemaphoreType.DMA((7,)),
                        pltpu.SemaphoreType.DMA((7,)), pltpu.SemaphoreType.DMA],
    )(v)


def _pad_cols(a, before, total):
    return jnp.pad(a, ((0, 0), (before, total - before - a.shape[1])))


def _layer_weights(cfg, w_in, w_uq, w_ukv):
    w = cfg.width
    qkv, f, cq, ckv, kr, qm, gates = jnp.split(w_in, list(_cumsum(cfg.in_splits))[:-1], axis=1)
    wa = jnp.concatenate([qkv, qm], axis=1)
    ws = jnp.concatenate([_pad_cols(f, 0, LANES), cq, ckv, _pad_cols(kr, MLA_NOPE, LANES)], axis=1)
    wq = jnp.pad(w_uq.reshape(cfg.q_rank, cfg.mla_h, MLA_NOPE + MLA_ROPE), ((0, 0), (0, 0), (0, LANES - MLA_NOPE - MLA_ROPE)))
    wq = wq.reshape(cfg.q_rank, cfg.mla_h * LANES)
    kv = w_ukv.reshape(cfg.kv_rank, cfg.mla_h, MLA_NOPE + MLA_V)
    wk = jnp.pad(kv[:, :, :MLA_NOPE], ((0, 0), (0, 0), (0, LANES - MLA_NOPE))).reshape(cfg.kv_rank, cfg.mla_h * LANES)
    wv = kv[:, :, MLA_NOPE:].reshape(cfg.kv_rank, cfg.mla_h * MLA_V)
    del w
    return wa, gates, ws, wq, wk, wv


def _cumsum(xs):
    out, t = [], 0
    for v in xs:
        t += v
        out.append(t)
    return out


def _layer_weight_grads(cfg, dwa, dwg, dws, dwq, dwk, dwv):
    w, qr, kvr = cfg.width, cfg.q_rank, cfg.kv_rank
    off_kr = LANES + qr + kvr + MLA_NOPE
    dw_in = jnp.concatenate([
        dwa[:, :3 * w], dws[:, :cfg.fox_h], dws[:, LANES:LANES + qr], dws[:, LANES + qr:LANES + qr + kvr],
        dws[:, off_kr:off_kr + MLA_ROPE], dwa[:, 3 * w:], dwg], axis=1)
    dw_uq = dwq.reshape(qr, cfg.mla_h, LANES)[:, :, :MLA_NOPE + MLA_ROPE].reshape(qr, cfg.mla_h * (MLA_NOPE + MLA_ROPE))
    dw_ukv = jnp.concatenate([dwk.reshape(kvr, cfg.mla_h, LANES)[:, :, :MLA_NOPE], dwv.reshape(kvr, cfg.mla_h, MLA_V)],
                             axis=2).reshape(kvr, cfg.mla_h * (MLA_NOPE + MLA_V))
    return dw_in, dw_uq, dw_ukv


def _attn_specs(cfg, batch):
    t = min(ATTN_TILE, cfg.seq)
    common = dict(batch=batch, sq=cfg.seq, chunk=cfg.chunk, tq=t)
    fox = Attn(sk=cfg.seq, groups=cfg.fox_h // 2, hq=2, hv=2, mode="fox", scale=FOX_DH ** -0.5, tk=t, **common)
    mla = Attn(sk=cfg.seq, groups=cfg.mla_h // 2, hq=1, hv=2, mode="chunk",
               scale=(MLA_NOPE + MLA_ROPE) ** -0.5, tk=t, **common)
    mem = Attn(sk=cfg.n_mem, groups=cfg.mem_h, hq=1, hv=1, mode="none", scale=MEM_DH ** -0.5, tk=cfg.n_mem, **common)
    return fox, mla, mem


def _small_core(cfg, ps, bf, gq, gkv):
    qr, kvr = cfg.q_rank, cfg.kv_rank
    z = ps[:, :LANES] + bf
    logf = jnp.minimum(z, 0.0) - jnp.log1p(jnp.exp(-jnp.abs(z)))
    nq = _rms(ps[:, LANES:LANES + qr], gq)
    nkv = _rms(ps[:, LANES + qr:LANES + qr + kvr], gkv)
    return logf, nq, nkv


def _layer_fwd(cfg, l, batch, h, hb, mem_b, rope_c, rope_s, lw, bf_pad, g_cq, g_ckv, ln1, ln2):
    wa, wg, ws, wq, wk, wv, wmkv, wbr, wout, wff1, wff2 = lw
    w, d = cfg.width, cfg.d
    fox, mla, mem = _attn_specs(cfg, batch)
    nw = w // LANES
    pa = _mm(f"proj_a_{l}", "nn", [(hb, wa)], [BF16])
    gl = _mm(f"proj_gates_{l}", "nn", [(hb, wg)], [F32])
    ps = _mm(f"proj_small_{l}", "nn", [(hb, ws)], [F32], tn=cfg.small_w)

    def small_fwd(ps_, c_, s_, bf_, gq_, gkv_):
        logf, nq, nkv = _small_core(cfg, ps_, bf_, gq_, gkv_)
        kpe = _rope(ps_[:, cfg.small_w - LANES:], c_, s_)
        return logf, nq, nkv, kpe

    logf, nq, nkv, kpe = _rowwise(
        f"small_fwd_{l}", small_fwd, [ps, rope_c, rope_s], [bf_pad, g_cq, g_ckv],
        [(LANES, F32), (cfg.q_rank, BF16), (cfg.kv_rank, BF16), (LANES, F32)])
    cfk, cfq = _forget_cumsum(f"cum_forget_{l}", logf, batch, cfg.seq, cfg.fox_h, fox.tq)

    qf = _mm(f"mla_q_{l}", "nn", [(nq, wq)], [BF16], tn=wq.shape[1],
             epi=lambda acc, c_, s_: (_rope(acc, c_, s_),), row_extras=[rope_c, rope_s])
    kf = _mm(f"mla_k_{l}", "nn", [(nkv, wk)], [BF16], tn=wk.shape[1],
             epi=lambda acc, kp: (acc + jnp.tile(kp, (1, cfg.mla_h)),), row_extras=[kpe])
    vb = _mm(f"mla_v_{l}", "nn", [(nkv, wv)], [BF16])
    mkv = _mm(f"mem_kv_{l}", "nn", [(mem_b, wmkv)], [BF16])

    o_a, lse_a = _attn_fwd(f"fox_fwd_{l}", fox, (pa, 0), (pa, nw), (pa, 2 * nw), cfq, cfk)
    o_b, lse_b = _attn_fwd(f"mla_fwd_{l}", mla, (qf, 0), (kf, 0), (vb, 0))
    o_c, lse_c = _attn_fwd(f"mem_fwd_{l}", mem, (pa, 3 * nw), (mkv, 0), (mkv, nw))
    bps = [_mm(f"branch_{n}_{l}", "nn", [(o, wbr[n])], [F32]) for n, o in enumerate((o_a, o_b, o_c))]

    def merge(gl_, b0, b1, b2):
        g = jax.nn.sigmoid(gl_)
        return (g[:, :d] * b0 + g[:, d:2 * d] * b1 + g[:, 2 * d:] * b2,)

    (merged,) = _rowwise(f"merge_{l}", merge, [gl] + bps, [], [(d, BF16)])

    def post_ln(acc, res, g_, b_):
        z = cfg.alpha * res + acc
        y = _ln(z, g_, b_)
        return z, y, y

    z1, h1, h1b = _mm(f"out_ln1_{l}", "nn", [(merged, wout)], [F32, F32, BF16], tm=256, tn=d,
                      epi=post_ln, row_extras=[h], bc_extras=list(ln1))
    u, a = _mm(f"ff1_{l}", "nn", [(h1b, wff1)], [BF16, BF16],
               epi=lambda acc: (acc, jnp.square(jnp.maximum(acc, 0.0))))
    z2, h2, h2b = _mm(f"ff2_ln2_{l}", "nn", [(a, wff2)], [F32, F32, BF16], tm=256, tn=d,
                      epi=post_ln, row_extras=[h1], bc_extras=list(ln2))
    saved = dict(hb=hb, pa=pa, gl=gl, ps=ps, nq=nq, nkv=nkv, cfq=cfq, cfk=cfk, qf=qf, kf=kf, vb=vb, mkv=mkv,
                 o=(o_a, o_b, o_c), lse=(lse_a, lse_b, lse_c), bps=bps, merged=merged, z1=z1, h1b=h1b, u=u, a=a, z2=z2)
    return h2, h2b, saved


def _ln_bwd(name, cfg, ga, gb, z, g, b):
    d = cfg.d

    def fn(*vals):
        if gb is None:
            ga_, z_, g_, b_ = vals
            dy = ga_
        else:
            ga_, gb_, z_, g_, b_ = vals
            dy = ga_ + cfg.alpha * gb_
        _, vjp = jax.vjp(_ln, z_, g_, b_)
        dz, dg, db = vjp(dy)
        return dz, dz, dg, db

    rows = [ga, z] if gb is None else [ga, gb, z]
    return _rowwise(name, fn, rows, [g, b], [(d, F32), (d, BF16)], accs=[(1, d), (1, d)])


def _layer_bwd(cfg, l, batch, ga, gb, sv, mem_b, rope_c, rope_s, lw, bf_pad, g_cq, g_ckv, ln1, ln2):
    wa, wg, ws, wq, wk, wv, wmkv, wbr, wout, wff1, wff2 = lw
    w, d = cfg.width, cfg.d
    fox, mla, mem = _attn_specs(cfg, batch)
    nw = w // LANES
    gdt = BF16

    dz2, dz2b, dg2, db2 = _ln_bwd(f"ln2_bwd_{l}", cfg, ga, gb, sv["z2"], *ln2)
    du = _mm(f"ff2_bwd_x_{l}", "nt", [(dz2b, wff2)], [BF16],
             epi=lambda acc, u_: (acc * (2.0 * jnp.maximum(u_.astype(F32), 0.0)),), row_extras=[sv["u"]])
    dwff2 = _mm(f"ff2_bwd_w_{l}", "tn", [(sv["a"], dz2b)], [gdt])
    dwff1 = _mm(f"ff1_bwd_w_{l}", "tn", [(sv["h1b"], du)], [gdt])
    dh1 = _mm(f"ff1_bwd_x_{l}", "nt", [(du, wff1)], [F32])
    dz1, dz1b, dg1, db1 = _ln_bwd(f"ln1_bwd_{l}", cfg, dh1, dz2, sv["z1"], *ln1)
    dmerged = _mm(f"out_bwd_x_{l}", "nt", [(dz1b, wout)], [F32])
    dwout = _mm(f"out_bwd_w_{l}", "tn", [(sv["merged"], dz1b)], [gdt])

    def merge_bwd(dm, gl_, b0, b1, b2):
        def f(gl__, b0_, b1_, b2_):
            g = jax.nn.sigmoid(gl__)
            return g[:, :d] * b0_ + g[:, d:2 * d] * b1_ + g[:, 2 * d:] * b2_

        _, vjp = jax.vjp(f, gl_, b0, b1, b2)
        return vjp(dm)

    dgl, db0, db1_, db2_ = _rowwise(f"merge_bwd_{l}", merge_bwd, [dmerged, sv["gl"]] + sv["bps"], [],
                                    [(3 * d, BF16), (d, BF16), (d, BF16), (d, BF16)])
    dbps = (db0, db1_, db2_)
    dos = [_mm(f"branch_bwd_x_{n}_{l}", "nt", [(dbps[n], wbr[n])], [BF16]) for n in range(3)]
    dwbr = [_mm(f"branch_bwd_w_{n}_{l}", "tn", [(sv["o"][n], dbps[n])], [gdt]) for n in range(3)]

    pa = sv["pa"]
    dq_a, dk_a, dv_a, dcfk = _attn_bwd(f"fox_bwd_{l}", fox, (pa, 0), (pa, nw), (pa, 2 * nw), sv["o"][0], sv["lse"][0],
                                       dos[0], sv["cfq"], sv["cfk"])
    dqf, dkf, dvb = _attn_bwd(f"mla_bwd_{l}", mla, (sv["qf"], 0), (sv["kf"], 0), (sv["vb"], 0), sv["o"][1],
                              sv["lse"][1], dos[1])
    dqm, dmk, dmv = _attn_bwd(f"mem_bwd_{l}", mem, (pa, 3 * nw), (sv["mkv"], 0), (sv["mkv"], nw), sv["o"][2],
                              sv["lse"][2], dos[2])
    dwmkv = _mm(f"mem_kv_bwd_w_{l}", "tn", [(mem_b, jnp.concatenate([dmk, dmv], axis=1))], [gdt])

    (dq_raw,) = _rowwise(f"mla_q_rope_bwd_{l}", lambda dy, c_, s_: (_rope_t(dy, c_, s_),), [dqf, rope_c, rope_s], [],
                         [(wq.shape[1], BF16)])
    dwq = _mm(f"mla_q_bwd_w_{l}", "tn", [(sv["nq"], dq_raw)], [gdt])
    dnq = _mm(f"mla_q_bwd_x_{l}", "nt", [(dq_raw, wq)], [F32])
    dwk = _mm(f"mla_k_bwd_w_{l}", "tn", [(sv["nkv"], dkf)], [gdt])
    dwv = _mm(f"mla_v_bwd_w_{l}", "tn", [(sv["nkv"], dvb)], [gdt])
    dnkv = _mm(f"mla_kv_bwd_x_{l}", "nt", [(dkf, wk), (dvb, wv)], [F32])

    dlogf = _forget_cumsum_bwd(f"cum_forget_bwd_{l}", dcfk, batch, cfg.seq, cfg.fox_h, fox.tq)

    def small_bwd(ps_, dlogf_, dnq_, dnkv_, dkf_, c_, s_, bf_, gq_, gkv_):
        _, vjp = jax.vjp(functools.partial(_small_core, cfg), ps_, bf_, gq_, gkv_)
        dps, dbf, dgq, dgkv = vjp((dlogf_, dnq_, dnkv_))
        dkpe = dkf_[:, :LANES].astype(F32)
        for hh in range(1, cfg.mla_h):
            dkpe = dkpe + dkf_[:, hh * LANES:(hh + 1) * LANES].astype(F32)
        lane = lax.broadcasted_iota(jnp.int32, (1, LANES), 1)
        dkpe = jnp.where((lane >= MLA_NOPE) & (lane < MLA_NOPE + MLA_ROPE), dkpe, 0.0)
        dkr = _rope_t(dkpe, c_, s_)
        dps = jnp.concatenate([dps[:, :cfg.small_w - LANES], dkr], axis=1)
        return dps, dbf, dgq, dgkv

    dps, dbf, dgq, dgkv = _rowwise(
        f"small_bwd_{l}", small_bwd, [sv["ps"], dlogf, dnq, dnkv, dkf, rope_c, rope_s], [bf_pad, g_cq, g_ckv],
        [(cfg.small_w, BF16)], accs=[(1, LANES), (1, cfg.q_rank), (1, cfg.kv_rank)])

    dpa = jnp.concatenate([dq_a.astype(BF16), dk_a, dv_a, dqm.astype(BF16)], axis=1)
    hb = sv["hb"]
    dh = _mm(f"proj_bwd_x_{l}", "nt", [(dpa, wa), (dgl, wg), (dps, ws)], [F32], tn=d)
    dwa = _mm(f"proj_a_bwd_w_{l}", "tn", [(hb, dpa)], [gdt])
    dwg = _mm(f"proj_gates_bwd_w_{l}", "tn", [(hb, dgl)], [gdt])
    dws = _mm(f"proj_small_bwd_w_{l}", "tn", [(hb, dps)], [gdt], tn=cfg.small_w)
    dw_in, dw_uq, dw_ukv = _layer_weight_grads(cfg, dwa, dwg, dws, dwq, dwk, dwv)
    big = dict(w_in=dw_in, w_uq=dw_uq, w_ukv=dw_ukv, w_mem_kv=dwmkv, w_br=jnp.stack(dwbr), w_out=dwout,
               w_ff1=dwff1, w_ff2=dwff2)
    small = dict(b_forget=dbf[0, :cfg.fox_h], g_cq=dgq[0], g_ckv=dgkv[0], ln1_g=dg1[0], ln1_b=db1[0],
                 ln2_g=dg2[0], ln2_b=db2[0])
    return dh, dz1, big, small


def _rope_tables(positions):
    inv_freq = ROPE_BASE ** (-jnp.arange(0, MLA_ROPE, 2, dtype=F32) / MLA_ROPE)
    ang = positions.astype(F32).reshape(-1)[:, None] * inv_freq
    cos, sin = jnp.cos(ang), jnp.sin(ang)
    t = ang.shape[0]
    rope_c = jnp.concatenate([jnp.ones((t, MLA_NOPE), F32), cos, cos, jnp.zeros((t, LANES - MLA_NOPE - MLA_ROPE), F32)], axis=1)
    rope_s = jnp.concatenate([jnp.zeros((t, MLA_NOPE), F32), -sin, sin, jnp.zeros((t, LANES - MLA_NOPE - MLA_ROPE), F32)], axis=1)
    return rope_c, rope_s


def _local_step(cfg, x, mem, positions, target, small_w, big_w):
    batch = x.shape[0]
    d, depth = cfg.d, cfg.depth
    t = batch * cfg.seq
    x2, tgt = x.reshape(t, d), target.reshape(t, d)
    mem_b = mem.reshape(batch * cfg.n_mem, d).astype(BF16)
    rope_c, rope_s = _rope_tables(positions)
    row = lambda v: v.reshape(1, -1)
    ln_in = (row(small_w["ln_in_g"]), row(small_w["ln_in_b"]))

    h, hb = _rowwise("ln_in", lambda x_, g_, b_: (_ln(x_, g_, b_),) * 2, [x2], list(ln_in), [(d, F32), (d, BF16)])
    layers, saves = [], []
    for l in range(depth):
        lw = _layer_weights(cfg, big_w["w_in"][l], big_w["w_uq"][l], big_w["w_ukv"][l]) + (
            big_w["w_mem_kv"][l], big_w["w_br"][l], big_w["w_out"][l], big_w["w_ff1"][l], big_w["w_ff2"][l])
        par = dict(
            lw=lw, bf_pad=jnp.pad(row(small_w["b_forget"][l]), ((0, 0), (0, LANES - cfg.fox_h))),
            g_cq=row(small_w["g_cq"][l]), g_ckv=row(small_w["g_ckv"][l]),
            ln1=(row(small_w["ln1_g"][l]), row(small_w["ln1_b"][l])),
            ln2=(row(small_w["ln2_g"][l]), row(small_w["ln2_b"][l])))
        layers.append(par)
        h, hb, sv = _layer_fwd(cfg, l, batch, h, hb, mem_b, rope_c, rope_s, **par)
        saves.append(sv)

    def loss_fn(y, tg):
        err = y - tg
        part = 0.5 * jnp.sum(jnp.mean(err * err, axis=-1, keepdims=True), axis=0, keepdims=True)
        return err * (1.0 / d), jnp.broadcast_to(part, (1, LANES))

    ga, loss_acc = _rowwise("loss", loss_fn, [h, tgt], [], [(d, F32)], accs=[(1, LANES)])
    gb = None
    big_g = {k: [None] * depth for k in ("w_in", "w_uq", "w_ukv", "w_mem_kv", "w_br", "w_out", "w_ff1", "w_ff2")}
    small_g = {k: [None] * depth for k in ("b_forget", "g_cq", "g_ckv", "ln1_g", "ln1_b", "ln2_g", "ln2_b")}
    for l in reversed(range(depth)):
        ga, gb, big, small = _layer_bwd(cfg, l, batch, ga, gb, saves[l], mem_b, rope_c, rope_s, **layers[l])
        for k, v in big.items():
            big_g[k][l] = v
        for k, v in small.items():
            small_g[k][l] = v
    dx, _, dg_in, db_in = _ln_bwd("ln_in_bwd", cfg, ga, gb, x2, *ln_in)
    big_g = {k: jnp.stack(v) for k, v in big_g.items()}
    small_g = {k: jnp.stack(v) for k, v in small_g.items()}
    small_g["ln_in_g"], small_g["ln_in_b"] = dg_in[0], db_in[0]
    return loss_acc[0, 0], dx.reshape(x.shape), big_g, small_g


BIG = ("w_in", "w_uq", "w_ukv", "w_mem_kv", "w_br", "w_out", "w_ff1", "w_ff2")
SMALL = ("ln_in_g", "ln_in_b", "b_forget", "g_cq", "g_ckv", "ln1_g", "ln1_b", "ln2_g", "ln2_b")
ROW_CUT = ("w_mem_kv", "w_out", "w_ff2")


def _shard_2d(a):
    cols = a.shape[-1]
    rows = a.size // cols
    return a.reshape(2, rows // 2, cols)


def _full_from_slots(name, slots, shard_shape):
    parts = slots.reshape((N_CHIPS,) + shard_shape)
    axis = len(shard_shape) - (2 if name in ROW_CUT else 1)
    return jnp.concatenate([parts[i] for i in range(N_CHIPS)], axis=axis)


def _slots_from_full(name, full, shard_shape):
    axis = len(shard_shape) - (2 if name in ROW_CUT else 1)
    parts = jnp.stack(jnp.split(full, N_CHIPS, axis=axis))
    cols = shard_shape[-1]
    return parts.reshape(N_CHIPS, 2, -1, cols)


def _pack_small(cfg, vals):
    flat = jnp.concatenate([vals[k].reshape(-1).astype(F32) for k in SMALL])
    pad = (-flat.shape[0]) % (LANES * LANES)
    return jnp.pad(flat, (0, pad)).reshape(-1, LANES)


def _unpack_small(packed, like):
    flat, out, off = packed.reshape(-1), {}, 0
    for k in SMALL:
        n = like[k].size
        out[k] = flat[off:off + n].reshape(like[k].shape)
        off += n
    return out


def _step(cfg, x, mem, positions, target, w, m, v):
    shards = [_shard_2d(w[k].astype(BF16)) for k in BIG]
    gathered = _gather_weights(shards)
    big_w = {k: _full_from_slots(k, _to_absolute(g, s), w[k].shape) for k, g, s in zip(BIG, gathered, shards)}
    small_w = {k: w[k] for k in SMALL}
    loss_local, dx, big_g, small_g = _local_step(cfg, x, mem, positions, target, small_w, big_w)
    loss = lax.psum(loss_local, ("x", "y", "c"))

    grads_rel = [_to_relative(_slots_from_full(k, big_g[k], w[k].shape)) for k in BIG]
    reduced = _reduce_grads(grads_rel)
    outs_big = {}
    for k, (mine, theirs) in zip(BIG, reduced):
        halves = (2,) + mine.shape[1:]
        res = _adamw_shard(f"adamw_{k}", *[a.reshape(halves) for a in (w[k], m[k], v[k])], mine, theirs)
        outs_big[k] = tuple(a.reshape(w[k].shape) for a in res)

    g_small = _allreduce_small(_pack_small(cfg, small_g))
    packs = [_pack_small(cfg, {k: d_[k] for k in SMALL}) for d_ in (w, m, v)]
    dl, nm, nv = _slabwise("adamw_small", _adamw_math, [a[None] for a in (packs[0], g_small, packs[1], packs[2])],
                           [F32, F32, F32])
    outs_small = [_unpack_small(a[0] if a.ndim == 3 else a, w) for a in (g_small, dl, nm, nv)]

    names = SMALL[:2] + ("w_in", "b_forget", "w_uq", "g_cq", "w_ukv", "g_ckv", "w_mem_kv", "w_br", "w_out",
                         "ln1_g", "ln1_b", "w_ff1", "w_ff2", "ln2_g", "ln2_b")
    result = [loss, dx]
    for part in range(4):
        for k in names:
            result.append(outs_big[k][part] if k in outs_big else outs_small[part][k])
    return tuple(result)


def kernel(x, mem, positions, ln_in_g, ln_in_b, w_in, b_forget, w_uq, g_cq, w_ukv, g_ckv, w_mem_kv, w_br, w_out, ln1_g, ln1_b, w_ff1, w_ff2, ln2_g, ln2_b, loss_target, m_ln_in_g, m_ln_in_b, m_w_in, m_b_forget, m_w_uq, m_g_cq, m_w_ukv, m_g_ckv, m_w_mem_kv, m_w_br, m_w_out, m_ln1_g, m_ln1_b, m_w_ff1, m_w_ff2, m_ln2_g, m_ln2_b, v_ln_in_g, v_ln_in_b, v_w_in, v_b_forget, v_w_uq, v_g_cq, v_w_ukv, v_g_ckv, v_w_mem_kv, v_w_br, v_w_out, v_ln1_g, v_ln1_b, v_w_ff1, v_w_ff2, v_ln2_g, v_ln2_b):
    w = dict(ln_in_g=ln_in_g, ln_in_b=ln_in_b, w_in=w_in, b_forget=b_forget, w_uq=w_uq, g_cq=g_cq, w_ukv=w_ukv,
             g_ckv=g_ckv, w_mem_kv=w_mem_kv, w_br=w_br, w_out=w_out, ln1_g=ln1_g, ln1_b=ln1_b, w_ff1=w_ff1,
             w_ff2=w_ff2, ln2_g=ln2_g, ln2_b=ln2_b)
    m = dict(ln_in_g=m_ln_in_g, ln_in_b=m_ln_in_b, w_in=m_w_in, b_forget=m_b_forget, w_uq=m_w_uq, g_cq=m_g_cq,
             w_ukv=m_w_ukv, g_ckv=m_g_ckv, w_mem_kv=m_w_mem_kv, w_br=m_w_br, w_out=m_w_out, ln1_g=m_ln1_g,
             ln1_b=m_ln1_b, w_ff1=m_w_ff1, w_ff2=m_w_ff2, ln2_g=m_ln2_g, ln2_b=m_ln2_b)
    v = dict(ln_in_g=v_ln_in_g, ln_in_b=v_ln_in_b, w_in=v_w_in, b_forget=v_b_forget, w_uq=v_w_uq, g_cq=v_g_cq,
             w_ukv=v_w_ukv, g_ckv=v_g_ckv, w_mem_kv=v_w_mem_kv, w_br=v_w_br, w_out=v_w_out, ln1_g=v_ln1_g,
             ln1_b=v_ln1_b, w_ff1=v_w_ff1, w_ff2=v_w_ff2, ln2_g=v_ln2_g, ln2_b=v_ln2_b)
    return _step(Cfg(), x, mem, positions, loss_target, w, m, v)
```

```python
import functools
from typing import NamedTuple

import jax
import jax.numpy as jnp
from jax import lax
from jax.experimental import pallas as pl
from jax.experimental.pallas import tpu as pltpu

F32 = jnp.float32
BF16 = jnp.bfloat16
MESH = pl.DeviceIdType.MESH

LANES = 128
SUBLANES = 8
VMEM_BYTES = 64 * 1024 * 1024
N_CHIPS = 4
N_DEV = 8

FOX_DH = 64
MLA_NOPE = 64
MLA_ROPE = 32
MLA_V = 64
MEM_DH = 128
ROPE_BASE = 10000.0
LN_EPS = 1e-5
RMS_EPS = 1e-6
NEG_INF = -1e30
ATTN_TILE = 512

ADAM_LR = 0.001
ADAM_B1 = 0.9
ADAM_B2 = 0.999
ADAM_EPS = 1e-08
ADAM_WD = 0.01
ADAM_STEP = 10


class Cfg(NamedTuple):
    d: int = 1024
    depth: int = 4
    seq: int = 2048
    chunk: int = 64
    n_mem: int = 256
    fox_h: int = 8
    mla_h: int = 8
    q_rank: int = 384
    kv_rank: int = 256
    mem_h: int = 4
    d_ff: int = 4096

    @property
    def width(self):
        return self.fox_h * FOX_DH

    @property
    def alpha(self):
        return (2 * self.depth) ** 0.25

    @property
    def small_w(self):
        return LANES + self.q_rank + self.kv_rank + LANES

    @property
    def in_splits(self):
        return (3 * self.width, self.fox_h, self.q_rank, self.kv_rank, MLA_ROPE, self.width, 3 * self.d)


class Exchange(NamedTuple):
    ins: tuple
    out_shapes: tuple
    plan: object
    n_copies: int
    aliases: dict = {}


def _peer(rel):
    x, y, c = lax.axis_index("x"), lax.axis_index("y"), lax.axis_index("c")
    if rel == "c":
        return (x, y, 1 - c)
    return ((1 - x) if rel in (1, 3) else x, (1 - y) if rel in (2, 3) else y, c)


def _exchange_copies(ex, in_refs, out_refs, send_sems, recv_sems):
    planned = ex.plan(in_refs, out_refs)
    assert len(planned) == ex.n_copies, len(planned)
    return [pltpu.make_async_remote_copy(src_ref=src, dst_ref=dst, send_sem=send_sems.at[i], recv_sem=recv_sems.at[i],
                                         device_id=_peer(rel), device_id_type=MESH)
            for i, (src, dst, rel) in enumerate(planned)]


def _pcall(body, rider=None, **kw):
    if rider is not None:
        n_in, n_out, grid = len(kw["in_specs"]), len(kw["out_specs"]), kw["grid"]
        n_rin, n_rout = len(rider.ins), len(rider.out_shapes)
        host = body

        def body(*refs):
            ins, rins = refs[:n_in], refs[n_in:n_in + n_rin]
            outs = refs[n_in + n_rin:n_in + n_rin + n_out]
            routs = refs[n_in + n_rin + n_out:n_in + n_rin + n_out + n_rout]
            scratch = refs[n_in + n_rin + n_out + n_rout:-2]
            copies = _exchange_copies(rider, rins, routs, refs[-2], refs[-1])
            first = functools.reduce(jnp.logical_and, [pl.program_id(a) == 0 for a in range(len(grid))])
            last = functools.reduce(jnp.logical_and, [pl.program_id(a) == n - 1 for a, n in enumerate(grid)])

            @pl.when(first)
            def _():
                for cp in copies:
                    cp.start()

            host(*ins, *outs, *scratch)

            @pl.when(last)
            def _():
                for cp in copies:
                    cp.wait()

        any_spec = pl.BlockSpec(memory_space=pl.ANY)
        sems = [pltpu.SemaphoreType.DMA((rider.n_copies,))] * 2
        kw = dict(
            kw,
            in_specs=list(kw["in_specs"]) + [any_spec] * n_rin,
            out_specs=list(kw["out_specs"]) + [any_spec] * n_rout,
            out_shape=list(kw["out_shape"]) + list(rider.out_shapes),
            scratch_shapes=list(kw.get("scratch_shapes", ())) + sems,
            input_output_aliases={**kw.get("input_output_aliases", {}),
                                  **{n_in + i: n_out + o for i, o in rider.aliases.items()}},
        )
    call = pl.pallas_call(body, **kw)
    return lambda *ops: call(*[pltpu.with_memory_space_constraint(o, pltpu.HBM) for o in ops])


class Hosts:
    def rider(self, host):
        return None

    def done(self, host, outs):
        pass


def _hosted(hosts, host, call, ops, n_results):
    rider = hosts.rider(host) if hosts is not None else None
    if rider is None:
        return call(None)(*ops)
    res = call(rider)(*ops, *rider.ins)
    hosts.done(host, list(res[n_results:]))
    return res[:n_results]


def _nbytes(shape, dtype):
    n = 1
    for s in shape:
        n *= s
    return n * jnp.dtype(dtype).itemsize


def _tile(dim, target):
    if dim <= target:
        return dim
    t = target - target % LANES
    while t >= LANES:
        if dim % t == 0:
            return t
        t -= LANES
    return dim


def _params(block_bytes, scratch_bytes=0):
    est = 2 * block_bytes + scratch_bytes + 24 * 1024 * 1024
    return pltpu.CompilerParams(vmem_limit_bytes=int(min(max(est, 32 * 1024 * 1024), VMEM_BYTES - 4 * 1024 * 1024)))


_DIMS = {"nn": (((1,), (0,)), ((), ())), "nt": (((1,), (1,)), ((), ())), "tn": (((0,), (0,)), ((), ()))}


MM_TILE = 1024
MM_BLOCK_BYTES = 16 * 1024 * 1024


def _mm_tiles(mode, pairs, out_dtypes, m, n, tm, tn):
    fixed_m, fixed_n = tm is not None, tn is not None
    tm, tn = _tile(m, tm or MM_TILE), _tile(n, tn or MM_TILE)

    def block_bytes(tm_, tn_):
        total = sum(_nbytes((tm_, tn_), dt) for dt in out_dtypes)
        for a, b in pairs:
            k = a.shape[0] if mode == "tn" else a.shape[1]
            total += _nbytes((k, tm_), a.dtype) + _nbytes((k, tn_), b.dtype)
        return total

    while block_bytes(tm, tn) > MM_BLOCK_BYTES:
        if not fixed_m and tm >= tn and tm > 2 * LANES:
            tm = _tile(m, tm // 2)
        elif not fixed_n and tn > 2 * LANES:
            tn = _tile(n, tn // 2)
        elif not fixed_m and tm > 2 * LANES:
            tm = _tile(m, tm // 2)
        else:
            break
    return tm, tn


def _mm(name, mode, pairs, out_dtypes, tm=None, tn=None, epi=None, row_extras=(), bc_extras=(), hosts=None, host=None):
    a0, b0 = pairs[0]
    m = a0.shape[1] if mode == "tn" else a0.shape[0]
    n = b0.shape[0] if mode == "nt" else b0.shape[1]
    tm, tn = _mm_tiles(mode, pairs, out_dtypes, m, n, tm, tn)
    in_specs, ops, blk = [], [], 0
    for a, b in pairs:
        if mode == "tn":
            k = a.shape[0]
            sa, sha = pl.BlockSpec((k, tm), lambda i, j: (0, i)), (k, tm)
        else:
            k = a.shape[1]
            sa, sha = pl.BlockSpec((tm, k), lambda i, j: (i, 0)), (tm, k)
        if mode == "nt":
            sb, shb = pl.BlockSpec((tn, k), lambda i, j: (j, 0)), (tn, k)
        else:
            sb, shb = pl.BlockSpec((k, tn), lambda i, j: (0, j)), (k, tn)
        in_specs += [sa, sb]
        ops += [a, b]
        blk += _nbytes(sha, a.dtype) + _nbytes(shb, b.dtype)
    for e in row_extras:
        w = e.shape[1]
        if w == n:
            in_specs.append(pl.BlockSpec((tm, tn), lambda i, j: (i, j)))
            blk += _nbytes((tm, tn), e.dtype)
        else:
            in_specs.append(pl.BlockSpec((tm, w), lambda i, j: (i, 0)))
            blk += _nbytes((tm, w), e.dtype)
        ops.append(e)
    for e in bc_extras:
        r, w = e.shape
        if w == n:
            in_specs.append(pl.BlockSpec((r, tn), lambda i, j: (0, j)))
        else:
            in_specs.append(pl.BlockSpec((r, w), lambda i, j: (0, 0)))
        blk += _nbytes((r, w), e.dtype)
        ops.append(e)
    npairs, nrow, nbc, nout = len(pairs), len(row_extras), len(bc_extras), len(out_dtypes)
    dims = _DIMS[mode]

    def body(*refs):
        acc = None
        for p in range(npairs):
            a = refs[2 * p][...].astype(BF16)
            b = refs[2 * p + 1][...].astype(BF16)
            d = lax.dot_general(a, b, dims, preferred_element_type=F32)
            acc = d if acc is None else acc + d
        ex = [r[...] for r in refs[2 * npairs:2 * npairs + nrow + nbc]]
        outs = (acc,) if epi is None else epi(acc, *ex)
        for o_ref, o in zip(refs[2 * npairs + nrow + nbc:], outs):
            o_ref[...] = o.astype(o_ref.dtype)

    blk += sum(_nbytes((tm, tn), dt) for dt in out_dtypes) + 2 * _nbytes((tm, tn), F32)
    call = lambda rider: _pcall(
        body,
        rider=rider,
        name=name,
        grid=(m // tm, n // tn),
        in_specs=in_specs,
        out_specs=[pl.BlockSpec((tm, tn), lambda i, j: (i, j)) for _ in range(nout)],
        out_shape=[jax.ShapeDtypeStruct((m, n), dt) for dt in out_dtypes],
        compiler_params=_params(blk),
    )
    res = _hosted(hosts, host, call, ops, nout)
    return res[0] if nout == 1 else res


def _rowwise(name, fn, row_ins, bc_ins, outs, accs=(), tm=256):
    rows = row_ins[0].shape[0]
    tm = min(tm, rows)
    assert rows % tm == 0
    nrow, nbc, nout, nacc = len(row_ins), len(bc_ins), len(outs), len(accs)
    in_specs = [pl.BlockSpec((tm, a.shape[1]), lambda i: (i, 0)) for a in row_ins]
    in_specs += [pl.BlockSpec(a.shape, lambda i: (0, 0)) for a in bc_ins]
    out_specs = [pl.BlockSpec((tm, w), lambda i: (i, 0)) for w, _ in outs]
    out_specs += [pl.BlockSpec(s, lambda i: (0, 0)) for s in accs]
    out_shape = [jax.ShapeDtypeStruct((rows, w), dt) for w, dt in outs]
    out_shape += [jax.ShapeDtypeStruct(s, F32) for s in accs]

    def body(*refs):
        vals = fn(*[r[...] for r in refs[:nrow + nbc]])
        o_refs = refs[nrow + nbc:]
        for r, v in zip(o_refs[:nout], vals[:nout]):
            r[...] = v.astype(r.dtype)
        if nacc:
            @pl.when(pl.program_id(0) == 0)
            def _():
                for r in o_refs[nout:]:
                    r[...] = jnp.zeros(r.shape, F32)

            for r, v in zip(o_refs[nout:], vals[nout:]):
                r[...] += v

    blk = sum(_nbytes((tm, a.shape[1]), a.dtype) for a in row_ins) + sum(_nbytes(a.shape, a.dtype) for a in bc_ins)
    blk += sum(_nbytes((tm, w), dt) for w, dt in outs) + sum(_nbytes(s, F32) for s in accs)
    res = _pcall(
        body,
        name=name,
        grid=(rows // tm,),
        in_specs=in_specs,
        out_specs=out_specs,
        out_shape=out_shape,
        compiler_params=_params(2 * blk),
    )(*row_ins, *bc_ins)
    return res


def _ln(z, g, b):
    mu = jnp.mean(z, axis=-1, keepdims=True)
    zc = z - mu
    var = jnp.mean(zc * zc, axis=-1, keepdims=True)
    return zc * lax.rsqrt(var + LN_EPS) * g + b


def _rms(x, g):
    return x * lax.rsqrt(jnp.mean(x * x, axis=-1, keepdims=True) + RMS_EPS) * g


def _colsum(v):
    return jnp.sum(v, axis=0, keepdims=True)


def _rope_swap(x):
    w = x.shape[1]
    lane = lax.broadcasted_iota(jnp.int32, (1, w), 1) % LANES
    from_left = pltpu.roll(x, 16, 1)
    from_right = pltpu.roll(x, w - 16, 1)
    lo = (lane >= MLA_NOPE) & (lane < MLA_NOPE + 16)
    hi = (lane >= MLA_NOPE + 16) & (lane < MLA_NOPE + 32)
    return jnp.where(hi, from_left, jnp.where(lo, from_right, 0.0))


def _rope(x, cos_t, sin_t):
    nh = x.shape[1] // LANES
    ct, st = jnp.tile(cos_t, (1, nh)), jnp.tile(sin_t, (1, nh))
    return x * ct + _rope_swap(x) * st


def _rope_t(dy, cos_t, sin_t):
    nh = dy.shape[1] // LANES
    ct, st = jnp.tile(cos_t, (1, nh)), jnp.tile(sin_t, (1, nh))
    return dy * ct + _rope_swap(dy * st)


def _block_cumsum(v, carry, reverse):
    tb = v.shape[0]
    r = lax.broadcasted_iota(jnp.int32, (tb, tb), 0)
    c = lax.broadcasted_iota(jnp.int32, (tb, tb), 1)
    tri = jnp.where((c >= r) if reverse else (c <= r), 1.0, 0.0).astype(BF16)
    hi = v.astype(BF16)
    r1 = v - hi.astype(F32)
    mid = r1.astype(BF16)
    lo = (r1 - mid.astype(F32)).astype(BF16)
    out = carry + sum(jnp.dot(tri, p, preferred_element_type=F32) for p in (hi, mid, lo))
    return out, (out[0:1, :] if reverse else out[tb - 1:tb, :])


def _forget_cumsum(name, logf, batch, seq, heads, tb):
    nb = seq // tb

    def body(x_ref, keys_ref, rows_ref, carry):
        @pl.when(pl.program_id(1) == 0)
        def _():
            carry[...] = jnp.zeros(carry.shape, F32)

        out, carry[...] = _block_cumsum(x_ref[...], carry[...], False)
        keys_ref[...] = jnp.concatenate([jnp.broadcast_to(out[:, h:h + 1], (tb, LANES)) for h in range(heads)], axis=1)
        out_t = out.T
        for g in range(heads // 2):
            rows_ref[g] = out_t[2 * g:2 * g + 2, :]

    return _pcall(
        body,
        name=name,
        grid=(batch, nb),
        in_specs=[pl.BlockSpec((tb, LANES), lambda b, i: (b * nb + i, 0))],
        out_specs=[pl.BlockSpec((tb, heads * LANES), lambda b, i: (b * nb + i, 0)),
                   pl.BlockSpec((None, None, heads // 2, 2, tb), lambda b, i: (b, i, 0, 0, 0))],
        out_shape=[jax.ShapeDtypeStruct((batch * seq, heads * LANES), F32),
                   jax.ShapeDtypeStruct((batch, nb, heads // 2, 2, tb), F32)],
        scratch_shapes=[pltpu.VMEM((1, LANES), F32)],
        compiler_params=_params(4 * tb * (heads + 2) * LANES * 4),
    )(logf)


def _forget_cumsum_bwd(name, dcf, batch, seq, heads, tb):
    nb = seq // tb

    def body(x_ref, o_ref, carry):
        @pl.when(pl.program_id(1) == 0)
        def _():
            carry[...] = jnp.zeros(carry.shape, F32)

        lane = lax.broadcasted_iota(jnp.int32, (1, LANES), 1)
        v = jnp.zeros((tb, LANES), F32)
        for g in range(heads // 2):
            blk = x_ref[:, g * LANES:(g + 1) * LANES]
            moved = pltpu.roll(blk, 2 * g, 1) if g else blk
            v = v + jnp.where((lane >= 2 * g) & (lane < 2 * g + 2), moved, 0.0)
        o_ref[...], carry[...] = _block_cumsum(v, carry[...], True)

    return _pcall(
        body,
        name=name,
        grid=(batch, nb),
        in_specs=[pl.BlockSpec((tb, (heads // 2) * LANES), lambda b, i: (b * nb + nb - 1 - i, 0))],
        out_specs=pl.BlockSpec((tb, LANES), lambda b, i: (b * nb + nb - 1 - i, 0)),
        out_shape=jax.ShapeDtypeStruct((batch * seq, LANES), F32),
        scratch_shapes=[pltpu.VMEM((1, LANES), F32)],
        compiler_params=_params(4 * tb * (heads // 2 + 1) * LANES * 4),
    )(dcf)


class Attn(NamedTuple):
    batch: int
    sq: int
    sk: int
    groups: int
    hq: int
    hv: int
    mode: str
    scale: float
    chunk: int
    tq: int
    tk: int

    @property
    def hg(self):
        return self.hv

    @property
    def qw(self):
        return LANES * self.hg // self.hq

    @property
    def dv(self):
        return LANES // self.hv


def _head_lanes(j, dv):
    lane = lax.broadcasted_iota(jnp.int32, (1, LANES), 1)
    return (lane >= j * dv) & (lane < (j + 1) * dv)


def _head_q(sp, j, q_blk):
    if sp.hq == 2:
        return jnp.where(_head_lanes(j, FOX_DH), q_blk, jnp.zeros_like(q_blk))
    return q_blk[:, LANES * j:LANES * (j + 1)]


def _head_rows(sp, j):
    return slice(j * sp.dv, (j + 1) * sp.dv) if sp.hg == 2 else slice(None)


def _scores_t(sp, j, k_c, q_j, cfq_rows, cfk_rep, k0, q0, masked):
    tk, tq = k_c.shape[0], q_j.shape[0]
    k_j = k_c if sp.hq == 2 else k_c[:, LANES * j:LANES * (j + 1)]
    st = lax.dot_general(k_j, q_j, _DIMS["nt"], preferred_element_type=F32) * sp.scale
    if sp.mode == "fox":
        st = st + (cfq_rows[j:j + 1, :] - jnp.tile(cfk_rep[:, LANES * j:LANES * (j + 1)], (1, tq // LANES)))
    if masked:
        kidx = k0 + lax.broadcasted_iota(jnp.int32, (tk, tq), 0)
        qidx = q0 + lax.broadcasted_iota(jnp.int32, (tk, tq), 1)
        if sp.mode == "chunk":
            shift = sp.chunk.bit_length() - 1
            kidx, qidx = jnp.right_shift(kidx, shift), jnp.right_shift(qidx, shift)
        st = jnp.where(kidx <= qidx, st, NEG_INF)
    return st


def _attn_fwd(name, sp, q, k, v, cfq=None, cfk=None, hosts=None, host=None):
    (qa, qo), (ka, ko), (va, vo) = q, k, v
    tq, tk, hg, qw = sp.tq, sp.tk, sp.hg, sp.qw
    nqb, nkc = sp.sq // tq, sp.sk // tk
    fox, causal = sp.mode == "fox", sp.mode != "none"
    assert sp.sq % tq == 0 and sp.sk % tk == 0 and (not causal or (tq == tk and sp.sq == sp.sk))

    def body(*refs):
        if fox:
            q_ref, k_ref, v_ref, cfq_ref, cfk_ref, o_ref, lse_ref, acc_scr = refs
        else:
            q_ref, k_ref, v_ref, o_ref, lse_ref, acc_scr = refs
        i = pl.program_id(2)
        q0 = i * tq
        q_blk = q_ref[...]
        qs = [_head_q(sp, j, q_blk) for j in range(hg)]
        acc_scr[...] = jnp.zeros(acc_scr.shape, F32)

        def chunk(kc, carry, masked):
            ms, ls = carry
            k0 = pl.multiple_of(kc * tk, tk)
            k_c = k_ref[pl.ds(k0, tk), :]
            v_c = v_ref[pl.ds(k0, tk), :]
            new_m, new_l = [], []
            for j in range(hg):
                st = _scores_t(sp, j, k_c, qs[j], cfq_ref[...] if fox else None,
                               cfk_ref[pl.ds(k0, tk), :] if fox else None, k0, q0, masked)
                m_new = jnp.maximum(ms[j], jnp.max(st, axis=0, keepdims=True))
                alpha = jnp.exp(ms[j] - m_new)
                pt = jnp.exp(st - m_new)
                new_m.append(m_new)
                new_l.append(alpha * ls[j] + jnp.sum(pt, axis=0, keepdims=True))
                pv = lax.dot_general(v_c, pt.astype(BF16), _DIMS["tn"], preferred_element_type=F32)
                r = _head_rows(sp, j)
                acc_scr[r, :] = acc_scr[r, :] * alpha + pv[r, :]
            return tuple(new_m), tuple(new_l)

        carry = (tuple(jnp.full((1, tq), NEG_INF, F32) for _ in range(hg)),
                 tuple(jnp.zeros((1, tq), F32) for _ in range(hg)))
        if causal:
            carry = lax.fori_loop(0, i, functools.partial(chunk, masked=False), carry)
            ms, ls = chunk(i, carry, True)
        else:
            ms, ls = lax.fori_loop(0, nkc, functools.partial(chunk, masked=False), carry)
        for j in range(hg):
            r = _head_rows(sp, j)
            acc_scr[r, :] = acc_scr[r, :] / ls[j]
            lse_ref[j:j + 1, :] = ms[j] + jnp.log(ls[j])
        o_ref[...] = acc_scr[...].T

    in_specs = [
        pl.BlockSpec((tq, qw), lambda b, g, i: (b * nqb + i, qo + g)),
        pl.BlockSpec((sp.sk, qw), lambda b, g, i: (b, ko + g)),
        pl.BlockSpec((sp.sk, LANES), lambda b, g, i: (b, vo + g)),
    ]
    ops = [qa, ka, va]
    stat_blk = pl.BlockSpec((None, None, None, hg, tq), lambda b, g, i: (b, g, i, 0, 0))
    if fox:
        in_specs += [pl.BlockSpec((None, None, None, hg, tq), lambda b, g, i: (b, i, g, 0, 0)),
                     pl.BlockSpec((sp.sk, hg * LANES), lambda b, g, i: (b, g))]
        ops += [cfq, cfk]
    blk = _nbytes((tq, qw), BF16) + _nbytes((sp.sk, qw + LANES), BF16) + 2 * _nbytes((tq, LANES), F32)
    blk += _nbytes((sp.sk, hg * LANES), F32) + 6 * _nbytes((tk, tq), F32)
    call = lambda rider: _pcall(
        body,
        rider=rider,
        name=name,
        grid=(sp.batch, sp.groups, nqb),
        in_specs=in_specs,
        out_specs=[pl.BlockSpec((tq, LANES), lambda b, g, i: (b * nqb + i, g)), stat_blk],
        out_shape=[
            jax.ShapeDtypeStruct((sp.batch * sp.sq, sp.groups * LANES), F32),
            jax.ShapeDtypeStruct((sp.batch, sp.groups, nqb, hg, tq), F32),
        ],
        scratch_shapes=[pltpu.VMEM((LANES, tq), F32)],
        compiler_params=_params(blk, tq * LANES * 4),
    )
    return _hosted(hosts, host, call, ops, 2)


def _attn_bwd(name, sp, q, k, v, o, lse, do, cfq=None, cfk=None, hosts=None, host=None):
    (qa, qo), (ka, ko), (va, vo) = q, k, v
    tq, tk, hg, qw, dv = sp.tq, sp.tk, sp.hg, sp.qw, sp.dv
    nqb, nkb = sp.sq // tq, sp.sk // tk
    fox, causal = sp.mode == "fox", sp.mode != "none"
    assert sp.sq % tq == 0 and sp.sk % tk == 0 and (not causal or (tq == tk and sp.sq == sp.sk))

    def body(*refs):
        if fox:
            (q_ref, k_ref, v_ref, lse_ref, do_ref, cfq_ref, cfk_ref, kall_ref, vall_ref, cfkall_ref,
             dq_ref, dk_ref, dv_ref, dcf_ref, delta_scr, dk_scr, dv_scr, dqt_scr, dcf_scr) = refs
        else:
            (q_ref, k_ref, v_ref, o_ref, lse_ref, do_ref,
             dq_ref, dk_ref, dv_ref, delta_scr, dk_scr, dv_scr, dqt_scr) = refs
        kb = pl.program_id(2)
        k0 = kb * tk
        heads = [_head_lanes(j, dv) for j in range(hg)]

        def head_do(j, do_c):
            return jnp.where(heads[j], do_c, jnp.zeros_like(do_c)) if hg == 2 else do_c

        def probs_t(j, k_c, v_c, q_c, do_c, i, cf_keys, c0, masked):
            st = _scores_t(sp, j, k_c, _head_q(sp, j, q_c), cfq_ref[i] if fox else None, cf_keys, c0, i * tq, masked)
            pt = jnp.exp(st - lse_ref[i][j:j + 1, :])
            dpt = lax.dot_general(v_c, head_do(j, do_c), _DIMS["nt"], preferred_element_type=F32)
            return pt, dpt

        @pl.when(kb == 0)
        def _():
            dqt_scr[...] = jnp.zeros(dqt_scr.shape, F32)

            def fill(i, carry):
                r0 = pl.multiple_of(i * tq, tq)
                do_c = do_ref[pl.ds(r0, tq), :]
                if fox:
                    q_c = q_ref[pl.ds(r0, tq), :]

                    def keys(kc, acc, masked):
                        c0 = pl.multiple_of(kc * tk, tk)
                        out = []
                        for j in range(hg):
                            pt, dpt = probs_t(j, kall_ref[pl.ds(c0, tk), :], vall_ref[pl.ds(c0, tk), :], q_c, do_c, i,
                                              cfkall_ref[pl.ds(c0, tk), :], c0, masked)
                            out.append(acc[j] + jnp.sum(pt * dpt, axis=0, keepdims=True))
                        return tuple(out)

                    d = lax.fori_loop(0, i, functools.partial(keys, masked=False),
                                      tuple(jnp.zeros((1, tq), F32) for _ in range(hg)))
                    d = keys(i, d, True)
                    for j in range(hg):
                        delta_scr[i, j:j + 1, :] = d[j]
                else:
                    prod_t = (do_c.astype(F32) * o_ref[pl.ds(r0, tq), :]).T
                    for j in range(hg):
                        delta_scr[i, j:j + 1, :] = jnp.sum(prod_t[_head_rows(sp, j), :], axis=0, keepdims=True)
                return carry

            lax.fori_loop(0, nqb, fill, 0)

        k_blk = k_ref[...]
        v_blk = v_ref[...]
        k_t = k_blk.astype(F32).T.astype(BF16)
        dk_scr[...] = jnp.zeros(dk_scr.shape, F32)
        dv_scr[...] = jnp.zeros(dv_scr.shape, F32)
        if fox:
            dcf_scr[...] = jnp.zeros(dcf_scr.shape, F32)

        def qblock(i, carry, masked):
            r0 = pl.multiple_of(i * tq, tq)
            q_c = q_ref[pl.ds(r0, tq), :]
            do_c = do_ref[pl.ds(r0, tq), :]
            for j in range(hg):
                pt, dpt = probs_t(j, k_blk, v_blk, q_c, do_c, i, cfk_ref[...] if fox else None, k0, masked)
                dst = pt * (dpt - delta_scr[i][j:j + 1, :])
                if fox:
                    part = dst[:, :LANES]
                    for t in range(1, tq // LANES):
                        part = part + dst[:, t * LANES:(t + 1) * LANES]
                    dcf_scr[j] += part
                ds_b = (dst * sp.scale).astype(BF16)
                dv_scr[j] += jnp.dot(pt.astype(BF16), do_c, preferred_element_type=F32)
                dk_scr[j] += jnp.dot(ds_b, q_c if sp.hq == 2 else _head_q(sp, j, q_c), preferred_element_type=F32)
                if sp.hq == 2:
                    r = pl.ds(j * FOX_DH, FOX_DH)
                    dqt_scr[i, r, :] += jnp.dot(k_t[j * FOX_DH:(j + 1) * FOX_DH, :], ds_b, preferred_element_type=F32)
                else:
                    r = pl.ds(j * LANES, LANES)
                    dqt_scr[i, r, :] += jnp.dot(k_t[j * LANES:(j + 1) * LANES, :], ds_b, preferred_element_type=F32)
            return carry

        if causal:
            qblock(kb, 0, True)
            lax.fori_loop(kb + 1, nqb, functools.partial(qblock, masked=False), 0)
        else:
            lax.fori_loop(0, nqb, functools.partial(qblock, masked=False), 0)

        @pl.when(kb == nkb - 1)
        def _():
            def untranspose(i, carry):
                dq_ref[pl.ds(pl.multiple_of(i * tq, tq), tq), :] = dqt_scr[i].T
                return carry

            lax.fori_loop(0, nqb, untranspose, 0)

        if hg == 2:
            dv_ref[...] = jnp.where(heads[0], dv_scr[0], dv_scr[1]).astype(dv_ref.dtype)
        else:
            dv_ref[...] = dv_scr[0].astype(dv_ref.dtype)
        if sp.hq == 2:
            dk_ref[...] = jnp.where(_head_lanes(0, FOX_DH), dk_scr[0], dk_scr[1]).astype(dk_ref.dtype)
        elif hg == 2:
            dk_ref[...] = jnp.concatenate([dk_scr[0], dk_scr[1]], axis=1).astype(dk_ref.dtype)
        else:
            dk_ref[...] = dk_scr[0].astype(dk_ref.dtype)
        if fox:
            lane = lax.broadcasted_iota(jnp.int32, (1, LANES), 1)
            sums = [jnp.sum(dcf_scr[j], axis=1, keepdims=True) for j in range(hg)]
            dcf_ref[...] = jnp.where(lane == 0, -sums[0], jnp.where(lane == 1, -sums[1], 0.0))

    seq_lanes = lambda b, g, kb: (b, g)
    key_blk = lambda b, g, kb: (b * nkb + kb, g)
    stats = pl.BlockSpec((None, None, nqb, hg, tq), lambda b, g, kb: (b, g, 0, 0, 0))
    in_specs = [
        pl.BlockSpec((sp.sq, qw), lambda b, g, kb: (b, qo + g)),
        pl.BlockSpec((tk, qw), lambda b, g, kb: (b * nkb + kb, ko + g)),
        pl.BlockSpec((tk, LANES), lambda b, g, kb: (b * nkb + kb, vo + g)),
    ]
    ops = [qa, ka, va]
    if not fox:
        in_specs.append(pl.BlockSpec((sp.sq, LANES), seq_lanes))
        ops.append(o)
    in_specs += [stats, pl.BlockSpec((sp.sq, LANES), seq_lanes)]
    ops += [lse, do]
    out_specs = [pl.BlockSpec((sp.sq, qw), seq_lanes), pl.BlockSpec((tk, qw), key_blk), pl.BlockSpec((tk, LANES), key_blk)]
    out_shape = [
        jax.ShapeDtypeStruct((sp.batch * sp.sq, sp.groups * qw), F32),
        jax.ShapeDtypeStruct((sp.batch * sp.sk, sp.groups * qw), BF16),
        jax.ShapeDtypeStruct((sp.batch * sp.sk, sp.groups * LANES), BF16),
    ]
    scratch = [pltpu.VMEM((nqb, hg, tq), F32), pltpu.VMEM((hg, tk, LANES), F32), pltpu.VMEM((hg, tk, LANES), F32),
               pltpu.VMEM((nqb, qw, tq), F32)]
    if fox:
        in_specs += [
            pl.BlockSpec((None, nqb, None, hg, tq), lambda b, g, kb: (b, 0, g, 0, 0)),
            pl.BlockSpec((tk, hg * LANES), key_blk),
            pl.BlockSpec((sp.sk, qw), lambda b, g, kb: (b, ko + g)),
            pl.BlockSpec((sp.sk, LANES), lambda b, g, kb: (b, vo + g)),
            pl.BlockSpec((sp.sk, hg * LANES), seq_lanes),
        ]
        ops += [cfq, cfk, ka, va, cfk]
        out_specs.append(pl.BlockSpec((tk, LANES), key_blk))
        out_shape.append(jax.ShapeDtypeStruct((sp.batch * sp.sk, sp.groups * LANES), F32))
        scratch.append(pltpu.VMEM((hg, tk, LANES), F32))
    blk = _nbytes((sp.sq, qw), BF16) + _nbytes((sp.sq, LANES), BF16) + 2 * _nbytes((sp.sq, LANES), F32)
    blk += _nbytes((sp.sq, qw), F32) + 4 * _nbytes((tk, qw), BF16) + 8 * _nbytes((tq, tk), F32)
    blk += (_nbytes((sp.sk, qw + LANES), BF16) + _nbytes((sp.sk, hg * LANES), F32)) if fox else 0
    call = lambda rider: _pcall(
        body,
        rider=rider,
        name=name,
        grid=(sp.batch, sp.groups, nkb),
        in_specs=in_specs,
        out_specs=out_specs,
        out_shape=out_shape,
        scratch_shapes=scratch,
        compiler_params=_params(blk, _nbytes((sp.sq, LANES), F32) + 4 * _nbytes((tk, LANES), F32)),
    )
    return _hosted(hosts, host, call, ops, len(out_shape))


def _slabwise(name, fn, ins, out_dtypes, rows_per_step=512):
    n = max(a.shape[0] for a in ins)
    rows, cols = ins[0].shape[1:]
    tr = min(rows_per_step, rows)
    while rows % tr:
        tr //= 2
    assert tr % 16 == 0 or tr == rows, (name, rows, tr)

    def spec(a):
        if a.shape[0] == 1:
            return pl.BlockSpec((None, tr, cols), lambda s, i: (0, i, 0))
        return pl.BlockSpec((None, tr, cols), lambda s, i: (s, i, 0))

    def body(*refs):
        vals = fn(*[r[...] for r in refs[:len(ins)]])
        for r, v in zip(refs[len(ins):], vals):
            r[...] = v.astype(r.dtype)

    blk = (len(ins) + len(out_dtypes)) * _nbytes((tr, cols + LANES), F32)
    res = _pcall(
        body,
        name=name,
        grid=(n, rows // tr),
        in_specs=[spec(a) for a in ins],
        out_specs=[pl.BlockSpec((None, tr, cols), lambda s, i: (s, i, 0)) for _ in out_dtypes],
        out_shape=[jax.ShapeDtypeStruct((n, rows, cols), dt) for dt in out_dtypes],
        compiler_params=_params(2 * blk),
    )(*ins)
    return res


def _adamw_math(w, g, m, v):
    m = ADAM_B1 * m + (1.0 - ADAM_B1) * g
    v = ADAM_B2 * v + (1.0 - ADAM_B2) * jnp.square(g)
    m_hat = m / (1.0 - ADAM_B1 ** ADAM_STEP)
    v_hat = v / (1.0 - ADAM_B2 ** ADAM_STEP)
    delta = -ADAM_LR * (m_hat / (jnp.sqrt(v_hat) + ADAM_EPS) + ADAM_WD * w)
    return delta, m, v


def _run_exchange(name, ex):
    n_in, n_out = len(ex.ins), len(ex.out_shapes)

    def body(*refs):
        copies = _exchange_copies(ex, refs[:n_in], refs[n_in:n_in + n_out], refs[-2], refs[-1])
        for cp in copies:
            cp.start()
        for cp in copies:
            cp.wait()

    any_spec = pl.BlockSpec(memory_space=pl.ANY)
    return _pcall(
        body,
        name=name,
        in_specs=[any_spec] * n_in,
        out_specs=[any_spec] * n_out,
        out_shape=list(ex.out_shapes),
        scratch_shapes=[pltpu.SemaphoreType.DMA((ex.n_copies,))] * 2,
        input_output_aliases=dict(ex.aliases),
    )(*ex.ins)


def _gather_ici(shards):
    def plan(in_refs, out_refs):
        c = lax.axis_index("c")
        return [(s.at[c], g.at[rel, 0], rel) for s, g in zip(in_refs, out_refs) for rel in (1, 2, 3)]

    shapes = tuple(jax.ShapeDtypeStruct((N_CHIPS,) + s.shape, s.dtype) for s in shards)
    return Exchange(tuple(shards), shapes, plan, 3 * len(shards))


def _gather_d2d(got):
    def plan(in_refs, out_refs):
        return [(g_in.at[pl.ds(1, 3), 0], g_out.at[pl.ds(1, 3), 1], "c") for g_in, g_out in zip(in_refs, out_refs)]

    shapes = tuple(jax.ShapeDtypeStruct(g.shape, g.dtype) for g in got)
    return Exchange(tuple(got), shapes, plan, len(got), {i: i for i in range(len(got))})


GATHER_FIRST = 5


class GatherNext(Hosts):
    def __init__(self, shards):
        self.shards, self.got = shards, []

    def rider(self, host):
        if host == "fox_fwd":
            return _gather_ici(self.shards[:GATHER_FIRST])
        if host == "mla_fwd":
            return _gather_ici(self.shards[GATHER_FIRST:])
        if host == "mem_fwd":
            return _gather_d2d(self.got)
        return None

    def done(self, host, outs):
        self.got = outs if host == "mem_fwd" else self.got + outs


def _gather_now(shards):
    got = _run_exchange("gather_weights_ici", _gather_ici(shards))
    return _run_exchange("gather_weights_d2d", _gather_d2d(got))


def _to_absolute(rel_arr, own):
    x, y, c = lax.axis_index("x"), lax.axis_index("y"), lax.axis_index("c")
    tail = rel_arr.shape[2:]
    zeros = (0,) * len(tail)
    own_rel = jnp.concatenate([lax.dynamic_slice(own, (h ^ c,) + zeros, (1,) + tail) for h in (0, 1)], axis=0)
    rel_arr = jnp.concatenate([own_rel[None], rel_arr[1:]], axis=0)
    parts = []
    for sx in (0, 1):
        for sy in (0, 1):
            for h in (0, 1):
                rel = (sx ^ x) + 2 * (sy ^ y)
                parts.append(lax.dynamic_slice(rel_arr, (rel, h ^ c) + zeros, (1, 1) + tail))
    return jnp.concatenate(parts, axis=0).reshape((N_CHIPS, 2) + tail)


def _to_relative(abs_arr):
    x, y, c = lax.axis_index("x"), lax.axis_index("y"), lax.axis_index("c")
    tail = abs_arr.shape[2:]
    zeros = (0,) * len(tail)
    parts = []
    for hr in (0, 1):
        for rel in (0, 1, 2, 3):
            chip = 2 * (x ^ (rel & 1)) + (y ^ (rel >> 1))
            parts.append(lax.dynamic_slice(abs_arr, (chip, c ^ hr) + zeros, (1, 1) + tail))
    return jnp.concatenate(parts, axis=0).reshape((2, N_CHIPS) + tail)


class ReduceLayer(Hosts):
    def __init__(self, tag, grads_rel):
        self.tag, self.grads, self.pair, self.total, self.theirs = tag, grads_rel, None, None, None

    def _swap_halves(self):
        def plan(in_refs, out_refs):
            return [(g.at[1], r, "c") for g, r in zip(in_refs, out_refs)]

        shapes = tuple(jax.ShapeDtypeStruct(g.shape[1:], g.dtype) for g in self.grads)
        return Exchange(tuple(self.grads), shapes, plan, len(self.grads))

    def _to_chips(self):
        def plan(in_refs, out_refs):
            return [(p.at[rel], r.at[rel - 1], rel) for p, r in zip(in_refs, out_refs) for rel in (1, 2, 3)]

        shapes = tuple(jax.ShapeDtypeStruct((3,) + p.shape[1:], p.dtype) for p in self.pair)
        return Exchange(tuple(self.pair), shapes, plan, 3 * len(self.pair))

    def _share(self):
        def plan(in_refs, out_refs):
            return [(t, r, "c") for t, r in zip(in_refs, out_refs)]

        shapes = tuple(jax.ShapeDtypeStruct(t.shape, F32) for t in self.total)
        return Exchange(tuple(self.total), shapes, plan, len(self.total))

    def rider(self, host):
        stages = {"ff2_bwd_x": self._swap_halves, "fox_bwd": self._to_chips, "mla_bwd": self._share}
        return stages[host]() if host in stages else None

    def done(self, host, outs):
        if host == "ff2_bwd_x":
            self.pair = [
                _slabwise(f"reduce_pair_sum_{self.tag}_{i}", lambda a, b: (a.astype(F32) + b.astype(F32),),
                          [g[0], r], [BF16])[0]
                for i, (g, r) in enumerate(zip(self.grads, outs))]
        elif host == "fox_bwd":
            self.total = [
                _slabwise(f"reduce_chip_sum_{self.tag}_{i}",
                          lambda a, b, c_, d: (a.astype(F32) + b.astype(F32) + c_.astype(F32) + d.astype(F32),),
                          [p[0:1], r[0:1], r[1:2], r[2:3]], [F32])[0]
                for i, (p, r) in enumerate(zip(self.pair, outs))]
        else:
            self.theirs = outs

    def run_now(self):
        self.done("ff2_bwd_x", _run_exchange(f"reduce_pair_{self.tag}", self._swap_halves()))
        self.done("fox_bwd", _run_exchange(f"reduce_chips_{self.tag}", self._to_chips()))
        self.done("mla_bwd", _run_exchange(f"reduce_share_{self.tag}", self._share()))

    def result(self):
        return list(zip(self.total, self.theirs))


def _adamw_layer(name, l, w, m, v, mine, theirs, prev, rows_per_step=256):
    _, _, rows, cols = w.shape
    tr = min(rows_per_step, rows)
    while rows % tr:
        tr //= 2

    def body(w_ref, m_ref, v_ref, mine_ref, theirs_ref, *rest):
        g_ref, d_ref, nm_ref, nv_ref = rest[-4:]
        g = jnp.where(pl.program_id(0) == lax.axis_index("c"), mine_ref[...], theirs_ref[...])
        d, nm, nv = _adamw_math(w_ref[...], g, m_ref[...], v_ref[...])
        g_ref[...], d_ref[...], nm_ref[...], nv_ref[...] = g, d, nm, nv

    half = pl.BlockSpec((None, None, tr, cols), lambda h, i: (l, h, i, 0))
    one = pl.BlockSpec((None, tr, cols), lambda h, i: (0, i, 0))
    kept = [] if prev is None else list(prev)
    return _pcall(
        body,
        name=name,
        grid=(2, rows // tr),
        in_specs=[half, half, half, one, one] + [pl.BlockSpec(memory_space=pl.ANY)] * len(kept),
        out_specs=[half] * 4,
        out_shape=[jax.ShapeDtypeStruct(w.shape, F32)] * 4,
        input_output_aliases={5 + i: i for i in range(len(kept))},
        compiler_params=_params(2 * 9 * _nbytes((tr, cols + LANES), F32)),
    )(w, m, v, mine, theirs, *kept)


def _allreduce_small(v):
    rows = v.shape[0]

    def body(v_ref, sum_ref, all_ref, send_sems, recv_sems, local_sem):
        x, y, c = lax.axis_index("x"), lax.axis_index("y"), lax.axis_index("c")
        sibling = (x, y, 1 - c)
        chips = [(1 - x, y), (x, 1 - y), (1 - x, 1 - y)]

        def slab(px, py, pc):
            return all_ref.at[pl.ds((4 * px + 2 * py + pc) * rows, rows), :]

        def copy(k, block, to, src=None):
            return pltpu.make_async_remote_copy(
                src_ref=slab(*block) if src is None else src, dst_ref=slab(*block), send_sem=send_sems.at[k],
                recv_sem=recv_sems.at[k], device_id=to, device_id_type=MESH)

        mine = pltpu.make_async_copy(v_ref, slab(x, y, c), local_sem)
        mine.start()
        first = [copy(0, (x, y, c), sibling, src=v_ref)]
        first += [copy(1 + j, (x, y, c), (*chip, c), src=v_ref) for j, chip in enumerate(chips)]
        for cp in first:
            cp.start()
        passed = [copy(4 + j, (*chip, c), sibling) for j, chip in enumerate(chips)]
        for j, chip in enumerate(chips):
            copy(1 + j, (*chip, c), (x, y, c)).wait_recv()
            passed[j].start()
        copy(0, (x, y, 1 - c), (x, y, c)).wait_recv()
        for j, chip in enumerate(chips):
            copy(4 + j, (*chip, 1 - c), (x, y, c)).wait_recv()
        for cp in first + passed:
            cp.wait_send()
        mine.wait()
        total = all_ref[pl.ds(0, rows), :]
        for d in range(1, N_DEV):
            total = total + all_ref[pl.ds(d * rows, rows), :]
        sum_ref[...] = total

    vm = pl.BlockSpec(memory_space=pltpu.VMEM)
    return _pcall(
        body,
        name="allreduce_small",
        in_specs=[vm],
        out_specs=vm,
        out_shape=jax.ShapeDtypeStruct((rows, LANES), F32),
        scratch_shapes=[pltpu.VMEM((N_DEV * rows, LANES), F32), pltpu.SemaphoreType.DMA((7,)),
                        pltpu.SemaphoreType.DMA((7,)), pltpu.SemaphoreType.DMA],
    )(v)


def _pad_cols(a, before, total):
    return jnp.pad(a, ((0, 0), (before, total - before - a.shape[1])))


def _layer_weights(cfg, w_in, w_uq, w_ukv):
    w = cfg.width
    qkv, f, cq, ckv, kr, qm, gates = jnp.split(w_in, list(_cumsum(cfg.in_splits))[:-1], axis=1)
    wa = jnp.concatenate([qkv, qm], axis=1)
    ws = jnp.concatenate([_pad_cols(f, 0, LANES), cq, ckv, _pad_cols(kr, MLA_NOPE, LANES)], axis=1)
    wq = jnp.pad(w_uq.reshape(cfg.q_rank, cfg.mla_h, MLA_NOPE + MLA_ROPE), ((0, 0), (0, 0), (0, LANES - MLA_NOPE - MLA_ROPE)))
    wq = wq.reshape(cfg.q_rank, cfg.mla_h * LANES)
    kv = w_ukv.reshape(cfg.kv_rank, cfg.mla_h, MLA_NOPE + MLA_V)
    wk = jnp.pad(kv[:, :, :MLA_NOPE], ((0, 0), (0, 0), (0, LANES - MLA_NOPE))).reshape(cfg.kv_rank, cfg.mla_h * LANES)
    wv = kv[:, :, MLA_NOPE:].reshape(cfg.kv_rank, cfg.mla_h * MLA_V)
    del w
    return wa, gates, ws, wq, wk, wv


def _cumsum(xs):
    out, t = [], 0
    for v in xs:
        t += v
        out.append(t)
    return out


def _layer_weight_grads(cfg, dwa, dwg, dws, dwq, dwk, dwv):
    w, qr, kvr = cfg.width, cfg.q_rank, cfg.kv_rank
    off_kr = LANES + qr + kvr + MLA_NOPE
    dw_in = jnp.concatenate([
        dwa[:, :3 * w], dws[:, :cfg.fox_h], dws[:, LANES:LANES + qr], dws[:, LANES + qr:LANES + qr + kvr],
        dws[:, off_kr:off_kr + MLA_ROPE], dwa[:, 3 * w:], dwg], axis=1)
    dw_uq = dwq.reshape(qr, cfg.mla_h, LANES)[:, :, :MLA_NOPE + MLA_ROPE].reshape(qr, cfg.mla_h * (MLA_NOPE + MLA_ROPE))
    dw_ukv = jnp.concatenate([dwk.reshape(kvr, cfg.mla_h, LANES)[:, :, :MLA_NOPE], dwv.reshape(kvr, cfg.mla_h, MLA_V)],
                             axis=2).reshape(kvr, cfg.mla_h * (MLA_NOPE + MLA_V))
    return dw_in, dw_uq, dw_ukv


def _attn_specs(cfg, batch):
    t = min(ATTN_TILE, cfg.seq)
    common = dict(batch=batch, sq=cfg.seq, chunk=cfg.chunk, tq=t)
    fox = Attn(sk=cfg.seq, groups=cfg.fox_h // 2, hq=2, hv=2, mode="fox", scale=FOX_DH ** -0.5, tk=t, **common)
    mla = Attn(sk=cfg.seq, groups=cfg.mla_h // 2, hq=1, hv=2, mode="chunk",
               scale=(MLA_NOPE + MLA_ROPE) ** -0.5, tk=t, **common)
    mem = Attn(sk=cfg.n_mem, groups=cfg.mem_h, hq=1, hv=1, mode="none", scale=MEM_DH ** -0.5, tk=cfg.n_mem, **common)
    return fox, mla, mem


def _small_core(cfg, ps, bf, gq, gkv):
    qr, kvr = cfg.q_rank, cfg.kv_rank
    z = ps[:, :LANES] + bf
    logf = jnp.minimum(z, 0.0) - jnp.log1p(jnp.exp(-jnp.abs(z)))
    nq = _rms(ps[:, LANES:LANES + qr], gq)
    nkv = _rms(ps[:, LANES + qr:LANES + qr + kvr], gkv)
    return logf, nq, nkv


def _layer_fwd(cfg, l, batch, h, hb, mem_b, rope_c, rope_s, hosts, lw, bf_pad, g_cq, g_ckv, ln1, ln2):
    wa, wg, ws, wq, wk, wv, wmkv, wbr, wout, wff1, wff2 = lw
    w, d = cfg.width, cfg.d
    fox, mla, mem = _attn_specs(cfg, batch)
    nw = w // LANES
    pa = _mm(f"proj_a_{l}", "nn", [(hb, wa)], [BF16])
    gl = _mm(f"proj_gates_{l}", "nn", [(hb, wg)], [F32])
    ps = _mm(f"proj_small_{l}", "nn", [(hb, ws)], [F32], tn=cfg.small_w)

    def small_fwd(ps_, c_, s_, bf_, gq_, gkv_):
        logf, nq, nkv = _small_core(cfg, ps_, bf_, gq_, gkv_)
        kpe = _rope(ps_[:, cfg.small_w - LANES:], c_, s_)
        return logf, nq, nkv, kpe

    logf, nq, nkv, kpe = _rowwise(
        f"small_fwd_{l}", small_fwd, [ps, rope_c, rope_s], [bf_pad, g_cq, g_ckv],
        [(LANES, F32), (cfg.q_rank, BF16), (cfg.kv_rank, BF16), (LANES, F32)])
    cfk, cfq = _forget_cumsum(f"cum_forget_{l}", logf, batch, cfg.seq, cfg.fox_h, fox.tq)

    qf = _mm(f"mla_q_{l}", "nn", [(nq, wq)], [BF16], tn=wq.shape[1],
             epi=lambda acc, c_, s_: (_rope(acc, c_, s_),), row_extras=[rope_c, rope_s])
    kf = _mm(f"mla_k_{l}", "nn", [(nkv, wk)], [BF16], tn=wk.shape[1],
             epi=lambda acc, kp: (acc + jnp.tile(kp, (1, cfg.mla_h)),), row_extras=[kpe])
    vb = _mm(f"mla_v_{l}", "nn", [(nkv, wv)], [BF16])
    mkv = _mm(f"mem_kv_{l}", "nn", [(mem_b, wmkv)], [BF16])

    o_a, lse_a = _attn_fwd(f"fox_fwd_{l}", fox, (pa, 0), (pa, nw), (pa, 2 * nw), cfq, cfk,
                           hosts=hosts, host="fox_fwd")
    o_b, lse_b = _attn_fwd(f"mla_fwd_{l}", mla, (qf, 0), (kf, 0), (vb, 0), hosts=hosts, host="mla_fwd")
    o_c, lse_c = _attn_fwd(f"mem_fwd_{l}", mem, (pa, 3 * nw), (mkv, 0), (mkv, nw), hosts=hosts, host="mem_fwd")
    bps = [_mm(f"branch_{n}_{l}", "nn", [(o, wbr[n])], [F32]) for n, o in enumerate((o_a, o_b, o_c))]

    def merge(gl_, b0, b1, b2):
        g = jax.nn.sigmoid(gl_)
        return (g[:, :d] * b0 + g[:, d:2 * d] * b1 + g[:, 2 * d:] * b2,)

    (merged,) = _rowwise(f"merge_{l}", merge, [gl] + bps, [], [(d, BF16)])

    def post_ln(acc, res, g_, b_):
        z = cfg.alpha * res + acc
        y = _ln(z, g_, b_)
        return z, y, y

    z1, h1, h1b = _mm(f"out_ln1_{l}", "nn", [(merged, wout)], [F32, F32, BF16], tm=256, tn=d,
                      epi=post_ln, row_extras=[h], bc_extras=list(ln1))
    u, a = _mm(f"ff1_{l}", "nn", [(h1b, wff1)], [BF16, BF16],
               epi=lambda acc: (acc, jnp.square(jnp.maximum(acc, 0.0))))
    z2, h2, h2b = _mm(f"ff2_ln2_{l}", "nn", [(a, wff2)], [F32, F32, BF16], tm=256, tn=d,
                      epi=post_ln, row_extras=[h1], bc_extras=list(ln2))
    saved = dict(hb=hb, pa=pa, gl=gl, ps=ps, nq=nq, nkv=nkv, cfq=cfq, cfk=cfk, qf=qf, kf=kf, vb=vb, mkv=mkv,
                 o=(o_a, o_b, o_c), lse=(lse_a, lse_b, lse_c), bps=bps, merged=merged, z1=z1, h1b=h1b, u=u, a=a, z2=z2)
    return h2, h2b, saved


def _ln_bwd(name, cfg, ga, gb, z, g, b):
    d = cfg.d

    def fn(*vals):
        if gb is None:
            ga_, z_, g_, b_ = vals
            dy = ga_
        else:
            ga_, gb_, z_, g_, b_ = vals
            dy = ga_ + cfg.alpha * gb_
        _, vjp = jax.vjp(_ln, z_, g_, b_)
        dz, dg, db = vjp(dy)
        return dz, dz, dg, db

    rows = [ga, z] if gb is None else [ga, gb, z]
    return _rowwise(name, fn, rows, [g, b], [(d, F32), (d, BF16)], accs=[(1, d), (1, d)])


def _layer_bwd(cfg, l, batch, ga, gb, sv, mem_b, rope_c, rope_s, hosts, lw, bf_pad, g_cq, g_ckv, ln1, ln2):
    wa, wg, ws, wq, wk, wv, wmkv, wbr, wout, wff1, wff2 = lw
    w, d = cfg.width, cfg.d
    fox, mla, mem = _attn_specs(cfg, batch)
    nw = w // LANES
    gdt = BF16

    dz2, dz2b, dg2, db2 = _ln_bwd(f"ln2_bwd_{l}", cfg, ga, gb, sv["z2"], *ln2)
    du = _mm(f"ff2_bwd_x_{l}", "nt", [(dz2b, wff2)], [BF16],
             epi=lambda acc, u_: (acc * (2.0 * jnp.maximum(u_.astype(F32), 0.0)),), row_extras=[sv["u"]],
             hosts=hosts, host="ff2_bwd_x")
    dwff2 = _mm(f"ff2_bwd_w_{l}", "tn", [(sv["a"], dz2b)], [gdt])
    dwff1 = _mm(f"ff1_bwd_w_{l}", "tn", [(sv["h1b"], du)], [gdt])
    dh1 = _mm(f"ff1_bwd_x_{l}", "nt", [(du, wff1)], [F32])
    dz1, dz1b, dg1, db1 = _ln_bwd(f"ln1_bwd_{l}", cfg, dh1, dz2, sv["z1"], *ln1)
    dmerged = _mm(f"out_bwd_x_{l}", "nt", [(dz1b, wout)], [F32])
    dwout = _mm(f"out_bwd_w_{l}", "tn", [(sv["merged"], dz1b)], [gdt])

    def merge_bwd(dm, gl_, b0, b1, b2):
        def f(gl__, b0_, b1_, b2_):
            g = jax.nn.sigmoid(gl__)
            return g[:, :d] * b0_ + g[:, d:2 * d] * b1_ + g[:, 2 * d:] * b2_

        _, vjp = jax.vjp(f, gl_, b0, b1, b2)
        return vjp(dm)

    dgl, db0, db1_, db2_ = _rowwise(f"merge_bwd_{l}", merge_bwd, [dmerged, sv["gl"]] + sv["bps"], [],
                                    [(3 * d, BF16), (d, BF16), (d, BF16), (d, BF16)])
    dbps = (db0, db1_, db2_)
    dos = [_mm(f"branch_bwd_x_{n}_{l}", "nt", [(dbps[n], wbr[n])], [BF16]) for n in range(3)]
    dwbr = [_mm(f"branch_bwd_w_{n}_{l}", "tn", [(sv["o"][n], dbps[n])], [gdt]) for n in range(3)]

    pa = sv["pa"]
    dq_a, dk_a, dv_a, dcfk = _attn_bwd(f"fox_bwd_{l}", fox, (pa, 0), (pa, nw), (pa, 2 * nw), sv["o"][0], sv["lse"][0],
                                       dos[0], sv["cfq"], sv["cfk"], hosts=hosts, host="fox_bwd")
    dqf, dkf, dvb = _attn_bwd(f"mla_bwd_{l}", mla, (sv["qf"], 0), (sv["kf"], 0), (sv["vb"], 0), sv["o"][1],
                              sv["lse"][1], dos[1], hosts=hosts, host="mla_bwd")
    dqm, dmk, dmv = _attn_bwd(f"mem_bwd_{l}", mem, (pa, 3 * nw), (sv["mkv"], 0), (sv["mkv"], nw), sv["o"][2],
                              sv["lse"][2], dos[2])
    dwmkv = _mm(f"mem_kv_bwd_w_{l}", "tn", [(mem_b, jnp.concatenate([dmk, dmv], axis=1))], [gdt])

    (dq_raw,) = _rowwise(f"mla_q_rope_bwd_{l}", lambda dy, c_, s_: (_rope_t(dy, c_, s_),), [dqf, rope_c, rope_s], [],
                         [(wq.shape[1], BF16)])
    dwq = _mm(f"mla_q_bwd_w_{l}", "tn", [(sv["nq"], dq_raw)], [gdt])
    dnq = _mm(f"mla_q_bwd_x_{l}", "nt", [(dq_raw, wq)], [F32])
    dwk = _mm(f"mla_k_bwd_w_{l}", "tn", [(sv["nkv"], dkf)], [gdt])
    dwv = _mm(f"mla_v_bwd_w_{l}", "tn", [(sv["nkv"], dvb)], [gdt])
    dnkv = _mm(f"mla_kv_bwd_x_{l}", "nt", [(dkf, wk), (dvb, wv)], [F32])

    dlogf = _forget_cumsum_bwd(f"cum_forget_bwd_{l}", dcfk, batch, cfg.seq, cfg.fox_h, fox.tq)

    def small_bwd(ps_, dlogf_, dnq_, dnkv_, dkf_, c_, s_, bf_, gq_, gkv_):
        _, vjp = jax.vjp(functools.partial(_small_core, cfg), ps_, bf_, gq_, gkv_)
        dps, dbf, dgq, dgkv = vjp((dlogf_, dnq_, dnkv_))
        dkpe = dkf_[:, :LANES].astype(F32)
        for hh in range(1, cfg.mla_h):
            dkpe = dkpe + dkf_[:, hh * LANES:(hh + 1) * LANES].astype(F32)
        lane = lax.broadcasted_iota(jnp.int32, (1, LANES), 1)
        dkpe = jnp.where((lane >= MLA_NOPE) & (lane < MLA_NOPE + MLA_ROPE), dkpe, 0.0)
        dkr = _rope_t(dkpe, c_, s_)
        dps = jnp.concatenate([dps[:, :cfg.small_w - LANES], dkr], axis=1)
        return dps, dbf, dgq, dgkv

    dps, dbf, dgq, dgkv = _rowwise(
        f"small_bwd_{l}", small_bwd, [sv["ps"], dlogf, dnq, dnkv, dkf, rope_c, rope_s], [bf_pad, g_cq, g_ckv],
        [(cfg.small_w, BF16)], accs=[(1, LANES), (1, cfg.q_rank), (1, cfg.kv_rank)])

    dpa = jnp.concatenate([dq_a.astype(BF16), dk_a, dv_a, dqm.astype(BF16)], axis=1)
    hb = sv["hb"]
    dh = _mm(f"proj_bwd_x_{l}", "nt", [(dpa, wa), (dgl, wg), (dps, ws)], [F32], tn=d)
    dwa = _mm(f"proj_a_bwd_w_{l}", "tn", [(hb, dpa)], [gdt])
    dwg = _mm(f"proj_gates_bwd_w_{l}", "tn", [(hb, dgl)], [gdt])
    dws = _mm(f"proj_small_bwd_w_{l}", "tn", [(hb, dps)], [gdt], tn=cfg.small_w)
    dw_in, dw_uq, dw_ukv = _layer_weight_grads(cfg, dwa, dwg, dws, dwq, dwk, dwv)
    big = dict(w_in=dw_in, w_uq=dw_uq, w_ukv=dw_ukv, w_mem_kv=dwmkv, w_br=jnp.stack(dwbr), w_out=dwout,
               w_ff1=dwff1, w_ff2=dwff2)
    small = dict(b_forget=dbf[0, :cfg.fox_h], g_cq=dgq[0], g_ckv=dgkv[0], ln1_g=dg1[0], ln1_b=db1[0],
                 ln2_g=dg2[0], ln2_b=db2[0])
    return dh, dz1, big, small


def _rope_tables(positions):
    inv_freq = ROPE_BASE ** (-jnp.arange(0, MLA_ROPE, 2, dtype=F32) / MLA_ROPE)
    ang = positions.astype(F32).reshape(-1)[:, None] * inv_freq
    cos, sin = jnp.cos(ang), jnp.sin(ang)
    t = ang.shape[0]
    rope_c = jnp.concatenate([jnp.ones((t, MLA_NOPE), F32), cos, cos, jnp.zeros((t, LANES - MLA_NOPE - MLA_ROPE), F32)], axis=1)
    rope_s = jnp.concatenate([jnp.zeros((t, MLA_NOPE), F32), -sin, sin, jnp.zeros((t, LANES - MLA_NOPE - MLA_ROPE), F32)], axis=1)
    return rope_c, rope_s


def _local_step(cfg, x, mem, positions, target, small_w, comm):
    batch = x.shape[0]
    d, depth = cfg.d, cfg.depth
    t = batch * cfg.seq
    x2, tgt = x.reshape(t, d), target.reshape(t, d)
    mem_b = mem.reshape(batch * cfg.n_mem, d).astype(BF16)
    rope_c, rope_s = _rope_tables(positions)
    row = lambda v: v.reshape(1, -1)
    ln_in = (row(small_w["ln_in_g"]), row(small_w["ln_in_b"]))

    h, hb = _rowwise("ln_in", lambda x_, g_, b_: (_ln(x_, g_, b_),) * 2, [x2], list(ln_in), [(d, F32), (d, BF16)])
    layers, saves = [], []
    for l in range(depth):
        big_w = comm.weights(l)
        lw = _layer_weights(cfg, big_w["w_in"], big_w["w_uq"], big_w["w_ukv"]) + (
            big_w["w_mem_kv"], big_w["w_br"], big_w["w_out"], big_w["w_ff1"], big_w["w_ff2"])
        par = dict(
            lw=lw, bf_pad=jnp.pad(row(small_w["b_forget"][l]), ((0, 0), (0, LANES - cfg.fox_h))),
            g_cq=row(small_w["g_cq"][l]), g_ckv=row(small_w["g_ckv"][l]),
            ln1=(row(small_w["ln1_g"][l]), row(small_w["ln1_b"][l])),
            ln2=(row(small_w["ln2_g"][l]), row(small_w["ln2_b"][l])))
        layers.append(par)
        h, hb, sv = _layer_fwd(cfg, l, batch, h, hb, mem_b, rope_c, rope_s, comm.forward_hosts(l), **par)
        saves.append(sv)

    def loss_fn(y, tg):
        err = y - tg
        part = 0.5 * jnp.sum(jnp.mean(err * err, axis=-1, keepdims=True), axis=0, keepdims=True)
        return err * (1.0 / d), jnp.broadcast_to(part, (1, LANES))

    ga, loss_acc = _rowwise("loss", loss_fn, [h, tgt], [], [(d, F32)], accs=[(1, LANES)])
    gb = None
    small_g = {k: [None] * depth for k in ("b_forget", "g_cq", "g_ckv", "ln1_g", "ln1_b", "ln2_g", "ln2_b")}
    for l in reversed(range(depth)):
        ga, gb, big, small = _layer_bwd(cfg, l, batch, ga, gb, saves[l], mem_b, rope_c, rope_s,
                                        comm.backward_hosts(l), **layers[l])
        comm.grads(l, big)
        for k, v in small.items():
            small_g[k][l] = v
    dx, _, dg_in, db_in = _ln_bwd("ln_in_bwd", cfg, ga, gb, x2, *ln_in)
    small_g = {k: jnp.stack(v) for k, v in small_g.items()}
    small_g["ln_in_g"], small_g["ln_in_b"] = dg_in[0], db_in[0]
    return loss_acc[0, 0], dx.reshape(x.shape), small_g


BIG = ("w_in", "w_uq", "w_ukv", "w_mem_kv", "w_br", "w_out", "w_ff1", "w_ff2")
SMALL = ("ln_in_g", "ln_in_b", "b_forget", "g_cq", "g_ckv", "ln1_g", "ln1_b", "ln2_g", "ln2_b")
ROW_CUT = ("w_mem_kv", "w_out", "w_ff2")


def _shard_2d(a):
    cols = a.shape[-1]
    rows = a.size // cols
    return a.reshape(2, rows // 2, cols)


def _full_from_slots(name, slots, shard_shape):
    parts = slots.reshape((N_CHIPS,) + shard_shape)
    axis = len(shard_shape) - (2 if name in ROW_CUT else 1)
    return jnp.concatenate([parts[i] for i in range(N_CHIPS)], axis=axis)


def _slots_from_full(name, full, shard_shape):
    axis = len(shard_shape) - (2 if name in ROW_CUT else 1)
    parts = jnp.stack(jnp.split(full, N_CHIPS, axis=axis))
    cols = shard_shape[-1]
    return parts.reshape(N_CHIPS, 2, -1, cols)


def _pack_small(cfg, vals):
    flat = jnp.concatenate([vals[k].reshape(-1).astype(F32) for k in SMALL])
    pad = (-flat.shape[0]) % (LANES * LANES)
    return jnp.pad(flat, (0, pad)).reshape(-1, LANES)


def _unpack_small(packed, like):
    flat, out, off = packed.reshape(-1), {}, 0
    for k in SMALL:
        n = like[k].size
        out[k] = flat[off:off + n].reshape(like[k].shape)
        off += n
    return out


class LayerComm:
    def __init__(self, cfg, w, m, v):
        self.cfg, self.w = cfg, w
        self.shards = [[_shard_2d(w[k][l].astype(BF16)) for k in BIG] for l in range(cfg.depth)]
        self.got = _gather_now(self.shards[0])
        self.next = None
        self.pending = None
        halves = lambda a: a.reshape(a.shape[0], 2, -1, a.shape[-1])
        self.state = {k: [halves(a[k]) for a in (w, m, v)] for k in BIG}
        self.outs = {k: None for k in BIG}

    def weights(self, l):
        if l > 0:
            self.got = self.next.got
        return {k: _full_from_slots(k, _to_absolute(g, s), self.w[k].shape[1:])
                for k, g, s in zip(BIG, self.got, self.shards[l])}

    def forward_hosts(self, l):
        self.next = GatherNext(self.shards[l + 1]) if l + 1 < self.cfg.depth else Hosts()
        return self.next

    def backward_hosts(self, l):
        return self.pending[1] if self.pending else Hosts()

    def _update(self):
        l, reduce = self.pending
        for k, (mine, theirs) in zip(BIG, reduce.result()):
            self.outs[k] = _adamw_layer(f"adamw_{k}_{l}", l, *self.state[k], mine, theirs, self.outs[k])

    def grads(self, l, big):
        if self.pending:
            self._update()
        rel = [_to_relative(_slots_from_full(k, big[k], self.w[k].shape[1:])) for k in BIG]
        self.pending = (l, ReduceLayer(l, rel))

    def finish(self):
        self.pending[1].run_now()
        self._update()
        return {k: tuple(a.reshape(self.w[k].shape) for a in self.outs[k]) for k in BIG}


def _step(cfg, x, mem, positions, target, w, m, v):
    comm = LayerComm(cfg, w, m, v)
    small_w = {k: w[k] for k in SMALL}
    loss_local, dx, small_g = _local_step(cfg, x, mem, positions, target, small_w, comm)
    loss = lax.psum(loss_local, ("x", "y", "c"))
    outs_big = comm.finish()

    g_small = _allreduce_small(_pack_small(cfg, small_g))
    packs = [_pack_small(cfg, {k: d_[k] for k in SMALL}) for d_ in (w, m, v)]
    dl, nm, nv = _slabwise("adamw_small", _adamw_math, [a[None] for a in (packs[0], g_small, packs[1], packs[2])],
                           [F32, F32, F32])
    outs_small = [_unpack_small(a[0] if a.ndim == 3 else a, w) for a in (g_small, dl, nm, nv)]

    names = SMALL[:2] + ("w_in", "b_forget", "w_uq", "g_cq", "w_ukv", "g_ckv", "w_mem_kv", "w_br", "w_out",
                         "ln1_g", "ln1_b", "w_ff1", "w_ff2", "ln2_g", "ln2_b")
    result = [loss, dx]
    for part in range(4):
        for k in names:
            result.append(outs_big[k][part] if k in outs_big else outs_small[part][k])
    return tuple(result)


def kernel(x, mem, positions, ln_in_g, ln_in_b, w_in, b_forget, w_uq, g_cq, w_ukv, g_ckv, w_mem_kv, w_br, w_out, ln1_g, ln1_b, w_ff1, w_ff2, ln2_g, ln2_b, loss_target, m_ln_in_g, m_ln_in_b, m_w_in, m_b_forget, m_w_uq, m_g_cq, m_w_ukv, m_g_ckv, m_w_mem_kv, m_w_br, m_w_out, m_ln1_g, m_ln1_b, m_w_ff1, m_w_ff2, m_ln2_g, m_ln2_b, v_ln_in_g, v_ln_in_b, v_w_in, v_b_forget, v_w_uq, v_g_cq, v_w_ukv, v_g_ckv, v_w_mem_kv, v_w_br, v_w_out, v_ln1_g, v_ln1_b, v_w_ff1, v_w_ff2, v_ln2_g, v_ln2_b):
    w = dict(ln_in_g=ln_in_g, ln_in_b=ln_in_b, w_in=w_in, b_forget=b_forget, w_uq=w_uq, g_cq=g_cq, w_ukv=w_ukv,
             g_ckv=g_ckv, w_mem_kv=w_mem_kv, w_br=w_br, w_out=w_out, ln1_g=ln1_g, ln1_b=ln1_b, w_ff1=w_ff1,
             w_ff2=w_ff2, ln2_g=ln2_g, ln2_b=ln2_b)
    m = dict(ln_in_g=m_ln_in_g, ln_in_b=m_ln_in_b, w_in=m_w_in, b_forget=m_b_forget, w_uq=m_w_uq, g_cq=m_g_cq,
             w_ukv=m_w_ukv, g_ckv=m_g_ckv, w_mem_kv=m_w_mem_kv, w_br=m_w_br, w_out=m_w_out, ln1_g=m_ln1_g,
             ln1_b=m_ln1_b, w_ff1=m_w_ff1, w_ff2=m_w_ff2, ln2_g=m_ln2_g, ln2_b=m_ln2_b)
    v = dict(ln_in_g=v_ln_in_g, ln_in_b=v_ln_in_b, w_in=v_w_in, b_forget=v_b_forget, w_uq=v_w_uq, g_cq=v_g_cq,
             w_ukv=v_w_ukv, g_ckv=v_g_ckv, w_mem_kv=v_w_mem_kv, w_br=v_w_br, w_out=v_w_out, ln1_g=v_ln1_g,
             ln1_b=v_ln1_b, w_ff1=v_w_ff1, w_ff2=v_w_ff2, ln2_g=v_ln2_g, ln2_b=v_ln2_b)
    return _step(Cfg(), x, mem, positions, loss_target, w, m, v)
```

```python
import functools
from typing import NamedTuple

import jax
import jax.numpy as jnp
from jax import lax
from jax.experimental import pallas as pl
from jax.experimental.pallas import tpu as pltpu

F32 = jnp.float32
BF16 = jnp.bfloat16
MESH = pl.DeviceIdType.MESH

LANES = 128
SUBLANES = 8
VMEM_BYTES = 64 * 1024 * 1024
N_CHIPS = 4
N_DEV = 8

FOX_DH = 64
MLA_NOPE = 64
MLA_ROPE = 32
MLA_V = 64
MEM_DH = 128
ROPE_BASE = 10000.0
LN_EPS = 1e-5
RMS_EPS = 1e-6
NEG_INF = -1e30
ATTN_TILE = 512

ADAM_LR = 0.001
ADAM_B1 = 0.9
ADAM_B2 = 0.999
ADAM_EPS = 1e-08
ADAM_WD = 0.01
ADAM_STEP = 10


class Cfg(NamedTuple):
    d: int = 1024
    depth: int = 4
    seq: int = 2048
    chunk: int = 64
    n_mem: int = 256
    fox_h: int = 8
    mla_h: int = 8
    q_rank: int = 384
    kv_rank: int = 256
    mem_h: int = 4
    d_ff: int = 4096

    @property
    def width(self):
        return self.fox_h * FOX_DH

    @property
    def alpha(self):
        return (2 * self.depth) ** 0.25

    @property
    def small_w(self):
        return LANES + self.q_rank + self.kv_rank + LANES

    @property
    def in_splits(self):
        return (3 * self.width, self.fox_h, self.q_rank, self.kv_rank, MLA_ROPE, self.width, 3 * self.d)


class Exchange(NamedTuple):
    ins: tuple
    out_shapes: tuple
    plan: object
    n_copies: int
    aliases: dict = {}


def _peer(rel):
    x, y, c = lax.axis_index("x"), lax.axis_index("y"), lax.axis_index("c")
    if rel == "c":
        return (x, y, 1 - c)
    return ((1 - x) if rel in (1, 3) else x, (1 - y) if rel in (2, 3) else y, c)


def _exchange_copies(ex, in_refs, out_refs, send_sems, recv_sems):
    planned = ex.plan(in_refs, out_refs)
    assert len(planned) == ex.n_copies, len(planned)
    return [pltpu.make_async_remote_copy(src_ref=src, dst_ref=dst, send_sem=send_sems.at[i], recv_sem=recv_sems.at[i],
                                         device_id=_peer(rel), device_id_type=MESH)
            for i, (src, dst, rel) in enumerate(planned)]


def _pcall(body, rider=None, **kw):
    if rider is not None:
        n_in, n_out, grid = len(kw["in_specs"]), len(kw["out_specs"]), kw["grid"]
        n_rin, n_rout = len(rider.ins), len(rider.out_shapes)
        host = body

        def body(*refs):
            ins, rins = refs[:n_in], refs[n_in:n_in + n_rin]
            outs = refs[n_in + n_rin:n_in + n_rin + n_out]
            routs = refs[n_in + n_rin + n_out:n_in + n_rin + n_out + n_rout]
            scratch = refs[n_in + n_rin + n_out + n_rout:-2]
            copies = _exchange_copies(rider, rins, routs, refs[-2], refs[-1])
            first = functools.reduce(jnp.logical_and, [pl.program_id(a) == 0 for a in range(len(grid))])
            last = functools.reduce(jnp.logical_and, [pl.program_id(a) == n - 1 for a, n in enumerate(grid)])

            @pl.when(first)
            def _():
                for cp in copies:
                    cp.start()

            host(*ins, *outs, *scratch)

            @pl.when(last)
            def _():
                for cp in copies:
                    cp.wait()

        any_spec = pl.BlockSpec(memory_space=pl.ANY)
        sems = [pltpu.SemaphoreType.DMA((rider.n_copies,))] * 2
        kw = dict(
            kw,
            in_specs=list(kw["in_specs"]) + [any_spec] * n_rin,
            out_specs=list(kw["out_specs"]) + [any_spec] * n_rout,
            out_shape=list(kw["out_shape"]) + list(rider.out_shapes),
            scratch_shapes=list(kw.get("scratch_shapes", ())) + sems,
            input_output_aliases={**kw.get("input_output_aliases", {}),
                                  **{n_in + i: n_out + o for i, o in rider.aliases.items()}},
        )
    call = pl.pallas_call(body, **kw)
    return lambda *ops: call(*[pltpu.with_memory_space_constraint(o, pltpu.HBM) for o in ops])


class Hosts:
    def rider(self, host):
        return None

    def done(self, host, outs):
        pass


def _hosted(hosts, host, call, ops, n_results):
    rider = hosts.rider(host) if hosts is not None else None
    if rider is None:
        return call(None)(*ops)
    res = call(rider)(*ops, *rider.ins)
    hosts.done(host, list(res[n_results:]))
    return res[:n_results]


def _nbytes(shape, dtype):
    n = 1
    for s in shape:
        n *= s
    return n * jnp.dtype(dtype).itemsize


def _tile(dim, target):
    if dim <= target:
        return dim
    t = target - target % LANES
    while t >= LANES:
        if dim % t == 0:
            return t
        t -= LANES
    return dim


def _params(block_bytes, scratch_bytes=0):
    est = 2 * block_bytes + scratch_bytes + 24 * 1024 * 1024
    return pltpu.CompilerParams(vmem_limit_bytes=int(min(max(est, 32 * 1024 * 1024), VMEM_BYTES - 4 * 1024 * 1024)))


_DIMS = {"nn": (((1,), (0,)), ((), ())), "nt": (((1,), (1,)), ((), ())), "tn": (((0,), (0,)), ((), ()))}


MM_TILE = 1024
MM_BLOCK_BYTES = 16 * 1024 * 1024


def _mm_tiles(mode, pairs, out_dtypes, m, n, tm, tn):
    fixed_m, fixed_n = tm is not None, tn is not None
    tm, tn = _tile(m, tm or MM_TILE), _tile(n, tn or MM_TILE)

    def block_bytes(tm_, tn_):
        total = sum(_nbytes((tm_, tn_), dt) for dt in out_dtypes)
        for a, b in pairs:
            k = a.shape[0] if mode == "tn" else a.shape[1]
            total += _nbytes((k, tm_), a.dtype) + _nbytes((k, tn_), b.dtype)
        return total

    while block_bytes(tm, tn) > MM_BLOCK_BYTES:
        if not fixed_m and tm >= tn and tm > 2 * LANES:
            tm = _tile(m, tm // 2)
        elif not fixed_n and tn > 2 * LANES:
            tn = _tile(n, tn // 2)
        elif not fixed_m and tm > 2 * LANES:
            tm = _tile(m, tm // 2)
        else:
            break
    return tm, tn


def _mm(name, mode, pairs, out_dtypes, tm=None, tn=None, epi=None, row_extras=(), bc_extras=(), hosts=None, host=None):
    a0, b0 = pairs[0]
    m = a0.shape[1] if mode == "tn" else a0.shape[0]
    n = b0.shape[0] if mode == "nt" else b0.shape[1]
    tm, tn = _mm_tiles(mode, pairs, out_dtypes, m, n, tm, tn)
    in_specs, ops, blk = [], [], 0
    for a, b in pairs:
        if mode == "tn":
            k = a.shape[0]
            sa, sha = pl.BlockSpec((k, tm), lambda i, j: (0, i)), (k, tm)
        else:
            k = a.shape[1]
            sa, sha = pl.BlockSpec((tm, k), lambda i, j: (i, 0)), (tm, k)
        if mode == "nt":
            sb, shb = pl.BlockSpec((tn, k), lambda i, j: (j, 0)), (tn, k)
        else:
            sb, shb = pl.BlockSpec((k, tn), lambda i, j: (0, j)), (k, tn)
        in_specs += [sa, sb]
        ops += [a, b]
        blk += _nbytes(sha, a.dtype) + _nbytes(shb, b.dtype)
    for e in row_extras:
        w = e.shape[1]
        if w == n:
            in_specs.append(pl.BlockSpec((tm, tn), lambda i, j: (i, j)))
            blk += _nbytes((tm, tn), e.dtype)
        else:
            in_specs.append(pl.BlockSpec((tm, w), lambda i, j: (i, 0)))
            blk += _nbytes((tm, w), e.dtype)
        ops.append(e)
    for e in bc_extras:
        r, w = e.shape
        if w == n:
            in_specs.append(pl.BlockSpec((r, tn), lambda i, j: (0, j)))
        else:
            in_specs.append(pl.BlockSpec((r, w), lambda i, j: (0, 0)))
        blk += _nbytes((r, w), e.dtype)
        ops.append(e)
    npairs, nrow, nbc, nout = len(pairs), len(row_extras), len(bc_extras), len(out_dtypes)
    dims = _DIMS[mode]

    def body(*refs):
        acc = None
        for p in range(npairs):
            a = refs[2 * p][...].astype(BF16)
            b = refs[2 * p + 1][...].astype(BF16)
            d = lax.dot_general(a, b, dims, preferred_element_type=F32)
            acc = d if acc is None else acc + d
        ex = [r[...] for r in refs[2 * npairs:2 * npairs + nrow + nbc]]
        outs = (acc,) if epi is None else epi(acc, *ex)
        for o_ref, o in zip(refs[2 * npairs + nrow + nbc:], outs):
            o_ref[...] = o.astype(o_ref.dtype)

    blk += sum(_nbytes((tm, tn), dt) for dt in out_dtypes) + 2 * _nbytes((tm, tn), F32)
    call = lambda rider: _pcall(
        body,
        rider=rider,
        name=name,
        grid=(m // tm, n // tn),
        in_specs=in_specs,
        out_specs=[pl.BlockSpec((tm, tn), lambda i, j: (i, j)) for _ in range(nout)],
        out_shape=[jax.ShapeDtypeStruct((m, n), dt) for dt in out_dtypes],
        compiler_params=_params(blk),
    )
    res = _hosted(hosts, host, call, ops, nout)
    return res[0] if nout == 1 else res


def _rowwise(name, fn, row_ins, bc_ins, outs, accs=(), tm=256):
    rows = row_ins[0].shape[0]
    tm = min(tm, rows)
    assert rows % tm == 0
    nrow, nbc, nout, nacc = len(row_ins), len(bc_ins), len(outs), len(accs)
    in_specs = [pl.BlockSpec((tm, a.shape[1]), lambda i: (i, 0)) for a in row_ins]
    in_specs += [pl.BlockSpec(a.shape, lambda i: (0, 0)) for a in bc_ins]
    out_specs = [pl.BlockSpec((tm, w), lambda i: (i, 0)) for w, _ in outs]
    out_specs += [pl.BlockSpec(s, lambda i: (0, 0)) for s in accs]
    out_shape = [jax.ShapeDtypeStruct((rows, w), dt) for w, dt in outs]
    out_shape += [jax.ShapeDtypeStruct(s, F32) for s in accs]

    def body(*refs):
        vals = fn(*[r[...] for r in refs[:nrow + nbc]])
        o_refs = refs[nrow + nbc:]
        for r, v in zip(o_refs[:nout], vals[:nout]):
            r[...] = v.astype(r.dtype)
        if nacc:
            @pl.when(pl.program_id(0) == 0)
            def _():
                for r in o_refs[nout:]:
                    r[...] = jnp.zeros(r.shape, F32)

            for r, v in zip(o_refs[nout:], vals[nout:]):
                r[...] += v

    blk = sum(_nbytes((tm, a.shape[1]), a.dtype) for a in row_ins) + sum(_nbytes(a.shape, a.dtype) for a in bc_ins)
    blk += sum(_nbytes((tm, w), dt) for w, dt in outs) + sum(_nbytes(s, F32) for s in accs)
    res = _pcall(
        body,
        name=name,
        grid=(rows // tm,),
        in_specs=in_specs,
        out_specs=out_specs,
        out_shape=out_shape,
        compiler_params=_params(2 * blk),
    )(*row_ins, *bc_ins)
    return res


def _ln(z, g, b):
    mu = jnp.mean(z, axis=-1, keepdims=True)
    zc = z - mu
    var = jnp.mean(zc * zc, axis=-1, keepdims=True)
    return zc * lax.rsqrt(var + LN_EPS) * g + b


def _rms(x, g):
    return x * lax.rsqrt(jnp.mean(x * x, axis=-1, keepdims=True) + RMS_EPS) * g


def _colsum(v):
    return jnp.sum(v, axis=0, keepdims=True)


def _rope_swap(x):
    w = x.shape[1]
    lane = lax.broadcasted_iota(jnp.int32, (1, w), 1) % LANES
    from_left = pltpu.roll(x, 16, 1)
    from_right = pltpu.roll(x, w - 16, 1)
    lo = (lane >= MLA_NOPE) & (lane < MLA_NOPE + 16)
    hi = (lane >= MLA_NOPE + 16) & (lane < MLA_NOPE + 32)
    return jnp.where(hi, from_left, jnp.where(lo, from_right, 0.0))


def _rope(x, cos_t, sin_t):
    nh = x.shape[1] // LANES
    ct, st = jnp.tile(cos_t, (1, nh)), jnp.tile(sin_t, (1, nh))
    return x * ct + _rope_swap(x) * st


def _rope_t(dy, cos_t, sin_t):
    nh = dy.shape[1] // LANES
    ct, st = jnp.tile(cos_t, (1, nh)), jnp.tile(sin_t, (1, nh))
    return dy * ct + _rope_swap(dy * st)


def _block_cumsum(v, carry, reverse):
    tb = v.shape[0]
    r = lax.broadcasted_iota(jnp.int32, (tb, tb), 0)
    c = lax.broadcasted_iota(jnp.int32, (tb, tb), 1)
    tri = jnp.where((c >= r) if reverse else (c <= r), 1.0, 0.0).astype(BF16)
    hi = v.astype(BF16)
    r1 = v - hi.astype(F32)
    mid = r1.astype(BF16)
    lo = (r1 - mid.astype(F32)).astype(BF16)
    out = carry + sum(jnp.dot(tri, p, preferred_element_type=F32) for p in (hi, mid, lo))
    return out, (out[0:1, :] if reverse else out[tb - 1:tb, :])


def _forget_cumsum(name, logf, batch, seq, heads, tb):
    nb = seq // tb

    def body(x_ref, keys_ref, rows_ref, carry):
        @pl.when(pl.program_id(1) == 0)
        def _():
            carry[...] = jnp.zeros(carry.shape, F32)

        out, carry[...] = _block_cumsum(x_ref[...], carry[...], False)
        keys_ref[...] = jnp.concatenate([jnp.broadcast_to(out[:, h:h + 1], (tb, LANES)) for h in range(heads)], axis=1)
        out_t = out.T
        for g in range(heads // 2):
            rows_ref[g] = out_t[2 * g:2 * g + 2, :]

    return _pcall(
        body,
        name=name,
        grid=(batch, nb),
        in_specs=[pl.BlockSpec((tb, LANES), lambda b, i: (b * nb + i, 0))],
        out_specs=[pl.BlockSpec((tb, heads * LANES), lambda b, i: (b * nb + i, 0)),
                   pl.BlockSpec((None, None, heads // 2, 2, tb), lambda b, i: (b, i, 0, 0, 0))],
        out_shape=[jax.ShapeDtypeStruct((batch * seq, heads * LANES), F32),
                   jax.ShapeDtypeStruct((batch, nb, heads // 2, 2, tb), F32)],
        scratch_shapes=[pltpu.VMEM((1, LANES), F32)],
        compiler_params=_params(4 * tb * (heads + 2) * LANES * 4),
    )(logf)


def _forget_cumsum_bwd(name, dcf, batch, seq, heads, tb):
    nb = seq // tb

    def body(x_ref, o_ref, carry):
        @pl.when(pl.program_id(1) == 0)
        def _():
            carry[...] = jnp.zeros(carry.shape, F32)

        lane = lax.broadcasted_iota(jnp.int32, (1, LANES), 1)
        v = jnp.zeros((tb, LANES), F32)
        for g in range(heads // 2):
            blk = x_ref[:, g * LANES:(g + 1) * LANES]
            moved = pltpu.roll(blk, 2 * g, 1) if g else blk
            v = v + jnp.where((lane >= 2 * g) & (lane < 2 * g + 2), moved, 0.0)
        o_ref[...], carry[...] = _block_cumsum(v, carry[...], True)

    return _pcall(
        body,
        name=name,
        grid=(batch, nb),
        in_specs=[pl.BlockSpec((tb, (heads // 2) * LANES), lambda b, i: (b * nb + nb - 1 - i, 0))],
        out_specs=pl.BlockSpec((tb, LANES), lambda b, i: (b * nb + nb - 1 - i, 0)),
        out_shape=jax.ShapeDtypeStruct((batch * seq, LANES), F32),
        scratch_shapes=[pltpu.VMEM((1, LANES), F32)],
        compiler_params=_params(4 * tb * (heads // 2 + 1) * LANES * 4),
    )(dcf)


class Attn(NamedTuple):
    batch: int
    sq: int
    sk: int
    groups: int
    hq: int
    hv: int
    mode: str
    scale: float
    chunk: int
    tq: int
    tk: int

    @property
    def hg(self):
        return self.hv

    @property
    def qw(self):
        return LANES * self.hg // self.hq

    @property
    def dv(self):
        return LANES // self.hv


def _head_lanes(j, dv):
    lane = lax.broadcasted_iota(jnp.int32, (1, LANES), 1)
    return (lane >= j * dv) & (lane < (j + 1) * dv)


def _head_q(sp, j, q_blk):
    if sp.hq == 2:
        return jnp.where(_head_lanes(j, FOX_DH), q_blk, jnp.zeros_like(q_blk))
    return q_blk[:, LANES * j:LANES * (j + 1)]


def _head_rows(sp, j):
    return slice(j * sp.dv, (j + 1) * sp.dv) if sp.hg == 2 else slice(None)


def _scores_t(sp, j, k_c, q_j, cfq_rows, cfk_rep, k0, q0, masked):
    tk, tq = k_c.shape[0], q_j.shape[0]
    k_j = k_c if sp.hq == 2 else k_c[:, LANES * j:LANES * (j + 1)]
    st = lax.dot_general(k_j, q_j, _DIMS["nt"], preferred_element_type=F32) * sp.scale
    if sp.mode == "fox":
        st = st + (cfq_rows[j:j + 1, :] - jnp.tile(cfk_rep[:, LANES * j:LANES * (j + 1)], (1, tq // LANES)))
    if masked:
        kidx = k0 + lax.broadcasted_iota(jnp.int32, (tk, tq), 0)
        qidx = q0 + lax.broadcasted_iota(jnp.int32, (tk, tq), 1)
        if sp.mode == "chunk":
            shift = sp.chunk.bit_length() - 1
            kidx, qidx = jnp.right_shift(kidx, shift), jnp.right_shift(qidx, shift)
        st = jnp.where(kidx <= qidx, st, NEG_INF)
    return st


def _attn_fwd(name, sp, q, k, v, cfq=None, cfk=None, hosts=None, host=None):
    (qa, qo), (ka, ko), (va, vo) = q, k, v
    tq, tk, hg, qw = sp.tq, sp.tk, sp.hg, sp.qw
    nqb, nkc = sp.sq // tq, sp.sk // tk
    fox, causal = sp.mode == "fox", sp.mode != "none"
    assert sp.sq % tq == 0 and sp.sk % tk == 0 and (not causal or (tq == tk and sp.sq == sp.sk))

    def body(*refs):
        if fox:
            q_ref, k_ref, v_ref, cfq_ref, cfk_ref, o_ref, lse_ref, acc_scr = refs
        else:
            q_ref, k_ref, v_ref, o_ref, lse_ref, acc_scr = refs
        i = pl.program_id(2)
        q0 = i * tq
        q_blk = q_ref[...]
        qs = [_head_q(sp, j, q_blk) for j in range(hg)]
        acc_scr[...] = jnp.zeros(acc_scr.shape, F32)

        def chunk(kc, carry, masked):
            ms, ls = carry
            k0 = pl.multiple_of(kc * tk, tk)
            k_c = k_ref[pl.ds(k0, tk), :]
            v_c = v_ref[pl.ds(k0, tk), :]
            new_m, new_l = [], []
            for j in range(hg):
                st = _scores_t(sp, j, k_c, qs[j], cfq_ref[...] if fox else None,
                               cfk_ref[pl.ds(k0, tk), :] if fox else None, k0, q0, masked)
                m_new = jnp.maximum(ms[j], jnp.max(st, axis=0, keepdims=True))
                alpha = jnp.exp(ms[j] - m_new)
                pt = jnp.exp(st - m_new)
                new_m.append(m_new)
                new_l.append(alpha * ls[j] + jnp.sum(pt, axis=0, keepdims=True))
                pv = lax.dot_general(v_c, pt.astype(BF16), _DIMS["tn"], preferred_element_type=F32)
                r = _head_rows(sp, j)
                acc_scr[r, :] = acc_scr[r, :] * alpha + pv[r, :]
            return tuple(new_m), tuple(new_l)

        carry = (tuple(jnp.full((1, tq), NEG_INF, F32) for _ in range(hg)),
                 tuple(jnp.zeros((1, tq), F32) for _ in range(hg)))
        if causal:
            carry = lax.fori_loop(0, i, functools.partial(chunk, masked=False), carry)
            ms, ls = chunk(i, carry, True)
        else:
            ms, ls = lax.fori_loop(0, nkc, functools.partial(chunk, masked=False), carry)
        for j in range(hg):
            r = _head_rows(sp, j)
            acc_scr[r, :] = acc_scr[r, :] / ls[j]
            lse_ref[j:j + 1, :] = ms[j] + jnp.log(ls[j])
        o_ref[...] = acc_scr[...].T

    in_specs = [
        pl.BlockSpec((tq, qw), lambda b, g, i: (b * nqb + i, qo + g)),
        pl.BlockSpec((sp.sk, qw), lambda b, g, i: (b, ko + g)),
        pl.BlockSpec((sp.sk, LANES), lambda b, g, i: (b, vo + g)),
    ]
    ops = [qa, ka, va]
    stat_blk = pl.BlockSpec((None, None, None, hg, tq), lambda b, g, i: (b, g, i, 0, 0))
    if fox:
        in_specs += [pl.BlockSpec((None, None, None, hg, tq), lambda b, g, i: (b, i, g, 0, 0)),
                     pl.BlockSpec((sp.sk, hg * LANES), lambda b, g, i: (b, g))]
        ops += [cfq, cfk]
    blk = _nbytes((tq, qw), BF16) + _nbytes((sp.sk, qw + LANES), BF16) + 2 * _nbytes((tq, LANES), F32)
    blk += _nbytes((sp.sk, hg * LANES), F32) + 6 * _nbytes((tk, tq), F32)
    call = lambda rider: _pcall(
        body,
        rider=rider,
        name=name,
        grid=(sp.batch, sp.groups, nqb),
        in_specs=in_specs,
        out_specs=[pl.BlockSpec((tq, LANES), lambda b, g, i: (b * nqb + i, g)), stat_blk],
        out_shape=[
            jax.ShapeDtypeStruct((sp.batch * sp.sq, sp.groups * LANES), F32),
            jax.ShapeDtypeStruct((sp.batch, sp.groups, nqb, hg, tq), F32),
        ],
        scratch_shapes=[pltpu.VMEM((LANES, tq), F32)],
        compiler_params=_params(blk, tq * LANES * 4),
    )
    return _hosted(hosts, host, call, ops, 2)


def _attn_bwd(name, sp, q, k, v, o, lse, do, cfq=None, cfk=None, hosts=None, host=None):
    (qa, qo), (ka, ko), (va, vo) = q, k, v
    tq, tk, hg, qw, dv = sp.tq, sp.tk, sp.hg, sp.qw, sp.dv
    nqb, nkb = sp.sq // tq, sp.sk // tk
    fox, causal = sp.mode == "fox", sp.mode != "none"
    assert sp.sq % tq == 0 and sp.sk % tk == 0 and (not causal or (tq == tk and sp.sq == sp.sk))

    def body(*refs):
        if fox:
            (q_ref, k_ref, v_ref, lse_ref, do_ref, cfq_ref, cfk_ref, kall_ref, vall_ref, cfkall_ref,
             dq_ref, dk_ref, dv_ref, dcf_ref, delta_scr, dk_scr, dv_scr, dqt_scr, dcf_scr) = refs
        else:
            (q_ref, k_ref, v_ref, o_ref, lse_ref, do_ref,
             dq_ref, dk_ref, dv_ref, delta_scr, dk_scr, dv_scr, dqt_scr) = refs
        kb = pl.program_id(2)
        k0 = kb * tk
        heads = [_head_lanes(j, dv) for j in range(hg)]

        def head_do(j, do_c):
            return jnp.where(heads[j], do_c, jnp.zeros_like(do_c)) if hg == 2 else do_c

        def probs_t(j, k_c, v_c, q_c, do_c, i, cf_keys, c0, masked):
            st = _scores_t(sp, j, k_c, _head_q(sp, j, q_c), cfq_ref[i] if fox else None, cf_keys, c0, i * tq, masked)
            pt = jnp.exp(st - lse_ref[i][j:j + 1, :])
            dpt = lax.dot_general(v_c, head_do(j, do_c), _DIMS["nt"], preferred_element_type=F32)
            return pt, dpt

        @pl.when(kb == 0)
        def _():
            dqt_scr[...] = jnp.zeros(dqt_scr.shape, F32)

            def fill(i, carry):
                r0 = pl.multiple_of(i * tq, tq)
                do_c = do_ref[pl.ds(r0, tq), :]
                if fox:
                    q_c = q_ref[pl.ds(r0, tq), :]

                    def keys(kc, acc, masked):
                        c0 = pl.multiple_of(kc * tk, tk)
                        out = []
                        for j in range(hg):
                            pt, dpt = probs_t(j, kall_ref[pl.ds(c0, tk), :], vall_ref[pl.ds(c0, tk), :], q_c, do_c, i,
                                              cfkall_ref[pl.ds(c0, tk), :], c0, masked)
                            out.append(acc[j] + jnp.sum(pt * dpt, axis=0, keepdims=True))
                        return tuple(out)

                    d = lax.fori_loop(0, i, functools.partial(keys, masked=False),
                                      tuple(jnp.zeros((1, tq), F32) for _ in range(hg)))
                    d = keys(i, d, True)
                    for j in range(hg):
                        delta_scr[i, j:j + 1, :] = d[j]
                else:
                    prod_t = (do_c.astype(F32) * o_ref[pl.ds(r0, tq), :]).T
                    for j in range(hg):
                        delta_scr[i, j:j + 1, :] = jnp.sum(prod_t[_head_rows(sp, j), :], axis=0, keepdims=True)
                return carry

            lax.fori_loop(0, nqb, fill, 0)

        k_blk = k_ref[...]
        v_blk = v_ref[...]
        k_t = k_blk.astype(F32).T.astype(BF16)
        dk_scr[...] = jnp.zeros(dk_scr.shape, F32)
        dv_scr[...] = jnp.zeros(dv_scr.shape, F32)
        if fox:
            dcf_scr[...] = jnp.zeros(dcf_scr.shape, F32)

        def qblock(i, carry, masked):
            r0 = pl.multiple_of(i * tq, tq)
            q_c = q_ref[pl.ds(r0, tq), :]
            do_c = do_ref[pl.ds(r0, tq), :]
            for j in range(hg):
                pt, dpt = probs_t(j, k_blk, v_blk, q_c, do_c, i, cfk_ref[...] if fox else None, k0, masked)
                dst = pt * (dpt - delta_scr[i][j:j + 1, :])
                if fox:
                    part = dst[:, :LANES]
                    for t in range(1, tq // LANES):
                        part = part + dst[:, t * LANES:(t + 1) * LANES]
                    dcf_scr[j] += part
                ds_b = (dst * sp.scale).astype(BF16)
                dv_scr[j] += jnp.dot(pt.astype(BF16), do_c, preferred_element_type=F32)
                dk_scr[j] += jnp.dot(ds_b, q_c if sp.hq == 2 else _head_q(sp, j, q_c), preferred_element_type=F32)
                if sp.hq == 2:
                    r = pl.ds(j * FOX_DH, FOX_DH)
                    dqt_scr[i, r, :] += jnp.dot(k_t[j * FOX_DH:(j + 1) * FOX_DH, :], ds_b, preferred_element_type=F32)
                else:
                    r = pl.ds(j * LANES, LANES)
                    dqt_scr[i, r, :] += jnp.dot(k_t[j * LANES:(j + 1) * LANES, :], ds_b, preferred_element_type=F32)
            return carry

        if causal:
            qblock(kb, 0, True)
            lax.fori_loop(kb + 1, nqb, functools.partial(qblock, masked=False), 0)
        else:
            lax.fori_loop(0, nqb, functools.partial(qblock, masked=False), 0)

        @pl.when(kb == nkb - 1)
        def _():
            def untranspose(i, carry):
                dq_ref[pl.ds(pl.multiple_of(i * tq, tq), tq), :] = dqt_scr[i].T
                return carry

            lax.fori_loop(0, nqb, untranspose, 0)

        if hg == 2:
            dv_ref[...] = jnp.where(heads[0], dv_scr[0], dv_scr[1]).astype(dv_ref.dtype)
        else:
            dv_ref[...] = dv_scr[0].astype(dv_ref.dtype)
        if sp.hq == 2:
            dk_ref[...] = jnp.where(_head_lanes(0, FOX_DH), dk_scr[0], dk_scr[1]).astype(dk_ref.dtype)
        elif hg == 2:
            dk_ref[...] = jnp.concatenate([dk_scr[0], dk_scr[1]], axis=1).astype(dk_ref.dtype)
        else:
            dk_ref[...] = dk_scr[0].astype(dk_ref.dtype)
        if fox:
            lane = lax.broadcasted_iota(jnp.int32, (1, LANES), 1)
            sums = [jnp.sum(dcf_scr[j], axis=1, keepdims=True) for j in range(hg)]
            dcf_ref[...] = jnp.where(lane == 0, -sums[0], jnp.where(lane == 1, -sums[1], 0.0))

    seq_lanes = lambda b, g, kb: (b, g)
    key_blk = lambda b, g, kb: (b * nkb + kb, g)
    stats = pl.BlockSpec((None, None, nqb, hg, tq), lambda b, g, kb: (b, g, 0, 0, 0))
    in_specs = [
        pl.BlockSpec((sp.sq, qw), lambda b, g, kb: (b, qo + g)),
        pl.BlockSpec((tk, qw), lambda b, g, kb: (b * nkb + kb, ko + g)),
        pl.BlockSpec((tk, LANES), lambda b, g, kb: (b * nkb + kb, vo + g)),
    ]
    ops = [qa, ka, va]
    if not fox:
        in_specs.append(pl.BlockSpec((sp.sq, LANES), seq_lanes))
        ops.append(o)
    in_specs += [stats, pl.BlockSpec((sp.sq, LANES), seq_lanes)]
    ops += [lse, do]
    out_specs = [pl.BlockSpec((sp.sq, qw), seq_lanes), pl.BlockSpec((tk, qw), key_blk), pl.BlockSpec((tk, LANES), key_blk)]
    out_shape = [
        jax.ShapeDtypeStruct((sp.batch * sp.sq, sp.groups * qw), F32),
        jax.ShapeDtypeStruct((sp.batch * sp.sk, sp.groups * qw), BF16),
        jax.ShapeDtypeStruct((sp.batch * sp.sk, sp.groups * LANES), BF16),
    ]
    scratch = [pltpu.VMEM((nqb, hg, tq), F32), pltpu.VMEM((hg, tk, LANES), F32), pltpu.VMEM((hg, tk, LANES), F32),
               pltpu.VMEM((nqb, qw, tq), F32)]
    if fox:
        in_specs += [
            pl.BlockSpec((None, nqb, None, hg, tq), lambda b, g, kb: (b, 0, g, 0, 0)),
            pl.BlockSpec((tk, hg * LANES), key_blk),
            pl.BlockSpec((sp.sk, qw), lambda b, g, kb: (b, ko + g)),
            pl.BlockSpec((sp.sk, LANES), lambda b, g, kb: (b, vo + g)),
            pl.BlockSpec((sp.sk, hg * LANES), seq_lanes),
        ]
        ops += [cfq, cfk, ka, va, cfk]
        out_specs.append(pl.BlockSpec((tk, LANES), key_blk))
        out_shape.append(jax.ShapeDtypeStruct((sp.batch * sp.sk, sp.groups * LANES), F32))
        scratch.append(pltpu.VMEM((hg, tk, LANES), F32))
    blk = _nbytes((sp.sq, qw), BF16) + _nbytes((sp.sq, LANES), BF16) + 2 * _nbytes((sp.sq, LANES), F32)
    blk += _nbytes((sp.sq, qw), F32) + 4 * _nbytes((tk, qw), BF16) + 8 * _nbytes((tq, tk), F32)
    blk += (_nbytes((sp.sk, qw + LANES), BF16) + _nbytes((sp.sk, hg * LANES), F32)) if fox else 0
    call = lambda rider: _pcall(
        body,
        rider=rider,
        name=name,
        grid=(sp.batch, sp.groups, nkb),
        in_specs=in_specs,
        out_specs=out_specs,
        out_shape=out_shape,
        scratch_shapes=scratch,
        compiler_params=_params(blk, _nbytes((sp.sq, LANES), F32) + 4 * _nbytes((tk, LANES), F32)),
    )
    return _hosted(hosts, host, call, ops, len(out_shape))


def _slabwise(name, fn, ins, out_dtypes, rows_per_step=512):
    ins = [a if isinstance(a, tuple) else (a, None) for a in ins]
    n = max(1 if fixed is not None else a.shape[0] for a, fixed in ins)
    rows, cols = ins[0][0].shape[1:]
    tr = min(rows_per_step, rows)
    while rows % tr:
        tr //= 2
    assert tr % 16 == 0 or tr == rows, (name, rows, tr)

    def spec(a, fixed):
        if fixed is not None or a.shape[0] == 1:
            return pl.BlockSpec((None, tr, cols), lambda s, i: (fixed or 0, i, 0))
        return pl.BlockSpec((None, tr, cols), lambda s, i: (s, i, 0))

    def body(*refs):
        vals = fn(*[r[...] for r in refs[:len(ins)]])
        for r, v in zip(refs[len(ins):], vals):
            r[...] = v.astype(r.dtype)

    blk = (len(ins) + len(out_dtypes)) * _nbytes((tr, cols + LANES), F32)
    res = _pcall(
        body,
        name=name,
        grid=(n, rows // tr),
        in_specs=[spec(a, fixed) for a, fixed in ins],
        out_specs=[pl.BlockSpec((None, tr, cols), lambda s, i: (s, i, 0)) for _ in out_dtypes],
        out_shape=[jax.ShapeDtypeStruct((n, rows, cols), dt) for dt in out_dtypes],
        compiler_params=_params(2 * blk),
    )(*[a for a, _ in ins])
    return res


def _adamw_math(w, g, m, v):
    m = ADAM_B1 * m + (1.0 - ADAM_B1) * g
    v = ADAM_B2 * v + (1.0 - ADAM_B2) * jnp.square(g)
    m_hat = m / (1.0 - ADAM_B1 ** ADAM_STEP)
    v_hat = v / (1.0 - ADAM_B2 ** ADAM_STEP)
    delta = -ADAM_LR * (m_hat / (jnp.sqrt(v_hat) + ADAM_EPS) + ADAM_WD * w)
    return delta, m, v


def _run_exchange(name, ex):
    n_in, n_out = len(ex.ins), len(ex.out_shapes)

    def body(*refs):
        copies = _exchange_copies(ex, refs[:n_in], refs[n_in:n_in + n_out], refs[-2], refs[-1])
        for cp in copies:
            cp.start()
        for cp in copies:
            cp.wait()

    any_spec = pl.BlockSpec(memory_space=pl.ANY)
    return _pcall(
        body,
        name=name,
        in_specs=[any_spec] * n_in,
        out_specs=[any_spec] * n_out,
        out_shape=list(ex.out_shapes),
        scratch_shapes=[pltpu.SemaphoreType.DMA((ex.n_copies,))] * 2,
        input_output_aliases=dict(ex.aliases),
    )(*ex.ins)


def _chip(rel=0):
    x, y = lax.axis_index("x"), lax.axis_index("y")
    return 2 * ((1 - x) if rel & 1 else x) + ((1 - y) if rel & 2 else y)


def _gather_ici(shards):
    def plan(in_refs, out_refs):
        c = lax.axis_index("c")
        return [(s.at[c], g.at[_chip(), c], rel) for s, g in zip(in_refs, out_refs) for rel in (1, 2, 3)]

    shapes = tuple(jax.ShapeDtypeStruct((N_CHIPS,) + s.shape, s.dtype) for s in shards)
    return Exchange(tuple(shards), shapes, plan, 3 * len(shards))


def _gather_d2d(got):
    def plan(in_refs, out_refs):
        c = lax.axis_index("c")
        return [(g_in.at[_chip(rel), c], g_out.at[_chip(rel), c], "c")
                for g_in, g_out in zip(in_refs, out_refs) for rel in (1, 2, 3)]

    shapes = tuple(jax.ShapeDtypeStruct(g.shape, g.dtype) for g in got)
    return Exchange(tuple(got), shapes, plan, 3 * len(got), {i: i for i in range(len(got))})


GATHER_FIRST = 5


class GatherNext(Hosts):
    def __init__(self, shards):
        self.shards, self.got = shards, []

    def rider(self, host):
        if host == "fox_fwd":
            return _gather_ici(self.shards[:GATHER_FIRST])
        if host == "mla_fwd":
            return _gather_ici(self.shards[GATHER_FIRST:])
        if host == "mem_fwd":
            return _gather_d2d(self.got)
        return None

    def done(self, host, outs):
        self.got = outs if host == "mem_fwd" else self.got + outs


def _gather_now(shards):
    got = _run_exchange("gather_weights_ici", _gather_ici(shards))
    return _run_exchange("gather_weights_d2d", _gather_d2d(got))


def _pair_sum(name, g, recv, rows_per_step=512):
    _, _, rows, cols = g.shape
    tr = min(rows_per_step, rows)
    while rows % tr:
        tr //= 2

    def body(g_ref, r_ref, p_ref, own_ref):
        mine = jnp.where(lax.axis_index("c") == 0, g_ref[0], g_ref[1])
        p = mine.astype(F32) + r_ref[...].astype(F32)
        p_ref[...] = p.astype(p_ref.dtype)

        @pl.when(pl.program_id(1) == _chip())
        def _():
            own_ref[...] = p

    slab = pl.BlockSpec((None, tr, cols), lambda i, s: (s, i, 0))
    return _pcall(
        body,
        name=name,
        grid=(rows // tr, N_CHIPS),
        in_specs=[pl.BlockSpec((None, 2, tr, cols), lambda i, s: (s, 0, i, 0)), slab],
        out_specs=[slab, pl.BlockSpec((None, tr, cols), lambda i, s: (0, i, 0))],
        out_shape=[jax.ShapeDtypeStruct((N_CHIPS, rows, cols), BF16), jax.ShapeDtypeStruct((1, rows, cols), F32)],
        compiler_params=_params(2 * 6 * _nbytes((tr, cols + LANES), F32)),
    )(g, recv)


class ReduceLayer(Hosts):
    def __init__(self, tag, grads):
        self.tag, self.grads, self.pair, self.own, self.total, self.theirs = tag, grads, None, None, None, None

    def _swap_halves(self):
        def plan(in_refs, out_refs):
            c = lax.axis_index("c")
            return [(g.at[pl.ds(0, N_CHIPS), 1 - c], r, "c") for g, r in zip(in_refs, out_refs)]

        shapes = tuple(jax.ShapeDtypeStruct((N_CHIPS,) + g.shape[2:], g.dtype) for g in self.grads)
        return Exchange(tuple(self.grads), shapes, plan, len(self.grads))

    def _to_chips(self):
        def plan(in_refs, out_refs):
            return [(p.at[_chip(rel)], r.at[rel - 1], rel) for p, r in zip(in_refs, out_refs) for rel in (1, 2, 3)]

        shapes = tuple(jax.ShapeDtypeStruct((3,) + p.shape[1:], p.dtype) for p in self.pair)
        return Exchange(tuple(self.pair), shapes, plan, 3 * len(self.pair))

    def _share(self):
        def plan(in_refs, out_refs):
            return [(t, r, "c") for t, r in zip(in_refs, out_refs)]

        shapes = tuple(jax.ShapeDtypeStruct(t.shape, F32) for t in self.total)
        return Exchange(tuple(self.total), shapes, plan, len(self.total))

    def rider(self, host):
        stages = {"ff2_bwd_x": self._swap_halves, "fox_bwd": self._to_chips, "mla_bwd": self._share}
        return stages[host]() if host in stages else None

    def done(self, host, outs):
        if host == "ff2_bwd_x":
            sums = [_pair_sum(f"reduce_pair_sum_{self.tag}_{i}", g, r) for i, (g, r) in enumerate(zip(self.grads, outs))]
            self.pair, self.own = [s[0] for s in sums], [s[1] for s in sums]
        elif host == "fox_bwd":
            self.total = [
                _slabwise(f"reduce_chip_sum_{self.tag}_{i}",
                          lambda a, b, c_, d: (a + b.astype(F32) + c_.astype(F32) + d.astype(F32),),
                          [own, (r, 0), (r, 1), (r, 2)], [F32])[0]
                for i, (own, r) in enumerate(zip(self.own, outs))]
        else:
            self.theirs = outs

    def run_now(self):
        self.done("ff2_bwd_x", _run_exchange(f"reduce_pair_{self.tag}", self._swap_halves()))
        self.done("fox_bwd", _run_exchange(f"reduce_chips_{self.tag}", self._to_chips()))
        self.done("mla_bwd", _run_exchange(f"reduce_share_{self.tag}", self._share()))

    def result(self):
        return list(zip(self.total, self.theirs))


def _adamw_layer(name, l, w, m, v, mine, theirs, prev, rows_per_step=256):
    _, _, rows, cols = w.shape
    tr = min(rows_per_step, rows)
    while rows % tr:
        tr //= 2

    def body(w_ref, m_ref, v_ref, mine_ref, theirs_ref, *rest):
        g_ref, d_ref, nm_ref, nv_ref = rest[-4:]
        g = jnp.where(pl.program_id(0) == lax.axis_index("c"), mine_ref[...], theirs_ref[...])
        d, nm, nv = _adamw_math(w_ref[...], g, m_ref[...], v_ref[...])
        g_ref[...], d_ref[...], nm_ref[...], nv_ref[...] = g, d, nm, nv

    half = pl.BlockSpec((None, None, tr, cols), lambda h, i: (l, h, i, 0))
    one = pl.BlockSpec((None, tr, cols), lambda h, i: (0, i, 0))
    kept = [] if prev is None else list(prev)
    return _pcall(
        body,
        name=name,
        grid=(2, rows // tr),
        in_specs=[half, half, half, one, one] + [pl.BlockSpec(memory_space=pl.ANY)] * len(kept),
        out_specs=[half] * 4,
        out_shape=[jax.ShapeDtypeStruct(w.shape, F32)] * 4,
        input_output_aliases={5 + i: i for i in range(len(kept))},
        compiler_params=_params(2 * 9 * _nbytes((tr, cols + LANES), F32)),
    )(w, m, v, mine, theirs, *kept)


def _allreduce_small(v):
    rows = v.shape[0]

    def body(v_ref, sum_ref, all_ref, send_sems, recv_sems, local_sem):
        x, y, c = lax.axis_index("x"), lax.axis_index("y"), lax.axis_index("c")
        sibling = (x, y, 1 - c)
        chips = [(1 - x, y), (x, 1 - y), (1 - x, 1 - y)]

        def slab(px, py, pc):
            return all_ref.at[pl.ds((4 * px + 2 * py + pc) * rows, rows), :]

        def copy(k, block, to, src=None):
            return pltpu.make_async_remote_copy(
                src_ref=slab(*block) if src is None else src, dst_ref=slab(*block), send_sem=send_sems.at[k],
                recv_sem=recv_sems.at[k], device_id=to, device_id_type=MESH)

        mine = pltpu.make_async_copy(v_ref, slab(x, y, c), local_sem)
        mine.start()
        first = [copy(0, (x, y, c), sibling, src=v_ref)]
        first += [copy(1 + j, (x, y, c), (*chip, c), src=v_ref) for j, chip in enumerate(chips)]
        for cp in first:
            cp.start()
        passed = [copy(4 + j, (*chip, c), sibling) for j, chip in enumerate(chips)]
        for j, chip in enumerate(chips):
            copy(1 + j, (*chip, c), (x, y, c)).wait_recv()
            passed[j].start()
        copy(0, (x, y, 1 - c), (x, y, c)).wait_recv()
        for j, chip in enumerate(chips):
            copy(4 + j, (*chip, 1 - c), (x, y, c)).wait_recv()
        for cp in first + passed:
            cp.wait_send()
        mine.wait()
        total = all_ref[pl.ds(0, rows), :]
        for d in range(1, N_DEV):
            total = total + all_ref[pl.ds(d * rows, rows), :]
        sum_ref[...] = total

    vm = pl.BlockSpec(memory_space=pltpu.VMEM)
    return _pcall(
        body,
        name="allreduce_small",
        in_specs=[vm],
        out_specs=vm,
        out_shape=jax.ShapeDtypeStruct((rows, LANES), F32),
        scratch_shapes=[pltpu.VMEM((N_DEV * rows, LANES), F32), pltpu.SemaphoreType.DMA((7,)),
                        pltpu.SemaphoreType.DMA((7,)), pltpu.SemaphoreType.DMA],
    )(v)


def _pad_cols(a, before, total):
    return jnp.pad(a, ((0, 0), (before, total - before - a.shape[1])))


def _layer_weights(cfg, w_in, w_uq, w_ukv):
    w = cfg.width
    qkv, f, cq, ckv, kr, qm, gates = jnp.split(w_in, list(_cumsum(cfg.in_splits))[:-1], axis=1)
    wa = jnp.concatenate([qkv, qm], axis=1)
    ws = jnp.concatenate([_pad_cols(f, 0, LANES), cq, ckv, _pad_cols(kr, MLA_NOPE, LANES)], axis=1)
    wq = jnp.pad(w_uq.reshape(cfg.q_rank, cfg.mla_h, MLA_NOPE + MLA_ROPE), ((0, 0), (0, 0), (0, LANES - MLA_NOPE - MLA_ROPE)))
    wq = wq.reshape(cfg.q_rank, cfg.mla_h * LANES)
    kv = w_ukv.reshape(cfg.kv_rank, cfg.mla_h, MLA_NOPE + MLA_V)
    wk = jnp.pad(kv[:, :, :MLA_NOPE], ((0, 0), (0, 0), (0, LANES - MLA_NOPE))).reshape(cfg.kv_rank, cfg.mla_h * LANES)
    wv = kv[:, :, MLA_NOPE:].reshape(cfg.kv_rank, cfg.mla_h * MLA_V)
    del w
    return wa, gates, ws, wq, wk, wv


def _cumsum(xs):
    out, t = [], 0
    for v in xs:
        t += v
        out.append(t)
    return out


def _layer_weight_grads(cfg, dwa, dwg, dws, dwq, dwk, dwv):
    w, qr, kvr = cfg.width, cfg.q_rank, cfg.kv_rank
    off_kr = LANES + qr + kvr + MLA_NOPE
    dw_in = jnp.concatenate([
        dwa[:, :3 * w], dws[:, :cfg.fox_h], dws[:, LANES:LANES + qr], dws[:, LANES + qr:LANES + qr + kvr],
        dws[:, off_kr:off_kr + MLA_ROPE], dwa[:, 3 * w:], dwg], axis=1)
    dw_uq = dwq.reshape(qr, cfg.mla_h, LANES)[:, :, :MLA_NOPE + MLA_ROPE].reshape(qr, cfg.mla_h * (MLA_NOPE + MLA_ROPE))
    dw_ukv = jnp.concatenate([dwk.reshape(kvr, cfg.mla_h, LANES)[:, :, :MLA_NOPE], dwv.reshape(kvr, cfg.mla_h, MLA_V)],
                             axis=2).reshape(kvr, cfg.mla_h * (MLA_NOPE + MLA_V))
    return dw_in, dw_uq, dw_ukv


def _attn_specs(cfg, batch):
    t = min(ATTN_TILE, cfg.seq)
    common = dict(batch=batch, sq=cfg.seq, chunk=cfg.chunk, tq=t)
    fox = Attn(sk=cfg.seq, groups=cfg.fox_h // 2, hq=2, hv=2, mode="fox", scale=FOX_DH ** -0.5, tk=t, **common)
    mla = Attn(sk=cfg.seq, groups=cfg.mla_h // 2, hq=1, hv=2, mode="chunk",
               scale=(MLA_NOPE + MLA_ROPE) ** -0.5, tk=t, **common)
    mem = Attn(sk=cfg.n_mem, groups=cfg.mem_h, hq=1, hv=1, mode="none", scale=MEM_DH ** -0.5, tk=cfg.n_mem, **common)
    return fox, mla, mem


def _small_core(cfg, ps, bf, gq, gkv):
    qr, kvr = cfg.q_rank, cfg.kv_rank
    z = ps[:, :LANES] + bf
    logf = jnp.minimum(z, 0.0) - jnp.log1p(jnp.exp(-jnp.abs(z)))
    nq = _rms(ps[:, LANES:LANES + qr], gq)
    nkv = _rms(ps[:, LANES + qr:LANES + qr + kvr], gkv)
    return logf, nq, nkv


def _layer_fwd(cfg, l, batch, h, hb, mem_b, rope_c, rope_s, hosts, lw, bf_pad, g_cq, g_ckv, ln1, ln2):
    wa, wg, ws, wq, wk, wv, wmkv, wbr, wout, wff1, wff2 = lw
    w, d = cfg.width, cfg.d
    fox, mla, mem = _attn_specs(cfg, batch)
    nw = w // LANES
    pa = _mm(f"proj_a_{l}", "nn", [(hb, wa)], [BF16])
    gl = _mm(f"proj_gates_{l}", "nn", [(hb, wg)], [F32])
    ps = _mm(f"proj_small_{l}", "nn", [(hb, ws)], [F32], tn=cfg.small_w)

    def small_fwd(ps_, c_, s_, bf_, gq_, gkv_):
        logf, nq, nkv = _small_core(cfg, ps_, bf_, gq_, gkv_)
        kpe = _rope(ps_[:, cfg.small_w - LANES:], c_, s_)
        return logf, nq, nkv, kpe

    logf, nq, nkv, kpe = _rowwise(
        f"small_fwd_{l}", small_fwd, [ps, rope_c, rope_s], [bf_pad, g_cq, g_ckv],
        [(LANES, F32), (cfg.q_rank, BF16), (cfg.kv_rank, BF16), (LANES, F32)])
    cfk, cfq = _forget_cumsum(f"cum_forget_{l}", logf, batch, cfg.seq, cfg.fox_h, fox.tq)

    qf = _mm(f"mla_q_{l}", "nn", [(nq, wq)], [BF16], tn=wq.shape[1],
             epi=lambda acc, c_, s_: (_rope(acc, c_, s_),), row_extras=[rope_c, rope_s])
    kf = _mm(f"mla_k_{l}", "nn", [(nkv, wk)], [BF16], tn=wk.shape[1],
             epi=lambda acc, kp: (acc + jnp.tile(kp, (1, cfg.mla_h)),), row_extras=[kpe])
    vb = _mm(f"mla_v_{l}", "nn", [(nkv, wv)], [BF16])
    mkv = _mm(f"mem_kv_{l}", "nn", [(mem_b, wmkv)], [BF16])

    o_a, lse_a = _attn_fwd(f"fox_fwd_{l}", fox, (pa, 0), (pa, nw), (pa, 2 * nw), cfq, cfk,
                           hosts=hosts, host="fox_fwd")
    o_b, lse_b = _attn_fwd(f"mla_fwd_{l}", mla, (qf, 0), (kf, 0), (vb, 0), hosts=hosts, host="mla_fwd")
    o_c, lse_c = _attn_fwd(f"mem_fwd_{l}", mem, (pa, 3 * nw), (mkv, 0), (mkv, nw), hosts=hosts, host="mem_fwd")
    bps = [_mm(f"branch_{n}_{l}", "nn", [(o, wbr[n])], [F32]) for n, o in enumerate((o_a, o_b, o_c))]

    def merge(gl_, b0, b1, b2):
        g = jax.nn.sigmoid(gl_)
        return (g[:, :d] * b0 + g[:, d:2 * d] * b1 + g[:, 2 * d:] * b2,)

    (merged,) = _rowwise(f"merge_{l}", merge, [gl] + bps, [], [(d, BF16)])

    def post_ln(acc, res, g_, b_):
        z = cfg.alpha * res + acc
        y = _ln(z, g_, b_)
        return z, y, y

    z1, h1, h1b = _mm(f"out_ln1_{l}", "nn", [(merged, wout)], [F32, F32, BF16], tm=256, tn=d,
                      epi=post_ln, row_extras=[h], bc_extras=list(ln1))
    u, a = _mm(f"ff1_{l}", "nn", [(h1b, wff1)], [BF16, BF16],
               epi=lambda acc: (acc, jnp.square(jnp.maximum(acc, 0.0))))
    z2, h2, h2b = _mm(f"ff2_ln2_{l}", "nn", [(a, wff2)], [F32, F32, BF16], tm=256, tn=d,
                      epi=post_ln, row_extras=[h1], bc_extras=list(ln2))
    saved = dict(hb=hb, pa=pa, gl=gl, ps=ps, nq=nq, nkv=nkv, cfq=cfq, cfk=cfk, qf=qf, kf=kf, vb=vb, mkv=mkv,
                 o=(o_a, o_b, o_c), lse=(lse_a, lse_b, lse_c), bps=bps, merged=merged, z1=z1, h1b=h1b, u=u, a=a, z2=z2)
    return h2, h2b, saved


def _ln_bwd(name, cfg, ga, gb, z, g, b):
    d = cfg.d

    def fn(*vals):
        if gb is None:
            ga_, z_, g_, b_ = vals
            dy = ga_
        else:
            ga_, gb_, z_, g_, b_ = vals
            dy = ga_ + cfg.alpha * gb_
        _, vjp = jax.vjp(_ln, z_, g_, b_)
        dz, dg, db = vjp(dy)
        return dz, dz, dg, db

    rows = [ga, z] if gb is None else [ga, gb, z]
    return _rowwise(name, fn, rows, [g, b], [(d, F32), (d, BF16)], accs=[(1, d), (1, d)])


def _layer_bwd(cfg, l, batch, ga, gb, sv, mem_b, rope_c, rope_s, hosts, lw, bf_pad, g_cq, g_ckv, ln1, ln2):
    wa, wg, ws, wq, wk, wv, wmkv, wbr, wout, wff1, wff2 = lw
    w, d = cfg.width, cfg.d
    fox, mla, mem = _attn_specs(cfg, batch)
    nw = w // LANES
    gdt = BF16

    dz2, dz2b, dg2, db2 = _ln_bwd(f"ln2_bwd_{l}", cfg, ga, gb, sv["z2"], *ln2)
    du = _mm(f"ff2_bwd_x_{l}", "nt", [(dz2b, wff2)], [BF16],
             epi=lambda acc, u_: (acc * (2.0 * jnp.maximum(u_.astype(F32), 0.0)),), row_extras=[sv["u"]],
             hosts=hosts, host="ff2_bwd_x")
    dwff2 = _mm(f"ff2_bwd_w_{l}", "tn", [(sv["a"], dz2b)], [gdt])
    dwff1 = _mm(f"ff1_bwd_w_{l}", "tn", [(sv["h1b"], du)], [gdt])
    dh1 = _mm(f"ff1_bwd_x_{l}", "nt", [(du, wff1)], [F32])
    dz1, dz1b, dg1, db1 = _ln_bwd(f"ln1_bwd_{l}", cfg, dh1, dz2, sv["z1"], *ln1)
    dmerged = _mm(f"out_bwd_x_{l}", "nt", [(dz1b, wout)], [F32])
    dwout = _mm(f"out_bwd_w_{l}", "tn", [(sv["merged"], dz1b)], [gdt])

    def merge_bwd(dm, gl_, b0, b1, b2):
        def f(gl__, b0_, b1_, b2_):
            g = jax.nn.sigmoid(gl__)
            return g[:, :d] * b0_ + g[:, d:2 * d] * b1_ + g[:, 2 * d:] * b2_

        _, vjp = jax.vjp(f, gl_, b0, b1, b2)
        return vjp(dm)

    dgl, db0, db1_, db2_ = _rowwise(f"merge_bwd_{l}", merge_bwd, [dmerged, sv["gl"]] + sv["bps"], [],
                                    [(3 * d, BF16), (d, BF16), (d, BF16), (d, BF16)])
    dbps = (db0, db1_, db2_)
    dos = [_mm(f"branch_bwd_x_{n}_{l}", "nt", [(dbps[n], wbr[n])], [BF16]) for n in range(3)]
    dwbr = [_mm(f"branch_bwd_w_{n}_{l}", "tn", [(sv["o"][n], dbps[n])], [gdt]) for n in range(3)]

    pa = sv["pa"]
    dq_a, dk_a, dv_a, dcfk = _attn_bwd(f"fox_bwd_{l}", fox, (pa, 0), (pa, nw), (pa, 2 * nw), sv["o"][0], sv["lse"][0],
                                       dos[0], sv["cfq"], sv["cfk"], hosts=hosts, host="fox_bwd")
    dqf, dkf, dvb = _attn_bwd(f"mla_bwd_{l}", mla, (sv["qf"], 0), (sv["kf"], 0), (sv["vb"], 0), sv["o"][1],
                              sv["lse"][1], dos[1], hosts=hosts, host="mla_bwd")
    dqm, dmk, dmv = _attn_bwd(f"mem_bwd_{l}", mem, (pa, 3 * nw), (sv["mkv"], 0), (sv["mkv"], nw), sv["o"][2],
                              sv["lse"][2], dos[2])
    dwmkv = _mm(f"mem_kv_bwd_w_{l}", "tn", [(mem_b, jnp.concatenate([dmk, dmv], axis=1))], [gdt])

    (dq_raw,) = _rowwise(f"mla_q_rope_bwd_{l}", lambda dy, c_, s_: (_rope_t(dy, c_, s_),), [dqf, rope_c, rope_s], [],
                         [(wq.shape[1], BF16)])
    dwq = _mm(f"mla_q_bwd_w_{l}", "tn", [(sv["nq"], dq_raw)], [gdt])
    dnq = _mm(f"mla_q_bwd_x_{l}", "nt", [(dq_raw, wq)], [F32])
    dwk = _mm(f"mla_k_bwd_w_{l}", "tn", [(sv["nkv"], dkf)], [gdt])
    dwv = _mm(f"mla_v_bwd_w_{l}", "tn", [(sv["nkv"], dvb)], [gdt])
    dnkv = _mm(f"mla_kv_bwd_x_{l}", "nt", [(dkf, wk), (dvb, wv)], [F32])

    dlogf = _forget_cumsum_bwd(f"cum_forget_bwd_{l}", dcfk, batch, cfg.seq, cfg.fox_h, fox.tq)

    def small_bwd(ps_, dlogf_, dnq_, dnkv_, dkf_, c_, s_, bf_, gq_, gkv_):
        _, vjp = jax.vjp(functools.partial(_small_core, cfg), ps_, bf_, gq_, gkv_)
        dps, dbf, dgq, dgkv = vjp((dlogf_, dnq_, dnkv_))
        dkpe = dkf_[:, :LANES].astype(F32)
        for hh in range(1, cfg.mla_h):
            dkpe = dkpe + dkf_[:, hh * LANES:(hh + 1) * LANES].astype(F32)
        lane = lax.broadcasted_iota(jnp.int32, (1, LANES), 1)
        dkpe = jnp.where((lane >= MLA_NOPE) & (lane < MLA_NOPE + MLA_ROPE), dkpe, 0.0)
        dkr = _rope_t(dkpe, c_, s_)
        dps = jnp.concatenate([dps[:, :cfg.small_w - LANES], dkr], axis=1)
        return dps, dbf, dgq, dgkv

    dps, dbf, dgq, dgkv = _rowwise(
        f"small_bwd_{l}", small_bwd, [sv["ps"], dlogf, dnq, dnkv, dkf, rope_c, rope_s], [bf_pad, g_cq, g_ckv],
        [(cfg.small_w, BF16)], accs=[(1, LANES), (1, cfg.q_rank), (1, cfg.kv_rank)])

    dpa = jnp.concatenate([dq_a.astype(BF16), dk_a, dv_a, dqm.astype(BF16)], axis=1)
    hb = sv["hb"]
    dh = _mm(f"proj_bwd_x_{l}", "nt", [(dpa, wa), (dgl, wg), (dps, ws)], [F32], tn=d)
    dwa = _mm(f"proj_a_bwd_w_{l}", "tn", [(hb, dpa)], [gdt])
    dwg = _mm(f"proj_gates_bwd_w_{l}", "tn", [(hb, dgl)], [gdt])
    dws = _mm(f"proj_small_bwd_w_{l}", "tn", [(hb, dps)], [gdt], tn=cfg.small_w)
    dw_in, dw_uq, dw_ukv = _layer_weight_grads(cfg, dwa, dwg, dws, dwq, dwk, dwv)
    big = dict(w_in=dw_in, w_uq=dw_uq, w_ukv=dw_ukv, w_mem_kv=dwmkv, w_br=jnp.stack(dwbr), w_out=dwout,
               w_ff1=dwff1, w_ff2=dwff2)
    small = dict(b_forget=dbf[0, :cfg.fox_h], g_cq=dgq[0], g_ckv=dgkv[0], ln1_g=dg1[0], ln1_b=db1[0],
                 ln2_g=dg2[0], ln2_b=db2[0])
    return dh, dz1, big, small


def _rope_tables(positions):
    inv_freq = ROPE_BASE ** (-jnp.arange(0, MLA_ROPE, 2, dtype=F32) / MLA_ROPE)
    ang = positions.astype(F32).reshape(-1)[:, None] * inv_freq
    cos, sin = jnp.cos(ang), jnp.sin(ang)
    t = ang.shape[0]
    rope_c = jnp.concatenate([jnp.ones((t, MLA_NOPE), F32), cos, cos, jnp.zeros((t, LANES - MLA_NOPE - MLA_ROPE), F32)], axis=1)
    rope_s = jnp.concatenate([jnp.zeros((t, MLA_NOPE), F32), -sin, sin, jnp.zeros((t, LANES - MLA_NOPE - MLA_ROPE), F32)], axis=1)
    return rope_c, rope_s


def _local_step(cfg, x, mem, positions, target, small_w, comm):
    batch = x.shape[0]
    d, depth = cfg.d, cfg.depth
    t = batch * cfg.seq
    x2, tgt = x.reshape(t, d), target.reshape(t, d)
    mem_b = mem.reshape(batch * cfg.n_mem, d).astype(BF16)
    rope_c, rope_s = _rope_tables(positions)
    row = lambda v: v.reshape(1, -1)
    ln_in = (row(small_w["ln_in_g"]), row(small_w["ln_in_b"]))

    h, hb = _rowwise("ln_in", lambda x_, g_, b_: (_ln(x_, g_, b_),) * 2, [x2], list(ln_in), [(d, F32), (d, BF16)])
    layers, saves = [], []
    for l in range(depth):
        big_w = comm.weights(l)
        lw = _layer_weights(cfg, big_w["w_in"], big_w["w_uq"], big_w["w_ukv"]) + (
            big_w["w_mem_kv"], big_w["w_br"], big_w["w_out"], big_w["w_ff1"], big_w["w_ff2"])
        par = dict(
            lw=lw, bf_pad=jnp.pad(row(small_w["b_forget"][l]), ((0, 0), (0, LANES - cfg.fox_h))),
            g_cq=row(small_w["g_cq"][l]), g_ckv=row(small_w["g_ckv"][l]),
            ln1=(row(small_w["ln1_g"][l]), row(small_w["ln1_b"][l])),
            ln2=(row(small_w["ln2_g"][l]), row(small_w["ln2_b"][l])))
        layers.append(par)
        h, hb, sv = _layer_fwd(cfg, l, batch, h, hb, mem_b, rope_c, rope_s, comm.forward_hosts(l), **par)
        saves.append(sv)

    def loss_fn(y, tg):
        err = y - tg
        part = 0.5 * jnp.sum(jnp.mean(err * err, axis=-1, keepdims=True), axis=0, keepdims=True)
        return err * (1.0 / d), jnp.broadcast_to(part, (1, LANES))

    ga, loss_acc = _rowwise("loss", loss_fn, [h, tgt], [], [(d, F32)], accs=[(1, LANES)])
    gb = None
    small_g = {k: [None] * depth for k in ("b_forget", "g_cq", "g_ckv", "ln1_g", "ln1_b", "ln2_g", "ln2_b")}
    for l in reversed(range(depth)):
        ga, gb, big, small = _layer_bwd(cfg, l, batch, ga, gb, saves[l], mem_b, rope_c, rope_s,
                                        comm.backward_hosts(l), **layers[l])
        comm.grads(l, big)
        for k, v in small.items():
            small_g[k][l] = v
    dx, _, dg_in, db_in = _ln_bwd("ln_in_bwd", cfg, ga, gb, x2, *ln_in)
    small_g = {k: jnp.stack(v) for k, v in small_g.items()}
    small_g["ln_in_g"], small_g["ln_in_b"] = dg_in[0], db_in[0]
    return loss_acc[0, 0], dx.reshape(x.shape), small_g


BIG = ("w_in", "w_uq", "w_ukv", "w_mem_kv", "w_br", "w_out", "w_ff1", "w_ff2")
SMALL = ("ln_in_g", "ln_in_b", "b_forget", "g_cq", "g_ckv", "ln1_g", "ln1_b", "ln2_g", "ln2_b")
ROW_CUT = ("w_mem_kv", "w_out", "w_ff2")


def _shard_2d(a):
    cols = a.shape[-1]
    rows = a.size // cols
    return a.reshape(2, rows // 2, cols)


def _full_from_slots(name, slots, shard_shape):
    parts = slots.reshape((N_CHIPS,) + shard_shape)
    axis = len(shard_shape) - (2 if name in ROW_CUT else 1)
    return jnp.concatenate([parts[i] for i in range(N_CHIPS)], axis=axis)


def _slots_from_full(name, full, shard_shape):
    axis = len(shard_shape) - (2 if name in ROW_CUT else 1)
    parts = jnp.stack(jnp.split(full, N_CHIPS, axis=axis))
    cols = shard_shape[-1]
    return parts.reshape(N_CHIPS, 2, -1, cols)


def _pack_small(cfg, vals):
    flat = jnp.concatenate([vals[k].reshape(-1).astype(F32) for k in SMALL])
    pad = (-flat.shape[0]) % (LANES * LANES)
    return jnp.pad(flat, (0, pad)).reshape(-1, LANES)


def _unpack_small(packed, like):
    flat, out, off = packed.reshape(-1), {}, 0
    for k in SMALL:
        n = like[k].size
        out[k] = flat[off:off + n].reshape(like[k].shape)
        off += n
    return out


class LayerComm:
    def __init__(self, cfg, w, m, v):
        self.cfg, self.w = cfg, w
        self.shards = [[_shard_2d(w[k][l].astype(BF16)) for k in BIG] for l in range(cfg.depth)]
        self.got = _gather_now(self.shards[0])
        self.next = None
        self.pending = None
        halves = lambda a: a.reshape(a.shape[0], 2, -1, a.shape[-1])
        self.state = {k: [halves(a[k]) for a in (w, m, v)] for k in BIG}
        self.outs = {k: None for k in BIG}

    def weights(self, l):
        if l > 0:
            self.got = self.next.got
        mine = (_chip(), 0, 0, 0)
        return {k: _full_from_slots(k, lax.dynamic_update_slice(g, s[None], mine), self.w[k].shape[1:])
                for k, g, s in zip(BIG, self.got, self.shards[l])}

    def forward_hosts(self, l):
        self.next = GatherNext(self.shards[l + 1]) if l + 1 < self.cfg.depth else Hosts()
        return self.next

    def backward_hosts(self, l):
        return self.pending[1] if self.pending else Hosts()

    def _update(self):
        l, reduce = self.pending
        for k, (mine, theirs) in zip(BIG, reduce.result()):
            self.outs[k] = _adamw_layer(f"adamw_{k}_{l}", l, *self.state[k], mine, theirs, self.outs[k])

    def grads(self, l, big):
        if self.pending:
            self._update()
        slots = [_slots_from_full(k, big[k], self.w[k].shape[1:]) for k in BIG]
        self.pending = (l, ReduceLayer(l, slots))

    def finish(self):
        self.pending[1].run_now()
        self._update()
        return {k: tuple(a.reshape(self.w[k].shape) for a in self.outs[k]) for k in BIG}


def _step(cfg, x, mem, positions, target, w, m, v):
    comm = LayerComm(cfg, w, m, v)
    small_w = {k: w[k] for k in SMALL}
    loss_local, dx, small_g = _local_step(cfg, x, mem, positions, target, small_w, comm)
    loss = lax.psum(loss_local, ("x", "y", "c"))
    outs_big = comm.finish()

    g_small = _allreduce_small(_pack_small(cfg, small_g))
    packs = [_pack_small(cfg, {k: d_[k] for k in SMALL}) for d_ in (w, m, v)]
    dl, nm, nv = _slabwise("adamw_small", _adamw_math, [a[None] for a in (packs[0], g_small, packs[1], packs[2])],
                           [F32, F32, F32])
    outs_small = [_unpack_small(a[0] if a.ndim == 3 else a, w) for a in (g_small, dl, nm, nv)]

    names = SMALL[:2] + ("w_in", "b_forget", "w_uq", "g_cq", "w_ukv", "g_ckv", "w_mem_kv", "w_br", "w_out",
                         "ln1_g", "ln1_b", "w_ff1", "w_ff2", "ln2_g", "ln2_b")
    result = [loss, dx]
    for part in range(4):
        for k in names:
            result.append(outs_big[k][part] if k in outs_big else outs_small[part][k])
    return tuple(result)


def kernel(x, mem, positions, ln_in_g, ln_in_b, w_in, b_forget, w_uq, g_cq, w_ukv, g_ckv, w_mem_kv, w_br, w_out, ln1_g, ln1_b, w_ff1, w_ff2, ln2_g, ln2_b, loss_target, m_ln_in_g, m_ln_in_b, m_w_in, m_b_forget, m_w_uq, m_g_cq, m_w_ukv, m_g_ckv, m_w_mem_kv, m_w_br, m_w_out, m_ln1_g, m_ln1_b, m_w_ff1, m_w_ff2, m_ln2_g, m_ln2_b, v_ln_in_g, v_ln_in_b, v_w_in, v_b_forget, v_w_uq, v_g_cq, v_w_ukv, v_g_ckv, v_w_mem_kv, v_w_br, v_w_out, v_ln1_g, v_ln1_b, v_w_ff1, v_w_ff2, v_ln2_g, v_ln2_b):
    w = dict(ln_in_g=ln_in_g, ln_in_b=ln_in_b, w_in=w_in, b_forget=b_forget, w_uq=w_uq, g_cq=g_cq, w_ukv=w_ukv,
             g_ckv=g_ckv, w_mem_kv=w_mem_kv, w_br=w_br, w_out=w_out, ln1_g=ln1_g, ln1_b=ln1_b, w_ff1=w_ff1,
             w_ff2=w_ff2, ln2_g=ln2_g, ln2_b=ln2_b)
    m = dict(ln_in_g=m_ln_in_g, ln_in_b=m_ln_in_b, w_in=m_w_in, b_forget=m_b_forget, w_uq=m_w_uq, g_cq=m_g_cq,
             w_ukv=m_w_ukv, g_ckv=m_g_ckv, w_mem_kv=m_w_mem_kv, w_br=m_w_br, w_out=m_w_out, ln1_g=m_ln1_g,
             ln1_b=m_ln1_b, w_ff1=m_w_ff1, w_ff2=m_w_ff2, ln2_g=m_ln2_g, ln2_b=m_ln2_b)
    v = dict(ln_in_g=v_ln_in_g, ln_in_b=v_ln_in_b, w_in=v_w_in, b_forget=v_b_forget, w_uq=v_w_uq, g_cq=v_g_cq,
             w_ukv=v_w_ukv, g_ckv=v_g_ckv, w_mem_kv=v_w_mem_kv, w_br=v_w_br, w_out=v_w_out, ln1_g=v_ln1_g,
             ln1_b=v_ln1_b, w_ff1=v_w_ff1, w_ff2=v_w_ff2, ln2_g=v_ln2_g, ln2_b=v_ln2_b)
    return _step(Cfg(), x, mem, positions, loss_target, w, m, v)
```

```python
import functools
from typing import NamedTuple

import jax
import jax.numpy as jnp
from jax import lax
from jax.experimental import pallas as pl
from jax.experimental.pallas import tpu as pltpu

F32 = jnp.float32
BF16 = jnp.bfloat16
MESH = pl.DeviceIdType.MESH

LANES = 128
SUBLANES = 8
VMEM_BYTES = 64 * 1024 * 1024
N_CHIPS = 4
N_DEV = 8

FOX_DH = 64
MLA_NOPE = 64
MLA_ROPE = 32
MLA_V = 64
MEM_DH = 128
ROPE_BASE = 10000.0
LN_EPS = 1e-5
RMS_EPS = 1e-6
NEG_INF = -1e30
ATTN_TILE = 512

ADAM_LR = 0.001
ADAM_B1 = 0.9
ADAM_B2 = 0.999
ADAM_EPS = 1e-08
ADAM_WD = 0.01
ADAM_STEP = 10


class Cfg(NamedTuple):
    d: int = 1024
    depth: int = 4
    seq: int = 2048
    chunk: int = 64
    n_mem: int = 256
    fox_h: int = 8
    mla_h: int = 8
    q_rank: int = 384
    kv_rank: int = 256
    mem_h: int = 4
    d_ff: int = 4096

    @property
    def width(self):
        return self.fox_h * FOX_DH

    @property
    def alpha(self):
        return (2 * self.depth) ** 0.25

    @property
    def small_w(self):
        return LANES + self.q_rank + self.kv_rank + LANES

    @property
    def in_splits(self):
        return (3 * self.width, self.fox_h, self.q_rank, self.kv_rank, MLA_ROPE, self.width, 3 * self.d)


class Exchange(NamedTuple):
    ins: tuple
    out_shapes: tuple
    plan: object
    n_copies: int
    aliases: dict = {}


def _peer(rel):
    x, y, c = lax.axis_index("x"), lax.axis_index("y"), lax.axis_index("c")
    if rel == "c":
        return (x, y, 1 - c)
    return ((1 - x) if rel in (1, 3) else x, (1 - y) if rel in (2, 3) else y, c)


def _exchange_copies(ex, in_refs, out_refs, send_sems, recv_sems):
    planned = ex.plan(in_refs, out_refs)
    assert len(planned) == ex.n_copies, len(planned)
    return [pltpu.make_async_remote_copy(src_ref=src, dst_ref=dst, send_sem=send_sems.at[i], recv_sem=recv_sems.at[i],
                                         device_id=_peer(rel), device_id_type=MESH)
            for i, (src, dst, rel) in enumerate(planned)]


def _pcall(body, rider=None, **kw):
    if rider is not None:
        n_in, n_out, grid = len(kw["in_specs"]), len(kw["out_specs"]), kw["grid"]
        n_rin, n_rout = len(rider.ins), len(rider.out_shapes)
        host = body

        def body(*refs):
            ins, rins = refs[:n_in], refs[n_in:n_in + n_rin]
            outs = refs[n_in + n_rin:n_in + n_rin + n_out]
            routs = refs[n_in + n_rin + n_out:n_in + n_rin + n_out + n_rout]
            scratch = refs[n_in + n_rin + n_out + n_rout:-2]
            copies = _exchange_copies(rider, rins, routs, refs[-2], refs[-1])
            first = functools.reduce(jnp.logical_and, [pl.program_id(a) == 0 for a in range(len(grid))])
            last = functools.reduce(jnp.logical_and, [pl.program_id(a) == n - 1 for a, n in enumerate(grid)])

            @pl.when(first)
            def _():
                for cp in copies:
                    cp.start()

            host(*ins, *outs, *scratch)

            @pl.when(last)
            def _():
                for cp in copies:
                    cp.wait()

        any_spec = pl.BlockSpec(memory_space=pl.ANY)
        sems = [pltpu.SemaphoreType.DMA((rider.n_copies,))] * 2
        kw = dict(
            kw,
            in_specs=list(kw["in_specs"]) + [any_spec] * n_rin,
            out_specs=list(kw["out_specs"]) + [any_spec] * n_rout,
            out_shape=list(kw["out_shape"]) + list(rider.out_shapes),
            scratch_shapes=list(kw.get("scratch_shapes", ())) + sems,
            input_output_aliases={**kw.get("input_output_aliases", {}),
                                  **{n_in + i: n_out + o for i, o in rider.aliases.items()}},
        )
    call = pl.pallas_call(body, **kw)
    return lambda *ops: call(*[pltpu.with_memory_space_constraint(o, pltpu.HBM) for o in ops])


class Hosts:
    def rider(self, host):
        return None

    def done(self, host, outs):
        pass

    def early_grads(self, grads):
        pass


def _merge_exchanges(exs):
    n_ins = [len(e.ins) for e in exs]
    n_outs = [len(e.out_shapes) for e in exs]

    def plan(in_refs, out_refs):
        copies, i, o = [], 0, 0
        for e, ni, no in zip(exs, n_ins, n_outs):
            copies += e.plan(in_refs[i:i + ni], out_refs[o:o + no])
            i, o = i + ni, o + no
        return copies

    aliases, i, o = {}, 0, 0
    for e, ni, no in zip(exs, n_ins, n_outs):
        aliases.update({i + a: o + b for a, b in e.aliases.items()})
        i, o = i + ni, o + no
    return Exchange(sum((e.ins for e in exs), ()), sum((e.out_shapes for e in exs), ()), plan,
                    sum(e.n_copies for e in exs), aliases)


class Together(Hosts):
    def __init__(self, members):
        self.members, self.active = list(members), []

    def rider(self, host):
        self.active = [(m, r) for m, r in ((m, m.rider(host)) for m in self.members) if r is not None]
        if not self.active:
            return None
        return _merge_exchanges([r for _, r in self.active])

    def done(self, host, outs):
        for m, r in self.active:
            m.done(host, outs[:len(r.out_shapes)])
            outs = outs[len(r.out_shapes):]


def _hosted(hosts, host, call, ops, n_results):
    rider = hosts.rider(host) if hosts is not None else None
    if rider is None:
        return call(None)(*ops)
    res = call(rider)(*ops, *rider.ins)
    hosts.done(host, list(res[n_results:]))
    return res[:n_results]


def _nbytes(shape, dtype):
    n = 1
    for s in shape:
        n *= s
    return n * jnp.dtype(dtype).itemsize


def _tile(dim, target):
    if dim <= target:
        return dim
    t = target - target % LANES
    while t >= LANES:
        if dim % t == 0:
            return t
        t -= LANES
    return dim


def _params(block_bytes, scratch_bytes=0):
    est = 2 * block_bytes + scratch_bytes + 24 * 1024 * 1024
    return pltpu.CompilerParams(vmem_limit_bytes=int(min(max(est, 32 * 1024 * 1024), VMEM_BYTES - 4 * 1024 * 1024)))


_DIMS = {"nn": (((1,), (0,)), ((), ())), "nt": (((1,), (1,)), ((), ())), "tn": (((0,), (0,)), ((), ()))}


MM_TILE = 1024
MM_BLOCK_BYTES = 16 * 1024 * 1024


def _mm_tiles(mode, pairs, out_dtypes, m, n, tm, tn):
    fixed_m, fixed_n = tm is not None, tn is not None
    tm, tn = _tile(m, tm or MM_TILE), _tile(n, tn or MM_TILE)

    def block_bytes(tm_, tn_):
        total = sum(_nbytes((tm_, tn_), dt) for dt in out_dtypes)
        for a, b in pairs:
            k = a.shape[0] if mode == "tn" else a.shape[1]
            total += _nbytes((k, tm_), a.dtype) + _nbytes((k, tn_), b.dtype)
        return total

    while block_bytes(tm, tn) > MM_BLOCK_BYTES:
        if not fixed_m and tm >= tn and tm > 2 * LANES:
            tm = _tile(m, tm // 2)
        elif not fixed_n and tn > 2 * LANES:
            tn = _tile(n, tn // 2)
        elif not fixed_m and tm > 2 * LANES:
            tm = _tile(m, tm // 2)
        else:
            break
    return tm, tn


def _mm(name, mode, pairs, out_dtypes, tm=None, tn=None, epi=None, row_extras=(), bc_extras=(), hosts=None, host=None):
    a0, b0 = pairs[0]
    m = a0.shape[1] if mode == "tn" else a0.shape[0]
    n = b0.shape[0] if mode == "nt" else b0.shape[1]
    tm, tn = _mm_tiles(mode, pairs, out_dtypes, m, n, tm, tn)
    in_specs, ops, blk = [], [], 0
    for a, b in pairs:
        if mode == "tn":
            k = a.shape[0]
            sa, sha = pl.BlockSpec((k, tm), lambda i, j: (0, i)), (k, tm)
        else:
            k = a.shape[1]
            sa, sha = pl.BlockSpec((tm, k), lambda i, j: (i, 0)), (tm, k)
        if mode == "nt":
            sb, shb = pl.BlockSpec((tn, k), lambda i, j: (j, 0)), (tn, k)
        else:
            sb, shb = pl.BlockSpec((k, tn), lambda i, j: (0, j)), (k, tn)
        in_specs += [sa, sb]
        ops += [a, b]
        blk += _nbytes(sha, a.dtype) + _nbytes(shb, b.dtype)
    for e in row_extras:
        w = e.shape[1]
        if w == n:
            in_specs.append(pl.BlockSpec((tm, tn), lambda i, j: (i, j)))
            blk += _nbytes((tm, tn), e.dtype)
        else:
            in_specs.append(pl.BlockSpec((tm, w), lambda i, j: (i, 0)))
            blk += _nbytes((tm, w), e.dtype)
        ops.append(e)
    for e in bc_extras:
        r, w = e.shape
        if w == n:
            in_specs.append(pl.BlockSpec((r, tn), lambda i, j: (0, j)))
        else:
            in_specs.append(pl.BlockSpec((r, w), lambda i, j: (0, 0)))
        blk += _nbytes((r, w), e.dtype)
        ops.append(e)
    npairs, nrow, nbc, nout = len(pairs), len(row_extras), len(bc_extras), len(out_dtypes)
    dims = _DIMS[mode]

    def body(*refs):
        acc = None
        for p in range(npairs):
            a = refs[2 * p][...].astype(BF16)
            b = refs[2 * p + 1][...].astype(BF16)
            d = lax.dot_general(a, b, dims, preferred_element_type=F32)
            acc = d if acc is None else acc + d
        ex = [r[...] for r in refs[2 * npairs:2 * npairs + nrow + nbc]]
        outs = (acc,) if epi is None else epi(acc, *ex)
        for o_ref, o in zip(refs[2 * npairs + nrow + nbc:], outs):
            o_ref[...] = o.astype(o_ref.dtype)

    blk += sum(_nbytes((tm, tn), dt) for dt in out_dtypes) + 2 * _nbytes((tm, tn), F32)
    call = lambda rider: _pcall(
        body,
        rider=rider,
        name=name,
        grid=(m // tm, n // tn),
        in_specs=in_specs,
        out_specs=[pl.BlockSpec((tm, tn), lambda i, j: (i, j)) for _ in range(nout)],
        out_shape=[jax.ShapeDtypeStruct((m, n), dt) for dt in out_dtypes],
        compiler_params=_params(blk),
    )
    res = _hosted(hosts, host, call, ops, nout)
    return res[0] if nout == 1 else res


def _rowwise(name, fn, row_ins, bc_ins, outs, accs=(), tm=256):
    rows = row_ins[0].shape[0]
    tm = min(tm, rows)
    assert rows % tm == 0
    nrow, nbc, nout, nacc = len(row_ins), len(bc_ins), len(outs), len(accs)
    in_specs = [pl.BlockSpec((tm, a.shape[1]), lambda i: (i, 0)) for a in row_ins]
    in_specs += [pl.BlockSpec(a.shape, lambda i: (0, 0)) for a in bc_ins]
    out_specs = [pl.BlockSpec((tm, w), lambda i: (i, 0)) for w, _ in outs]
    out_specs += [pl.BlockSpec(s, lambda i: (0, 0)) for s in accs]
    out_shape = [jax.ShapeDtypeStruct((rows, w), dt) for w, dt in outs]
    out_shape += [jax.ShapeDtypeStruct(s, F32) for s in accs]

    def body(*refs):
        vals = fn(*[r[...] for r in refs[:nrow + nbc]])
        o_refs = refs[nrow + nbc:]
        for r, v in zip(o_refs[:nout], vals[:nout]):
            r[...] = v.astype(r.dtype)
        if nacc:
            @pl.when(pl.program_id(0) == 0)
            def _():
                for r in o_refs[nout:]:
                    r[...] = jnp.zeros(r.shape, F32)

            for r, v in zip(o_refs[nout:], vals[nout:]):
                r[...] += v

    blk = sum(_nbytes((tm, a.shape[1]), a.dtype) for a in row_ins) + sum(_nbytes(a.shape, a.dtype) for a in bc_ins)
    blk += sum(_nbytes((tm, w), dt) for w, dt in outs) + sum(_nbytes(s, F32) for s in accs)
    res = _pcall(
        body,
        name=name,
        grid=(rows // tm,),
        in_specs=in_specs,
        out_specs=out_specs,
        out_shape=out_shape,
        compiler_params=_params(2 * blk),
    )(*row_ins, *bc_ins)
    return res


def _ln(z, g, b):
    mu = jnp.mean(z, axis=-1, keepdims=True)
    zc = z - mu
    var = jnp.mean(zc * zc, axis=-1, keepdims=True)
    return zc * lax.rsqrt(var + LN_EPS) * g + b


def _rms(x, g):
    return x * lax.rsqrt(jnp.mean(x * x, axis=-1, keepdims=True) + RMS_EPS) * g


def _colsum(v):
    return jnp.sum(v, axis=0, keepdims=True)


def _rope_swap(x):
    w = x.shape[1]
    lane = lax.broadcasted_iota(jnp.int32, (1, w), 1) % LANES
    from_left = pltpu.roll(x, 16, 1)
    from_right = pltpu.roll(x, w - 16, 1)
    lo = (lane >= MLA_NOPE) & (lane < MLA_NOPE + 16)
    hi = (lane >= MLA_NOPE + 16) & (lane < MLA_NOPE + 32)
    return jnp.where(hi, from_left, jnp.where(lo, from_right, 0.0))


def _rope(x, cos_t, sin_t):
    nh = x.shape[1] // LANES
    ct, st = jnp.tile(cos_t, (1, nh)), jnp.tile(sin_t, (1, nh))
    return x * ct + _rope_swap(x) * st


def _rope_t(dy, cos_t, sin_t):
    nh = dy.shape[1] // LANES
    ct, st = jnp.tile(cos_t, (1, nh)), jnp.tile(sin_t, (1, nh))
    return dy * ct + _rope_swap(dy * st)


def _block_cumsum(v, carry, reverse):
    tb = v.shape[0]
    r = lax.broadcasted_iota(jnp.int32, (tb, tb), 0)
    c = lax.broadcasted_iota(jnp.int32, (tb, tb), 1)
    tri = jnp.where((c >= r) if reverse else (c <= r), 1.0, 0.0).astype(BF16)
    hi = v.astype(BF16)
    r1 = v - hi.astype(F32)
    mid = r1.astype(BF16)
    lo = (r1 - mid.astype(F32)).astype(BF16)
    out = carry + sum(jnp.dot(tri, p, preferred_element_type=F32) for p in (hi, mid, lo))
    return out, (out[0:1, :] if reverse else out[tb - 1:tb, :])


def _forget_cumsum(name, logf, batch, seq, heads, tb):
    nb = seq // tb

    def body(x_ref, keys_ref, rows_ref, carry):
        @pl.when(pl.program_id(1) == 0)
        def _():
            carry[...] = jnp.zeros(carry.shape, F32)

        out, carry[...] = _block_cumsum(x_ref[...], carry[...], False)
        keys_ref[...] = jnp.concatenate([jnp.broadcast_to(out[:, h:h + 1], (tb, LANES)) for h in range(heads)], axis=1)
        out_t = out.T
        for g in range(heads // 2):
            rows_ref[g] = out_t[2 * g:2 * g + 2, :]

    return _pcall(
        body,
        name=name,
        grid=(batch, nb),
        in_specs=[pl.BlockSpec((tb, LANES), lambda b, i: (b * nb + i, 0))],
        out_specs=[pl.BlockSpec((tb, heads * LANES), lambda b, i: (b * nb + i, 0)),
                   pl.BlockSpec((None, None, heads // 2, 2, tb), lambda b, i: (b, i, 0, 0, 0))],
        out_shape=[jax.ShapeDtypeStruct((batch * seq, heads * LANES), F32),
                   jax.ShapeDtypeStruct((batch, nb, heads // 2, 2, tb), F32)],
        scratch_shapes=[pltpu.VMEM((1, LANES), F32)],
        compiler_params=_params(4 * tb * (heads + 2) * LANES * 4),
    )(logf)


def _forget_cumsum_bwd(name, dcf, batch, seq, heads, tb):
    nb = seq // tb

    def body(x_ref, o_ref, carry):
        @pl.when(pl.program_id(1) == 0)
        def _():
            carry[...] = jnp.zeros(carry.shape, F32)

        lane = lax.broadcasted_iota(jnp.int32, (1, LANES), 1)
        v = jnp.zeros((tb, LANES), F32)
        for g in range(heads // 2):
            blk = x_ref[:, g * LANES:(g + 1) * LANES]
            moved = pltpu.roll(blk, 2 * g, 1) if g else blk
            v = v + jnp.where((lane >= 2 * g) & (lane < 2 * g + 2), moved, 0.0)
        o_ref[...], carry[...] = _block_cumsum(v, carry[...], True)

    return _pcall(
        body,
        name=name,
        grid=(batch, nb),
        in_specs=[pl.BlockSpec((tb, (heads // 2) * LANES), lambda b, i: (b * nb + nb - 1 - i, 0))],
        out_specs=pl.BlockSpec((tb, LANES), lambda b, i: (b * nb + nb - 1 - i, 0)),
        out_shape=jax.ShapeDtypeStruct((batch * seq, LANES), F32),
        scratch_shapes=[pltpu.VMEM((1, LANES), F32)],
        compiler_params=_params(4 * tb * (heads // 2 + 1) * LANES * 4),
    )(dcf)


class Attn(NamedTuple):
    batch: int
    sq: int
    sk: int
    groups: int
    hq: int
    hv: int
    mode: str
    scale: float
    chunk: int
    tq: int
    tk: int

    @property
    def hg(self):
        return self.hv

    @property
    def qw(self):
        return LANES * self.hg // self.hq

    @property
    def dv(self):
        return LANES // self.hv


def _head_lanes(j, dv):
    lane = lax.broadcasted_iota(jnp.int32, (1, LANES), 1)
    return (lane >= j * dv) & (lane < (j + 1) * dv)


def _head_q(sp, j, q_blk):
    if sp.hq == 2:
        return jnp.where(_head_lanes(j, FOX_DH), q_blk, jnp.zeros_like(q_blk))
    return q_blk[:, LANES * j:LANES * (j + 1)]


def _head_rows(sp, j):
    return slice(j * sp.dv, (j + 1) * sp.dv) if sp.hg == 2 else slice(None)


def _scores_t(sp, j, k_c, q_j, cfq_rows, cfk_rep, k0, q0, masked):
    tk, tq = k_c.shape[0], q_j.shape[0]
    k_j = k_c if sp.hq == 2 else k_c[:, LANES * j:LANES * (j + 1)]
    st = lax.dot_general(k_j, q_j, _DIMS["nt"], preferred_element_type=F32) * sp.scale
    if sp.mode == "fox":
        st = st + (cfq_rows[j:j + 1, :] - jnp.tile(cfk_rep[:, LANES * j:LANES * (j + 1)], (1, tq // LANES)))
    if masked:
        kidx = k0 + lax.broadcasted_iota(jnp.int32, (tk, tq), 0)
        qidx = q0 + lax.broadcasted_iota(jnp.int32, (tk, tq), 1)
        if sp.mode == "chunk":
            shift = sp.chunk.bit_length() - 1
            kidx, qidx = jnp.right_shift(kidx, shift), jnp.right_shift(qidx, shift)
        st = jnp.where(kidx <= qidx, st, NEG_INF)
    return st


def _attn_fwd(name, sp, q, k, v, cfq=None, cfk=None, hosts=None, host=None):
    (qa, qo), (ka, ko), (va, vo) = q, k, v
    tq, tk, hg, qw = sp.tq, sp.tk, sp.hg, sp.qw
    nqb, nkc = sp.sq // tq, sp.sk // tk
    fox, causal = sp.mode == "fox", sp.mode != "none"
    assert sp.sq % tq == 0 and sp.sk % tk == 0 and (not causal or (tq == tk and sp.sq == sp.sk))

    def body(*refs):
        if fox:
            q_ref, k_ref, v_ref, cfq_ref, cfk_ref, o_ref, lse_ref, acc_scr = refs
        else:
            q_ref, k_ref, v_ref, o_ref, lse_ref, acc_scr = refs
        i = pl.program_id(2)
        q0 = i * tq
        q_blk = q_ref[...]
        qs = [_head_q(sp, j, q_blk) for j in range(hg)]
        acc_scr[...] = jnp.zeros(acc_scr.shape, F32)

        def chunk(kc, carry, masked):
            ms, ls = carry
            k0 = pl.multiple_of(kc * tk, tk)
            k_c = k_ref[pl.ds(k0, tk), :]
            v_c = v_ref[pl.ds(k0, tk), :]
            new_m, new_l = [], []
            for j in range(hg):
                st = _scores_t(sp, j, k_c, qs[j], cfq_ref[...] if fox else None,
                               cfk_ref[pl.ds(k0, tk), :] if fox else None, k0, q0, masked)
                m_new = jnp.maximum(ms[j], jnp.max(st, axis=0, keepdims=True))
                alpha = jnp.exp(ms[j] - m_new)
                pt = jnp.exp(st - m_new)
                new_m.append(m_new)
                new_l.append(alpha * ls[j] + jnp.sum(pt, axis=0, keepdims=True))
                pv = lax.dot_general(v_c, pt.astype(BF16), _DIMS["tn"], preferred_element_type=F32)
                r = _head_rows(sp, j)
                acc_scr[r, :] = acc_scr[r, :] * alpha + pv[r, :]
            return tuple(new_m), tuple(new_l)

        carry = (tuple(jnp.full((1, tq), NEG_INF, F32) for _ in range(hg)),
                 tuple(jnp.zeros((1, tq), F32) for _ in range(hg)))
        if causal:
            carry = lax.fori_loop(0, i, functools.partial(chunk, masked=False), carry)
            ms, ls = chunk(i, carry, True)
        else:
            ms, ls = lax.fori_loop(0, nkc, functools.partial(chunk, masked=False), carry)
        for j in range(hg):
            r = _head_rows(sp, j)
            acc_scr[r, :] = acc_scr[r, :] / ls[j]
            lse_ref[j:j + 1, :] = ms[j] + jnp.log(ls[j])
        o_ref[...] = acc_scr[...].T

    in_specs = [
        pl.BlockSpec((tq, qw), lambda b, g, i: (b * nqb + i, qo + g)),
        pl.BlockSpec((sp.sk, qw), lambda b, g, i: (b, ko + g)),
        pl.BlockSpec((sp.sk, LANES), lambda b, g, i: (b, vo + g)),
    ]
    ops = [qa, ka, va]
    stat_blk = pl.BlockSpec((None, None, None, hg, tq), lambda b, g, i: (b, g, i, 0, 0))
    if fox:
        in_specs += [pl.BlockSpec((None, None, None, hg, tq), lambda b, g, i: (b, i, g, 0, 0)),
                     pl.BlockSpec((sp.sk, hg * LANES), lambda b, g, i: (b, g))]
        ops += [cfq, cfk]
    blk = _nbytes((tq, qw), BF16) + _nbytes((sp.sk, qw + LANES), BF16) + 2 * _nbytes((tq, LANES), F32)
    blk += _nbytes((sp.sk, hg * LANES), F32) + 6 * _nbytes((tk, tq), F32)
    call = lambda rider: _pcall(
        body,
        rider=rider,
        name=name,
        grid=(sp.batch, sp.groups, nqb),
        in_specs=in_specs,
        out_specs=[pl.BlockSpec((tq, LANES), lambda b, g, i: (b * nqb + i, g)), stat_blk],
        out_shape=[
            jax.ShapeDtypeStruct((sp.batch * sp.sq, sp.groups * LANES), F32),
            jax.ShapeDtypeStruct((sp.batch, sp.groups, nqb, hg, tq), F32),
        ],
        scratch_shapes=[pltpu.VMEM((LANES, tq), F32)],
        compiler_params=_params(blk, tq * LANES * 4),
    )
    return _hosted(hosts, host, call, ops, 2)


def _attn_bwd(name, sp, q, k, v, o, lse, do, cfq=None, cfk=None, hosts=None, host=None):
    (qa, qo), (ka, ko), (va, vo) = q, k, v
    tq, tk, hg, qw, dv = sp.tq, sp.tk, sp.hg, sp.qw, sp.dv
    nqb, nkb = sp.sq // tq, sp.sk // tk
    fox, causal = sp.mode == "fox", sp.mode != "none"
    assert sp.sq % tq == 0 and sp.sk % tk == 0 and (not causal or (tq == tk and sp.sq == sp.sk))

    def body(*refs):
        if fox:
            (q_ref, k_ref, v_ref, lse_ref, do_ref, cfq_ref, cfk_ref, kall_ref, vall_ref, cfkall_ref,
             dq_ref, dk_ref, dv_ref, dcf_ref, delta_scr, dk_scr, dv_scr, dqt_scr, dcf_scr) = refs
        else:
            (q_ref, k_ref, v_ref, o_ref, lse_ref, do_ref,
             dq_ref, dk_ref, dv_ref, delta_scr, dk_scr, dv_scr, dqt_scr) = refs
        kb = pl.program_id(2)
        k0 = kb * tk
        heads = [_head_lanes(j, dv) for j in range(hg)]

        def head_do(j, do_c):
            return jnp.where(heads[j], do_c, jnp.zeros_like(do_c)) if hg == 2 else do_c

        def probs_t(j, k_c, v_c, q_c, do_c, i, cf_keys, c0, masked):
            st = _scores_t(sp, j, k_c, _head_q(sp, j, q_c), cfq_ref[i] if fox else None, cf_keys, c0, i * tq, masked)
            pt = jnp.exp(st - lse_ref[i][j:j + 1, :])
            dpt = lax.dot_general(v_c, head_do(j, do_c), _DIMS["nt"], preferred_element_type=F32)
            return pt, dpt

        @pl.when(kb == 0)
        def _():
            dqt_scr[...] = jnp.zeros(dqt_scr.shape, F32)

            def fill(i, carry):
                r0 = pl.multiple_of(i * tq, tq)
                do_c = do_ref[pl.ds(r0, tq), :]
                if fox:
                    q_c = q_ref[pl.ds(r0, tq), :]

                    def keys(kc, acc, masked):
                        c0 = pl.multiple_of(kc * tk, tk)
                        out = []
                        for j in range(hg):
                            pt, dpt = probs_t(j, kall_ref[pl.ds(c0, tk), :], vall_ref[pl.ds(c0, tk), :], q_c, do_c, i,
                                              cfkall_ref[pl.ds(c0, tk), :], c0, masked)
                            out.append(acc[j] + jnp.sum(pt * dpt, axis=0, keepdims=True))
                        return tuple(out)

                    d = lax.fori_loop(0, i, functools.partial(keys, masked=False),
                                      tuple(jnp.zeros((1, tq), F32) for _ in range(hg)))
                    d = keys(i, d, True)
                    for j in range(hg):
                        delta_scr[i, j:j + 1, :] = d[j]
                else:
                    prod_t = (do_c.astype(F32) * o_ref[pl.ds(r0, tq), :]).T
                    for j in range(hg):
                        delta_scr[i, j:j + 1, :] = jnp.sum(prod_t[_head_rows(sp, j), :], axis=0, keepdims=True)
                return carry

            lax.fori_loop(0, nqb, fill, 0)

        k_blk = k_ref[...]
        v_blk = v_ref[...]
        k_t = k_blk.astype(F32).T.astype(BF16)
        dk_scr[...] = jnp.zeros(dk_scr.shape, F32)
        dv_scr[...] = jnp.zeros(dv_scr.shape, F32)
        if fox:
            dcf_scr[...] = jnp.zeros(dcf_scr.shape, F32)

        def qblock(i, carry, masked):
            r0 = pl.multiple_of(i * tq, tq)
            q_c = q_ref[pl.ds(r0, tq), :]
            do_c = do_ref[pl.ds(r0, tq), :]
            for j in range(hg):
                pt, dpt = probs_t(j, k_blk, v_blk, q_c, do_c, i, cfk_ref[...] if fox else None, k0, masked)
                dst = pt * (dpt - delta_scr[i][j:j + 1, :])
                if fox:
                    part = dst[:, :LANES]
                    for t in range(1, tq // LANES):
                        part = part + dst[:, t * LANES:(t + 1) * LANES]
                    dcf_scr[j] += part
                ds_b = (dst * sp.scale).astype(BF16)
                dv_scr[j] += jnp.dot(pt.astype(BF16), do_c, preferred_element_type=F32)
                dk_scr[j] += jnp.dot(ds_b, q_c if sp.hq == 2 else _head_q(sp, j, q_c), preferred_element_type=F32)
                if sp.hq == 2:
                    r = pl.ds(j * FOX_DH, FOX_DH)
                    dqt_scr[i, r, :] += jnp.dot(k_t[j * FOX_DH:(j + 1) * FOX_DH, :], ds_b, preferred_element_type=F32)
                else:
                    r = pl.ds(j * LANES, LANES)
                    dqt_scr[i, r, :] += jnp.dot(k_t[j * LANES:(j + 1) * LANES, :], ds_b, preferred_element_type=F32)
            return carry

        if causal:
            qblock(kb, 0, True)
            lax.fori_loop(kb + 1, nqb, functools.partial(qblock, masked=False), 0)
        else:
            lax.fori_loop(0, nqb, functools.partial(qblock, masked=False), 0)

        @pl.when(kb == nkb - 1)
        def _():
            def untranspose(i, carry):
                dq_ref[pl.ds(pl.multiple_of(i * tq, tq), tq), :] = dqt_scr[i].T
                return carry

            lax.fori_loop(0, nqb, untranspose, 0)

        if hg == 2:
            dv_ref[...] = jnp.where(heads[0], dv_scr[0], dv_scr[1]).astype(dv_ref.dtype)
        else:
            dv_ref[...] = dv_scr[0].astype(dv_ref.dtype)
        if sp.hq == 2:
            dk_ref[...] = jnp.where(_head_lanes(0, FOX_DH), dk_scr[0], dk_scr[1]).astype(dk_ref.dtype)
        elif hg == 2:
            dk_ref[...] = jnp.concatenate([dk_scr[0], dk_scr[1]], axis=1).astype(dk_ref.dtype)
        else:
            dk_ref[...] = dk_scr[0].astype(dk_ref.dtype)
        if fox:
            lane = lax.broadcasted_iota(jnp.int32, (1, LANES), 1)
            sums = [jnp.sum(dcf_scr[j], axis=1, keepdims=True) for j in range(hg)]
            dcf_ref[...] = jnp.where(lane == 0, -sums[0], jnp.where(lane == 1, -sums[1], 0.0))

    seq_lanes = lambda b, g, kb: (b, g)
    key_blk = lambda b, g, kb: (b * nkb + kb, g)
    stats = pl.BlockSpec((None, None, nqb, hg, tq), lambda b, g, kb: (b, g, 0, 0, 0))
    in_specs = [
        pl.BlockSpec((sp.sq, qw), lambda b, g, kb: (b, qo + g)),
        pl.BlockSpec((tk, qw), lambda b, g, kb: (b * nkb + kb, ko + g)),
        pl.BlockSpec((tk, LANES), lambda b, g, kb: (b * nkb + kb, vo + g)),
    ]
    ops = [qa, ka, va]
    if not fox:
        in_specs.append(pl.BlockSpec((sp.sq, LANES), seq_lanes))
        ops.append(o)
    in_specs += [stats, pl.BlockSpec((sp.sq, LANES), seq_lanes)]
    ops += [lse, do]
    out_specs = [pl.BlockSpec((sp.sq, qw), seq_lanes), pl.BlockSpec((tk, qw), key_blk), pl.BlockSpec((tk, LANES), key_blk)]
    out_shape = [
        jax.ShapeDtypeStruct((sp.batch * sp.sq, sp.groups * qw), F32),
        jax.ShapeDtypeStruct((sp.batch * sp.sk, sp.groups * qw), BF16),
        jax.ShapeDtypeStruct((sp.batch * sp.sk, sp.groups * LANES), BF16),
    ]
    scratch = [pltpu.VMEM((nqb, hg, tq), F32), pltpu.VMEM((hg, tk, LANES), F32), pltpu.VMEM((hg, tk, LANES), F32),
               pltpu.VMEM((nqb, qw, tq), F32)]
    if fox:
        in_specs += [
            pl.BlockSpec((None, nqb, None, hg, tq), lambda b, g, kb: (b, 0, g, 0, 0)),
            pl.BlockSpec((tk, hg * LANES), key_blk),
            pl.BlockSpec((sp.sk, qw), lambda b, g, kb: (b, ko + g)),
            pl.BlockSpec((sp.sk, LANES), lambda b, g, kb: (b, vo + g)),
            pl.BlockSpec((sp.sk, hg * LANES), seq_lanes),
        ]
        ops += [cfq, cfk, ka, va, cfk]
        out_specs.append(pl.BlockSpec((tk, LANES), key_blk))
        out_shape.append(jax.ShapeDtypeStruct((sp.batch * sp.sk, sp.groups * LANES), F32))
        scratch.append(pltpu.VMEM((hg, tk, LANES), F32))
    blk = _nbytes((sp.sq, qw), BF16) + _nbytes((sp.sq, LANES), BF16) + 2 * _nbytes((sp.sq, LANES), F32)
    blk += _nbytes((sp.sq, qw), F32) + 4 * _nbytes((tk, qw), BF16) + 8 * _nbytes((tq, tk), F32)
    blk += (_nbytes((sp.sk, qw + LANES), BF16) + _nbytes((sp.sk, hg * LANES), F32)) if fox else 0
    call = lambda rider: _pcall(
        body,
        rider=rider,
        name=name,
        grid=(sp.batch, sp.groups, nkb),
        in_specs=in_specs,
        out_specs=out_specs,
        out_shape=out_shape,
        scratch_shapes=scratch,
        compiler_params=_params(blk, _nbytes((sp.sq, LANES), F32) + 4 * _nbytes((tk, LANES), F32)),
    )
    return _hosted(hosts, host, call, ops, len(out_shape))


def _slabwise(name, fn, ins, out_dtypes, rows_per_step=512):
    ins = [a if isinstance(a, tuple) else (a, None) for a in ins]
    n = max(1 if fixed is not None else a.shape[0] for a, fixed in ins)
    rows, cols = ins[0][0].shape[1:]
    tr = min(rows_per_step, rows)
    while rows % tr:
        tr //= 2
    assert tr % 16 == 0 or tr == rows, (name, rows, tr)

    def spec(a, fixed):
        if fixed is not None or a.shape[0] == 1:
            return pl.BlockSpec((None, tr, cols), lambda s, i: (fixed or 0, i, 0))
        return pl.BlockSpec((None, tr, cols), lambda s, i: (s, i, 0))

    def body(*refs):
        vals = fn(*[r[...] for r in refs[:len(ins)]])
        for r, v in zip(refs[len(ins):], vals):
            r[...] = v.astype(r.dtype)

    blk = (len(ins) + len(out_dtypes)) * _nbytes((tr, cols + LANES), F32)
    res = _pcall(
        body,
        name=name,
        grid=(n, rows // tr),
        in_specs=[spec(a, fixed) for a, fixed in ins],
        out_specs=[pl.BlockSpec((None, tr, cols), lambda s, i: (s, i, 0)) for _ in out_dtypes],
        out_shape=[jax.ShapeDtypeStruct((n, rows, cols), dt) for dt in out_dtypes],
        compiler_params=_params(2 * blk),
    )(*[a for a, _ in ins])
    return res


def _adamw_math(w, g, m, v):
    m = ADAM_B1 * m + (1.0 - ADAM_B1) * g
    v = ADAM_B2 * v + (1.0 - ADAM_B2) * jnp.square(g)
    m_hat = m / (1.0 - ADAM_B1 ** ADAM_STEP)
    v_hat = v / (1.0 - ADAM_B2 ** ADAM_STEP)
    delta = -ADAM_LR * (m_hat / (jnp.sqrt(v_hat) + ADAM_EPS) + ADAM_WD * w)
    return delta, m, v


def _run_exchange(name, ex):
    n_in, n_out = len(ex.ins), len(ex.out_shapes)

    def body(*refs):
        copies = _exchange_copies(ex, refs[:n_in], refs[n_in:n_in + n_out], refs[-2], refs[-1])
        for cp in copies:
            cp.start()
        for cp in copies:
            cp.wait()

    any_spec = pl.BlockSpec(memory_space=pl.ANY)
    return _pcall(
        body,
        name=name,
        in_specs=[any_spec] * n_in,
        out_specs=[any_spec] * n_out,
        out_shape=list(ex.out_shapes),
        scratch_shapes=[pltpu.SemaphoreType.DMA((ex.n_copies,))] * 2,
        input_output_aliases=dict(ex.aliases),
    )(*ex.ins)


def _chip(rel=0):
    x, y = lax.axis_index("x"), lax.axis_index("y")
    return 2 * ((1 - x) if rel & 1 else x) + ((1 - y) if rel & 2 else y)


def _gather_ici(shards):
    def plan(in_refs, out_refs):
        c = lax.axis_index("c")
        return [(s.at[c], g.at[_chip(), c], rel) for s, g in zip(in_refs, out_refs) for rel in (1, 2, 3)]

    shapes = tuple(jax.ShapeDtypeStruct((N_CHIPS,) + s.shape, s.dtype) for s in shards)
    return Exchange(tuple(shards), shapes, plan, 3 * len(shards))


def _gather_d2d(got):
    def plan(in_refs, out_refs):
        c = lax.axis_index("c")
        return [(g_in.at[_chip(rel), c], g_out.at[_chip(rel), c], "c")
                for g_in, g_out in zip(in_refs, out_refs) for rel in (1, 2, 3)]

    shapes = tuple(jax.ShapeDtypeStruct(g.shape, g.dtype) for g in got)
    return Exchange(tuple(got), shapes, plan, 3 * len(got), {i: i for i in range(len(got))})


GATHER_FIRST = 5


class GatherNext(Hosts):
    def __init__(self, shards):
        self.shards, self.got = shards, []

    def rider(self, host):
        if host == "fox_fwd":
            return _gather_ici(self.shards[:GATHER_FIRST])
        if host == "mla_fwd":
            return _gather_ici(self.shards[GATHER_FIRST:])
        if host == "mem_fwd":
            return _gather_d2d(self.got)
        return None

    def done(self, host, outs):
        self.got = outs if host == "mem_fwd" else self.got + outs


def _gather_now(shards):
    got = _run_exchange("gather_weights_ici", _gather_ici(shards))
    return _run_exchange("gather_weights_d2d", _gather_d2d(got))


def _pair_sum(name, g, recv, rows_per_step=512):
    _, _, rows, cols = g.shape
    tr = min(rows_per_step, rows)
    while rows % tr:
        tr //= 2

    def body(g_ref, r_ref, p_ref, own_ref):
        mine = jnp.where(lax.axis_index("c") == 0, g_ref[0], g_ref[1])
        p = mine.astype(F32) + r_ref[...].astype(F32)
        p_ref[...] = p.astype(p_ref.dtype)

        @pl.when(pl.program_id(1) == _chip())
        def _():
            own_ref[...] = p

    slab = pl.BlockSpec((None, tr, cols), lambda i, s: (s, i, 0))
    return _pcall(
        body,
        name=name,
        grid=(rows // tr, N_CHIPS),
        in_specs=[pl.BlockSpec((None, 2, tr, cols), lambda i, s: (s, 0, i, 0)), slab],
        out_specs=[slab, pl.BlockSpec((None, tr, cols), lambda i, s: (0, i, 0))],
        out_shape=[jax.ShapeDtypeStruct((N_CHIPS, rows, cols), BF16), jax.ShapeDtypeStruct((1, rows, cols), F32)],
        compiler_params=_params(2 * 6 * _nbytes((tr, cols + LANES), F32)),
    )(g, recv)


class ReduceLayer(Hosts):
    def __init__(self, tag, grads, hosts):
        self.tag, self.grads, self.pair, self.own, self.total, self.theirs = tag, grads, None, None, None, None
        self.stage_of = dict(zip(hosts, ("swap", "chips", "share")))

    def _swap_halves(self):
        def plan(in_refs, out_refs):
            c = lax.axis_index("c")
            return [(g.at[pl.ds(0, N_CHIPS), 1 - c], r, "c") for g, r in zip(in_refs, out_refs)]

        shapes = tuple(jax.ShapeDtypeStruct((N_CHIPS,) + g.shape[2:], g.dtype) for g in self.grads)
        return Exchange(tuple(self.grads), shapes, plan, len(self.grads))

    def _to_chips(self):
        def plan(in_refs, out_refs):
            return [(p.at[_chip(rel)], r.at[rel - 1], rel) for p, r in zip(in_refs, out_refs) for rel in (1, 2, 3)]

        shapes = tuple(jax.ShapeDtypeStruct((3,) + p.shape[1:], p.dtype) for p in self.pair)
        return Exchange(tuple(self.pair), shapes, plan, 3 * len(self.pair))

    def _share(self):
        def plan(in_refs, out_refs):
            return [(t, r, "c") for t, r in zip(in_refs, out_refs)]

        shapes = tuple(jax.ShapeDtypeStruct(t.shape, F32) for t in self.total)
        return Exchange(tuple(self.total), shapes, plan, len(self.total))

    def rider(self, host):
        stages = {"swap": self._swap_halves, "chips": self._to_chips, "share": self._share}
        return stages[self.stage_of[host]]() if host in self.stage_of else None

    def done(self, host, outs):
        self._after(self.stage_of[host], outs)

    def _after(self, stage, outs):
        if stage == "swap":
            sums = [_pair_sum(f"reduce_pair_sum_{self.tag}_{i}", g, r) for i, (g, r) in enumerate(zip(self.grads, outs))]
            self.pair, self.own = [s[0] for s in sums], [s[1] for s in sums]
        elif stage == "chips":
            self.total = [
                _slabwise(f"reduce_chip_sum_{self.tag}_{i}",
                          lambda a, b, c_, d: (a + b.astype(F32) + c_.astype(F32) + d.astype(F32),),
                          [own, (r, 0), (r, 1), (r, 2)], [F32])[0]
                for i, (own, r) in enumerate(zip(self.own, outs))]
        else:
            self.theirs = outs

    def run_now(self):
        self._after("swap", _run_exchange(f"reduce_pair_{self.tag}", self._swap_halves()))
        self._after("chips", _run_exchange(f"reduce_chips_{self.tag}", self._to_chips()))
        self._after("share", _run_exchange(f"reduce_share_{self.tag}", self._share()))

    def result(self):
        return list(zip(self.total, self.theirs))


def _adamw_layer(name, l, w, m, v, mine, theirs, prev, rows_per_step=256):
    _, _, rows, cols = w.shape
    tr = min(rows_per_step, rows)
    while rows % tr:
        tr //= 2

    def body(w_ref, m_ref, v_ref, mine_ref, theirs_ref, *rest):
        g_ref, d_ref, nm_ref, nv_ref = rest[-4:]
        g = jnp.where(pl.program_id(0) == lax.axis_index("c"), mine_ref[...], theirs_ref[...])
        d, nm, nv = _adamw_math(w_ref[...], g, m_ref[...], v_ref[...])
        g_ref[...], d_ref[...], nm_ref[...], nv_ref[...] = g, d, nm, nv

    half = pl.BlockSpec((None, None, tr, cols), lambda h, i: (l, h, i, 0))
    one = pl.BlockSpec((None, tr, cols), lambda h, i: (0, i, 0))
    kept = [] if prev is None else list(prev)
    return _pcall(
        body,
        name=name,
        grid=(2, rows // tr),
        in_specs=[half, half, half, one, one] + [pl.BlockSpec(memory_space=pl.ANY)] * len(kept),
        out_specs=[half] * 4,
        out_shape=[jax.ShapeDtypeStruct(w.shape, F32)] * 4,
        input_output_aliases={5 + i: i for i in range(len(kept))},
        compiler_params=_params(2 * 9 * _nbytes((tr, cols + LANES), F32)),
    )(w, m, v, mine, theirs, *kept)


def _allreduce_small(v):
    rows = v.shape[0]

    def body(v_ref, sum_ref, all_ref, send_sems, recv_sems, local_sem):
        x, y, c = lax.axis_index("x"), lax.axis_index("y"), lax.axis_index("c")
        sibling = (x, y, 1 - c)
        chips = [(1 - x, y), (x, 1 - y), (1 - x, 1 - y)]

        def slab(px, py, pc):
            return all_ref.at[pl.ds((4 * px + 2 * py + pc) * rows, rows), :]

        def copy(k, block, to, src=None):
            return pltpu.make_async_remote_copy(
                src_ref=slab(*block) if src is None else src, dst_ref=slab(*block), send_sem=send_sems.at[k],
                recv_sem=recv_sems.at[k], device_id=to, device_id_type=MESH)

        mine = pltpu.make_async_copy(v_ref, slab(x, y, c), local_sem)
        mine.start()
        first = [copy(0, (x, y, c), sibling, src=v_ref)]
        first += [copy(1 + j, (x, y, c), (*chip, c), src=v_ref) for j, chip in enumerate(chips)]
        for cp in first:
            cp.start()
        passed = [copy(4 + j, (*chip, c), sibling) for j, chip in enumerate(chips)]
        for j, chip in enumerate(chips):
            copy(1 + j, (*chip, c), (x, y, c)).wait_recv()
            passed[j].start()
        copy(0, (x, y, 1 - c), (x, y, c)).wait_recv()
        for j, chip in enumerate(chips):
            copy(4 + j, (*chip, 1 - c), (x, y, c)).wait_recv()
        for cp in first + passed:
            cp.wait_send()
        mine.wait()
        total = all_ref[pl.ds(0, rows), :]
        for d in range(1, N_DEV):
            total = total + all_ref[pl.ds(d * rows, rows), :]
        sum_ref[...] = total

    vm = pl.BlockSpec(memory_space=pltpu.VMEM)
    return _pcall(
        body,
        name="allreduce_small",
        in_specs=[vm],
        out_specs=vm,
        out_shape=jax.ShapeDtypeStruct((rows, LANES), F32),
        scratch_shapes=[pltpu.VMEM((N_DEV * rows, LANES), F32), pltpu.SemaphoreType.DMA((7,)),
                        pltpu.SemaphoreType.DMA((7,)), pltpu.SemaphoreType.DMA],
    )(v)


def _pad_cols(a, before, total):
    return jnp.pad(a, ((0, 0), (before, total - before - a.shape[1])))


def _layer_weights(cfg, w_in, w_uq, w_ukv):
    w = cfg.width
    qkv, f, cq, ckv, kr, qm, gates = jnp.split(w_in, list(_cumsum(cfg.in_splits))[:-1], axis=1)
    wa = jnp.concatenate([qkv, qm], axis=1)
    ws = jnp.concatenate([_pad_cols(f, 0, LANES), cq, ckv, _pad_cols(kr, MLA_NOPE, LANES)], axis=1)
    wq = jnp.pad(w_uq.reshape(cfg.q_rank, cfg.mla_h, MLA_NOPE + MLA_ROPE), ((0, 0), (0, 0), (0, LANES - MLA_NOPE - MLA_ROPE)))
    wq = wq.reshape(cfg.q_rank, cfg.mla_h * LANES)
    kv = w_ukv.reshape(cfg.kv_rank, cfg.mla_h, MLA_NOPE + MLA_V)
    wk = jnp.pad(kv[:, :, :MLA_NOPE], ((0, 0), (0, 0), (0, LANES - MLA_NOPE))).reshape(cfg.kv_rank, cfg.mla_h * LANES)
    wv = kv[:, :, MLA_NOPE:].reshape(cfg.kv_rank, cfg.mla_h * MLA_V)
    del w
    return wa, gates, ws, wq, wk, wv


def _cumsum(xs):
    out, t = [], 0
    for v in xs:
        t += v
        out.append(t)
    return out


def _layer_weight_grads(cfg, dwa, dwg, dws, dwq, dwk, dwv):
    w, qr, kvr = cfg.width, cfg.q_rank, cfg.kv_rank
    off_kr = LANES + qr + kvr + MLA_NOPE
    dw_in = jnp.concatenate([
        dwa[:, :3 * w], dws[:, :cfg.fox_h], dws[:, LANES:LANES + qr], dws[:, LANES + qr:LANES + qr + kvr],
        dws[:, off_kr:off_kr + MLA_ROPE], dwa[:, 3 * w:], dwg], axis=1)
    dw_uq = dwq.reshape(qr, cfg.mla_h, LANES)[:, :, :MLA_NOPE + MLA_ROPE].reshape(qr, cfg.mla_h * (MLA_NOPE + MLA_ROPE))
    dw_ukv = jnp.concatenate([dwk.reshape(kvr, cfg.mla_h, LANES)[:, :, :MLA_NOPE], dwv.reshape(kvr, cfg.mla_h, MLA_V)],
                             axis=2).reshape(kvr, cfg.mla_h * (MLA_NOPE + MLA_V))
    return dw_in, dw_uq, dw_ukv


def _attn_specs(cfg, batch):
    t = min(ATTN_TILE, cfg.seq)
    common = dict(batch=batch, sq=cfg.seq, chunk=cfg.chunk, tq=t)
    fox = Attn(sk=cfg.seq, groups=cfg.fox_h // 2, hq=2, hv=2, mode="fox", scale=FOX_DH ** -0.5, tk=t, **common)
    mla = Attn(sk=cfg.seq, groups=cfg.mla_h // 2, hq=1, hv=2, mode="chunk",
               scale=(MLA_NOPE + MLA_ROPE) ** -0.5, tk=t, **common)
    mem = Attn(sk=cfg.n_mem, groups=cfg.mem_h, hq=1, hv=1, mode="none", scale=MEM_DH ** -0.5, tk=cfg.n_mem, **common)
    return fox, mla, mem


def _small_core(cfg, ps, bf, gq, gkv):
    qr, kvr = cfg.q_rank, cfg.kv_rank
    z = ps[:, :LANES] + bf
    logf = jnp.minimum(z, 0.0) - jnp.log1p(jnp.exp(-jnp.abs(z)))
    nq = _rms(ps[:, LANES:LANES + qr], gq)
    nkv = _rms(ps[:, LANES + qr:LANES + qr + kvr], gkv)
    return logf, nq, nkv


def _layer_fwd(cfg, l, batch, h, hb, mem_b, rope_c, rope_s, hosts, lw, bf_pad, g_cq, g_ckv, ln1, ln2):
    wa, wg, ws, wq, wk, wv, wmkv, wbr, wout, wff1, wff2 = lw
    w, d = cfg.width, cfg.d
    fox, mla, mem = _attn_specs(cfg, batch)
    nw = w // LANES
    pa = _mm(f"proj_a_{l}", "nn", [(hb, wa)], [BF16])
    gl = _mm(f"proj_gates_{l}", "nn", [(hb, wg)], [F32])
    ps = _mm(f"proj_small_{l}", "nn", [(hb, ws)], [F32], tn=cfg.small_w)

    def small_fwd(ps_, c_, s_, bf_, gq_, gkv_):
        logf, nq, nkv = _small_core(cfg, ps_, bf_, gq_, gkv_)
        kpe = _rope(ps_[:, cfg.small_w - LANES:], c_, s_)
        return logf, nq, nkv, kpe

    logf, nq, nkv, kpe = _rowwise(
        f"small_fwd_{l}", small_fwd, [ps, rope_c, rope_s], [bf_pad, g_cq, g_ckv],
        [(LANES, F32), (cfg.q_rank, BF16), (cfg.kv_rank, BF16), (LANES, F32)])
    cfk, cfq = _forget_cumsum(f"cum_forget_{l}", logf, batch, cfg.seq, cfg.fox_h, fox.tq)

    qf = _mm(f"mla_q_{l}", "nn", [(nq, wq)], [BF16], tn=wq.shape[1],
             epi=lambda acc, c_, s_: (_rope(acc, c_, s_),), row_extras=[rope_c, rope_s])
    kf = _mm(f"mla_k_{l}", "nn", [(nkv, wk)], [BF16], tn=wk.shape[1],
             epi=lambda acc, kp: (acc + jnp.tile(kp, (1, cfg.mla_h)),), row_extras=[kpe])
    vb = _mm(f"mla_v_{l}", "nn", [(nkv, wv)], [BF16])
    mkv = _mm(f"mem_kv_{l}", "nn", [(mem_b, wmkv)], [BF16])

    o_a, lse_a = _attn_fwd(f"fox_fwd_{l}", fox, (pa, 0), (pa, nw), (pa, 2 * nw), cfq, cfk,
                           hosts=hosts, host="fox_fwd")
    o_b, lse_b = _attn_fwd(f"mla_fwd_{l}", mla, (qf, 0), (kf, 0), (vb, 0), hosts=hosts, host="mla_fwd")
    o_c, lse_c = _attn_fwd(f"mem_fwd_{l}", mem, (pa, 3 * nw), (mkv, 0), (mkv, nw), hosts=hosts, host="mem_fwd")
    bps = [_mm(f"branch_{n}_{l}", "nn", [(o, wbr[n])], [F32]) for n, o in enumerate((o_a, o_b, o_c))]

    def merge(gl_, b0, b1, b2):
        g = jax.nn.sigmoid(gl_)
        return (g[:, :d] * b0 + g[:, d:2 * d] * b1 + g[:, 2 * d:] * b2,)

    (merged,) = _rowwise(f"merge_{l}", merge, [gl] + bps, [], [(d, BF16)])

    def post_ln(acc, res, g_, b_):
        z = cfg.alpha * res + acc
        y = _ln(z, g_, b_)
        return z, y, y

    z1, h1, h1b = _mm(f"out_ln1_{l}", "nn", [(merged, wout)], [F32, F32, BF16], tm=256, tn=d,
                      epi=post_ln, row_extras=[h], bc_extras=list(ln1))
    u, a = _mm(f"ff1_{l}", "nn", [(h1b, wff1)], [BF16, BF16],
               epi=lambda acc: (acc, jnp.square(jnp.maximum(acc, 0.0))))
    z2, h2, h2b = _mm(f"ff2_ln2_{l}", "nn", [(a, wff2)], [F32, F32, BF16], tm=256, tn=d,
                      epi=post_ln, row_extras=[h1], bc_extras=list(ln2))
    saved = dict(hb=hb, pa=pa, gl=gl, ps=ps, nq=nq, nkv=nkv, cfq=cfq, cfk=cfk, qf=qf, kf=kf, vb=vb, mkv=mkv,
                 o=(o_a, o_b, o_c), lse=(lse_a, lse_b, lse_c), bps=bps, merged=merged, z1=z1, h1b=h1b, u=u, a=a, z2=z2)
    return h2, h2b, saved


def _ln_bwd(name, cfg, ga, gb, z, g, b):
    d = cfg.d

    def fn(*vals):
        if gb is None:
            ga_, z_, g_, b_ = vals
            dy = ga_
        else:
            ga_, gb_, z_, g_, b_ = vals
            dy = ga_ + cfg.alpha * gb_
        _, vjp = jax.vjp(_ln, z_, g_, b_)
        dz, dg, db = vjp(dy)
        return dz, dz, dg, db

    rows = [ga, z] if gb is None else [ga, gb, z]
    return _rowwise(name, fn, rows, [g, b], [(d, F32), (d, BF16)], accs=[(1, d), (1, d)])


def _layer_bwd(cfg, l, batch, ga, gb, sv, mem_b, rope_c, rope_s, hosts, lw, bf_pad, g_cq, g_ckv, ln1, ln2):
    wa, wg, ws, wq, wk, wv, wmkv, wbr, wout, wff1, wff2 = lw
    w, d = cfg.width, cfg.d
    fox, mla, mem = _attn_specs(cfg, batch)
    nw = w // LANES
    gdt = BF16

    dz2, dz2b, dg2, db2 = _ln_bwd(f"ln2_bwd_{l}", cfg, ga, gb, sv["z2"], *ln2)
    du = _mm(f"ff2_bwd_x_{l}", "nt", [(dz2b, wff2)], [BF16],
             epi=lambda acc, u_: (acc * (2.0 * jnp.maximum(u_.astype(F32), 0.0)),), row_extras=[sv["u"]],
             hosts=hosts, host="ff2_bwd_x")
    dwff2 = _mm(f"ff2_bwd_w_{l}", "tn", [(sv["a"], dz2b)], [gdt])
    dwff1 = _mm(f"ff1_bwd_w_{l}", "tn", [(sv["h1b"], du)], [gdt])
    dh1 = _mm(f"ff1_bwd_x_{l}", "nt", [(du, wff1)], [F32])
    dz1, dz1b, dg1, db1 = _ln_bwd(f"ln1_bwd_{l}", cfg, dh1, dz2, sv["z1"], *ln1)
    dmerged = _mm(f"out_bwd_x_{l}", "nt", [(dz1b, wout)], [F32])
    dwout = _mm(f"out_bwd_w_{l}", "tn", [(sv["merged"], dz1b)], [gdt])

    def merge_bwd(dm, gl_, b0, b1, b2):
        def f(gl__, b0_, b1_, b2_):
            g = jax.nn.sigmoid(gl__)
            return g[:, :d] * b0_ + g[:, d:2 * d] * b1_ + g[:, 2 * d:] * b2_

        _, vjp = jax.vjp(f, gl_, b0, b1, b2)
        return vjp(dm)

    dgl, db0, db1_, db2_ = _rowwise(f"merge_bwd_{l}", merge_bwd, [dmerged, sv["gl"]] + sv["bps"], [],
                                    [(3 * d, BF16), (d, BF16), (d, BF16), (d, BF16)])
    dbps = (db0, db1_, db2_)
    dos = [_mm(f"branch_bwd_x_{n}_{l}", "nt", [(dbps[n], wbr[n])], [BF16]) for n in range(3)]
    dwbr = jnp.stack([_mm(f"branch_bwd_w_{n}_{l}", "tn", [(sv["o"][n], dbps[n])], [gdt]) for n in range(3)])
    if hosts is not None:
        hosts.early_grads(dict(w_br=dwbr, w_out=dwout, w_ff1=dwff1, w_ff2=dwff2))

    pa = sv["pa"]
    dq_a, dk_a, dv_a, dcfk = _attn_bwd(f"fox_bwd_{l}", fox, (pa, 0), (pa, nw), (pa, 2 * nw), sv["o"][0], sv["lse"][0],
                                       dos[0], sv["cfq"], sv["cfk"], hosts=hosts, host="fox_bwd")
    dqf, dkf, dvb = _attn_bwd(f"mla_bwd_{l}", mla, (sv["qf"], 0), (sv["kf"], 0), (sv["vb"], 0), sv["o"][1],
                              sv["lse"][1], dos[1], hosts=hosts, host="mla_bwd")
    dqm, dmk, dmv = _attn_bwd(f"mem_bwd_{l}", mem, (pa, 3 * nw), (sv["mkv"], 0), (sv["mkv"], nw), sv["o"][2],
                              sv["lse"][2], dos[2], hosts=hosts, host="mem_bwd")
    dwmkv = _mm(f"mem_kv_bwd_w_{l}", "tn", [(mem_b, jnp.concatenate([dmk, dmv], axis=1))], [gdt])

    (dq_raw,) = _rowwise(f"mla_q_rope_bwd_{l}", lambda dy, c_, s_: (_rope_t(dy, c_, s_),), [dqf, rope_c, rope_s], [],
                         [(wq.shape[1], BF16)])
    dwq = _mm(f"mla_q_bwd_w_{l}", "tn", [(sv["nq"], dq_raw)], [gdt])
    dnq = _mm(f"mla_q_bwd_x_{l}", "nt", [(dq_raw, wq)], [F32])
    dwk = _mm(f"mla_k_bwd_w_{l}", "tn", [(sv["nkv"], dkf)], [gdt])
    dwv = _mm(f"mla_v_bwd_w_{l}", "tn", [(sv["nkv"], dvb)], [gdt])
    dnkv = _mm(f"mla_kv_bwd_x_{l}", "nt", [(dkf, wk), (dvb, wv)], [F32])

    dlogf = _forget_cumsum_bwd(f"cum_forget_bwd_{l}", dcfk, batch, cfg.seq, cfg.fox_h, fox.tq)

    def small_bwd(ps_, dlogf_, dnq_, dnkv_, dkf_, c_, s_, bf_, gq_, gkv_):
        _, vjp = jax.vjp(functools.partial(_small_core, cfg), ps_, bf_, gq_, gkv_)
        dps, dbf, dgq, dgkv = vjp((dlogf_, dnq_, dnkv_))
        dkpe = dkf_[:, :LANES].astype(F32)
        for hh in range(1, cfg.mla_h):
            dkpe = dkpe + dkf_[:, hh * LANES:(hh + 1) * LANES].astype(F32)
        lane = lax.broadcasted_iota(jnp.int32, (1, LANES), 1)
        dkpe = jnp.where((lane >= MLA_NOPE) & (lane < MLA_NOPE + MLA_ROPE), dkpe, 0.0)
        dkr = _rope_t(dkpe, c_, s_)
        dps = jnp.concatenate([dps[:, :cfg.small_w - LANES], dkr], axis=1)
        return dps, dbf, dgq, dgkv

    dps, dbf, dgq, dgkv = _rowwise(
        f"small_bwd_{l}", small_bwd, [sv["ps"], dlogf, dnq, dnkv, dkf, rope_c, rope_s], [bf_pad, g_cq, g_ckv],
        [(cfg.small_w, BF16)], accs=[(1, LANES), (1, cfg.q_rank), (1, cfg.kv_rank)])

    dpa = jnp.concatenate([dq_a.astype(BF16), dk_a, dv_a, dqm.astype(BF16)], axis=1)
    hb = sv["hb"]
    dh = _mm(f"proj_bwd_x_{l}", "nt", [(dpa, wa), (dgl, wg), (dps, ws)], [F32], tn=d)
    dwa = _mm(f"proj_a_bwd_w_{l}", "tn", [(hb, dpa)], [gdt])
    dwg = _mm(f"proj_gates_bwd_w_{l}", "tn", [(hb, dgl)], [gdt])
    dws = _mm(f"proj_small_bwd_w_{l}", "tn", [(hb, dps)], [gdt], tn=cfg.small_w)
    dw_in, dw_uq, dw_ukv = _layer_weight_grads(cfg, dwa, dwg, dws, dwq, dwk, dwv)
    big = dict(w_in=dw_in, w_uq=dw_uq, w_ukv=dw_ukv, w_mem_kv=dwmkv, w_br=dwbr, w_out=dwout,
               w_ff1=dwff1, w_ff2=dwff2)
    small = dict(b_forget=dbf[0, :cfg.fox_h], g_cq=dgq[0], g_ckv=dgkv[0], ln1_g=dg1[0], ln1_b=db1[0],
                 ln2_g=dg2[0], ln2_b=db2[0])
    return dh, dz1, big, small


def _rope_tables(positions):
    inv_freq = ROPE_BASE ** (-jnp.arange(0, MLA_ROPE, 2, dtype=F32) / MLA_ROPE)
    ang = positions.astype(F32).reshape(-1)[:, None] * inv_freq
    cos, sin = jnp.cos(ang), jnp.sin(ang)
    t = ang.shape[0]
    rope_c = jnp.concatenate([jnp.ones((t, MLA_NOPE), F32), cos, cos, jnp.zeros((t, LANES - MLA_NOPE - MLA_ROPE), F32)], axis=1)
    rope_s = jnp.concatenate([jnp.zeros((t, MLA_NOPE), F32), -sin, sin, jnp.zeros((t, LANES - MLA_NOPE - MLA_ROPE), F32)], axis=1)
    return rope_c, rope_s


def _local_step(cfg, x, mem, positions, target, small_w, comm):
    batch = x.shape[0]
    d, depth = cfg.d, cfg.depth
    t = batch * cfg.seq
    x2, tgt = x.reshape(t, d), target.reshape(t, d)
    mem_b = mem.reshape(batch * cfg.n_mem, d).astype(BF16)
    rope_c, rope_s = _rope_tables(positions)
    row = lambda v: v.reshape(1, -1)
    ln_in = (row(small_w["ln_in_g"]), row(small_w["ln_in_b"]))

    h, hb = _rowwise("ln_in", lambda x_, g_, b_: (_ln(x_, g_, b_),) * 2, [x2], list(ln_in), [(d, F32), (d, BF16)])
    layers, saves = [], []
    for l in range(depth):
        big_w = comm.weights(l)
        lw = _layer_weights(cfg, big_w["w_in"], big_w["w_uq"], big_w["w_ukv"]) + (
            big_w["w_mem_kv"], big_w["w_br"], big_w["w_out"], big_w["w_ff1"], big_w["w_ff2"])
        par = dict(
            lw=lw, bf_pad=jnp.pad(row(small_w["b_forget"][l]), ((0, 0), (0, LANES - cfg.fox_h))),
            g_cq=row(small_w["g_cq"][l]), g_ckv=row(small_w["g_ckv"][l]),
            ln1=(row(small_w["ln1_g"][l]), row(small_w["ln1_b"][l])),
            ln2=(row(small_w["ln2_g"][l]), row(small_w["ln2_b"][l])))
        layers.append(par)
        h, hb, sv = _layer_fwd(cfg, l, batch, h, hb, mem_b, rope_c, rope_s, comm.forward_hosts(l), **par)
        saves.append(sv)

    def loss_fn(y, tg):
        err = y - tg
        part = 0.5 * jnp.sum(jnp.mean(err * err, axis=-1, keepdims=True), axis=0, keepdims=True)
        return err * (1.0 / d), jnp.broadcast_to(part, (1, LANES))

    ga, loss_acc = _rowwise("loss", loss_fn, [h, tgt], [], [(d, F32)], accs=[(1, LANES)])
    gb = None
    small_g = {k: [None] * depth for k in ("b_forget", "g_cq", "g_ckv", "ln1_g", "ln1_b", "ln2_g", "ln2_b")}
    for l in reversed(range(depth)):
        ga, gb, big, small = _layer_bwd(cfg, l, batch, ga, gb, saves[l], mem_b, rope_c, rope_s,
                                        comm.backward_hosts(l), **layers[l])
        comm.grads(l, big)
        for k, v in small.items():
            small_g[k][l] = v
    dx, _, dg_in, db_in = _ln_bwd("ln_in_bwd", cfg, ga, gb, x2, *ln_in)
    small_g = {k: jnp.stack(v) for k, v in small_g.items()}
    small_g["ln_in_g"], small_g["ln_in_b"] = dg_in[0], db_in[0]
    return loss_acc[0, 0], dx.reshape(x.shape), small_g


BIG = ("w_in", "w_uq", "w_ukv", "w_mem_kv", "w_br", "w_out", "w_ff1", "w_ff2")
SMALL = ("ln_in_g", "ln_in_b", "b_forget", "g_cq", "g_ckv", "ln1_g", "ln1_b", "ln2_g", "ln2_b")
ROW_CUT = ("w_mem_kv", "w_out", "w_ff2")
LATE, EARLY = BIG[:4], BIG[4:]


def _shard_2d(a):
    cols = a.shape[-1]
    rows = a.size // cols
    return a.reshape(2, rows // 2, cols)


def _full_from_slots(name, slots, shard_shape):
    if name in ROW_CUT and len(shard_shape) == 2:
        return slots.reshape((N_CHIPS * shard_shape[0], shard_shape[1]))
    parts = slots.reshape((N_CHIPS,) + shard_shape)
    axis = len(shard_shape) - (2 if name in ROW_CUT else 1)
    return jnp.concatenate([parts[i] for i in range(N_CHIPS)], axis=axis)


def _slots_from_full(name, full, shard_shape):
    if name in ROW_CUT and len(shard_shape) == 2:
        return full.reshape(N_CHIPS, 2, shard_shape[0] // 2, shard_shape[1])
    axis = len(shard_shape) - (2 if name in ROW_CUT else 1)
    parts = jnp.stack(jnp.split(full, N_CHIPS, axis=axis))
    cols = shard_shape[-1]
    return parts.reshape(N_CHIPS, 2, -1, cols)


def _pack_small(cfg, vals):
    flat = jnp.concatenate([vals[k].reshape(-1).astype(F32) for k in SMALL])
    pad = (-flat.shape[0]) % (LANES * LANES)
    return jnp.pad(flat, (0, pad)).reshape(-1, LANES)


def _unpack_small(packed, like):
    flat, out, off = packed.reshape(-1), {}, 0
    for k in SMALL:
        n = like[k].size
        out[k] = flat[off:off + n].reshape(like[k].shape)
        off += n
    return out


class LayerComm:
    def __init__(self, cfg, w, m, v):
        self.cfg, self.w = cfg, w
        self.shards = [[_shard_2d(w[k][l].astype(BF16)) for k in BIG] for l in range(cfg.depth)]
        self.got = _gather_now(self.shards[0])
        self.next = None
        self.pending = None
        halves = lambda a: a.reshape(a.shape[0], 2, -1, a.shape[-1])
        self.state = {k: [halves(a[k]) for a in (w, m, v)] for k in BIG}
        self.outs = {k: None for k in BIG}

    def weights(self, l):
        if l > 0:
            self.got = self.next.got
        mine = (_chip(), 0, 0, 0)
        return {k: _full_from_slots(k, lax.dynamic_update_slice(g, s[None], mine), self.w[k].shape[1:])
                for k, g, s in zip(BIG, self.got, self.shards[l])}

    def forward_hosts(self, l):
        self.next = GatherNext(self.shards[l + 1]) if l + 1 < self.cfg.depth else Hosts()
        return self.next

    def _reduce(self, tag, names, big, hosts):
        return ReduceLayer(tag, [_slots_from_full(k, big[k], self.w[k].shape[1:]) for k in names], hosts)

    def backward_hosts(self, l):
        comm = self

        class Riders(Together):
            def early_grads(self, grads):
                comm.early = comm._reduce(f"{l}e", EARLY, grads, ("fox_bwd", "mla_bwd", "mem_bwd"))
                self.members.append(comm.early)

        return Riders([self.pending[1]] if self.pending else [])

    def _update(self, l, names, reduce):
        for k, (mine, theirs) in zip(names, reduce.result()):
            self.outs[k] = _adamw_layer(f"adamw_{k}_{l}", l, *self.state[k], mine, theirs, self.outs[k])

    def grads(self, l, big):
        self._update(l, EARLY, self.early)
        if self.pending:
            self._update(self.pending[0], LATE, self.pending[1])
        self.pending = (l, self._reduce(f"{l}l", LATE, big, ("ff2_bwd_x", "fox_bwd", "mla_bwd")))

    def finish(self):
        self.pending[1].run_now()
        self._update(self.pending[0], LATE, self.pending[1])
        return {k: tuple(a.reshape(self.w[k].shape) for a in self.outs[k]) for k in BIG}


def _step(cfg, x, mem, positions, target, w, m, v):
    comm = LayerComm(cfg, w, m, v)
    small_w = {k: w[k] for k in SMALL}
    loss_local, dx, small_g = _local_step(cfg, x, mem, positions, target, small_w, comm)
    loss = lax.psum(loss_local, ("x", "y", "c"))
    outs_big = comm.finish()

    g_small = _allreduce_small(_pack_small(cfg, small_g))
    packs = [_pack_small(cfg, {k: d_[k] for k in SMALL}) for d_ in (w, m, v)]
    dl, nm, nv = _slabwise("adamw_small", _adamw_math, [a[None] for a in (packs[0], g_small, packs[1], packs[2])],
                           [F32, F32, F32])
    outs_small = [_unpack_small(a[0] if a.ndim == 3 else a, w) for a in (g_small, dl, nm, nv)]

    names = SMALL[:2] + ("w_in", "b_forget", "w_uq", "g_cq", "w_ukv", "g_ckv", "w_mem_kv", "w_br", "w_out",
                         "ln1_g", "ln1_b", "w_ff1", "w_ff2", "ln2_g", "ln2_b")
    result = [loss, dx]
    for part in range(4):
        for k in names:
            result.append(outs_big[k][part] if k in outs_big else outs_small[part][k])
    return tuple(result)


def kernel(x, mem, positions, ln_in_g, ln_in_b, w_in, b_forget, w_uq, g_cq, w_ukv, g_ckv, w_mem_kv, w_br, w_out, ln1_g, ln1_b, w_ff1, w_ff2, ln2_g, ln2_b, loss_target, m_ln_in_g, m_ln_in_b, m_w_in, m_b_forget, m_w_uq, m_g_cq, m_w_ukv, m_g_ckv, m_w_mem_kv, m_w_br, m_w_out, m_ln1_g, m_ln1_b, m_w_ff1, m_w_ff2, m_ln2_g, m_ln2_b, v_ln_in_g, v_ln_in_b, v_w_in, v_b_forget, v_w_uq, v_g_cq, v_w_ukv, v_g_ckv, v_w_mem_kv, v_w_br, v_w_out, v_ln1_g, v_ln1_b, v_w_ff1, v_w_ff2, v_ln2_g, v_ln2_b):
    w = dict(ln_in_g=ln_in_g, ln_in_b=ln_in_b, w_in=w_in, b_forget=b_forget, w_uq=w_uq, g_cq=g_cq, w_ukv=w_ukv,
             g_ckv=g_ckv, w_mem_kv=w_mem_kv, w_br=w_br, w_out=w_out, ln1_g=ln1_g, ln1_b=ln1_b, w_ff1=w_ff1,
             w_ff2=w_ff2, ln2_g=ln2_g, ln2_b=ln2_b)
    m = dict(ln_in_g=m_ln_in_g, ln_in_b=m_ln_in_b, w_in=m_w_in, b_forget=m_b_forget, w_uq=m_w_uq, g_cq=m_g_cq,
             w_ukv=m_w_ukv, g_ckv=m_g_ckv, w_mem_kv=m_w_mem_kv, w_br=m_w_br, w_out=m_w_out, ln1_g=m_ln1_g,
             ln1_b=m_ln1_b, w_ff1=m_w_ff1, w_ff2=m_w_ff2, ln2_g=m_ln2_g, ln2_b=m_ln2_b)
    v = dict(ln_in_g=v_ln_in_g, ln_in_b=v_ln_in_b, w_in=v_w_in, b_forget=v_b_forget, w_uq=v_w_uq, g_cq=v_g_cq,
             w_ukv=v_w_ukv, g_ckv=v_g_ckv, w_mem_kv=v_w_mem_kv, w_br=v_w_br, w_out=v_w_out, ln1_g=v_ln1_g,
             ln1_b=v_ln1_b, w_ff1=v_w_ff1, w_ff2=v_w_ff2, ln2_g=v_ln2_g, ln2_b=v_ln2_b)
    return _step(Cfg(), x, mem, positions, loss_target, w, m, v)
```

```python
import functools
from typing import NamedTuple

import jax
import jax.numpy as jnp
from jax import lax
from jax.experimental import pallas as pl
from jax.experimental.pallas import tpu as pltpu

F32 = jnp.float32
BF16 = jnp.bfloat16
MESH = pl.DeviceIdType.MESH

LANES = 128
SUBLANES = 8
VMEM_BYTES = 64 * 1024 * 1024
N_CHIPS = 4
N_DEV = 8

FOX_DH = 64
MLA_NOPE = 64
MLA_ROPE = 32
MLA_V = 64
MEM_DH = 128
ROPE_BASE = 10000.0
LN_EPS = 1e-5
RMS_EPS = 1e-6
NEG_INF = -1e30
ATTN_TILE = 512

ADAM_LR = 0.001
ADAM_B1 = 0.9
ADAM_B2 = 0.999
ADAM_EPS = 1e-08
ADAM_WD = 0.01
ADAM_STEP = 10


class Cfg(NamedTuple):
    d: int = 1024
    depth: int = 4
    seq: int = 2048
    chunk: int = 64
    n_mem: int = 256
    fox_h: int = 8
    mla_h: int = 8
    q_rank: int = 384
    kv_rank: int = 256
    mem_h: int = 4
    d_ff: int = 4096

    @property
    def width(self):
        return self.fox_h * FOX_DH

    @property
    def alpha(self):
        return (2 * self.depth) ** 0.25

    @property
    def small_w(self):
        return LANES + self.q_rank + self.kv_rank + LANES

    @property
    def in_splits(self):
        return (3 * self.width, self.fox_h, self.q_rank, self.kv_rank, MLA_ROPE, self.width, 3 * self.d)


class Exchange(NamedTuple):
    ins: tuple
    out_shapes: tuple
    plan: object
    n_copies: int
    aliases: dict = {}


def _peer(rel):
    x, y, c = lax.axis_index("x"), lax.axis_index("y"), lax.axis_index("c")
    if rel == "c":
        return (x, y, 1 - c)
    return ((1 - x) if rel in (1, 3) else x, (1 - y) if rel in (2, 3) else y, c)


def _exchange_copies(ex, in_refs, out_refs, send_sems, recv_sems):
    planned = ex.plan(in_refs, out_refs)
    assert len(planned) == ex.n_copies, len(planned)
    return [pltpu.make_async_remote_copy(src_ref=src, dst_ref=dst, send_sem=send_sems.at[i], recv_sem=recv_sems.at[i],
                                         device_id=_peer(rel), device_id_type=MESH)
            for i, (src, dst, rel) in enumerate(planned)]


def _pcall(body, rider=None, **kw):
    if rider is not None:
        n_in, n_out, grid = len(kw["in_specs"]), len(kw["out_specs"]), kw["grid"]
        n_rin, n_rout = len(rider.ins), len(rider.out_shapes)
        host = body

        def body(*refs):
            ins, rins = refs[:n_in], refs[n_in:n_in + n_rin]
            outs = refs[n_in + n_rin:n_in + n_rin + n_out]
            routs = refs[n_in + n_rin + n_out:n_in + n_rin + n_out + n_rout]
            scratch = refs[n_in + n_rin + n_out + n_rout:-2]
            copies = _exchange_copies(rider, rins, routs, refs[-2], refs[-1])
            first = functools.reduce(jnp.logical_and, [pl.program_id(a) == 0 for a in range(len(grid))])
            last = functools.reduce(jnp.logical_and, [pl.program_id(a) == n - 1 for a, n in enumerate(grid)])

            @pl.when(first)
            def _():
                for cp in copies:
                    cp.start()

            host(*ins, *outs, *scratch)

            @pl.when(last)
            def _():
                for cp in copies:
                    cp.wait()

        any_spec = pl.BlockSpec(memory_space=pl.ANY)
        sems = [pltpu.SemaphoreType.DMA((rider.n_copies,))] * 2
        kw = dict(
            kw,
            in_specs=list(kw["in_specs"]) + [any_spec] * n_rin,
            out_specs=list(kw["out_specs"]) + [any_spec] * n_rout,
            out_shape=list(kw["out_shape"]) + list(rider.out_shapes),
            scratch_shapes=list(kw.get("scratch_shapes", ())) + sems,
            input_output_aliases={**kw.get("input_output_aliases", {}),
                                  **{n_in + i: n_out + o for i, o in rider.aliases.items()}},
        )
    call = pl.pallas_call(body, **kw)
    return lambda *ops: call(*[pltpu.with_memory_space_constraint(o, pltpu.HBM) for o in ops])


class Hosts:
    def rider(self, host):
        return None

    def done(self, host, outs):
        pass

    def early_grads(self, grads):
        pass


def _merge_exchanges(exs):
    n_ins = [len(e.ins) for e in exs]
    n_outs = [len(e.out_shapes) for e in exs]

    def plan(in_refs, out_refs):
        copies, i, o = [], 0, 0
        for e, ni, no in zip(exs, n_ins, n_outs):
            copies += e.plan(in_refs[i:i + ni], out_refs[o:o + no])
            i, o = i + ni, o + no
        return copies

    aliases, i, o = {}, 0, 0
    for e, ni, no in zip(exs, n_ins, n_outs):
        aliases.update({i + a: o + b for a, b in e.aliases.items()})
        i, o = i + ni, o + no
    return Exchange(sum((e.ins for e in exs), ()), sum((e.out_shapes for e in exs), ()), plan,
                    sum(e.n_copies for e in exs), aliases)


class Together(Hosts):
    def __init__(self, members):
        self.members, self.active = list(members), []

    def rider(self, host):
        self.active = [(m, r) for m, r in ((m, m.rider(host)) for m in self.members) if r is not None]
        if not self.active:
            return None
        return _merge_exchanges([r for _, r in self.active])

    def done(self, host, outs):
        for m, r in self.active:
            m.done(host, outs[:len(r.out_shapes)])
            outs = outs[len(r.out_shapes):]


def _hosted(hosts, host, call, ops, n_results):
    rider = hosts.rider(host) if hosts is not None else None
    if rider is None:
        return call(None)(*ops)
    res = call(rider)(*ops, *rider.ins)
    hosts.done(host, list(res[n_results:]))
    return res[:n_results]


def _nbytes(shape, dtype):
    n = 1
    for s in shape:
        n *= s
    return n * jnp.dtype(dtype).itemsize


def _tile(dim, target):
    if dim <= target:
        return dim
    t = target - target % LANES
    while t >= LANES:
        if dim % t == 0:
            return t
        t -= LANES
    return dim


def _params(block_bytes, scratch_bytes=0):
    est = 2 * block_bytes + scratch_bytes + 24 * 1024 * 1024
    return pltpu.CompilerParams(vmem_limit_bytes=int(min(max(est, 32 * 1024 * 1024), VMEM_BYTES - 4 * 1024 * 1024)))


_DIMS = {"nn": (((1,), (0,)), ((), ())), "nt": (((1,), (1,)), ((), ())), "tn": (((0,), (0,)), ((), ()))}


MM_TILE = 1024
MM_BLOCK_BYTES = 16 * 1024 * 1024


def _mm_tiles(mode, pairs, out_dtypes, m, n, tm, tn):
    fixed_m, fixed_n = tm is not None, tn is not None
    tm, tn = _tile(m, tm or MM_TILE), _tile(n, tn or MM_TILE)

    def block_bytes(tm_, tn_):
        total = sum(_nbytes((tm_, tn_), dt) for dt in out_dtypes)
        for a, b in pairs:
            k = a.shape[0] if mode == "tn" else a.shape[1]
            total += _nbytes((k, tm_), a.dtype) + _nbytes((k, tn_), b.dtype)
        return total

    while block_bytes(tm, tn) > MM_BLOCK_BYTES:
        if not fixed_m and tm >= tn and tm > 2 * LANES:
            tm = _tile(m, tm // 2)
        elif not fixed_n and tn > 2 * LANES:
            tn = _tile(n, tn // 2)
        elif not fixed_m and tm > 2 * LANES:
            tm = _tile(m, tm // 2)
        else:
            break
    return tm, tn


def _mm(name, mode, pairs, out_dtypes, tm=None, tn=None, epi=None, row_extras=(), bc_extras=(), hosts=None, host=None):
    a0, b0 = pairs[0]
    m = a0.shape[1] if mode == "tn" else a0.shape[0]
    n = b0.shape[0] if mode == "nt" else b0.shape[1]
    tm, tn = _mm_tiles(mode, pairs, out_dtypes, m, n, tm, tn)
    in_specs, ops, blk = [], [], 0
    for a, b in pairs:
        if mode == "tn":
            k = a.shape[0]
            sa, sha = pl.BlockSpec((k, tm), lambda i, j: (0, i)), (k, tm)
        else:
            k = a.shape[1]
            sa, sha = pl.BlockSpec((tm, k), lambda i, j: (i, 0)), (tm, k)
        if mode == "nt":
            sb, shb = pl.BlockSpec((tn, k), lambda i, j: (j, 0)), (tn, k)
        else:
            sb, shb = pl.BlockSpec((k, tn), lambda i, j: (0, j)), (k, tn)
        in_specs += [sa, sb]
        ops += [a, b]
        blk += _nbytes(sha, a.dtype) + _nbytes(shb, b.dtype)
    for e in row_extras:
        w = e.shape[1]
        if w == n:
            in_specs.append(pl.BlockSpec((tm, tn), lambda i, j: (i, j)))
            blk += _nbytes((tm, tn), e.dtype)
        else:
            in_specs.append(pl.BlockSpec((tm, w), lambda i, j: (i, 0)))
            blk += _nbytes((tm, w), e.dtype)
        ops.append(e)
    for e in bc_extras:
        r, w = e.shape
        if w == n:
            in_specs.append(pl.BlockSpec((r, tn), lambda i, j: (0, j)))
        else:
            in_specs.append(pl.BlockSpec((r, w), lambda i, j: (0, 0)))
        blk += _nbytes((r, w), e.dtype)
        ops.append(e)
    npairs, nrow, nbc, nout = len(pairs), len(row_extras), len(bc_extras), len(out_dtypes)
    dims = _DIMS[mode]

    def body(*refs):
        acc = None
        for p in range(npairs):
            a = refs[2 * p][...].astype(BF16)
            b = refs[2 * p + 1][...].astype(BF16)
            d = lax.dot_general(a, b, dims, preferred_element_type=F32)
            acc = d if acc is None else acc + d
        ex = [r[...] for r in refs[2 * npairs:2 * npairs + nrow + nbc]]
        outs = (acc,) if epi is None else epi(acc, *ex)
        for o_ref, o in zip(refs[2 * npairs + nrow + nbc:], outs):
            o_ref[...] = o.astype(o_ref.dtype)

    blk += sum(_nbytes((tm, tn), dt) for dt in out_dtypes) + 2 * _nbytes((tm, tn), F32)
    call = lambda rider: _pcall(
        body,
        rider=rider,
        name=name,
        grid=(m // tm, n // tn),
        in_specs=in_specs,
        out_specs=[pl.BlockSpec((tm, tn), lambda i, j: (i, j)) for _ in range(nout)],
        out_shape=[jax.ShapeDtypeStruct((m, n), dt) for dt in out_dtypes],
        compiler_params=_params(blk),
    )
    res = _hosted(hosts, host, call, ops, nout)
    return res[0] if nout == 1 else res


def _rowwise(name, fn, row_ins, bc_ins, outs, accs=(), tm=256):
    rows = row_ins[0].shape[0]
    tm = min(tm, rows)
    assert rows % tm == 0
    nrow, nbc, nout, nacc = len(row_ins), len(bc_ins), len(outs), len(accs)
    in_specs = [pl.BlockSpec((tm, a.shape[1]), lambda i: (i, 0)) for a in row_ins]
    in_specs += [pl.BlockSpec(a.shape, lambda i: (0, 0)) for a in bc_ins]
    out_specs = [pl.BlockSpec((tm, w), lambda i: (i, 0)) for w, _ in outs]
    out_specs += [pl.BlockSpec(s, lambda i: (0, 0)) for s in accs]
    out_shape = [jax.ShapeDtypeStruct((rows, w), dt) for w, dt in outs]
    out_shape += [jax.ShapeDtypeStruct(s, F32) for s in accs]

    def body(*refs):
        vals = fn(*[r[...] for r in refs[:nrow + nbc]])
        o_refs = refs[nrow + nbc:]
        for r, v in zip(o_refs[:nout], vals[:nout]):
            r[...] = v.astype(r.dtype)
        if nacc:
            @pl.when(pl.program_id(0) == 0)
            def _():
                for r in o_refs[nout:]:
                    r[...] = jnp.zeros(r.shape, F32)

            for r, v in zip(o_refs[nout:], vals[nout:]):
                r[...] += v

    blk = sum(_nbytes((tm, a.shape[1]), a.dtype) for a in row_ins) + sum(_nbytes(a.shape, a.dtype) for a in bc_ins)
    blk += sum(_nbytes((tm, w), dt) for w, dt in outs) + sum(_nbytes(s, F32) for s in accs)
    res = _pcall(
        body,
        name=name,
        grid=(rows // tm,),
        in_specs=in_specs,
        out_specs=out_specs,
        out_shape=out_shape,
        compiler_params=_params(2 * blk),
    )(*row_ins, *bc_ins)
    return res


def _ln(z, g, b):
    mu = jnp.mean(z, axis=-1, keepdims=True)
    zc = z - mu
    var = jnp.mean(zc * zc, axis=-1, keepdims=True)
    return zc * lax.rsqrt(var + LN_EPS) * g + b


def _rms(x, g):
    return x * lax.rsqrt(jnp.mean(x * x, axis=-1, keepdims=True) + RMS_EPS) * g


def _colsum(v):
    return jnp.sum(v, axis=0, keepdims=True)


def _rope_swap(x):
    w = x.shape[1]
    lane = lax.broadcasted_iota(jnp.int32, (1, w), 1) % LANES
    from_left = pltpu.roll(x, 16, 1)
    from_right = pltpu.roll(x, w - 16, 1)
    lo = (lane >= MLA_NOPE) & (lane < MLA_NOPE + 16)
    hi = (lane >= MLA_NOPE + 16) & (lane < MLA_NOPE + 32)
    return jnp.where(hi, from_left, jnp.where(lo, from_right, 0.0))


def _rope(x, cos_t, sin_t):
    nh = x.shape[1] // LANES
    ct, st = jnp.tile(cos_t, (1, nh)), jnp.tile(sin_t, (1, nh))
    return x * ct + _rope_swap(x) * st


def _rope_t(dy, cos_t, sin_t):
    nh = dy.shape[1] // LANES
    ct, st = jnp.tile(cos_t, (1, nh)), jnp.tile(sin_t, (1, nh))
    return dy * ct + _rope_swap(dy * st)


def _block_cumsum(v, carry, reverse):
    tb = v.shape[0]
    r = lax.broadcasted_iota(jnp.int32, (tb, tb), 0)
    c = lax.broadcasted_iota(jnp.int32, (tb, tb), 1)
    tri = jnp.where((c >= r) if reverse else (c <= r), 1.0, 0.0).astype(BF16)
    hi = v.astype(BF16)
    r1 = v - hi.astype(F32)
    mid = r1.astype(BF16)
    lo = (r1 - mid.astype(F32)).astype(BF16)
    out = carry + sum(jnp.dot(tri, p, preferred_element_type=F32) for p in (hi, mid, lo))
    return out, (out[0:1, :] if reverse else out[tb - 1:tb, :])


def _forget_cumsum(name, logf, batch, seq, heads, tb):
    nb = seq // tb

    def body(x_ref, keys_ref, rows_ref, carry):
        @pl.when(pl.program_id(1) == 0)
        def _():
            carry[...] = jnp.zeros(carry.shape, F32)

        out, carry[...] = _block_cumsum(x_ref[...], carry[...], False)
        keys_ref[...] = jnp.concatenate([jnp.broadcast_to(out[:, h:h + 1], (tb, LANES)) for h in range(heads)], axis=1)
        out_t = out.T
        for g in range(heads // 2):
            rows_ref[g] = out_t[2 * g:2 * g + 2, :]

    return _pcall(
        body,
        name=name,
        grid=(batch, nb),
        in_specs=[pl.BlockSpec((tb, LANES), lambda b, i: (b * nb + i, 0))],
        out_specs=[pl.BlockSpec((tb, heads * LANES), lambda b, i: (b * nb + i, 0)),
                   pl.BlockSpec((None, None, heads // 2, 2, tb), lambda b, i: (b, i, 0, 0, 0))],
        out_shape=[jax.ShapeDtypeStruct((batch * seq, heads * LANES), F32),
                   jax.ShapeDtypeStruct((batch, nb, heads // 2, 2, tb), F32)],
        scratch_shapes=[pltpu.VMEM((1, LANES), F32)],
        compiler_params=_params(4 * tb * (heads + 2) * LANES * 4),
    )(logf)


def _forget_cumsum_bwd(name, dcf, batch, seq, heads, tb):
    nb = seq // tb

    def body(x_ref, o_ref, carry):
        @pl.when(pl.program_id(1) == 0)
        def _():
            carry[...] = jnp.zeros(carry.shape, F32)

        lane = lax.broadcasted_iota(jnp.int32, (1, LANES), 1)
        v = jnp.zeros((tb, LANES), F32)
        for g in range(heads // 2):
            blk = x_ref[:, g * LANES:(g + 1) * LANES]
            moved = pltpu.roll(blk, 2 * g, 1) if g else blk
            v = v + jnp.where((lane >= 2 * g) & (lane < 2 * g + 2), moved, 0.0)
        o_ref[...], carry[...] = _block_cumsum(v, carry[...], True)

    return _pcall(
        body,
        name=name,
        grid=(batch, nb),
        in_specs=[pl.BlockSpec((tb, (heads // 2) * LANES), lambda b, i: (b * nb + nb - 1 - i, 0))],
        out_specs=pl.BlockSpec((tb, LANES), lambda b, i: (b * nb + nb - 1 - i, 0)),
        out_shape=jax.ShapeDtypeStruct((batch * seq, LANES), F32),
        scratch_shapes=[pltpu.VMEM((1, LANES), F32)],
        compiler_params=_params(4 * tb * (heads // 2 + 1) * LANES * 4),
    )(dcf)


class Attn(NamedTuple):
    batch: int
    sq: int
    sk: int
    groups: int
    hq: int
    hv: int
    mode: str
    scale: float
    chunk: int
    tq: int
    tk: int

    @property
    def hg(self):
        return self.hv

    @property
    def qw(self):
        return LANES * self.hg // self.hq

    @property
    def dv(self):
        return LANES // self.hv


def _head_lanes(j, dv):
    lane = lax.broadcasted_iota(jnp.int32, (1, LANES), 1)
    return (lane >= j * dv) & (lane < (j + 1) * dv)


def _head_q(sp, j, q_blk):
    if sp.hq == 2:
        return jnp.where(_head_lanes(j, FOX_DH), q_blk, jnp.zeros_like(q_blk))
    return q_blk[:, LANES * j:LANES * (j + 1)]


def _head_rows(sp, j):
    return slice(j * sp.dv, (j + 1) * sp.dv) if sp.hg == 2 else slice(None)


def _scores_t(sp, j, k_c, q_j, cfq_rows, cfk_rep, k0, q0, masked):
    tk, tq = k_c.shape[0], q_j.shape[0]
    k_j = k_c if sp.hq == 2 else k_c[:, LANES * j:LANES * (j + 1)]
    st = lax.dot_general(k_j, q_j, _DIMS["nt"], preferred_element_type=F32) * sp.scale
    if sp.mode == "fox":
        st = st + (cfq_rows[j:j + 1, :] - jnp.tile(cfk_rep[:, LANES * j:LANES * (j + 1)], (1, tq // LANES)))
    if masked:
        kidx = k0 + lax.broadcasted_iota(jnp.int32, (tk, tq), 0)
        qidx = q0 + lax.broadcasted_iota(jnp.int32, (tk, tq), 1)
        if sp.mode == "chunk":
            shift = sp.chunk.bit_length() - 1
            kidx, qidx = jnp.right_shift(kidx, shift), jnp.right_shift(qidx, shift)
        st = jnp.where(kidx <= qidx, st, NEG_INF)
    return st


def _attn_fwd(name, sp, q, k, v, cfq=None, cfk=None, hosts=None, host=None):
    (qa, qo), (ka, ko), (va, vo) = q, k, v
    tq, tk, hg, qw = sp.tq, sp.tk, sp.hg, sp.qw
    nqb, nkc = sp.sq // tq, sp.sk // tk
    fox, causal = sp.mode == "fox", sp.mode != "none"
    assert sp.sq % tq == 0 and sp.sk % tk == 0 and (not causal or (tq == tk and sp.sq == sp.sk))

    def body(*refs):
        if fox:
            q_ref, k_ref, v_ref, cfq_ref, cfk_ref, o_ref, lse_ref, acc_scr = refs
        else:
            q_ref, k_ref, v_ref, o_ref, lse_ref, acc_scr = refs
        i = pl.program_id(2)
        q0 = i * tq
        q_blk = q_ref[...]
        qs = [_head_q(sp, j, q_blk) for j in range(hg)]
        acc_scr[...] = jnp.zeros(acc_scr.shape, F32)

        def chunk(kc, carry, masked):
            ms, ls = carry
            k0 = pl.multiple_of(kc * tk, tk)
            k_c = k_ref[pl.ds(k0, tk), :]
            v_c = v_ref[pl.ds(k0, tk), :]
            new_m, new_l = [], []
            for j in range(hg):
                st = _scores_t(sp, j, k_c, qs[j], cfq_ref[...] if fox else None,
                               cfk_ref[pl.ds(k0, tk), :] if fox else None, k0, q0, masked)
                m_new = jnp.maximum(ms[j], jnp.max(st, axis=0, keepdims=True))
                alpha = jnp.exp(ms[j] - m_new)
                pt = jnp.exp(st - m_new)
                new_m.append(m_new)
                new_l.append(alpha * ls[j] + jnp.sum(pt, axis=0, keepdims=True))
                pv = lax.dot_general(v_c, pt.astype(BF16), _DIMS["tn"], preferred_element_type=F32)
                r = _head_rows(sp, j)
                acc_scr[r, :] = acc_scr[r, :] * alpha + pv[r, :]
            return tuple(new_m), tuple(new_l)

        carry = (tuple(jnp.full((1, tq), NEG_INF, F32) for _ in range(hg)),
                 tuple(jnp.zeros((1, tq), F32) for _ in range(hg)))
        if causal:
            carry = lax.fori_loop(0, i, functools.partial(chunk, masked=False), carry)
            ms, ls = chunk(i, carry, True)
        else:
            ms, ls = lax.fori_loop(0, nkc, functools.partial(chunk, masked=False), carry)
        for j in range(hg):
            r = _head_rows(sp, j)
            acc_scr[r, :] = acc_scr[r, :] / ls[j]
            lse_ref[j:j + 1, :] = ms[j] + jnp.log(ls[j])
        o_ref[...] = acc_scr[...].T

    in_specs = [
        pl.BlockSpec((tq, qw), lambda b, g, i: (b * nqb + i, qo + g)),
        pl.BlockSpec((sp.sk, qw), lambda b, g, i: (b, ko + g)),
        pl.BlockSpec((sp.sk, LANES), lambda b, g, i: (b, vo + g)),
    ]
    ops = [qa, ka, va]
    stat_blk = pl.BlockSpec((None, None, None, hg, tq), lambda b, g, i: (b, g, i, 0, 0))
    if fox:
        in_specs += [pl.BlockSpec((None, None, None, hg, tq), lambda b, g, i: (b, i, g, 0, 0)),
                     pl.BlockSpec((sp.sk, hg * LANES), lambda b, g, i: (b, g))]
        ops += [cfq, cfk]
    blk = _nbytes((tq, qw), BF16) + _nbytes((sp.sk, qw + LANES), BF16) + 2 * _nbytes((tq, LANES), F32)
    blk += _nbytes((sp.sk, hg * LANES), F32) + 6 * _nbytes((tk, tq), F32)
    call = lambda rider: _pcall(
        body,
        rider=rider,
        name=name,
        grid=(sp.batch, sp.groups, nqb),
        in_specs=in_specs,
        out_specs=[pl.BlockSpec((tq, LANES), lambda b, g, i: (b * nqb + i, g)), stat_blk],
        out_shape=[
            jax.ShapeDtypeStruct((sp.batch * sp.sq, sp.groups * LANES), F32),
            jax.ShapeDtypeStruct((sp.batch, sp.groups, nqb, hg, tq), F32),
        ],
        scratch_shapes=[pltpu.VMEM((LANES, tq), F32)],
        compiler_params=_params(blk, tq * LANES * 4),
    )
    return _hosted(hosts, host, call, ops, 2)


def _attn_bwd(name, sp, q, k, v, o, lse, do, cfq=None, cfk=None, hosts=None, host=None):
    (qa, qo), (ka, ko), (va, vo) = q, k, v
    tq, tk, hg, qw, dv = sp.tq, sp.tk, sp.hg, sp.qw, sp.dv
    nqb, nkb = sp.sq // tq, sp.sk // tk
    fox, causal = sp.mode == "fox", sp.mode != "none"
    assert sp.sq % tq == 0 and sp.sk % tk == 0 and (not causal or (tq == tk and sp.sq == sp.sk))

    def body(*refs):
        if fox:
            (q_ref, k_ref, v_ref, lse_ref, do_ref, cfq_ref, cfk_ref, kall_ref, vall_ref, cfkall_ref,
             dq_ref, dk_ref, dv_ref, dcf_ref, delta_scr, dk_scr, dv_scr, dqt_scr, dcf_scr) = refs
        else:
            (q_ref, k_ref, v_ref, o_ref, lse_ref, do_ref,
             dq_ref, dk_ref, dv_ref, delta_scr, dk_scr, dv_scr, dqt_scr) = refs
        kb = pl.program_id(2)
        k0 = kb * tk
        heads = [_head_lanes(j, dv) for j in range(hg)]

        def head_do(j, do_c):
            return jnp.where(heads[j], do_c, jnp.zeros_like(do_c)) if hg == 2 else do_c

        def probs_t(j, k_c, v_c, q_c, do_c, i, cf_keys, c0, masked):
            st = _scores_t(sp, j, k_c, _head_q(sp, j, q_c), cfq_ref[i] if fox else None, cf_keys, c0, i * tq, masked)
            pt = jnp.exp(st - lse_ref[i][j:j + 1, :])
            dpt = lax.dot_general(v_c, head_do(j, do_c), _DIMS["nt"], preferred_element_type=F32)
            return pt, dpt

        @pl.when(kb == 0)
        def _():
            dqt_scr[...] = jnp.zeros(dqt_scr.shape, F32)

            def fill(i, carry):
                r0 = pl.multiple_of(i * tq, tq)
                do_c = do_ref[pl.ds(r0, tq), :]
                if fox:
                    q_c = q_ref[pl.ds(r0, tq), :]

                    def keys(kc, acc, masked):
                        c0 = pl.multiple_of(kc * tk, tk)
                        out = []
                        for j in range(hg):
                            pt, dpt = probs_t(j, kall_ref[pl.ds(c0, tk), :], vall_ref[pl.ds(c0, tk), :], q_c, do_c, i,
                                              cfkall_ref[pl.ds(c0, tk), :], c0, masked)
                            out.append(acc[j] + jnp.sum(pt * dpt, axis=0, keepdims=True))
                        return tuple(out)

                    d = lax.fori_loop(0, i, functools.partial(keys, masked=False),
                                      tuple(jnp.zeros((1, tq), F32) for _ in range(hg)))
                    d = keys(i, d, True)
                    for j in range(hg):
                        delta_scr[i, j:j + 1, :] = d[j]
                else:
                    prod_t = (do_c.astype(F32) * o_ref[pl.ds(r0, tq), :]).T
                    for j in range(hg):
                        delta_scr[i, j:j + 1, :] = jnp.sum(prod_t[_head_rows(sp, j), :], axis=0, keepdims=True)
                return carry

            lax.fori_loop(0, nqb, fill, 0)

        k_blk = k_ref[...]
        v_blk = v_ref[...]
        k_t = k_blk.astype(F32).T.astype(BF16)
        dk_scr[...] = jnp.zeros(dk_scr.shape, F32)
        dv_scr[...] = jnp.zeros(dv_scr.shape, F32)
        if fox:
            dcf_scr[...] = jnp.zeros(dcf_scr.shape, F32)

        def qblock(i, carry, masked):
            r0 = pl.multiple_of(i * tq, tq)
            q_c = q_ref[pl.ds(r0, tq), :]
            do_c = do_ref[pl.ds(r0, tq), :]
            for j in range(hg):
                pt, dpt = probs_t(j, k_blk, v_blk, q_c, do_c, i, cfk_ref[...] if fox else None, k0, masked)
                dst = pt * (dpt - delta_scr[i][j:j + 1, :])
                if fox:
                    part = dst[:, :LANES]
                    for t in range(1, tq // LANES):
                        part = part + dst[:, t * LANES:(t + 1) * LANES]
                    dcf_scr[j] += part
                ds_b = (dst * sp.scale).astype(BF16)
                dv_scr[j] += jnp.dot(pt.astype(BF16), do_c, preferred_element_type=F32)
                dk_scr[j] += jnp.dot(ds_b, q_c if sp.hq == 2 else _head_q(sp, j, q_c), preferred_element_type=F32)
                if sp.hq == 2:
                    r = pl.ds(j * FOX_DH, FOX_DH)
                    dqt_scr[i, r, :] += jnp.dot(k_t[j * FOX_DH:(j + 1) * FOX_DH, :], ds_b, preferred_element_type=F32)
                else:
                    r = pl.ds(j * LANES, LANES)
                    dqt_scr[i, r, :] += jnp.dot(k_t[j * LANES:(j + 1) * LANES, :], ds_b, preferred_element_type=F32)
            return carry

        if causal:
            qblock(kb, 0, True)
            lax.fori_loop(kb + 1, nqb, functools.partial(qblock, masked=False), 0)
        else:
            lax.fori_loop(0, nqb, functools.partial(qblock, masked=False), 0)

        @pl.when(kb == nkb - 1)
        def _():
            def untranspose(i, carry):
                dq_ref[pl.ds(pl.multiple_of(i * tq, tq), tq), :] = dqt_scr[i].T
                return carry

            lax.fori_loop(0, nqb, untranspose, 0)

        if hg == 2:
            dv_ref[...] = jnp.where(heads[0], dv_scr[0], dv_scr[1]).astype(dv_ref.dtype)
        else:
            dv_ref[...] = dv_scr[0].astype(dv_ref.dtype)
        if sp.hq == 2:
            dk_ref[...] = jnp.where(_head_lanes(0, FOX_DH), dk_scr[0], dk_scr[1]).astype(dk_ref.dtype)
        elif hg == 2:
            dk_ref[...] = jnp.concatenate([dk_scr[0], dk_scr[1]], axis=1).astype(dk_ref.dtype)
        else:
            dk_ref[...] = dk_scr[0].astype(dk_ref.dtype)
        if fox:
            lane = lax.broadcasted_iota(jnp.int32, (1, LANES), 1)
            sums = [jnp.sum(dcf_scr[j], axis=1, keepdims=True) for j in range(hg)]
            dcf_ref[...] = jnp.where(lane == 0, -sums[0], jnp.where(lane == 1, -sums[1], 0.0))

    seq_lanes = lambda b, g, kb: (b, g)
    key_blk = lambda b, g, kb: (b * nkb + kb, g)
    stats = pl.BlockSpec((None, None, nqb, hg, tq), lambda b, g, kb: (b, g, 0, 0, 0))
    in_specs = [
        pl.BlockSpec((sp.sq, qw), lambda b, g, kb: (b, qo + g)),
        pl.BlockSpec((tk, qw), lambda b, g, kb: (b * nkb + kb, ko + g)),
        pl.BlockSpec((tk, LANES), lambda b, g, kb: (b * nkb + kb, vo + g)),
    ]
    ops = [qa, ka, va]
    if not fox:
        in_specs.append(pl.BlockSpec((sp.sq, LANES), seq_lanes))
        ops.append(o)
    in_specs += [stats, pl.BlockSpec((sp.sq, LANES), seq_lanes)]
    ops += [lse, do]
    out_specs = [pl.BlockSpec((sp.sq, qw), seq_lanes), pl.BlockSpec((tk, qw), key_blk), pl.BlockSpec((tk, LANES), key_blk)]
    out_shape = [
        jax.ShapeDtypeStruct((sp.batch * sp.sq, sp.groups * qw), F32),
        jax.ShapeDtypeStruct((sp.batch * sp.sk, sp.groups * qw), BF16),
        jax.ShapeDtypeStruct((sp.batch * sp.sk, sp.groups * LANES), BF16),
    ]
    scratch = [pltpu.VMEM((nqb, hg, tq), F32), pltpu.VMEM((hg, tk, LANES), F32), pltpu.VMEM((hg, tk, LANES), F32),
               pltpu.VMEM((nqb, qw, tq), F32)]
    if fox:
        in_specs += [
            pl.BlockSpec((None, nqb, None, hg, tq), lambda b, g, kb: (b, 0, g, 0, 0)),
            pl.BlockSpec((tk, hg * LANES), key_blk),
            pl.BlockSpec((sp.sk, qw), lambda b, g, kb: (b, ko + g)),
            pl.BlockSpec((sp.sk, LANES), lambda b, g, kb: (b, vo + g)),
            pl.BlockSpec((sp.sk, hg * LANES), seq_lanes),
        ]
        ops += [cfq, cfk, ka, va, cfk]
        out_specs.append(pl.BlockSpec((tk, LANES), key_blk))
        out_shape.append(jax.ShapeDtypeStruct((sp.batch * sp.sk, sp.groups * LANES), F32))
        scratch.append(pltpu.VMEM((hg, tk, LANES), F32))
    blk = _nbytes((sp.sq, qw), BF16) + _nbytes((sp.sq, LANES), BF16) + 2 * _nbytes((sp.sq, LANES), F32)
    blk += _nbytes((sp.sq, qw), F32) + 4 * _nbytes((tk, qw), BF16) + 8 * _nbytes((tq, tk), F32)
    blk += (_nbytes((sp.sk, qw + LANES), BF16) + _nbytes((sp.sk, hg * LANES), F32)) if fox else 0
    call = lambda rider: _pcall(
        body,
        rider=rider,
        name=name,
        grid=(sp.batch, sp.groups, nkb),
        in_specs=in_specs,
        out_specs=out_specs,
        out_shape=out_shape,
        scratch_shapes=scratch,
        compiler_params=_params(blk, _nbytes((sp.sq, LANES), F32) + 4 * _nbytes((tk, LANES), F32)),
    )
    return _hosted(hosts, host, call, ops, len(out_shape))


def _slabwise(name, fn, ins, out_dtypes, rows_per_step=512):
    ins = [a if isinstance(a, tuple) else (a, None) for a in ins]
    n = max(1 if fixed is not None else a.shape[0] for a, fixed in ins)
    rows, cols = ins[0][0].shape[1:]
    tr = min(rows_per_step, rows)
    while rows % tr:
        tr //= 2
    assert tr % 16 == 0 or tr == rows, (name, rows, tr)

    def spec(a, fixed):
        if fixed is not None or a.shape[0] == 1:
            return pl.BlockSpec((None, tr, cols), lambda s, i: (fixed or 0, i, 0))
        return pl.BlockSpec((None, tr, cols), lambda s, i: (s, i, 0))

    def body(*refs):
        vals = fn(*[r[...] for r in refs[:len(ins)]])
        for r, v in zip(refs[len(ins):], vals):
            r[...] = v.astype(r.dtype)

    blk = (len(ins) + len(out_dtypes)) * _nbytes((tr, cols + LANES), F32)
    res = _pcall(
        body,
        name=name,
        grid=(n, rows // tr),
        in_specs=[spec(a, fixed) for a, fixed in ins],
        out_specs=[pl.BlockSpec((None, tr, cols), lambda s, i: (s, i, 0)) for _ in out_dtypes],
        out_shape=[jax.ShapeDtypeStruct((n, rows, cols), dt) for dt in out_dtypes],
        compiler_params=_params(2 * blk),
    )(*[a for a, _ in ins])
    return res


def _adamw_math(w, g, m, v):
    m = ADAM_B1 * m + (1.0 - ADAM_B1) * g
    v = ADAM_B2 * v + (1.0 - ADAM_B2) * jnp.square(g)
    m_hat = m / (1.0 - ADAM_B1 ** ADAM_STEP)
    v_hat = v / (1.0 - ADAM_B2 ** ADAM_STEP)
    delta = -ADAM_LR * (m_hat / (jnp.sqrt(v_hat) + ADAM_EPS) + ADAM_WD * w)
    return delta, m, v


def _run_exchange(name, ex):
    n_in, n_out = len(ex.ins), len(ex.out_shapes)

    def body(*refs):
        copies = _exchange_copies(ex, refs[:n_in], refs[n_in:n_in + n_out], refs[-2], refs[-1])
        for cp in copies:
            cp.start()
        for cp in copies:
            cp.wait()

    any_spec = pl.BlockSpec(memory_space=pl.ANY)
    return _pcall(
        body,
        name=name,
        in_specs=[any_spec] * n_in,
        out_specs=[any_spec] * n_out,
        out_shape=list(ex.out_shapes),
        scratch_shapes=[pltpu.SemaphoreType.DMA((ex.n_copies,))] * 2,
        input_output_aliases=dict(ex.aliases),
    )(*ex.ins)


def _chip(rel=0):
    x, y = lax.axis_index("x"), lax.axis_index("y")
    return 2 * ((1 - x) if rel & 1 else x) + ((1 - y) if rel & 2 else y)


def _gather_ici(shards):
    def plan(in_refs, out_refs):
        c = lax.axis_index("c")
        return [(s.at[c], g.at[_chip(), c], rel) for s, g in zip(in_refs, out_refs) for rel in (1, 2, 3)]

    shapes = tuple(jax.ShapeDtypeStruct((N_CHIPS,) + s.shape, s.dtype) for s in shards)
    return Exchange(tuple(shards), shapes, plan, 3 * len(shards))


def _gather_d2d(got):
    def plan(in_refs, out_refs):
        c = lax.axis_index("c")
        return [(g_in.at[_chip(rel), c], g_out.at[_chip(rel), c], "c")
                for g_in, g_out in zip(in_refs, out_refs) for rel in (1, 2, 3)]

    shapes = tuple(jax.ShapeDtypeStruct(g.shape, g.dtype) for g in got)
    return Exchange(tuple(got), shapes, plan, 3 * len(got), {i: i for i in range(len(got))})


class GatherSet(Hosts):
    def __init__(self, shards, hosts):
        self.shards, self.hosts, self.got = shards, hosts, None

    def rider(self, host):
        if host == self.hosts[0]:
            return _gather_ici(self.shards)
        if host == self.hosts[1]:
            return _gather_d2d(self.got)
        return None

    def done(self, host, outs):
        self.got = outs


def _gather_now(shards):
    got = _run_exchange("gather_weights_ici", _gather_ici(shards))
    return _run_exchange("gather_weights_d2d", _gather_d2d(got))


def _pair_sum(name, g, recv, rows_per_step=512):
    _, _, rows, cols = g.shape
    tr = min(rows_per_step, rows)
    while rows % tr:
        tr //= 2

    def body(g_ref, r_ref, p_ref, own_ref):
        mine = jnp.where(lax.axis_index("c") == 0, g_ref[0], g_ref[1])
        p = mine.astype(F32) + r_ref[...].astype(F32)
        p_ref[...] = p.astype(p_ref.dtype)

        @pl.when(pl.program_id(1) == _chip())
        def _():
            own_ref[...] = p

    slab = pl.BlockSpec((None, tr, cols), lambda i, s: (s, i, 0))
    return _pcall(
        body,
        name=name,
        grid=(rows // tr, N_CHIPS),
        in_specs=[pl.BlockSpec((None, 2, tr, cols), lambda i, s: (s, 0, i, 0)), slab],
        out_specs=[slab, pl.BlockSpec((None, tr, cols), lambda i, s: (0, i, 0))],
        out_shape=[jax.ShapeDtypeStruct((N_CHIPS, rows, cols), BF16), jax.ShapeDtypeStruct((1, rows, cols), F32)],
        compiler_params=_params(2 * 6 * _nbytes((tr, cols + LANES), F32)),
    )(g, recv)


class ReduceLayer(Hosts):
    def __init__(self, tag, grads, hosts):
        self.tag, self.grads, self.pair, self.own, self.total, self.theirs = tag, grads, None, None, None, None
        self.stage_of = dict(zip(hosts, ("swap", "chips", "share")))

    def _swap_halves(self):
        def plan(in_refs, out_refs):
            c = lax.axis_index("c")
            return [(g.at[pl.ds(0, N_CHIPS), 1 - c], r, "c") for g, r in zip(in_refs, out_refs)]

        shapes = tuple(jax.ShapeDtypeStruct((N_CHIPS,) + g.shape[2:], g.dtype) for g in self.grads)
        return Exchange(tuple(self.grads), shapes, plan, len(self.grads))

    def _to_chips(self):
        def plan(in_refs, out_refs):
            return [(p.at[_chip(rel)], r.at[rel - 1], rel) for p, r in zip(in_refs, out_refs) for rel in (1, 2, 3)]

        shapes = tuple(jax.ShapeDtypeStruct((3,) + p.shape[1:], p.dtype) for p in self.pair)
        return Exchange(tuple(self.pair), shapes, plan, 3 * len(self.pair))

    def _share(self):
        def plan(in_refs, out_refs):
            return [(t, r, "c") for t, r in zip(in_refs, out_refs)]

        shapes = tuple(jax.ShapeDtypeStruct(t.shape, F32) for t in self.total)
        return Exchange(tuple(self.total), shapes, plan, len(self.total))

    def rider(self, host):
        stages = {"swap": self._swap_halves, "chips": self._to_chips, "share": self._share}
        return stages[self.stage_of[host]]() if host in self.stage_of else None

    def done(self, host, outs):
        self._after(self.stage_of[host], outs)

    def _after(self, stage, outs):
        if stage == "swap":
            sums = [_pair_sum(f"reduce_pair_sum_{self.tag}_{i}", g, r) for i, (g, r) in enumerate(zip(self.grads, outs))]
            self.pair, self.own = [s[0] for s in sums], [s[1] for s in sums]
        elif stage == "chips":
            self.total = [
                _slabwise(f"reduce_chip_sum_{self.tag}_{i}",
                          lambda a, b, c_, d: (a + b.astype(F32) + c_.astype(F32) + d.astype(F32),),
                          [own, (r, 0), (r, 1), (r, 2)], [F32])[0]
                for i, (own, r) in enumerate(zip(self.own, outs))]
        else:
            self.theirs = outs

    def run_now(self):
        self._after("swap", _run_exchange(f"reduce_pair_{self.tag}", self._swap_halves()))
        self._after("chips", _run_exchange(f"reduce_chips_{self.tag}", self._to_chips()))
        self._after("share", _run_exchange(f"reduce_share_{self.tag}", self._share()))

    def result(self):
        return list(zip(self.total, self.theirs))


def _adamw_layer(name, l, w, m, v, mine, theirs, prev, rows_per_step=256):
    _, _, rows, cols = w.shape
    tr = min(rows_per_step, rows)
    while rows % tr:
        tr //= 2

    def body(w_ref, m_ref, v_ref, mine_ref, theirs_ref, *rest):
        g_ref, d_ref, nm_ref, nv_ref = rest[-4:]
        g = jnp.where(pl.program_id(0) == lax.axis_index("c"), mine_ref[...], theirs_ref[...])
        d, nm, nv = _adamw_math(w_ref[...], g, m_ref[...], v_ref[...])
        g_ref[...], d_ref[...], nm_ref[...], nv_ref[...] = g, d, nm, nv

    half = pl.BlockSpec((None, None, tr, cols), lambda h, i: (l, h, i, 0))
    one = pl.BlockSpec((None, tr, cols), lambda h, i: (0, i, 0))
    kept = [] if prev is None else list(prev)
    return _pcall(
        body,
        name=name,
        grid=(2, rows // tr),
        in_specs=[half, half, half, one, one] + [pl.BlockSpec(memory_space=pl.ANY)] * len(kept),
        out_specs=[half] * 4,
        out_shape=[jax.ShapeDtypeStruct(w.shape, F32)] * 4,
        input_output_aliases={5 + i: i for i in range(len(kept))},
        compiler_params=_params(2 * 9 * _nbytes((tr, cols + LANES), F32)),
    )(w, m, v, mine, theirs, *kept)


def _allreduce_small(v):
    rows = v.shape[0]

    def body(v_ref, sum_ref, all_ref, send_sems, recv_sems, local_sem):
        x, y, c = lax.axis_index("x"), lax.axis_index("y"), lax.axis_index("c")
        sibling = (x, y, 1 - c)
        chips = [(1 - x, y), (x, 1 - y), (1 - x, 1 - y)]

        def slab(px, py, pc):
            return all_ref.at[pl.ds((4 * px + 2 * py + pc) * rows, rows), :]

        def copy(k, block, to, src=None):
            return pltpu.make_async_remote_copy(
                src_ref=slab(*block) if src is None else src, dst_ref=slab(*block), send_sem=send_sems.at[k],
                recv_sem=recv_sems.at[k], device_id=to, device_id_type=MESH)

        mine = pltpu.make_async_copy(v_ref, slab(x, y, c), local_sem)
        mine.start()
        first = [copy(0, (x, y, c), sibling, src=v_ref)]
        first += [copy(1 + j, (x, y, c), (*chip, c), src=v_ref) for j, chip in enumerate(chips)]
        for cp in first:
            cp.start()
        passed = [copy(4 + j, (*chip, c), sibling) for j, chip in enumerate(chips)]
        for j, chip in enumerate(chips):
            copy(1 + j, (*chip, c), (x, y, c)).wait_recv()
            passed[j].start()
        copy(0, (x, y, 1 - c), (x, y, c)).wait_recv()
        for j, chip in enumerate(chips):
            copy(4 + j, (*chip, 1 - c), (x, y, c)).wait_recv()
        for cp in first + passed:
            cp.wait_send()
        mine.wait()
        total = all_ref[pl.ds(0, rows), :]
        for d in range(1, N_DEV):
            total = total + all_ref[pl.ds(d * rows, rows), :]
        sum_ref[...] = total

    vm = pl.BlockSpec(memory_space=pltpu.VMEM)
    return _pcall(
        body,
        name="allreduce_small",
        in_specs=[vm],
        out_specs=vm,
        out_shape=jax.ShapeDtypeStruct((rows, LANES), F32),
        scratch_shapes=[pltpu.VMEM((N_DEV * rows, LANES), F32), pltpu.SemaphoreType.DMA((7,)),
                        pltpu.SemaphoreType.DMA((7,)), pltpu.SemaphoreType.DMA],
    )(v)


def _pad_cols(a, before, total):
    return jnp.pad(a, ((0, 0), (before, total - before - a.shape[1])))


def _layer_weights(cfg, w_in, w_uq, w_ukv):
    w = cfg.width
    qkv, f, cq, ckv, kr, qm, gates = jnp.split(w_in, list(_cumsum(cfg.in_splits))[:-1], axis=1)
    wa = jnp.concatenate([qkv, qm], axis=1)
    ws = jnp.concatenate([_pad_cols(f, 0, LANES), cq, ckv, _pad_cols(kr, MLA_NOPE, LANES)], axis=1)
    wq = jnp.pad(w_uq.reshape(cfg.q_rank, cfg.mla_h, MLA_NOPE + MLA_ROPE), ((0, 0), (0, 0), (0, LANES - MLA_NOPE - MLA_ROPE)))
    wq = wq.reshape(cfg.q_rank, cfg.mla_h * LANES)
    kv = w_ukv.reshape(cfg.kv_rank, cfg.mla_h, MLA_NOPE + MLA_V)
    wk = jnp.pad(kv[:, :, :MLA_NOPE], ((0, 0), (0, 0), (0, LANES - MLA_NOPE))).reshape(cfg.kv_rank, cfg.mla_h * LANES)
    wv = kv[:, :, MLA_NOPE:].reshape(cfg.kv_rank, cfg.mla_h * MLA_V)
    del w
    return wa, gates, ws, wq, wk, wv


def _cumsum(xs):
    out, t = [], 0
    for v in xs:
        t += v
        out.append(t)
    return out


def _layer_weight_grads(cfg, dwa, dwg, dws, dwq, dwk, dwv):
    w, qr, kvr = cfg.width, cfg.q_rank, cfg.kv_rank
    off_kr = LANES + qr + kvr + MLA_NOPE
    dw_in = jnp.concatenate([
        dwa[:, :3 * w], dws[:, :cfg.fox_h], dws[:, LANES:LANES + qr], dws[:, LANES + qr:LANES + qr + kvr],
        dws[:, off_kr:off_kr + MLA_ROPE], dwa[:, 3 * w:], dwg], axis=1)
    dw_uq = dwq.reshape(qr, cfg.mla_h, LANES)[:, :, :MLA_NOPE + MLA_ROPE].reshape(qr, cfg.mla_h * (MLA_NOPE + MLA_ROPE))
    dw_ukv = jnp.concatenate([dwk.reshape(kvr, cfg.mla_h, LANES)[:, :, :MLA_NOPE], dwv.reshape(kvr, cfg.mla_h, MLA_V)],
                             axis=2).reshape(kvr, cfg.mla_h * (MLA_NOPE + MLA_V))
    return dw_in, dw_uq, dw_ukv


def _attn_specs(cfg, batch):
    t = min(ATTN_TILE, cfg.seq)
    common = dict(batch=batch, sq=cfg.seq, chunk=cfg.chunk, tq=t)
    fox = Attn(sk=cfg.seq, groups=cfg.fox_h // 2, hq=2, hv=2, mode="fox", scale=FOX_DH ** -0.5, tk=t, **common)
    mla = Attn(sk=cfg.seq, groups=cfg.mla_h // 2, hq=1, hv=2, mode="chunk",
               scale=(MLA_NOPE + MLA_ROPE) ** -0.5, tk=t, **common)
    mem = Attn(sk=cfg.n_mem, groups=cfg.mem_h, hq=1, hv=1, mode="none", scale=MEM_DH ** -0.5, tk=cfg.n_mem, **common)
    return fox, mla, mem


def _small_core(cfg, ps, bf, gq, gkv):
    qr, kvr = cfg.q_rank, cfg.kv_rank
    z = ps[:, :LANES] + bf
    logf = jnp.minimum(z, 0.0) - jnp.log1p(jnp.exp(-jnp.abs(z)))
    nq = _rms(ps[:, LANES:LANES + qr], gq)
    nkv = _rms(ps[:, LANES + qr:LANES + qr + kvr], gkv)
    return logf, nq, nkv


def _layer_fwd(cfg, l, batch, h, hb, mem_b, rope_c, rope_s, hosts, lw, bf_pad, g_cq, g_ckv, ln1, ln2):
    (wa, wg, ws, wq, wk, wv, wmkv), later_weights = lw
    w, d = cfg.width, cfg.d
    fox, mla, mem = _attn_specs(cfg, batch)
    nw = w // LANES
    pa = _mm(f"proj_a_{l}", "nn", [(hb, wa)], [BF16])
    gl = _mm(f"proj_gates_{l}", "nn", [(hb, wg)], [F32])
    ps = _mm(f"proj_small_{l}", "nn", [(hb, ws)], [F32], tn=cfg.small_w)

    def small_fwd(ps_, c_, s_, bf_, gq_, gkv_):
        logf, nq, nkv = _small_core(cfg, ps_, bf_, gq_, gkv_)
        kpe = _rope(ps_[:, cfg.small_w - LANES:], c_, s_)
        return logf, nq, nkv, kpe

    logf, nq, nkv, kpe = _rowwise(
        f"small_fwd_{l}", small_fwd, [ps, rope_c, rope_s], [bf_pad, g_cq, g_ckv],
        [(LANES, F32), (cfg.q_rank, BF16), (cfg.kv_rank, BF16), (LANES, F32)])
    cfk, cfq = _forget_cumsum(f"cum_forget_{l}", logf, batch, cfg.seq, cfg.fox_h, fox.tq)

    qf = _mm(f"mla_q_{l}", "nn", [(nq, wq)], [BF16], tn=wq.shape[1],
             epi=lambda acc, c_, s_: (_rope(acc, c_, s_),), row_extras=[rope_c, rope_s])
    kf = _mm(f"mla_k_{l}", "nn", [(nkv, wk)], [BF16], tn=wk.shape[1],
             epi=lambda acc, kp: (acc + jnp.tile(kp, (1, cfg.mla_h)),), row_extras=[kpe])
    vb = _mm(f"mla_v_{l}", "nn", [(nkv, wv)], [BF16])
    mkv = _mm(f"mem_kv_{l}", "nn", [(mem_b, wmkv)], [BF16])

    o_a, lse_a = _attn_fwd(f"fox_fwd_{l}", fox, (pa, 0), (pa, nw), (pa, 2 * nw), cfq, cfk,
                           hosts=hosts, host="fox_fwd")
    o_b, lse_b = _attn_fwd(f"mla_fwd_{l}", mla, (qf, 0), (kf, 0), (vb, 0), hosts=hosts, host="mla_fwd")
    o_c, lse_c = _attn_fwd(f"mem_fwd_{l}", mem, (pa, 3 * nw), (mkv, 0), (mkv, nw), hosts=hosts, host="mem_fwd")
    wbr, wout, wff1, wff2 = later_weights()
    bps = [_mm(f"branch_{n}_{l}", "nn", [(o, wbr[n])], [F32]) for n, o in enumerate((o_a, o_b, o_c))]

    def merge(gl_, b0, b1, b2):
        g = jax.nn.sigmoid(gl_)
        return (g[:, :d] * b0 + g[:, d:2 * d] * b1 + g[:, 2 * d:] * b2,)

    (merged,) = _rowwise(f"merge_{l}", merge, [gl] + bps, [], [(d, BF16)])

    def post_ln(acc, res, g_, b_):
        z = cfg.alpha * res + acc
        y = _ln(z, g_, b_)
        return z, y, y

    z1, h1, h1b = _mm(f"out_ln1_{l}", "nn", [(merged, wout)], [F32, F32, BF16], tm=256, tn=d,
                      epi=post_ln, row_extras=[h], bc_extras=list(ln1))
    u, a = _mm(f"ff1_{l}", "nn", [(h1b, wff1)], [BF16, BF16],
               epi=lambda acc: (acc, jnp.square(jnp.maximum(acc, 0.0))))
    z2, h2, h2b = _mm(f"ff2_ln2_{l}", "nn", [(a, wff2)], [F32, F32, BF16], tm=256, tn=d,
                      epi=post_ln, row_extras=[h1], bc_extras=list(ln2))
    saved = dict(hb=hb, pa=pa, gl=gl, ps=ps, nq=nq, nkv=nkv, cfq=cfq, cfk=cfk, qf=qf, kf=kf, vb=vb, mkv=mkv,
                 o=(o_a, o_b, o_c), lse=(lse_a, lse_b, lse_c), bps=bps, merged=merged, z1=z1, h1b=h1b, u=u, a=a, z2=z2,
                 lw=(wa, wg, ws, wq, wk, wv, wmkv, wbr, wout, wff1, wff2))
    return h2, h2b, saved


def _ln_bwd(name, cfg, ga, gb, z, g, b):
    d = cfg.d

    def fn(*vals):
        if gb is None:
            ga_, z_, g_, b_ = vals
            dy = ga_
        else:
            ga_, gb_, z_, g_, b_ = vals
            dy = ga_ + cfg.alpha * gb_
        _, vjp = jax.vjp(_ln, z_, g_, b_)
        dz, dg, db = vjp(dy)
        return dz, dz, dg, db

    rows = [ga, z] if gb is None else [ga, gb, z]
    return _rowwise(name, fn, rows, [g, b], [(d, F32), (d, BF16)], accs=[(1, d), (1, d)])


def _layer_bwd(cfg, l, batch, ga, gb, sv, mem_b, rope_c, rope_s, hosts, lw, bf_pad, g_cq, g_ckv, ln1, ln2):
    wa, wg, ws, wq, wk, wv, wmkv, wbr, wout, wff1, wff2 = lw
    w, d = cfg.width, cfg.d
    fox, mla, mem = _attn_specs(cfg, batch)
    nw = w // LANES
    gdt = BF16

    dz2, dz2b, dg2, db2 = _ln_bwd(f"ln2_bwd_{l}", cfg, ga, gb, sv["z2"], *ln2)
    du = _mm(f"ff2_bwd_x_{l}", "nt", [(dz2b, wff2)], [BF16],
             epi=lambda acc, u_: (acc * (2.0 * jnp.maximum(u_.astype(F32), 0.0)),), row_extras=[sv["u"]],
             hosts=hosts, host="ff2_bwd_x")
    dwff2 = _mm(f"ff2_bwd_w_{l}", "tn", [(sv["a"], dz2b)], [gdt])
    dwff1 = _mm(f"ff1_bwd_w_{l}", "tn", [(sv["h1b"], du)], [gdt])
    dh1 = _mm(f"ff1_bwd_x_{l}", "nt", [(du, wff1)], [F32])
    dz1, dz1b, dg1, db1 = _ln_bwd(f"ln1_bwd_{l}", cfg, dh1, dz2, sv["z1"], *ln1)
    dmerged = _mm(f"out_bwd_x_{l}", "nt", [(dz1b, wout)], [F32])
    dwout = _mm(f"out_bwd_w_{l}", "tn", [(sv["merged"], dz1b)], [gdt])

    def merge_bwd(dm, gl_, b0, b1, b2):
        def f(gl__, b0_, b1_, b2_):
            g = jax.nn.sigmoid(gl__)
            return g[:, :d] * b0_ + g[:, d:2 * d] * b1_ + g[:, 2 * d:] * b2_

        _, vjp = jax.vjp(f, gl_, b0, b1, b2)
        return vjp(dm)

    dgl, db0, db1_, db2_ = _rowwise(f"merge_bwd_{l}", merge_bwd, [dmerged, sv["gl"]] + sv["bps"], [],
                                    [(3 * d, BF16), (d, BF16), (d, BF16), (d, BF16)])
    dbps = (db0, db1_, db2_)
    dos = [_mm(f"branch_bwd_x_{n}_{l}", "nt", [(dbps[n], wbr[n])], [BF16]) for n in range(3)]
    dwbr = jnp.stack([_mm(f"branch_bwd_w_{n}_{l}", "tn", [(sv["o"][n], dbps[n])], [gdt]) for n in range(3)])
    if hosts is not None:
        hosts.early_grads(dict(w_br=dwbr, w_out=dwout, w_ff1=dwff1, w_ff2=dwff2))

    pa = sv["pa"]
    dq_a, dk_a, dv_a, dcfk = _attn_bwd(f"fox_bwd_{l}", fox, (pa, 0), (pa, nw), (pa, 2 * nw), sv["o"][0], sv["lse"][0],
                                       dos[0], sv["cfq"], sv["cfk"], hosts=hosts, host="fox_bwd")
    dqf, dkf, dvb = _attn_bwd(f"mla_bwd_{l}", mla, (sv["qf"], 0), (sv["kf"], 0), (sv["vb"], 0), sv["o"][1],
                              sv["lse"][1], dos[1], hosts=hosts, host="mla_bwd")
    dqm, dmk, dmv = _attn_bwd(f"mem_bwd_{l}", mem, (pa, 3 * nw), (sv["mkv"], 0), (sv["mkv"], nw), sv["o"][2],
                              sv["lse"][2], dos[2], hosts=hosts, host="mem_bwd")
    dwmkv = _mm(f"mem_kv_bwd_w_{l}", "tn", [(mem_b, jnp.concatenate([dmk, dmv], axis=1))], [gdt])

    (dq_raw,) = _rowwise(f"mla_q_rope_bwd_{l}", lambda dy, c_, s_: (_rope_t(dy, c_, s_),), [dqf, rope_c, rope_s], [],
                         [(wq.shape[1], BF16)])
    dwq = _mm(f"mla_q_bwd_w_{l}", "tn", [(sv["nq"], dq_raw)], [gdt])
    dnq = _mm(f"mla_q_bwd_x_{l}", "nt", [(dq_raw, wq)], [F32])
    dwk = _mm(f"mla_k_bwd_w_{l}", "tn", [(sv["nkv"], dkf)], [gdt])
    dwv = _mm(f"mla_v_bwd_w_{l}", "tn", [(sv["nkv"], dvb)], [gdt])
    dnkv = _mm(f"mla_kv_bwd_x_{l}", "nt", [(dkf, wk), (dvb, wv)], [F32])

    dlogf = _forget_cumsum_bwd(f"cum_forget_bwd_{l}", dcfk, batch, cfg.seq, cfg.fox_h, fox.tq)

    def small_bwd(ps_, dlogf_, dnq_, dnkv_, dkf_, c_, s_, bf_, gq_, gkv_):
        _, vjp = jax.vjp(functools.partial(_small_core, cfg), ps_, bf_, gq_, gkv_)
        dps, dbf, dgq, dgkv = vjp((dlogf_, dnq_, dnkv_))
        dkpe = dkf_[:, :LANES].astype(F32)
        for hh in range(1, cfg.mla_h):
            dkpe = dkpe + dkf_[:, hh * LANES:(hh + 1) * LANES].astype(F32)
        lane = lax.broadcasted_iota(jnp.int32, (1, LANES), 1)
        dkpe = jnp.where((lane >= MLA_NOPE) & (lane < MLA_NOPE + MLA_ROPE), dkpe, 0.0)
        dkr = _rope_t(dkpe, c_, s_)
        dps = jnp.concatenate([dps[:, :cfg.small_w - LANES], dkr], axis=1)
        return dps, dbf, dgq, dgkv

    dps, dbf, dgq, dgkv = _rowwise(
        f"small_bwd_{l}", small_bwd, [sv["ps"], dlogf, dnq, dnkv, dkf, rope_c, rope_s], [bf_pad, g_cq, g_ckv],
        [(cfg.small_w, BF16)], accs=[(1, LANES), (1, cfg.q_rank), (1, cfg.kv_rank)])

    dpa = jnp.concatenate([dq_a.astype(BF16), dk_a, dv_a, dqm.astype(BF16)], axis=1)
    hb = sv["hb"]
    dh = _mm(f"proj_bwd_x_{l}", "nt", [(dpa, wa), (dgl, wg), (dps, ws)], [F32], tn=d)
    dwa = _mm(f"proj_a_bwd_w_{l}", "tn", [(hb, dpa)], [gdt])
    dwg = _mm(f"proj_gates_bwd_w_{l}", "tn", [(hb, dgl)], [gdt])
    dws = _mm(f"proj_small_bwd_w_{l}", "tn", [(hb, dps)], [gdt], tn=cfg.small_w)
    dw_in, dw_uq, dw_ukv = _layer_weight_grads(cfg, dwa, dwg, dws, dwq, dwk, dwv)
    big = dict(w_in=dw_in, w_uq=dw_uq, w_ukv=dw_ukv, w_mem_kv=dwmkv, w_br=dwbr, w_out=dwout,
               w_ff1=dwff1, w_ff2=dwff2)
    small = dict(b_forget=dbf[0, :cfg.fox_h], g_cq=dgq[0], g_ckv=dgkv[0], ln1_g=dg1[0], ln1_b=db1[0],
                 ln2_g=dg2[0], ln2_b=db2[0])
    return dh, dz1, big, small


def _rope_tables(positions):
    inv_freq = ROPE_BASE ** (-jnp.arange(0, MLA_ROPE, 2, dtype=F32) / MLA_ROPE)
    ang = positions.astype(F32).reshape(-1)[:, None] * inv_freq
    cos, sin = jnp.cos(ang), jnp.sin(ang)
    t = ang.shape[0]
    rope_c = jnp.concatenate([jnp.ones((t, MLA_NOPE), F32), cos, cos, jnp.zeros((t, LANES - MLA_NOPE - MLA_ROPE), F32)], axis=1)
    rope_s = jnp.concatenate([jnp.zeros((t, MLA_NOPE), F32), -sin, sin, jnp.zeros((t, LANES - MLA_NOPE - MLA_ROPE), F32)], axis=1)
    return rope_c, rope_s


def _local_step(cfg, x, mem, positions, target, small_w, comm):
    batch = x.shape[0]
    d, depth = cfg.d, cfg.depth
    t = batch * cfg.seq
    x2, tgt = x.reshape(t, d), target.reshape(t, d)
    mem_b = mem.reshape(batch * cfg.n_mem, d).astype(BF16)
    rope_c, rope_s = _rope_tables(positions)
    row = lambda v: v.reshape(1, -1)
    ln_in = (row(small_w["ln_in_g"]), row(small_w["ln_in_b"]))

    h, hb = _rowwise("ln_in", lambda x_, g_, b_: (_ln(x_, g_, b_),) * 2, [x2], list(ln_in), [(d, F32), (d, BF16)])
    layers, saves = [], []
    for l in range(depth):
        first = comm.weights(l, LATE)
        lw = _layer_weights(cfg, first["w_in"], first["w_uq"], first["w_ukv"]) + (first["w_mem_kv"],)
        later = lambda l=l: tuple(comm.weights(l, EARLY).values())
        par = dict(
            bf_pad=jnp.pad(row(small_w["b_forget"][l]), ((0, 0), (0, LANES - cfg.fox_h))),
            g_cq=row(small_w["g_cq"][l]), g_ckv=row(small_w["g_ckv"][l]),
            ln1=(row(small_w["ln1_g"][l]), row(small_w["ln1_b"][l])),
            ln2=(row(small_w["ln2_g"][l]), row(small_w["ln2_b"][l])))
        h, hb, sv = _layer_fwd(cfg, l, batch, h, hb, mem_b, rope_c, rope_s, comm.forward_hosts(l), (lw, later), **par)
        layers.append(dict(par, lw=sv.pop("lw")))
        saves.append(sv)

    def loss_fn(y, tg):
        err = y - tg
        part = 0.5 * jnp.sum(jnp.mean(err * err, axis=-1, keepdims=True), axis=0, keepdims=True)
        return err * (1.0 / d), jnp.broadcast_to(part, (1, LANES))

    ga, loss_acc = _rowwise("loss", loss_fn, [h, tgt], [], [(d, F32)], accs=[(1, LANES)])
    gb = None
    small_g = {k: [None] * depth for k in ("b_forget", "g_cq", "g_ckv", "ln1_g", "ln1_b", "ln2_g", "ln2_b")}
    for l in reversed(range(depth)):
        ga, gb, big, small = _layer_bwd(cfg, l, batch, ga, gb, saves[l], mem_b, rope_c, rope_s,
                                        comm.backward_hosts(l), **layers[l])
        comm.grads(l, big)
        for k, v in small.items():
            small_g[k][l] = v
    dx, _, dg_in, db_in = _ln_bwd("ln_in_bwd", cfg, ga, gb, x2, *ln_in)
    small_g = {k: jnp.stack(v) for k, v in small_g.items()}
    small_g["ln_in_g"], small_g["ln_in_b"] = dg_in[0], db_in[0]
    return loss_acc[0, 0], dx.reshape(x.shape), small_g


BIG = ("w_in", "w_uq", "w_ukv", "w_mem_kv", "w_br", "w_out", "w_ff1", "w_ff2")
SMALL = ("ln_in_g", "ln_in_b", "b_forget", "g_cq", "g_ckv", "ln1_g", "ln1_b", "ln2_g", "ln2_b")
ROW_CUT = ("w_mem_kv", "w_out", "w_ff2")
LATE, EARLY = BIG[:4], BIG[4:]


def _shard_2d(a):
    cols = a.shape[-1]
    rows = a.size // cols
    return a.reshape(2, rows // 2, cols)


def _full_from_slots(name, slots, shard_shape):
    if name in ROW_CUT and len(shard_shape) == 2:
        return slots.reshape((N_CHIPS * shard_shape[0], shard_shape[1]))
    parts = slots.reshape((N_CHIPS,) + shard_shape)
    axis = len(shard_shape) - (2 if name in ROW_CUT else 1)
    return jnp.concatenate([parts[i] for i in range(N_CHIPS)], axis=axis)


def _slots_from_full(name, full, shard_shape):
    if name in ROW_CUT and len(shard_shape) == 2:
        return full.reshape(N_CHIPS, 2, shard_shape[0] // 2, shard_shape[1])
    axis = len(shard_shape) - (2 if name in ROW_CUT else 1)
    parts = jnp.stack(jnp.split(full, N_CHIPS, axis=axis))
    cols = shard_shape[-1]
    return parts.reshape(N_CHIPS, 2, -1, cols)


def _pack_small(cfg, vals):
    flat = jnp.concatenate([vals[k].reshape(-1).astype(F32) for k in SMALL])
    pad = (-flat.shape[0]) % (LANES * LANES)
    return jnp.pad(flat, (0, pad)).reshape(-1, LANES)


def _unpack_small(packed, like):
    flat, out, off = packed.reshape(-1), {}, 0
    for k in SMALL:
        n = like[k].size
        out[k] = flat[off:off + n].reshape(like[k].shape)
        off += n
    return out


class LayerComm:
    def __init__(self, cfg, w, m, v):
        self.cfg, self.w = cfg, w
        self.shards = [{k: _shard_2d(w[k][l].astype(BF16)) for k in BIG} for l in range(cfg.depth)]
        self.got = {LATE: _gather_now([self.shards[0][k] for k in LATE])}
        self.sets = {}
        self.pending = None
        halves = lambda a: a.reshape(a.shape[0], 2, -1, a.shape[-1])
        self.state = {k: [halves(a[k]) for a in (w, m, v)] for k in BIG}
        self.outs = {k: None for k in BIG}

    def weights(self, l, names):
        got = self.got.pop(names) if names in self.got else self.sets.pop((l, names)).got
        mine = (_chip(), 0, 0, 0)
        return {k: _full_from_slots(k, lax.dynamic_update_slice(g, self.shards[l][k][None], mine), self.w[k].shape[1:])
                for k, g in zip(names, got)}

    def forward_hosts(self, l):
        self.sets[l, EARLY] = GatherSet([self.shards[l][k] for k in EARLY], ("fox_fwd", "mla_fwd"))
        if l + 1 < self.cfg.depth:
            self.sets[l + 1, LATE] = GatherSet([self.shards[l + 1][k] for k in LATE], ("mla_fwd", "mem_fwd"))
        return Together([s for (layer, _), s in self.sets.items() if layer in (l, l + 1)])

    def _reduce(self, tag, names, big, hosts):
        return ReduceLayer(tag, [_slots_from_full(k, big[k], self.w[k].shape[1:]) for k in names], hosts)

    def backward_hosts(self, l):
        comm = self

        class Riders(Together):
            def early_grads(self, grads):
                comm.early = comm._reduce(f"{l}e", EARLY, grads, ("fox_bwd", "mla_bwd", "mem_bwd"))
                self.members.append(comm.early)

        return Riders([self.pending[1]] if self.pending else [])

    def _update(self, l, names, reduce):
        for k, (mine, theirs) in zip(names, reduce.result()):
            self.outs[k] = _adamw_layer(f"adamw_{k}_{l}", l, *self.state[k], mine, theirs, self.outs[k])

    def grads(self, l, big):
        self._update(l, EARLY, self.early)
        if self.pending:
            self._update(self.pending[0], LATE, self.pending[1])
        self.pending = (l, self._reduce(f"{l}l", LATE, big, ("ff2_bwd_x", "fox_bwd", "mla_bwd")))

    def finish(self):
        self.pending[1].run_now()
        self._update(self.pending[0], LATE, self.pending[1])
        return {k: tuple(a.reshape(self.w[k].shape) for a in self.outs[k]) for k in BIG}


def _step(cfg, x, mem, positions, target, w, m, v):
    comm = LayerComm(cfg, w, m, v)
    small_w = {k: w[k] for k in SMALL}
    loss_local, dx, small_g = _local_step(cfg, x, mem, positions, target, small_w, comm)
    loss = lax.psum(loss_local, ("x", "y", "c"))
    outs_big = comm.finish()

    g_small = _allreduce_small(_pack_small(cfg, small_g))
    packs = [_pack_small(cfg, {k: d_[k] for k in SMALL}) for d_ in (w, m, v)]
    dl, nm, nv = _slabwise("adamw_small", _adamw_math, [a[None] for a in (packs[0], g_small, packs[1], packs[2])],
                           [F32, F32, F32])
    outs_small = [_unpack_small(a[0] if a.ndim == 3 else a, w) for a in (g_small, dl, nm, nv)]

    names = SMALL[:2] + ("w_in", "b_forget", "w_uq", "g_cq", "w_ukv", "g_ckv", "w_mem_kv", "w_br", "w_out",
                         "ln1_g", "ln1_b", "w_ff1", "w_ff2", "ln2_g", "ln2_b")
    result = [loss, dx]
    for part in range(4):
        for k in names:
            result.append(outs_big[k][part] if k in outs_big else outs_small[part][k])
    return tuple(result)


def kernel(x, mem, positions, ln_in_g, ln_in_b, w_in, b_forget, w_uq, g_cq, w_ukv, g_ckv, w_mem_kv, w_br, w_out, ln1_g, ln1_b, w_ff1, w_ff2, ln2_g, ln2_b, loss_target, m_ln_in_g, m_ln_in_b, m_w_in, m_b_forget, m_w_uq, m_g_cq, m_w_ukv, m_g_ckv, m_w_mem_kv, m_w_br, m_w_out, m_ln1_g, m_ln1_b, m_w_ff1, m_w_ff2, m_ln2_g, m_ln2_b, v_ln_in_g, v_ln_in_b, v_w_in, v_b_forget, v_w_uq, v_g_cq, v_w_ukv, v_g_ckv, v_w_mem_kv, v_w_br, v_w_out, v_ln1_g, v_ln1_b, v_w_ff1, v_w_ff2, v_ln2_g, v_ln2_b):
    w = dict(ln_in_g=ln_in_g, ln_in_b=ln_in_b, w_in=w_in, b_forget=b_forget, w_uq=w_uq, g_cq=g_cq, w_ukv=w_ukv,
             g_ckv=g_ckv, w_mem_kv=w_mem_kv, w_br=w_br, w_out=w_out, ln1_g=ln1_g, ln1_b=ln1_b, w_ff1=w_ff1,
             w_ff2=w_ff2, ln2_g=ln2_g, ln2_b=ln2_b)
    m = dict(ln_in_g=m_ln_in_g, ln_in_b=m_ln_in_b, w_in=m_w_in, b_forget=m_b_forget, w_uq=m_w_uq, g_cq=m_g_cq,
             w_ukv=m_w_ukv, g_ckv=m_g_ckv, w_mem_kv=m_w_mem_kv, w_br=m_w_br, w_out=m_w_out, ln1_g=m_ln1_g,
             ln1_b=m_ln1_b, w_ff1=m_w_ff1, w_ff2=m_w_ff2, ln2_g=m_ln2_g, ln2_b=m_ln2_b)
    v = dict(ln_in_g=v_ln_in_g, ln_in_b=v_ln_in_b, w_in=v_w_in, b_forget=v_b_forget, w_uq=v_w_uq, g_cq=v_g_cq,
             w_ukv=v_w_ukv, g_ckv=v_g_ckv, w_mem_kv=v_w_mem_kv, w_br=v_w_br, w_out=v_w_out, ln1_g=v_ln1_g,
             ln1_b=v_ln1_b, w_ff1=v_w_ff1, w_ff2=v_w_ff2, ln2_g=v_ln2_g, ln2_b=v_ln2_b)
    return _step(Cfg(), x, mem, positions, loss_target, w, m, v)
```

```python
import functools
from typing import NamedTuple

import jax
import jax.numpy as jnp
from jax import lax
from jax.experimental import pallas as pl
from jax.experimental.pallas import tpu as pltpu

F32 = jnp.float32
BF16 = jnp.bfloat16
MESH = pl.DeviceIdType.MESH

LANES = 128
SUBLANES = 8
VMEM_BYTES = 64 * 1024 * 1024
N_CHIPS = 4
N_DEV = 8

FOX_DH = 64
MLA_NOPE = 64
MLA_ROPE = 32
MLA_V = 64
MEM_DH = 128
ROPE_BASE = 10000.0
LN_EPS = 1e-5
RMS_EPS = 1e-6
NEG_INF = -1e30
LOG2E = 1.4426950408889634
ATTN_TILE = 512

ADAM_LR = 0.001
ADAM_B1 = 0.9
ADAM_B2 = 0.999
ADAM_EPS = 1e-08
ADAM_WD = 0.01
ADAM_STEP = 10


class Cfg(NamedTuple):
    d: int = 1024
    depth: int = 4
    seq: int = 2048
    chunk: int = 64
    n_mem: int = 256
    fox_h: int = 8
    mla_h: int = 8
    q_rank: int = 384
    kv_rank: int = 256
    mem_h: int = 4
    d_ff: int = 4096

    @property
    def width(self):
        return self.fox_h * FOX_DH

    @property
    def alpha(self):
        return (2 * self.depth) ** 0.25

    @property
    def small_w(self):
        return LANES + self.q_rank + self.kv_rank + LANES

    @property
    def in_splits(self):
        return (3 * self.width, self.fox_h, self.q_rank, self.kv_rank, MLA_ROPE, self.width, 3 * self.d)


class Exchange(NamedTuple):
    ins: tuple
    out_shapes: tuple
    plan: object
    n_copies: int
    aliases: dict = {}


def _peer(rel):
    x, y, c = lax.axis_index("x"), lax.axis_index("y"), lax.axis_index("c")
    if rel == "c":
        return (x, y, 1 - c)
    return ((1 - x) if rel in (1, 3) else x, (1 - y) if rel in (2, 3) else y, c)


def _exchange_copies(ex, in_refs, out_refs, send_sems, recv_sems):
    planned = ex.plan(in_refs, out_refs)
    assert len(planned) == ex.n_copies, len(planned)
    return [pltpu.make_async_remote_copy(src_ref=src, dst_ref=dst, send_sem=send_sems.at[i], recv_sem=recv_sems.at[i],
                                         device_id=_peer(rel), device_id_type=MESH)
            for i, (src, dst, rel) in enumerate(planned)]


def _pcall(body, rider=None, **kw):
    if rider is not None:
        n_in, n_out, grid = len(kw["in_specs"]), len(kw["out_specs"]), kw["grid"]
        n_rin, n_rout = len(rider.ins), len(rider.out_shapes)
        host = body

        def body(*refs):
            ins, rins = refs[:n_in], refs[n_in:n_in + n_rin]
            outs = refs[n_in + n_rin:n_in + n_rin + n_out]
            routs = refs[n_in + n_rin + n_out:n_in + n_rin + n_out + n_rout]
            scratch = refs[n_in + n_rin + n_out + n_rout:-2]
            copies = _exchange_copies(rider, rins, routs, refs[-2], refs[-1])
            first = functools.reduce(jnp.logical_and, [pl.program_id(a) == 0 for a in range(len(grid))])
            last = functools.reduce(jnp.logical_and, [pl.program_id(a) == n - 1 for a, n in enumerate(grid)])

            @pl.when(first)
            def _():
                for cp in copies:
                    cp.start()

            host(*ins, *outs, *scratch)

            @pl.when(last)
            def _():
                for cp in copies:
                    cp.wait()

        any_spec = pl.BlockSpec(memory_space=pl.ANY)
        sems = [pltpu.SemaphoreType.DMA((rider.n_copies,))] * 2
        kw = dict(
            kw,
            in_specs=list(kw["in_specs"]) + [any_spec] * n_rin,
            out_specs=list(kw["out_specs"]) + [any_spec] * n_rout,
            out_shape=list(kw["out_shape"]) + list(rider.out_shapes),
            scratch_shapes=list(kw.get("scratch_shapes", ())) + sems,
            input_output_aliases={**kw.get("input_output_aliases", {}),
                                  **{n_in + i: n_out + o for i, o in rider.aliases.items()}},
        )
    call = pl.pallas_call(body, **kw)
    return lambda *ops: call(*[pltpu.with_memory_space_constraint(o, pltpu.HBM) for o in ops])


class Hosts:
    def rider(self, host):
        return None

    def done(self, host, outs):
        pass

    def early_grads(self, grads):
        pass


def _merge_exchanges(exs):
    n_ins = [len(e.ins) for e in exs]
    n_outs = [len(e.out_shapes) for e in exs]

    def plan(in_refs, out_refs):
        copies, i, o = [], 0, 0
        for e, ni, no in zip(exs, n_ins, n_outs):
            copies += e.plan(in_refs[i:i + ni], out_refs[o:o + no])
            i, o = i + ni, o + no
        return copies

    aliases, i, o = {}, 0, 0
    for e, ni, no in zip(exs, n_ins, n_outs):
        aliases.update({i + a: o + b for a, b in e.aliases.items()})
        i, o = i + ni, o + no
    return Exchange(sum((e.ins for e in exs), ()), sum((e.out_shapes for e in exs), ()), plan,
                    sum(e.n_copies for e in exs), aliases)


class Together(Hosts):
    def __init__(self, members):
        self.members, self.active = list(members), []

    def rider(self, host):
        self.active = [(m, r) for m, r in ((m, m.rider(host)) for m in self.members) if r is not None]
        if not self.active:
            return None
        return _merge_exchanges([r for _, r in self.active])

    def done(self, host, outs):
        for m, r in self.active:
            m.done(host, outs[:len(r.out_shapes)])
            outs = outs[len(r.out_shapes):]


def _hosted(hosts, host, call, ops, n_results):
    rider = hosts.rider(host) if hosts is not None else None
    if rider is None:
        return call(None)(*ops)
    res = call(rider)(*ops, *rider.ins)
    hosts.done(host, list(res[n_results:]))
    return res[:n_results]


def _nbytes(shape, dtype):
    n = 1
    for s in shape:
        n *= s
    return n * jnp.dtype(dtype).itemsize


def _tile(dim, target):
    if dim <= target:
        return dim
    t = target - target % LANES
    while t >= LANES:
        if dim % t == 0:
            return t
        t -= LANES
    return dim


def _params(block_bytes, scratch_bytes=0):
    est = 2 * block_bytes + scratch_bytes + 24 * 1024 * 1024
    return pltpu.CompilerParams(vmem_limit_bytes=int(min(max(est, 32 * 1024 * 1024), VMEM_BYTES - 4 * 1024 * 1024)))


_DIMS = {"nn": (((1,), (0,)), ((), ())), "nt": (((1,), (1,)), ((), ())), "tn": (((0,), (0,)), ((), ()))}


MM_TILE = 1024
MM_BLOCK_BYTES = 16 * 1024 * 1024


def _mm_tiles(mode, pairs, out_dtypes, m, n, tm, tn):
    fixed_m, fixed_n = tm is not None, tn is not None
    tm, tn = _tile(m, tm or MM_TILE), _tile(n, tn or MM_TILE)

    def block_bytes(tm_, tn_):
        total = sum(_nbytes((tm_, tn_), dt) for dt in out_dtypes)
        for a, b in pairs:
            k = a.shape[0] if mode == "tn" else a.shape[1]
            total += _nbytes((k, tm_), a.dtype) + _nbytes((k, tn_), b.dtype)
        return total

    while block_bytes(tm, tn) > MM_BLOCK_BYTES:
        if not fixed_m and tm >= tn and tm > 2 * LANES:
            tm = _tile(m, tm // 2)
        elif not fixed_n and tn > 2 * LANES:
            tn = _tile(n, tn // 2)
        elif not fixed_m and tm > 2 * LANES:
            tm = _tile(m, tm // 2)
        else:
            break
    return tm, tn


def _mm(name, mode, pairs, out_dtypes, tm=None, tn=None, epi=None, row_extras=(), bc_extras=(), hosts=None, host=None):
    a0, b0 = pairs[0]
    m = a0.shape[1] if mode == "tn" else a0.shape[0]
    n = b0.shape[0] if mode == "nt" else b0.shape[1]
    tm, tn = _mm_tiles(mode, pairs, out_dtypes, m, n, tm, tn)
    in_specs, ops, blk = [], [], 0
    for a, b in pairs:
        if mode == "tn":
            k = a.shape[0]
            sa, sha = pl.BlockSpec((k, tm), lambda i, j: (0, i)), (k, tm)
        else:
            k = a.shape[1]
            sa, sha = pl.BlockSpec((tm, k), lambda i, j: (i, 0)), (tm, k)
        if mode == "nt":
            sb, shb = pl.BlockSpec((tn, k), lambda i, j: (j, 0)), (tn, k)
        else:
            sb, shb = pl.BlockSpec((k, tn), lambda i, j: (0, j)), (k, tn)
        in_specs += [sa, sb]
        ops += [a, b]
        blk += _nbytes(sha, a.dtype) + _nbytes(shb, b.dtype)
    for e in row_extras:
        w = e.shape[1]
        if w == n:
            in_specs.append(pl.BlockSpec((tm, tn), lambda i, j: (i, j)))
            blk += _nbytes((tm, tn), e.dtype)
        else:
            in_specs.append(pl.BlockSpec((tm, w), lambda i, j: (i, 0)))
            blk += _nbytes((tm, w), e.dtype)
        ops.append(e)
    for e in bc_extras:
        r, w = e.shape
        if w == n:
            in_specs.append(pl.BlockSpec((r, tn), lambda i, j: (0, j)))
        else:
            in_specs.append(pl.BlockSpec((r, w), lambda i, j: (0, 0)))
        blk += _nbytes((r, w), e.dtype)
        ops.append(e)
    npairs, nrow, nbc, nout = len(pairs), len(row_extras), len(bc_extras), len(out_dtypes)
    dims = _DIMS[mode]

    def body(*refs):
        acc = None
        for p in range(npairs):
            a = refs[2 * p][...].astype(BF16)
            b = refs[2 * p + 1][...].astype(BF16)
            d = lax.dot_general(a, b, dims, preferred_element_type=F32)
            acc = d if acc is None else acc + d
        ex = [r[...] for r in refs[2 * npairs:2 * npairs + nrow + nbc]]
        outs = (acc,) if epi is None else epi(acc, *ex)
        for o_ref, o in zip(refs[2 * npairs + nrow + nbc:], outs):
            o_ref[...] = o.astype(o_ref.dtype)

    blk += sum(_nbytes((tm, tn), dt) for dt in out_dtypes) + 2 * _nbytes((tm, tn), F32)
    call = lambda rider: _pcall(
        body,
        rider=rider,
        name=name,
        grid=(m // tm, n // tn),
        in_specs=in_specs,
        out_specs=[pl.BlockSpec((tm, tn), lambda i, j: (i, j)) for _ in range(nout)],
        out_shape=[jax.ShapeDtypeStruct((m, n), dt) for dt in out_dtypes],
        compiler_params=_params(blk),
    )
    res = _hosted(hosts, host, call, ops, nout)
    return res[0] if nout == 1 else res


def _rowwise(name, fn, row_ins, bc_ins, outs, accs=(), tm=256):
    rows = row_ins[0].shape[0]
    tm = min(tm, rows)
    assert rows % tm == 0
    nrow, nbc, nout, nacc = len(row_ins), len(bc_ins), len(outs), len(accs)
    in_specs = [pl.BlockSpec((tm, a.shape[1]), lambda i: (i, 0)) for a in row_ins]
    in_specs += [pl.BlockSpec(a.shape, lambda i: (0, 0)) for a in bc_ins]
    out_specs = [pl.BlockSpec((tm, w), lambda i: (i, 0)) for w, _ in outs]
    out_specs += [pl.BlockSpec(s, lambda i: (0, 0)) for s in accs]
    out_shape = [jax.ShapeDtypeStruct((rows, w), dt) for w, dt in outs]
    out_shape += [jax.ShapeDtypeStruct(s, F32) for s in accs]

    def body(*refs):
        vals = fn(*[r[...] for r in refs[:nrow + nbc]])
        o_refs = refs[nrow + nbc:]
        for r, v in zip(o_refs[:nout], vals[:nout]):
            r[...] = v.astype(r.dtype)
        if nacc:
            @pl.when(pl.program_id(0) == 0)
            def _():
                for r in o_refs[nout:]:
                    r[...] = jnp.zeros(r.shape, F32)

            for r, v in zip(o_refs[nout:], vals[nout:]):
                r[...] += v

    blk = sum(_nbytes((tm, a.shape[1]), a.dtype) for a in row_ins) + sum(_nbytes(a.shape, a.dtype) for a in bc_ins)
    blk += sum(_nbytes((tm, w), dt) for w, dt in outs) + sum(_nbytes(s, F32) for s in accs)
    res = _pcall(
        body,
        name=name,
        grid=(rows // tm,),
        in_specs=in_specs,
        out_specs=out_specs,
        out_shape=out_shape,
        compiler_params=_params(2 * blk),
    )(*row_ins, *bc_ins)
    return res


def _ln(z, g, b):
    mu = jnp.mean(z, axis=-1, keepdims=True)
    zc = z - mu
    var = jnp.mean(zc * zc, axis=-1, keepdims=True)
    return zc * lax.rsqrt(var + LN_EPS) * g + b


def _rms(x, g):
    return x * lax.rsqrt(jnp.mean(x * x, axis=-1, keepdims=True) + RMS_EPS) * g


def _colsum(v):
    return jnp.sum(v, axis=0, keepdims=True)


def _rope_swap(x):
    w = x.shape[1]
    lane = lax.broadcasted_iota(jnp.int32, (1, w), 1) % LANES
    from_left = pltpu.roll(x, 16, 1)
    from_right = pltpu.roll(x, w - 16, 1)
    lo = (lane >= MLA_NOPE) & (lane < MLA_NOPE + 16)
    hi = (lane >= MLA_NOPE + 16) & (lane < MLA_NOPE + 32)
    return jnp.where(hi, from_left, jnp.where(lo, from_right, 0.0))


def _rope(x, cos_t, sin_t):
    nh = x.shape[1] // LANES
    ct, st = jnp.tile(cos_t, (1, nh)), jnp.tile(sin_t, (1, nh))
    return x * ct + _rope_swap(x) * st


def _rope_t(dy, cos_t, sin_t):
    nh = dy.shape[1] // LANES
    ct, st = jnp.tile(cos_t, (1, nh)), jnp.tile(sin_t, (1, nh))
    return dy * ct + _rope_swap(dy * st)


def _block_cumsum(v, carry, reverse):
    tb = v.shape[0]
    r = lax.broadcasted_iota(jnp.int32, (tb, tb), 0)
    c = lax.broadcasted_iota(jnp.int32, (tb, tb), 1)
    tri = jnp.where((c >= r) if reverse else (c <= r), 1.0, 0.0).astype(BF16)
    hi = v.astype(BF16)
    r1 = v - hi.astype(F32)
    mid = r1.astype(BF16)
    lo = (r1 - mid.astype(F32)).astype(BF16)
    out = carry + sum(jnp.dot(tri, p, preferred_element_type=F32) for p in (hi, mid, lo))
    return out, (out[0:1, :] if reverse else out[tb - 1:tb, :])


def _forget_cumsum(name, logf, batch, seq, heads, tb):
    nb = seq // tb

    def body(x_ref, keys_ref, carry):
        @pl.when(pl.program_id(1) == 0)
        def _():
            carry[...] = jnp.zeros(carry.shape, F32)

        out, carry[...] = _block_cumsum(x_ref[...], carry[...], False)
        out = out * LOG2E
        keys_ref[...] = jnp.concatenate([jnp.broadcast_to(out[:, h:h + 1], (tb, LANES)) for h in range(heads)], axis=1)

    return _pcall(
        body,
        name=name,
        grid=(batch, nb),
        in_specs=[pl.BlockSpec((tb, LANES), lambda b, i: (b * nb + i, 0))],
        out_specs=pl.BlockSpec((tb, heads * LANES), lambda b, i: (b * nb + i, 0)),
        out_shape=jax.ShapeDtypeStruct((batch * seq, heads * LANES), F32),
        scratch_shapes=[pltpu.VMEM((1, LANES), F32)],
        compiler_params=_params(4 * tb * (heads + 2) * LANES * 4),
    )(logf)


def _forget_cumsum_bwd(name, dcf, batch, seq, heads, tb):
    nb = seq // tb

    def body(x_ref, o_ref, carry):
        @pl.when(pl.program_id(1) == 0)
        def _():
            carry[...] = jnp.zeros(carry.shape, F32)

        lane = lax.broadcasted_iota(jnp.int32, (1, LANES), 1)
        v = jnp.zeros((tb, LANES), F32)
        for g in range(heads // 2):
            blk = x_ref[:, g * LANES:(g + 1) * LANES]
            moved = pltpu.roll(blk, 2 * g, 1) if g else blk
            v = v + jnp.where((lane >= 2 * g) & (lane < 2 * g + 2), moved, 0.0)
        o_ref[...], carry[...] = _block_cumsum(v, carry[...], True)

    return _pcall(
        body,
        name=name,
        grid=(batch, nb),
        in_specs=[pl.BlockSpec((tb, (heads // 2) * LANES), lambda b, i: (b * nb + nb - 1 - i, 0))],
        out_specs=pl.BlockSpec((tb, LANES), lambda b, i: (b * nb + nb - 1 - i, 0)),
        out_shape=jax.ShapeDtypeStruct((batch * seq, LANES), F32),
        scratch_shapes=[pltpu.VMEM((1, LANES), F32)],
        compiler_params=_params(4 * tb * (heads // 2 + 1) * LANES * 4),
    )(dcf)


class Attn(NamedTuple):
    batch: int
    sq: int
    sk: int
    groups: int
    hq: int
    hv: int
    mode: str
    scale: float
    chunk: int
    tq: int
    tk: int

    @property
    def hg(self):
        return self.hv

    @property
    def qw(self):
        return LANES * self.hg // self.hq

    @property
    def dv(self):
        return LANES // self.hv


def _head_lanes(j, dv):
    lane = lax.broadcasted_iota(jnp.int32, (1, LANES), 1)
    return (lane >= j * dv) & (lane < (j + 1) * dv)


def _head_q(sp, j, q_blk):
    if sp.hq == 2:
        return jnp.where(_head_lanes(j, FOX_DH), q_blk, jnp.zeros_like(q_blk))
    return q_blk[:, LANES * j:LANES * (j + 1)]


def _head_rows(sp, j):
    return slice(j * sp.dv, (j + 1) * sp.dv) if sp.hg == 2 else slice(None)


def _scores_t(sp, j, k_c, q_j, cfk_rep, k0, q0, masked):
    tk, tq = k_c.shape[0], q_j.shape[0]
    k_j = k_c if sp.hq == 2 else k_c[:, LANES * j:LANES * (j + 1)]
    st = lax.dot_general(k_j, q_j, _DIMS["nt"], preferred_element_type=F32) * (sp.scale * LOG2E)
    if sp.mode == "fox":
        st = st - jnp.tile(cfk_rep[:, LANES * j:LANES * (j + 1)], (1, tq // LANES))
    if masked:
        kidx = k0 + lax.broadcasted_iota(jnp.int32, (tk, tq), 0)
        qidx = q0 + lax.broadcasted_iota(jnp.int32, (tk, tq), 1)
        if sp.mode == "chunk":
            shift = sp.chunk.bit_length() - 1
            kidx, qidx = jnp.right_shift(kidx, shift), jnp.right_shift(qidx, shift)
        st = jnp.where(kidx <= qidx, st, NEG_INF)
    return st


def _attn_fwd(name, sp, q, k, v, cfk=None, hosts=None, host=None):
    (qa, qo), (ka, ko), (va, vo) = q, k, v
    tq, tk, hg, qw = sp.tq, sp.tk, sp.hg, sp.qw
    nqb, nkc = sp.sq // tq, sp.sk // tk
    fox, causal = sp.mode == "fox", sp.mode != "none"
    assert sp.sq % tq == 0 and sp.sk % tk == 0 and (not causal or (tq == tk and sp.sq == sp.sk))

    def body(*refs):
        if fox:
            q_ref, k_ref, v_ref, cfk_ref, o_ref, lse_ref, acc_scr = refs
        else:
            q_ref, k_ref, v_ref, o_ref, lse_ref, acc_scr = refs
        i = pl.program_id(2)
        q0 = i * tq
        q_blk = q_ref[...]
        qs = [_head_q(sp, j, q_blk) for j in range(hg)]
        acc_scr[...] = jnp.zeros(acc_scr.shape, F32)

        def chunk(kc, carry, masked):
            ms, ls = carry
            k0 = pl.multiple_of(kc * tk, tk)
            k_c = k_ref[pl.ds(k0, tk), :]
            v_c = v_ref[pl.ds(k0, tk), :]
            new_m, new_l = [], []
            for j in range(hg):
                st = _scores_t(sp, j, k_c, qs[j], cfk_ref[pl.ds(k0, tk), :] if fox else None, k0, q0, masked)
                m_new = jnp.maximum(ms[j], jnp.max(st, axis=0, keepdims=True))
                alpha = jnp.exp2(ms[j] - m_new)
                pt = jnp.exp2(st - m_new)
                new_m.append(m_new)
                new_l.append(alpha * ls[j] + jnp.sum(pt, axis=0, keepdims=True))
                pv = lax.dot_general(v_c, pt.astype(BF16), _DIMS["tn"], preferred_element_type=F32)
                r = _head_rows(sp, j)
                acc_scr[r, :] = acc_scr[r, :] * alpha + pv[r, :]
            return tuple(new_m), tuple(new_l)

        carry = (tuple(jnp.full((1, tq), NEG_INF, F32) for _ in range(hg)),
                 tuple(jnp.zeros((1, tq), F32) for _ in range(hg)))
        if causal:
            carry = lax.fori_loop(0, i, functools.partial(chunk, masked=False), carry)
            ms, ls = chunk(i, carry, True)
        else:
            ms, ls = lax.fori_loop(0, nkc, functools.partial(chunk, masked=False), carry)
        for j in range(hg):
            r = _head_rows(sp, j)
            acc_scr[r, :] = acc_scr[r, :] / ls[j]
            lse_ref[j:j + 1, :] = ms[j] + jnp.log(ls[j]) * LOG2E
        o_ref[...] = acc_scr[...].T

    in_specs = [
        pl.BlockSpec((tq, qw), lambda b, g, i: (b * nqb + i, qo + g)),
        pl.BlockSpec((sp.sk, qw), lambda b, g, i: (b, ko + g)),
        pl.BlockSpec((sp.sk, LANES), lambda b, g, i: (b, vo + g)),
    ]
    ops = [qa, ka, va]
    stat_blk = pl.BlockSpec((None, None, None, hg, tq), lambda b, g, i: (b, g, i, 0, 0))
    if fox:
        in_specs.append(pl.BlockSpec((sp.sk, hg * LANES), lambda b, g, i: (b, g)))
        ops.append(cfk)
    blk = _nbytes((tq, qw), BF16) + _nbytes((sp.sk, qw + LANES), BF16) + 2 * _nbytes((tq, LANES), F32)
    blk += _nbytes((sp.sk, hg * LANES), F32) + 6 * _nbytes((tk, tq), F32)
    call = lambda rider: _pcall(
        body,
        rider=rider,
        name=name,
        grid=(sp.batch, sp.groups, nqb),
        in_specs=in_specs,
        out_specs=[pl.BlockSpec((tq, LANES), lambda b, g, i: (b * nqb + i, g)), stat_blk],
        out_shape=[
            jax.ShapeDtypeStruct((sp.batch * sp.sq, sp.groups * LANES), F32),
            jax.ShapeDtypeStruct((sp.batch, sp.groups, nqb, hg, tq), F32),
        ],
        scratch_shapes=[pltpu.VMEM((LANES, tq), F32)],
        compiler_params=_params(blk, tq * LANES * 4),
    )
    return _hosted(hosts, host, call, ops, 2)


def _attn_bwd(name, sp, q, k, v, o, lse, do, cfk=None, hosts=None, host=None):
    (qa, qo), (ka, ko), (va, vo) = q, k, v
    tq, tk, hg, qw, dv = sp.tq, sp.tk, sp.hg, sp.qw, sp.dv
    nqb, nkb = sp.sq // tq, sp.sk // tk
    fox, causal = sp.mode == "fox", sp.mode != "none"
    assert sp.sq % tq == 0 and sp.sk % tk == 0 and (not causal or (tq == tk and sp.sq == sp.sk))

    def body(*refs):
        if fox:
            (q_ref, k_ref, v_ref, lse_ref, do_ref, cfk_ref, kall_ref, vall_ref, cfkall_ref,
             dq_ref, dk_ref, dv_ref, dcf_ref, delta_scr, dk_scr, dv_scr, dqt_scr, dcf_scr) = refs
        else:
            (q_ref, k_ref, v_ref, o_ref, lse_ref, do_ref,
             dq_ref, dk_ref, dv_ref, delta_scr, dk_scr, dv_scr, dqt_scr) = refs
        kb = pl.program_id(2)
        k0 = kb * tk
        heads = [_head_lanes(j, dv) for j in range(hg)]

        def head_do(j, do_c):
            return jnp.where(heads[j], do_c, jnp.zeros_like(do_c)) if hg == 2 else do_c

        def probs_t(j, k_c, v_c, q_c, do_c, i, cf_keys, c0, masked):
            st = _scores_t(sp, j, k_c, _head_q(sp, j, q_c), cf_keys, c0, i * tq, masked)
            pt = jnp.exp2(st - lse_ref[i][j:j + 1, :])
            dpt = lax.dot_general(v_c, head_do(j, do_c), _DIMS["nt"], preferred_element_type=F32)
            return pt, dpt

        @pl.when(kb == 0)
        def _():
            dqt_scr[...] = jnp.zeros(dqt_scr.shape, F32)

            def fill(i, carry):
                r0 = pl.multiple_of(i * tq, tq)
                do_c = do_ref[pl.ds(r0, tq), :]
                if fox:
                    q_c = q_ref[pl.ds(r0, tq), :]

                    def keys(kc, acc, masked):
                        c0 = pl.multiple_of(kc * tk, tk)
                        out = []
                        for j in range(hg):
                            pt, dpt = probs_t(j, kall_ref[pl.ds(c0, tk), :], vall_ref[pl.ds(c0, tk), :], q_c, do_c, i,
                                              cfkall_ref[pl.ds(c0, tk), :], c0, masked)
                            out.append(acc[j] + jnp.sum(pt * dpt, axis=0, keepdims=True))
                        return tuple(out)

                    d = lax.fori_loop(0, i, functools.partial(keys, masked=False),
                                      tuple(jnp.zeros((1, tq), F32) for _ in range(hg)))
                    d = keys(i, d, True)
                    for j in range(hg):
                        delta_scr[i, j:j + 1, :] = d[j]
                else:
                    prod_t = (do_c.astype(F32) * o_ref[pl.ds(r0, tq), :]).T
                    for j in range(hg):
                        delta_scr[i, j:j + 1, :] = jnp.sum(prod_t[_head_rows(sp, j), :], axis=0, keepdims=True)
                return carry

            lax.fori_loop(0, nqb, fill, 0)

        k_blk = k_ref[...]
        v_blk = v_ref[...]
        k_t = k_blk.astype(F32).T.astype(BF16)
        dk_scr[...] = jnp.zeros(dk_scr.shape, F32)
        dv_scr[...] = jnp.zeros(dv_scr.shape, F32)
        if fox:
            dcf_scr[...] = jnp.zeros(dcf_scr.shape, F32)

        def qblock(i, carry, masked):
            r0 = pl.multiple_of(i * tq, tq)
            q_c = q_ref[pl.ds(r0, tq), :]
            do_c = do_ref[pl.ds(r0, tq), :]
            for j in range(hg):
                pt, dpt = probs_t(j, k_blk, v_blk, q_c, do_c, i, cfk_ref[...] if fox else None, k0, masked)
                dst = pt * (dpt - delta_scr[i][j:j + 1, :])
                if fox:
                    part = dst[:, :LANES]
                    for t in range(1, tq // LANES):
                        part = part + dst[:, t * LANES:(t + 1) * LANES]
                    dcf_scr[j] += part
                ds_b = (dst * sp.scale).astype(BF16)
                dv_scr[j] += jnp.dot(pt.astype(BF16), do_c, preferred_element_type=F32)
                dk_scr[j] += jnp.dot(ds_b, q_c if sp.hq == 2 else _head_q(sp, j, q_c), preferred_element_type=F32)
                if sp.hq == 2:
                    r = pl.ds(j * FOX_DH, FOX_DH)
                    dqt_scr[i, r, :] += jnp.dot(k_t[j * FOX_DH:(j + 1) * FOX_DH, :], ds_b, preferred_element_type=F32)
                else:
                    r = pl.ds(j * LANES, LANES)
                    dqt_scr[i, r, :] += jnp.dot(k_t[j * LANES:(j + 1) * LANES, :], ds_b, preferred_element_type=F32)
            return carry

        if causal:
            qblock(kb, 0, True)
            lax.fori_loop(kb + 1, nqb, functools.partial(qblock, masked=False), 0)
        else:
            lax.fori_loop(0, nqb, functools.partial(qblock, masked=False), 0)

        @pl.when(kb == nkb - 1)
        def _():
            def untranspose(i, carry):
                dq_ref[pl.ds(pl.multiple_of(i * tq, tq), tq), :] = dqt_scr[i].T
                return carry

            lax.fori_loop(0, nqb, untranspose, 0)

        if hg == 2:
            dv_ref[...] = jnp.where(heads[0], dv_scr[0], dv_scr[1]).astype(dv_ref.dtype)
        else:
            dv_ref[...] = dv_scr[0].astype(dv_ref.dtype)
        if sp.hq == 2:
            dk_ref[...] = jnp.where(_head_lanes(0, FOX_DH), dk_scr[0], dk_scr[1]).astype(dk_ref.dtype)
        elif hg == 2:
            dk_ref[...] = jnp.concatenate([dk_scr[0], dk_scr[1]], axis=1).astype(dk_ref.dtype)
        else:
            dk_ref[...] = dk_scr[0].astype(dk_ref.dtype)
        if fox:
            lane = lax.broadcasted_iota(jnp.int32, (1, LANES), 1)
            sums = [jnp.sum(dcf_scr[j], axis=1, keepdims=True) for j in range(hg)]
            dcf_ref[...] = jnp.where(lane == 0, -sums[0], jnp.where(lane == 1, -sums[1], 0.0))

    seq_lanes = lambda b, g, kb: (b, g)
    key_blk = lambda b, g, kb: (b * nkb + kb, g)
    stats = pl.BlockSpec((None, None, nqb, hg, tq), lambda b, g, kb: (b, g, 0, 0, 0))
    in_specs = [
        pl.BlockSpec((sp.sq, qw), lambda b, g, kb: (b, qo + g)),
        pl.BlockSpec((tk, qw), lambda b, g, kb: (b * nkb + kb, ko + g)),
        pl.BlockSpec((tk, LANES), lambda b, g, kb: (b * nkb + kb, vo + g)),
    ]
    ops = [qa, ka, va]
    if not fox:
        in_specs.append(pl.BlockSpec((sp.sq, LANES), seq_lanes))
        ops.append(o)
    in_specs += [stats, pl.BlockSpec((sp.sq, LANES), seq_lanes)]
    ops += [lse, do]
    out_specs = [pl.BlockSpec((sp.sq, qw), seq_lanes), pl.BlockSpec((tk, qw), key_blk), pl.BlockSpec((tk, LANES), key_blk)]
    out_shape = [
        jax.ShapeDtypeStruct((sp.batch * sp.sq, sp.groups * qw), F32),
        jax.ShapeDtypeStruct((sp.batch * sp.sk, sp.groups * qw), BF16),
        jax.ShapeDtypeStruct((sp.batch * sp.sk, sp.groups * LANES), BF16),
    ]
    scratch = [pltpu.VMEM((nqb, hg, tq), F32), pltpu.VMEM((hg, tk, LANES), F32), pltpu.VMEM((hg, tk, LANES), F32),
               pltpu.VMEM((nqb, qw, tq), F32)]
    if fox:
        in_specs += [
            pl.BlockSpec((tk, hg * LANES), key_blk),
            pl.BlockSpec((sp.sk, qw), lambda b, g, kb: (b, ko + g)),
            pl.BlockSpec((sp.sk, LANES), lambda b, g, kb: (b, vo + g)),
            pl.BlockSpec((sp.sk, hg * LANES), seq_lanes),
        ]
        ops += [cfk, ka, va, cfk]
        out_specs.append(pl.BlockSpec((tk, LANES), key_blk))
        out_shape.append(jax.ShapeDtypeStruct((sp.batch * sp.sk, sp.groups * LANES), F32))
        scratch.append(pltpu.VMEM((hg, tk, LANES), F32))
    blk = _nbytes((sp.sq, qw), BF16) + _nbytes((sp.sq, LANES), BF16) + 2 * _nbytes((sp.sq, LANES), F32)
    blk += _nbytes((sp.sq, qw), F32) + 4 * _nbytes((tk, qw), BF16) + 8 * _nbytes((tq, tk), F32)
    blk += (_nbytes((sp.sk, qw + LANES), BF16) + _nbytes((sp.sk, hg * LANES), F32)) if fox else 0
    call = lambda rider: _pcall(
        body,
        rider=rider,
        name=name,
        grid=(sp.batch, sp.groups, nkb),
        in_specs=in_specs,
        out_specs=out_specs,
        out_shape=out_shape,
        scratch_shapes=scratch,
        compiler_params=_params(blk, _nbytes((sp.sq, LANES), F32) + 4 * _nbytes((tk, LANES), F32)),
    )
    return _hosted(hosts, host, call, ops, len(out_shape))


def _slabwise(name, fn, ins, out_dtypes, rows_per_step=512):
    ins = [a if isinstance(a, tuple) else (a, None) for a in ins]
    n = max(1 if fixed is not None else a.shape[0] for a, fixed in ins)
    rows, cols = ins[0][0].shape[1:]
    tr = min(rows_per_step, rows)
    while rows % tr:
        tr //= 2
    assert tr % 16 == 0 or tr == rows, (name, rows, tr)

    def spec(a, fixed):
        if fixed is not None or a.shape[0] == 1:
            return pl.BlockSpec((None, tr, cols), lambda s, i: (fixed or 0, i, 0))
        return pl.BlockSpec((None, tr, cols), lambda s, i: (s, i, 0))

    def body(*refs):
        vals = fn(*[r[...] for r in refs[:len(ins)]])
        for r, v in zip(refs[len(ins):], vals):
            r[...] = v.astype(r.dtype)

    blk = (len(ins) + len(out_dtypes)) * _nbytes((tr, cols + LANES), F32)
    res = _pcall(
        body,
        name=name,
        grid=(n, rows // tr),
        in_specs=[spec(a, fixed) for a, fixed in ins],
        out_specs=[pl.BlockSpec((None, tr, cols), lambda s, i: (s, i, 0)) for _ in out_dtypes],
        out_shape=[jax.ShapeDtypeStruct((n, rows, cols), dt) for dt in out_dtypes],
        compiler_params=_params(2 * blk),
    )(*[a for a, _ in ins])
    return res


def _adamw_math(w, g, m, v):
    m = ADAM_B1 * m + (1.0 - ADAM_B1) * g
    v = ADAM_B2 * v + (1.0 - ADAM_B2) * jnp.square(g)
    m_hat = m / (1.0 - ADAM_B1 ** ADAM_STEP)
    v_hat = v / (1.0 - ADAM_B2 ** ADAM_STEP)
    delta = -ADAM_LR * (m_hat / (jnp.sqrt(v_hat) + ADAM_EPS) + ADAM_WD * w)
    return delta, m, v


def _run_exchange(name, ex):
    n_in, n_out = len(ex.ins), len(ex.out_shapes)

    def body(*refs):
        copies = _exchange_copies(ex, refs[:n_in], refs[n_in:n_in + n_out], refs[-2], refs[-1])
        for cp in copies:
            cp.start()
        for cp in copies:
            cp.wait()

    any_spec = pl.BlockSpec(memory_space=pl.ANY)
    return _pcall(
        body,
        name=name,
        in_specs=[any_spec] * n_in,
        out_specs=[any_spec] * n_out,
        out_shape=list(ex.out_shapes),
        scratch_shapes=[pltpu.SemaphoreType.DMA((ex.n_copies,))] * 2,
        input_output_aliases=dict(ex.aliases),
    )(*ex.ins)


def _chip(rel=0):
    x, y = lax.axis_index("x"), lax.axis_index("y")
    return 2 * ((1 - x) if rel & 1 else x) + ((1 - y) if rel & 2 else y)


def _gather_ici(shards):
    def plan(in_refs, out_refs):
        c = lax.axis_index("c")
        return [(s.at[c], g.at[_chip(), c], rel) for s, g in zip(in_refs, out_refs) for rel in (1, 2, 3)]

    shapes = tuple(jax.ShapeDtypeStruct((N_CHIPS,) + s.shape, s.dtype) for s in shards)
    return Exchange(tuple(shards), shapes, plan, 3 * len(shards))


def _gather_d2d(got):
    def plan(in_refs, out_refs):
        c = lax.axis_index("c")
        return [(g_in.at[_chip(rel), c], g_out.at[_chip(rel), c], "c")
                for g_in, g_out in zip(in_refs, out_refs) for rel in (1, 2, 3)]

    shapes = tuple(jax.ShapeDtypeStruct(g.shape, g.dtype) for g in got)
    return Exchange(tuple(got), shapes, plan, 3 * len(got), {i: i for i in range(len(got))})


class GatherSet(Hosts):
    def __init__(self, shards, hosts):
        self.shards, self.hosts, self.got = shards, hosts, None

    def rider(self, host):
        if host == self.hosts[0]:
            return _gather_ici(self.shards)
        if host == self.hosts[1]:
            return _gather_d2d(self.got)
        return None

    def done(self, host, outs):
        self.got = outs


def _gather_now(shards):
    got = _run_exchange("gather_weights_ici", _gather_ici(shards))
    return _run_exchange("gather_weights_d2d", _gather_d2d(got))


def _pair_sum(name, g, recv, rows_per_step=512):
    _, _, rows, cols = g.shape
    tr = min(rows_per_step, rows)
    while rows % tr:
        tr //= 2

    def body(g_ref, r_ref, p_ref, own_ref):
        mine = jnp.where(lax.axis_index("c") == 0, g_ref[0], g_ref[1])
        p = mine.astype(F32) + r_ref[...].astype(F32)
        p_ref[...] = p.astype(p_ref.dtype)

        @pl.when(pl.program_id(1) == _chip())
        def _():
            own_ref[...] = p

    slab = pl.BlockSpec((None, tr, cols), lambda i, s: (s, i, 0))
    return _pcall(
        body,
        name=name,
        grid=(rows // tr, N_CHIPS),
        in_specs=[pl.BlockSpec((None, 2, tr, cols), lambda i, s: (s, 0, i, 0)), slab],
        out_specs=[slab, pl.BlockSpec((None, tr, cols), lambda i, s: (0, i, 0))],
        out_shape=[jax.ShapeDtypeStruct((N_CHIPS, rows, cols), BF16), jax.ShapeDtypeStruct((1, rows, cols), F32)],
        compiler_params=_params(2 * 6 * _nbytes((tr, cols + LANES), F32)),
    )(g, recv)


class ReduceLayer(Hosts):
    def __init__(self, tag, grads, hosts):
        self.tag, self.grads, self.pair, self.own, self.total, self.theirs = tag, grads, None, None, None, None
        self.stage_of = dict(zip(hosts, ("swap", "chips", "share")))

    def _swap_halves(self):
        def plan(in_refs, out_refs):
            c = lax.axis_index("c")
            return [(g.at[pl.ds(0, N_CHIPS), 1 - c], r, "c") for g, r in zip(in_refs, out_refs)]

        shapes = tuple(jax.ShapeDtypeStruct((N_CHIPS,) + g.shape[2:], g.dtype) for g in self.grads)
        return Exchange(tuple(self.grads), shapes, plan, len(self.grads))

    def _to_chips(self):
        def plan(in_refs, out_refs):
            return [(p.at[_chip(rel)], r.at[rel - 1], rel) for p, r in zip(in_refs, out_refs) for rel in (1, 2, 3)]

        shapes = tuple(jax.ShapeDtypeStruct((3,) + p.shape[1:], p.dtype) for p in self.pair)
        return Exchange(tuple(self.pair), shapes, plan, 3 * len(self.pair))

    def _share(self):
        def plan(in_refs, out_refs):
            return [(t, r, "c") for t, r in zip(in_refs, out_refs)]

        shapes = tuple(jax.ShapeDtypeStruct(t.shape, F32) for t in self.total)
        return Exchange(tuple(self.total), shapes, plan, len(self.total))

    def rider(self, host):
        stages = {"swap": self._swap_halves, "chips": self._to_chips, "share": self._share}
        return stages[self.stage_of[host]]() if host in self.stage_of else None

    def done(self, host, outs):
        self._after(self.stage_of[host], outs)

    def _after(self, stage, outs):
        if stage == "swap":
            sums = [_pair_sum(f"reduce_pair_sum_{self.tag}_{i}", g, r) for i, (g, r) in enumerate(zip(self.grads, outs))]
            self.pair, self.own = [s[0] for s in sums], [s[1] for s in sums]
        elif stage == "chips":
            self.total = [
                _slabwise(f"reduce_chip_sum_{self.tag}_{i}",
                          lambda a, b, c_, d: (a + b.astype(F32) + c_.astype(F32) + d.astype(F32),),
                          [own, (r, 0), (r, 1), (r, 2)], [F32])[0]
                for i, (own, r) in enumerate(zip(self.own, outs))]
        else:
            self.theirs = outs

    def run_now(self):
        self._after("swap", _run_exchange(f"reduce_pair_{self.tag}", self._swap_halves()))
        self._after("chips", _run_exchange(f"reduce_chips_{self.tag}", self._to_chips()))
        self._after("share", _run_exchange(f"reduce_share_{self.tag}", self._share()))

    def result(self):
        return list(zip(self.total, self.theirs))


def _adamw_layer(name, l, w, m, v, mine, theirs, prev, rows_per_step=256):
    _, _, rows, cols = w.shape
    tr = min(rows_per_step, rows)
    while rows % tr:
        tr //= 2

    def body(w_ref, m_ref, v_ref, mine_ref, theirs_ref, *rest):
        g_ref, d_ref, nm_ref, nv_ref = rest[-4:]
        g = jnp.where(pl.program_id(0) == lax.axis_index("c"), mine_ref[...], theirs_ref[...])
        d, nm, nv = _adamw_math(w_ref[...], g, m_ref[...], v_ref[...])
        g_ref[...], d_ref[...], nm_ref[...], nv_ref[...] = g, d, nm, nv

    half = pl.BlockSpec((None, None, tr, cols), lambda h, i: (l, h, i, 0))
    one = pl.BlockSpec((None, tr, cols), lambda h, i: (0, i, 0))
    kept = [] if prev is None else list(prev)
    return _pcall(
        body,
        name=name,
        grid=(2, rows // tr),
        in_specs=[half, half, half, one, one] + [pl.BlockSpec(memory_space=pl.ANY)] * len(kept),
        out_specs=[half] * 4,
        out_shape=[jax.ShapeDtypeStruct(w.shape, F32)] * 4,
        input_output_aliases={5 + i: i for i in range(len(kept))},
        compiler_params=_params(2 * 9 * _nbytes((tr, cols + LANES), F32)),
    )(w, m, v, mine, theirs, *kept)


def _allreduce_small(v):
    rows = v.shape[0]

    def body(v_ref, sum_ref, all_ref, send_sems, recv_sems, local_sem):
        x, y, c = lax.axis_index("x"), lax.axis_index("y"), lax.axis_index("c")
        sibling = (x, y, 1 - c)
        chips = [(1 - x, y), (x, 1 - y), (1 - x, 1 - y)]

        def slab(px, py, pc):
            return all_ref.at[pl.ds((4 * px + 2 * py + pc) * rows, rows), :]

        def copy(k, block, to, src=None):
            return pltpu.make_async_remote_copy(
                src_ref=slab(*block) if src is None else src, dst_ref=slab(*block), send_sem=send_sems.at[k],
                recv_sem=recv_sems.at[k], device_id=to, device_id_type=MESH)

        mine = pltpu.make_async_copy(v_ref, slab(x, y, c), local_sem)
        mine.start()
        first = [copy(0, (x, y, c), sibling, src=v_ref)]
        first += [copy(1 + j, (x, y, c), (*chip, c), src=v_ref) for j, chip in enumerate(chips)]
        for cp in first:
            cp.start()
        passed = [copy(4 + j, (*chip, c), sibling) for j, chip in enumerate(chips)]
        for j, chip in enumerate(chips):
            copy(1 + j, (*chip, c), (x, y, c)).wait_recv()
            passed[j].start()
        copy(0, (x, y, 1 - c), (x, y, c)).wait_recv()
        for j, chip in enumerate(chips):
            copy(4 + j, (*chip, 1 - c), (x, y, c)).wait_recv()
        for cp in first + passed:
            cp.wait_send()
        mine.wait()
        total = all_ref[pl.ds(0, rows), :]
        for d in range(1, N_DEV):
            total = total + all_ref[pl.ds(d * rows, rows), :]
        sum_ref[...] = total

    vm = pl.BlockSpec(memory_space=pltpu.VMEM)
    return _pcall(
        body,
        name="allreduce_small",
        in_specs=[vm],
        out_specs=vm,
        out_shape=jax.ShapeDtypeStruct((rows, LANES), F32),
        scratch_shapes=[pltpu.VMEM((N_DEV * rows, LANES), F32), pltpu.SemaphoreType.DMA((7,)),
                        pltpu.SemaphoreType.DMA((7,)), pltpu.SemaphoreType.DMA],
    )(v)


def _pad_cols(a, before, total):
    return jnp.pad(a, ((0, 0), (before, total - before - a.shape[1])))


def _layer_weights(cfg, w_in, w_uq, w_ukv):
    w = cfg.width
    qkv, f, cq, ckv, kr, qm, gates = jnp.split(w_in, list(_cumsum(cfg.in_splits))[:-1], axis=1)
    wa = jnp.concatenate([qkv, qm], axis=1)
    ws = jnp.concatenate([_pad_cols(f, 0, LANES), cq, ckv, _pad_cols(kr, MLA_NOPE, LANES)], axis=1)
    wq = jnp.pad(w_uq.reshape(cfg.q_rank, cfg.mla_h, MLA_NOPE + MLA_ROPE), ((0, 0), (0, 0), (0, LANES - MLA_NOPE - MLA_ROPE)))
    wq = wq.reshape(cfg.q_rank, cfg.mla_h * LANES)
    kv = w_ukv.reshape(cfg.kv_rank, cfg.mla_h, MLA_NOPE + MLA_V)
    wk = jnp.pad(kv[:, :, :MLA_NOPE], ((0, 0), (0, 0), (0, LANES - MLA_NOPE))).reshape(cfg.kv_rank, cfg.mla_h * LANES)
    wv = kv[:, :, MLA_NOPE:].reshape(cfg.kv_rank, cfg.mla_h * MLA_V)
    del w
    return wa, gates, ws, wq, wk, wv


def _cumsum(xs):
    out, t = [], 0
    for v in xs:
        t += v
        out.append(t)
    return out


def _layer_weight_grads(cfg, dwa, dwg, dws, dwq, dwk, dwv):
    w, qr, kvr = cfg.width, cfg.q_rank, cfg.kv_rank
    off_kr = LANES + qr + kvr + MLA_NOPE
    dw_in = jnp.concatenate([
        dwa[:, :3 * w], dws[:, :cfg.fox_h], dws[:, LANES:LANES + qr], dws[:, LANES + qr:LANES + qr + kvr],
        dws[:, off_kr:off_kr + MLA_ROPE], dwa[:, 3 * w:], dwg], axis=1)
    dw_uq = dwq.reshape(qr, cfg.mla_h, LANES)[:, :, :MLA_NOPE + MLA_ROPE].reshape(qr, cfg.mla_h * (MLA_NOPE + MLA_ROPE))
    dw_ukv = jnp.concatenate([dwk.reshape(kvr, cfg.mla_h, LANES)[:, :, :MLA_NOPE], dwv.reshape(kvr, cfg.mla_h, MLA_V)],
                             axis=2).reshape(kvr, cfg.mla_h * (MLA_NOPE + MLA_V))
    return dw_in, dw_uq, dw_ukv


def _attn_specs(cfg, batch):
    t = min(ATTN_TILE, cfg.seq)
    common = dict(batch=batch, sq=cfg.seq, chunk=cfg.chunk, tq=t)
    fox = Attn(sk=cfg.seq, groups=cfg.fox_h // 2, hq=2, hv=2, mode="fox", scale=FOX_DH ** -0.5, tk=t, **common)
    mla = Attn(sk=cfg.seq, groups=cfg.mla_h // 2, hq=1, hv=2, mode="chunk",
               scale=(MLA_NOPE + MLA_ROPE) ** -0.5, tk=t, **common)
    mem = Attn(sk=cfg.n_mem, groups=cfg.mem_h, hq=1, hv=1, mode="none", scale=MEM_DH ** -0.5, tk=cfg.n_mem, **common)
    return fox, mla, mem


def _small_core(cfg, ps, bf, gq, gkv):
    qr, kvr = cfg.q_rank, cfg.kv_rank
    z = ps[:, :LANES] + bf
    logf = jnp.minimum(z, 0.0) - jnp.log1p(jnp.exp(-jnp.abs(z)))
    nq = _rms(ps[:, LANES:LANES + qr], gq)
    nkv = _rms(ps[:, LANES + qr:LANES + qr + kvr], gkv)
    return logf, nq, nkv


def _layer_fwd(cfg, l, batch, h, hb, mem_b, rope_c, rope_s, hosts, lw, bf_pad, g_cq, g_ckv, ln1, ln2):
    (wa, wg, ws, wq, wk, wv, wmkv), later_weights = lw
    w, d = cfg.width, cfg.d
    fox, mla, mem = _attn_specs(cfg, batch)
    nw = w // LANES
    pa = _mm(f"proj_a_{l}", "nn", [(hb, wa)], [BF16])
    gl = _mm(f"proj_gates_{l}", "nn", [(hb, wg)], [F32])
    ps = _mm(f"proj_small_{l}", "nn", [(hb, ws)], [F32], tn=cfg.small_w)

    def small_fwd(ps_, c_, s_, bf_, gq_, gkv_):
        logf, nq, nkv = _small_core(cfg, ps_, bf_, gq_, gkv_)
        kpe = _rope(ps_[:, cfg.small_w - LANES:], c_, s_)
        return logf, nq, nkv, kpe

    logf, nq, nkv, kpe = _rowwise(
        f"small_fwd_{l}", small_fwd, [ps, rope_c, rope_s], [bf_pad, g_cq, g_ckv],
        [(LANES, F32), (cfg.q_rank, BF16), (cfg.kv_rank, BF16), (LANES, F32)])
    cfk = _forget_cumsum(f"cum_forget_{l}", logf, batch, cfg.seq, cfg.fox_h, fox.tq)

    qf = _mm(f"mla_q_{l}", "nn", [(nq, wq)], [BF16], tn=wq.shape[1],
             epi=lambda acc, c_, s_: (_rope(acc, c_, s_),), row_extras=[rope_c, rope_s])
    kf = _mm(f"mla_k_{l}", "nn", [(nkv, wk)], [BF16], tn=wk.shape[1],
             epi=lambda acc, kp: (acc + jnp.tile(kp, (1, cfg.mla_h)),), row_extras=[kpe])
    vb = _mm(f"mla_v_{l}", "nn", [(nkv, wv)], [BF16])
    mkv = _mm(f"mem_kv_{l}", "nn", [(mem_b, wmkv)], [BF16])

    o_a, lse_a = _attn_fwd(f"fox_fwd_{l}", fox, (pa, 0), (pa, nw), (pa, 2 * nw), cfk,
                           hosts=hosts, host="fox_fwd")
    o_b, lse_b = _attn_fwd(f"mla_fwd_{l}", mla, (qf, 0), (kf, 0), (vb, 0), hosts=hosts, host="mla_fwd")
    o_c, lse_c = _attn_fwd(f"mem_fwd_{l}", mem, (pa, 3 * nw), (mkv, 0), (mkv, nw), hosts=hosts, host="mem_fwd")
    wbr, wout, wff1, wff2 = later_weights()
    bps = [_mm(f"branch_{n}_{l}", "nn", [(o, wbr[n])], [F32]) for n, o in enumerate((o_a, o_b, o_c))]

    def merge(gl_, b0, b1, b2):
        g = jax.nn.sigmoid(gl_)
        return (g[:, :d] * b0 + g[:, d:2 * d] * b1 + g[:, 2 * d:] * b2,)

    (merged,) = _rowwise(f"merge_{l}", merge, [gl] + bps, [], [(d, BF16)])

    def post_ln(acc, res, g_, b_):
        z = cfg.alpha * res + acc
        y = _ln(z, g_, b_)
        return z, y, y

    z1, h1, h1b = _mm(f"out_ln1_{l}", "nn", [(merged, wout)], [F32, F32, BF16], tm=256, tn=d,
                      epi=post_ln, row_extras=[h], bc_extras=list(ln1))
    u, a = _mm(f"ff1_{l}", "nn", [(h1b, wff1)], [BF16, BF16],
               epi=lambda acc: (acc, jnp.square(jnp.maximum(acc, 0.0))))
    z2, h2, h2b = _mm(f"ff2_ln2_{l}", "nn", [(a, wff2)], [F32, F32, BF16], tm=256, tn=d,
                      epi=post_ln, row_extras=[h1], bc_extras=list(ln2))
    saved = dict(hb=hb, pa=pa, gl=gl, ps=ps, nq=nq, nkv=nkv, cfk=cfk, qf=qf, kf=kf, vb=vb, mkv=mkv,
                 o=(o_a, o_b, o_c), lse=(lse_a, lse_b, lse_c), bps=bps, merged=merged, z1=z1, h1b=h1b, u=u, a=a, z2=z2,
                 lw=(wa, wg, ws, wq, wk, wv, wmkv, wbr, wout, wff1, wff2))
    return h2, h2b, saved


def _ln_bwd(name, cfg, ga, gb, z, g, b):
    d = cfg.d

    def fn(*vals):
        if gb is None:
            ga_, z_, g_, b_ = vals
            dy = ga_
        else:
            ga_, gb_, z_, g_, b_ = vals
            dy = ga_ + cfg.alpha * gb_
        _, vjp = jax.vjp(_ln, z_, g_, b_)
        dz, dg, db = vjp(dy)
        return dz, dz, dg, db

    rows = [ga, z] if gb is None else [ga, gb, z]
    return _rowwise(name, fn, rows, [g, b], [(d, F32), (d, BF16)], accs=[(1, d), (1, d)])


def _layer_bwd(cfg, l, batch, ga, gb, sv, mem_b, rope_c, rope_s, hosts, lw, bf_pad, g_cq, g_ckv, ln1, ln2):
    wa, wg, ws, wq, wk, wv, wmkv, wbr, wout, wff1, wff2 = lw
    w, d = cfg.width, cfg.d
    fox, mla, mem = _attn_specs(cfg, batch)
    nw = w // LANES
    gdt = BF16

    dz2, dz2b, dg2, db2 = _ln_bwd(f"ln2_bwd_{l}", cfg, ga, gb, sv["z2"], *ln2)
    du = _mm(f"ff2_bwd_x_{l}", "nt", [(dz2b, wff2)], [BF16],
             epi=lambda acc, u_: (acc * (2.0 * jnp.maximum(u_.astype(F32), 0.0)),), row_extras=[sv["u"]],
             hosts=hosts, host="ff2_bwd_x")
    dwff2 = _mm(f"ff2_bwd_w_{l}", "tn", [(sv["a"], dz2b)], [gdt])
    dwff1 = _mm(f"ff1_bwd_w_{l}", "tn", [(sv["h1b"], du)], [gdt])
    dh1 = _mm(f"ff1_bwd_x_{l}", "nt", [(du, wff1)], [F32])
    dz1, dz1b, dg1, db1 = _ln_bwd(f"ln1_bwd_{l}", cfg, dh1, dz2, sv["z1"], *ln1)
    dmerged = _mm(f"out_bwd_x_{l}", "nt", [(dz1b, wout)], [F32])
    dwout = _mm(f"out_bwd_w_{l}", "tn", [(sv["merged"], dz1b)], [gdt])

    def merge_bwd(dm, gl_, b0, b1, b2):
        def f(gl__, b0_, b1_, b2_):
            g = jax.nn.sigmoid(gl__)
            return g[:, :d] * b0_ + g[:, d:2 * d] * b1_ + g[:, 2 * d:] * b2_

        _, vjp = jax.vjp(f, gl_, b0, b1, b2)
        return vjp(dm)

    dgl, db0, db1_, db2_ = _rowwise(f"merge_bwd_{l}", merge_bwd, [dmerged, sv["gl"]] + sv["bps"], [],
                                    [(3 * d, BF16), (d, BF16), (d, BF16), (d, BF16)])
    dbps = (db0, db1_, db2_)
    dos = [_mm(f"branch_bwd_x_{n}_{l}", "nt", [(dbps[n], wbr[n])], [BF16]) for n in range(3)]
    dwbr = jnp.stack([_mm(f"branch_bwd_w_{n}_{l}", "tn", [(sv["o"][n], dbps[n])], [gdt]) for n in range(3)])
    if hosts is not None:
        hosts.early_grads(dict(w_br=dwbr, w_out=dwout, w_ff1=dwff1, w_ff2=dwff2))

    pa = sv["pa"]
    dq_a, dk_a, dv_a, dcfk = _attn_bwd(f"fox_bwd_{l}", fox, (pa, 0), (pa, nw), (pa, 2 * nw), sv["o"][0], sv["lse"][0],
                                       dos[0], sv["cfk"], hosts=hosts, host="fox_bwd")
    dqf, dkf, dvb = _attn_bwd(f"mla_bwd_{l}", mla, (sv["qf"], 0), (sv["kf"], 0), (sv["vb"], 0), sv["o"][1],
                              sv["lse"][1], dos[1], hosts=hosts, host="mla_bwd")
    dqm, dmk, dmv = _attn_bwd(f"mem_bwd_{l}", mem, (pa, 3 * nw), (sv["mkv"], 0), (sv["mkv"], nw), sv["o"][2],
                              sv["lse"][2], dos[2], hosts=hosts, host="mem_bwd")
    dwmkv = _mm(f"mem_kv_bwd_w_{l}", "tn", [(mem_b, jnp.concatenate([dmk, dmv], axis=1))], [gdt])

    (dq_raw,) = _rowwise(f"mla_q_rope_bwd_{l}", lambda dy, c_, s_: (_rope_t(dy, c_, s_),), [dqf, rope_c, rope_s], [],
                         [(wq.shape[1], BF16)])
    dwq = _mm(f"mla_q_bwd_w_{l}", "tn", [(sv["nq"], dq_raw)], [gdt])
    dnq = _mm(f"mla_q_bwd_x_{l}", "nt", [(dq_raw, wq)], [F32])
    dwk = _mm(f"mla_k_bwd_w_{l}", "tn", [(sv["nkv"], dkf)], [gdt])
    dwv = _mm(f"mla_v_bwd_w_{l}", "tn", [(sv["nkv"], dvb)], [gdt])
    dnkv = _mm(f"mla_kv_bwd_x_{l}", "nt", [(dkf, wk), (dvb, wv)], [F32])

    dlogf = _forget_cumsum_bwd(f"cum_forget_bwd_{l}", dcfk, batch, cfg.seq, cfg.fox_h, fox.tq)

    def small_bwd(ps_, dlogf_, dnq_, dnkv_, dkf_, c_, s_, bf_, gq_, gkv_):
        _, vjp = jax.vjp(functools.partial(_small_core, cfg), ps_, bf_, gq_, gkv_)
        dps, dbf, dgq, dgkv = vjp((dlogf_, dnq_, dnkv_))
        dkpe = dkf_[:, :LANES].astype(F32)
        for hh in range(1, cfg.mla_h):
            dkpe = dkpe + dkf_[:, hh * LANES:(hh + 1) * LANES].astype(F32)
        lane = lax.broadcasted_iota(jnp.int32, (1, LANES), 1)
        dkpe = jnp.where((lane >= MLA_NOPE) & (lane < MLA_NOPE + MLA_ROPE), dkpe, 0.0)
        dkr = _rope_t(dkpe, c_, s_)
        dps = jnp.concatenate([dps[:, :cfg.small_w - LANES], dkr], axis=1)
        return dps, dbf, dgq, dgkv

    dps, dbf, dgq, dgkv = _rowwise(
        f"small_bwd_{l}", small_bwd, [sv["ps"], dlogf, dnq, dnkv, dkf, rope_c, rope_s], [bf_pad, g_cq, g_ckv],
        [(cfg.small_w, BF16)], accs=[(1, LANES), (1, cfg.q_rank), (1, cfg.kv_rank)])

    dpa = jnp.concatenate([dq_a.astype(BF16), dk_a, dv_a, dqm.astype(BF16)], axis=1)
    hb = sv["hb"]
    dh = _mm(f"proj_bwd_x_{l}", "nt", [(dpa, wa), (dgl, wg), (dps, ws)], [F32], tn=d)
    dwa = _mm(f"proj_a_bwd_w_{l}", "tn", [(hb, dpa)], [gdt])
    dwg = _mm(f"proj_gates_bwd_w_{l}", "tn", [(hb, dgl)], [gdt])
    dws = _mm(f"proj_small_bwd_w_{l}", "tn", [(hb, dps)], [gdt], tn=cfg.small_w)
    dw_in, dw_uq, dw_ukv = _layer_weight_grads(cfg, dwa, dwg, dws, dwq, dwk, dwv)
    big = dict(w_in=dw_in, w_uq=dw_uq, w_ukv=dw_ukv, w_mem_kv=dwmkv, w_br=dwbr, w_out=dwout,
               w_ff1=dwff1, w_ff2=dwff2)
    small = dict(b_forget=dbf[0, :cfg.fox_h], g_cq=dgq[0], g_ckv=dgkv[0], ln1_g=dg1[0], ln1_b=db1[0],
                 ln2_g=dg2[0], ln2_b=db2[0])
    return dh, dz1, big, small


def _rope_tables(positions):
    inv_freq = ROPE_BASE ** (-jnp.arange(0, MLA_ROPE, 2, dtype=F32) / MLA_ROPE)
    ang = positions.astype(F32).reshape(-1)[:, None] * inv_freq
    cos, sin = jnp.cos(ang), jnp.sin(ang)
    t = ang.shape[0]
    rope_c = jnp.concatenate([jnp.ones((t, MLA_NOPE), F32), cos, cos, jnp.zeros((t, LANES - MLA_NOPE - MLA_ROPE), F32)], axis=1)
    rope_s = jnp.concatenate([jnp.zeros((t, MLA_NOPE), F32), -sin, sin, jnp.zeros((t, LANES - MLA_NOPE - MLA_ROPE), F32)], axis=1)
    return rope_c, rope_s


def _local_step(cfg, x, mem, positions, target, small_w, comm):
    batch = x.shape[0]
    d, depth = cfg.d, cfg.depth
    t = batch * cfg.seq
    x2, tgt = x.reshape(t, d), target.reshape(t, d)
    mem_b = mem.reshape(batch * cfg.n_mem, d).astype(BF16)
    rope_c, rope_s = _rope_tables(positions)
    row = lambda v: v.reshape(1, -1)
    ln_in = (row(small_w["ln_in_g"]), row(small_w["ln_in_b"]))

    h, hb = _rowwise("ln_in", lambda x_, g_, b_: (_ln(x_, g_, b_),) * 2, [x2], list(ln_in), [(d, F32), (d, BF16)])
    layers, saves = [], []
    for l in range(depth):
        first = comm.weights(l, LATE)
        lw = _layer_weights(cfg, first["w_in"], first["w_uq"], first["w_ukv"]) + (first["w_mem_kv"],)
        later = lambda l=l: tuple(comm.weights(l, EARLY).values())
        par = dict(
            bf_pad=jnp.pad(row(small_w["b_forget"][l]), ((0, 0), (0, LANES - cfg.fox_h))),
            g_cq=row(small_w["g_cq"][l]), g_ckv=row(small_w["g_ckv"][l]),
            ln1=(row(small_w["ln1_g"][l]), row(small_w["ln1_b"][l])),
            ln2=(row(small_w["ln2_g"][l]), row(small_w["ln2_b"][l])))
        h, hb, sv = _layer_fwd(cfg, l, batch, h, hb, mem_b, rope_c, rope_s, comm.forward_hosts(l), (lw, later), **par)
        layers.append(dict(par, lw=sv.pop("lw")))
        saves.append(sv)

    def loss_fn(y, tg):
        err = y - tg
        part = 0.5 * jnp.sum(jnp.mean(err * err, axis=-1, keepdims=True), axis=0, keepdims=True)
        return err * (1.0 / d), jnp.broadcast_to(part, (1, LANES))

    ga, loss_acc = _rowwise("loss", loss_fn, [h, tgt], [], [(d, F32)], accs=[(1, LANES)])
    gb = None
    small_g = {k: [None] * depth for k in ("b_forget", "g_cq", "g_ckv", "ln1_g", "ln1_b", "ln2_g", "ln2_b")}
    for l in reversed(range(depth)):
        ga, gb, big, small = _layer_bwd(cfg, l, batch, ga, gb, saves[l], mem_b, rope_c, rope_s,
                                        comm.backward_hosts(l), **layers[l])
        comm.grads(l, big)
        for k, v in small.items():
            small_g[k][l] = v
    dx, _, dg_in, db_in = _ln_bwd("ln_in_bwd", cfg, ga, gb, x2, *ln_in)
    small_g = {k: jnp.stack(v) for k, v in small_g.items()}
    small_g["ln_in_g"], small_g["ln_in_b"] = dg_in[0], db_in[0]
    return loss_acc[0, 0], dx.reshape(x.shape), small_g


BIG = ("w_in", "w_uq", "w_ukv", "w_mem_kv", "w_br", "w_out", "w_ff1", "w_ff2")
SMALL = ("ln_in_g", "ln_in_b", "b_forget", "g_cq", "g_ckv", "ln1_g", "ln1_b", "ln2_g", "ln2_b")
ROW_CUT = ("w_mem_kv", "w_out", "w_ff2")
LATE, EARLY = BIG[:4], BIG[4:]


def _shard_2d(a):
    cols = a.shape[-1]
    rows = a.size // cols
    return a.reshape(2, rows // 2, cols)


def _full_from_slots(name, slots, shard_shape):
    if name in ROW_CUT and len(shard_shape) == 2:
        return slots.reshape((N_CHIPS * shard_shape[0], shard_shape[1]))
    parts = slots.reshape((N_CHIPS,) + shard_shape)
    axis = len(shard_shape) - (2 if name in ROW_CUT else 1)
    return jnp.concatenate([parts[i] for i in range(N_CHIPS)], axis=axis)


def _slots_from_full(name, full, shard_shape):
    if name in ROW_CUT and len(shard_shape) == 2:
        return full.reshape(N_CHIPS, 2, shard_shape[0] // 2, shard_shape[1])
    axis = len(shard_shape) - (2 if name in ROW_CUT else 1)
    parts = jnp.stack(jnp.split(full, N_CHIPS, axis=axis))
    cols = shard_shape[-1]
    return parts.reshape(N_CHIPS, 2, -1, cols)


def _pack_small(cfg, vals):
    flat = jnp.concatenate([vals[k].reshape(-1).astype(F32) for k in SMALL])
    pad = (-flat.shape[0]) % (LANES * LANES)
    return jnp.pad(flat, (0, pad)).reshape(-1, LANES)


def _unpack_small(packed, like):
    flat, out, off = packed.reshape(-1), {}, 0
    for k in SMALL:
        n = like[k].size
        out[k] = flat[off:off + n].reshape(like[k].shape)
        off += n
    return out


class LayerComm:
    def __init__(self, cfg, w, m, v):
        self.cfg, self.w = cfg, w
        self.shards = [{k: _shard_2d(w[k][l].astype(BF16)) for k in BIG} for l in range(cfg.depth)]
        self.got = {LATE: _gather_now([self.shards[0][k] for k in LATE])}
        self.sets = {}
        self.pending = None
        halves = lambda a: a.reshape(a.shape[0], 2, -1, a.shape[-1])
        self.state = {k: [halves(a[k]) for a in (w, m, v)] for k in BIG}
        self.outs = {k: None for k in BIG}

    def weights(self, l, names):
        got = self.got.pop(names) if names in self.got else self.sets.pop((l, names)).got
        mine = (_chip(), 0, 0, 0)
        return {k: _full_from_slots(k, lax.dynamic_update_slice(g, self.shards[l][k][None], mine), self.w[k].shape[1:])
                for k, g in zip(names, got)}

    def forward_hosts(self, l):
        self.sets[l, EARLY] = GatherSet([self.shards[l][k] for k in EARLY], ("fox_fwd", "mla_fwd"))
        if l + 1 < self.cfg.depth:
            self.sets[l + 1, LATE] = GatherSet([self.shards[l + 1][k] for k in LATE], ("mla_fwd", "mem_fwd"))
        return Together([s for (layer, _), s in self.sets.items() if layer in (l, l + 1)])

    def _reduce(self, tag, names, big, hosts):
        return ReduceLayer(tag, [_slots_from_full(k, big[k], self.w[k].shape[1:]) for k in names], hosts)

    def backward_hosts(self, l):
        comm = self

        class Riders(Together):
            def early_grads(self, grads):
                comm.early = comm._reduce(f"{l}e", EARLY, grads, ("fox_bwd", "mla_bwd", "mem_bwd"))
                self.members.append(comm.early)

        return Riders([self.pending[1]] if self.pending else [])

    def _update(self, l, names, reduce):
        for k, (mine, theirs) in zip(names, reduce.result()):
            self.outs[k] = _adamw_layer(f"adamw_{k}_{l}", l, *self.state[k], mine, theirs, self.outs[k])

    def grads(self, l, big):
        self._update(l, EARLY, self.early)
        if self.pending:
            self._update(self.pending[0], LATE, self.pending[1])
        self.pending = (l, self._reduce(f"{l}l", LATE, big, ("ff2_bwd_x", "fox_bwd", "mla_bwd")))

    def finish(self):
        self.pending[1].run_now()
        self._update(self.pending[0], LATE, self.pending[1])
        return {k: tuple(a.reshape(self.w[k].shape) for a in self.outs[k]) for k in BIG}


def _step(cfg, x, mem, positions, target, w, m, v):
    comm = LayerComm(cfg, w, m, v)
    small_w = {k: w[k] for k in SMALL}
    loss_local, dx, small_g = _local_step(cfg, x, mem, positions, target, small_w, comm)
    loss = lax.psum(loss_local, ("x", "y", "c"))
    outs_big = comm.finish()

    g_small = _allreduce_small(_pack_small(cfg, small_g))
    packs = [_pack_small(cfg, {k: d_[k] for k in SMALL}) for d_ in (w, m, v)]
    dl, nm, nv = _slabwise("adamw_small", _adamw_math, [a[None] for a in (packs[0], g_small, packs[1], packs[2])],
                           [F32, F32, F32])
    outs_small = [_unpack_small(a[0] if a.ndim == 3 else a, w) for a in (g_small, dl, nm, nv)]

    names = SMALL[:2] + ("w_in", "b_forget", "w_uq", "g_cq", "w_ukv", "g_ckv", "w_mem_kv", "w_br", "w_out",
                         "ln1_g", "ln1_b", "w_ff1", "w_ff2", "ln2_g", "ln2_b")
    result = [loss, dx]
    for part in range(4):
        for k in names:
            result.append(outs_big[k][part] if k in outs_big else outs_small[part][k])
    return tuple(result)


def kernel(x, mem, positions, ln_in_g, ln_in_b, w_in, b_forget, w_uq, g_cq, w_ukv, g_ckv, w_mem_kv, w_br, w_out, ln1_g, ln1_b, w_ff1, w_ff2, ln2_g, ln2_b, loss_target, m_ln_in_g, m_ln_in_b, m_w_in, m_b_forget, m_w_uq, m_g_cq, m_w_ukv, m_g_ckv, m_w_mem_kv, m_w_br, m_w_out, m_ln1_g, m_ln1_b, m_w_ff1, m_w_ff2, m_ln2_g, m_ln2_b, v_ln_in_g, v_ln_in_b, v_w_in, v_b_forget, v_w_uq, v_g_cq, v_w_ukv, v_g_ckv, v_w_mem_kv, v_w_br, v_w_out, v_ln1_g, v_ln1_b, v_w_ff1, v_w_ff2, v_ln2_g, v_ln2_b):
    w = dict(ln_in_g=ln_in_g, ln_in_b=ln_in_b, w_in=w_in, b_forget=b_forget, w_uq=w_uq, g_cq=g_cq, w_ukv=w_ukv,
             g_ckv=g_ckv, w_mem_kv=w_mem_kv, w_br=w_br, w_out=w_out, ln1_g=ln1_g, ln1_b=ln1_b, w_ff1=w_ff1,
             w_ff2=w_ff2, ln2_g=ln2_g, ln2_b=ln2_b)
    m = dict(ln_in_g=m_ln_in_g, ln_in_b=m_ln_in_b, w_in=m_w_in, b_forget=m_b_forget, w_uq=m_w_uq, g_cq=m_g_cq,
             w_ukv=m_w_ukv, g_ckv=m_g_ckv, w_mem_kv=m_w_mem_kv, w_br=m_w_br, w_out=m_w_out, ln1_g=m_ln1_g,
             ln1_b=m_ln1_b, w_ff1=m_w_ff1, w_ff2=m_w_ff2, ln2_g=m_ln2_g, ln2_b=m_ln2_b)
    v = dict(ln_in_g=v_ln_in_g, ln_in_b=v_ln_in_b, w_in=v_w_in, b_forget=v_b_forget, w_uq=v_w_uq, g_cq=v_g_cq,
             w_ukv=v_w_ukv, g_ckv=v_g_ckv, w_mem_kv=v_w_mem_kv, w_br=v_w_br, w_out=v_w_out, ln1_g=v_ln1_g,
             ln1_b=v_ln1_b, w_ff1=v_w_ff1, w_ff2=v_w_ff2, ln2_g=v_ln2_g, ln2_b=v_ln2_b)
    return _step(Cfg(), x, mem, positions, loss_target, w, m, v)
```

```python
import functools
from typing import NamedTuple

import jax
import jax.numpy as jnp
from jax import lax
from jax.experimental import pallas as pl
from jax.experimental.pallas import tpu as pltpu

F32 = jnp.float32
BF16 = jnp.bfloat16
MESH = pl.DeviceIdType.MESH

LANES = 128
SUBLANES = 8
VMEM_BYTES = 64 * 1024 * 1024
N_CHIPS = 4
N_DEV = 8

FOX_DH = 64
MLA_NOPE = 64
MLA_ROPE = 32
MLA_V = 64
MEM_DH = 128
ROPE_BASE = 10000.0
LN_EPS = 1e-5
RMS_EPS = 1e-6
NEG_INF = -1e30
LOG2E = 1.4426950408889634
ATTN_TILE = 512
MEM_ATTN_TILE = 1024

ADAM_LR = 0.001
ADAM_B1 = 0.9
ADAM_B2 = 0.999
ADAM_EPS = 1e-08
ADAM_WD = 0.01
ADAM_STEP = 10


class Cfg(NamedTuple):
    d: int = 1024
    depth: int = 4
    seq: int = 2048
    chunk: int = 64
    n_mem: int = 256
    fox_h: int = 8
    mla_h: int = 8
    q_rank: int = 384
    kv_rank: int = 256
    mem_h: int = 4
    d_ff: int = 4096

    @property
    def width(self):
        return self.fox_h * FOX_DH

    @property
    def alpha(self):
        return (2 * self.depth) ** 0.25

    @property
    def small_w(self):
        return LANES + self.q_rank + self.kv_rank + LANES

    @property
    def in_splits(self):
        return (3 * self.width, self.fox_h, self.q_rank, self.kv_rank, MLA_ROPE, self.width, 3 * self.d)


class Exchange(NamedTuple):
    ins: tuple
    out_shapes: tuple
    plan: object
    n_copies: int
    aliases: dict = {}


def _peer(rel):
    x, y, c = lax.axis_index("x"), lax.axis_index("y"), lax.axis_index("c")
    if rel == "c":
        return (x, y, 1 - c)
    return ((1 - x) if rel in (1, 3) else x, (1 - y) if rel in (2, 3) else y, c)


def _exchange_copies(ex, in_refs, out_refs, send_sems, recv_sems):
    planned = ex.plan(in_refs, out_refs)
    assert len(planned) == ex.n_copies, len(planned)
    return [pltpu.make_async_remote_copy(src_ref=src, dst_ref=dst, send_sem=send_sems.at[i], recv_sem=recv_sems.at[i],
                                         device_id=_peer(rel), device_id_type=MESH)
            for i, (src, dst, rel) in enumerate(planned)]


def _pcall(body, rider=None, **kw):
    if rider is not None:
        n_in, n_out, grid = len(kw["in_specs"]), len(kw["out_specs"]), kw["grid"]
        n_rin, n_rout = len(rider.ins), len(rider.out_shapes)
        host = body

        def body(*refs):
            ins, rins = refs[:n_in], refs[n_in:n_in + n_rin]
            outs = refs[n_in + n_rin:n_in + n_rin + n_out]
            routs = refs[n_in + n_rin + n_out:n_in + n_rin + n_out + n_rout]
            scratch = refs[n_in + n_rin + n_out + n_rout:-2]
            copies = _exchange_copies(rider, rins, routs, refs[-2], refs[-1])
            first = functools.reduce(jnp.logical_and, [pl.program_id(a) == 0 for a in range(len(grid))])
            last = functools.reduce(jnp.logical_and, [pl.program_id(a) == n - 1 for a, n in enumerate(grid)])

            @pl.when(first)
            def _():
                for cp in copies:
                    cp.start()

            host(*ins, *outs, *scratch)

            @pl.when(last)
            def _():
                for cp in copies:
                    cp.wait()

        any_spec = pl.BlockSpec(memory_space=pl.ANY)
        sems = [pltpu.SemaphoreType.DMA((rider.n_copies,))] * 2
        kw = dict(
            kw,
            in_specs=list(kw["in_specs"]) + [any_spec] * n_rin,
            out_specs=list(kw["out_specs"]) + [any_spec] * n_rout,
            out_shape=list(kw["out_shape"]) + list(rider.out_shapes),
            scratch_shapes=list(kw.get("scratch_shapes", ())) + sems,
            input_output_aliases={**kw.get("input_output_aliases", {}),
                                  **{n_in + i: n_out + o for i, o in rider.aliases.items()}},
        )
    call = pl.pallas_call(body, **kw)
    return lambda *ops: call(*[pltpu.with_memory_space_constraint(o, pltpu.HBM) for o in ops])


class Hosts:
    def rider(self, host):
        return None

    def done(self, host, outs):
        pass

    def early_grads(self, grads):
        pass


def _merge_exchanges(exs):
    n_ins = [len(e.ins) for e in exs]
    n_outs = [len(e.out_shapes) for e in exs]

    def plan(in_refs, out_refs):
        copies, i, o = [], 0, 0
        for e, ni, no in zip(exs, n_ins, n_outs):
            copies += e.plan(in_refs[i:i + ni], out_refs[o:o + no])
            i, o = i + ni, o + no
        return copies

    aliases, i, o = {}, 0, 0
    for e, ni, no in zip(exs, n_ins, n_outs):
        aliases.update({i + a: o + b for a, b in e.aliases.items()})
        i, o = i + ni, o + no
    return Exchange(sum((e.ins for e in exs), ()), sum((e.out_shapes for e in exs), ()), plan,
                    sum(e.n_copies for e in exs), aliases)


class Together(Hosts):
    def __init__(self, members):
        self.members, self.active = list(members), []

    def rider(self, host):
        self.active = [(m, r) for m, r in ((m, m.rider(host)) for m in self.members) if r is not None]
        if not self.active:
            return None
        return _merge_exchanges([r for _, r in self.active])

    def done(self, host, outs):
        for m, r in self.active:
            m.done(host, outs[:len(r.out_shapes)])
            outs = outs[len(r.out_shapes):]


def _hosted(hosts, host, call, ops, n_results):
    rider = hosts.rider(host) if hosts is not None else None
    if rider is None:
        return call(None)(*ops)
    res = call(rider)(*ops, *rider.ins)
    hosts.done(host, list(res[n_results:]))
    return res[:n_results]


def _nbytes(shape, dtype):
    n = 1
    for s in shape:
        n *= s
    return n * jnp.dtype(dtype).itemsize


def _tile(dim, target):
    if dim <= target:
        return dim
    t = target - target % LANES
    while t >= LANES:
        if dim % t == 0:
            return t
        t -= LANES
    return dim


def _params(block_bytes, scratch_bytes=0):
    est = 2 * block_bytes + scratch_bytes + 24 * 1024 * 1024
    return pltpu.CompilerParams(vmem_limit_bytes=int(min(max(est, 32 * 1024 * 1024), VMEM_BYTES - 4 * 1024 * 1024)))


_DIMS = {"nn": (((1,), (0,)), ((), ())), "nt": (((1,), (1,)), ((), ())), "tn": (((0,), (0,)), ((), ()))}


MM_TILE = 1024
MM_BLOCK_BYTES = 16 * 1024 * 1024


def _mm_tiles(mode, pairs, out_dtypes, m, n, tm, tn):
    fixed_m, fixed_n = tm is not None, tn is not None
    tm, tn = _tile(m, tm or MM_TILE), _tile(n, tn or MM_TILE)

    def block_bytes(tm_, tn_):
        total = sum(_nbytes((tm_, tn_), dt) for dt in out_dtypes)
        for a, b in pairs:
            k = a.shape[0] if mode == "tn" else a.shape[1]
            total += _nbytes((k, tm_), a.dtype) + _nbytes((k, tn_), b.dtype)
        return total

    while block_bytes(tm, tn) > MM_BLOCK_BYTES:
        if not fixed_m and tm >= tn and tm > 2 * LANES:
            tm = _tile(m, tm // 2)
        elif not fixed_n and tn > 2 * LANES:
            tn = _tile(n, tn // 2)
        elif not fixed_m and tm > 2 * LANES:
            tm = _tile(m, tm // 2)
        else:
            break
    return tm, tn


def _mm(name, mode, pairs, out_dtypes, tm=None, tn=None, epi=None, row_extras=(), bc_extras=(), hosts=None, host=None):
    a0, b0 = pairs[0]
    m = a0.shape[1] if mode == "tn" else a0.shape[0]
    n = b0.shape[0] if mode == "nt" else b0.shape[1]
    tm, tn = _mm_tiles(mode, pairs, out_dtypes, m, n, tm, tn)
    in_specs, ops, blk = [], [], 0
    for a, b in pairs:
        if mode == "tn":
            k = a.shape[0]
            sa, sha = pl.BlockSpec((k, tm), lambda i, j: (0, i)), (k, tm)
        else:
            k = a.shape[1]
            sa, sha = pl.BlockSpec((tm, k), lambda i, j: (i, 0)), (tm, k)
        if mode == "nt":
            sb, shb = pl.BlockSpec((tn, k), lambda i, j: (j, 0)), (tn, k)
        else:
            sb, shb = pl.BlockSpec((k, tn), lambda i, j: (0, j)), (k, tn)
        in_specs += [sa, sb]
        ops += [a, b]
        blk += _nbytes(sha, a.dtype) + _nbytes(shb, b.dtype)
    for e in row_extras:
        w = e.shape[1]
        if w == n:
            in_specs.append(pl.BlockSpec((tm, tn), lambda i, j: (i, j)))
            blk += _nbytes((tm, tn), e.dtype)
        else:
            in_specs.append(pl.BlockSpec((tm, w), lambda i, j: (i, 0)))
            blk += _nbytes((tm, w), e.dtype)
        ops.append(e)
    for e in bc_extras:
        r, w = e.shape
        if w == n:
            in_specs.append(pl.BlockSpec((r, tn), lambda i, j: (0, j)))
        else:
            in_specs.append(pl.BlockSpec((r, w), lambda i, j: (0, 0)))
        blk += _nbytes((r, w), e.dtype)
        ops.append(e)
    npairs, nrow, nbc, nout = len(pairs), len(row_extras), len(bc_extras), len(out_dtypes)
    dims = _DIMS[mode]

    def body(*refs):
        acc = None
        for p in range(npairs):
            a = refs[2 * p][...].astype(BF16)
            b = refs[2 * p + 1][...].astype(BF16)
            d = lax.dot_general(a, b, dims, preferred_element_type=F32)
            acc = d if acc is None else acc + d
        ex = [r[...] for r in refs[2 * npairs:2 * npairs + nrow + nbc]]
        outs = (acc,) if epi is None else epi(acc, *ex)
        for o_ref, o in zip(refs[2 * npairs + nrow + nbc:], outs):
            o_ref[...] = o.astype(o_ref.dtype)

    blk += sum(_nbytes((tm, tn), dt) for dt in out_dtypes) + 2 * _nbytes((tm, tn), F32)
    call = lambda rider: _pcall(
        body,
        rider=rider,
        name=name,
        grid=(m // tm, n // tn),
        in_specs=in_specs,
        out_specs=[pl.BlockSpec((tm, tn), lambda i, j: (i, j)) for _ in range(nout)],
        out_shape=[jax.ShapeDtypeStruct((m, n), dt) for dt in out_dtypes],
        compiler_params=_params(blk),
    )
    res = _hosted(hosts, host, call, ops, nout)
    return res[0] if nout == 1 else res


def _rowwise(name, fn, row_ins, bc_ins, outs, accs=(), tm=256):
    rows = row_ins[0].shape[0]
    tm = min(tm, rows)
    assert rows % tm == 0
    nrow, nbc, nout, nacc = len(row_ins), len(bc_ins), len(outs), len(accs)
    in_specs = [pl.BlockSpec((tm, a.shape[1]), lambda i: (i, 0)) for a in row_ins]
    in_specs += [pl.BlockSpec(a.shape, lambda i: (0, 0)) for a in bc_ins]
    out_specs = [pl.BlockSpec((tm, w), lambda i: (i, 0)) for w, _ in outs]
    out_specs += [pl.BlockSpec(s, lambda i: (0, 0)) for s in accs]
    out_shape = [jax.ShapeDtypeStruct((rows, w), dt) for w, dt in outs]
    out_shape += [jax.ShapeDtypeStruct(s, F32) for s in accs]

    def body(*refs):
        vals = fn(*[r[...] for r in refs[:nrow + nbc]])
        o_refs = refs[nrow + nbc:]
        for r, v in zip(o_refs[:nout], vals[:nout]):
            r[...] = v.astype(r.dtype)
        if nacc:
            @pl.when(pl.program_id(0) == 0)
            def _():
                for r in o_refs[nout:]:
                    r[...] = jnp.zeros(r.shape, F32)

            for r, v in zip(o_refs[nout:], vals[nout:]):
                r[...] += v

    blk = sum(_nbytes((tm, a.shape[1]), a.dtype) for a in row_ins) + sum(_nbytes(a.shape, a.dtype) for a in bc_ins)
    blk += sum(_nbytes((tm, w), dt) for w, dt in outs) + sum(_nbytes(s, F32) for s in accs)
    res = _pcall(
        body,
        name=name,
        grid=(rows // tm,),
        in_specs=in_specs,
        out_specs=out_specs,
        out_shape=out_shape,
        compiler_params=_params(2 * blk),
    )(*row_ins, *bc_ins)
    return res


def _ln(z, g, b):
    mu = jnp.mean(z, axis=-1, keepdims=True)
    zc = z - mu
    var = jnp.mean(zc * zc, axis=-1, keepdims=True)
    return zc * lax.rsqrt(var + LN_EPS) * g + b


def _rms(x, g):
    return x * lax.rsqrt(jnp.mean(x * x, axis=-1, keepdims=True) + RMS_EPS) * g


def _colsum(v):
    return jnp.sum(v, axis=0, keepdims=True)


def _rope_swap(x):
    w = x.shape[1]
    lane = lax.broadcasted_iota(jnp.int32, (1, w), 1) % LANES
    from_left = pltpu.roll(x, 16, 1)
    from_right = pltpu.roll(x, w - 16, 1)
    lo = (lane >= MLA_NOPE) & (lane < MLA_NOPE + 16)
    hi = (lane >= MLA_NOPE + 16) & (lane < MLA_NOPE + 32)
    return jnp.where(hi, from_left, jnp.where(lo, from_right, 0.0))


def _rope(x, cos_t, sin_t):
    nh = x.shape[1] // LANES
    ct, st = jnp.tile(cos_t, (1, nh)), jnp.tile(sin_t, (1, nh))
    return x * ct + _rope_swap(x) * st


def _rope_t(dy, cos_t, sin_t):
    nh = dy.shape[1] // LANES
    ct, st = jnp.tile(cos_t, (1, nh)), jnp.tile(sin_t, (1, nh))
    return dy * ct + _rope_swap(dy * st)


def _block_cumsum(v, carry, reverse):
    tb = v.shape[0]
    r = lax.broadcasted_iota(jnp.int32, (tb, tb), 0)
    c = lax.broadcasted_iota(jnp.int32, (tb, tb), 1)
    tri = jnp.where((c >= r) if reverse else (c <= r), 1.0, 0.0).astype(BF16)
    hi = v.astype(BF16)
    r1 = v - hi.astype(F32)
    mid = r1.astype(BF16)
    lo = (r1 - mid.astype(F32)).astype(BF16)
    out = carry + sum(jnp.dot(tri, p, preferred_element_type=F32) for p in (hi, mid, lo))
    return out, (out[0:1, :] if reverse else out[tb - 1:tb, :])


def _forget_cumsum(name, logf, batch, seq, heads, tb):
    nb = seq // tb

    def body(x_ref, keys_ref, carry):
        @pl.when(pl.program_id(1) == 0)
        def _():
            carry[...] = jnp.zeros(carry.shape, F32)

        out, carry[...] = _block_cumsum(x_ref[...], carry[...], False)
        out = out * LOG2E
        keys_ref[...] = jnp.concatenate([jnp.broadcast_to(out[:, h:h + 1], (tb, LANES)) for h in range(heads)], axis=1)

    return _pcall(
        body,
        name=name,
        grid=(batch, nb),
        in_specs=[pl.BlockSpec((tb, LANES), lambda b, i: (b * nb + i, 0))],
        out_specs=pl.BlockSpec((tb, heads * LANES), lambda b, i: (b * nb + i, 0)),
        out_shape=jax.ShapeDtypeStruct((batch * seq, heads * LANES), F32),
        scratch_shapes=[pltpu.VMEM((1, LANES), F32)],
        compiler_params=_params(4 * tb * (heads + 2) * LANES * 4),
    )(logf)


def _forget_cumsum_bwd(name, dcf, batch, seq, heads, tb):
    nb = seq // tb

    def body(x_ref, o_ref, carry):
        @pl.when(pl.program_id(1) == 0)
        def _():
            carry[...] = jnp.zeros(carry.shape, F32)

        lane = lax.broadcasted_iota(jnp.int32, (1, LANES), 1)
        v = jnp.zeros((tb, LANES), F32)
        for g in range(heads // 2):
            blk = x_ref[:, g * LANES:(g + 1) * LANES]
            moved = pltpu.roll(blk, 2 * g, 1) if g else blk
            v = v + jnp.where((lane >= 2 * g) & (lane < 2 * g + 2), moved, 0.0)
        o_ref[...], carry[...] = _block_cumsum(v, carry[...], True)

    return _pcall(
        body,
        name=name,
        grid=(batch, nb),
        in_specs=[pl.BlockSpec((tb, (heads // 2) * LANES), lambda b, i: (b * nb + nb - 1 - i, 0))],
        out_specs=pl.BlockSpec((tb, LANES), lambda b, i: (b * nb + nb - 1 - i, 0)),
        out_shape=jax.ShapeDtypeStruct((batch * seq, LANES), F32),
        scratch_shapes=[pltpu.VMEM((1, LANES), F32)],
        compiler_params=_params(4 * tb * (heads // 2 + 1) * LANES * 4),
    )(dcf)


class Attn(NamedTuple):
    batch: int
    sq: int
    sk: int
    groups: int
    hq: int
    hv: int
    mode: str
    scale: float
    chunk: int
    tq: int
    tk: int

    @property
    def hg(self):
        return self.hv

    @property
    def qw(self):
        return LANES * self.hg // self.hq

    @property
    def dv(self):
        return LANES // self.hv


def _head_lanes(j, dv):
    lane = lax.broadcasted_iota(jnp.int32, (1, LANES), 1)
    return (lane >= j * dv) & (lane < (j + 1) * dv)


def _head_q(sp, j, q_blk):
    if sp.hq == 2:
        return jnp.where(_head_lanes(j, FOX_DH), q_blk, jnp.zeros_like(q_blk))
    return q_blk[:, LANES * j:LANES * (j + 1)]


def _head_rows(sp, j):
    return slice(j * sp.dv, (j + 1) * sp.dv) if sp.hg == 2 else slice(None)


def _scores_t(sp, j, k_c, q_j, cfk_rep, k0, q0, masked):
    tk, tq = k_c.shape[0], q_j.shape[0]
    k_j = k_c if sp.hq == 2 else k_c[:, LANES * j:LANES * (j + 1)]
    st = lax.dot_general(k_j, q_j, _DIMS["nt"], preferred_element_type=F32) * (sp.scale * LOG2E)
    if sp.mode == "fox":
        st = st - jnp.tile(cfk_rep[:, LANES * j:LANES * (j + 1)], (1, tq // LANES))
    if masked:
        kidx = k0 + lax.broadcasted_iota(jnp.int32, (tk, tq), 0)
        qidx = q0 + lax.broadcasted_iota(jnp.int32, (tk, tq), 1)
        if sp.mode == "chunk":
            shift = sp.chunk.bit_length() - 1
            kidx, qidx = jnp.right_shift(kidx, shift), jnp.right_shift(qidx, shift)
        st = jnp.where(kidx <= qidx, st, NEG_INF)
    return st


def _attn_fwd(name, sp, q, k, v, cfk=None, hosts=None, host=None):
    (qa, qo), (ka, ko), (va, vo) = q, k, v
    tq, tk, hg, qw = sp.tq, sp.tk, sp.hg, sp.qw
    nqb, nkc = sp.sq // tq, sp.sk // tk
    fox, causal = sp.mode == "fox", sp.mode != "none"
    assert sp.sq % tq == 0 and sp.sk % tk == 0 and (not causal or (tq == tk and sp.sq == sp.sk))

    def body(*refs):
        if fox:
            q_ref, k_ref, v_ref, cfk_ref, o_ref, lse_ref, acc_scr = refs
        else:
            q_ref, k_ref, v_ref, o_ref, lse_ref, acc_scr = refs
        i = pl.program_id(2)
        q0 = i * tq
        q_blk = q_ref[...]
        qs = [_head_q(sp, j, q_blk) for j in range(hg)]
        acc_scr[...] = jnp.zeros(acc_scr.shape, F32)

        def chunk(kc, carry, masked):
            ms, ls = carry
            k0 = pl.multiple_of(kc * tk, tk)
            k_c = k_ref[pl.ds(k0, tk), :]
            v_c = v_ref[pl.ds(k0, tk), :]
            new_m, new_l = [], []
            for j in range(hg):
                st = _scores_t(sp, j, k_c, qs[j], cfk_ref[pl.ds(k0, tk), :] if fox else None, k0, q0, masked)
                m_new = jnp.maximum(ms[j], jnp.max(st, axis=0, keepdims=True))
                alpha = jnp.exp2(ms[j] - m_new)
                pt = jnp.exp2(st - m_new)
                new_m.append(m_new)
                new_l.append(alpha * ls[j] + jnp.sum(pt, axis=0, keepdims=True))
                pv = lax.dot_general(v_c, pt.astype(BF16), _DIMS["tn"], preferred_element_type=F32)
                r = _head_rows(sp, j)
                acc_scr[r, :] = acc_scr[r, :] * alpha + pv[r, :]
            return tuple(new_m), tuple(new_l)

        carry = (tuple(jnp.full((1, tq), NEG_INF, F32) for _ in range(hg)),
                 tuple(jnp.zeros((1, tq), F32) for _ in range(hg)))
        if causal:
            carry = lax.fori_loop(0, i, functools.partial(chunk, masked=False), carry)
            ms, ls = chunk(i, carry, True)
        else:
            ms, ls = lax.fori_loop(0, nkc, functools.partial(chunk, masked=False), carry)
        for j in range(hg):
            r = _head_rows(sp, j)
            acc_scr[r, :] = acc_scr[r, :] / ls[j]
            lse_ref[j:j + 1, :] = ms[j] + jnp.log(ls[j]) * LOG2E
        o_ref[...] = acc_scr[...].T

    in_specs = [
        pl.BlockSpec((tq, qw), lambda b, g, i: (b * nqb + i, qo + g)),
        pl.BlockSpec((sp.sk, qw), lambda b, g, i: (b, ko + g)),
        pl.BlockSpec((sp.sk, LANES), lambda b, g, i: (b, vo + g)),
    ]
    ops = [qa, ka, va]
    stat_blk = pl.BlockSpec((None, None, None, hg, tq), lambda b, g, i: (b, g, i, 0, 0))
    if fox:
        in_specs.append(pl.BlockSpec((sp.sk, hg * LANES), lambda b, g, i: (b, g)))
        ops.append(cfk)
    blk = _nbytes((tq, qw), BF16) + _nbytes((sp.sk, qw + LANES), BF16) + 2 * _nbytes((tq, LANES), F32)
    blk += _nbytes((sp.sk, hg * LANES), F32) + 6 * _nbytes((tk, tq), F32)
    call = lambda rider: _pcall(
        body,
        rider=rider,
        name=name,
        grid=(sp.batch, sp.groups, nqb),
        in_specs=in_specs,
        out_specs=[pl.BlockSpec((tq, LANES), lambda b, g, i: (b * nqb + i, g)), stat_blk],
        out_shape=[
            jax.ShapeDtypeStruct((sp.batch * sp.sq, sp.groups * LANES), F32),
            jax.ShapeDtypeStruct((sp.batch, sp.groups, nqb, hg, tq), F32),
        ],
        scratch_shapes=[pltpu.VMEM((LANES, tq), F32)],
        compiler_params=_params(blk, tq * LANES * 4),
    )
    return _hosted(hosts, host, call, ops, 2)


def _attn_bwd(name, sp, q, k, v, o, lse, do, cfk=None, hosts=None, host=None):
    (qa, qo), (ka, ko), (va, vo) = q, k, v
    tq, tk, hg, qw, dv = sp.tq, sp.tk, sp.hg, sp.qw, sp.dv
    nqb, nkb = sp.sq // tq, sp.sk // tk
    fox, causal = sp.mode == "fox", sp.mode != "none"
    assert sp.sq % tq == 0 and sp.sk % tk == 0 and (not causal or (tq == tk and sp.sq == sp.sk))

    def body(*refs):
        if fox:
            (q_ref, k_ref, v_ref, lse_ref, do_ref, cfk_ref, kall_ref, vall_ref, cfkall_ref,
             dq_ref, dk_ref, dv_ref, dcf_ref, delta_scr, dk_scr, dv_scr, dqt_scr, dcf_scr) = refs
        else:
            (q_ref, k_ref, v_ref, o_ref, lse_ref, do_ref,
             dq_ref, dk_ref, dv_ref, delta_scr, dk_scr, dv_scr, dqt_scr) = refs
        kb = pl.program_id(2)
        k0 = kb * tk
        heads = [_head_lanes(j, dv) for j in range(hg)]

        def head_do(j, do_c):
            return jnp.where(heads[j], do_c, jnp.zeros_like(do_c)) if hg == 2 else do_c

        def probs_t(j, k_c, v_c, q_c, do_c, i, cf_keys, c0, masked):
            st = _scores_t(sp, j, k_c, _head_q(sp, j, q_c), cf_keys, c0, i * tq, masked)
            pt = jnp.exp2(st - lse_ref[i][j:j + 1, :])
            dpt = lax.dot_general(v_c, head_do(j, do_c), _DIMS["nt"], preferred_element_type=F32)
            return pt, dpt

        @pl.when(kb == 0)
        def _():
            dqt_scr[...] = jnp.zeros(dqt_scr.shape, F32)

            def fill(i, carry):
                r0 = pl.multiple_of(i * tq, tq)
                do_c = do_ref[pl.ds(r0, tq), :]
                if fox:
                    q_c = q_ref[pl.ds(r0, tq), :]

                    def keys(kc, acc, masked):
                        c0 = pl.multiple_of(kc * tk, tk)
                        out = []
                        for j in range(hg):
                            pt, dpt = probs_t(j, kall_ref[pl.ds(c0, tk), :], vall_ref[pl.ds(c0, tk), :], q_c, do_c, i,
                                              cfkall_ref[pl.ds(c0, tk), :], c0, masked)
                            out.append(acc[j] + jnp.sum(pt * dpt, axis=0, keepdims=True))
                        return tuple(out)

                    d = lax.fori_loop(0, i, functools.partial(keys, masked=False),
                                      tuple(jnp.zeros((1, tq), F32) for _ in range(hg)))
                    d = keys(i, d, True)
                    for j in range(hg):
                        delta_scr[i, j:j + 1, :] = d[j]
                else:
                    prod_t = (do_c.astype(F32) * o_ref[pl.ds(r0, tq), :]).T
                    for j in range(hg):
                        delta_scr[i, j:j + 1, :] = jnp.sum(prod_t[_head_rows(sp, j), :], axis=0, keepdims=True)
                return carry

            lax.fori_loop(0, nqb, fill, 0)

        k_blk = k_ref[...]
        v_blk = v_ref[...]
        k_t = k_blk.astype(F32).T.astype(BF16)
        dk_scr[...] = jnp.zeros(dk_scr.shape, F32)
        dv_scr[...] = jnp.zeros(dv_scr.shape, F32)
        if fox:
            dcf_scr[...] = jnp.zeros(dcf_scr.shape, F32)

        def qblock(i, carry, masked):
            r0 = pl.multiple_of(i * tq, tq)
            q_c = q_ref[pl.ds(r0, tq), :]
            do_c = do_ref[pl.ds(r0, tq), :]
            for j in range(hg):
                pt, dpt = probs_t(j, k_blk, v_blk, q_c, do_c, i, cfk_ref[...] if fox else None, k0, masked)
                dst = pt * (dpt - delta_scr[i][j:j + 1, :])
                if fox:
                    part = dst[:, :LANES]
                    for t in range(1, tq // LANES):
                        part = part + dst[:, t * LANES:(t + 1) * LANES]
                    dcf_scr[j] += part
                ds_b = (dst * sp.scale).astype(BF16)
                dv_scr[j] += jnp.dot(pt.astype(BF16), do_c, preferred_element_type=F32)
                dk_scr[j] += jnp.dot(ds_b, q_c if sp.hq == 2 else _head_q(sp, j, q_c), preferred_element_type=F32)
                if sp.hq == 2:
                    r = pl.ds(j * FOX_DH, FOX_DH)
                    dqt_scr[i, r, :] += jnp.dot(k_t[j * FOX_DH:(j + 1) * FOX_DH, :], ds_b, preferred_element_type=F32)
                else:
                    r = pl.ds(j * LANES, LANES)
                    dqt_scr[i, r, :] += jnp.dot(k_t[j * LANES:(j + 1) * LANES, :], ds_b, preferred_element_type=F32)
            return carry

        if causal:
            qblock(kb, 0, True)
            lax.fori_loop(kb + 1, nqb, functools.partial(qblock, masked=False), 0)
        else:
            lax.fori_loop(0, nqb, functools.partial(qblock, masked=False), 0)

        @pl.when(kb == nkb - 1)
        def _():
            def untranspose(i, carry):
                dq_ref[pl.ds(pl.multiple_of(i * tq, tq), tq), :] = dqt_scr[i].T
                return carry

            lax.fori_loop(0, nqb, untranspose, 0)

        if hg == 2:
            dv_ref[...] = jnp.where(heads[0], dv_scr[0], dv_scr[1]).astype(dv_ref.dtype)
        else:
            dv_ref[...] = dv_scr[0].astype(dv_ref.dtype)
        if sp.hq == 2:
            dk_ref[...] = jnp.where(_head_lanes(0, FOX_DH), dk_scr[0], dk_scr[1]).astype(dk_ref.dtype)
        elif hg == 2:
            dk_ref[...] = jnp.concatenate([dk_scr[0], dk_scr[1]], axis=1).astype(dk_ref.dtype)
        else:
            dk_ref[...] = dk_scr[0].astype(dk_ref.dtype)
        if fox:
            lane = lax.broadcasted_iota(jnp.int32, (1, LANES), 1)
            sums = [jnp.sum(dcf_scr[j], axis=1, keepdims=True) for j in range(hg)]
            dcf_ref[...] = jnp.where(lane == 0, -sums[0], jnp.where(lane == 1, -sums[1], 0.0))

    seq_lanes = lambda b, g, kb: (b, g)
    key_blk = lambda b, g, kb: (b * nkb + kb, g)
    stats = pl.BlockSpec((None, None, nqb, hg, tq), lambda b, g, kb: (b, g, 0, 0, 0))
    in_specs = [
        pl.BlockSpec((sp.sq, qw), lambda b, g, kb: (b, qo + g)),
        pl.BlockSpec((tk, qw), lambda b, g, kb: (b * nkb + kb, ko + g)),
        pl.BlockSpec((tk, LANES), lambda b, g, kb: (b * nkb + kb, vo + g)),
    ]
    ops = [qa, ka, va]
    if not fox:
        in_specs.append(pl.BlockSpec((sp.sq, LANES), seq_lanes))
        ops.append(o)
    in_specs += [stats, pl.BlockSpec((sp.sq, LANES), seq_lanes)]
    ops += [lse, do]
    out_specs = [pl.BlockSpec((sp.sq, qw), seq_lanes), pl.BlockSpec((tk, qw), key_blk), pl.BlockSpec((tk, LANES), key_blk)]
    out_shape = [
        jax.ShapeDtypeStruct((sp.batch * sp.sq, sp.groups * qw), F32),
        jax.ShapeDtypeStruct((sp.batch * sp.sk, sp.groups * qw), BF16),
        jax.ShapeDtypeStruct((sp.batch * sp.sk, sp.groups * LANES), BF16),
    ]
    scratch = [pltpu.VMEM((nqb, hg, tq), F32), pltpu.VMEM((hg, tk, LANES), F32), pltpu.VMEM((hg, tk, LANES), F32),
               pltpu.VMEM((nqb, qw, tq), F32)]
    if fox:
        in_specs += [
            pl.BlockSpec((tk, hg * LANES), key_blk),
            pl.BlockSpec((sp.sk, qw), lambda b, g, kb: (b, ko + g)),
            pl.BlockSpec((sp.sk, LANES), lambda b, g, kb: (b, vo + g)),
            pl.BlockSpec((sp.sk, hg * LANES), seq_lanes),
        ]
        ops += [cfk, ka, va, cfk]
        out_specs.append(pl.BlockSpec((tk, LANES), key_blk))
        out_shape.append(jax.ShapeDtypeStruct((sp.batch * sp.sk, sp.groups * LANES), F32))
        scratch.append(pltpu.VMEM((hg, tk, LANES), F32))
    blk = _nbytes((sp.sq, qw), BF16) + _nbytes((sp.sq, LANES), BF16) + 2 * _nbytes((sp.sq, LANES), F32)
    blk += _nbytes((sp.sq, qw), F32) + 4 * _nbytes((tk, qw), BF16) + 8 * _nbytes((tq, tk), F32)
    blk += (_nbytes((sp.sk, qw + LANES), BF16) + _nbytes((sp.sk, hg * LANES), F32)) if fox else 0
    call = lambda rider: _pcall(
        body,
        rider=rider,
        name=name,
        grid=(sp.batch, sp.groups, nkb),
        in_specs=in_specs,
        out_specs=out_specs,
        out_shape=out_shape,
        scratch_shapes=scratch,
        compiler_params=_params(blk, _nbytes((sp.sq, LANES), F32) + 4 * _nbytes((tk, LANES), F32)),
    )
    return _hosted(hosts, host, call, ops, len(out_shape))


def _slabwise(name, fn, ins, out_dtypes, rows_per_step=512):
    ins = [a if isinstance(a, tuple) else (a, None) for a in ins]
    n = max(1 if fixed is not None else a.shape[0] for a, fixed in ins)
    rows, cols = ins[0][0].shape[1:]
    tr = min(rows_per_step, rows)
    while rows % tr:
        tr //= 2
    assert tr % 16 == 0 or tr == rows, (name, rows, tr)

    def spec(a, fixed):
        if fixed is not None or a.shape[0] == 1:
            return pl.BlockSpec((None, tr, cols), lambda s, i: (fixed or 0, i, 0))
        return pl.BlockSpec((None, tr, cols), lambda s, i: (s, i, 0))

    def body(*refs):
        vals = fn(*[r[...] for r in refs[:len(ins)]])
        for r, v in zip(refs[len(ins):], vals):
            r[...] = v.astype(r.dtype)

    blk = (len(ins) + len(out_dtypes)) * _nbytes((tr, cols + LANES), F32)
    res = _pcall(
        body,
        name=name,
        grid=(n, rows // tr),
        in_specs=[spec(a, fixed) for a, fixed in ins],
        out_specs=[pl.BlockSpec((None, tr, cols), lambda s, i: (s, i, 0)) for _ in out_dtypes],
        out_shape=[jax.ShapeDtypeStruct((n, rows, cols), dt) for dt in out_dtypes],
        compiler_params=_params(2 * blk),
    )(*[a for a, _ in ins])
    return res


def _adamw_math(w, g, m, v):
    m = ADAM_B1 * m + (1.0 - ADAM_B1) * g
    v = ADAM_B2 * v + (1.0 - ADAM_B2) * jnp.square(g)
    m_hat = m / (1.0 - ADAM_B1 ** ADAM_STEP)
    v_hat = v / (1.0 - ADAM_B2 ** ADAM_STEP)
    delta = -ADAM_LR * (m_hat / (jnp.sqrt(v_hat) + ADAM_EPS) + ADAM_WD * w)
    return delta, m, v


def _run_exchange(name, ex):
    n_in, n_out = len(ex.ins), len(ex.out_shapes)

    def body(*refs):
        copies = _exchange_copies(ex, refs[:n_in], refs[n_in:n_in + n_out], refs[-2], refs[-1])
        for cp in copies:
            cp.start()
        for cp in copies:
            cp.wait()

    any_spec = pl.BlockSpec(memory_space=pl.ANY)
    return _pcall(
        body,
        name=name,
        in_specs=[any_spec] * n_in,
        out_specs=[any_spec] * n_out,
        out_shape=list(ex.out_shapes),
        scratch_shapes=[pltpu.SemaphoreType.DMA((ex.n_copies,))] * 2,
        input_output_aliases=dict(ex.aliases),
    )(*ex.ins)


def _chip(rel=0):
    x, y = lax.axis_index("x"), lax.axis_index("y")
    return 2 * ((1 - x) if rel & 1 else x) + ((1 - y) if rel & 2 else y)


def _gather_ici(shards):
    def plan(in_refs, out_refs):
        c = lax.axis_index("c")
        return [(s.at[c], g.at[_chip(), c], rel) for s, g in zip(in_refs, out_refs) for rel in (1, 2, 3)]

    shapes = tuple(jax.ShapeDtypeStruct((N_CHIPS,) + s.shape, s.dtype) for s in shards)
    return Exchange(tuple(shards), shapes, plan, 3 * len(shards))


def _gather_d2d(got):
    def plan(in_refs, out_refs):
        c = lax.axis_index("c")
        return [(g_in.at[_chip(rel), c], g_out.at[_chip(rel), c], "c")
                for g_in, g_out in zip(in_refs, out_refs) for rel in (1, 2, 3)]

    shapes = tuple(jax.ShapeDtypeStruct(g.shape, g.dtype) for g in got)
    return Exchange(tuple(got), shapes, plan, 3 * len(got), {i: i for i in range(len(got))})


class GatherSet(Hosts):
    def __init__(self, shards, hosts):
        self.shards, self.hosts, self.got = shards, hosts, None

    def rider(self, host):
        if host == self.hosts[0]:
            return _gather_ici(self.shards)
        if host == self.hosts[1]:
            return _gather_d2d(self.got)
        return None

    def done(self, host, outs):
        self.got = outs


def _gather_now(shards):
    got = _run_exchange("gather_weights_ici", _gather_ici(shards))
    return _run_exchange("gather_weights_d2d", _gather_d2d(got))


def _pair_sum(name, g, recv, rows_per_step=512):
    _, _, rows, cols = g.shape
    tr = min(rows_per_step, rows)
    while rows % tr:
        tr //= 2

    def body(g_ref, r_ref, p_ref, own_ref):
        mine = jnp.where(lax.axis_index("c") == 0, g_ref[0], g_ref[1])
        p = mine.astype(F32) + r_ref[...].astype(F32)
        p_ref[...] = p.astype(p_ref.dtype)

        @pl.when(pl.program_id(1) == _chip())
        def _():
            own_ref[...] = p

    slab = pl.BlockSpec((None, tr, cols), lambda i, s: (s, i, 0))
    return _pcall(
        body,
        name=name,
        grid=(rows // tr, N_CHIPS),
        in_specs=[pl.BlockSpec((None, 2, tr, cols), lambda i, s: (s, 0, i, 0)), slab],
        out_specs=[slab, pl.BlockSpec((None, tr, cols), lambda i, s: (0, i, 0))],
        out_shape=[jax.ShapeDtypeStruct((N_CHIPS, rows, cols), BF16), jax.ShapeDtypeStruct((1, rows, cols), F32)],
        compiler_params=_params(2 * 6 * _nbytes((tr, cols + LANES), F32)),
    )(g, recv)


class ReduceLayer(Hosts):
    def __init__(self, tag, grads, hosts):
        self.tag, self.grads, self.pair, self.own, self.total, self.theirs = tag, grads, None, None, None, None
        self.stage_of = dict(zip(hosts, ("swap", "chips", "share")))

    def _swap_halves(self):
        def plan(in_refs, out_refs):
            c = lax.axis_index("c")
            return [(g.at[pl.ds(0, N_CHIPS), 1 - c], r, "c") for g, r in zip(in_refs, out_refs)]

        shapes = tuple(jax.ShapeDtypeStruct((N_CHIPS,) + g.shape[2:], g.dtype) for g in self.grads)
        return Exchange(tuple(self.grads), shapes, plan, len(self.grads))

    def _to_chips(self):
        def plan(in_refs, out_refs):
            return [(p.at[_chip(rel)], r.at[rel - 1], rel) for p, r in zip(in_refs, out_refs) for rel in (1, 2, 3)]

        shapes = tuple(jax.ShapeDtypeStruct((3,) + p.shape[1:], p.dtype) for p in self.pair)
        return Exchange(tuple(self.pair), shapes, plan, 3 * len(self.pair))

    def _share(self):
        def plan(in_refs, out_refs):
            return [(t, r, "c") for t, r in zip(in_refs, out_refs)]

        shapes = tuple(jax.ShapeDtypeStruct(t.shape, F32) for t in self.total)
        return Exchange(tuple(self.total), shapes, plan, len(self.total))

    def rider(self, host):
        stages = {"swap": self._swap_halves, "chips": self._to_chips, "share": self._share}
        return stages[self.stage_of[host]]() if host in self.stage_of else None

    def done(self, host, outs):
        self._after(self.stage_of[host], outs)

    def _after(self, stage, outs):
        if stage == "swap":
            sums = [_pair_sum(f"reduce_pair_sum_{self.tag}_{i}", g, r) for i, (g, r) in enumerate(zip(self.grads, outs))]
            self.pair, self.own = [s[0] for s in sums], [s[1] for s in sums]
        elif stage == "chips":
            self.total = [
                _slabwise(f"reduce_chip_sum_{self.tag}_{i}",
                          lambda a, b, c_, d: (a + b.astype(F32) + c_.astype(F32) + d.astype(F32),),
                          [own, (r, 0), (r, 1), (r, 2)], [F32])[0]
                for i, (own, r) in enumerate(zip(self.own, outs))]
        else:
            self.theirs = outs

    def run_now(self):
        self._after("swap", _run_exchange(f"reduce_pair_{self.tag}", self._swap_halves()))
        self._after("chips", _run_exchange(f"reduce_chips_{self.tag}", self._to_chips()))
        self._after("share", _run_exchange(f"reduce_share_{self.tag}", self._share()))

    def result(self):
        return list(zip(self.total, self.theirs))


def _adamw_layer(name, l, w, m, v, mine, theirs, prev, rows_per_step=256):
    _, _, rows, cols = w.shape
    tr = min(rows_per_step, rows)
    while rows % tr:
        tr //= 2

    def body(w_ref, m_ref, v_ref, mine_ref, theirs_ref, *rest):
        g_ref, d_ref, nm_ref, nv_ref = rest[-4:]
        g = jnp.where(pl.program_id(0) == lax.axis_index("c"), mine_ref[...], theirs_ref[...])
        d, nm, nv = _adamw_math(w_ref[...], g, m_ref[...], v_ref[...])
        g_ref[...], d_ref[...], nm_ref[...], nv_ref[...] = g, d, nm, nv

    half = pl.BlockSpec((None, None, tr, cols), lambda h, i: (l, h, i, 0))
    one = pl.BlockSpec((None, tr, cols), lambda h, i: (0, i, 0))
    kept = [] if prev is None else list(prev)
    return _pcall(
        body,
        name=name,
        grid=(2, rows // tr),
        in_specs=[half, half, half, one, one] + [pl.BlockSpec(memory_space=pl.ANY)] * len(kept),
        out_specs=[half] * 4,
        out_shape=[jax.ShapeDtypeStruct(w.shape, F32)] * 4,
        input_output_aliases={5 + i: i for i in range(len(kept))},
        compiler_params=_params(2 * 9 * _nbytes((tr, cols + LANES), F32)),
    )(w, m, v, mine, theirs, *kept)


def _allreduce_small(v):
    rows = v.shape[0]

    def body(v_ref, sum_ref, all_ref, send_sems, recv_sems, local_sem):
        x, y, c = lax.axis_index("x"), lax.axis_index("y"), lax.axis_index("c")
        sibling = (x, y, 1 - c)
        chips = [(1 - x, y), (x, 1 - y), (1 - x, 1 - y)]

        def slab(px, py, pc):
            return all_ref.at[pl.ds((4 * px + 2 * py + pc) * rows, rows), :]

        def copy(k, block, to, src=None):
            return pltpu.make_async_remote_copy(
                src_ref=slab(*block) if src is None else src, dst_ref=slab(*block), send_sem=send_sems.at[k],
                recv_sem=recv_sems.at[k], device_id=to, device_id_type=MESH)

        mine = pltpu.make_async_copy(v_ref, slab(x, y, c), local_sem)
        mine.start()
        first = [copy(0, (x, y, c), sibling, src=v_ref)]
        first += [copy(1 + j, (x, y, c), (*chip, c), src=v_ref) for j, chip in enumerate(chips)]
        for cp in first:
            cp.start()
        passed = [copy(4 + j, (*chip, c), sibling) for j, chip in enumerate(chips)]
        for j, chip in enumerate(chips):
            copy(1 + j, (*chip, c), (x, y, c)).wait_recv()
            passed[j].start()
        copy(0, (x, y, 1 - c), (x, y, c)).wait_recv()
        for j, chip in enumerate(chips):
            copy(4 + j, (*chip, 1 - c), (x, y, c)).wait_recv()
        for cp in first + passed:
            cp.wait_send()
        mine.wait()
        total = all_ref[pl.ds(0, rows), :]
        for d in range(1, N_DEV):
            total = total + all_ref[pl.ds(d * rows, rows), :]
        sum_ref[...] = total

    vm = pl.BlockSpec(memory_space=pltpu.VMEM)
    return _pcall(
        body,
        name="allreduce_small",
        in_specs=[vm],
        out_specs=vm,
        out_shape=jax.ShapeDtypeStruct((rows, LANES), F32),
        scratch_shapes=[pltpu.VMEM((N_DEV * rows, LANES), F32), pltpu.SemaphoreType.DMA((7,)),
                        pltpu.SemaphoreType.DMA((7,)), pltpu.SemaphoreType.DMA],
    )(v)


def _pad_cols(a, before, total):
    return jnp.pad(a, ((0, 0), (before, total - before - a.shape[1])))


def _layer_weights(cfg, w_in, w_uq, w_ukv):
    w = cfg.width
    qkv, f, cq, ckv, kr, qm, gates = jnp.split(w_in, list(_cumsum(cfg.in_splits))[:-1], axis=1)
    wa = jnp.concatenate([qkv, qm], axis=1)
    ws = jnp.concatenate([_pad_cols(f, 0, LANES), cq, ckv, _pad_cols(kr, MLA_NOPE, LANES)], axis=1)
    wq = jnp.pad(w_uq.reshape(cfg.q_rank, cfg.mla_h, MLA_NOPE + MLA_ROPE), ((0, 0), (0, 0), (0, LANES - MLA_NOPE - MLA_ROPE)))
    wq = wq.reshape(cfg.q_rank, cfg.mla_h * LANES)
    kv = w_ukv.reshape(cfg.kv_rank, cfg.mla_h, MLA_NOPE + MLA_V)
    wk = jnp.pad(kv[:, :, :MLA_NOPE], ((0, 0), (0, 0), (0, LANES - MLA_NOPE))).reshape(cfg.kv_rank, cfg.mla_h * LANES)
    wv = kv[:, :, MLA_NOPE:].reshape(cfg.kv_rank, cfg.mla_h * MLA_V)
    del w
    return wa, gates, ws, wq, wk, wv


def _cumsum(xs):
    out, t = [], 0
    for v in xs:
        t += v
        out.append(t)
    return out


def _layer_weight_grads(cfg, dwa, dwg, dws, dwq, dwk, dwv):
    w, qr, kvr = cfg.width, cfg.q_rank, cfg.kv_rank
    off_kr = LANES + qr + kvr + MLA_NOPE
    dw_in = jnp.concatenate([
        dwa[:, :3 * w], dws[:, :cfg.fox_h], dws[:, LANES:LANES + qr], dws[:, LANES + qr:LANES + qr + kvr],
        dws[:, off_kr:off_kr + MLA_ROPE], dwa[:, 3 * w:], dwg], axis=1)
    dw_uq = dwq.reshape(qr, cfg.mla_h, LANES)[:, :, :MLA_NOPE + MLA_ROPE].reshape(qr, cfg.mla_h * (MLA_NOPE + MLA_ROPE))
    dw_ukv = jnp.concatenate([dwk.reshape(kvr, cfg.mla_h, LANES)[:, :, :MLA_NOPE], dwv.reshape(kvr, cfg.mla_h, MLA_V)],
                             axis=2).reshape(kvr, cfg.mla_h * (MLA_NOPE + MLA_V))
    return dw_in, dw_uq, dw_ukv


def _attn_specs(cfg, batch):
    t = min(ATTN_TILE, cfg.seq)
    common = dict(batch=batch, sq=cfg.seq, chunk=cfg.chunk, tq=t)
    fox = Attn(sk=cfg.seq, groups=cfg.fox_h // 2, hq=2, hv=2, mode="fox", scale=FOX_DH ** -0.5, tk=t, **common)
    mla = Attn(sk=cfg.seq, groups=cfg.mla_h // 2, hq=1, hv=2, mode="chunk",
               scale=(MLA_NOPE + MLA_ROPE) ** -0.5, tk=t, **common)
    mem = Attn(sk=cfg.n_mem, groups=cfg.mem_h, hq=1, hv=1, mode="none", scale=MEM_DH ** -0.5, tk=cfg.n_mem,
               **dict(common, tq=min(MEM_ATTN_TILE, cfg.seq)))
    return fox, mla, mem


def _small_core(cfg, ps, bf, gq, gkv):
    qr, kvr = cfg.q_rank, cfg.kv_rank
    z = ps[:, :LANES] + bf
    logf = jnp.minimum(z, 0.0) - jnp.log1p(jnp.exp(-jnp.abs(z)))
    nq = _rms(ps[:, LANES:LANES + qr], gq)
    nkv = _rms(ps[:, LANES + qr:LANES + qr + kvr], gkv)
    return logf, nq, nkv


def _layer_fwd(cfg, l, batch, h, hb, mem_b, rope_c, rope_s, hosts, lw, bf_pad, g_cq, g_ckv, ln1, ln2):
    (wa, wg, ws, wq, wk, wv, wmkv), later_weights = lw
    w, d = cfg.width, cfg.d
    fox, mla, mem = _attn_specs(cfg, batch)
    nw = w // LANES
    pa = _mm(f"proj_a_{l}", "nn", [(hb, wa)], [BF16])
    gl = _mm(f"proj_gates_{l}", "nn", [(hb, wg)], [F32])
    ps = _mm(f"proj_small_{l}", "nn", [(hb, ws)], [F32], tn=cfg.small_w)

    def small_fwd(ps_, c_, s_, bf_, gq_, gkv_):
        logf, nq, nkv = _small_core(cfg, ps_, bf_, gq_, gkv_)
        kpe = _rope(ps_[:, cfg.small_w - LANES:], c_, s_)
        return logf, nq, nkv, kpe

    logf, nq, nkv, kpe = _rowwise(
        f"small_fwd_{l}", small_fwd, [ps, rope_c, rope_s], [bf_pad, g_cq, g_ckv],
        [(LANES, F32), (cfg.q_rank, BF16), (cfg.kv_rank, BF16), (LANES, F32)])
    cfk = _forget_cumsum(f"cum_forget_{l}", logf, batch, cfg.seq, cfg.fox_h, fox.tq)

    qf = _mm(f"mla_q_{l}", "nn", [(nq, wq)], [BF16], tn=wq.shape[1],
             epi=lambda acc, c_, s_: (_rope(acc, c_, s_),), row_extras=[rope_c, rope_s])
    kf = _mm(f"mla_k_{l}", "nn", [(nkv, wk)], [BF16], tn=wk.shape[1],
             epi=lambda acc, kp: (acc + jnp.tile(kp, (1, cfg.mla_h)),), row_extras=[kpe])
    vb = _mm(f"mla_v_{l}", "nn", [(nkv, wv)], [BF16])
    mkv = _mm(f"mem_kv_{l}", "nn", [(mem_b, wmkv)], [BF16])

    o_a, lse_a = _attn_fwd(f"fox_fwd_{l}", fox, (pa, 0), (pa, nw), (pa, 2 * nw), cfk,
                           hosts=hosts, host="fox_fwd")
    o_b, lse_b = _attn_fwd(f"mla_fwd_{l}", mla, (qf, 0), (kf, 0), (vb, 0), hosts=hosts, host="mla_fwd")
    o_c, lse_c = _attn_fwd(f"mem_fwd_{l}", mem, (pa, 3 * nw), (mkv, 0), (mkv, nw), hosts=hosts, host="mem_fwd")
    wbr, wout, wff1, wff2 = later_weights()
    bps = [_mm(f"branch_{n}_{l}", "nn", [(o, wbr[n])], [F32]) for n, o in enumerate((o_a, o_b, o_c))]

    def merge(gl_, b0, b1, b2):
        g = jax.nn.sigmoid(gl_)
        return (g[:, :d] * b0 + g[:, d:2 * d] * b1 + g[:, 2 * d:] * b2,)

    (merged,) = _rowwise(f"merge_{l}", merge, [gl] + bps, [], [(d, BF16)])

    def post_ln(acc, res, g_, b_):
        z = cfg.alpha * res + acc
        y = _ln(z, g_, b_)
        return z, y, y

    z1, h1, h1b = _mm(f"out_ln1_{l}", "nn", [(merged, wout)], [F32, F32, BF16], tm=256, tn=d,
                      epi=post_ln, row_extras=[h], bc_extras=list(ln1))
    u, a = _mm(f"ff1_{l}", "nn", [(h1b, wff1)], [BF16, BF16],
               epi=lambda acc: (acc, jnp.square(jnp.maximum(acc, 0.0))))
    z2, h2, h2b = _mm(f"ff2_ln2_{l}", "nn", [(a, wff2)], [F32, F32, BF16], tm=256, tn=d,
                      epi=post_ln, row_extras=[h1], bc_extras=list(ln2))
    saved = dict(hb=hb, pa=pa, gl=gl, ps=ps, nq=nq, nkv=nkv, cfk=cfk, qf=qf, kf=kf, vb=vb, mkv=mkv,
                 o=(o_a, o_b, o_c), lse=(lse_a, lse_b, lse_c), bps=bps, merged=merged, z1=z1, h1b=h1b, u=u, a=a, z2=z2,
                 lw=(wa, wg, ws, wq, wk, wv, wmkv, wbr, wout, wff1, wff2))
    return h2, h2b, saved


def _ln_bwd(name, cfg, ga, gb, z, g, b):
    d = cfg.d

    def fn(*vals):
        if gb is None:
            ga_, z_, g_, b_ = vals
            dy = ga_
        else:
            ga_, gb_, z_, g_, b_ = vals
            dy = ga_ + cfg.alpha * gb_
        _, vjp = jax.vjp(_ln, z_, g_, b_)
        dz, dg, db = vjp(dy)
        return dz, dz, dg, db

    rows = [ga, z] if gb is None else [ga, gb, z]
    return _rowwise(name, fn, rows, [g, b], [(d, F32), (d, BF16)], accs=[(1, d), (1, d)])


def _layer_bwd(cfg, l, batch, ga, gb, sv, mem_b, rope_c, rope_s, hosts, lw, bf_pad, g_cq, g_ckv, ln1, ln2):
    wa, wg, ws, wq, wk, wv, wmkv, wbr, wout, wff1, wff2 = lw
    w, d = cfg.width, cfg.d
    fox, mla, mem = _attn_specs(cfg, batch)
    nw = w // LANES
    gdt = BF16

    dz2, dz2b, dg2, db2 = _ln_bwd(f"ln2_bwd_{l}", cfg, ga, gb, sv["z2"], *ln2)
    du = _mm(f"ff2_bwd_x_{l}", "nt", [(dz2b, wff2)], [BF16],
             epi=lambda acc, u_: (acc * (2.0 * jnp.maximum(u_.astype(F32), 0.0)),), row_extras=[sv["u"]],
             hosts=hosts, host="ff2_bwd_x")
    dwff2 = _mm(f"ff2_bwd_w_{l}", "tn", [(sv["a"], dz2b)], [gdt])
    dwff1 = _mm(f"ff1_bwd_w_{l}", "tn", [(sv["h1b"], du)], [gdt])
    dh1 = _mm(f"ff1_bwd_x_{l}", "nt", [(du, wff1)], [F32])
    dz1, dz1b, dg1, db1 = _ln_bwd(f"ln1_bwd_{l}", cfg, dh1, dz2, sv["z1"], *ln1)
    dmerged = _mm(f"out_bwd_x_{l}", "nt", [(dz1b, wout)], [F32])
    dwout = _mm(f"out_bwd_w_{l}", "tn", [(sv["merged"], dz1b)], [gdt])

    def merge_bwd(dm, gl_, b0, b1, b2):
        def f(gl__, b0_, b1_, b2_):
            g = jax.nn.sigmoid(gl__)
            return g[:, :d] * b0_ + g[:, d:2 * d] * b1_ + g[:, 2 * d:] * b2_

        _, vjp = jax.vjp(f, gl_, b0, b1, b2)
        return vjp(dm)

    dgl, db0, db1_, db2_ = _rowwise(f"merge_bwd_{l}", merge_bwd, [dmerged, sv["gl"]] + sv["bps"], [],
                                    [(3 * d, BF16), (d, BF16), (d, BF16), (d, BF16)])
    dbps = (db0, db1_, db2_)
    dos = [_mm(f"branch_bwd_x_{n}_{l}", "nt", [(dbps[n], wbr[n])], [BF16]) for n in range(3)]
    dwbr = jnp.stack([_mm(f"branch_bwd_w_{n}_{l}", "tn", [(sv["o"][n], dbps[n])], [gdt]) for n in range(3)])
    if hosts is not None:
        hosts.early_grads(dict(w_br=dwbr, w_out=dwout, w_ff1=dwff1, w_ff2=dwff2))

    pa = sv["pa"]
    dq_a, dk_a, dv_a, dcfk = _attn_bwd(f"fox_bwd_{l}", fox, (pa, 0), (pa, nw), (pa, 2 * nw), sv["o"][0], sv["lse"][0],
                                       dos[0], sv["cfk"], hosts=hosts, host="fox_bwd")
    dqf, dkf, dvb = _attn_bwd(f"mla_bwd_{l}", mla, (sv["qf"], 0), (sv["kf"], 0), (sv["vb"], 0), sv["o"][1],
                              sv["lse"][1], dos[1], hosts=hosts, host="mla_bwd")
    dqm, dmk, dmv = _attn_bwd(f"mem_bwd_{l}", mem, (pa, 3 * nw), (sv["mkv"], 0), (sv["mkv"], nw), sv["o"][2],
                              sv["lse"][2], dos[2], hosts=hosts, host="mem_bwd")
    dwmkv = _mm(f"mem_kv_bwd_w_{l}", "tn", [(mem_b, jnp.concatenate([dmk, dmv], axis=1))], [gdt])

    (dq_raw,) = _rowwise(f"mla_q_rope_bwd_{l}", lambda dy, c_, s_: (_rope_t(dy, c_, s_),), [dqf, rope_c, rope_s], [],
                         [(wq.shape[1], BF16)])
    dwq = _mm(f"mla_q_bwd_w_{l}", "tn", [(sv["nq"], dq_raw)], [gdt])
    dnq = _mm(f"mla_q_bwd_x_{l}", "nt", [(dq_raw, wq)], [F32])
    dwk = _mm(f"mla_k_bwd_w_{l}", "tn", [(sv["nkv"], dkf)], [gdt])
    dwv = _mm(f"mla_v_bwd_w_{l}", "tn", [(sv["nkv"], dvb)], [gdt])
    dnkv = _mm(f"mla_kv_bwd_x_{l}", "nt", [(dkf, wk), (dvb, wv)], [F32])

    dlogf = _forget_cumsum_bwd(f"cum_forget_bwd_{l}", dcfk, batch, cfg.seq, cfg.fox_h, fox.tq)

    def small_bwd(ps_, dlogf_, dnq_, dnkv_, dkf_, c_, s_, bf_, gq_, gkv_):
        _, vjp = jax.vjp(functools.partial(_small_core, cfg), ps_, bf_, gq_, gkv_)
        dps, dbf, dgq, dgkv = vjp((dlogf_, dnq_, dnkv_))
        dkpe = dkf_[:, :LANES].astype(F32)
        for hh in range(1, cfg.mla_h):
            dkpe = dkpe + dkf_[:, hh * LANES:(hh + 1) * LANES].astype(F32)
        lane = lax.broadcasted_iota(jnp.int32, (1, LANES), 1)
        dkpe = jnp.where((lane >= MLA_NOPE) & (lane < MLA_NOPE + MLA_ROPE), dkpe, 0.0)
        dkr = _rope_t(dkpe, c_, s_)
        dps = jnp.concatenate([dps[:, :cfg.small_w - LANES], dkr], axis=1)
        return dps, dbf, dgq, dgkv

    dps, dbf, dgq, dgkv = _rowwise(
        f"small_bwd_{l}", small_bwd, [sv["ps"], dlogf, dnq, dnkv, dkf, rope_c, rope_s], [bf_pad, g_cq, g_ckv],
        [(cfg.small_w, BF16)], accs=[(1, LANES), (1, cfg.q_rank), (1, cfg.kv_rank)])

    dpa = jnp.concatenate([dq_a.astype(BF16), dk_a, dv_a, dqm.astype(BF16)], axis=1)
    hb = sv["hb"]
    dh = _mm(f"proj_bwd_x_{l}", "nt", [(dpa, wa), (dgl, wg), (dps, ws)], [F32], tn=d)
    dwa = _mm(f"proj_a_bwd_w_{l}", "tn", [(hb, dpa)], [gdt])
    dwg = _mm(f"proj_gates_bwd_w_{l}", "tn", [(hb, dgl)], [gdt])
    dws = _mm(f"proj_small_bwd_w_{l}", "tn", [(hb, dps)], [gdt], tn=cfg.small_w)
    dw_in, dw_uq, dw_ukv = _layer_weight_grads(cfg, dwa, dwg, dws, dwq, dwk, dwv)
    big = dict(w_in=dw_in, w_uq=dw_uq, w_ukv=dw_ukv, w_mem_kv=dwmkv, w_br=dwbr, w_out=dwout,
               w_ff1=dwff1, w_ff2=dwff2)
    small = dict(b_forget=dbf[0, :cfg.fox_h], g_cq=dgq[0], g_ckv=dgkv[0], ln1_g=dg1[0], ln1_b=db1[0],
                 ln2_g=dg2[0], ln2_b=db2[0])
    return dh, dz1, big, small


def _rope_tables(positions):
    inv_freq = ROPE_BASE ** (-jnp.arange(0, MLA_ROPE, 2, dtype=F32) / MLA_ROPE)
    ang = positions.astype(F32).reshape(-1)[:, None] * inv_freq
    cos, sin = jnp.cos(ang), jnp.sin(ang)
    t = ang.shape[0]
    rope_c = jnp.concatenate([jnp.ones((t, MLA_NOPE), F32), cos, cos, jnp.zeros((t, LANES - MLA_NOPE - MLA_ROPE), F32)], axis=1)
    rope_s = jnp.concatenate([jnp.zeros((t, MLA_NOPE), F32), -sin, sin, jnp.zeros((t, LANES - MLA_NOPE - MLA_ROPE), F32)], axis=1)
    return rope_c, rope_s


def _local_step(cfg, x, mem, positions, target, small_w, comm):
    batch = x.shape[0]
    d, depth = cfg.d, cfg.depth
    t = batch * cfg.seq
    x2, tgt = x.reshape(t, d), target.reshape(t, d)
    mem_b = mem.reshape(batch * cfg.n_mem, d).astype(BF16)
    rope_c, rope_s = _rope_tables(positions)
    row = lambda v: v.reshape(1, -1)
    ln_in = (row(small_w["ln_in_g"]), row(small_w["ln_in_b"]))

    h, hb = _rowwise("ln_in", lambda x_, g_, b_: (_ln(x_, g_, b_),) * 2, [x2], list(ln_in), [(d, F32), (d, BF16)])
    layers, saves = [], []
    for l in range(depth):
        first = comm.weights(l, LATE)
        lw = _layer_weights(cfg, first["w_in"], first["w_uq"], first["w_ukv"]) + (first["w_mem_kv"],)
        later = lambda l=l: tuple(comm.weights(l, EARLY).values())
        par = dict(
            bf_pad=jnp.pad(row(small_w["b_forget"][l]), ((0, 0), (0, LANES - cfg.fox_h))),
            g_cq=row(small_w["g_cq"][l]), g_ckv=row(small_w["g_ckv"][l]),
            ln1=(row(small_w["ln1_g"][l]), row(small_w["ln1_b"][l])),
            ln2=(row(small_w["ln2_g"][l]), row(small_w["ln2_b"][l])))
        h, hb, sv = _layer_fwd(cfg, l, batch, h, hb, mem_b, rope_c, rope_s, comm.forward_hosts(l), (lw, later), **par)
        layers.append(dict(par, lw=sv.pop("lw")))
        saves.append(sv)

    def loss_fn(y, tg):
        err = y - tg
        part = 0.5 * jnp.sum(jnp.mean(err * err, axis=-1, keepdims=True), axis=0, keepdims=True)
        return err * (1.0 / d), jnp.broadcast_to(part, (1, LANES))

    ga, loss_acc = _rowwise("loss", loss_fn, [h, tgt], [], [(d, F32)], accs=[(1, LANES)])
    gb = None
    small_g = {k: [None] * depth for k in ("b_forget", "g_cq", "g_ckv", "ln1_g", "ln1_b", "ln2_g", "ln2_b")}
    for l in reversed(range(depth)):
        ga, gb, big, small = _layer_bwd(cfg, l, batch, ga, gb, saves[l], mem_b, rope_c, rope_s,
                                        comm.backward_hosts(l), **layers[l])
        comm.grads(l, big)
        for k, v in small.items():
            small_g[k][l] = v
    dx, _, dg_in, db_in = _ln_bwd("ln_in_bwd", cfg, ga, gb, x2, *ln_in)
    small_g = {k: jnp.stack(v) for k, v in small_g.items()}
    small_g["ln_in_g"], small_g["ln_in_b"] = dg_in[0], db_in[0]
    return loss_acc[0, 0], dx.reshape(x.shape), small_g


BIG = ("w_in", "w_uq", "w_ukv", "w_mem_kv", "w_br", "w_out", "w_ff1", "w_ff2")
SMALL = ("ln_in_g", "ln_in_b", "b_forget", "g_cq", "g_ckv", "ln1_g", "ln1_b", "ln2_g", "ln2_b")
ROW_CUT = ("w_mem_kv", "w_out", "w_ff2")
LATE, EARLY = BIG[:4], BIG[4:]


def _shard_2d(a):
    cols = a.shape[-1]
    rows = a.size // cols
    return a.reshape(2, rows // 2, cols)


def _full_from_slots(name, slots, shard_shape):
    if name in ROW_CUT and len(shard_shape) == 2:
        return slots.reshape((N_CHIPS * shard_shape[0], shard_shape[1]))
    parts = slots.reshape((N_CHIPS,) + shard_shape)
    axis = len(shard_shape) - (2 if name in ROW_CUT else 1)
    return jnp.concatenate([parts[i] for i in range(N_CHIPS)], axis=axis)


def _slots_from_full(name, full, shard_shape):
    if name in ROW_CUT and len(shard_shape) == 2:
        return full.reshape(N_CHIPS, 2, shard_shape[0] // 2, shard_shape[1])
    axis = len(shard_shape) - (2 if name in ROW_CUT else 1)
    parts = jnp.stack(jnp.split(full, N_CHIPS, axis=axis))
    cols = shard_shape[-1]
    return parts.reshape(N_CHIPS, 2, -1, cols)


def _pack_small(cfg, vals):
    flat = jnp.concatenate([vals[k].reshape(-1).astype(F32) for k in SMALL])
    pad = (-flat.shape[0]) % (LANES * LANES)
    return jnp.pad(flat, (0, pad)).reshape(-1, LANES)


def _unpack_small(packed, like):
    flat, out, off = packed.reshape(-1), {}, 0
    for k in SMALL:
        n = like[k].size
        out[k] = flat[off:off + n].reshape(like[k].shape)
        off += n
    return out


class LayerComm:
    def __init__(self, cfg, w, m, v):
        self.cfg, self.w = cfg, w
        self.shards = [{k: _shard_2d(w[k][l].astype(BF16)) for k in BIG} for l in range(cfg.depth)]
        self.got = {LATE: _gather_now([self.shards[0][k] for k in LATE])}
        self.sets = {}
        self.pending = None
        halves = lambda a: a.reshape(a.shape[0], 2, -1, a.shape[-1])
        self.state = {k: [halves(a[k]) for a in (w, m, v)] for k in BIG}
        self.outs = {k: None for k in BIG}

    def weights(self, l, names):
        if names in self.got:
            got = self.got.pop(names)
        elif names is EARLY:
            got = self.sets.pop((l, EARLY[:2])).got + self.sets.pop((l, EARLY[2:])).got
        else:
            got = self.sets.pop((l, names)).got
        mine = (_chip(), 0, 0, 0)
        return {k: _full_from_slots(k, lax.dynamic_update_slice(g, self.shards[l][k][None], mine), self.w[k].shape[1:])
                for k, g in zip(names, got)}

    def forward_hosts(self, l):
        for names, hosts in ((EARLY[2:], ("fox_fwd", "mla_fwd")), (EARLY[:2], ("mla_fwd", "mem_fwd"))):
            self.sets[l, names] = GatherSet([self.shards[l][k] for k in names], hosts)
        if l + 1 < self.cfg.depth:
            self.sets[l + 1, LATE] = GatherSet([self.shards[l + 1][k] for k in LATE], ("mla_fwd", "mem_fwd"))
        return Together([s for (layer, _), s in self.sets.items() if layer in (l, l + 1)])

    def _reduce(self, tag, names, big, hosts):
        return ReduceLayer(tag, [_slots_from_full(k, big[k], self.w[k].shape[1:]) for k in names], hosts)

    def backward_hosts(self, l):
        comm = self

        class Riders(Together):
            def early_grads(self, grads):
                comm.early = comm._reduce(f"{l}e", EARLY, grads, ("fox_bwd", "mla_bwd", "mem_bwd"))
                self.members.append(comm.early)

        return Riders([self.pending[1]] if self.pending else [])

    def _update(self, l, names, reduce):
        for k, (mine, theirs) in zip(names, reduce.result()):
            self.outs[k] = _adamw_layer(f"adamw_{k}_{l}", l, *self.state[k], mine, theirs, self.outs[k])

    def grads(self, l, big):
        self._update(l, EARLY, self.early)
        if self.pending:
            self._update(self.pending[0], LATE, self.pending[1])
        self.pending = (l, self._reduce(f"{l}l", LATE, big, ("ff2_bwd_x", "fox_bwd", "mla_bwd")))

    def finish(self):
        self.pending[1].run_now()
        self._update(self.pending[0], LATE, self.pending[1])
        return {k: tuple(a.reshape(self.w[k].shape) for a in self.outs[k]) for k in BIG}


def _step(cfg, x, mem, positions, target, w, m, v):
    comm = LayerComm(cfg, w, m, v)
    small_w = {k: w[k] for k in SMALL}
    loss_local, dx, small_g = _local_step(cfg, x, mem, positions, target, small_w, comm)
    loss = lax.psum(loss_local, ("x", "y", "c"))
    outs_big = comm.finish()

    g_small = _allreduce_small(_pack_small(cfg, small_g))
    packs = [_pack_small(cfg, {k: d_[k] for k in SMALL}) for d_ in (w, m, v)]
    dl, nm, nv = _slabwise("adamw_small", _adamw_math, [a[None] for a in (packs[0], g_small, packs[1], packs[2])],
                           [F32, F32, F32])
    outs_small = [_unpack_small(a[0] if a.ndim == 3 else a, w) for a in (g_small, dl, nm, nv)]

    names = SMALL[:2] + ("w_in", "b_forget", "w_uq", "g_cq", "w_ukv", "g_ckv", "w_mem_kv", "w_br", "w_out",
                         "ln1_g", "ln1_b", "w_ff1", "w_ff2", "ln2_g", "ln2_b")
    result = [loss, dx]
    for part in range(4):
        for k in names:
            result.append(outs_big[k][part] if k in outs_big else outs_small[part][k])
    return tuple(result)


def kernel(x, mem, positions, ln_in_g, ln_in_b, w_in, b_forget, w_uq, g_cq, w_ukv, g_ckv, w_mem_kv, w_br, w_out, ln1_g, ln1_b, w_ff1, w_ff2, ln2_g, ln2_b, loss_target, m_ln_in_g, m_ln_in_b, m_w_in, m_b_forget, m_w_uq, m_g_cq, m_w_ukv, m_g_ckv, m_w_mem_kv, m_w_br, m_w_out, m_ln1_g, m_ln1_b, m_w_ff1, m_w_ff2, m_ln2_g, m_ln2_b, v_ln_in_g, v_ln_in_b, v_w_in, v_b_forget, v_w_uq, v_g_cq, v_w_ukv, v_g_ckv, v_w_mem_kv, v_w_br, v_w_out, v_ln1_g, v_ln1_b, v_w_ff1, v_w_ff2, v_ln2_g, v_ln2_b):
    w = dict(ln_in_g=ln_in_g, ln_in_b=ln_in_b, w_in=w_in, b_forget=b_forget, w_uq=w_uq, g_cq=g_cq, w_ukv=w_ukv,
             g_ckv=g_ckv, w_mem_kv=w_mem_kv, w_br=w_br, w_out=w_out, ln1_g=ln1_g, ln1_b=ln1_b, w_ff1=w_ff1,
             w_ff2=w_ff2, ln2_g=ln2_g, ln2_b=ln2_b)
    m = dict(ln_in_g=m_ln_in_g, ln_in_b=m_ln_in_b, w_in=m_w_in, b_forget=m_b_forget, w_uq=m_w_uq, g_cq=m_g_cq,
             w_ukv=m_w_ukv, g_ckv=m_g_ckv, w_mem_kv=m_w_mem_kv, w_br=m_w_br, w_out=m_w_out, ln1_g=m_ln1_g,
             ln1_b=m_ln1_b, w_ff1=m_w_ff1, w_ff2=m_w_ff2, ln2_g=m_ln2_g, ln2_b=m_ln2_b)
    v = dict(ln_in_g=v_ln_in_g, ln_in_b=v_ln_in_b, w_in=v_w_in, b_forget=v_b_forget, w_uq=v_w_uq, g_cq=v_g_cq,
             w_ukv=v_w_ukv, g_ckv=v_g_ckv, w_mem_kv=v_w_mem_kv, w_br=v_w_br, w_out=v_w_out, ln1_g=v_ln1_g,
             ln1_b=v_ln1_b, w_ff1=v_w_ff1, w_ff2=v_w_ff2, ln2_g=v_ln2_g, ln2_b=v_ln2_b)
    return _step(Cfg(), x, mem, positions, loss_target, w, m, v)
```

```python
import functools
from typing import NamedTuple

import jax
import jax.numpy as jnp
from jax import lax
from jax.experimental import pallas as pl
from jax.experimental.pallas import tpu as pltpu

F32 = jnp.float32
BF16 = jnp.bfloat16
MESH = pl.DeviceIdType.MESH

LANES = 128
SUBLANES = 8
VMEM_BYTES = 64 * 1024 * 1024
N_CHIPS = 4
N_DEV = 8

FOX_DH = 64
MLA_NOPE = 64
MLA_ROPE = 32
MLA_V = 64
MEM_DH = 128
ROPE_BASE = 10000.0
LN_EPS = 1e-5
RMS_EPS = 1e-6
NEG_INF = -1e30
LOG2E = 1.4426950408889634
ATTN_TILE = 512
MEM_ATTN_TILE = 1024

ADAM_LR = 0.001
ADAM_B1 = 0.9
ADAM_B2 = 0.999
ADAM_EPS = 1e-08
ADAM_WD = 0.01
ADAM_STEP = 10


class Cfg(NamedTuple):
    d: int = 1024
    depth: int = 4
    seq: int = 2048
    chunk: int = 64
    n_mem: int = 256
    fox_h: int = 8
    mla_h: int = 8
    q_rank: int = 384
    kv_rank: int = 256
    mem_h: int = 4
    d_ff: int = 4096

    @property
    def width(self):
        return self.fox_h * FOX_DH

    @property
    def alpha(self):
        return (2 * self.depth) ** 0.25

    @property
    def small_w(self):
        return LANES + self.q_rank + self.kv_rank + LANES

    @property
    def in_splits(self):
        return (3 * self.width, self.fox_h, self.q_rank, self.kv_rank, MLA_ROPE, self.width, 3 * self.d)


class Exchange(NamedTuple):
    ins: tuple
    out_shapes: tuple
    plan: object
    n_copies: int
    aliases: dict = {}


def _peer(rel):
    x, y, c = lax.axis_index("x"), lax.axis_index("y"), lax.axis_index("c")
    if rel == "c":
        return (x, y, 1 - c)
    return ((1 - x) if rel in (1, 3) else x, (1 - y) if rel in (2, 3) else y, c)


def _exchange_copies(ex, in_refs, out_refs, send_sems, recv_sems):
    planned = ex.plan(in_refs, out_refs)
    assert len(planned) == ex.n_copies, len(planned)
    return [pltpu.make_async_remote_copy(src_ref=src, dst_ref=dst, send_sem=send_sems.at[i], recv_sem=recv_sems.at[i],
                                         device_id=_peer(rel), device_id_type=MESH)
            for i, (src, dst, rel) in enumerate(planned)]


def _pcall(body, rider=None, **kw):
    if rider is not None:
        n_in, n_out, grid = len(kw["in_specs"]), len(kw["out_specs"]), kw["grid"]
        n_rin, n_rout = len(rider.ins), len(rider.out_shapes)
        host = body

        def body(*refs):
            ins, rins = refs[:n_in], refs[n_in:n_in + n_rin]
            outs = refs[n_in + n_rin:n_in + n_rin + n_out]
            routs = refs[n_in + n_rin + n_out:n_in + n_rin + n_out + n_rout]
            scratch = refs[n_in + n_rin + n_out + n_rout:-2]
            copies = _exchange_copies(rider, rins, routs, refs[-2], refs[-1])
            first = functools.reduce(jnp.logical_and, [pl.program_id(a) == 0 for a in range(len(grid))])
            last = functools.reduce(jnp.logical_and, [pl.program_id(a) == n - 1 for a, n in enumerate(grid)])

            @pl.when(first)
            def _():
                for cp in copies:
                    cp.start()

            host(*ins, *outs, *scratch)

            @pl.when(last)
            def _():
                for cp in copies:
                    cp.wait()

        any_spec = pl.BlockSpec(memory_space=pl.ANY)
        sems = [pltpu.SemaphoreType.DMA((rider.n_copies,))] * 2
        kw = dict(
            kw,
            in_specs=list(kw["in_specs"]) + [any_spec] * n_rin,
            out_specs=list(kw["out_specs"]) + [any_spec] * n_rout,
            out_shape=list(kw["out_shape"]) + list(rider.out_shapes),
            scratch_shapes=list(kw.get("scratch_shapes", ())) + sems,
            input_output_aliases={**kw.get("input_output_aliases", {}),
                                  **{n_in + i: n_out + o for i, o in rider.aliases.items()}},
        )
    call = pl.pallas_call(body, **kw)
    return lambda *ops: call(*[pltpu.with_memory_space_constraint(o, pltpu.HBM) for o in ops])


class Hosts:
    def rider(self, host):
        return None

    def done(self, host, outs):
        pass

    def early_grads(self, grads):
        pass


def _merge_exchanges(exs):
    n_ins = [len(e.ins) for e in exs]
    n_outs = [len(e.out_shapes) for e in exs]

    def plan(in_refs, out_refs):
        copies, i, o = [], 0, 0
        for e, ni, no in zip(exs, n_ins, n_outs):
            copies += e.plan(in_refs[i:i + ni], out_refs[o:o + no])
            i, o = i + ni, o + no
        return copies

    aliases, i, o = {}, 0, 0
    for e, ni, no in zip(exs, n_ins, n_outs):
        aliases.update({i + a: o + b for a, b in e.aliases.items()})
        i, o = i + ni, o + no
    return Exchange(sum((e.ins for e in exs), ()), sum((e.out_shapes for e in exs), ()), plan,
                    sum(e.n_copies for e in exs), aliases)


class Together(Hosts):
    def __init__(self, members):
        self.members, self.active = list(members), []

    def rider(self, host):
        self.active = [(m, r) for m, r in ((m, m.rider(host)) for m in self.members) if r is not None]
        if not self.active:
            return None
        return _merge_exchanges([r for _, r in self.active])

    def done(self, host, outs):
        for m, r in self.active:
            m.done(host, outs[:len(r.out_shapes)])
            outs = outs[len(r.out_shapes):]


def _hosted(hosts, host, call, ops, n_results):
    rider = hosts.rider(host) if hosts is not None else None
    if rider is None:
        return call(None)(*ops)
    res = call(rider)(*ops, *rider.ins)
    hosts.done(host, list(res[n_results:]))
    return res[:n_results]


def _nbytes(shape, dtype):
    n = 1
    for s in shape:
        n *= s
    return n * jnp.dtype(dtype).itemsize


def _tile(dim, target):
    if dim <= target:
        return dim
    t = target - target % LANES
    while t >= LANES:
        if dim % t == 0:
            return t
        t -= LANES
    return dim


def _params(block_bytes, scratch_bytes=0):
    est = 2 * block_bytes + scratch_bytes + 24 * 1024 * 1024
    return pltpu.CompilerParams(vmem_limit_bytes=int(min(max(est, 32 * 1024 * 1024), VMEM_BYTES - 4 * 1024 * 1024)))


_DIMS = {"nn": (((1,), (0,)), ((), ())), "nt": (((1,), (1,)), ((), ())), "tn": (((0,), (0,)), ((), ()))}


MM_TILE = 1024
MM_BLOCK_BYTES = 16 * 1024 * 1024


def _mm_tiles(mode, pairs, out_dtypes, m, n, tm, tn):
    fixed_m, fixed_n = tm is not None, tn is not None
    tm, tn = _tile(m, tm or MM_TILE), _tile(n, tn or MM_TILE)

    def block_bytes(tm_, tn_):
        total = sum(_nbytes((tm_, tn_), dt) for dt in out_dtypes)
        for a, b in pairs:
            k = a.shape[0] if mode == "tn" else a.shape[1]
            total += _nbytes((k, tm_), a.dtype) + _nbytes((k, tn_), b.dtype)
        return total

    while block_bytes(tm, tn) > MM_BLOCK_BYTES:
        if not fixed_m and tm >= tn and tm > 2 * LANES:
            tm = _tile(m, tm // 2)
        elif not fixed_n and tn > 2 * LANES:
            tn = _tile(n, tn // 2)
        elif not fixed_m and tm > 2 * LANES:
            tm = _tile(m, tm // 2)
        else:
            break
    return tm, tn


def _mm(name, mode, pairs, out_dtypes, tm=None, tn=None, epi=None, combine=None, row_extras=(), bc_extras=(),
        hosts=None, host=None):
    a0, b0 = pairs[0]
    m = a0.shape[1] if mode == "tn" else a0.shape[0]
    n = b0.shape[0] if mode == "nt" else b0.shape[1]
    tm, tn = _mm_tiles(mode, pairs, out_dtypes, m, n, tm, tn)
    in_specs, ops, blk = [], [], 0
    for a, b in pairs:
        if mode == "tn":
            k = a.shape[0]
            sa, sha = pl.BlockSpec((k, tm), lambda i, j: (0, i)), (k, tm)
        else:
            k = a.shape[1]
            sa, sha = pl.BlockSpec((tm, k), lambda i, j: (i, 0)), (tm, k)
        if mode == "nt":
            sb, shb = pl.BlockSpec((tn, k), lambda i, j: (j, 0)), (tn, k)
        else:
            sb, shb = pl.BlockSpec((k, tn), lambda i, j: (0, j)), (k, tn)
        in_specs += [sa, sb]
        ops += [a, b]
        blk += _nbytes(sha, a.dtype) + _nbytes(shb, b.dtype)
    for e in row_extras:
        w = e.shape[1]
        if w == n:
            in_specs.append(pl.BlockSpec((tm, tn), lambda i, j: (i, j)))
            blk += _nbytes((tm, tn), e.dtype)
        else:
            in_specs.append(pl.BlockSpec((tm, w), lambda i, j: (i, 0)))
            blk += _nbytes((tm, w), e.dtype)
        ops.append(e)
    for e in bc_extras:
        r, w = e.shape
        if w == n:
            in_specs.append(pl.BlockSpec((r, tn), lambda i, j: (0, j)))
        else:
            in_specs.append(pl.BlockSpec((r, w), lambda i, j: (0, 0)))
        blk += _nbytes((r, w), e.dtype)
        ops.append(e)
    npairs, nrow, nbc, nout = len(pairs), len(row_extras), len(bc_extras), len(out_dtypes)
    dims = _DIMS[mode]

    def body(*refs):
        prods = [lax.dot_general(refs[2 * p][...].astype(BF16), refs[2 * p + 1][...].astype(BF16), dims,
                                 preferred_element_type=F32) for p in range(npairs)]
        ex = [r[...] for r in refs[2 * npairs:2 * npairs + nrow + nbc]]
        if combine is not None:
            outs = combine(prods, *ex)
        else:
            acc = functools.reduce(lambda s, d: s + d, prods)
            outs = (acc,) if epi is None else epi(acc, *ex)
        for o_ref, o in zip(refs[2 * npairs + nrow + nbc:], outs):
            o_ref[...] = o.astype(o_ref.dtype)

    blk += sum(_nbytes((tm, tn), dt) for dt in out_dtypes) + 2 * _nbytes((tm, tn), F32)
    call = lambda rider: _pcall(
        body,
        rider=rider,
        name=name,
        grid=(m // tm, n // tn),
        in_specs=in_specs,
        out_specs=[pl.BlockSpec((tm, tn), lambda i, j: (i, j)) for _ in range(nout)],
        out_shape=[jax.ShapeDtypeStruct((m, n), dt) for dt in out_dtypes],
        compiler_params=_params(blk),
    )
    res = _hosted(hosts, host, call, ops, nout)
    return res[0] if nout == 1 else res


def _rowwise(name, fn, row_ins, bc_ins, outs, accs=(), tm=256):
    rows = row_ins[0].shape[0]
    tm = min(tm, rows)
    assert rows % tm == 0
    nrow, nbc, nout, nacc = len(row_ins), len(bc_ins), len(outs), len(accs)
    in_specs = [pl.BlockSpec((tm, a.shape[1]), lambda i: (i, 0)) for a in row_ins]
    in_specs += [pl.BlockSpec(a.shape, lambda i: (0, 0)) for a in bc_ins]
    out_specs = [pl.BlockSpec((tm, w), lambda i: (i, 0)) for w, _ in outs]
    out_specs += [pl.BlockSpec(s, lambda i: (0, 0)) for s in accs]
    out_shape = [jax.ShapeDtypeStruct((rows, w), dt) for w, dt in outs]
    out_shape += [jax.ShapeDtypeStruct(s, F32) for s in accs]

    def body(*refs):
        vals = fn(*[r[...] for r in refs[:nrow + nbc]])
        o_refs = refs[nrow + nbc:]
        for r, v in zip(o_refs[:nout], vals[:nout]):
            r[...] = v.astype(r.dtype)
        if nacc:
            @pl.when(pl.program_id(0) == 0)
            def _():
                for r in o_refs[nout:]:
                    r[...] = jnp.zeros(r.shape, F32)

            for r, v in zip(o_refs[nout:], vals[nout:]):
                r[...] += v

    blk = sum(_nbytes((tm, a.shape[1]), a.dtype) for a in row_ins) + sum(_nbytes(a.shape, a.dtype) for a in bc_ins)
    blk += sum(_nbytes((tm, w), dt) for w, dt in outs) + sum(_nbytes(s, F32) for s in accs)
    res = _pcall(
        body,
        name=name,
        grid=(rows // tm,),
        in_specs=in_specs,
        out_specs=out_specs,
        out_shape=out_shape,
        compiler_params=_params(2 * blk),
    )(*row_ins, *bc_ins)
    return res


def _ln(z, g, b):
    mu = jnp.mean(z, axis=-1, keepdims=True)
    zc = z - mu
    var = jnp.mean(zc * zc, axis=-1, keepdims=True)
    return zc * lax.rsqrt(var + LN_EPS) * g + b


def _rms(x, g):
    return x * lax.rsqrt(jnp.mean(x * x, axis=-1, keepdims=True) + RMS_EPS) * g


def _colsum(v):
    return jnp.sum(v, axis=0, keepdims=True)


def _rope_swap(x):
    w = x.shape[1]
    lane = lax.broadcasted_iota(jnp.int32, (1, w), 1) % LANES
    from_left = pltpu.roll(x, 16, 1)
    from_right = pltpu.roll(x, w - 16, 1)
    lo = (lane >= MLA_NOPE) & (lane < MLA_NOPE + 16)
    hi = (lane >= MLA_NOPE + 16) & (lane < MLA_NOPE + 32)
    return jnp.where(hi, from_left, jnp.where(lo, from_right, 0.0))


def _rope(x, cos_t, sin_t):
    nh = x.shape[1] // LANES
    ct, st = jnp.tile(cos_t, (1, nh)), jnp.tile(sin_t, (1, nh))
    return x * ct + _rope_swap(x) * st


def _rope_t(dy, cos_t, sin_t):
    nh = dy.shape[1] // LANES
    ct, st = jnp.tile(cos_t, (1, nh)), jnp.tile(sin_t, (1, nh))
    return dy * ct + _rope_swap(dy * st)


def _block_cumsum(v, carry, reverse):
    tb = v.shape[0]
    r = lax.broadcasted_iota(jnp.int32, (tb, tb), 0)
    c = lax.broadcasted_iota(jnp.int32, (tb, tb), 1)
    tri = jnp.where((c >= r) if reverse else (c <= r), 1.0, 0.0).astype(BF16)
    hi = v.astype(BF16)
    r1 = v - hi.astype(F32)
    mid = r1.astype(BF16)
    lo = (r1 - mid.astype(F32)).astype(BF16)
    out = carry + sum(jnp.dot(tri, p, preferred_element_type=F32) for p in (hi, mid, lo))
    return out, (out[0:1, :] if reverse else out[tb - 1:tb, :])


def _forget_cumsum(name, logf, batch, seq, heads, tb):
    nb = seq // tb

    def body(x_ref, keys_ref, carry):
        @pl.when(pl.program_id(1) == 0)
        def _():
            carry[...] = jnp.zeros(carry.shape, F32)

        out, carry[...] = _block_cumsum(x_ref[...], carry[...], False)
        out = out * LOG2E
        keys_ref[...] = jnp.concatenate([jnp.broadcast_to(out[:, h:h + 1], (tb, LANES)) for h in range(heads)], axis=1)

    return _pcall(
        body,
        name=name,
        grid=(batch, nb),
        in_specs=[pl.BlockSpec((tb, LANES), lambda b, i: (b * nb + i, 0))],
        out_specs=pl.BlockSpec((tb, heads * LANES), lambda b, i: (b * nb + i, 0)),
        out_shape=jax.ShapeDtypeStruct((batch * seq, heads * LANES), F32),
        scratch_shapes=[pltpu.VMEM((1, LANES), F32)],
        compiler_params=_params(4 * tb * (heads + 2) * LANES * 4),
    )(logf)


def _forget_cumsum_bwd(name, dcf, batch, seq, heads, tb):
    nb = seq // tb

    def body(x_ref, o_ref, carry):
        @pl.when(pl.program_id(1) == 0)
        def _():
            carry[...] = jnp.zeros(carry.shape, F32)

        lane = lax.broadcasted_iota(jnp.int32, (1, LANES), 1)
        v = jnp.zeros((tb, LANES), F32)
        for g in range(heads // 2):
            blk = x_ref[:, g * LANES:(g + 1) * LANES]
            moved = pltpu.roll(blk, 2 * g, 1) if g else blk
            v = v + jnp.where((lane >= 2 * g) & (lane < 2 * g + 2), moved, 0.0)
        o_ref[...], carry[...] = _block_cumsum(v, carry[...], True)

    return _pcall(
        body,
        name=name,
        grid=(batch, nb),
        in_specs=[pl.BlockSpec((tb, (heads // 2) * LANES), lambda b, i: (b * nb + nb - 1 - i, 0))],
        out_specs=pl.BlockSpec((tb, LANES), lambda b, i: (b * nb + nb - 1 - i, 0)),
        out_shape=jax.ShapeDtypeStruct((batch * seq, LANES), F32),
        scratch_shapes=[pltpu.VMEM((1, LANES), F32)],
        compiler_params=_params(4 * tb * (heads // 2 + 1) * LANES * 4),
    )(dcf)


class Attn(NamedTuple):
    batch: int
    sq: int
    sk: int
    groups: int
    hq: int
    hv: int
    mode: str
    scale: float
    chunk: int
    tq: int
    tk: int

    @property
    def hg(self):
        return self.hv

    @property
    def qw(self):
        return LANES * self.hg // self.hq

    @property
    def dv(self):
        return LANES // self.hv


def _head_lanes(j, dv):
    lane = lax.broadcasted_iota(jnp.int32, (1, LANES), 1)
    return (lane >= j * dv) & (lane < (j + 1) * dv)


def _head_q(sp, j, q_blk):
    if sp.hq == 2:
        return jnp.where(_head_lanes(j, FOX_DH), q_blk, jnp.zeros_like(q_blk))
    return q_blk[:, LANES * j:LANES * (j + 1)]


def _head_rows(sp, j):
    return slice(j * sp.dv, (j + 1) * sp.dv) if sp.hg == 2 else slice(None)


def _scores_t(sp, j, k_c, q_j, cfk_rep, k0, q0, masked):
    tk, tq = k_c.shape[0], q_j.shape[0]
    k_j = k_c if sp.hq == 2 else k_c[:, LANES * j:LANES * (j + 1)]
    st = lax.dot_general(k_j, q_j, _DIMS["nt"], preferred_element_type=F32) * (sp.scale * LOG2E)
    if sp.mode == "fox":
        st = st - jnp.tile(cfk_rep[:, LANES * j:LANES * (j + 1)], (1, tq // LANES))
    if masked:
        kidx = k0 + lax.broadcasted_iota(jnp.int32, (tk, tq), 0)
        qidx = q0 + lax.broadcasted_iota(jnp.int32, (tk, tq), 1)
        if sp.mode == "chunk":
            shift = sp.chunk.bit_length() - 1
            kidx, qidx = jnp.right_shift(kidx, shift), jnp.right_shift(qidx, shift)
        st = jnp.where(kidx <= qidx, st, NEG_INF)
    return st


def _attn_fwd(name, sp, q, k, v, cfk=None, hosts=None, host=None):
    (qa, qo), (ka, ko), (va, vo) = q, k, v
    tq, tk, hg, qw = sp.tq, sp.tk, sp.hg, sp.qw
    nqb, nkc = sp.sq // tq, sp.sk // tk
    fox, causal = sp.mode == "fox", sp.mode != "none"
    assert sp.sq % tq == 0 and sp.sk % tk == 0 and (not causal or (tq == tk and sp.sq == sp.sk))

    def body(*refs):
        if fox:
            q_ref, k_ref, v_ref, cfk_ref, o_ref, lse_ref, acc_scr = refs
        else:
            q_ref, k_ref, v_ref, o_ref, lse_ref, acc_scr = refs
        i = pl.program_id(2)
        q0 = i * tq
        q_blk = q_ref[...]
        qs = [_head_q(sp, j, q_blk) for j in range(hg)]
        acc_scr[...] = jnp.zeros(acc_scr.shape, F32)

        def chunk(kc, carry, masked):
            ms, ls = carry
            k0 = pl.multiple_of(kc * tk, tk)
            k_c = k_ref[pl.ds(k0, tk), :]
            v_c = v_ref[pl.ds(k0, tk), :]
            new_m, new_l = [], []
            for j in range(hg):
                st = _scores_t(sp, j, k_c, qs[j], cfk_ref[pl.ds(k0, tk), :] if fox else None, k0, q0, masked)
                m_new = jnp.maximum(ms[j], jnp.max(st, axis=0, keepdims=True))
                alpha = jnp.exp2(ms[j] - m_new)
                pt = jnp.exp2(st - m_new)
                new_m.append(m_new)
                new_l.append(alpha * ls[j] + jnp.sum(pt, axis=0, keepdims=True))
                pv = lax.dot_general(v_c, pt.astype(BF16), _DIMS["tn"], preferred_element_type=F32)
                r = _head_rows(sp, j)
                acc_scr[r, :] = acc_scr[r, :] * alpha + pv[r, :]
            return tuple(new_m), tuple(new_l)

        carry = (tuple(jnp.full((1, tq), NEG_INF, F32) for _ in range(hg)),
                 tuple(jnp.zeros((1, tq), F32) for _ in range(hg)))
        if causal:
            carry = lax.fori_loop(0, i, functools.partial(chunk, masked=False), carry)
            ms, ls = chunk(i, carry, True)
        else:
            ms, ls = lax.fori_loop(0, nkc, functools.partial(chunk, masked=False), carry)
        for j in range(hg):
            r = _head_rows(sp, j)
            acc_scr[r, :] = acc_scr[r, :] / ls[j]
            lse_ref[j:j + 1, :] = ms[j] + jnp.log(ls[j]) * LOG2E
        o_ref[...] = acc_scr[...].T

    in_specs = [
        pl.BlockSpec((tq, qw), lambda b, g, i: (b * nqb + i, qo + g)),
        pl.BlockSpec((sp.sk, qw), lambda b, g, i: (b, ko + g)),
        pl.BlockSpec((sp.sk, LANES), lambda b, g, i: (b, vo + g)),
    ]
    ops = [qa, ka, va]
    stat_blk = pl.BlockSpec((None, None, None, hg, tq), lambda b, g, i: (b, g, i, 0, 0))
    if fox:
        in_specs.append(pl.BlockSpec((sp.sk, hg * LANES), lambda b, g, i: (b, g)))
        ops.append(cfk)
    blk = _nbytes((tq, qw), BF16) + _nbytes((sp.sk, qw + LANES), BF16) + 2 * _nbytes((tq, LANES), F32)
    blk += _nbytes((sp.sk, hg * LANES), F32) + 6 * _nbytes((tk, tq), F32)
    call = lambda rider: _pcall(
        body,
        rider=rider,
        name=name,
        grid=(sp.batch, sp.groups, nqb),
        in_specs=in_specs,
        out_specs=[pl.BlockSpec((tq, LANES), lambda b, g, i: (b * nqb + i, g)), stat_blk],
        out_shape=[
            jax.ShapeDtypeStruct((sp.batch * sp.sq, sp.groups * LANES), F32),
            jax.ShapeDtypeStruct((sp.batch, sp.groups, nqb, hg, tq), F32),
        ],
        scratch_shapes=[pltpu.VMEM((LANES, tq), F32)],
        compiler_params=_params(blk, tq * LANES * 4),
    )
    return _hosted(hosts, host, call, ops, 2)


def _attn_bwd(name, sp, q, k, v, o, lse, do, cfk=None, hosts=None, host=None):
    (qa, qo), (ka, ko), (va, vo) = q, k, v
    tq, tk, hg, qw, dv = sp.tq, sp.tk, sp.hg, sp.qw, sp.dv
    nqb, nkb = sp.sq // tq, sp.sk // tk
    fox, causal = sp.mode == "fox", sp.mode != "none"
    assert sp.sq % tq == 0 and sp.sk % tk == 0 and (not causal or (tq == tk and sp.sq == sp.sk))

    def body(*refs):
        if fox:
            (q_ref, k_ref, v_ref, lse_ref, do_ref, cfk_ref, kall_ref, vall_ref, cfkall_ref,
             dq_ref, dk_ref, dv_ref, dcf_ref, delta_scr, dk_scr, dv_scr, dqt_scr, dcf_scr) = refs
        else:
            (q_ref, k_ref, v_ref, o_ref, lse_ref, do_ref,
             dq_ref, dk_ref, dv_ref, delta_scr, dk_scr, dv_scr, dqt_scr) = refs
        kb = pl.program_id(2)
        k0 = kb * tk
        heads = [_head_lanes(j, dv) for j in range(hg)]

        def head_do(j, do_c):
            return jnp.where(heads[j], do_c, jnp.zeros_like(do_c)) if hg == 2 else do_c

        def probs_t(j, k_c, v_c, q_c, do_c, i, cf_keys, c0, masked):
            st = _scores_t(sp, j, k_c, _head_q(sp, j, q_c), cf_keys, c0, i * tq, masked)
            pt = jnp.exp2(st - lse_ref[i][j:j + 1, :])
            dpt = lax.dot_general(v_c, head_do(j, do_c), _DIMS["nt"], preferred_element_type=F32)
            return pt, dpt

        @pl.when(kb == 0)
        def _():
            dqt_scr[...] = jnp.zeros(dqt_scr.shape, F32)

            def fill(i, carry):
                r0 = pl.multiple_of(i * tq, tq)
                do_c = do_ref[pl.ds(r0, tq), :]
                if fox:
                    q_c = q_ref[pl.ds(r0, tq), :]

                    def keys(kc, acc, masked):
                        c0 = pl.multiple_of(kc * tk, tk)
                        out = []
                        for j in range(hg):
                            pt, dpt = probs_t(j, kall_ref[pl.ds(c0, tk), :], vall_ref[pl.ds(c0, tk), :], q_c, do_c, i,
                                              cfkall_ref[pl.ds(c0, tk), :], c0, masked)
                            out.append(acc[j] + jnp.sum(pt * dpt, axis=0, keepdims=True))
                        return tuple(out)

                    d = lax.fori_loop(0, i, functools.partial(keys, masked=False),
                                      tuple(jnp.zeros((1, tq), F32) for _ in range(hg)))
                    d = keys(i, d, True)
                    for j in range(hg):
                        delta_scr[i, j:j + 1, :] = d[j]
                else:
                    prod_t = (do_c.astype(F32) * o_ref[pl.ds(r0, tq), :]).T
                    for j in range(hg):
                        delta_scr[i, j:j + 1, :] = jnp.sum(prod_t[_head_rows(sp, j), :], axis=0, keepdims=True)
                return carry

            lax.fori_loop(0, nqb, fill, 0)

        k_blk = k_ref[...]
        v_blk = v_ref[...]
        k_t = k_blk.astype(F32).T.astype(BF16)
        dk_scr[...] = jnp.zeros(dk_scr.shape, F32)
        dv_scr[...] = jnp.zeros(dv_scr.shape, F32)
        if fox:
            dcf_scr[...] = jnp.zeros(dcf_scr.shape, F32)

        def qblock(i, carry, masked):
            r0 = pl.multiple_of(i * tq, tq)
            q_c = q_ref[pl.ds(r0, tq), :]
            do_c = do_ref[pl.ds(r0, tq), :]
            for j in range(hg):
                pt, dpt = probs_t(j, k_blk, v_blk, q_c, do_c, i, cfk_ref[...] if fox else None, k0, masked)
                dst = pt * (dpt - delta_scr[i][j:j + 1, :])
                if fox:
                    part = dst[:, :LANES]
                    for t in range(1, tq // LANES):
                        part = part + dst[:, t * LANES:(t + 1) * LANES]
                    dcf_scr[j] += part
                ds_b = (dst * sp.scale).astype(BF16)
                dv_scr[j] += jnp.dot(pt.astype(BF16), do_c, preferred_element_type=F32)
                dk_scr[j] += jnp.dot(ds_b, q_c if sp.hq == 2 else _head_q(sp, j, q_c), preferred_element_type=F32)
                if sp.hq == 2:
                    r = pl.ds(j * FOX_DH, FOX_DH)
                    dqt_scr[i, r, :] += jnp.dot(k_t[j * FOX_DH:(j + 1) * FOX_DH, :], ds_b, preferred_element_type=F32)
                else:
                    r = pl.ds(j * LANES, LANES)
                    dqt_scr[i, r, :] += jnp.dot(k_t[j * LANES:(j + 1) * LANES, :], ds_b, preferred_element_type=F32)
            return carry

        if causal:
            qblock(kb, 0, True)
            lax.fori_loop(kb + 1, nqb, functools.partial(qblock, masked=False), 0)
        else:
            lax.fori_loop(0, nqb, functools.partial(qblock, masked=False), 0)

        @pl.when(kb == nkb - 1)
        def _():
            def untranspose(i, carry):
                dq_ref[pl.ds(pl.multiple_of(i * tq, tq), tq), :] = dqt_scr[i].T
                return carry

            lax.fori_loop(0, nqb, untranspose, 0)

        if hg == 2:
            dv_ref[...] = jnp.where(heads[0], dv_scr[0], dv_scr[1]).astype(dv_ref.dtype)
        else:
            dv_ref[...] = dv_scr[0].astype(dv_ref.dtype)
        if sp.hq == 2:
            dk_ref[...] = jnp.where(_head_lanes(0, FOX_DH), dk_scr[0], dk_scr[1]).astype(dk_ref.dtype)
        elif hg == 2:
            dk_ref[...] = jnp.concatenate([dk_scr[0], dk_scr[1]], axis=1).astype(dk_ref.dtype)
        else:
            dk_ref[...] = dk_scr[0].astype(dk_ref.dtype)
        if fox:
            lane = lax.broadcasted_iota(jnp.int32, (1, LANES), 1)
            sums = [jnp.sum(dcf_scr[j], axis=1, keepdims=True) for j in range(hg)]
            dcf_ref[...] = jnp.where(lane == 0, -sums[0], jnp.where(lane == 1, -sums[1], 0.0))

    seq_lanes = lambda b, g, kb: (b, g)
    key_blk = lambda b, g, kb: (b * nkb + kb, g)
    stats = pl.BlockSpec((None, None, nqb, hg, tq), lambda b, g, kb: (b, g, 0, 0, 0))
    in_specs = [
        pl.BlockSpec((sp.sq, qw), lambda b, g, kb: (b, qo + g)),
        pl.BlockSpec((tk, qw), lambda b, g, kb: (b * nkb + kb, ko + g)),
        pl.BlockSpec((tk, LANES), lambda b, g, kb: (b * nkb + kb, vo + g)),
    ]
    ops = [qa, ka, va]
    if not fox:
        in_specs.append(pl.BlockSpec((sp.sq, LANES), seq_lanes))
        ops.append(o)
    in_specs += [stats, pl.BlockSpec((sp.sq, LANES), seq_lanes)]
    ops += [lse, do]
    out_specs = [pl.BlockSpec((sp.sq, qw), seq_lanes), pl.BlockSpec((tk, qw), key_blk), pl.BlockSpec((tk, LANES), key_blk)]
    out_shape = [
        jax.ShapeDtypeStruct((sp.batch * sp.sq, sp.groups * qw), F32),
        jax.ShapeDtypeStruct((sp.batch * sp.sk, sp.groups * qw), BF16),
        jax.ShapeDtypeStruct((sp.batch * sp.sk, sp.groups * LANES), BF16),
    ]
    scratch = [pltpu.VMEM((nqb, hg, tq), F32), pltpu.VMEM((hg, tk, LANES), F32), pltpu.VMEM((hg, tk, LANES), F32),
               pltpu.VMEM((nqb, qw, tq), F32)]
    if fox:
        in_specs += [
            pl.BlockSpec((tk, hg * LANES), key_blk),
            pl.BlockSpec((sp.sk, qw), lambda b, g, kb: (b, ko + g)),
            pl.BlockSpec((sp.sk, LANES), lambda b, g, kb: (b, vo + g)),
            pl.BlockSpec((sp.sk, hg * LANES), seq_lanes),
        ]
        ops += [cfk, ka, va, cfk]
        out_specs.append(pl.BlockSpec((tk, LANES), key_blk))
        out_shape.append(jax.ShapeDtypeStruct((sp.batch * sp.sk, sp.groups * LANES), F32))
        scratch.append(pltpu.VMEM((hg, tk, LANES), F32))
    blk = _nbytes((sp.sq, qw), BF16) + _nbytes((sp.sq, LANES), BF16) + 2 * _nbytes((sp.sq, LANES), F32)
    blk += _nbytes((sp.sq, qw), F32) + 4 * _nbytes((tk, qw), BF16) + 8 * _nbytes((tq, tk), F32)
    blk += (_nbytes((sp.sk, qw + LANES), BF16) + _nbytes((sp.sk, hg * LANES), F32)) if fox else 0
    call = lambda rider: _pcall(
        body,
        rider=rider,
        name=name,
        grid=(sp.batch, sp.groups, nkb),
        in_specs=in_specs,
        out_specs=out_specs,
        out_shape=out_shape,
        scratch_shapes=scratch,
        compiler_params=_params(blk, _nbytes((sp.sq, LANES), F32) + 4 * _nbytes((tk, LANES), F32)),
    )
    return _hosted(hosts, host, call, ops, len(out_shape))


def _slabwise(name, fn, ins, out_dtypes, rows_per_step=512):
    ins = [a if isinstance(a, tuple) else (a, None) for a in ins]
    n = max(1 if fixed is not None else a.shape[0] for a, fixed in ins)
    rows, cols = ins[0][0].shape[1:]
    tr = min(rows_per_step, rows)
    while rows % tr:
        tr //= 2
    assert tr % 16 == 0 or tr == rows, (name, rows, tr)

    def spec(a, fixed):
        if fixed is not None or a.shape[0] == 1:
            return pl.BlockSpec((None, tr, cols), lambda s, i: (fixed or 0, i, 0))
        return pl.BlockSpec((None, tr, cols), lambda s, i: (s, i, 0))

    def body(*refs):
        vals = fn(*[r[...] for r in refs[:len(ins)]])
        for r, v in zip(refs[len(ins):], vals):
            r[...] = v.astype(r.dtype)

    blk = (len(ins) + len(out_dtypes)) * _nbytes((tr, cols + LANES), F32)
    res = _pcall(
        body,
        name=name,
        grid=(n, rows // tr),
        in_specs=[spec(a, fixed) for a, fixed in ins],
        out_specs=[pl.BlockSpec((None, tr, cols), lambda s, i: (s, i, 0)) for _ in out_dtypes],
        out_shape=[jax.ShapeDtypeStruct((n, rows, cols), dt) for dt in out_dtypes],
        compiler_params=_params(2 * blk),
    )(*[a for a, _ in ins])
    return res


def _adamw_math(w, g, m, v):
    m = ADAM_B1 * m + (1.0 - ADAM_B1) * g
    v = ADAM_B2 * v + (1.0 - ADAM_B2) * jnp.square(g)
    m_hat = m / (1.0 - ADAM_B1 ** ADAM_STEP)
    v_hat = v / (1.0 - ADAM_B2 ** ADAM_STEP)
    delta = -ADAM_LR * (m_hat / (jnp.sqrt(v_hat) + ADAM_EPS) + ADAM_WD * w)
    return delta, m, v


def _run_exchange(name, ex):
    n_in, n_out = len(ex.ins), len(ex.out_shapes)

    def body(*refs):
        copies = _exchange_copies(ex, refs[:n_in], refs[n_in:n_in + n_out], refs[-2], refs[-1])
        for cp in copies:
            cp.start()
        for cp in copies:
            cp.wait()

    any_spec = pl.BlockSpec(memory_space=pl.ANY)
    return _pcall(
        body,
        name=name,
        in_specs=[any_spec] * n_in,
        out_specs=[any_spec] * n_out,
        out_shape=list(ex.out_shapes),
        scratch_shapes=[pltpu.SemaphoreType.DMA((ex.n_copies,))] * 2,
        input_output_aliases=dict(ex.aliases),
    )(*ex.ins)


def _chip(rel=0):
    x, y = lax.axis_index("x"), lax.axis_index("y")
    return 2 * ((1 - x) if rel & 1 else x) + ((1 - y) if rel & 2 else y)


def _gather_ici(shards):
    def plan(in_refs, out_refs):
        c = lax.axis_index("c")
        return [(s.at[c], g.at[_chip(), c], rel) for s, g in zip(in_refs, out_refs) for rel in (1, 2, 3)]

    shapes = tuple(jax.ShapeDtypeStruct((N_CHIPS,) + s.shape, s.dtype) for s in shards)
    return Exchange(tuple(shards), shapes, plan, 3 * len(shards))


def _gather_d2d(got):
    def plan(in_refs, out_refs):
        c = lax.axis_index("c")
        return [(g_in.at[_chip(rel), c], g_out.at[_chip(rel), c], "c")
                for g_in, g_out in zip(in_refs, out_refs) for rel in (1, 2, 3)]

    shapes = tuple(jax.ShapeDtypeStruct(g.shape, g.dtype) for g in got)
    return Exchange(tuple(got), shapes, plan, 3 * len(got), {i: i for i in range(len(got))})


class GatherSet(Hosts):
    def __init__(self, shards, hosts):
        self.shards, self.hosts, self.got = shards, hosts, None

    def rider(self, host):
        if host == self.hosts[0]:
            return _gather_ici(self.shards)
        if host == self.hosts[1]:
            return _gather_d2d(self.got)
        return None

    def done(self, host, outs):
        self.got = outs


def _gather_now(shards):
    got = _run_exchange("gather_weights_ici", _gather_ici(shards))
    return _run_exchange("gather_weights_d2d", _gather_d2d(got))


def _pair_sum(name, g, recv, rows_per_step=512):
    _, _, rows, cols = g.shape
    tr = min(rows_per_step, rows)
    while rows % tr:
        tr //= 2

    def body(g_ref, r_ref, p_ref, own_ref):
        mine = jnp.where(lax.axis_index("c") == 0, g_ref[0], g_ref[1])
        p = mine.astype(F32) + r_ref[...].astype(F32)
        p_ref[...] = p.astype(p_ref.dtype)

        @pl.when(pl.program_id(1) == _chip())
        def _():
            own_ref[...] = p

    slab = pl.BlockSpec((None, tr, cols), lambda i, s: (s, i, 0))
    return _pcall(
        body,
        name=name,
        grid=(rows // tr, N_CHIPS),
        in_specs=[pl.BlockSpec((None, 2, tr, cols), lambda i, s: (s, 0, i, 0)), slab],
        out_specs=[slab, pl.BlockSpec((None, tr, cols), lambda i, s: (0, i, 0))],
        out_shape=[jax.ShapeDtypeStruct((N_CHIPS, rows, cols), BF16), jax.ShapeDtypeStruct((1, rows, cols), F32)],
        compiler_params=_params(2 * 6 * _nbytes((tr, cols + LANES), F32)),
    )(g, recv)


class ReduceLayer(Hosts):
    def __init__(self, tag, grads, hosts):
        self.tag, self.grads, self.pair, self.own, self.total, self.theirs = tag, grads, None, None, None, None
        self.stage_of = dict(zip(hosts, ("swap", "chips", "share")))

    def _swap_halves(self):
        def plan(in_refs, out_refs):
            c = lax.axis_index("c")
            return [(g.at[pl.ds(0, N_CHIPS), 1 - c], r, "c") for g, r in zip(in_refs, out_refs)]

        shapes = tuple(jax.ShapeDtypeStruct((N_CHIPS,) + g.shape[2:], g.dtype) for g in self.grads)
        return Exchange(tuple(self.grads), shapes, plan, len(self.grads))

    def _to_chips(self):
        def plan(in_refs, out_refs):
            return [(p.at[_chip(rel)], r.at[rel - 1], rel) for p, r in zip(in_refs, out_refs) for rel in (1, 2, 3)]

        shapes = tuple(jax.ShapeDtypeStruct((3,) + p.shape[1:], p.dtype) for p in self.pair)
        return Exchange(tuple(self.pair), shapes, plan, 3 * len(self.pair))

    def _share(self):
        def plan(in_refs, out_refs):
            return [(t, r, "c") for t, r in zip(in_refs, out_refs)]

        shapes = tuple(jax.ShapeDtypeStruct(t.shape, F32) for t in self.total)
        return Exchange(tuple(self.total), shapes, plan, len(self.total))

    def rider(self, host):
        stages = {"swap": self._swap_halves, "chips": self._to_chips, "share": self._share}
        return stages[self.stage_of[host]]() if host in self.stage_of else None

    def done(self, host, outs):
        self._after(self.stage_of[host], outs)

    def _after(self, stage, outs):
        if stage == "swap":
            sums = [_pair_sum(f"reduce_pair_sum_{self.tag}_{i}", g, r) for i, (g, r) in enumerate(zip(self.grads, outs))]
            self.pair, self.own = [s[0] for s in sums], [s[1] for s in sums]
        elif stage == "chips":
            self.total = [
                _slabwise(f"reduce_chip_sum_{self.tag}_{i}",
                          lambda a, b, c_, d: (a + b.astype(F32) + c_.astype(F32) + d.astype(F32),),
                          [own, (r, 0), (r, 1), (r, 2)], [F32])[0]
                for i, (own, r) in enumerate(zip(self.own, outs))]
        else:
            self.theirs = outs

    def run_now(self):
        self._after("swap", _run_exchange(f"reduce_pair_{self.tag}", self._swap_halves()))
        self._after("chips", _run_exchange(f"reduce_chips_{self.tag}", self._to_chips()))
        self._after("share", _run_exchange(f"reduce_share_{self.tag}", self._share()))

    def result(self):
        return list(zip(self.total, self.theirs))


def _adamw_layer(name, l, w, m, v, mine, theirs, prev, rows_per_step=256):
    _, _, rows, cols = w.shape
    tr = min(rows_per_step, rows)
    while rows % tr:
        tr //= 2

    def body(w_ref, m_ref, v_ref, mine_ref, theirs_ref, *rest):
        g_ref, d_ref, nm_ref, nv_ref = rest[-4:]
        g = jnp.where(pl.program_id(0) == lax.axis_index("c"), mine_ref[...], theirs_ref[...])
        d, nm, nv = _adamw_math(w_ref[...], g, m_ref[...], v_ref[...])
        g_ref[...], d_ref[...], nm_ref[...], nv_ref[...] = g, d, nm, nv

    half = pl.BlockSpec((None, None, tr, cols), lambda h, i: (l, h, i, 0))
    one = pl.BlockSpec((None, tr, cols), lambda h, i: (0, i, 0))
    kept = [] if prev is None else list(prev)
    return _pcall(
        body,
        name=name,
        grid=(2, rows // tr),
        in_specs=[half, half, half, one, one] + [pl.BlockSpec(memory_space=pl.ANY)] * len(kept),
        out_specs=[half] * 4,
        out_shape=[jax.ShapeDtypeStruct(w.shape, F32)] * 4,
        input_output_aliases={5 + i: i for i in range(len(kept))},
        compiler_params=_params(2 * 9 * _nbytes((tr, cols + LANES), F32)),
    )(w, m, v, mine, theirs, *kept)


def _allreduce_small(v):
    rows = v.shape[0]

    def body(v_ref, sum_ref, all_ref, send_sems, recv_sems, local_sem):
        x, y, c = lax.axis_index("x"), lax.axis_index("y"), lax.axis_index("c")
        sibling = (x, y, 1 - c)
        chips = [(1 - x, y), (x, 1 - y), (1 - x, 1 - y)]

        def slab(px, py, pc):
            return all_ref.at[pl.ds((4 * px + 2 * py + pc) * rows, rows), :]

        def copy(k, block, to, src=None):
            return pltpu.make_async_remote_copy(
                src_ref=slab(*block) if src is None else src, dst_ref=slab(*block), send_sem=send_sems.at[k],
                recv_sem=recv_sems.at[k], device_id=to, device_id_type=MESH)

        mine = pltpu.make_async_copy(v_ref, slab(x, y, c), local_sem)
        mine.start()
        first = [copy(0, (x, y, c), sibling, src=v_ref)]
        first += [copy(1 + j, (x, y, c), (*chip, c), src=v_ref) for j, chip in enumerate(chips)]
        for cp in first:
            cp.start()
        passed = [copy(4 + j, (*chip, c), sibling) for j, chip in enumerate(chips)]
        for j, chip in enumerate(chips):
            copy(1 + j, (*chip, c), (x, y, c)).wait_recv()
            passed[j].start()
        copy(0, (x, y, 1 - c), (x, y, c)).wait_recv()
        for j, chip in enumerate(chips):
            copy(4 + j, (*chip, 1 - c), (x, y, c)).wait_recv()
        for cp in first + passed:
            cp.wait_send()
        mine.wait()
        total = all_ref[pl.ds(0, rows), :]
        for d in range(1, N_DEV):
            total = total + all_ref[pl.ds(d * rows, rows), :]
        sum_ref[...] = total

    vm = pl.BlockSpec(memory_space=pltpu.VMEM)
    return _pcall(
        body,
        name="allreduce_small",
        in_specs=[vm],
        out_specs=vm,
        out_shape=jax.ShapeDtypeStruct((rows, LANES), F32),
        scratch_shapes=[pltpu.VMEM((N_DEV * rows, LANES), F32), pltpu.SemaphoreType.DMA((7,)),
                        pltpu.SemaphoreType.DMA((7,)), pltpu.SemaphoreType.DMA],
    )(v)


def _pad_cols(a, before, total):
    return jnp.pad(a, ((0, 0), (before, total - before - a.shape[1])))


def _layer_weights(cfg, w_in, w_uq, w_ukv):
    w = cfg.width
    qkv, f, cq, ckv, kr, qm, gates = jnp.split(w_in, list(_cumsum(cfg.in_splits))[:-1], axis=1)
    wa = jnp.concatenate([qkv, qm], axis=1)
    ws = jnp.concatenate([_pad_cols(f, 0, LANES), cq, ckv, _pad_cols(kr, MLA_NOPE, LANES)], axis=1)
    wq = jnp.pad(w_uq.reshape(cfg.q_rank, cfg.mla_h, MLA_NOPE + MLA_ROPE), ((0, 0), (0, 0), (0, LANES - MLA_NOPE - MLA_ROPE)))
    wq = wq.reshape(cfg.q_rank, cfg.mla_h * LANES)
    kv = w_ukv.reshape(cfg.kv_rank, cfg.mla_h, MLA_NOPE + MLA_V)
    wk = jnp.pad(kv[:, :, :MLA_NOPE], ((0, 0), (0, 0), (0, LANES - MLA_NOPE))).reshape(cfg.kv_rank, cfg.mla_h * LANES)
    wv = kv[:, :, MLA_NOPE:].reshape(cfg.kv_rank, cfg.mla_h * MLA_V)
    del w
    return wa, gates, ws, wq, wk, wv


def _cumsum(xs):
    out, t = [], 0
    for v in xs:
        t += v
        out.append(t)
    return out


def _layer_weight_grads(cfg, dwa, dwg, dws, dwq, dwk, dwv):
    w, qr, kvr = cfg.width, cfg.q_rank, cfg.kv_rank
    off_kr = LANES + qr + kvr + MLA_NOPE
    dw_in = jnp.concatenate([
        dwa[:, :3 * w], dws[:, :cfg.fox_h], dws[:, LANES:LANES + qr], dws[:, LANES + qr:LANES + qr + kvr],
        dws[:, off_kr:off_kr + MLA_ROPE], dwa[:, 3 * w:], *dwg], axis=1)
    dw_uq = dwq.reshape(qr, cfg.mla_h, LANES)[:, :, :MLA_NOPE + MLA_ROPE].reshape(qr, cfg.mla_h * (MLA_NOPE + MLA_ROPE))
    dw_ukv = jnp.concatenate([dwk.reshape(kvr, cfg.mla_h, LANES)[:, :, :MLA_NOPE], dwv.reshape(kvr, cfg.mla_h, MLA_V)],
                             axis=2).reshape(kvr, cfg.mla_h * (MLA_NOPE + MLA_V))
    return dw_in, dw_uq, dw_ukv


def _attn_specs(cfg, batch):
    t = min(ATTN_TILE, cfg.seq)
    common = dict(batch=batch, sq=cfg.seq, chunk=cfg.chunk, tq=t)
    fox = Attn(sk=cfg.seq, groups=cfg.fox_h // 2, hq=2, hv=2, mode="fox", scale=FOX_DH ** -0.5, tk=t, **common)
    mla = Attn(sk=cfg.seq, groups=cfg.mla_h // 2, hq=1, hv=2, mode="chunk",
               scale=(MLA_NOPE + MLA_ROPE) ** -0.5, tk=t, **common)
    mem = Attn(sk=cfg.n_mem, groups=cfg.mem_h, hq=1, hv=1, mode="none", scale=MEM_DH ** -0.5, tk=cfg.n_mem,
               **dict(common, tq=min(MEM_ATTN_TILE, cfg.seq)))
    return fox, mla, mem


def _small_core(cfg, ps, bf, gq, gkv):
    qr, kvr = cfg.q_rank, cfg.kv_rank
    z = ps[:, :LANES] + bf
    logf = jnp.minimum(z, 0.0) - jnp.log1p(jnp.exp(-jnp.abs(z)))
    nq = _rms(ps[:, LANES:LANES + qr], gq)
    nkv = _rms(ps[:, LANES + qr:LANES + qr + kvr], gkv)
    return logf, nq, nkv


def _layer_fwd(cfg, l, batch, h, hb, mem_b, rope_c, rope_s, hosts, lw, bf_pad, g_cq, g_ckv, ln1, ln2):
    (wa, wg, ws, wq, wk, wv, wmkv), later_weights = lw
    w, d = cfg.width, cfg.d
    fox, mla, mem = _attn_specs(cfg, batch)
    nw = w // LANES
    pa = _mm(f"proj_a_{l}", "nn", [(hb, wa)], [BF16])
    gl = _mm(f"proj_gates_{l}", "nn", [(hb, wg)], [F32])
    ps = _mm(f"proj_small_{l}", "nn", [(hb, ws)], [F32], tn=cfg.small_w)

    def small_fwd(ps_, c_, s_, bf_, gq_, gkv_):
        logf, nq, nkv = _small_core(cfg, ps_, bf_, gq_, gkv_)
        kpe = _rope(ps_[:, cfg.small_w - LANES:], c_, s_)
        return logf, nq, nkv, kpe

    logf, nq, nkv, kpe = _rowwise(
        f"small_fwd_{l}", small_fwd, [ps, rope_c, rope_s], [bf_pad, g_cq, g_ckv],
        [(LANES, F32), (cfg.q_rank, BF16), (cfg.kv_rank, BF16), (LANES, F32)])
    cfk = _forget_cumsum(f"cum_forget_{l}", logf, batch, cfg.seq, cfg.fox_h, fox.tq)

    qf = _mm(f"mla_q_{l}", "nn", [(nq, wq)], [BF16], tn=wq.shape[1],
             epi=lambda acc, c_, s_: (_rope(acc, c_, s_),), row_extras=[rope_c, rope_s])
    kf = _mm(f"mla_k_{l}", "nn", [(nkv, wk)], [BF16], tn=wk.shape[1],
             epi=lambda acc, kp: (acc + jnp.tile(kp, (1, cfg.mla_h)),), row_extras=[kpe])
    vb = _mm(f"mla_v_{l}", "nn", [(nkv, wv)], [BF16])
    mkv = _mm(f"mem_kv_{l}", "nn", [(mem_b, wmkv)], [BF16])

    o_a, lse_a = _attn_fwd(f"fox_fwd_{l}", fox, (pa, 0), (pa, nw), (pa, 2 * nw), cfk,
                           hosts=hosts, host="fox_fwd")
    o_b, lse_b = _attn_fwd(f"mla_fwd_{l}", mla, (qf, 0), (kf, 0), (vb, 0), hosts=hosts, host="mla_fwd")
    o_c, lse_c = _attn_fwd(f"mem_fwd_{l}", mem, (pa, 3 * nw), (mkv, 0), (mkv, nw), hosts=hosts, host="mem_fwd")
    wbr, wout, wff1, wff2 = later_weights()

    def gated_sum(branches, gl_):
        g = jax.nn.sigmoid(gl_)
        return (sum(g[:, n * d:(n + 1) * d] * bp for n, bp in enumerate(branches)),)

    merged = _mm(f"merge_{l}", "nn", [(o, wbr[n]) for n, o in enumerate((o_a, o_b, o_c))], [BF16], tm=256, tn=d,
                 combine=gated_sum, row_extras=[gl])

    def post_ln(acc, res, g_, b_):
        z = cfg.alpha * res + acc
        y = _ln(z, g_, b_)
        return z, y, y

    z1, h1, h1b = _mm(f"out_ln1_{l}", "nn", [(merged, wout)], [F32, F32, BF16], tm=256, tn=d,
                      epi=post_ln, row_extras=[h], bc_extras=list(ln1))
    u, a = _mm(f"ff1_{l}", "nn", [(h1b, wff1)], [BF16, BF16],
               epi=lambda acc: (acc, jnp.square(jnp.maximum(acc, 0.0))))
    z2, h2, h2b = _mm(f"ff2_ln2_{l}", "nn", [(a, wff2)], [F32, F32, BF16], tm=256, tn=d,
                      epi=post_ln, row_extras=[h1], bc_extras=list(ln2))
    saved = dict(hb=hb, pa=pa, gl=gl, ps=ps, nq=nq, nkv=nkv, cfk=cfk, qf=qf, kf=kf, vb=vb, mkv=mkv,
                 o=(o_a, o_b, o_c), lse=(lse_a, lse_b, lse_c), merged=merged, z1=z1, h1b=h1b, u=u, a=a, z2=z2,
                 lw=(wa, wg, ws, wq, wk, wv, wmkv, wbr, wout, wff1, wff2))
    return h2, h2b, saved


def _ln_bwd(name, cfg, ga, gb, z, g, b):
    d = cfg.d

    def fn(*vals):
        if gb is None:
            ga_, z_, g_, b_ = vals
            dy = ga_
        else:
            ga_, gb_, z_, g_, b_ = vals
            dy = ga_ + cfg.alpha * gb_
        _, vjp = jax.vjp(_ln, z_, g_, b_)
        dz, dg, db = vjp(dy)
        return dz, dz, dg, db

    rows = [ga, z] if gb is None else [ga, gb, z]
    return _rowwise(name, fn, rows, [g, b], [(d, F32), (d, BF16)], accs=[(1, d), (1, d)])


def _layer_bwd(cfg, l, batch, ga, gb, sv, mem_b, rope_c, rope_s, hosts, lw, bf_pad, g_cq, g_ckv, ln1, ln2):
    wa, wg, ws, wq, wk, wv, wmkv, wbr, wout, wff1, wff2 = lw
    w, d = cfg.width, cfg.d
    fox, mla, mem = _attn_specs(cfg, batch)
    nw = w // LANES
    gdt = BF16

    dz2, dz2b, dg2, db2 = _ln_bwd(f"ln2_bwd_{l}", cfg, ga, gb, sv["z2"], *ln2)
    du = _mm(f"ff2_bwd_x_{l}", "nt", [(dz2b, wff2)], [BF16],
             epi=lambda acc, u_: (acc * (2.0 * jnp.maximum(u_.astype(F32), 0.0)),), row_extras=[sv["u"]],
             hosts=hosts, host="ff2_bwd_x")
    dwff2 = _mm(f"ff2_bwd_w_{l}", "tn", [(sv["a"], dz2b)], [gdt])
    dwff1 = _mm(f"ff1_bwd_w_{l}", "tn", [(sv["h1b"], du)], [gdt])
    dh1 = _mm(f"ff1_bwd_x_{l}", "nt", [(du, wff1)], [F32])
    dz1, dz1b, dg1, db1 = _ln_bwd(f"ln1_bwd_{l}", cfg, dh1, dz2, sv["z1"], *ln1)
    dmerged = _mm(f"out_bwd_x_{l}", "nt", [(dz1b, wout)], [F32])
    dwout = _mm(f"out_bwd_w_{l}", "tn", [(sv["merged"], dz1b)], [gdt])

    def gated_sum_bwd(branches, dm, gl_):
        g = jax.nn.sigmoid(gl_)
        gates = [g[:, n * d:(n + 1) * d] for n in range(3)]
        d_logits = [dm * bp * gn * (1.0 - gn) for bp, gn in zip(branches, gates)]
        return tuple(d_logits) + tuple(dm * gn for gn in gates)

    res = _mm(f"merge_bwd_{l}", "nn", [(sv["o"][n], wbr[n]) for n in range(3)], [BF16] * 6, tm=256, tn=d,
              combine=gated_sum_bwd, row_extras=[dmerged, sv["gl"]])
    dgls, dbps = res[:3], res[3:]
    dos = [_mm(f"branch_bwd_x_{n}_{l}", "nt", [(dbps[n], wbr[n])], [BF16]) for n in range(3)]
    dwbr = jnp.stack([_mm(f"branch_bwd_w_{n}_{l}", "tn", [(sv["o"][n], dbps[n])], [gdt]) for n in range(3)])
    if hosts is not None:
        hosts.early_grads(dict(w_br=dwbr, w_out=dwout, w_ff1=dwff1, w_ff2=dwff2))

    pa = sv["pa"]
    dq_a, dk_a, dv_a, dcfk = _attn_bwd(f"fox_bwd_{l}", fox, (pa, 0), (pa, nw), (pa, 2 * nw), sv["o"][0], sv["lse"][0],
                                       dos[0], sv["cfk"], hosts=hosts, host="fox_bwd")
    dqf, dkf, dvb = _attn_bwd(f"mla_bwd_{l}", mla, (sv["qf"], 0), (sv["kf"], 0), (sv["vb"], 0), sv["o"][1],
                              sv["lse"][1], dos[1], hosts=hosts, host="mla_bwd")
    dqm, dmk, dmv = _attn_bwd(f"mem_bwd_{l}", mem, (pa, 3 * nw), (sv["mkv"], 0), (sv["mkv"], nw), sv["o"][2],
                              sv["lse"][2], dos[2], hosts=hosts, host="mem_bwd")
    dwmkv = _mm(f"mem_kv_bwd_w_{l}", "tn", [(mem_b, jnp.concatenate([dmk, dmv], axis=1))], [gdt])

    (dq_raw,) = _rowwise(f"mla_q_rope_bwd_{l}", lambda dy, c_, s_: (_rope_t(dy, c_, s_),), [dqf, rope_c, rope_s], [],
                         [(wq.shape[1], BF16)])
    dwq = _mm(f"mla_q_bwd_w_{l}", "tn", [(sv["nq"], dq_raw)], [gdt])
    dnq = _mm(f"mla_q_bwd_x_{l}", "nt", [(dq_raw, wq)], [F32])
    dwk = _mm(f"mla_k_bwd_w_{l}", "tn", [(sv["nkv"], dkf)], [gdt])
    dwv = _mm(f"mla_v_bwd_w_{l}", "tn", [(sv["nkv"], dvb)], [gdt])
    dnkv = _mm(f"mla_kv_bwd_x_{l}", "nt", [(dkf, wk), (dvb, wv)], [F32])

    dlogf = _forget_cumsum_bwd(f"cum_forget_bwd_{l}", dcfk, batch, cfg.seq, cfg.fox_h, fox.tq)

    def small_bwd(ps_, dlogf_, dnq_, dnkv_, dkf_, c_, s_, bf_, gq_, gkv_):
        _, vjp = jax.vjp(functools.partial(_small_core, cfg), ps_, bf_, gq_, gkv_)
        dps, dbf, dgq, dgkv = vjp((dlogf_, dnq_, dnkv_))
        dkpe = dkf_[:, :LANES].astype(F32)
        for hh in range(1, cfg.mla_h):
            dkpe = dkpe + dkf_[:, hh * LANES:(hh + 1) * LANES].astype(F32)
        lane = lax.broadcasted_iota(jnp.int32, (1, LANES), 1)
        dkpe = jnp.where((lane >= MLA_NOPE) & (lane < MLA_NOPE + MLA_ROPE), dkpe, 0.0)
        dkr = _rope_t(dkpe, c_, s_)
        dps = jnp.concatenate([dps[:, :cfg.small_w - LANES], dkr], axis=1)
        return dps, dbf, dgq, dgkv

    dps, dbf, dgq, dgkv = _rowwise(
        f"small_bwd_{l}", small_bwd, [sv["ps"], dlogf, dnq, dnkv, dkf, rope_c, rope_s], [bf_pad, g_cq, g_ckv],
        [(cfg.small_w, BF16)], accs=[(1, LANES), (1, cfg.q_rank), (1, cfg.kv_rank)])

    dpa = jnp.concatenate([dq_a.astype(BF16), dk_a, dv_a, dqm.astype(BF16)], axis=1)
    hb = sv["hb"]
    gate_pairs = [(dgls[n], wg[:, n * d:(n + 1) * d]) for n in range(3)]
    dh = _mm(f"proj_bwd_x_{l}", "nt", [(dpa, wa)] + gate_pairs + [(dps, ws)], [F32], tn=d)
    dwa = _mm(f"proj_a_bwd_w_{l}", "tn", [(hb, dpa)], [gdt])
    dwg = [_mm(f"proj_gates_bwd_w_{n}_{l}", "tn", [(hb, dgls[n])], [gdt]) for n in range(3)]
    dws = _mm(f"proj_small_bwd_w_{l}", "tn", [(hb, dps)], [gdt], tn=cfg.small_w)
    dw_in, dw_uq, dw_ukv = _layer_weight_grads(cfg, dwa, dwg, dws, dwq, dwk, dwv)
    big = dict(w_in=dw_in, w_uq=dw_uq, w_ukv=dw_ukv, w_mem_kv=dwmkv, w_br=dwbr, w_out=dwout,
               w_ff1=dwff1, w_ff2=dwff2)
    small = dict(b_forget=dbf[0, :cfg.fox_h], g_cq=dgq[0], g_ckv=dgkv[0], ln1_g=dg1[0], ln1_b=db1[0],
                 ln2_g=dg2[0], ln2_b=db2[0])
    return dh, dz1, big, small


def _rope_tables(positions):
    inv_freq = ROPE_BASE ** (-jnp.arange(0, MLA_ROPE, 2, dtype=F32) / MLA_ROPE)
    ang = positions.astype(F32).reshape(-1)[:, None] * inv_freq
    cos, sin = jnp.cos(ang), jnp.sin(ang)
    t = ang.shape[0]
    rope_c = jnp.concatenate([jnp.ones((t, MLA_NOPE), F32), cos, cos, jnp.zeros((t, LANES - MLA_NOPE - MLA_ROPE), F32)], axis=1)
    rope_s = jnp.concatenate([jnp.zeros((t, MLA_NOPE), F32), -sin, sin, jnp.zeros((t, LANES - MLA_NOPE - MLA_ROPE), F32)], axis=1)
    return rope_c, rope_s


def _local_step(cfg, x, mem, positions, target, small_w, comm):
    batch = x.shape[0]
    d, depth = cfg.d, cfg.depth
    t = batch * cfg.seq
    x2, tgt = x.reshape(t, d), target.reshape(t, d)
    mem_b = mem.reshape(batch * cfg.n_mem, d).astype(BF16)
    rope_c, rope_s = _rope_tables(positions)
    row = lambda v: v.reshape(1, -1)
    ln_in = (row(small_w["ln_in_g"]), row(small_w["ln_in_b"]))

    h, hb = _rowwise("ln_in", lambda x_, g_, b_: (_ln(x_, g_, b_),) * 2, [x2], list(ln_in), [(d, F32), (d, BF16)])
    layers, saves = [], []
    for l in range(depth):
        first = comm.weights(l, LATE)
        lw = _layer_weights(cfg, first["w_in"], first["w_uq"], first["w_ukv"]) + (first["w_mem_kv"],)
        later = lambda l=l: tuple(comm.weights(l, EARLY).values())
        par = dict(
            bf_pad=jnp.pad(row(small_w["b_forget"][l]), ((0, 0), (0, LANES - cfg.fox_h))),
            g_cq=row(small_w["g_cq"][l]), g_ckv=row(small_w["g_ckv"][l]),
            ln1=(row(small_w["ln1_g"][l]), row(small_w["ln1_b"][l])),
            ln2=(row(small_w["ln2_g"][l]), row(small_w["ln2_b"][l])))
        h, hb, sv = _layer_fwd(cfg, l, batch, h, hb, mem_b, rope_c, rope_s, comm.forward_hosts(l), (lw, later), **par)
        layers.append(dict(par, lw=sv.pop("lw")))
        saves.append(sv)

    def loss_fn(y, tg):
        err = y - tg
        part = 0.5 * jnp.sum(jnp.mean(err * err, axis=-1, keepdims=True), axis=0, keepdims=True)
        return err * (1.0 / d), jnp.broadcast_to(part, (1, LANES))

    ga, loss_acc = _rowwise("loss", loss_fn, [h, tgt], [], [(d, F32)], accs=[(1, LANES)])
    gb = None
    small_g = {k: [None] * depth for k in ("b_forget", "g_cq", "g_ckv", "ln1_g", "ln1_b", "ln2_g", "ln2_b")}
    for l in reversed(range(depth)):
        ga, gb, big, small = _layer_bwd(cfg, l, batch, ga, gb, saves[l], mem_b, rope_c, rope_s,
                                        comm.backward_hosts(l), **layers[l])
        comm.grads(l, big)
        for k, v in small.items():
            small_g[k][l] = v
    dx, _, dg_in, db_in = _ln_bwd("ln_in_bwd", cfg, ga, gb, x2, *ln_in)
    small_g = {k: jnp.stack(v) for k, v in small_g.items()}
    small_g["ln_in_g"], small_g["ln_in_b"] = dg_in[0], db_in[0]
    return loss_acc[0, 0], dx.reshape(x.shape), small_g


BIG = ("w_in", "w_uq", "w_ukv", "w_mem_kv", "w_br", "w_out", "w_ff1", "w_ff2")
SMALL = ("ln_in_g", "ln_in_b", "b_forget", "g_cq", "g_ckv", "ln1_g", "ln1_b", "ln2_g", "ln2_b")
ROW_CUT = ("w_mem_kv", "w_out", "w_ff2")
LATE, EARLY = BIG[:4], BIG[4:]


def _shard_2d(a):
    cols = a.shape[-1]
    rows = a.size // cols
    return a.reshape(2, rows // 2, cols)


def _full_from_slots(name, slots, shard_shape):
    if name in ROW_CUT and len(shard_shape) == 2:
        return slots.reshape((N_CHIPS * shard_shape[0], shard_shape[1]))
    parts = slots.reshape((N_CHIPS,) + shard_shape)
    axis = len(shard_shape) - (2 if name in ROW_CUT else 1)
    return jnp.concatenate([parts[i] for i in range(N_CHIPS)], axis=axis)


def _slots_from_full(name, full, shard_shape):
    if name in ROW_CUT and len(shard_shape) == 2:
        return full.reshape(N_CHIPS, 2, shard_shape[0] // 2, shard_shape[1])
    axis = len(shard_shape) - (2 if name in ROW_CUT else 1)
    parts = jnp.stack(jnp.split(full, N_CHIPS, axis=axis))
    cols = shard_shape[-1]
    return parts.reshape(N_CHIPS, 2, -1, cols)


def _pack_small(cfg, vals):
    flat = jnp.concatenate([vals[k].reshape(-1).astype(F32) for k in SMALL])
    pad = (-flat.shape[0]) % (LANES * LANES)
    return jnp.pad(flat, (0, pad)).reshape(-1, LANES)


def _unpack_small(packed, like):
    flat, out, off = packed.reshape(-1), {}, 0
    for k in SMALL:
        n = like[k].size
        out[k] = flat[off:off + n].reshape(like[k].shape)
        off += n
    return out


class LayerComm:
    def __init__(self, cfg, w, m, v):
        self.cfg, self.w = cfg, w
        self.shards = [{k: _shard_2d(w[k][l].astype(BF16)) for k in BIG} for l in range(cfg.depth)]
        self.got = {LATE: _gather_now([self.shards[0][k] for k in LATE])}
        self.sets = {}
        self.pending = None
        halves = lambda a: a.reshape(a.shape[0], 2, -1, a.shape[-1])
        self.state = {k: [halves(a[k]) for a in (w, m, v)] for k in BIG}
        self.outs = {k: None for k in BIG}

    def weights(self, l, names):
        if names in self.got:
            got = self.got.pop(names)
        elif names is EARLY:
            got = self.sets.pop((l, EARLY[:2])).got + self.sets.pop((l, EARLY[2:])).got
        else:
            got = self.sets.pop((l, names)).got
        mine = (_chip(), 0, 0, 0)
        return {k: _full_from_slots(k, lax.dynamic_update_slice(g, self.shards[l][k][None], mine), self.w[k].shape[1:])
                for k, g in zip(names, got)}

    def forward_hosts(self, l):
        for names, hosts in ((EARLY[2:], ("fox_fwd", "mla_fwd")), (EARLY[:2], ("mla_fwd", "mem_fwd"))):
            self.sets[l, names] = GatherSet([self.shards[l][k] for k in names], hosts)
        if l + 1 < self.cfg.depth:
            self.sets[l + 1, LATE] = GatherSet([self.shards[l + 1][k] for k in LATE], ("mla_fwd", "mem_fwd"))
        return Together([s for (layer, _), s in self.sets.items() if layer in (l, l + 1)])

    def _reduce(self, tag, names, big, hosts):
        return ReduceLayer(tag, [_slots_from_full(k, big[k], self.w[k].shape[1:]) for k in names], hosts)

    def backward_hosts(self, l):
        comm = self

        class Riders(Together):
            def early_grads(self, grads):
                comm.early = comm._reduce(f"{l}e", EARLY, grads, ("fox_bwd", "mla_bwd", "mem_bwd"))
                self.members.append(comm.early)

        return Riders([self.pending[1]] if self.pending else [])

    def _update(self, l, names, reduce):
        for k, (mine, theirs) in zip(names, reduce.result()):
            self.outs[k] = _adamw_layer(f"adamw_{k}_{l}", l, *self.state[k], mine, theirs, self.outs[k])

    def grads(self, l, big):
        self._update(l, EARLY, self.early)
        if self.pending:
            self._update(self.pending[0], LATE, self.pending[1])
        self.pending = (l, self._reduce(f"{l}l", LATE, big, ("ff2_bwd_x", "fox_bwd", "mla_bwd")))

    def finish(self):
        self.pending[1].run_now()
        self._update(self.pending[0], LATE, self.pending[1])
        return {k: tuple(a.reshape(self.w[k].shape) for a in self.outs[k]) for k in BIG}


def _step(cfg, x, mem, positions, target, w, m, v):
    comm = LayerComm(cfg, w, m, v)
    small_w = {k: w[k] for k in SMALL}
    loss_local, dx, small_g = _local_step(cfg, x, mem, positions, target, small_w, comm)
    loss = lax.psum(loss_local, ("x", "y", "c"))
    outs_big = comm.finish()

    g_small = _allreduce_small(_pack_small(cfg, small_g))
    packs = [_pack_small(cfg, {k: d_[k] for k in SMALL}) for d_ in (w, m, v)]
    dl, nm, nv = _slabwise("adamw_small", _adamw_math, [a[None] for a in (packs[0], g_small, packs[1], packs[2])],
                           [F32, F32, F32])
    outs_small = [_unpack_small(a[0] if a.ndim == 3 else a, w) for a in (g_small, dl, nm, nv)]

    names = SMALL[:2] + ("w_in", "b_forget", "w_uq", "g_cq", "w_ukv", "g_ckv", "w_mem_kv", "w_br", "w_out",
                         "ln1_g", "ln1_b", "w_ff1", "w_ff2", "ln2_g", "ln2_b")
    result = [loss, dx]
    for part in range(4):
        for k in names:
            result.append(outs_big[k][part] if k in outs_big else outs_small[part][k])
    return tuple(result)


def kernel(x, mem, positions, ln_in_g, ln_in_b, w_in, b_forget, w_uq, g_cq, w_ukv, g_ckv, w_mem_kv, w_br, w_out, ln1_g, ln1_b, w_ff1, w_ff2, ln2_g, ln2_b, loss_target, m_ln_in_g, m_ln_in_b, m_w_in, m_b_forget, m_w_uq, m_g_cq, m_w_ukv, m_g_ckv, m_w_mem_kv, m_w_br, m_w_out, m_ln1_g, m_ln1_b, m_w_ff1, m_w_ff2, m_ln2_g, m_ln2_b, v_ln_in_g, v_ln_in_b, v_w_in, v_b_forget, v_w_uq, v_g_cq, v_w_ukv, v_g_ckv, v_w_mem_kv, v_w_br, v_w_out, v_ln1_g, v_ln1_b, v_w_ff1, v_w_ff2, v_ln2_g, v_ln2_b):
    w = dict(ln_in_g=ln_in_g, ln_in_b=ln_in_b, w_in=w_in, b_forget=b_forget, w_uq=w_uq, g_cq=g_cq, w_ukv=w_ukv,
             g_ckv=g_ckv, w_mem_kv=w_mem_kv, w_br=w_br, w_out=w_out, ln1_g=ln1_g, ln1_b=ln1_b, w_ff1=w_ff1,
             w_ff2=w_ff2, ln2_g=ln2_g, ln2_b=ln2_b)
    m = dict(ln_in_g=m_ln_in_g, ln_in_b=m_ln_in_b, w_in=m_w_in, b_forget=m_b_forget, w_uq=m_w_uq, g_cq=m_g_cq,
             w_ukv=m_w_ukv, g_ckv=m_g_ckv, w_mem_kv=m_w_mem_kv, w_br=m_w_br, w_out=m_w_out, ln1_g=m_ln1_g,
             ln1_b=m_ln1_b, w_ff1=m_w_ff1, w_ff2=m_w_ff2, ln2_g=m_ln2_g, ln2_b=m_ln2_b)
    v = dict(ln_in_g=v_ln_in_g, ln_in_b=v_ln_in_b, w_in=v_w_in, b_forget=v_b_forget, w_uq=v_w_uq, g_cq=v_g_cq,
             w_ukv=v_w_ukv, g_ckv=v_g_ckv, w_mem_kv=v_w_mem_kv, w_br=v_w_br, w_out=v_w_out, ln1_g=v_ln1_g,
             ln1_b=v_ln1_b, w_ff1=v_w_ff1, w_ff2=v_w_ff2, ln2_g=v_ln2_g, ln2_b=v_ln2_b)
    return _step(Cfg(), x, mem, positions, loss_target, w, m, v)
```

```python
import functools
from typing import NamedTuple

import jax
import jax.numpy as jnp
from jax import lax
from jax.experimental import pallas as pl
from jax.experimental.pallas import tpu as pltpu

F32 = jnp.float32
BF16 = jnp.bfloat16
MESH = pl.DeviceIdType.MESH

LANES = 128
SUBLANES = 8
VMEM_BYTES = 64 * 1024 * 1024
N_CHIPS = 4
N_DEV = 8

FOX_DH = 64
MLA_NOPE = 64
MLA_ROPE = 32
MLA_V = 64
MEM_DH = 128
ROPE_BASE = 10000.0
LN_EPS = 1e-5
RMS_EPS = 1e-6
NEG_INF = -1e30
LOG2E = 1.4426950408889634
ATTN_TILE = 512
MEM_ATTN_TILE = 1024

ADAM_LR = 0.001
ADAM_B1 = 0.9
ADAM_B2 = 0.999
ADAM_EPS = 1e-08
ADAM_WD = 0.01
ADAM_STEP = 10


class Cfg(NamedTuple):
    d: int = 1024
    depth: int = 4
    seq: int = 2048
    chunk: int = 64
    n_mem: int = 256
    fox_h: int = 8
    mla_h: int = 8
    q_rank: int = 384
    kv_rank: int = 256
    mem_h: int = 4
    d_ff: int = 4096

    @property
    def width(self):
        return self.fox_h * FOX_DH

    @property
    def alpha(self):
        return (2 * self.depth) ** 0.25

    @property
    def small_w(self):
        return LANES + self.q_rank + self.kv_rank + LANES

    @property
    def in_splits(self):
        return (3 * self.width, self.fox_h, self.q_rank, self.kv_rank, MLA_ROPE, self.width, 3 * self.d)


class Exchange(NamedTuple):
    ins: tuple
    out_shapes: tuple
    plan: object
    n_copies: int
    aliases: dict = {}


def _peer(rel):
    x, y, c = lax.axis_index("x"), lax.axis_index("y"), lax.axis_index("c")
    if rel == "c":
        return (x, y, 1 - c)
    return ((1 - x) if rel in (1, 3) else x, (1 - y) if rel in (2, 3) else y, c)


def _exchange_copies(ex, in_refs, out_refs, send_sems, recv_sems):
    planned = ex.plan(in_refs, out_refs)
    assert len(planned) == ex.n_copies, len(planned)
    return [pltpu.make_async_remote_copy(src_ref=src, dst_ref=dst, send_sem=send_sems.at[i], recv_sem=recv_sems.at[i],
                                         device_id=_peer(rel), device_id_type=MESH)
            for i, (src, dst, rel) in enumerate(planned)]


def _pcall(body, rider=None, **kw):
    if rider is not None:
        n_in, n_out, grid = len(kw["in_specs"]), len(kw["out_specs"]), kw["grid"]
        n_rin, n_rout = len(rider.ins), len(rider.out_shapes)
        host = body

        def body(*refs):
            ins, rins = refs[:n_in], refs[n_in:n_in + n_rin]
            outs = refs[n_in + n_rin:n_in + n_rin + n_out]
            routs = refs[n_in + n_rin + n_out:n_in + n_rin + n_out + n_rout]
            scratch = refs[n_in + n_rin + n_out + n_rout:-2]
            copies = _exchange_copies(rider, rins, routs, refs[-2], refs[-1])
            first = functools.reduce(jnp.logical_and, [pl.program_id(a) == 0 for a in range(len(grid))])
            last = functools.reduce(jnp.logical_and, [pl.program_id(a) == n - 1 for a, n in enumerate(grid)])

            @pl.when(first)
            def _():
                for cp in copies:
                    cp.start()

            host(*ins, *outs, *scratch)

            @pl.when(last)
            def _():
                for cp in copies:
                    cp.wait()

        any_spec = pl.BlockSpec(memory_space=pl.ANY)
        sems = [pltpu.SemaphoreType.DMA((rider.n_copies,))] * 2
        kw = dict(
            kw,
            in_specs=list(kw["in_specs"]) + [any_spec] * n_rin,
            out_specs=list(kw["out_specs"]) + [any_spec] * n_rout,
            out_shape=list(kw["out_shape"]) + list(rider.out_shapes),
            scratch_shapes=list(kw.get("scratch_shapes", ())) + sems,
            input_output_aliases={**kw.get("input_output_aliases", {}),
                                  **{n_in + i: n_out + o for i, o in rider.aliases.items()}},
        )
    call = pl.pallas_call(body, **kw)
    return lambda *ops: call(*[pltpu.with_memory_space_constraint(o, pltpu.HBM) for o in ops])


class Hosts:
    def rider(self, host):
        return None

    def done(self, host, outs):
        pass

    def early_grads(self, grads):
        pass


def _merge_exchanges(exs):
    n_ins = [len(e.ins) for e in exs]
    n_outs = [len(e.out_shapes) for e in exs]

    def plan(in_refs, out_refs):
        copies, i, o = [], 0, 0
        for e, ni, no in zip(exs, n_ins, n_outs):
            copies += e.plan(in_refs[i:i + ni], out_refs[o:o + no])
            i, o = i + ni, o + no
        return copies

    aliases, i, o = {}, 0, 0
    for e, ni, no in zip(exs, n_ins, n_outs):
        aliases.update({i + a: o + b for a, b in e.aliases.items()})
        i, o = i + ni, o + no
    return Exchange(sum((e.ins for e in exs), ()), sum((e.out_shapes for e in exs), ()), plan,
                    sum(e.n_copies for e in exs), aliases)


class Together(Hosts):
    def __init__(self, members):
        self.members, self.active = list(members), []

    def rider(self, host):
        self.active = [(m, r) for m, r in ((m, m.rider(host)) for m in self.members) if r is not None]
        if not self.active:
            return None
        return _merge_exchanges([r for _, r in self.active])

    def done(self, host, outs):
        for m, r in self.active:
            m.done(host, outs[:len(r.out_shapes)])
            outs = outs[len(r.out_shapes):]


def _hosted(hosts, host, call, ops, n_results):
    rider = hosts.rider(host) if hosts is not None else None
    if rider is None:
        return call(None)(*ops)
    res = call(rider)(*ops, *rider.ins)
    hosts.done(host, list(res[n_results:]))
    return res[:n_results]


def _nbytes(shape, dtype):
    n = 1
    for s in shape:
        n *= s
    return n * jnp.dtype(dtype).itemsize


def _tile(dim, target):
    if dim <= target:
        return dim
    t = target - target % LANES
    while t >= LANES:
        if dim % t == 0:
            return t
        t -= LANES
    return dim


def _params(block_bytes, scratch_bytes=0):
    est = 2 * block_bytes + scratch_bytes + 24 * 1024 * 1024
    return pltpu.CompilerParams(vmem_limit_bytes=int(min(max(est, 32 * 1024 * 1024), VMEM_BYTES - 4 * 1024 * 1024)))


_DIMS = {"nn": (((1,), (0,)), ((), ())), "nt": (((1,), (1,)), ((), ())), "tn": (((0,), (0,)), ((), ()))}


MM_TILE = 1024
MM_BLOCK_BYTES = 16 * 1024 * 1024


def _mm_tiles(mode, pairs, out_dtypes, m, n, tm, tn):
    fixed_m, fixed_n = tm is not None, tn is not None
    tm, tn = _tile(m, tm or MM_TILE), _tile(n, tn or MM_TILE)

    def block_bytes(tm_, tn_):
        total = sum(_nbytes((tm_, tn_), dt) for dt in out_dtypes)
        for a, b in pairs:
            k = a.shape[0] if mode == "tn" else a.shape[1]
            total += _nbytes((k, tm_), a.dtype) + _nbytes((k, tn_), b.dtype)
        return total

    while block_bytes(tm, tn) > MM_BLOCK_BYTES:
        if not fixed_m and tm >= tn and tm > 2 * LANES:
            tm = _tile(m, tm // 2)
        elif not fixed_n and tn > 2 * LANES:
            tn = _tile(n, tn // 2)
        elif not fixed_m and tm > 2 * LANES:
            tm = _tile(m, tm // 2)
        else:
            break
    return tm, tn


def _mm(name, mode, pairs, out_dtypes, tm=None, tn=None, epi=None, combine=None, row_extras=(), bc_extras=(),
        hosts=None, host=None):
    a0, b0 = pairs[0]
    m = a0.shape[1] if mode == "tn" else a0.shape[0]
    n = b0.shape[0] if mode == "nt" else b0.shape[1]
    tm, tn = _mm_tiles(mode, pairs, out_dtypes, m, n, tm, tn)
    in_specs, ops, blk = [], [], 0
    for a, b in pairs:
        if mode == "tn":
            k = a.shape[0]
            sa, sha = pl.BlockSpec((k, tm), lambda i, j: (0, i)), (k, tm)
        else:
            k = a.shape[1]
            sa, sha = pl.BlockSpec((tm, k), lambda i, j: (i, 0)), (tm, k)
        if mode == "nt":
            sb, shb = pl.BlockSpec((tn, k), lambda i, j: (j, 0)), (tn, k)
        else:
            sb, shb = pl.BlockSpec((k, tn), lambda i, j: (0, j)), (k, tn)
        in_specs += [sa, sb]
        ops += [a, b]
        blk += _nbytes(sha, a.dtype) + _nbytes(shb, b.dtype)
    for e in row_extras:
        w = e.shape[1]
        if w == n:
            in_specs.append(pl.BlockSpec((tm, tn), lambda i, j: (i, j)))
            blk += _nbytes((tm, tn), e.dtype)
        else:
            in_specs.append(pl.BlockSpec((tm, w), lambda i, j: (i, 0)))
            blk += _nbytes((tm, w), e.dtype)
        ops.append(e)
    for e in bc_extras:
        r, w = e.shape
        if w == n:
            in_specs.append(pl.BlockSpec((r, tn), lambda i, j: (0, j)))
        else:
            in_specs.append(pl.BlockSpec((r, w), lambda i, j: (0, 0)))
        blk += _nbytes((r, w), e.dtype)
        ops.append(e)
    npairs, nrow, nbc, nout = len(pairs), len(row_extras), len(bc_extras), len(out_dtypes)
    dims = _DIMS[mode]

    def body(*refs):
        prods = [lax.dot_general(refs[2 * p][...].astype(BF16), refs[2 * p + 1][...].astype(BF16), dims,
                                 preferred_element_type=F32) for p in range(npairs)]
        ex = [r[...] for r in refs[2 * npairs:2 * npairs + nrow + nbc]]
        if combine is not None:
            outs = combine(prods, *ex)
        else:
            acc = functools.reduce(lambda s, d: s + d, prods)
            outs = (acc,) if epi is None else epi(acc, *ex)
        for o_ref, o in zip(refs[2 * npairs + nrow + nbc:], outs):
            o_ref[...] = o.astype(o_ref.dtype)

    blk += sum(_nbytes((tm, tn), dt) for dt in out_dtypes) + 2 * _nbytes((tm, tn), F32)
    call = lambda rider: _pcall(
        body,
        rider=rider,
        name=name,
        grid=(m // tm, n // tn),
        in_specs=in_specs,
        out_specs=[pl.BlockSpec((tm, tn), lambda i, j: (i, j)) for _ in range(nout)],
        out_shape=[jax.ShapeDtypeStruct((m, n), dt) for dt in out_dtypes],
        compiler_params=_params(blk),
    )
    res = _hosted(hosts, host, call, ops, nout)
    return res[0] if nout == 1 else res


def _rowwise(name, fn, row_ins, bc_ins, outs, accs=(), tm=256):
    rows = row_ins[0].shape[0]
    tm = min(tm, rows)
    assert rows % tm == 0
    nrow, nbc, nout, nacc = len(row_ins), len(bc_ins), len(outs), len(accs)
    in_specs = [pl.BlockSpec((tm, a.shape[1]), lambda i: (i, 0)) for a in row_ins]
    in_specs += [pl.BlockSpec(a.shape, lambda i: (0, 0)) for a in bc_ins]
    out_specs = [pl.BlockSpec((tm, w), lambda i: (i, 0)) for w, _ in outs]
    out_specs += [pl.BlockSpec(s, lambda i: (0, 0)) for s in accs]
    out_shape = [jax.ShapeDtypeStruct((rows, w), dt) for w, dt in outs]
    out_shape += [jax.ShapeDtypeStruct(s, F32) for s in accs]

    def body(*refs):
        vals = fn(*[r[...] for r in refs[:nrow + nbc]])
        o_refs = refs[nrow + nbc:]
        for r, v in zip(o_refs[:nout], vals[:nout]):
            r[...] = v.astype(r.dtype)
        if nacc:
            @pl.when(pl.program_id(0) == 0)
            def _():
                for r in o_refs[nout:]:
                    r[...] = jnp.zeros(r.shape, F32)

            for r, v in zip(o_refs[nout:], vals[nout:]):
                r[...] += v

    blk = sum(_nbytes((tm, a.shape[1]), a.dtype) for a in row_ins) + sum(_nbytes(a.shape, a.dtype) for a in bc_ins)
    blk += sum(_nbytes((tm, w), dt) for w, dt in outs) + sum(_nbytes(s, F32) for s in accs)
    res = _pcall(
        body,
        name=name,
        grid=(rows // tm,),
        in_specs=in_specs,
        out_specs=out_specs,
        out_shape=out_shape,
        compiler_params=_params(2 * blk),
    )(*row_ins, *bc_ins)
    return res


def _ln(z, g, b):
    mu = jnp.mean(z, axis=-1, keepdims=True)
    zc = z - mu
    var = jnp.mean(zc * zc, axis=-1, keepdims=True)
    return zc * lax.rsqrt(var + LN_EPS) * g + b


def _rms(x, g):
    return x * lax.rsqrt(jnp.mean(x * x, axis=-1, keepdims=True) + RMS_EPS) * g


def _rope_swap(x):
    w = x.shape[1]
    lane = lax.broadcasted_iota(jnp.int32, (1, w), 1) % LANES
    from_left = pltpu.roll(x, 16, 1)
    from_right = pltpu.roll(x, w - 16, 1)
    lo = (lane >= MLA_NOPE) & (lane < MLA_NOPE + 16)
    hi = (lane >= MLA_NOPE + 16) & (lane < MLA_NOPE + 32)
    return jnp.where(hi, from_left, jnp.where(lo, from_right, 0.0))


def _rope(x, cos_t, sin_t):
    nh = x.shape[1] // LANES
    ct, st = jnp.tile(cos_t, (1, nh)), jnp.tile(sin_t, (1, nh))
    return x * ct + _rope_swap(x) * st


def _rope_t(dy, cos_t, sin_t):
    nh = dy.shape[1] // LANES
    ct, st = jnp.tile(cos_t, (1, nh)), jnp.tile(sin_t, (1, nh))
    return dy * ct + _rope_swap(dy * st)


def _block_cumsum(v, carry, reverse):
    tb = v.shape[0]
    r = lax.broadcasted_iota(jnp.int32, (tb, tb), 0)
    c = lax.broadcasted_iota(jnp.int32, (tb, tb), 1)
    tri = jnp.where((c >= r) if reverse else (c <= r), 1.0, 0.0).astype(BF16)
    hi = v.astype(BF16)
    r1 = v - hi.astype(F32)
    mid = r1.astype(BF16)
    lo = (r1 - mid.astype(F32)).astype(BF16)
    out = carry + sum(jnp.dot(tri, p, preferred_element_type=F32) for p in (hi, mid, lo))
    return out, (out[0:1, :] if reverse else out[tb - 1:tb, :])


def _forget_cumsum(name, logf, batch, seq, heads, tb):
    nb = seq // tb

    def body(x_ref, keys_ref, carry):
        @pl.when(pl.program_id(1) == 0)
        def _():
            carry[...] = jnp.zeros(carry.shape, F32)

        out, carry[...] = _block_cumsum(x_ref[...], carry[...], False)
        out = out * LOG2E
        keys_ref[...] = jnp.concatenate([jnp.broadcast_to(out[:, h:h + 1], (tb, LANES)) for h in range(heads)], axis=1)

    return _pcall(
        body,
        name=name,
        grid=(batch, nb),
        in_specs=[pl.BlockSpec((tb, LANES), lambda b, i: (b * nb + i, 0))],
        out_specs=pl.BlockSpec((tb, heads * LANES), lambda b, i: (b * nb + i, 0)),
        out_shape=jax.ShapeDtypeStruct((batch * seq, heads * LANES), F32),
        scratch_shapes=[pltpu.VMEM((1, LANES), F32)],
        compiler_params=_params(4 * tb * (heads + 2) * LANES * 4),
    )(logf)


def _forget_cumsum_bwd(name, dcf, batch, seq, heads, tb):
    nb = seq // tb

    def body(x_ref, o_ref, carry):
        @pl.when(pl.program_id(1) == 0)
        def _():
            carry[...] = jnp.zeros(carry.shape, F32)

        lane = lax.broadcasted_iota(jnp.int32, (1, LANES), 1)
        v = jnp.zeros((tb, LANES), F32)
        for g in range(heads // 2):
            blk = x_ref[:, g * LANES:(g + 1) * LANES]
            moved = pltpu.roll(blk, 2 * g, 1) if g else blk
            v = v + jnp.where((lane >= 2 * g) & (lane < 2 * g + 2), moved, 0.0)
        o_ref[...], carry[...] = _block_cumsum(v, carry[...], True)

    return _pcall(
        body,
        name=name,
        grid=(batch, nb),
        in_specs=[pl.BlockSpec((tb, (heads // 2) * LANES), lambda b, i: (b * nb + nb - 1 - i, 0))],
        out_specs=pl.BlockSpec((tb, LANES), lambda b, i: (b * nb + nb - 1 - i, 0)),
        out_shape=jax.ShapeDtypeStruct((batch * seq, LANES), F32),
        scratch_shapes=[pltpu.VMEM((1, LANES), F32)],
        compiler_params=_params(4 * tb * (heads // 2 + 1) * LANES * 4),
    )(dcf)


class Attn(NamedTuple):
    batch: int
    sq: int
    sk: int
    groups: int
    hq: int
    hv: int
    mode: str
    scale: float
    chunk: int
    tq: int
    tk: int

    @property
    def hg(self):
        return self.hv

    @property
    def qw(self):
        return LANES * self.hg // self.hq

    @property
    def dv(self):
        return LANES // self.hv


def _head_lanes(j, dv):
    lane = lax.broadcasted_iota(jnp.int32, (1, LANES), 1)
    return (lane >= j * dv) & (lane < (j + 1) * dv)


def _head_q(sp, j, q_blk):
    if sp.hq == 2:
        return jnp.where(_head_lanes(j, FOX_DH), q_blk, jnp.zeros_like(q_blk))
    return q_blk[:, LANES * j:LANES * (j + 1)]


def _head_rows(sp, j):
    return slice(j * sp.dv, (j + 1) * sp.dv) if sp.hg == 2 else slice(None)


def _scores_t(sp, j, k_c, q_j, cfk_rep, k0, q0, masked):
    tk, tq = k_c.shape[0], q_j.shape[0]
    k_j = k_c if sp.hq == 2 else k_c[:, LANES * j:LANES * (j + 1)]
    st = lax.dot_general(k_j, q_j, _DIMS["nt"], preferred_element_type=F32) * (sp.scale * LOG2E)
    if sp.mode == "fox":
        st = st - jnp.tile(cfk_rep[:, LANES * j:LANES * (j + 1)], (1, tq // LANES))
    if masked:
        kidx = k0 + lax.broadcasted_iota(jnp.int32, (tk, tq), 0)
        qidx = q0 + lax.broadcasted_iota(jnp.int32, (tk, tq), 1)
        if sp.mode == "chunk":
            shift = sp.chunk.bit_length() - 1
            kidx, qidx = jnp.right_shift(kidx, shift), jnp.right_shift(qidx, shift)
        st = jnp.where(kidx <= qidx, st, NEG_INF)
    return st


def _attn_fwd(name, sp, q, k, v, cfk=None, hosts=None, host=None):
    (qa, qo), (ka, ko), (va, vo) = q, k, v
    tq, tk, hg, qw = sp.tq, sp.tk, sp.hg, sp.qw
    nqb, nkc = sp.sq // tq, sp.sk // tk
    fox, causal = sp.mode == "fox", sp.mode != "none"
    assert sp.sq % tq == 0 and sp.sk % tk == 0 and (not causal or (tq == tk and sp.sq == sp.sk))

    def body(*refs):
        if fox:
            q_ref, k_ref, v_ref, cfk_ref, o_ref, lse_ref, acc_scr = refs
        else:
            q_ref, k_ref, v_ref, o_ref, lse_ref, acc_scr = refs
        i = pl.program_id(2)
        q0 = i * tq
        q_blk = q_ref[...]
        qs = [_head_q(sp, j, q_blk) for j in range(hg)]
        acc_scr[...] = jnp.zeros(acc_scr.shape, F32)

        def chunk(kc, carry, masked):
            ms, ls = carry
            k0 = pl.multiple_of(kc * tk, tk)
            k_c = k_ref[pl.ds(k0, tk), :]
            v_c = v_ref[pl.ds(k0, tk), :]
            new_m, new_l = [], []
            for j in range(hg):
                st = _scores_t(sp, j, k_c, qs[j], cfk_ref[pl.ds(k0, tk), :] if fox else None, k0, q0, masked)
                m_new = jnp.maximum(ms[j], jnp.max(st, axis=0, keepdims=True))
                alpha = jnp.exp2(ms[j] - m_new)
                pt = jnp.exp2(st - m_new)
                new_m.append(m_new)
                new_l.append(alpha * ls[j] + jnp.sum(pt, axis=0, keepdims=True))
                pv = lax.dot_general(v_c, pt.astype(BF16), _DIMS["tn"], preferred_element_type=F32)
                r = _head_rows(sp, j)
                acc_scr[r, :] = acc_scr[r, :] * alpha + pv[r, :]
            return tuple(new_m), tuple(new_l)

        carry = (tuple(jnp.full((1, tq), NEG_INF, F32) for _ in range(hg)),
                 tuple(jnp.zeros((1, tq), F32) for _ in range(hg)))
        if causal:
            carry = lax.fori_loop(0, i, functools.partial(chunk, masked=False), carry)
            ms, ls = chunk(i, carry, True)
        else:
            ms, ls = lax.fori_loop(0, nkc, functools.partial(chunk, masked=False), carry)
        for j in range(hg):
            r = _head_rows(sp, j)
            acc_scr[r, :] = acc_scr[r, :] / ls[j]
            lse_ref[j:j + 1, :] = ms[j] + jnp.log(ls[j]) * LOG2E
        o_ref[...] = acc_scr[...].T

    in_specs = [
        pl.BlockSpec((tq, qw), lambda b, g, i: (b * nqb + i, qo + g)),
        pl.BlockSpec((sp.sk, qw), lambda b, g, i: (b, ko + g)),
        pl.BlockSpec((sp.sk, LANES), lambda b, g, i: (b, vo + g)),
    ]
    ops = [qa, ka, va]
    stat_blk = pl.BlockSpec((None, None, None, hg, tq), lambda b, g, i: (b, g, i, 0, 0))
    if fox:
        in_specs.append(pl.BlockSpec((sp.sk, hg * LANES), lambda b, g, i: (b, g)))
        ops.append(cfk)
    blk = _nbytes((tq, qw), BF16) + _nbytes((sp.sk, qw + LANES), BF16) + 2 * _nbytes((tq, LANES), F32)
    blk += _nbytes((sp.sk, hg * LANES), F32) + 6 * _nbytes((tk, tq), F32)
    call = lambda rider: _pcall(
        body,
        rider=rider,
        name=name,
        grid=(sp.batch, sp.groups, nqb),
        in_specs=in_specs,
        out_specs=[pl.BlockSpec((tq, LANES), lambda b, g, i: (b * nqb + i, g)), stat_blk],
        out_shape=[
            jax.ShapeDtypeStruct((sp.batch * sp.sq, sp.groups * LANES), F32),
            jax.ShapeDtypeStruct((sp.batch, sp.groups, nqb, hg, tq), F32),
        ],
        scratch_shapes=[pltpu.VMEM((LANES, tq), F32)],
        compiler_params=_params(blk, tq * LANES * 4),
    )
    return _hosted(hosts, host, call, ops, 2)


def _attn_bwd(name, sp, q, k, v, o, lse, do, cfk=None, hosts=None, host=None):
    (qa, qo), (ka, ko), (va, vo) = q, k, v
    tq, tk, hg, qw, dv = sp.tq, sp.tk, sp.hg, sp.qw, sp.dv
    nqb, nkb = sp.sq // tq, sp.sk // tk
    fox, causal = sp.mode == "fox", sp.mode != "none"
    assert sp.sq % tq == 0 and sp.sk % tk == 0 and (not causal or (tq == tk and sp.sq == sp.sk))

    def body(*refs):
        if fox:
            (q_ref, k_ref, v_ref, lse_ref, do_ref, cfk_ref, kall_ref, vall_ref, cfkall_ref,
             dq_ref, dk_ref, dv_ref, dcf_ref, delta_scr, dk_scr, dv_scr, dqt_scr, dcf_scr) = refs
        else:
            (q_ref, k_ref, v_ref, o_ref, lse_ref, do_ref,
             dq_ref, dk_ref, dv_ref, delta_scr, dk_scr, dv_scr, dqt_scr) = refs
        kb = pl.program_id(2)
        k0 = kb * tk
        heads = [_head_lanes(j, dv) for j in range(hg)]

        def head_do(j, do_c):
            return jnp.where(heads[j], do_c, jnp.zeros_like(do_c)) if hg == 2 else do_c

        def probs_t(j, k_c, v_c, q_c, do_c, i, cf_keys, c0, masked):
            st = _scores_t(sp, j, k_c, _head_q(sp, j, q_c), cf_keys, c0, i * tq, masked)
            pt = jnp.exp2(st - lse_ref[i][j:j + 1, :])
            dpt = lax.dot_general(v_c, head_do(j, do_c), _DIMS["nt"], preferred_element_type=F32)
            return pt, dpt

        @pl.when(kb == 0)
        def _():
            dqt_scr[...] = jnp.zeros(dqt_scr.shape, F32)

            def fill(i, carry):
                r0 = pl.multiple_of(i * tq, tq)
                do_c = do_ref[pl.ds(r0, tq), :]
                if fox:
                    q_c = q_ref[pl.ds(r0, tq), :]

                    def keys(kc, acc, masked):
                        c0 = pl.multiple_of(kc * tk, tk)
                        out = []
                        for j in range(hg):
                            pt, dpt = probs_t(j, kall_ref[pl.ds(c0, tk), :], vall_ref[pl.ds(c0, tk), :], q_c, do_c, i,
                                              cfkall_ref[pl.ds(c0, tk), :], c0, masked)
                            out.append(acc[j] + jnp.sum(pt * dpt, axis=0, keepdims=True))
                        return tuple(out)

                    d = lax.fori_loop(0, i, functools.partial(keys, masked=False),
                                      tuple(jnp.zeros((1, tq), F32) for _ in range(hg)))
                    d = keys(i, d, True)
                    for j in range(hg):
                        delta_scr[i, j:j + 1, :] = d[j]
                else:
                    prod_t = (do_c.astype(F32) * o_ref[pl.ds(r0, tq), :]).T
                    for j in range(hg):
                        delta_scr[i, j:j + 1, :] = jnp.sum(prod_t[_head_rows(sp, j), :], axis=0, keepdims=True)
                return carry

            lax.fori_loop(0, nqb, fill, 0)

        k_blk = k_ref[...]
        v_blk = v_ref[...]
        k_t = k_blk.astype(F32).T.astype(BF16)
        dk_scr[...] = jnp.zeros(dk_scr.shape, F32)
        dv_scr[...] = jnp.zeros(dv_scr.shape, F32)
        if fox:
            dcf_scr[...] = jnp.zeros(dcf_scr.shape, F32)

        def qblock(i, carry, masked):
            r0 = pl.multiple_of(i * tq, tq)
            q_c = q_ref[pl.ds(r0, tq), :]
            do_c = do_ref[pl.ds(r0, tq), :]
            for j in range(hg):
                pt, dpt = probs_t(j, k_blk, v_blk, q_c, do_c, i, cfk_ref[...] if fox else None, k0, masked)
                dst = pt * (dpt - delta_scr[i][j:j + 1, :])
                if fox:
                    part = dst[:, :LANES]
                    for t in range(1, tq // LANES):
                        part = part + dst[:, t * LANES:(t + 1) * LANES]
                    dcf_scr[j] += part
                ds_b = (dst * sp.scale).astype(BF16)
                dv_scr[j] += jnp.dot(pt.astype(BF16), do_c, preferred_element_type=F32)
                dk_scr[j] += jnp.dot(ds_b, q_c if sp.hq == 2 else _head_q(sp, j, q_c), preferred_element_type=F32)
                if sp.hq == 2:
                    r = pl.ds(j * FOX_DH, FOX_DH)
                    dqt_scr[i, r, :] += jnp.dot(k_t[j * FOX_DH:(j + 1) * FOX_DH, :], ds_b, preferred_element_type=F32)
                else:
                    r = pl.ds(j * LANES, LANES)
                    dqt_scr[i, r, :] += jnp.dot(k_t[j * LANES:(j + 1) * LANES, :], ds_b, preferred_element_type=F32)
            return carry

        if causal:
            qblock(kb, 0, True)
            lax.fori_loop(kb + 1, nqb, functools.partial(qblock, masked=False), 0)
        else:
            lax.fori_loop(0, nqb, functools.partial(qblock, masked=False), 0)

        @pl.when(kb == nkb - 1)
        def _():
            def untranspose(i, carry):
                dq_ref[pl.ds(pl.multiple_of(i * tq, tq), tq), :] = dqt_scr[i].T
                return carry

            lax.fori_loop(0, nqb, untranspose, 0)

        if hg == 2:
            dv_ref[...] = jnp.where(heads[0], dv_scr[0], dv_scr[1]).astype(dv_ref.dtype)
        else:
            dv_ref[...] = dv_scr[0].astype(dv_ref.dtype)
        if sp.hq == 2:
            dk_ref[...] = jnp.where(_head_lanes(0, FOX_DH), dk_scr[0], dk_scr[1]).astype(dk_ref.dtype)
        elif hg == 2:
            dk_ref[...] = jnp.concatenate([dk_scr[0], dk_scr[1]], axis=1).astype(dk_ref.dtype)
        else:
            dk_ref[...] = dk_scr[0].astype(dk_ref.dtype)
        if fox:
            lane = lax.broadcasted_iota(jnp.int32, (1, LANES), 1)
            sums = [jnp.sum(dcf_scr[j], axis=1, keepdims=True) for j in range(hg)]
            dcf_ref[...] = jnp.where(lane == 0, -sums[0], jnp.where(lane == 1, -sums[1], 0.0))

    seq_lanes = lambda b, g, kb: (b, g)
    key_blk = lambda b, g, kb: (b * nkb + kb, g)
    stats = pl.BlockSpec((None, None, nqb, hg, tq), lambda b, g, kb: (b, g, 0, 0, 0))
    in_specs = [
        pl.BlockSpec((sp.sq, qw), lambda b, g, kb: (b, qo + g)),
        pl.BlockSpec((tk, qw), lambda b, g, kb: (b * nkb + kb, ko + g)),
        pl.BlockSpec((tk, LANES), lambda b, g, kb: (b * nkb + kb, vo + g)),
    ]
    ops = [qa, ka, va]
    if not fox:
        in_specs.append(pl.BlockSpec((sp.sq, LANES), seq_lanes))
        ops.append(o)
    in_specs += [stats, pl.BlockSpec((sp.sq, LANES), seq_lanes)]
    ops += [lse, do]
    out_specs = [pl.BlockSpec((sp.sq, qw), seq_lanes), pl.BlockSpec((tk, qw), key_blk), pl.BlockSpec((tk, LANES), key_blk)]
    out_shape = [
        jax.ShapeDtypeStruct((sp.batch * sp.sq, sp.groups * qw), F32),
        jax.ShapeDtypeStruct((sp.batch * sp.sk, sp.groups * qw), BF16),
        jax.ShapeDtypeStruct((sp.batch * sp.sk, sp.groups * LANES), BF16),
    ]
    scratch = [pltpu.VMEM((nqb, hg, tq), F32), pltpu.VMEM((hg, tk, LANES), F32), pltpu.VMEM((hg, tk, LANES), F32),
               pltpu.VMEM((nqb, qw, tq), F32)]
    if fox:
        in_specs += [
            pl.BlockSpec((tk, hg * LANES), key_blk),
            pl.BlockSpec((sp.sk, qw), lambda b, g, kb: (b, ko + g)),
            pl.BlockSpec((sp.sk, LANES), lambda b, g, kb: (b, vo + g)),
            pl.BlockSpec((sp.sk, hg * LANES), seq_lanes),
        ]
        ops += [cfk, ka, va, cfk]
        out_specs.append(pl.BlockSpec((tk, LANES), key_blk))
        out_shape.append(jax.ShapeDtypeStruct((sp.batch * sp.sk, sp.groups * LANES), F32))
        scratch.append(pltpu.VMEM((hg, tk, LANES), F32))
    blk = _nbytes((sp.sq, qw), BF16) + _nbytes((sp.sq, LANES), BF16) + 2 * _nbytes((sp.sq, LANES), F32)
    blk += _nbytes((sp.sq, qw), F32) + 4 * _nbytes((tk, qw), BF16) + 8 * _nbytes((tq, tk), F32)
    blk += (_nbytes((sp.sk, qw + LANES), BF16) + _nbytes((sp.sk, hg * LANES), F32)) if fox else 0
    call = lambda rider: _pcall(
        body,
        rider=rider,
        name=name,
        grid=(sp.batch, sp.groups, nkb),
        in_specs=in_specs,
        out_specs=out_specs,
        out_shape=out_shape,
        scratch_shapes=scratch,
        compiler_params=_params(blk, _nbytes((sp.sq, LANES), F32) + 4 * _nbytes((tk, LANES), F32)),
    )
    return _hosted(hosts, host, call, ops, len(out_shape))


def _slabwise(name, fn, ins, out_dtypes, rows_per_step=512):
    ins = [a if isinstance(a, tuple) else (a, None) for a in ins]
    n = max(1 if fixed is not None else a.shape[0] for a, fixed in ins)
    rows, cols = ins[0][0].shape[1:]
    tr = min(rows_per_step, rows)
    while rows % tr:
        tr //= 2
    assert tr % 16 == 0 or tr == rows, (name, rows, tr)

    def spec(a, fixed):
        if fixed is not None or a.shape[0] == 1:
            return pl.BlockSpec((None, tr, cols), lambda s, i: (fixed or 0, i, 0))
        return pl.BlockSpec((None, tr, cols), lambda s, i: (s, i, 0))

    def body(*refs):
        vals = fn(*[r[...] for r in refs[:len(ins)]])
        for r, v in zip(refs[len(ins):], vals):
            r[...] = v.astype(r.dtype)

    blk = (len(ins) + len(out_dtypes)) * _nbytes((tr, cols + LANES), F32)
    res = _pcall(
        body,
        name=name,
        grid=(n, rows // tr),
        in_specs=[spec(a, fixed) for a, fixed in ins],
        out_specs=[pl.BlockSpec((None, tr, cols), lambda s, i: (s, i, 0)) for _ in out_dtypes],
        out_shape=[jax.ShapeDtypeStruct((n, rows, cols), dt) for dt in out_dtypes],
        compiler_params=_params(2 * blk),
    )(*[a for a, _ in ins])
    return res


def _adamw_math(w, g, m, v):
    m = ADAM_B1 * m + (1.0 - ADAM_B1) * g
    v = ADAM_B2 * v + (1.0 - ADAM_B2) * jnp.square(g)
    m_hat = m / (1.0 - ADAM_B1 ** ADAM_STEP)
    v_hat = v / (1.0 - ADAM_B2 ** ADAM_STEP)
    delta = -ADAM_LR * (m_hat / (jnp.sqrt(v_hat) + ADAM_EPS) + ADAM_WD * w)
    return delta, m, v


def _run_exchange(name, ex):
    n_in, n_out = len(ex.ins), len(ex.out_shapes)

    def body(*refs):
        copies = _exchange_copies(ex, refs[:n_in], refs[n_in:n_in + n_out], refs[-2], refs[-1])
        for cp in copies:
            cp.start()
        for cp in copies:
            cp.wait()

    any_spec = pl.BlockSpec(memory_space=pl.ANY)
    return _pcall(
        body,
        name=name,
        in_specs=[any_spec] * n_in,
        out_specs=[any_spec] * n_out,
        out_shape=list(ex.out_shapes),
        scratch_shapes=[pltpu.SemaphoreType.DMA((ex.n_copies,))] * 2,
        input_output_aliases=dict(ex.aliases),
    )(*ex.ins)


def _chip(rel=0):
    x, y = lax.axis_index("x"), lax.axis_index("y")
    return 2 * ((1 - x) if rel & 1 else x) + ((1 - y) if rel & 2 else y)


def _gather_ici(shards):
    def plan(in_refs, out_refs):
        c = lax.axis_index("c")
        return [(s.at[c], g.at[_chip(), c], rel) for s, g in zip(in_refs, out_refs) for rel in (1, 2, 3)]

    shapes = tuple(jax.ShapeDtypeStruct((N_CHIPS,) + s.shape, s.dtype) for s in shards)
    return Exchange(tuple(shards), shapes, plan, 3 * len(shards))


def _gather_d2d(got):
    def plan(in_refs, out_refs):
        c = lax.axis_index("c")
        return [(g_in.at[_chip(rel), c], g_out.at[_chip(rel), c], "c")
                for g_in, g_out in zip(in_refs, out_refs) for rel in (1, 2, 3)]

    shapes = tuple(jax.ShapeDtypeStruct(g.shape, g.dtype) for g in got)
    return Exchange(tuple(got), shapes, plan, 3 * len(got), {i: i for i in range(len(got))})


class GatherSet(Hosts):
    def __init__(self, shards, hosts):
        self.shards, self.hosts, self.got = shards, hosts, None

    def rider(self, host):
        if host == self.hosts[0]:
            return _gather_ici(self.shards)
        if host == self.hosts[1]:
            return _gather_d2d(self.got)
        return None

    def done(self, host, outs):
        self.got = outs


def _gather_now(shards):
    got = _run_exchange("gather_weights_ici", _gather_ici(shards))
    return _run_exchange("gather_weights_d2d", _gather_d2d(got))


def _pair_sum(name, g, recv, rows_per_step=512):
    _, _, rows, cols = g.shape
    tr = min(rows_per_step, rows)
    while rows % tr:
        tr //= 2

    def body(g_ref, r_ref, p_ref, own_ref):
        mine = jnp.where(lax.axis_index("c") == 0, g_ref[0], g_ref[1])
        p = mine.astype(F32) + r_ref[...].astype(F32)
        p_ref[...] = p.astype(p_ref.dtype)

        @pl.when(pl.program_id(1) == _chip())
        def _():
            own_ref[...] = p

    slab = pl.BlockSpec((None, tr, cols), lambda i, s: (s, i, 0))
    return _pcall(
        body,
        name=name,
        grid=(rows // tr, N_CHIPS),
        in_specs=[pl.BlockSpec((None, 2, tr, cols), lambda i, s: (s, 0, i, 0)), slab],
        out_specs=[slab, pl.BlockSpec((None, tr, cols), lambda i, s: (0, i, 0))],
        out_shape=[jax.ShapeDtypeStruct((N_CHIPS, rows, cols), BF16), jax.ShapeDtypeStruct((1, rows, cols), F32)],
        compiler_params=_params(2 * 6 * _nbytes((tr, cols + LANES), F32)),
    )(g, recv)


class ReduceLayer(Hosts):
    def __init__(self, tag, grads, hosts):
        self.tag, self.grads, self.pair, self.own, self.total, self.theirs = tag, grads, None, None, None, None
        self.stage_of = dict(zip(hosts, ("swap", "chips", "share")))

    def _swap_halves(self):
        def plan(in_refs, out_refs):
            c = lax.axis_index("c")
            return [(g.at[pl.ds(0, N_CHIPS), 1 - c], r, "c") for g, r in zip(in_refs, out_refs)]

        shapes = tuple(jax.ShapeDtypeStruct((N_CHIPS,) + g.shape[2:], g.dtype) for g in self.grads)
        return Exchange(tuple(self.grads), shapes, plan, len(self.grads))

    def _to_chips(self):
        def plan(in_refs, out_refs):
            return [(p.at[_chip(rel)], r.at[rel - 1], rel) for p, r in zip(in_refs, out_refs) for rel in (1, 2, 3)]

        shapes = tuple(jax.ShapeDtypeStruct((3,) + p.shape[1:], p.dtype) for p in self.pair)
        return Exchange(tuple(self.pair), shapes, plan, 3 * len(self.pair))

    def _share(self):
        def plan(in_refs, out_refs):
            return [(t, r, "c") for t, r in zip(in_refs, out_refs)]

        shapes = tuple(jax.ShapeDtypeStruct(t.shape, F32) for t in self.total)
        return Exchange(tuple(self.total), shapes, plan, len(self.total))

    def rider(self, host):
        stages = {"swap": self._swap_halves, "chips": self._to_chips, "share": self._share}
        return stages[self.stage_of[host]]() if host in self.stage_of else None

    def done(self, host, outs):
        self._after(self.stage_of[host], outs)

    def _after(self, stage, outs):
        if stage == "swap":
            sums = [_pair_sum(f"reduce_pair_sum_{self.tag}_{i}", g, r) for i, (g, r) in enumerate(zip(self.grads, outs))]
            self.pair, self.own = [s[0] for s in sums], [s[1] for s in sums]
        elif stage == "chips":
            self.total = [
                _slabwise(f"reduce_chip_sum_{self.tag}_{i}",
                          lambda a, b, c_, d: (a + b.astype(F32) + c_.astype(F32) + d.astype(F32),),
                          [own, (r, 0), (r, 1), (r, 2)], [F32])[0]
                for i, (own, r) in enumerate(zip(self.own, outs))]
        else:
            self.theirs = outs

    def run_now(self):
        self._after("swap", _run_exchange(f"reduce_pair_{self.tag}", self._swap_halves()))
        self._after("chips", _run_exchange(f"reduce_chips_{self.tag}", self._to_chips()))
        self._after("share", _run_exchange(f"reduce_share_{self.tag}", self._share()))

    def result(self):
        return list(zip(self.total, self.theirs))


def _adamw_layer(name, l, w, m, v, mine, theirs, prev, rows_per_step=256):
    _, rows2, cols = w.shape
    rows = rows2 // 2
    tr = min(rows_per_step, rows)
    while rows % tr:
        tr //= 2
    steps = rows // tr

    def body(w_ref, m_ref, v_ref, mine_ref, theirs_ref, *rest):
        g_ref, d_ref, nm_ref, nv_ref = rest[-4:]
        g = jnp.where(pl.program_id(0) == lax.axis_index("c"), mine_ref[...], theirs_ref[...])
        d, nm, nv = _adamw_math(w_ref[...], g, m_ref[...], v_ref[...])
        g_ref[...], d_ref[...], nm_ref[...], nv_ref[...] = g, d, nm, nv

    half = pl.BlockSpec((None, tr, cols), lambda h, i: (l, h * steps + i, 0))
    one = pl.BlockSpec((None, tr, cols), lambda h, i: (0, i, 0))
    kept = [] if prev is None else list(prev)
    return _pcall(
        body,
        name=name,
        grid=(2, rows // tr),
        in_specs=[half, half, half, one, one] + [pl.BlockSpec(memory_space=pl.ANY)] * len(kept),
        out_specs=[half] * 4,
        out_shape=[jax.ShapeDtypeStruct(w.shape, F32)] * 4,
        input_output_aliases={5 + i: i for i in range(len(kept))},
        compiler_params=_params(2 * 9 * _nbytes((tr, cols + LANES), F32)),
    )(w, m, v, mine, theirs, *kept)


def _allreduce_small(v):
    rows = v.shape[0]

    def body(v_ref, sum_ref, all_ref, send_sems, recv_sems, local_sem):
        x, y, c = lax.axis_index("x"), lax.axis_index("y"), lax.axis_index("c")
        sibling = (x, y, 1 - c)
        chips = [(1 - x, y), (x, 1 - y), (1 - x, 1 - y)]

        def slab(px, py, pc):
            return all_ref.at[pl.ds((4 * px + 2 * py + pc) * rows, rows), :]

        def copy(k, block, to, src=None):
            return pltpu.make_async_remote_copy(
                src_ref=slab(*block) if src is None else src, dst_ref=slab(*block), send_sem=send_sems.at[k],
                recv_sem=recv_sems.at[k], device_id=to, device_id_type=MESH)

        mine = pltpu.make_async_copy(v_ref, slab(x, y, c), local_sem)
        mine.start()
        first = [copy(0, (x, y, c), sibling, src=v_ref)]
        first += [copy(1 + j, (x, y, c), (*chip, c), src=v_ref) for j, chip in enumerate(chips)]
        for cp in first:
            cp.start()
        passed = [copy(4 + j, (*chip, c), sibling) for j, chip in enumerate(chips)]
        for j, chip in enumerate(chips):
            copy(1 + j, (*chip, c), (x, y, c)).wait_recv()
            passed[j].start()
        copy(0, (x, y, 1 - c), (x, y, c)).wait_recv()
        for j, chip in enumerate(chips):
            copy(4 + j, (*chip, 1 - c), (x, y, c)).wait_recv()
        for cp in first + passed:
            cp.wait_send()
        mine.wait()
        total = all_ref[pl.ds(0, rows), :]
        for d in range(1, N_DEV):
            total = total + all_ref[pl.ds(d * rows, rows), :]
        sum_ref[...] = total

    vm = pl.BlockSpec(memory_space=pltpu.VMEM)
    return _pcall(
        body,
        name="allreduce_small",
        in_specs=[vm],
        out_specs=vm,
        out_shape=jax.ShapeDtypeStruct((rows, LANES), F32),
        scratch_shapes=[pltpu.VMEM((N_DEV * rows, LANES), F32), pltpu.SemaphoreType.DMA((7,)),
                        pltpu.SemaphoreType.DMA((7,)), pltpu.SemaphoreType.DMA],
    )(v)


def _pad_cols(a, before, total):
    return jnp.pad(a, ((0, 0), (before, total - before - a.shape[1])))


def _layer_weights(cfg, w_in, w_uq, w_ukv):
    w = cfg.width
    qkv, f, cq, ckv, kr, qm, gates = jnp.split(w_in, list(_cumsum(cfg.in_splits))[:-1], axis=1)
    wa = jnp.concatenate([qkv, qm], axis=1)
    ws = jnp.concatenate([_pad_cols(f, 0, LANES), cq, ckv, _pad_cols(kr, MLA_NOPE, LANES)], axis=1)
    wq = jnp.pad(w_uq.reshape(cfg.q_rank, cfg.mla_h, MLA_NOPE + MLA_ROPE), ((0, 0), (0, 0), (0, LANES - MLA_NOPE - MLA_ROPE)))
    wq = wq.reshape(cfg.q_rank, cfg.mla_h * LANES)
    kv = w_ukv.reshape(cfg.kv_rank, cfg.mla_h, MLA_NOPE + MLA_V)
    wk = jnp.pad(kv[:, :, :MLA_NOPE], ((0, 0), (0, 0), (0, LANES - MLA_NOPE))).reshape(cfg.kv_rank, cfg.mla_h * LANES)
    wv = kv[:, :, MLA_NOPE:].reshape(cfg.kv_rank, cfg.mla_h * MLA_V)
    del w
    return wa, gates, ws, wq, wk, wv


def _cumsum(xs):
    out, t = [], 0
    for v in xs:
        t += v
        out.append(t)
    return out


def _layer_weight_grads(cfg, dwa, dwg, dws, dwq, dwk, dwv):
    w, qr, kvr = cfg.width, cfg.q_rank, cfg.kv_rank
    off_kr = LANES + qr + kvr + MLA_NOPE
    dw_in = jnp.concatenate([
        dwa[:, :3 * w], dws[:, :cfg.fox_h], dws[:, LANES:LANES + qr], dws[:, LANES + qr:LANES + qr + kvr],
        dws[:, off_kr:off_kr + MLA_ROPE], dwa[:, 3 * w:], *dwg], axis=1)
    dw_uq = dwq.reshape(qr, cfg.mla_h, LANES)[:, :, :MLA_NOPE + MLA_ROPE].reshape(qr, cfg.mla_h * (MLA_NOPE + MLA_ROPE))
    dw_ukv = jnp.concatenate([dwk.reshape(kvr, cfg.mla_h, LANES)[:, :, :MLA_NOPE], dwv.reshape(kvr, cfg.mla_h, MLA_V)],
                             axis=2).reshape(kvr, cfg.mla_h * (MLA_NOPE + MLA_V))
    return dw_in, dw_uq, dw_ukv


def _attn_specs(cfg, batch):
    t = min(ATTN_TILE, cfg.seq)
    common = dict(batch=batch, sq=cfg.seq, chunk=cfg.chunk, tq=t)
    fox = Attn(sk=cfg.seq, groups=cfg.fox_h // 2, hq=2, hv=2, mode="fox", scale=FOX_DH ** -0.5, tk=t, **common)
    mla = Attn(sk=cfg.seq, groups=cfg.mla_h // 2, hq=1, hv=2, mode="chunk",
               scale=(MLA_NOPE + MLA_ROPE) ** -0.5, tk=t, **common)
    mem = Attn(sk=cfg.n_mem, groups=cfg.mem_h, hq=1, hv=1, mode="none", scale=MEM_DH ** -0.5, tk=cfg.n_mem,
               **dict(common, tq=min(MEM_ATTN_TILE, cfg.seq)))
    return fox, mla, mem


def _small_core(cfg, ps, bf, gq, gkv):
    qr, kvr = cfg.q_rank, cfg.kv_rank
    z = ps[:, :LANES] + bf
    logf = jnp.minimum(z, 0.0) - jnp.log1p(jnp.exp(-jnp.abs(z)))
    nq = _rms(ps[:, LANES:LANES + qr], gq)
    nkv = _rms(ps[:, LANES + qr:LANES + qr + kvr], gkv)
    return logf, nq, nkv


def _layer_fwd(cfg, l, batch, h, hb, mem_b, rope_c, rope_s, hosts, lw, bf_pad, g_cq, g_ckv, ln1, ln2):
    (wa, wg, ws, wq, wk, wv, wmkv), later_weights = lw
    w, d = cfg.width, cfg.d
    fox, mla, mem = _attn_specs(cfg, batch)
    nw = w // LANES
    pa = _mm(f"proj_a_{l}", "nn", [(hb, wa)], [BF16])
    gl = _mm(f"proj_gates_{l}", "nn", [(hb, wg)], [F32])
    ps = _mm(f"proj_small_{l}", "nn", [(hb, ws)], [F32], tn=cfg.small_w)

    def small_fwd(ps_, c_, s_, bf_, gq_, gkv_):
        logf, nq, nkv = _small_core(cfg, ps_, bf_, gq_, gkv_)
        kpe = _rope(ps_[:, cfg.small_w - LANES:], c_, s_)
        return logf, nq, nkv, kpe

    logf, nq, nkv, kpe = _rowwise(
        f"small_fwd_{l}", small_fwd, [ps, rope_c, rope_s], [bf_pad, g_cq, g_ckv],
        [(LANES, F32), (cfg.q_rank, BF16), (cfg.kv_rank, BF16), (LANES, F32)])
    cfk = _forget_cumsum(f"cum_forget_{l}", logf, batch, cfg.seq, cfg.fox_h, fox.tq)

    qf = _mm(f"mla_q_{l}", "nn", [(nq, wq)], [BF16], tn=wq.shape[1],
             epi=lambda acc, c_, s_: (_rope(acc, c_, s_),), row_extras=[rope_c, rope_s])
    kf = _mm(f"mla_k_{l}", "nn", [(nkv, wk)], [BF16], tn=wk.shape[1],
             epi=lambda acc, kp: (acc + jnp.tile(kp, (1, cfg.mla_h)),), row_extras=[kpe])
    vb = _mm(f"mla_v_{l}", "nn", [(nkv, wv)], [BF16])
    mkv = _mm(f"mem_kv_{l}", "nn", [(mem_b, wmkv)], [BF16])

    o_a, lse_a = _attn_fwd(f"fox_fwd_{l}", fox, (pa, 0), (pa, nw), (pa, 2 * nw), cfk,
                           hosts=hosts, host="fox_fwd")
    o_b, lse_b = _attn_fwd(f"mla_fwd_{l}", mla, (qf, 0), (kf, 0), (vb, 0), hosts=hosts, host="mla_fwd")
    o_c, lse_c = _attn_fwd(f"mem_fwd_{l}", mem, (pa, 3 * nw), (mkv, 0), (mkv, nw), hosts=hosts, host="mem_fwd")
    wbr, wout, wff1, wff2 = later_weights()

    def gated_sum(branches, gl_):
        g = jax.nn.sigmoid(gl_)
        return (sum(g[:, n * d:(n + 1) * d] * bp for n, bp in enumerate(branches)),)

    merged = _mm(f"merge_{l}", "nn", [(o, wbr[n]) for n, o in enumerate((o_a, o_b, o_c))], [BF16], tm=256, tn=d,
                 combine=gated_sum, row_extras=[gl])

    def post_ln(acc, res, g_, b_):
        z = cfg.alpha * res + acc
        y = _ln(z, g_, b_)
        return z, y, y

    z1, h1, h1b = _mm(f"out_ln1_{l}", "nn", [(merged, wout)], [F32, F32, BF16], tm=256, tn=d,
                      epi=post_ln, row_extras=[h], bc_extras=list(ln1))
    u, a = _mm(f"ff1_{l}", "nn", [(h1b, wff1)], [BF16, BF16],
               epi=lambda acc: (acc, jnp.square(jnp.maximum(acc, 0.0))))
    z2, h2, h2b = _mm(f"ff2_ln2_{l}", "nn", [(a, wff2)], [F32, F32, BF16], tm=256, tn=d,
                      epi=post_ln, row_extras=[h1], bc_extras=list(ln2))
    saved = dict(hb=hb, pa=pa, gl=gl, ps=ps, nq=nq, nkv=nkv, cfk=cfk, qf=qf, kf=kf, vb=vb, mkv=mkv,
                 o=(o_a, o_b, o_c), lse=(lse_a, lse_b, lse_c), merged=merged, z1=z1, h1b=h1b, u=u, a=a, z2=z2,
                 lw=(wa, wg, ws, wq, wk, wv, wmkv, wbr, wout, wff1, wff2))
    return h2, h2b, saved


def _ln_bwd(name, cfg, ga, gb, z, g, b):
    d = cfg.d

    def fn(*vals):
        if gb is None:
            ga_, z_, g_, b_ = vals
            dy = ga_
        else:
            ga_, gb_, z_, g_, b_ = vals
            dy = ga_ + cfg.alpha * gb_
        _, vjp = jax.vjp(_ln, z_, g_, b_)
        dz, dg, db = vjp(dy)
        return dz, dz, dg, db

    rows = [ga, z] if gb is None else [ga, gb, z]
    return _rowwise(name, fn, rows, [g, b], [(d, F32), (d, BF16)], accs=[(1, d), (1, d)])


def _layer_bwd(cfg, l, batch, ga, gb, sv, mem_b, rope_c, rope_s, hosts, lw, bf_pad, g_cq, g_ckv, ln1, ln2):
    wa, wg, ws, wq, wk, wv, wmkv, wbr, wout, wff1, wff2 = lw
    w, d = cfg.width, cfg.d
    fox, mla, mem = _attn_specs(cfg, batch)
    nw = w // LANES
    gdt = BF16

    dz2, dz2b, dg2, db2 = _ln_bwd(f"ln2_bwd_{l}", cfg, ga, gb, sv["z2"], *ln2)
    du = _mm(f"ff2_bwd_x_{l}", "nt", [(dz2b, wff2)], [BF16],
             epi=lambda acc, u_: (acc * (2.0 * jnp.maximum(u_.astype(F32), 0.0)),), row_extras=[sv["u"]],
             hosts=hosts, host="ff2_bwd_x")
    dwff2 = _mm(f"ff2_bwd_w_{l}", "tn", [(sv["a"], dz2b)], [gdt])
    dwff1 = _mm(f"ff1_bwd_w_{l}", "tn", [(sv["h1b"], du)], [gdt])
    dh1 = _mm(f"ff1_bwd_x_{l}", "nt", [(du, wff1)], [F32])
    dz1, dz1b, dg1, db1 = _ln_bwd(f"ln1_bwd_{l}", cfg, dh1, dz2, sv["z1"], *ln1)
    dmerged = _mm(f"out_bwd_x_{l}", "nt", [(dz1b, wout)], [F32])
    dwout = _mm(f"out_bwd_w_{l}", "tn", [(sv["merged"], dz1b)], [gdt])

    def gated_sum_bwd(branches, dm, gl_):
        g = jax.nn.sigmoid(gl_)
        gates = [g[:, n * d:(n + 1) * d] for n in range(3)]
        d_logits = [dm * bp * gn * (1.0 - gn) for bp, gn in zip(branches, gates)]
        return tuple(d_logits) + tuple(dm * gn for gn in gates)

    res = _mm(f"merge_bwd_{l}", "nn", [(sv["o"][n], wbr[n]) for n in range(3)], [BF16] * 6, tm=256, tn=d,
              combine=gated_sum_bwd, row_extras=[dmerged, sv["gl"]])
    dgls, dbps = res[:3], res[3:]
    dos = [_mm(f"branch_bwd_x_{n}_{l}", "nt", [(dbps[n], wbr[n])], [BF16]) for n in range(3)]
    dwbr = jnp.stack([_mm(f"branch_bwd_w_{n}_{l}", "tn", [(sv["o"][n], dbps[n])], [gdt]) for n in range(3)])
    if hosts is not None:
        hosts.early_grads(dict(w_br=dwbr, w_out=dwout, w_ff1=dwff1, w_ff2=dwff2))

    pa = sv["pa"]
    dq_a, dk_a, dv_a, dcfk = _attn_bwd(f"fox_bwd_{l}", fox, (pa, 0), (pa, nw), (pa, 2 * nw), sv["o"][0], sv["lse"][0],
                                       dos[0], sv["cfk"], hosts=hosts, host="fox_bwd")
    dqf, dkf, dvb = _attn_bwd(f"mla_bwd_{l}", mla, (sv["qf"], 0), (sv["kf"], 0), (sv["vb"], 0), sv["o"][1],
                              sv["lse"][1], dos[1], hosts=hosts, host="mla_bwd")
    dqm, dmk, dmv = _attn_bwd(f"mem_bwd_{l}", mem, (pa, 3 * nw), (sv["mkv"], 0), (sv["mkv"], nw), sv["o"][2],
                              sv["lse"][2], dos[2], hosts=hosts, host="mem_bwd")
    dwmkv = _mm(f"mem_kv_bwd_w_{l}", "tn", [(mem_b, jnp.concatenate([dmk, dmv], axis=1))], [gdt])

    (dq_raw,) = _rowwise(f"mla_q_rope_bwd_{l}", lambda dy, c_, s_: (_rope_t(dy, c_, s_),), [dqf, rope_c, rope_s], [],
                         [(wq.shape[1], BF16)])
    dwq = _mm(f"mla_q_bwd_w_{l}", "tn", [(sv["nq"], dq_raw)], [gdt])
    dnq = _mm(f"mla_q_bwd_x_{l}", "nt", [(dq_raw, wq)], [F32])
    dwk = _mm(f"mla_k_bwd_w_{l}", "tn", [(sv["nkv"], dkf)], [gdt])
    dwv = _mm(f"mla_v_bwd_w_{l}", "tn", [(sv["nkv"], dvb)], [gdt])
    dnkv = _mm(f"mla_kv_bwd_x_{l}", "nt", [(dkf, wk), (dvb, wv)], [F32])

    dlogf = _forget_cumsum_bwd(f"cum_forget_bwd_{l}", dcfk, batch, cfg.seq, cfg.fox_h, fox.tq)

    def small_bwd(ps_, dlogf_, dnq_, dnkv_, dkf_, c_, s_, bf_, gq_, gkv_):
        _, vjp = jax.vjp(functools.partial(_small_core, cfg), ps_, bf_, gq_, gkv_)
        dps, dbf, dgq, dgkv = vjp((dlogf_, dnq_, dnkv_))
        dkpe = dkf_[:, :LANES].astype(F32)
        for hh in range(1, cfg.mla_h):
            dkpe = dkpe + dkf_[:, hh * LANES:(hh + 1) * LANES].astype(F32)
        lane = lax.broadcasted_iota(jnp.int32, (1, LANES), 1)
        dkpe = jnp.where((lane >= MLA_NOPE) & (lane < MLA_NOPE + MLA_ROPE), dkpe, 0.0)
        dkr = _rope_t(dkpe, c_, s_)
        dps = jnp.concatenate([dps[:, :cfg.small_w - LANES], dkr], axis=1)
        return dps, dbf, dgq, dgkv

    dps, dbf, dgq, dgkv = _rowwise(
        f"small_bwd_{l}", small_bwd, [sv["ps"], dlogf, dnq, dnkv, dkf, rope_c, rope_s], [bf_pad, g_cq, g_ckv],
        [(cfg.small_w, BF16)], accs=[(1, LANES), (1, cfg.q_rank), (1, cfg.kv_rank)])

    dpa = jnp.concatenate([dq_a.astype(BF16), dk_a, dv_a, dqm.astype(BF16)], axis=1)
    hb = sv["hb"]
    gate_pairs = [(dgls[n], wg[:, n * d:(n + 1) * d]) for n in range(3)]
    dh = _mm(f"proj_bwd_x_{l}", "nt", [(dpa, wa)] + gate_pairs + [(dps, ws)], [F32], tn=d)
    dwa = _mm(f"proj_a_bwd_w_{l}", "tn", [(hb, dpa)], [gdt])
    dwg = [_mm(f"proj_gates_bwd_w_{n}_{l}", "tn", [(hb, dgls[n])], [gdt]) for n in range(3)]
    dws = _mm(f"proj_small_bwd_w_{l}", "tn", [(hb, dps)], [gdt], tn=cfg.small_w)
    dw_in, dw_uq, dw_ukv = _layer_weight_grads(cfg, dwa, dwg, dws, dwq, dwk, dwv)
    big = dict(w_in=dw_in, w_uq=dw_uq, w_ukv=dw_ukv, w_mem_kv=dwmkv, w_br=dwbr, w_out=dwout,
               w_ff1=dwff1, w_ff2=dwff2)
    small = dict(b_forget=dbf[0, :cfg.fox_h], g_cq=dgq[0], g_ckv=dgkv[0], ln1_g=dg1[0], ln1_b=db1[0],
                 ln2_g=dg2[0], ln2_b=db2[0])
    return dh, dz1, big, small


def _rope_tables(positions):
    inv_freq = ROPE_BASE ** (-jnp.arange(0, MLA_ROPE, 2, dtype=F32) / MLA_ROPE)
    ang = positions.astype(F32).reshape(-1)[:, None] * inv_freq
    cos, sin = jnp.cos(ang), jnp.sin(ang)
    t = ang.shape[0]
    rope_c = jnp.concatenate([jnp.ones((t, MLA_NOPE), F32), cos, cos, jnp.zeros((t, LANES - MLA_NOPE - MLA_ROPE), F32)], axis=1)
    rope_s = jnp.concatenate([jnp.zeros((t, MLA_NOPE), F32), -sin, sin, jnp.zeros((t, LANES - MLA_NOPE - MLA_ROPE), F32)], axis=1)
    return rope_c, rope_s


def _local_step(cfg, x, mem, positions, target, small_w, comm):
    batch = x.shape[0]
    d, depth = cfg.d, cfg.depth
    t = batch * cfg.seq
    x2, tgt = x.reshape(t, d), target.reshape(t, d)
    mem_b = mem.reshape(batch * cfg.n_mem, d).astype(BF16)
    rope_c, rope_s = _rope_tables(positions)
    row = lambda v: v.reshape(1, -1)
    ln_in = (row(small_w["ln_in_g"]), row(small_w["ln_in_b"]))

    h, hb = _rowwise("ln_in", lambda x_, g_, b_: (_ln(x_, g_, b_),) * 2, [x2], list(ln_in), [(d, F32), (d, BF16)])
    layers, saves = [], []
    for l in range(depth):
        first = comm.weights(l, LATE)
        lw = _layer_weights(cfg, first["w_in"], first["w_uq"], first["w_ukv"]) + (first["w_mem_kv"],)
        later = lambda l=l: tuple(comm.weights(l, EARLY).values())
        par = dict(
            bf_pad=jnp.pad(row(small_w["b_forget"][l]), ((0, 0), (0, LANES - cfg.fox_h))),
            g_cq=row(small_w["g_cq"][l]), g_ckv=row(small_w["g_ckv"][l]),
            ln1=(row(small_w["ln1_g"][l]), row(small_w["ln1_b"][l])),
            ln2=(row(small_w["ln2_g"][l]), row(small_w["ln2_b"][l])))
        h, hb, sv = _layer_fwd(cfg, l, batch, h, hb, mem_b, rope_c, rope_s, comm.forward_hosts(l), (lw, later), **par)
        layers.append(dict(par, lw=sv.pop("lw")))
        saves.append(sv)

    def loss_fn(y, tg):
        err = y - tg
        part = 0.5 * jnp.sum(jnp.mean(err * err, axis=-1, keepdims=True), axis=0, keepdims=True)
        return err * (1.0 / d), jnp.broadcast_to(part, (1, LANES))

    ga, loss_acc = _rowwise("loss", loss_fn, [h, tgt], [], [(d, F32)], accs=[(1, LANES)])
    gb = None
    small_g = {k: [None] * depth for k in ("b_forget", "g_cq", "g_ckv", "ln1_g", "ln1_b", "ln2_g", "ln2_b")}
    for l in reversed(range(depth)):
        ga, gb, big, small = _layer_bwd(cfg, l, batch, ga, gb, saves[l], mem_b, rope_c, rope_s,
                                        comm.backward_hosts(l), **layers[l])
        comm.grads(l, big)
        for k, v in small.items():
            small_g[k][l] = v
    dx, _, dg_in, db_in = _ln_bwd("ln_in_bwd", cfg, ga, gb, x2, *ln_in)
    small_g = {k: jnp.stack(v) for k, v in small_g.items()}
    small_g["ln_in_g"], small_g["ln_in_b"] = dg_in[0], db_in[0]
    return loss_acc[0, 0], dx.reshape(x.shape), small_g


BIG = ("w_in", "w_uq", "w_ukv", "w_mem_kv", "w_br", "w_out", "w_ff1", "w_ff2")
SMALL = ("ln_in_g", "ln_in_b", "b_forget", "g_cq", "g_ckv", "ln1_g", "ln1_b", "ln2_g", "ln2_b")
ROW_CUT = ("w_mem_kv", "w_out", "w_ff2")
LATE, EARLY = BIG[:4], BIG[4:]


def _shard_2d(a):
    cols = a.shape[-1]
    rows = a.size // cols
    return a.reshape(2, rows // 2, cols)


def _full_from_slots(name, slots, shard_shape):
    if name in ROW_CUT and len(shard_shape) == 2:
        return slots.reshape((N_CHIPS * shard_shape[0], shard_shape[1]))
    parts = slots.reshape((N_CHIPS,) + shard_shape)
    axis = len(shard_shape) - (2 if name in ROW_CUT else 1)
    return jnp.concatenate([parts[i] for i in range(N_CHIPS)], axis=axis)


def _slots_from_full(name, full, shard_shape):
    if name in ROW_CUT and len(shard_shape) == 2:
        return full.reshape(N_CHIPS, 2, shard_shape[0] // 2, shard_shape[1])
    axis = len(shard_shape) - (2 if name in ROW_CUT else 1)
    parts = jnp.stack(jnp.split(full, N_CHIPS, axis=axis))
    cols = shard_shape[-1]
    return parts.reshape(N_CHIPS, 2, -1, cols)


def _pack_small(cfg, vals):
    flat = jnp.concatenate([vals[k].reshape(-1).astype(F32) for k in SMALL])
    pad = (-flat.shape[0]) % (LANES * LANES)
    return jnp.pad(flat, (0, pad)).reshape(-1, LANES)


def _unpack_small(packed, like):
    flat, out, off = packed.reshape(-1), {}, 0
    for k in SMALL:
        n = like[k].size
        out[k] = flat[off:off + n].reshape(like[k].shape)
        off += n
    return out


class LayerComm:
    def __init__(self, cfg, w, m, v):
        self.cfg, self.w = cfg, w
        self.shards = [{k: _shard_2d(w[k][l].astype(BF16)) for k in BIG} for l in range(cfg.depth)]
        self.got = {LATE: _gather_now([self.shards[0][k] for k in LATE])}
        self.sets = {}
        self.pending = None
        rows = lambda a: a.reshape(a.shape[0], -1, a.shape[-1])
        self.state = {k: [rows(a[k]) for a in (w, m, v)] for k in BIG}
        self.outs = {k: None for k in BIG}

    def weights(self, l, names):
        if names in self.got:
            got = self.got.pop(names)
        elif names is EARLY:
            got = self.sets.pop((l, EARLY[:2])).got + self.sets.pop((l, EARLY[2:])).got
        else:
            got = self.sets.pop((l, names)).got
        mine = (_chip(), 0, 0, 0)
        return {k: _full_from_slots(k, lax.dynamic_update_slice(g, self.shards[l][k][None], mine), self.w[k].shape[1:])
                for k, g in zip(names, got)}

    def forward_hosts(self, l):
        for names, hosts in ((EARLY[2:], ("fox_fwd", "mla_fwd")), (EARLY[:2], ("mla_fwd", "mem_fwd"))):
            self.sets[l, names] = GatherSet([self.shards[l][k] for k in names], hosts)
        if l + 1 < self.cfg.depth:
            self.sets[l + 1, LATE] = GatherSet([self.shards[l + 1][k] for k in LATE], ("mla_fwd", "mem_fwd"))
        return Together([s for (layer, _), s in self.sets.items() if layer in (l, l + 1)])

    def _reduce(self, tag, names, big, hosts):
        return ReduceLayer(tag, [_slots_from_full(k, big[k], self.w[k].shape[1:]) for k in names], hosts)

    def backward_hosts(self, l):
        comm = self

        class Riders(Together):
            def early_grads(self, grads):
                comm.early = comm._reduce(f"{l}e", EARLY, grads, ("fox_bwd", "mla_bwd", "mem_bwd"))
                self.members.append(comm.early)

        return Riders([self.pending[1]] if self.pending else [])

    def _update(self, l, names, reduce):
        for k, (mine, theirs) in zip(names, reduce.result()):
            self.outs[k] = _adamw_layer(f"adamw_{k}_{l}", l, *self.state[k], mine, theirs, self.outs[k])

    def grads(self, l, big):
        self._update(l, EARLY, self.early)
        if self.pending:
            self._update(self.pending[0], LATE, self.pending[1])
        self.pending = (l, self._reduce(f"{l}l", LATE, big, ("ff2_bwd_x", "fox_bwd", "mla_bwd")))

    def finish(self):
        self.pending[1].run_now()
        self._update(self.pending[0], LATE, self.pending[1])
        return {k: tuple(a.reshape(self.w[k].shape) for a in self.outs[k]) for k in BIG}


def _step(cfg, x, mem, positions, target, w, m, v):
    comm = LayerComm(cfg, w, m, v)
    small_w = {k: w[k] for k in SMALL}
    loss_local, dx, small_g = _local_step(cfg, x, mem, positions, target, small_w, comm)
    loss = lax.psum(loss_local, ("x", "y", "c"))
    outs_big = comm.finish()

    g_small = _allreduce_small(_pack_small(cfg, small_g))
    packs = [_pack_small(cfg, {k: d_[k] for k in SMALL}) for d_ in (w, m, v)]
    dl, nm, nv = _slabwise("adamw_small", _adamw_math, [a[None] for a in (packs[0], g_small, packs[1], packs[2])],
                           [F32, F32, F32])
    outs_small = [_unpack_small(a[0] if a.ndim == 3 else a, w) for a in (g_small, dl, nm, nv)]

    names = SMALL[:2] + ("w_in", "b_forget", "w_uq", "g_cq", "w_ukv", "g_ckv", "w_mem_kv", "w_br", "w_out",
                         "ln1_g", "ln1_b", "w_ff1", "w_ff2", "ln2_g", "ln2_b")
    result = [loss, dx]
    for part in range(4):
        for k in names:
            result.append(outs_big[k][part] if k in outs_big else outs_small[part][k])
    return tuple(result)


def kernel(x, mem, positions, ln_in_g, ln_in_b, w_in, b_forget, w_uq, g_cq, w_ukv, g_ckv, w_mem_kv, w_br, w_out, ln1_g, ln1_b, w_ff1, w_ff2, ln2_g, ln2_b, loss_target, m_ln_in_g, m_ln_in_b, m_w_in, m_b_forget, m_w_uq, m_g_cq, m_w_ukv, m_g_ckv, m_w_mem_kv, m_w_br, m_w_out, m_ln1_g, m_ln1_b, m_w_ff1, m_w_ff2, m_ln2_g, m_ln2_b, v_ln_in_g, v_ln_in_b, v_w_in, v_b_forget, v_w_uq, v_g_cq, v_w_ukv, v_g_ckv, v_w_mem_kv, v_w_br, v_w_out, v_ln1_g, v_ln1_b, v_w_ff1, v_w_ff2, v_ln2_g, v_ln2_b):
    w = dict(ln_in_g=ln_in_g, ln_in_b=ln_in_b, w_in=w_in, b_forget=b_forget, w_uq=w_uq, g_cq=g_cq, w_ukv=w_ukv,
             g_ckv=g_ckv, w_mem_kv=w_mem_kv, w_br=w_br, w_out=w_out, ln1_g=ln1_g, ln1_b=ln1_b, w_ff1=w_ff1,
             w_ff2=w_ff2, ln2_g=ln2_g, ln2_b=ln2_b)
    m = dict(ln_in_g=m_ln_in_g, ln_in_b=m_ln_in_b, w_in=m_w_in, b_forget=m_b_forget, w_uq=m_w_uq, g_cq=m_g_cq,
             w_ukv=m_w_ukv, g_ckv=m_g_ckv, w_mem_kv=m_w_mem_kv, w_br=m_w_br, w_out=m_w_out, ln1_g=m_ln1_g,
             ln1_b=m_ln1_b, w_ff1=m_w_ff1, w_ff2=m_w_ff2, ln2_g=m_ln2_g, ln2_b=m_ln2_b)
    v = dict(ln_in_g=v_ln_in_g, ln_in_b=v_ln_in_b, w_in=v_w_in, b_forget=v_b_forget, w_uq=v_w_uq, g_cq=v_g_cq,
             w_ukv=v_w_ukv, g_ckv=v_g_ckv, w_mem_kv=v_w_mem_kv, w_br=v_w_br, w_out=v_w_out, ln1_g=v_ln1_g,
             ln1_b=v_ln1_b, w_ff1=v_w_ff1, w_ff2=v_w_ff2, ln2_g=v_ln2_g, ln2_b=v_ln2_b)
    return _step(Cfg(), x, mem, positions, loss_target, w, m, v)
```

```python
import functools
from typing import NamedTuple

import jax
import jax.numpy as jnp
from jax import lax
from jax.experimental import pallas as pl
from jax.experimental.pallas import tpu as pltpu

F32 = jnp.float32
BF16 = jnp.bfloat16
MESH = pl.DeviceIdType.MESH

LANES = 128
SUBLANES = 8
VMEM_BYTES = 64 * 1024 * 1024
N_CHIPS = 4
N_DEV = 8

FOX_DH = 64
MLA_NOPE = 64
MLA_ROPE = 32
MLA_V = 64
MEM_DH = 128
ROPE_BASE = 10000.0
LN_EPS = 1e-5
RMS_EPS = 1e-6
NEG_INF = -1e30
LOG2E = 1.4426950408889634
ATTN_TILE = 512
MEM_ATTN_TILE = 1024

ADAM_LR = 0.001
ADAM_B1 = 0.9
ADAM_B2 = 0.999
ADAM_EPS = 1e-08
ADAM_WD = 0.01
ADAM_STEP = 10


class Cfg(NamedTuple):
    d: int = 1024
    depth: int = 4
    seq: int = 2048
    chunk: int = 64
    n_mem: int = 256
    fox_h: int = 8
    mla_h: int = 8
    q_rank: int = 384
    kv_rank: int = 256
    mem_h: int = 4
    d_ff: int = 4096

    @property
    def width(self):
        return self.fox_h * FOX_DH

    @property
    def alpha(self):
        return (2 * self.depth) ** 0.25

    @property
    def small_w(self):
        return LANES + self.q_rank + self.kv_rank + LANES

    @property
    def in_splits(self):
        return (3 * self.width, self.fox_h, self.q_rank, self.kv_rank, MLA_ROPE, self.width, 3 * self.d)


class Exchange(NamedTuple):
    ins: tuple
    out_shapes: tuple
    plan: object
    n_copies: int
    aliases: dict = {}


def _peer(rel):
    x, y, c = lax.axis_index("x"), lax.axis_index("y"), lax.axis_index("c")
    if rel == "c":
        return (x, y, 1 - c)
    return ((1 - x) if rel in (1, 3) else x, (1 - y) if rel in (2, 3) else y, c)


def _exchange_copies(ex, in_refs, out_refs, send_sems, recv_sems):
    planned = ex.plan(in_refs, out_refs)
    assert len(planned) == ex.n_copies, len(planned)
    return [pltpu.make_async_remote_copy(src_ref=src, dst_ref=dst, send_sem=send_sems.at[i], recv_sem=recv_sems.at[i],
                                         device_id=_peer(rel), device_id_type=MESH)
            for i, (src, dst, rel) in enumerate(planned)]


def _pcall(body, rider=None, pin_all=False, **kw):
    if rider is not None:
        n_in, n_out, grid = len(kw["in_specs"]), len(kw["out_specs"]), kw["grid"]
        n_rin, n_rout = len(rider.ins), len(rider.out_shapes)
        host = body

        def body(*refs):
            ins, rins = refs[:n_in], refs[n_in:n_in + n_rin]
            outs = refs[n_in + n_rin:n_in + n_rin + n_out]
            routs = refs[n_in + n_rin + n_out:n_in + n_rin + n_out + n_rout]
            scratch = refs[n_in + n_rin + n_out + n_rout:-2]
            copies = _exchange_copies(rider, rins, routs, refs[-2], refs[-1])
            first = functools.reduce(jnp.logical_and, [pl.program_id(a) == 0 for a in range(len(grid))])
            last = functools.reduce(jnp.logical_and, [pl.program_id(a) == n - 1 for a, n in enumerate(grid)])

            @pl.when(first)
            def _():
                for cp in copies:
                    cp.start()

            host(*ins, *outs, *scratch)

            @pl.when(last)
            def _():
                for cp in copies:
                    cp.wait()

        any_spec = pl.BlockSpec(memory_space=pl.ANY)
        sems = [pltpu.SemaphoreType.DMA((rider.n_copies,))] * 2
        kw = dict(
            kw,
            in_specs=list(kw["in_specs"]) + [any_spec] * n_rin,
            out_specs=list(kw["out_specs"]) + [any_spec] * n_rout,
            out_shape=list(kw["out_shape"]) + list(rider.out_shapes),
            scratch_shapes=list(kw.get("scratch_shapes", ())) + sems,
            input_output_aliases={**kw.get("input_output_aliases", {}),
                                  **{n_in + i: n_out + o for i, o in rider.aliases.items()}},
        )
    call = pl.pallas_call(body, **kw)
    n_pinned = len(kw["in_specs"]) if pin_all else (len(rider.ins) if rider is not None else 0)

    def run(*ops):
        n_free = len(ops) - n_pinned
        return call(*ops[:n_free], *[pltpu.with_memory_space_constraint(o, pltpu.HBM) for o in ops[n_free:]])

    return run


class Hosts:
    def rider(self, host):
        return None

    def done(self, host, outs):
        pass

    def early_grads(self, grads):
        pass


def _merge_exchanges(exs):
    n_ins = [len(e.ins) for e in exs]
    n_outs = [len(e.out_shapes) for e in exs]

    def plan(in_refs, out_refs):
        copies, i, o = [], 0, 0
        for e, ni, no in zip(exs, n_ins, n_outs):
            copies += e.plan(in_refs[i:i + ni], out_refs[o:o + no])
            i, o = i + ni, o + no
        return copies

    aliases, i, o = {}, 0, 0
    for e, ni, no in zip(exs, n_ins, n_outs):
        aliases.update({i + a: o + b for a, b in e.aliases.items()})
        i, o = i + ni, o + no
    return Exchange(sum((e.ins for e in exs), ()), sum((e.out_shapes for e in exs), ()), plan,
                    sum(e.n_copies for e in exs), aliases)


class Together(Hosts):
    def __init__(self, members):
        self.members, self.active = list(members), []

    def rider(self, host):
        self.active = [(m, r) for m, r in ((m, m.rider(host)) for m in self.members) if r is not None]
        if not self.active:
            return None
        return _merge_exchanges([r for _, r in self.active])

    def done(self, host, outs):
        for m, r in self.active:
            m.done(host, outs[:len(r.out_shapes)])
            outs = outs[len(r.out_shapes):]


def _hosted(hosts, host, call, ops, n_results):
    rider = hosts.rider(host) if hosts is not None else None
    if rider is None:
        return call(None)(*ops)
    res = call(rider)(*ops, *rider.ins)
    hosts.done(host, list(res[n_results:]))
    return res[:n_results]


def _nbytes(shape, dtype):
    n = 1
    for s in shape:
        n *= s
    return n * jnp.dtype(dtype).itemsize


def _tile(dim, target):
    if dim <= target:
        return dim
    t = target - target % LANES
    while t >= LANES:
        if dim % t == 0:
            return t
        t -= LANES
    return dim


def _params(block_bytes, scratch_bytes=0):
    est = 2 * block_bytes + scratch_bytes + 24 * 1024 * 1024
    return pltpu.CompilerParams(vmem_limit_bytes=int(min(max(est, 32 * 1024 * 1024), VMEM_BYTES - 4 * 1024 * 1024)))


_DIMS = {"nn": (((1,), (0,)), ((), ())), "nt": (((1,), (1,)), ((), ())), "tn": (((0,), (0,)), ((), ()))}


MM_TILE = 1024
MM_BLOCK_BYTES = 16 * 1024 * 1024


def _mm_tiles(mode, pairs, out_dtypes, m, n, tm, tn):
    fixed_m, fixed_n = tm is not None, tn is not None
    tm, tn = _tile(m, tm or MM_TILE), _tile(n, tn or MM_TILE)

    def block_bytes(tm_, tn_):
        total = sum(_nbytes((tm_, tn_), dt) for dt in out_dtypes)
        for a, b in pairs:
            k = a.shape[0] if mode == "tn" else a.shape[1]
            total += _nbytes((k, tm_), a.dtype) + _nbytes((k, tn_), b.dtype)
        return total

    while block_bytes(tm, tn) > MM_BLOCK_BYTES:
        if not fixed_m and tm >= tn and tm > 2 * LANES:
            tm = _tile(m, tm // 2)
        elif not fixed_n and tn > 2 * LANES:
            tn = _tile(n, tn // 2)
        elif not fixed_m and tm > 2 * LANES:
            tm = _tile(m, tm // 2)
        else:
            break
    return tm, tn


def _mm(name, mode, pairs, out_dtypes, tm=None, tn=None, epi=None, combine=None, row_extras=(), bc_extras=(),
        hosts=None, host=None):
    a0, b0 = pairs[0]
    m = a0.shape[1] if mode == "tn" else a0.shape[0]
    n = b0.shape[0] if mode == "nt" else b0.shape[1]
    tm, tn = _mm_tiles(mode, pairs, out_dtypes, m, n, tm, tn)
    in_specs, ops, blk = [], [], 0
    for a, b in pairs:
        if mode == "tn":
            k = a.shape[0]
            sa, sha = pl.BlockSpec((k, tm), lambda i, j: (0, i)), (k, tm)
        else:
            k = a.shape[1]
            sa, sha = pl.BlockSpec((tm, k), lambda i, j: (i, 0)), (tm, k)
        if mode == "nt":
            sb, shb = pl.BlockSpec((tn, k), lambda i, j: (j, 0)), (tn, k)
        else:
            sb, shb = pl.BlockSpec((k, tn), lambda i, j: (0, j)), (k, tn)
        in_specs += [sa, sb]
        ops += [a, b]
        blk += _nbytes(sha, a.dtype) + _nbytes(shb, b.dtype)
    for e in row_extras:
        w = e.shape[1]
        if w == n:
            in_specs.append(pl.BlockSpec((tm, tn), lambda i, j: (i, j)))
            blk += _nbytes((tm, tn), e.dtype)
        else:
            in_specs.append(pl.BlockSpec((tm, w), lambda i, j: (i, 0)))
            blk += _nbytes((tm, w), e.dtype)
        ops.append(e)
    for e in bc_extras:
        r, w = e.shape
        if w == n:
            in_specs.append(pl.BlockSpec((r, tn), lambda i, j: (0, j)))
        else:
            in_specs.append(pl.BlockSpec((r, w), lambda i, j: (0, 0)))
        blk += _nbytes((r, w), e.dtype)
        ops.append(e)
    npairs, nrow, nbc, nout = len(pairs), len(row_extras), len(bc_extras), len(out_dtypes)
    dims = _DIMS[mode]

    def body(*refs):
        prods = [lax.dot_general(refs[2 * p][...].astype(BF16), refs[2 * p + 1][...].astype(BF16), dims,
                                 preferred_element_type=F32) for p in range(npairs)]
        ex = [r[...] for r in refs[2 * npairs:2 * npairs + nrow + nbc]]
        if combine is not None:
            outs = combine(prods, *ex)
        else:
            acc = functools.reduce(lambda s, d: s + d, prods)
            outs = (acc,) if epi is None else epi(acc, *ex)
        for o_ref, o in zip(refs[2 * npairs + nrow + nbc:], outs):
            o_ref[...] = o.astype(o_ref.dtype)

    blk += sum(_nbytes((tm, tn), dt) for dt in out_dtypes) + 2 * _nbytes((tm, tn), F32)
    call = lambda rider: _pcall(
        body,
        rider=rider,
        name=name,
        grid=(m // tm, n // tn),
        in_specs=in_specs,
        out_specs=[pl.BlockSpec((tm, tn), lambda i, j: (i, j)) for _ in range(nout)],
        out_shape=[jax.ShapeDtypeStruct((m, n), dt) for dt in out_dtypes],
        compiler_params=_params(blk),
    )
    res = _hosted(hosts, host, call, ops, nout)
    return res[0] if nout == 1 else res


def _rowwise(name, fn, row_ins, bc_ins, outs, accs=(), tm=256):
    rows = row_ins[0].shape[0]
    tm = min(tm, rows)
    assert rows % tm == 0
    nrow, nbc, nout, nacc = len(row_ins), len(bc_ins), len(outs), len(accs)
    in_specs = [pl.BlockSpec((tm, a.shape[1]), lambda i: (i, 0)) for a in row_ins]
    in_specs += [pl.BlockSpec(a.shape, lambda i: (0, 0)) for a in bc_ins]
    out_specs = [pl.BlockSpec((tm, w), lambda i: (i, 0)) for w, _ in outs]
    out_specs += [pl.BlockSpec(s, lambda i: (0, 0)) for s in accs]
    out_shape = [jax.ShapeDtypeStruct((rows, w), dt) for w, dt in outs]
    out_shape += [jax.ShapeDtypeStruct(s, F32) for s in accs]

    def body(*refs):
        vals = fn(*[r[...] for r in refs[:nrow + nbc]])
        o_refs = refs[nrow + nbc:]
        for r, v in zip(o_refs[:nout], vals[:nout]):
            r[...] = v.astype(r.dtype)
        if nacc:
            @pl.when(pl.program_id(0) == 0)
            def _():
                for r in o_refs[nout:]:
                    r[...] = jnp.zeros(r.shape, F32)

            for r, v in zip(o_refs[nout:], vals[nout:]):
                r[...] += v

    blk = sum(_nbytes((tm, a.shape[1]), a.dtype) for a in row_ins) + sum(_nbytes(a.shape, a.dtype) for a in bc_ins)
    blk += sum(_nbytes((tm, w), dt) for w, dt in outs) + sum(_nbytes(s, F32) for s in accs)
    res = _pcall(
        body,
        name=name,
        grid=(rows // tm,),
        in_specs=in_specs,
        out_specs=out_specs,
        out_shape=out_shape,
        compiler_params=_params(2 * blk),
    )(*row_ins, *bc_ins)
    return res


def _ln(z, g, b):
    mu = jnp.mean(z, axis=-1, keepdims=True)
    zc = z - mu
    var = jnp.mean(zc * zc, axis=-1, keepdims=True)
    return zc * lax.rsqrt(var + LN_EPS) * g + b


def _rms(x, g):
    return x * lax.rsqrt(jnp.mean(x * x, axis=-1, keepdims=True) + RMS_EPS) * g


def _rope_swap(x):
    w = x.shape[1]
    lane = lax.broadcasted_iota(jnp.int32, (1, w), 1) % LANES
    from_left = pltpu.roll(x, 16, 1)
    from_right = pltpu.roll(x, w - 16, 1)
    lo = (lane >= MLA_NOPE) & (lane < MLA_NOPE + 16)
    hi = (lane >= MLA_NOPE + 16) & (lane < MLA_NOPE + 32)
    return jnp.where(hi, from_left, jnp.where(lo, from_right, 0.0))


def _rope(x, cos_t, sin_t):
    nh = x.shape[1] // LANES
    ct, st = jnp.tile(cos_t, (1, nh)), jnp.tile(sin_t, (1, nh))
    return x * ct + _rope_swap(x) * st


def _rope_t(dy, cos_t, sin_t):
    nh = dy.shape[1] // LANES
    ct, st = jnp.tile(cos_t, (1, nh)), jnp.tile(sin_t, (1, nh))
    return dy * ct + _rope_swap(dy * st)


def _block_cumsum(v, carry, reverse):
    tb = v.shape[0]
    r = lax.broadcasted_iota(jnp.int32, (tb, tb), 0)
    c = lax.broadcasted_iota(jnp.int32, (tb, tb), 1)
    tri = jnp.where((c >= r) if reverse else (c <= r), 1.0, 0.0).astype(BF16)
    hi = v.astype(BF16)
    r1 = v - hi.astype(F32)
    mid = r1.astype(BF16)
    lo = (r1 - mid.astype(F32)).astype(BF16)
    out = carry + sum(jnp.dot(tri, p, preferred_element_type=F32) for p in (hi, mid, lo))
    return out, (out[0:1, :] if reverse else out[tb - 1:tb, :])


def _forget_cumsum(name, logf, batch, seq, heads, tb):
    nb = seq // tb

    def body(x_ref, keys_ref, carry):
        @pl.when(pl.program_id(1) == 0)
        def _():
            carry[...] = jnp.zeros(carry.shape, F32)

        out, carry[...] = _block_cumsum(x_ref[...], carry[...], False)
        out = out * LOG2E
        keys_ref[...] = jnp.concatenate([jnp.broadcast_to(out[:, h:h + 1], (tb, LANES)) for h in range(heads)], axis=1)

    return _pcall(
        body,
        name=name,
        grid=(batch, nb),
        in_specs=[pl.BlockSpec((tb, LANES), lambda b, i: (b * nb + i, 0))],
        out_specs=pl.BlockSpec((tb, heads * LANES), lambda b, i: (b * nb + i, 0)),
        out_shape=jax.ShapeDtypeStruct((batch * seq, heads * LANES), F32),
        scratch_shapes=[pltpu.VMEM((1, LANES), F32)],
        compiler_params=_params(4 * tb * (heads + 2) * LANES * 4),
    )(logf)


def _forget_cumsum_bwd(name, dcf, batch, seq, heads, tb):
    nb = seq // tb

    def body(x_ref, o_ref, carry):
        @pl.when(pl.program_id(1) == 0)
        def _():
            carry[...] = jnp.zeros(carry.shape, F32)

        lane = lax.broadcasted_iota(jnp.int32, (1, LANES), 1)
        v = jnp.zeros((tb, LANES), F32)
        for g in range(heads // 2):
            blk = x_ref[:, g * LANES:(g + 1) * LANES]
            moved = pltpu.roll(blk, 2 * g, 1) if g else blk
            v = v + jnp.where((lane >= 2 * g) & (lane < 2 * g + 2), moved, 0.0)
        o_ref[...], carry[...] = _block_cumsum(v, carry[...], True)

    return _pcall(
        body,
        name=name,
        grid=(batch, nb),
        in_specs=[pl.BlockSpec((tb, (heads // 2) * LANES), lambda b, i: (b * nb + nb - 1 - i, 0))],
        out_specs=pl.BlockSpec((tb, LANES), lambda b, i: (b * nb + nb - 1 - i, 0)),
        out_shape=jax.ShapeDtypeStruct((batch * seq, LANES), F32),
        scratch_shapes=[pltpu.VMEM((1, LANES), F32)],
        compiler_params=_params(4 * tb * (heads // 2 + 1) * LANES * 4),
    )(dcf)


class Attn(NamedTuple):
    batch: int
    sq: int
    sk: int
    groups: int
    hq: int
    hv: int
    mode: str
    scale: float
    chunk: int
    tq: int
    tk: int

    @property
    def hg(self):
        return self.hv

    @property
    def qw(self):
        return LANES * self.hg // self.hq

    @property
    def dv(self):
        return LANES // self.hv


def _head_lanes(j, dv):
    lane = lax.broadcasted_iota(jnp.int32, (1, LANES), 1)
    return (lane >= j * dv) & (lane < (j + 1) * dv)


def _head_q(sp, j, q_blk):
    if sp.hq == 2:
        return jnp.where(_head_lanes(j, FOX_DH), q_blk, jnp.zeros_like(q_blk))
    return q_blk[:, LANES * j:LANES * (j + 1)]


def _head_rows(sp, j):
    return slice(j * sp.dv, (j + 1) * sp.dv) if sp.hg == 2 else slice(None)


def _scores_t(sp, j, k_c, q_j, cfk_rep, k0, q0, masked):
    tk, tq = k_c.shape[0], q_j.shape[0]
    k_j = k_c if sp.hq == 2 else k_c[:, LANES * j:LANES * (j + 1)]
    st = lax.dot_general(k_j, q_j, _DIMS["nt"], preferred_element_type=F32) * (sp.scale * LOG2E)
    if sp.mode == "fox":
        st = st - jnp.tile(cfk_rep[:, LANES * j:LANES * (j + 1)], (1, tq // LANES))
    if masked:
        kidx = k0 + lax.broadcasted_iota(jnp.int32, (tk, tq), 0)
        qidx = q0 + lax.broadcasted_iota(jnp.int32, (tk, tq), 1)
        if sp.mode == "chunk":
            shift = sp.chunk.bit_length() - 1
            kidx, qidx = jnp.right_shift(kidx, shift), jnp.right_shift(qidx, shift)
        st = jnp.where(kidx <= qidx, st, NEG_INF)
    return st


def _attn_fwd(name, sp, q, k, v, cfk=None, hosts=None, host=None):
    (qa, qo), (ka, ko), (va, vo) = q, k, v
    tq, tk, hg, qw = sp.tq, sp.tk, sp.hg, sp.qw
    nqb, nkc = sp.sq // tq, sp.sk // tk
    fox, causal = sp.mode == "fox", sp.mode != "none"
    assert sp.sq % tq == 0 and sp.sk % tk == 0 and (not causal or (tq == tk and sp.sq == sp.sk))

    def body(*refs):
        if fox:
            q_ref, k_ref, v_ref, cfk_ref, o_ref, lse_ref, acc_scr = refs
        else:
            q_ref, k_ref, v_ref, o_ref, lse_ref, acc_scr = refs
        i = pl.program_id(2)
        q0 = i * tq
        q_blk = q_ref[...]
        qs = [_head_q(sp, j, q_blk) for j in range(hg)]
        acc_scr[...] = jnp.zeros(acc_scr.shape, F32)

        def chunk(kc, carry, masked):
            ms, ls = carry
            k0 = pl.multiple_of(kc * tk, tk)
            k_c = k_ref[pl.ds(k0, tk), :]
            v_c = v_ref[pl.ds(k0, tk), :]
            new_m, new_l = [], []
            for j in range(hg):
                st = _scores_t(sp, j, k_c, qs[j], cfk_ref[pl.ds(k0, tk), :] if fox else None, k0, q0, masked)
                m_new = jnp.maximum(ms[j], jnp.max(st, axis=0, keepdims=True))
                alpha = jnp.exp2(ms[j] - m_new)
                pt = jnp.exp2(st - m_new)
                new_m.append(m_new)
                new_l.append(alpha * ls[j] + jnp.sum(pt, axis=0, keepdims=True))
                pv = lax.dot_general(v_c, pt.astype(BF16), _DIMS["tn"], preferred_element_type=F32)
                r = _head_rows(sp, j)
                acc_scr[r, :] = acc_scr[r, :] * alpha + pv[r, :]
            return tuple(new_m), tuple(new_l)

        carry = (tuple(jnp.full((1, tq), NEG_INF, F32) for _ in range(hg)),
                 tuple(jnp.zeros((1, tq), F32) for _ in range(hg)))
        if causal:
            carry = lax.fori_loop(0, i, functools.partial(chunk, masked=False), carry)
            ms, ls = chunk(i, carry, True)
        else:
            ms, ls = lax.fori_loop(0, nkc, functools.partial(chunk, masked=False), carry)
        for j in range(hg):
            r = _head_rows(sp, j)
            acc_scr[r, :] = acc_scr[r, :] / ls[j]
            lse_ref[j:j + 1, :] = ms[j] + jnp.log(ls[j]) * LOG2E
        o_ref[...] = acc_scr[...].T

    in_specs = [
        pl.BlockSpec((tq, qw), lambda b, g, i: (b * nqb + i, qo + g)),
        pl.BlockSpec((sp.sk, qw), lambda b, g, i: (b, ko + g)),
        pl.BlockSpec((sp.sk, LANES), lambda b, g, i: (b, vo + g)),
    ]
    ops = [qa, ka, va]
    stat_blk = pl.BlockSpec((None, None, None, hg, tq), lambda b, g, i: (b, g, i, 0, 0))
    if fox:
        in_specs.append(pl.BlockSpec((sp.sk, hg * LANES), lambda b, g, i: (b, g)))
        ops.append(cfk)
    blk = _nbytes((tq, qw), BF16) + _nbytes((sp.sk, qw + LANES), BF16) + 2 * _nbytes((tq, LANES), F32)
    blk += _nbytes((sp.sk, hg * LANES), F32) + 6 * _nbytes((tk, tq), F32)
    call = lambda rider: _pcall(
        body,
        rider=rider,
        name=name,
        grid=(sp.batch, sp.groups, nqb),
        in_specs=in_specs,
        out_specs=[pl.BlockSpec((tq, LANES), lambda b, g, i: (b * nqb + i, g)), stat_blk],
        out_shape=[
            jax.ShapeDtypeStruct((sp.batch * sp.sq, sp.groups * LANES), F32),
            jax.ShapeDtypeStruct((sp.batch, sp.groups, nqb, hg, tq), F32),
        ],
        scratch_shapes=[pltpu.VMEM((LANES, tq), F32)],
        compiler_params=_params(blk, tq * LANES * 4),
    )
    return _hosted(hosts, host, call, ops, 2)


def _attn_bwd(name, sp, q, k, v, o, lse, do, cfk=None, hosts=None, host=None):
    (qa, qo), (ka, ko), (va, vo) = q, k, v
    tq, tk, hg, qw, dv = sp.tq, sp.tk, sp.hg, sp.qw, sp.dv
    nqb, nkb = sp.sq // tq, sp.sk // tk
    fox, causal = sp.mode == "fox", sp.mode != "none"
    assert sp.sq % tq == 0 and sp.sk % tk == 0 and (not causal or (tq == tk and sp.sq == sp.sk))

    def body(*refs):
        if fox:
            (q_ref, k_ref, v_ref, lse_ref, do_ref, cfk_ref, kall_ref, vall_ref, cfkall_ref,
             dq_ref, dk_ref, dv_ref, dcf_ref, delta_scr, dk_scr, dv_scr, dqt_scr, dcf_scr) = refs
        else:
            (q_ref, k_ref, v_ref, o_ref, lse_ref, do_ref,
             dq_ref, dk_ref, dv_ref, delta_scr, dk_scr, dv_scr, dqt_scr) = refs
        kb = pl.program_id(2)
        k0 = kb * tk
        heads = [_head_lanes(j, dv) for j in range(hg)]

        def head_do(j, do_c):
            return jnp.where(heads[j], do_c, jnp.zeros_like(do_c)) if hg == 2 else do_c

        def probs_t(j, k_c, v_c, q_c, do_c, i, cf_keys, c0, masked):
            st = _scores_t(sp, j, k_c, _head_q(sp, j, q_c), cf_keys, c0, i * tq, masked)
            pt = jnp.exp2(st - lse_ref[i][j:j + 1, :])
            dpt = lax.dot_general(v_c, head_do(j, do_c), _DIMS["nt"], preferred_element_type=F32)
            return pt, dpt

        @pl.when(kb == 0)
        def _():
            dqt_scr[...] = jnp.zeros(dqt_scr.shape, F32)

            def fill(i, carry):
                r0 = pl.multiple_of(i * tq, tq)
                do_c = do_ref[pl.ds(r0, tq), :]
                if fox:
                    q_c = q_ref[pl.ds(r0, tq), :]

                    def keys(kc, acc, masked):
                        c0 = pl.multiple_of(kc * tk, tk)
                        out = []
                        for j in range(hg):
                            pt, dpt = probs_t(j, kall_ref[pl.ds(c0, tk), :], vall_ref[pl.ds(c0, tk), :], q_c, do_c, i,
                                              cfkall_ref[pl.ds(c0, tk), :], c0, masked)
                            out.append(acc[j] + jnp.sum(pt * dpt, axis=0, keepdims=True))
                        return tuple(out)

                    d = lax.fori_loop(0, i, functools.partial(keys, masked=False),
                                      tuple(jnp.zeros((1, tq), F32) for _ in range(hg)))
                    d = keys(i, d, True)
                    for j in range(hg):
                        delta_scr[i, j:j + 1, :] = d[j]
                else:
                    prod_t = (do_c.astype(F32) * o_ref[pl.ds(r0, tq), :]).T
                    for j in range(hg):
                        delta_scr[i, j:j + 1, :] = jnp.sum(prod_t[_head_rows(sp, j), :], axis=0, keepdims=True)
                return carry

            lax.fori_loop(0, nqb, fill, 0)

        k_blk = k_ref[...]
        v_blk = v_ref[...]
        k_t = k_blk.astype(F32).T.astype(BF16)
        dk_scr[...] = jnp.zeros(dk_scr.shape, F32)
        dv_scr[...] = jnp.zeros(dv_scr.shape, F32)
        if fox:
            dcf_scr[...] = jnp.zeros(dcf_scr.shape, F32)

        def qblock(i, carry, masked):
            r0 = pl.multiple_of(i * tq, tq)
            q_c = q_ref[pl.ds(r0, tq), :]
            do_c = do_ref[pl.ds(r0, tq), :]
            for j in range(hg):
                pt, dpt = probs_t(j, k_blk, v_blk, q_c, do_c, i, cfk_ref[...] if fox else None, k0, masked)
                dst = pt * (dpt - delta_scr[i][j:j + 1, :])
                if fox:
                    part = dst[:, :LANES]
                    for t in range(1, tq // LANES):
                        part = part + dst[:, t * LANES:(t + 1) * LANES]
                    dcf_scr[j] += part
                ds_b = (dst * sp.scale).astype(BF16)
                dv_scr[j] += jnp.dot(pt.astype(BF16), do_c, preferred_element_type=F32)
                dk_scr[j] += jnp.dot(ds_b, q_c if sp.hq == 2 else _head_q(sp, j, q_c), preferred_element_type=F32)
                if sp.hq == 2:
                    r = pl.ds(j * FOX_DH, FOX_DH)
                    dqt_scr[i, r, :] += jnp.dot(k_t[j * FOX_DH:(j + 1) * FOX_DH, :], ds_b, preferred_element_type=F32)
                else:
                    r = pl.ds(j * LANES, LANES)
                    dqt_scr[i, r, :] += jnp.dot(k_t[j * LANES:(j + 1) * LANES, :], ds_b, preferred_element_type=F32)
            return carry

        if causal:
            qblock(kb, 0, True)
            lax.fori_loop(kb + 1, nqb, functools.partial(qblock, masked=False), 0)
        else:
            lax.fori_loop(0, nqb, functools.partial(qblock, masked=False), 0)

        @pl.when(kb == nkb - 1)
        def _():
            def untranspose(i, carry):
                dq_ref[pl.ds(pl.multiple_of(i * tq, tq), tq), :] = dqt_scr[i].T
                return carry

            lax.fori_loop(0, nqb, untranspose, 0)

        if hg == 2:
            dv_ref[...] = jnp.where(heads[0], dv_scr[0], dv_scr[1]).astype(dv_ref.dtype)
        else:
            dv_ref[...] = dv_scr[0].astype(dv_ref.dtype)
        if sp.hq == 2:
            dk_ref[...] = jnp.where(_head_lanes(0, FOX_DH), dk_scr[0], dk_scr[1]).astype(dk_ref.dtype)
        elif hg == 2:
            dk_ref[...] = jnp.concatenate([dk_scr[0], dk_scr[1]], axis=1).astype(dk_ref.dtype)
        else:
            dk_ref[...] = dk_scr[0].astype(dk_ref.dtype)
        if fox:
            lane = lax.broadcasted_iota(jnp.int32, (1, LANES), 1)
            sums = [jnp.sum(dcf_scr[j], axis=1, keepdims=True) for j in range(hg)]
            dcf_ref[...] = jnp.where(lane == 0, -sums[0], jnp.where(lane == 1, -sums[1], 0.0))

    seq_lanes = lambda b, g, kb: (b, g)
    key_blk = lambda b, g, kb: (b * nkb + kb, g)
    stats = pl.BlockSpec((None, None, nqb, hg, tq), lambda b, g, kb: (b, g, 0, 0, 0))
    in_specs = [
        pl.BlockSpec((sp.sq, qw), lambda b, g, kb: (b, qo + g)),
        pl.BlockSpec((tk, qw), lambda b, g, kb: (b * nkb + kb, ko + g)),
        pl.BlockSpec((tk, LANES), lambda b, g, kb: (b * nkb + kb, vo + g)),
    ]
    ops = [qa, ka, va]
    if not fox:
        in_specs.append(pl.BlockSpec((sp.sq, LANES), seq_lanes))
        ops.append(o)
    in_specs += [stats, pl.BlockSpec((sp.sq, LANES), seq_lanes)]
    ops += [lse, do]
    out_specs = [pl.BlockSpec((sp.sq, qw), seq_lanes), pl.BlockSpec((tk, qw), key_blk), pl.BlockSpec((tk, LANES), key_blk)]
    out_shape = [
        jax.ShapeDtypeStruct((sp.batch * sp.sq, sp.groups * qw), F32),
        jax.ShapeDtypeStruct((sp.batch * sp.sk, sp.groups * qw), BF16),
        jax.ShapeDtypeStruct((sp.batch * sp.sk, sp.groups * LANES), BF16),
    ]
    scratch = [pltpu.VMEM((nqb, hg, tq), F32), pltpu.VMEM((hg, tk, LANES), F32), pltpu.VMEM((hg, tk, LANES), F32),
               pltpu.VMEM((nqb, qw, tq), F32)]
    if fox:
        in_specs += [
            pl.BlockSpec((tk, hg * LANES), key_blk),
            pl.BlockSpec((sp.sk, qw), lambda b, g, kb: (b, ko + g)),
            pl.BlockSpec((sp.sk, LANES), lambda b, g, kb: (b, vo + g)),
            pl.BlockSpec((sp.sk, hg * LANES), seq_lanes),
        ]
        ops += [cfk, ka, va, cfk]
        out_specs.append(pl.BlockSpec((tk, LANES), key_blk))
        out_shape.append(jax.ShapeDtypeStruct((sp.batch * sp.sk, sp.groups * LANES), F32))
        scratch.append(pltpu.VMEM((hg, tk, LANES), F32))
    blk = _nbytes((sp.sq, qw), BF16) + _nbytes((sp.sq, LANES), BF16) + 2 * _nbytes((sp.sq, LANES), F32)
    blk += _nbytes((sp.sq, qw), F32) + 4 * _nbytes((tk, qw), BF16) + 8 * _nbytes((tq, tk), F32)
    blk += (_nbytes((sp.sk, qw + LANES), BF16) + _nbytes((sp.sk, hg * LANES), F32)) if fox else 0
    call = lambda rider: _pcall(
        body,
        rider=rider,
        name=name,
        grid=(sp.batch, sp.groups, nkb),
        in_specs=in_specs,
        out_specs=out_specs,
        out_shape=out_shape,
        scratch_shapes=scratch,
        compiler_params=_params(blk, _nbytes((sp.sq, LANES), F32) + 4 * _nbytes((tk, LANES), F32)),
    )
    return _hosted(hosts, host, call, ops, len(out_shape))


def _slabwise(name, fn, ins, out_dtypes, rows_per_step=512):
    ins = [a if isinstance(a, tuple) else (a, None) for a in ins]
    n = max(1 if fixed is not None else a.shape[0] for a, fixed in ins)
    rows, cols = ins[0][0].shape[1:]
    tr = min(rows_per_step, rows)
    while rows % tr:
        tr //= 2
    assert tr % 16 == 0 or tr == rows, (name, rows, tr)

    def spec(a, fixed):
        if fixed is not None or a.shape[0] == 1:
            return pl.BlockSpec((None, tr, cols), lambda s, i: (fixed or 0, i, 0))
        return pl.BlockSpec((None, tr, cols), lambda s, i: (s, i, 0))

    def body(*refs):
        vals = fn(*[r[...] for r in refs[:len(ins)]])
        for r, v in zip(refs[len(ins):], vals):
            r[...] = v.astype(r.dtype)

    blk = (len(ins) + len(out_dtypes)) * _nbytes((tr, cols + LANES), F32)
    res = _pcall(
        body,
        name=name,
        grid=(n, rows // tr),
        in_specs=[spec(a, fixed) for a, fixed in ins],
        out_specs=[pl.BlockSpec((None, tr, cols), lambda s, i: (s, i, 0)) for _ in out_dtypes],
        out_shape=[jax.ShapeDtypeStruct((n, rows, cols), dt) for dt in out_dtypes],
        compiler_params=_params(2 * blk),
    )(*[a for a, _ in ins])
    return res


def _adamw_math(w, g, m, v):
    m = ADAM_B1 * m + (1.0 - ADAM_B1) * g
    v = ADAM_B2 * v + (1.0 - ADAM_B2) * jnp.square(g)
    m_hat = m / (1.0 - ADAM_B1 ** ADAM_STEP)
    v_hat = v / (1.0 - ADAM_B2 ** ADAM_STEP)
    delta = -ADAM_LR * (m_hat / (jnp.sqrt(v_hat) + ADAM_EPS) + ADAM_WD * w)
    return delta, m, v


def _run_exchange(name, ex):
    n_in, n_out = len(ex.ins), len(ex.out_shapes)

    def body(*refs):
        copies = _exchange_copies(ex, refs[:n_in], refs[n_in:n_in + n_out], refs[-2], refs[-1])
        for cp in copies:
            cp.start()
        for cp in copies:
            cp.wait()

    any_spec = pl.BlockSpec(memory_space=pl.ANY)
    return _pcall(
        body,
        pin_all=True,
        name=name,
        in_specs=[any_spec] * n_in,
        out_specs=[any_spec] * n_out,
        out_shape=list(ex.out_shapes),
        scratch_shapes=[pltpu.SemaphoreType.DMA((ex.n_copies,))] * 2,
        input_output_aliases=dict(ex.aliases),
    )(*ex.ins)


def _chip(rel=0):
    x, y = lax.axis_index("x"), lax.axis_index("y")
    return 2 * ((1 - x) if rel & 1 else x) + ((1 - y) if rel & 2 else y)


def _gather_ici(shards):
    def plan(in_refs, out_refs):
        c = lax.axis_index("c")
        return [(s.at[c], g.at[_chip(), c], rel) for s, g in zip(in_refs, out_refs) for rel in (1, 2, 3)]

    shapes = tuple(jax.ShapeDtypeStruct((N_CHIPS,) + s.shape, s.dtype) for s in shards)
    return Exchange(tuple(shards), shapes, plan, 3 * len(shards))


def _gather_d2d(got):
    def plan(in_refs, out_refs):
        c = lax.axis_index("c")
        return [(g_in.at[_chip(rel), c], g_out.at[_chip(rel), c], "c")
                for g_in, g_out in zip(in_refs, out_refs) for rel in (1, 2, 3)]

    shapes = tuple(jax.ShapeDtypeStruct(g.shape, g.dtype) for g in got)
    return Exchange(tuple(got), shapes, plan, 3 * len(got), {i: i for i in range(len(got))})


class GatherSet(Hosts):
    def __init__(self, shards, hosts):
        self.shards, self.hosts, self.got = shards, hosts, None

    def rider(self, host):
        if host == self.hosts[0]:
            return _gather_ici(self.shards)
        if host == self.hosts[1]:
            return _gather_d2d(self.got)
        return None

    def done(self, host, outs):
        self.got = outs


def _gather_now(shards):
    got = _run_exchange("gather_weights_ici", _gather_ici(shards))
    return _run_exchange("gather_weights_d2d", _gather_d2d(got))


def _pair_sum(name, g, recv, rows_per_step=512):
    _, _, rows, cols = g.shape
    tr = min(rows_per_step, rows)
    while rows % tr:
        tr //= 2

    def body(g_ref, r_ref, p_ref, own_ref):
        mine = jnp.where(lax.axis_index("c") == 0, g_ref[0], g_ref[1])
        p = mine.astype(F32) + r_ref[...].astype(F32)
        p_ref[...] = p.astype(p_ref.dtype)

        @pl.when(pl.program_id(1) == _chip())
        def _():
            own_ref[...] = p

    slab = pl.BlockSpec((None, tr, cols), lambda i, s: (s, i, 0))
    return _pcall(
        body,
        name=name,
        grid=(rows // tr, N_CHIPS),
        in_specs=[pl.BlockSpec((None, 2, tr, cols), lambda i, s: (s, 0, i, 0)), slab],
        out_specs=[slab, pl.BlockSpec((None, tr, cols), lambda i, s: (0, i, 0))],
        out_shape=[jax.ShapeDtypeStruct((N_CHIPS, rows, cols), BF16), jax.ShapeDtypeStruct((1, rows, cols), F32)],
        compiler_params=_params(2 * 6 * _nbytes((tr, cols + LANES), F32)),
    )(g, recv)


class ReduceLayer(Hosts):
    def __init__(self, tag, grads, hosts):
        self.tag, self.grads, self.pair, self.own, self.total, self.theirs = tag, grads, None, None, None, None
        self.stage_of = dict(zip(hosts, ("swap", "chips", "share")))

    def _swap_halves(self):
        def plan(in_refs, out_refs):
            c = lax.axis_index("c")
            return [(g.at[pl.ds(0, N_CHIPS), 1 - c], r, "c") for g, r in zip(in_refs, out_refs)]

        shapes = tuple(jax.ShapeDtypeStruct((N_CHIPS,) + g.shape[2:], g.dtype) for g in self.grads)
        return Exchange(tuple(self.grads), shapes, plan, len(self.grads))

    def _to_chips(self):
        def plan(in_refs, out_refs):
            return [(p.at[_chip(rel)], r.at[rel - 1], rel) for p, r in zip(in_refs, out_refs) for rel in (1, 2, 3)]

        shapes = tuple(jax.ShapeDtypeStruct((3,) + p.shape[1:], p.dtype) for p in self.pair)
        return Exchange(tuple(self.pair), shapes, plan, 3 * len(self.pair))

    def _share(self):
        def plan(in_refs, out_refs):
            return [(t, r, "c") for t, r in zip(in_refs, out_refs)]

        shapes = tuple(jax.ShapeDtypeStruct(t.shape, F32) for t in self.total)
        return Exchange(tuple(self.total), shapes, plan, len(self.total))

    def rider(self, host):
        stages = {"swap": self._swap_halves, "chips": self._to_chips, "share": self._share}
        return stages[self.stage_of[host]]() if host in self.stage_of else None

    def done(self, host, outs):
        self._after(self.stage_of[host], outs)

    def _after(self, stage, outs):
        if stage == "swap":
            sums = [_pair_sum(f"reduce_pair_sum_{self.tag}_{i}", g, r) for i, (g, r) in enumerate(zip(self.grads, outs))]
            self.pair, self.own = [s[0] for s in sums], [s[1] for s in sums]
        elif stage == "chips":
            self.total = [
                _slabwise(f"reduce_chip_sum_{self.tag}_{i}",
                          lambda a, b, c_, d: (a + b.astype(F32) + c_.astype(F32) + d.astype(F32),),
                          [own, (r, 0), (r, 1), (r, 2)], [F32])[0]
                for i, (own, r) in enumerate(zip(self.own, outs))]
        else:
            self.theirs = outs

    def run_now(self):
        self._after("swap", _run_exchange(f"reduce_pair_{self.tag}", self._swap_halves()))
        self._after("chips", _run_exchange(f"reduce_chips_{self.tag}", self._to_chips()))
        self._after("share", _run_exchange(f"reduce_share_{self.tag}", self._share()))

    def result(self):
        return list(zip(self.total, self.theirs))


def _adamw_layer(name, l, w, m, v, mine, theirs, prev, rows_per_step=256):
    _, rows2, cols = w.shape
    rows = rows2 // 2
    tr = min(rows_per_step, rows)
    while rows % tr:
        tr //= 2
    steps = rows // tr

    def body(w_ref, m_ref, v_ref, mine_ref, theirs_ref, *rest):
        g_ref, d_ref, nm_ref, nv_ref = rest[-4:]
        g = jnp.where(pl.program_id(0) == lax.axis_index("c"), mine_ref[...], theirs_ref[...])
        d, nm, nv = _adamw_math(w_ref[...], g, m_ref[...], v_ref[...])
        g_ref[...], d_ref[...], nm_ref[...], nv_ref[...] = g, d, nm, nv

    half = pl.BlockSpec((None, tr, cols), lambda h, i: (l, h * steps + i, 0))
    one = pl.BlockSpec((None, tr, cols), lambda h, i: (0, i, 0))
    kept = [] if prev is None else list(prev)
    return _pcall(
        body,
        name=name,
        grid=(2, rows // tr),
        in_specs=[half, half, half, one, one] + [pl.BlockSpec(memory_space=pl.ANY)] * len(kept),
        out_specs=[half] * 4,
        out_shape=[jax.ShapeDtypeStruct(w.shape, F32)] * 4,
        input_output_aliases={5 + i: i for i in range(len(kept))},
        compiler_params=_params(2 * 9 * _nbytes((tr, cols + LANES), F32)),
    )(w, m, v, mine, theirs, *kept)


def _allreduce_small(v):
    rows = v.shape[0]

    def body(v_ref, sum_ref, all_ref, send_sems, recv_sems, local_sem):
        x, y, c = lax.axis_index("x"), lax.axis_index("y"), lax.axis_index("c")
        sibling = (x, y, 1 - c)
        chips = [(1 - x, y), (x, 1 - y), (1 - x, 1 - y)]

        def slab(px, py, pc):
            return all_ref.at[pl.ds((4 * px + 2 * py + pc) * rows, rows), :]

        def copy(k, block, to, src=None):
            return pltpu.make_async_remote_copy(
                src_ref=slab(*block) if src is None else src, dst_ref=slab(*block), send_sem=send_sems.at[k],
                recv_sem=recv_sems.at[k], device_id=to, device_id_type=MESH)

        mine = pltpu.make_async_copy(v_ref, slab(x, y, c), local_sem)
        mine.start()
        first = [copy(0, (x, y, c), sibling, src=v_ref)]
        first += [copy(1 + j, (x, y, c), (*chip, c), src=v_ref) for j, chip in enumerate(chips)]
        for cp in first:
            cp.start()
        passed = [copy(4 + j, (*chip, c), sibling) for j, chip in enumerate(chips)]
        for j, chip in enumerate(chips):
            copy(1 + j, (*chip, c), (x, y, c)).wait_recv()
            passed[j].start()
        copy(0, (x, y, 1 - c), (x, y, c)).wait_recv()
        for j, chip in enumerate(chips):
            copy(4 + j, (*chip, 1 - c), (x, y, c)).wait_recv()
        for cp in first + passed:
            cp.wait_send()
        mine.wait()
        total = all_ref[pl.ds(0, rows), :]
        for d in range(1, N_DEV):
            total = total + all_ref[pl.ds(d * rows, rows), :]
        sum_ref[...] = total

    vm = pl.BlockSpec(memory_space=pltpu.VMEM)
    return _pcall(
        body,
        name="allreduce_small",
        in_specs=[vm],
        out_specs=vm,
        out_shape=jax.ShapeDtypeStruct((rows, LANES), F32),
        scratch_shapes=[pltpu.VMEM((N_DEV * rows, LANES), F32), pltpu.SemaphoreType.DMA((7,)),
                        pltpu.SemaphoreType.DMA((7,)), pltpu.SemaphoreType.DMA],
    )(v)


def _pad_cols(a, before, total):
    return jnp.pad(a, ((0, 0), (before, total - before - a.shape[1])))


def _layer_weights(cfg, w_in, w_uq, w_ukv):
    w = cfg.width
    qkv, f, cq, ckv, kr, qm, gates = jnp.split(w_in, list(_cumsum(cfg.in_splits))[:-1], axis=1)
    wa = jnp.concatenate([qkv, qm], axis=1)
    ws = jnp.concatenate([_pad_cols(f, 0, LANES), cq, ckv, _pad_cols(kr, MLA_NOPE, LANES)], axis=1)
    wq = jnp.pad(w_uq.reshape(cfg.q_rank, cfg.mla_h, MLA_NOPE + MLA_ROPE), ((0, 0), (0, 0), (0, LANES - MLA_NOPE - MLA_ROPE)))
    wq = wq.reshape(cfg.q_rank, cfg.mla_h * LANES)
    kv = w_ukv.reshape(cfg.kv_rank, cfg.mla_h, MLA_NOPE + MLA_V)
    wk = jnp.pad(kv[:, :, :MLA_NOPE], ((0, 0), (0, 0), (0, LANES - MLA_NOPE))).reshape(cfg.kv_rank, cfg.mla_h * LANES)
    wv = kv[:, :, MLA_NOPE:].reshape(cfg.kv_rank, cfg.mla_h * MLA_V)
    del w
    return wa, gates, ws, wq, wk, wv


def _cumsum(xs):
    out, t = [], 0
    for v in xs:
        t += v
        out.append(t)
    return out


def _layer_weight_grads(cfg, dwa, dwg, dws, dwq, dwk, dwv):
    w, qr, kvr = cfg.width, cfg.q_rank, cfg.kv_rank
    off_kr = LANES + qr + kvr + MLA_NOPE
    dw_in = jnp.concatenate([
        dwa[:, :3 * w], dws[:, :cfg.fox_h], dws[:, LANES:LANES + qr], dws[:, LANES + qr:LANES + qr + kvr],
        dws[:, off_kr:off_kr + MLA_ROPE], dwa[:, 3 * w:], *dwg], axis=1)
    dw_uq = dwq.reshape(qr, cfg.mla_h, LANES)[:, :, :MLA_NOPE + MLA_ROPE].reshape(qr, cfg.mla_h * (MLA_NOPE + MLA_ROPE))
    dw_ukv = jnp.concatenate([dwk.reshape(kvr, cfg.mla_h, LANES)[:, :, :MLA_NOPE], dwv.reshape(kvr, cfg.mla_h, MLA_V)],
                             axis=2).reshape(kvr, cfg.mla_h * (MLA_NOPE + MLA_V))
    return dw_in, dw_uq, dw_ukv


def _attn_specs(cfg, batch):
    t = min(ATTN_TILE, cfg.seq)
    common = dict(batch=batch, sq=cfg.seq, chunk=cfg.chunk, tq=t)
    fox = Attn(sk=cfg.seq, groups=cfg.fox_h // 2, hq=2, hv=2, mode="fox", scale=FOX_DH ** -0.5, tk=t, **common)
    mla = Attn(sk=cfg.seq, groups=cfg.mla_h // 2, hq=1, hv=2, mode="chunk",
               scale=(MLA_NOPE + MLA_ROPE) ** -0.5, tk=t, **common)
    mem = Attn(sk=cfg.n_mem, groups=cfg.mem_h, hq=1, hv=1, mode="none", scale=MEM_DH ** -0.5, tk=cfg.n_mem,
               **dict(common, tq=min(MEM_ATTN_TILE, cfg.seq)))
    return fox, mla, mem


def _small_core(cfg, ps, bf, gq, gkv):
    qr, kvr = cfg.q_rank, cfg.kv_rank
    z = ps[:, :LANES] + bf
    logf = jnp.minimum(z, 0.0) - jnp.log1p(jnp.exp(-jnp.abs(z)))
    nq = _rms(ps[:, LANES:LANES + qr], gq)
    nkv = _rms(ps[:, LANES + qr:LANES + qr + kvr], gkv)
    return logf, nq, nkv


def _layer_fwd(cfg, l, batch, h, hb, mem_b, rope_c, rope_s, hosts, lw, bf_pad, g_cq, g_ckv, ln1, ln2):
    (wa, wg, ws, wq, wk, wv, wmkv), later_weights = lw
    w, d = cfg.width, cfg.d
    fox, mla, mem = _attn_specs(cfg, batch)
    nw = w // LANES
    pa = _mm(f"proj_a_{l}", "nn", [(hb, wa)], [BF16])
    gl = _mm(f"proj_gates_{l}", "nn", [(hb, wg)], [F32])
    ps = _mm(f"proj_small_{l}", "nn", [(hb, ws)], [F32], tn=cfg.small_w)

    def small_fwd(ps_, c_, s_, bf_, gq_, gkv_):
        logf, nq, nkv = _small_core(cfg, ps_, bf_, gq_, gkv_)
        kpe = _rope(ps_[:, cfg.small_w - LANES:], c_, s_)
        return logf, nq, nkv, kpe

    logf, nq, nkv, kpe = _rowwise(
        f"small_fwd_{l}", small_fwd, [ps, rope_c, rope_s], [bf_pad, g_cq, g_ckv],
        [(LANES, F32), (cfg.q_rank, BF16), (cfg.kv_rank, BF16), (LANES, F32)])
    cfk = _forget_cumsum(f"cum_forget_{l}", logf, batch, cfg.seq, cfg.fox_h, fox.tq)

    qf = _mm(f"mla_q_{l}", "nn", [(nq, wq)], [BF16], tn=wq.shape[1],
             epi=lambda acc, c_, s_: (_rope(acc, c_, s_),), row_extras=[rope_c, rope_s])
    kf = _mm(f"mla_k_{l}", "nn", [(nkv, wk)], [BF16], tn=wk.shape[1],
             epi=lambda acc, kp: (acc + jnp.tile(kp, (1, cfg.mla_h)),), row_extras=[kpe])
    vb = _mm(f"mla_v_{l}", "nn", [(nkv, wv)], [BF16])
    mkv = _mm(f"mem_kv_{l}", "nn", [(mem_b, wmkv)], [BF16])

    o_a, lse_a = _attn_fwd(f"fox_fwd_{l}", fox, (pa, 0), (pa, nw), (pa, 2 * nw), cfk,
                           hosts=hosts, host="fox_fwd")
    o_b, lse_b = _attn_fwd(f"mla_fwd_{l}", mla, (qf, 0), (kf, 0), (vb, 0), hosts=hosts, host="mla_fwd")
    o_c, lse_c = _attn_fwd(f"mem_fwd_{l}", mem, (pa, 3 * nw), (mkv, 0), (mkv, nw), hosts=hosts, host="mem_fwd")
    wbr, wout, wff1, wff2 = later_weights()

    def gated_sum(branches, gl_):
        g = jax.nn.sigmoid(gl_)
        return (sum(g[:, n * d:(n + 1) * d] * bp for n, bp in enumerate(branches)),)

    merged = _mm(f"merge_{l}", "nn", [(o, wbr[n]) for n, o in enumerate((o_a, o_b, o_c))], [BF16], tm=256, tn=d,
                 combine=gated_sum, row_extras=[gl])

    def post_ln(acc, res, g_, b_):
        z = cfg.alpha * res + acc
        y = _ln(z, g_, b_)
        return z, y, y

    z1, h1, h1b = _mm(f"out_ln1_{l}", "nn", [(merged, wout)], [F32, F32, BF16], tm=256, tn=d,
                      epi=post_ln, row_extras=[h], bc_extras=list(ln1))
    u, a = _mm(f"ff1_{l}", "nn", [(h1b, wff1)], [BF16, BF16],
               epi=lambda acc: (acc, jnp.square(jnp.maximum(acc, 0.0))))
    z2, h2, h2b = _mm(f"ff2_ln2_{l}", "nn", [(a, wff2)], [F32, F32, BF16], tm=256, tn=d,
                      epi=post_ln, row_extras=[h1], bc_extras=list(ln2))
    saved = dict(hb=hb, pa=pa, gl=gl, ps=ps, nq=nq, nkv=nkv, cfk=cfk, qf=qf, kf=kf, vb=vb, mkv=mkv,
                 o=(o_a, o_b, o_c), lse=(lse_a, lse_b, lse_c), merged=merged, z1=z1, h1b=h1b, u=u, a=a, z2=z2,
                 lw=(wa, wg, ws, wq, wk, wv, wmkv, wbr, wout, wff1, wff2))
    return h2, h2b, saved


def _ln_bwd(name, cfg, ga, gb, z, g, b):
    d = cfg.d

    def fn(*vals):
        if gb is None:
            ga_, z_, g_, b_ = vals
            dy = ga_
        else:
            ga_, gb_, z_, g_, b_ = vals
            dy = ga_ + cfg.alpha * gb_
        _, vjp = jax.vjp(_ln, z_, g_, b_)
        dz, dg, db = vjp(dy)
        return dz, dz, dg, db

    rows = [ga, z] if gb is None else [ga, gb, z]
    return _rowwise(name, fn, rows, [g, b], [(d, F32), (d, BF16)], accs=[(1, d), (1, d)])


def _layer_bwd(cfg, l, batch, ga, gb, sv, mem_b, rope_c, rope_s, hosts, lw, bf_pad, g_cq, g_ckv, ln1, ln2):
    wa, wg, ws, wq, wk, wv, wmkv, wbr, wout, wff1, wff2 = lw
    w, d = cfg.width, cfg.d
    fox, mla, mem = _attn_specs(cfg, batch)
    nw = w // LANES
    gdt = BF16

    dz2, dz2b, dg2, db2 = _ln_bwd(f"ln2_bwd_{l}", cfg, ga, gb, sv["z2"], *ln2)
    du = _mm(f"ff2_bwd_x_{l}", "nt", [(dz2b, wff2)], [BF16],
             epi=lambda acc, u_: (acc * (2.0 * jnp.maximum(u_.astype(F32), 0.0)),), row_extras=[sv["u"]],
             hosts=hosts, host="ff2_bwd_x")
    dwff2 = _mm(f"ff2_bwd_w_{l}", "tn", [(sv["a"], dz2b)], [gdt])
    dwff1 = _mm(f"ff1_bwd_w_{l}", "tn", [(sv["h1b"], du)], [gdt])
    dh1 = _mm(f"ff1_bwd_x_{l}", "nt", [(du, wff1)], [F32])
    dz1, dz1b, dg1, db1 = _ln_bwd(f"ln1_bwd_{l}", cfg, dh1, dz2, sv["z1"], *ln1)
    dmerged = _mm(f"out_bwd_x_{l}", "nt", [(dz1b, wout)], [F32])
    dwout = _mm(f"out_bwd_w_{l}", "tn", [(sv["merged"], dz1b)], [gdt])

    def gated_sum_bwd(branches, dm, gl_):
        g = jax.nn.sigmoid(gl_)
        gates = [g[:, n * d:(n + 1) * d] for n in range(3)]
        d_logits = [dm * bp * gn * (1.0 - gn) for bp, gn in zip(branches, gates)]
        return tuple(d_logits) + tuple(dm * gn for gn in gates)

    res = _mm(f"merge_bwd_{l}", "nn", [(sv["o"][n], wbr[n]) for n in range(3)], [BF16] * 6, tm=256, tn=d,
              combine=gated_sum_bwd, row_extras=[dmerged, sv["gl"]])
    dgls, dbps = res[:3], res[3:]
    dos = [_mm(f"branch_bwd_x_{n}_{l}", "nt", [(dbps[n], wbr[n])], [BF16]) for n in range(3)]
    dwbr = jnp.stack([_mm(f"branch_bwd_w_{n}_{l}", "tn", [(sv["o"][n], dbps[n])], [gdt]) for n in range(3)])
    if hosts is not None:
        hosts.early_grads(dict(w_br=dwbr, w_out=dwout, w_ff1=dwff1, w_ff2=dwff2))

    pa = sv["pa"]
    dq_a, dk_a, dv_a, dcfk = _attn_bwd(f"fox_bwd_{l}", fox, (pa, 0), (pa, nw), (pa, 2 * nw), sv["o"][0], sv["lse"][0],
                                       dos[0], sv["cfk"], hosts=hosts, host="fox_bwd")
    dqf, dkf, dvb = _attn_bwd(f"mla_bwd_{l}", mla, (sv["qf"], 0), (sv["kf"], 0), (sv["vb"], 0), sv["o"][1],
                              sv["lse"][1], dos[1], hosts=hosts, host="mla_bwd")
    dqm, dmk, dmv = _attn_bwd(f"mem_bwd_{l}", mem, (pa, 3 * nw), (sv["mkv"], 0), (sv["mkv"], nw), sv["o"][2],
                              sv["lse"][2], dos[2], hosts=hosts, host="mem_bwd")
    dwmkv = _mm(f"mem_kv_bwd_w_{l}", "tn", [(mem_b, jnp.concatenate([dmk, dmv], axis=1))], [gdt])

    (dq_raw,) = _rowwise(f"mla_q_rope_bwd_{l}", lambda dy, c_, s_: (_rope_t(dy, c_, s_),), [dqf, rope_c, rope_s], [],
                         [(wq.shape[1], BF16)])
    dwq = _mm(f"mla_q_bwd_w_{l}", "tn", [(sv["nq"], dq_raw)], [gdt])
    dnq = _mm(f"mla_q_bwd_x_{l}", "nt", [(dq_raw, wq)], [F32])
    dwk = _mm(f"mla_k_bwd_w_{l}", "tn", [(sv["nkv"], dkf)], [gdt])
    dwv = _mm(f"mla_v_bwd_w_{l}", "tn", [(sv["nkv"], dvb)], [gdt])
    dnkv = _mm(f"mla_kv_bwd_x_{l}", "nt", [(dkf, wk), (dvb, wv)], [F32])

    dlogf = _forget_cumsum_bwd(f"cum_forget_bwd_{l}", dcfk, batch, cfg.seq, cfg.fox_h, fox.tq)

    def small_bwd(ps_, dlogf_, dnq_, dnkv_, dkf_, c_, s_, bf_, gq_, gkv_):
        _, vjp = jax.vjp(functools.partial(_small_core, cfg), ps_, bf_, gq_, gkv_)
        dps, dbf, dgq, dgkv = vjp((dlogf_, dnq_, dnkv_))
        dkpe = dkf_[:, :LANES].astype(F32)
        for hh in range(1, cfg.mla_h):
            dkpe = dkpe + dkf_[:, hh * LANES:(hh + 1) * LANES].astype(F32)
        lane = lax.broadcasted_iota(jnp.int32, (1, LANES), 1)
        dkpe = jnp.where((lane >= MLA_NOPE) & (lane < MLA_NOPE + MLA_ROPE), dkpe, 0.0)
        dkr = _rope_t(dkpe, c_, s_)
        dps = jnp.concatenate([dps[:, :cfg.small_w - LANES], dkr], axis=1)
        return dps, dbf, dgq, dgkv

    dps, dbf, dgq, dgkv = _rowwise(
        f"small_bwd_{l}", small_bwd, [sv["ps"], dlogf, dnq, dnkv, dkf, rope_c, rope_s], [bf_pad, g_cq, g_ckv],
        [(cfg.small_w, BF16)], accs=[(1, LANES), (1, cfg.q_rank), (1, cfg.kv_rank)])

    dpa = jnp.concatenate([dq_a.astype(BF16), dk_a, dv_a, dqm.astype(BF16)], axis=1)
    hb = sv["hb"]
    gate_pairs = [(dgls[n], wg[:, n * d:(n + 1) * d]) for n in range(3)]
    dh = _mm(f"proj_bwd_x_{l}", "nt", [(dpa, wa)] + gate_pairs + [(dps, ws)], [F32], tn=d)
    dwa = _mm(f"proj_a_bwd_w_{l}", "tn", [(hb, dpa)], [gdt])
    dwg = [_mm(f"proj_gates_bwd_w_{n}_{l}", "tn", [(hb, dgls[n])], [gdt]) for n in range(3)]
    dws = _mm(f"proj_small_bwd_w_{l}", "tn", [(hb, dps)], [gdt], tn=cfg.small_w)
    dw_in, dw_uq, dw_ukv = _layer_weight_grads(cfg, dwa, dwg, dws, dwq, dwk, dwv)
    big = dict(w_in=dw_in, w_uq=dw_uq, w_ukv=dw_ukv, w_mem_kv=dwmkv, w_br=dwbr, w_out=dwout,
               w_ff1=dwff1, w_ff2=dwff2)
    small = dict(b_forget=dbf[0, :cfg.fox_h], g_cq=dgq[0], g_ckv=dgkv[0], ln1_g=dg1[0], ln1_b=db1[0],
                 ln2_g=dg2[0], ln2_b=db2[0])
    return dh, dz1, big, small


def _rope_tables(positions):
    inv_freq = ROPE_BASE ** (-jnp.arange(0, MLA_ROPE, 2, dtype=F32) / MLA_ROPE)
    ang = positions.astype(F32).reshape(-1)[:, None] * inv_freq
    cos, sin = jnp.cos(ang), jnp.sin(ang)
    t = ang.shape[0]
    rope_c = jnp.concatenate([jnp.ones((t, MLA_NOPE), F32), cos, cos, jnp.zeros((t, LANES - MLA_NOPE - MLA_ROPE), F32)], axis=1)
    rope_s = jnp.concatenate([jnp.zeros((t, MLA_NOPE), F32), -sin, sin, jnp.zeros((t, LANES - MLA_NOPE - MLA_ROPE), F32)], axis=1)
    return rope_c, rope_s


def _local_step(cfg, x, mem, positions, target, small_w, comm):
    batch = x.shape[0]
    d, depth = cfg.d, cfg.depth
    t = batch * cfg.seq
    x2, tgt = x.reshape(t, d), target.reshape(t, d)
    mem_b = mem.reshape(batch * cfg.n_mem, d).astype(BF16)
    rope_c, rope_s = _rope_tables(positions)
    row = lambda v: v.reshape(1, -1)
    ln_in = (row(small_w["ln_in_g"]), row(small_w["ln_in_b"]))

    h, hb = _rowwise("ln_in", lambda x_, g_, b_: (_ln(x_, g_, b_),) * 2, [x2], list(ln_in), [(d, F32), (d, BF16)])
    layers, saves = [], []
    for l in range(depth):
        first = comm.weights(l, LATE)
        lw = _layer_weights(cfg, first["w_in"], first["w_uq"], first["w_ukv"]) + (first["w_mem_kv"],)
        later = lambda l=l: tuple(comm.weights(l, EARLY).values())
        par = dict(
            bf_pad=jnp.pad(row(small_w["b_forget"][l]), ((0, 0), (0, LANES - cfg.fox_h))),
            g_cq=row(small_w["g_cq"][l]), g_ckv=row(small_w["g_ckv"][l]),
            ln1=(row(small_w["ln1_g"][l]), row(small_w["ln1_b"][l])),
            ln2=(row(small_w["ln2_g"][l]), row(small_w["ln2_b"][l])))
        h, hb, sv = _layer_fwd(cfg, l, batch, h, hb, mem_b, rope_c, rope_s, comm.forward_hosts(l), (lw, later), **par)
        layers.append(dict(par, lw=sv.pop("lw")))
        saves.append(sv)

    def loss_fn(y, tg):
        err = y - tg
        part = 0.5 * jnp.sum(jnp.mean(err * err, axis=-1, keepdims=True), axis=0, keepdims=True)
        return err * (1.0 / d), jnp.broadcast_to(part, (1, LANES))

    ga, loss_acc = _rowwise("loss", loss_fn, [h, tgt], [], [(d, F32)], accs=[(1, LANES)])
    gb = None
    small_g = {k: [None] * depth for k in ("b_forget", "g_cq", "g_ckv", "ln1_g", "ln1_b", "ln2_g", "ln2_b")}
    for l in reversed(range(depth)):
        ga, gb, big, small = _layer_bwd(cfg, l, batch, ga, gb, saves[l], mem_b, rope_c, rope_s,
                                        comm.backward_hosts(l), **layers[l])
        comm.grads(l, big)
        for k, v in small.items():
            small_g[k][l] = v
    dx, _, dg_in, db_in = _ln_bwd("ln_in_bwd", cfg, ga, gb, x2, *ln_in)
    small_g = {k: jnp.stack(v) for k, v in small_g.items()}
    small_g["ln_in_g"], small_g["ln_in_b"] = dg_in[0], db_in[0]
    return loss_acc[0, 0], dx.reshape(x.shape), small_g


BIG = ("w_in", "w_uq", "w_ukv", "w_mem_kv", "w_br", "w_out", "w_ff1", "w_ff2")
SMALL = ("ln_in_g", "ln_in_b", "b_forget", "g_cq", "g_ckv", "ln1_g", "ln1_b", "ln2_g", "ln2_b")
ROW_CUT = ("w_mem_kv", "w_out", "w_ff2")
LATE, EARLY = BIG[:4], BIG[4:]


def _shard_2d(a):
    cols = a.shape[-1]
    rows = a.size // cols
    return a.reshape(2, rows // 2, cols)


def _full_from_slots(name, slots, shard_shape):
    if name in ROW_CUT and len(shard_shape) == 2:
        return slots.reshape((N_CHIPS * shard_shape[0], shard_shape[1]))
    parts = slots.reshape((N_CHIPS,) + shard_shape)
    axis = len(shard_shape) - (2 if name in ROW_CUT else 1)
    return jnp.concatenate([parts[i] for i in range(N_CHIPS)], axis=axis)


def _slots_from_full(name, full, shard_shape):
    if name in ROW_CUT and len(shard_shape) == 2:
        return full.reshape(N_CHIPS, 2, shard_shape[0] // 2, shard_shape[1])
    axis = len(shard_shape) - (2 if name in ROW_CUT else 1)
    parts = jnp.stack(jnp.split(full, N_CHIPS, axis=axis))
    cols = shard_shape[-1]
    return parts.reshape(N_CHIPS, 2, -1, cols)


def _pack_small(cfg, vals):
    flat = jnp.concatenate([vals[k].reshape(-1).astype(F32) for k in SMALL])
    pad = (-flat.shape[0]) % (LANES * LANES)
    return jnp.pad(flat, (0, pad)).reshape(-1, LANES)


def _unpack_small(packed, like):
    flat, out, off = packed.reshape(-1), {}, 0
    for k in SMALL:
        n = like[k].size
        out[k] = flat[off:off + n].reshape(like[k].shape)
        off += n
    return out


class LayerComm:
    def __init__(self, cfg, w, m, v):
        self.cfg, self.w = cfg, w
        self.shards = [{k: _shard_2d(w[k][l].astype(BF16)) for k in BIG} for l in range(cfg.depth)]
        self.got = {LATE: _gather_now([self.shards[0][k] for k in LATE])}
        self.sets = {}
        self.pending = None
        rows = lambda a: a.reshape(a.shape[0], -1, a.shape[-1])
        self.state = {k: [rows(a[k]) for a in (w, m, v)] for k in BIG}
        self.outs = {k: None for k in BIG}

    def weights(self, l, names):
        if names in self.got:
            got = self.got.pop(names)
        elif names is EARLY:
            got = self.sets.pop((l, EARLY[:2])).got + self.sets.pop((l, EARLY[2:])).got
        else:
            got = self.sets.pop((l, names)).got
        mine = (_chip(), 0, 0, 0)
        return {k: _full_from_slots(k, lax.dynamic_update_slice(g, self.shards[l][k][None], mine), self.w[k].shape[1:])
                for k, g in zip(names, got)}

    def forward_hosts(self, l):
        for names, hosts in ((EARLY[2:], ("fox_fwd", "mla_fwd")), (EARLY[:2], ("mla_fwd", "mem_fwd"))):
            self.sets[l, names] = GatherSet([self.shards[l][k] for k in names], hosts)
        if l + 1 < self.cfg.depth:
            self.sets[l + 1, LATE] = GatherSet([self.shards[l + 1][k] for k in LATE], ("mla_fwd", "mem_fwd"))
        return Together([s for (layer, _), s in self.sets.items() if layer in (l, l + 1)])

    def _reduce(self, tag, names, big, hosts):
        return ReduceLayer(tag, [_slots_from_full(k, big[k], self.w[k].shape[1:]) for k in names], hosts)

    def backward_hosts(self, l):
        comm = self

        class Riders(Together):
            def early_grads(self, grads):
                comm.early = comm._reduce(f"{l}e", EARLY, grads, ("fox_bwd", "mla_bwd", "mem_bwd"))
                self.members.append(comm.early)

        return Riders([self.pending[1]] if self.pending else [])

    def _update(self, l, names, reduce):
        for k, (mine, theirs) in zip(names, reduce.result()):
            self.outs[k] = _adamw_layer(f"adamw_{k}_{l}", l, *self.state[k], mine, theirs, self.outs[k])

    def grads(self, l, big):
        self._update(l, EARLY, self.early)
        if self.pending:
            self._update(self.pending[0], LATE, self.pending[1])
        self.pending = (l, self._reduce(f"{l}l", LATE, big, ("ff2_bwd_x", "fox_bwd", "mla_bwd")))

    def finish(self):
        self.pending[1].run_now()
        self._update(self.pending[0], LATE, self.pending[1])
        return {k: tuple(a.reshape(self.w[k].shape) for a in self.outs[k]) for k in BIG}


def _step(cfg, x, mem, positions, target, w, m, v):
    comm = LayerComm(cfg, w, m, v)
    small_w = {k: w[k] for k in SMALL}
    loss_local, dx, small_g = _local_step(cfg, x, mem, positions, target, small_w, comm)
    loss = lax.psum(loss_local, ("x", "y", "c"))
    outs_big = comm.finish()

    g_small = _allreduce_small(_pack_small(cfg, small_g))
    packs = [_pack_small(cfg, {k: d_[k] for k in SMALL}) for d_ in (w, m, v)]
    dl, nm, nv = _slabwise("adamw_small", _adamw_math, [a[None] for a in (packs[0], g_small, packs[1], packs[2])],
                           [F32, F32, F32])
    outs_small = [_unpack_small(a[0] if a.ndim == 3 else a, w) for a in (g_small, dl, nm, nv)]

    names = SMALL[:2] + ("w_in", "b_forget", "w_uq", "g_cq", "w_ukv", "g_ckv", "w_mem_kv", "w_br", "w_out",
                         "ln1_g", "ln1_b", "w_ff1", "w_ff2", "ln2_g", "ln2_b")
    result = [loss, dx]
    for part in range(4):
        for k in names:
            result.append(outs_big[k][part] if k in outs_big else outs_small[part][k])
    return tuple(result)


def kernel(x, mem, positions, ln_in_g, ln_in_b, w_in, b_forget, w_uq, g_cq, w_ukv, g_ckv, w_mem_kv, w_br, w_out, ln1_g, ln1_b, w_ff1, w_ff2, ln2_g, ln2_b, loss_target, m_ln_in_g, m_ln_in_b, m_w_in, m_b_forget, m_w_uq, m_g_cq, m_w_ukv, m_g_ckv, m_w_mem_kv, m_w_br, m_w_out, m_ln1_g, m_ln1_b, m_w_ff1, m_w_ff2, m_ln2_g, m_ln2_b, v_ln_in_g, v_ln_in_b, v_w_in, v_b_forget, v_w_uq, v_g_cq, v_w_ukv, v_g_ckv, v_w_mem_kv, v_w_br, v_w_out, v_ln1_g, v_ln1_b, v_w_ff1, v_w_ff2, v_ln2_g, v_ln2_b):
    w = dict(ln_in_g=ln_in_g, ln_in_b=ln_in_b, w_in=w_in, b_forget=b_forget, w_uq=w_uq, g_cq=g_cq, w_ukv=w_ukv,
             g_ckv=g_ckv, w_mem_kv=w_mem_kv, w_br=w_br, w_out=w_out, ln1_g=ln1_g, ln1_b=ln1_b, w_ff1=w_ff1,
             w_ff2=w_ff2, ln2_g=ln2_g, ln2_b=ln2_b)
    m = dict(ln_in_g=m_ln_in_g, ln_in_b=m_ln_in_b, w_in=m_w_in, b_forget=m_b_forget, w_uq=m_w_uq, g_cq=m_g_cq,
             w_ukv=m_w_ukv, g_ckv=m_g_ckv, w_mem_kv=m_w_mem_kv, w_br=m_w_br, w_out=m_w_out, ln1_g=m_ln1_g,
             ln1_b=m_ln1_b, w_ff1=m_w_ff1, w_ff2=m_w_ff2, ln2_g=m_ln2_g, ln2_b=m_ln2_b)
    v = dict(ln_in_g=v_ln_in_g, ln_in_b=v_ln_in_b, w_in=v_w_in, b_forget=v_b_forget, w_uq=v_w_uq, g_cq=v_g_cq,
             w_ukv=v_w_ukv, g_ckv=v_g_ckv, w_mem_kv=v_w_mem_kv, w_br=v_w_br, w_out=v_w_out, ln1_g=v_ln1_g,
             ln1_b=v_ln1_b, w_ff1=v_w_ff1, w_ff2=v_w_ff2, ln2_g=v_ln2_g, ln2_b=v_ln2_b)
    return _step(Cfg(), x, mem, positions, loss_target, w, m, v)
```

```python
import functools
from typing import NamedTuple

import jax
import jax.numpy as jnp
from jax import lax
from jax.experimental import pallas as pl
from jax.experimental.pallas import tpu as pltpu

F32 = jnp.float32
BF16 = jnp.bfloat16
MESH = pl.DeviceIdType.MESH

LANES = 128
SUBLANES = 8
VMEM_BYTES = 64 * 1024 * 1024
N_CHIPS = 4
N_DEV = 8

FOX_DH = 64
MLA_NOPE = 64
MLA_ROPE = 32
MLA_V = 64
MEM_DH = 128
ROPE_BASE = 10000.0
LN_EPS = 1e-5
RMS_EPS = 1e-6
NEG_INF = -1e30
LOG2E = 1.4426950408889634
ATTN_TILE = 512
MEM_ATTN_TILE = 1024

ADAM_LR = 0.001
ADAM_B1 = 0.9
ADAM_B2 = 0.999
ADAM_EPS = 1e-08
ADAM_WD = 0.01
ADAM_STEP = 10


class Cfg(NamedTuple):
    d: int = 1024
    depth: int = 4
    seq: int = 2048
    chunk: int = 64
    n_mem: int = 256
    fox_h: int = 8
    mla_h: int = 8
    q_rank: int = 384
    kv_rank: int = 256
    mem_h: int = 4
    d_ff: int = 4096

    @property
    def width(self):
        return self.fox_h * FOX_DH

    @property
    def alpha(self):
        return (2 * self.depth) ** 0.25

    @property
    def small_w(self):
        return LANES + self.q_rank + self.kv_rank + LANES

    @property
    def in_splits(self):
        return (3 * self.width, self.fox_h, self.q_rank, self.kv_rank, MLA_ROPE, self.width, 3 * self.d)


class Exchange(NamedTuple):
    ins: tuple
    out_shapes: tuple
    plan: object
    n_copies: int
    aliases: dict = {}


def _peer(rel):
    x, y, c = lax.axis_index("x"), lax.axis_index("y"), lax.axis_index("c")
    if rel == "c":
        return (x, y, 1 - c)
    return ((1 - x) if rel in (1, 3) else x, (1 - y) if rel in (2, 3) else y, c)


def _exchange_copies(ex, in_refs, out_refs, send_sems, recv_sems):
    planned = ex.plan(in_refs, out_refs)
    assert len(planned) == ex.n_copies, len(planned)
    return [pltpu.make_async_remote_copy(src_ref=src, dst_ref=dst, send_sem=send_sems.at[i], recv_sem=recv_sems.at[i],
                                         device_id=_peer(rel), device_id_type=MESH)
            for i, (src, dst, rel) in enumerate(planned)]


def _pcall(body, rider=None, pin_all=False, **kw):
    if rider is not None:
        n_in, n_out, grid = len(kw["in_specs"]), len(kw["out_specs"]), kw["grid"]
        n_rin, n_rout = len(rider.ins), len(rider.out_shapes)
        host = body

        def body(*refs):
            ins, rins = refs[:n_in], refs[n_in:n_in + n_rin]
            outs = refs[n_in + n_rin:n_in + n_rin + n_out]
            routs = refs[n_in + n_rin + n_out:n_in + n_rin + n_out + n_rout]
            scratch = refs[n_in + n_rin + n_out + n_rout:-2]
            copies = _exchange_copies(rider, rins, routs, refs[-2], refs[-1])
            first = functools.reduce(jnp.logical_and, [pl.program_id(a) == 0 for a in range(len(grid))])
            last = functools.reduce(jnp.logical_and, [pl.program_id(a) == n - 1 for a, n in enumerate(grid)])

            @pl.when(first)
            def _():
                for cp in copies:
                    cp.start()

            host(*ins, *outs, *scratch)

            @pl.when(last)
            def _():
                for cp in copies:
                    cp.wait()

        any_spec = pl.BlockSpec(memory_space=pl.ANY)
        sems = [pltpu.SemaphoreType.DMA((rider.n_copies,))] * 2
        kw = dict(
            kw,
            in_specs=list(kw["in_specs"]) + [any_spec] * n_rin,
            out_specs=list(kw["out_specs"]) + [any_spec] * n_rout,
            out_shape=list(kw["out_shape"]) + list(rider.out_shapes),
            scratch_shapes=list(kw.get("scratch_shapes", ())) + sems,
            input_output_aliases={**kw.get("input_output_aliases", {}),
                                  **{n_in + i: n_out + o for i, o in rider.aliases.items()}},
        )
    call = pl.pallas_call(body, **kw)
    n_pinned = len(kw["in_specs"]) if pin_all else (len(rider.ins) if rider is not None else 0)

    def run(*ops):
        n_free = len(ops) - n_pinned
        return call(*ops[:n_free], *[pltpu.with_memory_space_constraint(o, pltpu.HBM) for o in ops[n_free:]])

    return run


class Hosts:
    def rider(self, host):
        return None

    def done(self, host, outs):
        pass

    def early_grads(self, grads):
        pass


def _merge_exchanges(exs):
    n_ins = [len(e.ins) for e in exs]
    n_outs = [len(e.out_shapes) for e in exs]

    def plan(in_refs, out_refs):
        copies, i, o = [], 0, 0
        for e, ni, no in zip(exs, n_ins, n_outs):
            copies += e.plan(in_refs[i:i + ni], out_refs[o:o + no])
            i, o = i + ni, o + no
        return copies

    aliases, i, o = {}, 0, 0
    for e, ni, no in zip(exs, n_ins, n_outs):
        aliases.update({i + a: o + b for a, b in e.aliases.items()})
        i, o = i + ni, o + no
    return Exchange(sum((e.ins for e in exs), ()), sum((e.out_shapes for e in exs), ()), plan,
                    sum(e.n_copies for e in exs), aliases)


class Together(Hosts):
    def __init__(self, members):
        self.members, self.active = list(members), []

    def rider(self, host):
        self.active = [(m, r) for m, r in ((m, m.rider(host)) for m in self.members) if r is not None]
        if not self.active:
            return None
        return _merge_exchanges([r for _, r in self.active])

    def done(self, host, outs):
        for m, r in self.active:
            m.done(host, outs[:len(r.out_shapes)])
            outs = outs[len(r.out_shapes):]


def _hosted(hosts, host, call, ops, n_results):
    rider = hosts.rider(host) if hosts is not None else None
    if rider is None:
        return call(None)(*ops)
    res = call(rider)(*ops, *rider.ins)
    hosts.done(host, list(res[n_results:]))
    return res[:n_results]


def _nbytes(shape, dtype):
    n = 1
    for s in shape:
        n *= s
    return n * jnp.dtype(dtype).itemsize


def _tile(dim, target):
    if dim <= target:
        return dim
    t = target - target % LANES
    while t >= LANES:
        if dim % t == 0:
            return t
        t -= LANES
    return dim


def _params(block_bytes, scratch_bytes=0):
    est = 2 * block_bytes + scratch_bytes + 12 * 1024 * 1024
    return pltpu.CompilerParams(vmem_limit_bytes=int(min(max(est, 20 * 1024 * 1024), VMEM_BYTES - 4 * 1024 * 1024)))


_DIMS = {"nn": (((1,), (0,)), ((), ())), "nt": (((1,), (1,)), ((), ())), "tn": (((0,), (0,)), ((), ()))}


MM_TILE = 1024
MM_BLOCK_BYTES = 16 * 1024 * 1024


def _mm_tiles(mode, pairs, out_dtypes, m, n, tm, tn):
    fixed_m, fixed_n = tm is not None, tn is not None
    tm, tn = _tile(m, tm or MM_TILE), _tile(n, tn or MM_TILE)

    def block_bytes(tm_, tn_):
        total = sum(_nbytes((tm_, tn_), dt) for dt in out_dtypes)
        for a, b in pairs:
            k = a.shape[0] if mode == "tn" else a.shape[1]
            total += _nbytes((k, tm_), a.dtype) + _nbytes((k, tn_), b.dtype)
        return total

    while block_bytes(tm, tn) > MM_BLOCK_BYTES:
        if not fixed_m and tm >= tn and tm > 2 * LANES:
            tm = _tile(m, tm // 2)
        elif not fixed_n and tn > 2 * LANES:
            tn = _tile(n, tn // 2)
        elif not fixed_m and tm > 2 * LANES:
            tm = _tile(m, tm // 2)
        else:
            break
    return tm, tn


def _mm(name, mode, pairs, out_dtypes, tm=None, tn=None, epi=None, combine=None, row_extras=(), bc_extras=(),
        hosts=None, host=None):
    a0, b0 = pairs[0]
    m = a0.shape[1] if mode == "tn" else a0.shape[0]
    n = b0.shape[0] if mode == "nt" else b0.shape[1]
    tm, tn = _mm_tiles(mode, pairs, out_dtypes, m, n, tm, tn)
    in_specs, ops, blk = [], [], 0
    for a, b in pairs:
        if mode == "tn":
            k = a.shape[0]
            sa, sha = pl.BlockSpec((k, tm), lambda i, j: (0, i)), (k, tm)
        else:
            k = a.shape[1]
            sa, sha = pl.BlockSpec((tm, k), lambda i, j: (i, 0)), (tm, k)
        if mode == "nt":
            sb, shb = pl.BlockSpec((tn, k), lambda i, j: (j, 0)), (tn, k)
        else:
            sb, shb = pl.BlockSpec((k, tn), lambda i, j: (0, j)), (k, tn)
        in_specs += [sa, sb]
        ops += [a, b]
        blk += _nbytes(sha, a.dtype) + _nbytes(shb, b.dtype)
    for e in row_extras:
        w = e.shape[1]
        if w == n:
            in_specs.append(pl.BlockSpec((tm, tn), lambda i, j: (i, j)))
            blk += _nbytes((tm, tn), e.dtype)
        else:
            in_specs.append(pl.BlockSpec((tm, w), lambda i, j: (i, 0)))
            blk += _nbytes((tm, w), e.dtype)
        ops.append(e)
    for e in bc_extras:
        r, w = e.shape
        if w == n:
            in_specs.append(pl.BlockSpec((r, tn), lambda i, j: (0, j)))
        else:
            in_specs.append(pl.BlockSpec((r, w), lambda i, j: (0, 0)))
        blk += _nbytes((r, w), e.dtype)
        ops.append(e)
    npairs, nrow, nbc, nout = len(pairs), len(row_extras), len(bc_extras), len(out_dtypes)
    dims = _DIMS[mode]

    def body(*refs):
        prods = [lax.dot_general(refs[2 * p][...].astype(BF16), refs[2 * p + 1][...].astype(BF16), dims,
                                 preferred_element_type=F32) for p in range(npairs)]
        ex = [r[...] for r in refs[2 * npairs:2 * npairs + nrow + nbc]]
        if combine is not None:
            outs = combine(prods, *ex)
        else:
            acc = functools.reduce(lambda s, d: s + d, prods)
            outs = (acc,) if epi is None else epi(acc, *ex)
        for o_ref, o in zip(refs[2 * npairs + nrow + nbc:], outs):
            o_ref[...] = o.astype(o_ref.dtype)

    blk += sum(_nbytes((tm, tn), dt) for dt in out_dtypes) + 2 * _nbytes((tm, tn), F32)
    call = lambda rider: _pcall(
        body,
        rider=rider,
        name=name,
        grid=(m // tm, n // tn),
        in_specs=in_specs,
        out_specs=[pl.BlockSpec((tm, tn), lambda i, j: (i, j)) for _ in range(nout)],
        out_shape=[jax.ShapeDtypeStruct((m, n), dt) for dt in out_dtypes],
        compiler_params=_params(blk),
    )
    res = _hosted(hosts, host, call, ops, nout)
    return res[0] if nout == 1 else res


def _rowwise(name, fn, row_ins, bc_ins, outs, accs=(), tm=256):
    rows = row_ins[0].shape[0]
    tm = min(tm, rows)
    assert rows % tm == 0
    nrow, nbc, nout, nacc = len(row_ins), len(bc_ins), len(outs), len(accs)
    in_specs = [pl.BlockSpec((tm, a.shape[1]), lambda i: (i, 0)) for a in row_ins]
    in_specs += [pl.BlockSpec(a.shape, lambda i: (0, 0)) for a in bc_ins]
    out_specs = [pl.BlockSpec((tm, w), lambda i: (i, 0)) for w, _ in outs]
    out_specs += [pl.BlockSpec(s, lambda i: (0, 0)) for s in accs]
    out_shape = [jax.ShapeDtypeStruct((rows, w), dt) for w, dt in outs]
    out_shape += [jax.ShapeDtypeStruct(s, F32) for s in accs]

    def body(*refs):
        vals = fn(*[r[...] for r in refs[:nrow + nbc]])
        o_refs = refs[nrow + nbc:]
        for r, v in zip(o_refs[:nout], vals[:nout]):
            r[...] = v.astype(r.dtype)
        if nacc:
            @pl.when(pl.program_id(0) == 0)
            def _():
                for r in o_refs[nout:]:
                    r[...] = jnp.zeros(r.shape, F32)

            for r, v in zip(o_refs[nout:], vals[nout:]):
                r[...] += v

    blk = sum(_nbytes((tm, a.shape[1]), a.dtype) for a in row_ins) + sum(_nbytes(a.shape, a.dtype) for a in bc_ins)
    blk += sum(_nbytes((tm, w), dt) for w, dt in outs) + sum(_nbytes(s, F32) for s in accs)
    res = _pcall(
        body,
        name=name,
        grid=(rows // tm,),
        in_specs=in_specs,
        out_specs=out_specs,
        out_shape=out_shape,
        compiler_params=_params(2 * blk),
    )(*row_ins, *bc_ins)
    return res


def _ln(z, g, b):
    mu = jnp.mean(z, axis=-1, keepdims=True)
    zc = z - mu
    var = jnp.mean(zc * zc, axis=-1, keepdims=True)
    return zc * lax.rsqrt(var + LN_EPS) * g + b


def _rms(x, g):
    return x * lax.rsqrt(jnp.mean(x * x, axis=-1, keepdims=True) + RMS_EPS) * g


def _rope_swap(x):
    w = x.shape[1]
    lane = lax.broadcasted_iota(jnp.int32, (1, w), 1) % LANES
    from_left = pltpu.roll(x, 16, 1)
    from_right = pltpu.roll(x, w - 16, 1)
    lo = (lane >= MLA_NOPE) & (lane < MLA_NOPE + 16)
    hi = (lane >= MLA_NOPE + 16) & (lane < MLA_NOPE + 32)
    return jnp.where(hi, from_left, jnp.where(lo, from_right, 0.0))


def _rope(x, cos_t, sin_t):
    nh = x.shape[1] // LANES
    ct, st = jnp.tile(cos_t, (1, nh)), jnp.tile(sin_t, (1, nh))
    return x * ct + _rope_swap(x) * st


def _rope_t(dy, cos_t, sin_t):
    nh = dy.shape[1] // LANES
    ct, st = jnp.tile(cos_t, (1, nh)), jnp.tile(sin_t, (1, nh))
    return dy * ct + _rope_swap(dy * st)


def _block_cumsum(v, carry, reverse):
    tb = v.shape[0]
    r = lax.broadcasted_iota(jnp.int32, (tb, tb), 0)
    c = lax.broadcasted_iota(jnp.int32, (tb, tb), 1)
    tri = jnp.where((c >= r) if reverse else (c <= r), 1.0, 0.0).astype(BF16)
    hi = v.astype(BF16)
    r1 = v - hi.astype(F32)
    mid = r1.astype(BF16)
    lo = (r1 - mid.astype(F32)).astype(BF16)
    out = carry + sum(jnp.dot(tri, p, preferred_element_type=F32) for p in (hi, mid, lo))
    return out, (out[0:1, :] if reverse else out[tb - 1:tb, :])


def _forget_cumsum(name, logf, batch, seq, heads, tb):
    nb = seq // tb

    def body(x_ref, keys_ref, carry):
        @pl.when(pl.program_id(1) == 0)
        def _():
            carry[...] = jnp.zeros(carry.shape, F32)

        out, carry[...] = _block_cumsum(x_ref[...], carry[...], False)
        out = out * LOG2E
        keys_ref[...] = jnp.concatenate([jnp.broadcast_to(out[:, h:h + 1], (tb, LANES)) for h in range(heads)], axis=1)

    return _pcall(
        body,
        name=name,
        grid=(batch, nb),
        in_specs=[pl.BlockSpec((tb, LANES), lambda b, i: (b * nb + i, 0))],
        out_specs=pl.BlockSpec((tb, heads * LANES), lambda b, i: (b * nb + i, 0)),
        out_shape=jax.ShapeDtypeStruct((batch * seq, heads * LANES), F32),
        scratch_shapes=[pltpu.VMEM((1, LANES), F32)],
        compiler_params=_params(4 * tb * (heads + 2) * LANES * 4),
    )(logf)


def _forget_cumsum_bwd(name, dcf, batch, seq, heads, tb):
    nb = seq // tb

    def body(x_ref, o_ref, carry):
        @pl.when(pl.program_id(1) == 0)
        def _():
            carry[...] = jnp.zeros(carry.shape, F32)

        lane = lax.broadcasted_iota(jnp.int32, (1, LANES), 1)
        v = jnp.zeros((tb, LANES), F32)
        for g in range(heads // 2):
            blk = x_ref[:, g * LANES:(g + 1) * LANES]
            moved = pltpu.roll(blk, 2 * g, 1) if g else blk
            v = v + jnp.where((lane >= 2 * g) & (lane < 2 * g + 2), moved, 0.0)
        o_ref[...], carry[...] = _block_cumsum(v, carry[...], True)

    return _pcall(
        body,
        name=name,
        grid=(batch, nb),
        in_specs=[pl.BlockSpec((tb, (heads // 2) * LANES), lambda b, i: (b * nb + nb - 1 - i, 0))],
        out_specs=pl.BlockSpec((tb, LANES), lambda b, i: (b * nb + nb - 1 - i, 0)),
        out_shape=jax.ShapeDtypeStruct((batch * seq, LANES), F32),
        scratch_shapes=[pltpu.VMEM((1, LANES), F32)],
        compiler_params=_params(4 * tb * (heads // 2 + 1) * LANES * 4),
    )(dcf)


class Attn(NamedTuple):
    batch: int
    sq: int
    sk: int
    groups: int
    hq: int
    hv: int
    mode: str
    scale: float
    chunk: int
    tq: int
    tk: int

    @property
    def hg(self):
        return self.hv

    @property
    def qw(self):
        return LANES * self.hg // self.hq

    @property
    def dv(self):
        return LANES // self.hv


def _head_lanes(j, dv):
    lane = lax.broadcasted_iota(jnp.int32, (1, LANES), 1)
    return (lane >= j * dv) & (lane < (j + 1) * dv)


def _head_q(sp, j, q_blk):
    if sp.hq == 2:
        return jnp.where(_head_lanes(j, FOX_DH), q_blk, jnp.zeros_like(q_blk))
    return q_blk[:, LANES * j:LANES * (j + 1)]


def _head_rows(sp, j):
    return slice(j * sp.dv, (j + 1) * sp.dv) if sp.hg == 2 else slice(None)


def _scores_t(sp, j, k_c, q_j, cfk_rep, k0, q0, masked):
    tk, tq = k_c.shape[0], q_j.shape[0]
    k_j = k_c if sp.hq == 2 else k_c[:, LANES * j:LANES * (j + 1)]
    st = lax.dot_general(k_j, q_j, _DIMS["nt"], preferred_element_type=F32) * (sp.scale * LOG2E)
    if sp.mode == "fox":
        st = st - jnp.tile(cfk_rep[:, LANES * j:LANES * (j + 1)], (1, tq // LANES))
    if masked:
        kidx = k0 + lax.broadcasted_iota(jnp.int32, (tk, tq), 0)
        qidx = q0 + lax.broadcasted_iota(jnp.int32, (tk, tq), 1)
        if sp.mode == "chunk":
            shift = sp.chunk.bit_length() - 1
            kidx, qidx = jnp.right_shift(kidx, shift), jnp.right_shift(qidx, shift)
        st = jnp.where(kidx <= qidx, st, NEG_INF)
    return st


def _attn_fwd(name, sp, q, k, v, cfk=None, hosts=None, host=None):
    (qa, qo), (ka, ko), (va, vo) = q, k, v
    tq, tk, hg, qw = sp.tq, sp.tk, sp.hg, sp.qw
    nqb, nkc = sp.sq // tq, sp.sk // tk
    fox, causal = sp.mode == "fox", sp.mode != "none"
    assert sp.sq % tq == 0 and sp.sk % tk == 0 and (not causal or (tq == tk and sp.sq == sp.sk))

    def body(*refs):
        if fox:
            q_ref, k_ref, v_ref, cfk_ref, o_ref, lse_ref, acc_scr = refs
        else:
            q_ref, k_ref, v_ref, o_ref, lse_ref, acc_scr = refs
        i = pl.program_id(2)
        q0 = i * tq
        q_blk = q_ref[...]
        qs = [_head_q(sp, j, q_blk) for j in range(hg)]
        acc_scr[...] = jnp.zeros(acc_scr.shape, F32)

        def chunk(kc, carry, masked):
            ms, ls = carry
            k0 = pl.multiple_of(kc * tk, tk)
            k_c = k_ref[pl.ds(k0, tk), :]
            v_c = v_ref[pl.ds(k0, tk), :]
            new_m, new_l = [], []
            for j in range(hg):
                st = _scores_t(sp, j, k_c, qs[j], cfk_ref[pl.ds(k0, tk), :] if fox else None, k0, q0, masked)
                m_new = jnp.maximum(ms[j], jnp.max(st, axis=0, keepdims=True))
                alpha = jnp.exp2(ms[j] - m_new)
                pt = jnp.exp2(st - m_new)
                new_m.append(m_new)
                new_l.append(alpha * ls[j] + jnp.sum(pt, axis=0, keepdims=True))
                pv = lax.dot_general(v_c, pt.astype(BF16), _DIMS["tn"], preferred_element_type=F32)
                r = _head_rows(sp, j)
                acc_scr[r, :] = acc_scr[r, :] * alpha + pv[r, :]
            return tuple(new_m), tuple(new_l)

        carry = (tuple(jnp.full((1, tq), NEG_INF, F32) for _ in range(hg)),
                 tuple(jnp.zeros((1, tq), F32) for _ in range(hg)))
        if causal:
            carry = lax.fori_loop(0, i, functools.partial(chunk, masked=False), carry)
            ms, ls = chunk(i, carry, True)
        else:
            ms, ls = lax.fori_loop(0, nkc, functools.partial(chunk, masked=False), carry)
        for j in range(hg):
            r = _head_rows(sp, j)
            acc_scr[r, :] = acc_scr[r, :] / ls[j]
            lse_ref[j:j + 1, :] = ms[j] + jnp.log(ls[j]) * LOG2E
        o_ref[...] = acc_scr[...].T

    in_specs = [
        pl.BlockSpec((tq, qw), lambda b, g, i: (b * nqb + i, qo + g)),
        pl.BlockSpec((sp.sk, qw), lambda b, g, i: (b, ko + g)),
        pl.BlockSpec((sp.sk, LANES), lambda b, g, i: (b, vo + g)),
    ]
    ops = [qa, ka, va]
    stat_blk = pl.BlockSpec((None, None, None, hg, tq), lambda b, g, i: (b, g, i, 0, 0))
    if fox:
        in_specs.append(pl.BlockSpec((sp.sk, hg * LANES), lambda b, g, i: (b, g)))
        ops.append(cfk)
    blk = _nbytes((tq, qw), BF16) + _nbytes((sp.sk, qw + LANES), BF16) + 2 * _nbytes((tq, LANES), F32)
    blk += _nbytes((sp.sk, hg * LANES), F32) + 6 * _nbytes((tk, tq), F32)
    call = lambda rider: _pcall(
        body,
        rider=rider,
        name=name,
        grid=(sp.batch, sp.groups, nqb),
        in_specs=in_specs,
        out_specs=[pl.BlockSpec((tq, LANES), lambda b, g, i: (b * nqb + i, g)), stat_blk],
        out_shape=[
            jax.ShapeDtypeStruct((sp.batch * sp.sq, sp.groups * LANES), F32),
            jax.ShapeDtypeStruct((sp.batch, sp.groups, nqb, hg, tq), F32),
        ],
        scratch_shapes=[pltpu.VMEM((LANES, tq), F32)],
        compiler_params=_params(blk, tq * LANES * 4),
    )
    return _hosted(hosts, host, call, ops, 2)


def _attn_bwd(name, sp, q, k, v, o, lse, do, cfk=None, hosts=None, host=None):
    (qa, qo), (ka, ko), (va, vo) = q, k, v
    tq, tk, hg, qw, dv = sp.tq, sp.tk, sp.hg, sp.qw, sp.dv
    nqb, nkb = sp.sq // tq, sp.sk // tk
    fox, causal = sp.mode == "fox", sp.mode != "none"
    assert sp.sq % tq == 0 and sp.sk % tk == 0 and (not causal or (tq == tk and sp.sq == sp.sk))

    def body(*refs):
        if fox:
            (q_ref, k_ref, v_ref, lse_ref, do_ref, cfk_ref, kall_ref, vall_ref, cfkall_ref,
             dq_ref, dk_ref, dv_ref, dcf_ref, delta_scr, dk_scr, dv_scr, dqt_scr, dcf_scr) = refs
        else:
            (q_ref, k_ref, v_ref, o_ref, lse_ref, do_ref,
             dq_ref, dk_ref, dv_ref, delta_scr, dk_scr, dv_scr, dqt_scr) = refs
        kb = pl.program_id(2)
        k0 = kb * tk
        heads = [_head_lanes(j, dv) for j in range(hg)]

        def head_do(j, do_c):
            return jnp.where(heads[j], do_c, jnp.zeros_like(do_c)) if hg == 2 else do_c

        def probs_t(j, k_c, v_c, q_c, do_c, i, cf_keys, c0, masked):
            st = _scores_t(sp, j, k_c, _head_q(sp, j, q_c), cf_keys, c0, i * tq, masked)
            pt = jnp.exp2(st - lse_ref[i][j:j + 1, :])
            dpt = lax.dot_general(v_c, head_do(j, do_c), _DIMS["nt"], preferred_element_type=F32)
            return pt, dpt

        @pl.when(kb == 0)
        def _():
            dqt_scr[...] = jnp.zeros(dqt_scr.shape, F32)

            def fill(i, carry):
                r0 = pl.multiple_of(i * tq, tq)
                do_c = do_ref[pl.ds(r0, tq), :]
                if fox:
                    q_c = q_ref[pl.ds(r0, tq), :]

                    def keys(kc, acc, masked):
                        c0 = pl.multiple_of(kc * tk, tk)
                        out = []
                        for j in range(hg):
                            pt, dpt = probs_t(j, kall_ref[pl.ds(c0, tk), :], vall_ref[pl.ds(c0, tk), :], q_c, do_c, i,
                                              cfkall_ref[pl.ds(c0, tk), :], c0, masked)
                            out.append(acc[j] + jnp.sum(pt * dpt, axis=0, keepdims=True))
                        return tuple(out)

                    d = lax.fori_loop(0, i, functools.partial(keys, masked=False),
                                      tuple(jnp.zeros((1, tq), F32) for _ in range(hg)))
                    d = keys(i, d, True)
                    for j in range(hg):
                        delta_scr[i, j:j + 1, :] = d[j]
                else:
                    prod_t = (do_c.astype(F32) * o_ref[pl.ds(r0, tq), :]).T
                    for j in range(hg):
                        delta_scr[i, j:j + 1, :] = jnp.sum(prod_t[_head_rows(sp, j), :], axis=0, keepdims=True)
                return carry

            lax.fori_loop(0, nqb, fill, 0)

        k_blk = k_ref[...]
        v_blk = v_ref[...]
        k_t = k_blk.astype(F32).T.astype(BF16)
        dk_scr[...] = jnp.zeros(dk_scr.shape, F32)
        dv_scr[...] = jnp.zeros(dv_scr.shape, F32)
        if fox:
            dcf_scr[...] = jnp.zeros(dcf_scr.shape, F32)

        def qblock(i, carry, masked):
            r0 = pl.multiple_of(i * tq, tq)
            q_c = q_ref[pl.ds(r0, tq), :]
            do_c = do_ref[pl.ds(r0, tq), :]
            for j in range(hg):
                pt, dpt = probs_t(j, k_blk, v_blk, q_c, do_c, i, cfk_ref[...] if fox else None, k0, masked)
                dst = pt * (dpt - delta_scr[i][j:j + 1, :])
                if fox:
                    part = dst[:, :LANES]
                    for t in range(1, tq // LANES):
                        part = part + dst[:, t * LANES:(t + 1) * LANES]
                    dcf_scr[j] += part
                ds_b = (dst * sp.scale).astype(BF16)
                dv_scr[j] += jnp.dot(pt.astype(BF16), do_c, preferred_element_type=F32)
                dk_scr[j] += jnp.dot(ds_b, q_c if sp.hq == 2 else _head_q(sp, j, q_c), preferred_element_type=F32)
                if sp.hq == 2:
                    r = pl.ds(j * FOX_DH, FOX_DH)
                    dqt_scr[i, r, :] += jnp.dot(k_t[j * FOX_DH:(j + 1) * FOX_DH, :], ds_b, preferred_element_type=F32)
                else:
                    r = pl.ds(j * LANES, LANES)
                    dqt_scr[i, r, :] += jnp.dot(k_t[j * LANES:(j + 1) * LANES, :], ds_b, preferred_element_type=F32)
            return carry

        if causal:
            qblock(kb, 0, True)
            lax.fori_loop(kb + 1, nqb, functools.partial(qblock, masked=False), 0)
        else:
            lax.fori_loop(0, nqb, functools.partial(qblock, masked=False), 0)

        @pl.when(kb == nkb - 1)
        def _():
            def untranspose(i, carry):
                dq_ref[pl.ds(pl.multiple_of(i * tq, tq), tq), :] = dqt_scr[i].T
                return carry

            lax.fori_loop(0, nqb, untranspose, 0)

        if hg == 2:
            dv_ref[...] = jnp.where(heads[0], dv_scr[0], dv_scr[1]).astype(dv_ref.dtype)
        else:
            dv_ref[...] = dv_scr[0].astype(dv_ref.dtype)
        if sp.hq == 2:
            dk_ref[...] = jnp.where(_head_lanes(0, FOX_DH), dk_scr[0], dk_scr[1]).astype(dk_ref.dtype)
        elif hg == 2:
            dk_ref[...] = jnp.concatenate([dk_scr[0], dk_scr[1]], axis=1).astype(dk_ref.dtype)
        else:
            dk_ref[...] = dk_scr[0].astype(dk_ref.dtype)
        if fox:
            lane = lax.broadcasted_iota(jnp.int32, (1, LANES), 1)
            sums = [jnp.sum(dcf_scr[j], axis=1, keepdims=True) for j in range(hg)]
            dcf_ref[...] = jnp.where(lane == 0, -sums[0], jnp.where(lane == 1, -sums[1], 0.0))

    seq_lanes = lambda b, g, kb: (b, g)
    key_blk = lambda b, g, kb: (b * nkb + kb, g)
    stats = pl.BlockSpec((None, None, nqb, hg, tq), lambda b, g, kb: (b, g, 0, 0, 0))
    in_specs = [
        pl.BlockSpec((sp.sq, qw), lambda b, g, kb: (b, qo + g)),
        pl.BlockSpec((tk, qw), lambda b, g, kb: (b * nkb + kb, ko + g)),
        pl.BlockSpec((tk, LANES), lambda b, g, kb: (b * nkb + kb, vo + g)),
    ]
    ops = [qa, ka, va]
    if not fox:
        in_specs.append(pl.BlockSpec((sp.sq, LANES), seq_lanes))
        ops.append(o)
    in_specs += [stats, pl.BlockSpec((sp.sq, LANES), seq_lanes)]
    ops += [lse, do]
    out_specs = [pl.BlockSpec((sp.sq, qw), seq_lanes), pl.BlockSpec((tk, qw), key_blk), pl.BlockSpec((tk, LANES), key_blk)]
    out_shape = [
        jax.ShapeDtypeStruct((sp.batch * sp.sq, sp.groups * qw), F32),
        jax.ShapeDtypeStruct((sp.batch * sp.sk, sp.groups * qw), BF16),
        jax.ShapeDtypeStruct((sp.batch * sp.sk, sp.groups * LANES), BF16),
    ]
    scratch = [pltpu.VMEM((nqb, hg, tq), F32), pltpu.VMEM((hg, tk, LANES), F32), pltpu.VMEM((hg, tk, LANES), F32),
               pltpu.VMEM((nqb, qw, tq), F32)]
    if fox:
        in_specs += [
            pl.BlockSpec((tk, hg * LANES), key_blk),
            pl.BlockSpec((sp.sk, qw), lambda b, g, kb: (b, ko + g)),
            pl.BlockSpec((sp.sk, LANES), lambda b, g, kb: (b, vo + g)),
            pl.BlockSpec((sp.sk, hg * LANES), seq_lanes),
        ]
        ops += [cfk, ka, va, cfk]
        out_specs.append(pl.BlockSpec((tk, LANES), key_blk))
        out_shape.append(jax.ShapeDtypeStruct((sp.batch * sp.sk, sp.groups * LANES), F32))
        scratch.append(pltpu.VMEM((hg, tk, LANES), F32))
    blk = _nbytes((sp.sq, qw), BF16) + _nbytes((sp.sq, LANES), BF16) + 2 * _nbytes((sp.sq, LANES), F32)
    blk += _nbytes((sp.sq, qw), F32) + 4 * _nbytes((tk, qw), BF16) + 8 * _nbytes((tq, tk), F32)
    blk += (_nbytes((sp.sk, qw + LANES), BF16) + _nbytes((sp.sk, hg * LANES), F32)) if fox else 0
    call = lambda rider: _pcall(
        body,
        rider=rider,
        name=name,
        grid=(sp.batch, sp.groups, nkb),
        in_specs=in_specs,
        out_specs=out_specs,
        out_shape=out_shape,
        scratch_shapes=scratch,
        compiler_params=_params(blk, _nbytes((sp.sq, LANES), F32) + 4 * _nbytes((tk, LANES), F32)),
    )
    return _hosted(hosts, host, call, ops, len(out_shape))


def _slabwise(name, fn, ins, out_dtypes, rows_per_step=512):
    ins = [a if isinstance(a, tuple) else (a, None) for a in ins]
    n = max(1 if fixed is not None else a.shape[0] for a, fixed in ins)
    rows, cols = ins[0][0].shape[1:]
    tr = min(rows_per_step, rows)
    while rows % tr:
        tr //= 2
    assert tr % 16 == 0 or tr == rows, (name, rows, tr)

    def spec(a, fixed):
        if fixed is not None or a.shape[0] == 1:
            return pl.BlockSpec((None, tr, cols), lambda s, i: (fixed or 0, i, 0))
        return pl.BlockSpec((None, tr, cols), lambda s, i: (s, i, 0))

    def body(*refs):
        vals = fn(*[r[...] for r in refs[:len(ins)]])
        for r, v in zip(refs[len(ins):], vals):
            r[...] = v.astype(r.dtype)

    blk = (len(ins) + len(out_dtypes)) * _nbytes((tr, cols + LANES), F32)
    res = _pcall(
        body,
        name=name,
        grid=(n, rows // tr),
        in_specs=[spec(a, fixed) for a, fixed in ins],
        out_specs=[pl.BlockSpec((None, tr, cols), lambda s, i: (s, i, 0)) for _ in out_dtypes],
        out_shape=[jax.ShapeDtypeStruct((n, rows, cols), dt) for dt in out_dtypes],
        compiler_params=_params(2 * blk),
    )(*[a for a, _ in ins])
    return res


def _adamw_math(w, g, m, v):
    m = ADAM_B1 * m + (1.0 - ADAM_B1) * g
    v = ADAM_B2 * v + (1.0 - ADAM_B2) * jnp.square(g)
    m_hat = m / (1.0 - ADAM_B1 ** ADAM_STEP)
    v_hat = v / (1.0 - ADAM_B2 ** ADAM_STEP)
    delta = -ADAM_LR * (m_hat / (jnp.sqrt(v_hat) + ADAM_EPS) + ADAM_WD * w)
    return delta, m, v


def _run_exchange(name, ex):
    n_in, n_out = len(ex.ins), len(ex.out_shapes)

    def body(*refs):
        copies = _exchange_copies(ex, refs[:n_in], refs[n_in:n_in + n_out], refs[-2], refs[-1])
        for cp in copies:
            cp.start()
        for cp in copies:
            cp.wait()

    any_spec = pl.BlockSpec(memory_space=pl.ANY)
    return _pcall(
        body,
        pin_all=True,
        name=name,
        in_specs=[any_spec] * n_in,
        out_specs=[any_spec] * n_out,
        out_shape=list(ex.out_shapes),
        scratch_shapes=[pltpu.SemaphoreType.DMA((ex.n_copies,))] * 2,
        input_output_aliases=dict(ex.aliases),
    )(*ex.ins)


def _chip(rel=0):
    x, y = lax.axis_index("x"), lax.axis_index("y")
    return 2 * ((1 - x) if rel & 1 else x) + ((1 - y) if rel & 2 else y)


def _gather_ici(shards):
    def plan(in_refs, out_refs):
        c = lax.axis_index("c")
        return [(s.at[c], g.at[_chip(), c], rel) for s, g in zip(in_refs, out_refs) for rel in (1, 2, 3)]

    shapes = tuple(jax.ShapeDtypeStruct((N_CHIPS,) + s.shape, s.dtype) for s in shards)
    return Exchange(tuple(shards), shapes, plan, 3 * len(shards))


def _gather_d2d(got):
    def plan(in_refs, out_refs):
        c = lax.axis_index("c")
        return [(g_in.at[_chip(rel), c], g_out.at[_chip(rel), c], "c")
                for g_in, g_out in zip(in_refs, out_refs) for rel in (1, 2, 3)]

    shapes = tuple(jax.ShapeDtypeStruct(g.shape, g.dtype) for g in got)
    return Exchange(tuple(got), shapes, plan, 3 * len(got), {i: i for i in range(len(got))})


class GatherSet(Hosts):
    def __init__(self, shards, hosts):
        self.shards, self.hosts, self.got = shards, hosts, None

    def rider(self, host):
        if host == self.hosts[0]:
            return _gather_ici(self.shards)
        if host == self.hosts[1]:
            return _gather_d2d(self.got)
        return None

    def done(self, host, outs):
        self.got = outs


def _gather_now(shards):
    got = _run_exchange("gather_weights_ici", _gather_ici(shards))
    return _run_exchange("gather_weights_d2d", _gather_d2d(got))


def _pair_sum(name, g, recv, rows_per_step=512):
    _, _, rows, cols = g.shape
    tr = min(rows_per_step, rows)
    while rows % tr:
        tr //= 2

    def body(g_ref, r_ref, p_ref, own_ref):
        mine = jnp.where(lax.axis_index("c") == 0, g_ref[0], g_ref[1])
        p = mine.astype(F32) + r_ref[...].astype(F32)
        p_ref[...] = p.astype(p_ref.dtype)

        @pl.when(pl.program_id(1) == _chip())
        def _():
            own_ref[...] = p

    slab = pl.BlockSpec((None, tr, cols), lambda i, s: (s, i, 0))
    return _pcall(
        body,
        name=name,
        grid=(rows // tr, N_CHIPS),
        in_specs=[pl.BlockSpec((None, 2, tr, cols), lambda i, s: (s, 0, i, 0)), slab],
        out_specs=[slab, pl.BlockSpec((None, tr, cols), lambda i, s: (0, i, 0))],
        out_shape=[jax.ShapeDtypeStruct((N_CHIPS, rows, cols), BF16), jax.ShapeDtypeStruct((1, rows, cols), F32)],
        compiler_params=_params(2 * 6 * _nbytes((tr, cols + LANES), F32)),
    )(g, recv)


class ReduceLayer(Hosts):
    def __init__(self, tag, grads, hosts):
        self.tag, self.grads, self.pair, self.own, self.total, self.theirs = tag, grads, None, None, None, None
        self.stage_of = dict(zip(hosts, ("swap", "chips", "share")))

    def _swap_halves(self):
        def plan(in_refs, out_refs):
            c = lax.axis_index("c")
            return [(g.at[pl.ds(0, N_CHIPS), 1 - c], r, "c") for g, r in zip(in_refs, out_refs)]

        shapes = tuple(jax.ShapeDtypeStruct((N_CHIPS,) + g.shape[2:], g.dtype) for g in self.grads)
        return Exchange(tuple(self.grads), shapes, plan, len(self.grads))

    def _to_chips(self):
        def plan(in_refs, out_refs):
            return [(p.at[_chip(rel)], r.at[rel - 1], rel) for p, r in zip(in_refs, out_refs) for rel in (1, 2, 3)]

        shapes = tuple(jax.ShapeDtypeStruct((3,) + p.shape[1:], p.dtype) for p in self.pair)
        return Exchange(tuple(self.pair), shapes, plan, 3 * len(self.pair))

    def _share(self):
        def plan(in_refs, out_refs):
            return [(t, r, "c") for t, r in zip(in_refs, out_refs)]

        shapes = tuple(jax.ShapeDtypeStruct(t.shape, F32) for t in self.total)
        return Exchange(tuple(self.total), shapes, plan, len(self.total))

    def rider(self, host):
        stages = {"swap": self._swap_halves, "chips": self._to_chips, "share": self._share}
        return stages[self.stage_of[host]]() if host in self.stage_of else None

    def done(self, host, outs):
        self._after(self.stage_of[host], outs)

    def _after(self, stage, outs):
        if stage == "swap":
            sums = [_pair_sum(f"reduce_pair_sum_{self.tag}_{i}", g, r) for i, (g, r) in enumerate(zip(self.grads, outs))]
            self.pair, self.own = [s[0] for s in sums], [s[1] for s in sums]
        elif stage == "chips":
            self.total = [
                _slabwise(f"reduce_chip_sum_{self.tag}_{i}",
                          lambda a, b, c_, d: (a + b.astype(F32) + c_.astype(F32) + d.astype(F32),),
                          [own, (r, 0), (r, 1), (r, 2)], [F32])[0]
                for i, (own, r) in enumerate(zip(self.own, outs))]
        else:
            self.theirs = outs

    def run_now(self):
        self._after("swap", _run_exchange(f"reduce_pair_{self.tag}", self._swap_halves()))
        self._after("chips", _run_exchange(f"reduce_chips_{self.tag}", self._to_chips()))
        self._after("share", _run_exchange(f"reduce_share_{self.tag}", self._share()))

    def result(self):
        return list(zip(self.total, self.theirs))


def _adamw_layer(name, l, w, m, v, mine, theirs, prev, rows_per_step=256):
    _, rows2, cols = w.shape
    rows = rows2 // 2
    tr = min(rows_per_step, rows)
    while rows % tr:
        tr //= 2
    steps = rows // tr

    def body(w_ref, m_ref, v_ref, mine_ref, theirs_ref, *rest):
        g_ref, d_ref, nm_ref, nv_ref = rest[-4:]
        g = jnp.where(pl.program_id(0) == lax.axis_index("c"), mine_ref[...], theirs_ref[...])
        d, nm, nv = _adamw_math(w_ref[...], g, m_ref[...], v_ref[...])
        g_ref[...], d_ref[...], nm_ref[...], nv_ref[...] = g, d, nm, nv

    half = pl.BlockSpec((None, tr, cols), lambda h, i: (l, h * steps + i, 0))
    one = pl.BlockSpec((None, tr, cols), lambda h, i: (0, i, 0))
    kept = [] if prev is None else list(prev)
    return _pcall(
        body,
        name=name,
        grid=(2, rows // tr),
        in_specs=[half, half, half, one, one] + [pl.BlockSpec(memory_space=pl.ANY)] * len(kept),
        out_specs=[half] * 4,
        out_shape=[jax.ShapeDtypeStruct(w.shape, F32)] * 4,
        input_output_aliases={5 + i: i for i in range(len(kept))},
        compiler_params=_params(2 * 9 * _nbytes((tr, cols + LANES), F32)),
    )(w, m, v, mine, theirs, *kept)


def _allreduce_small(v):
    rows = v.shape[0]

    def body(v_ref, sum_ref, all_ref, send_sems, recv_sems, local_sem):
        x, y, c = lax.axis_index("x"), lax.axis_index("y"), lax.axis_index("c")
        sibling = (x, y, 1 - c)
        chips = [(1 - x, y), (x, 1 - y), (1 - x, 1 - y)]

        def slab(px, py, pc):
            return all_ref.at[pl.ds((4 * px + 2 * py + pc) * rows, rows), :]

        def copy(k, block, to, src=None):
            return pltpu.make_async_remote_copy(
                src_ref=slab(*block) if src is None else src, dst_ref=slab(*block), send_sem=send_sems.at[k],
                recv_sem=recv_sems.at[k], device_id=to, device_id_type=MESH)

        mine = pltpu.make_async_copy(v_ref, slab(x, y, c), local_sem)
        mine.start()
        first = [copy(0, (x, y, c), sibling, src=v_ref)]
        first += [copy(1 + j, (x, y, c), (*chip, c), src=v_ref) for j, chip in enumerate(chips)]
        for cp in first:
            cp.start()
        passed = [copy(4 + j, (*chip, c), sibling) for j, chip in enumerate(chips)]
        for j, chip in enumerate(chips):
            copy(1 + j, (*chip, c), (x, y, c)).wait_recv()
            passed[j].start()
        copy(0, (x, y, 1 - c), (x, y, c)).wait_recv()
        for j, chip in enumerate(chips):
            copy(4 + j, (*chip, 1 - c), (x, y, c)).wait_recv()
        for cp in first + passed:
            cp.wait_send()
        mine.wait()
        total = all_ref[pl.ds(0, rows), :]
        for d in range(1, N_DEV):
            total = total + all_ref[pl.ds(d * rows, rows), :]
        sum_ref[...] = total

    vm = pl.BlockSpec(memory_space=pltpu.VMEM)
    return _pcall(
        body,
        name="allreduce_small",
        in_specs=[vm],
        out_specs=vm,
        out_shape=jax.ShapeDtypeStruct((rows, LANES), F32),
        scratch_shapes=[pltpu.VMEM((N_DEV * rows, LANES), F32), pltpu.SemaphoreType.DMA((7,)),
                        pltpu.SemaphoreType.DMA((7,)), pltpu.SemaphoreType.DMA],
    )(v)


def _pad_cols(a, before, total):
    return jnp.pad(a, ((0, 0), (before, total - before - a.shape[1])))


def _layer_weights(cfg, w_in, w_uq, w_ukv):
    w = cfg.width
    qkv, f, cq, ckv, kr, qm, gates = jnp.split(w_in, list(_cumsum(cfg.in_splits))[:-1], axis=1)
    wa = jnp.concatenate([qkv, qm], axis=1)
    ws = jnp.concatenate([_pad_cols(f, 0, LANES), cq, ckv, _pad_cols(kr, MLA_NOPE, LANES)], axis=1)
    wq = jnp.pad(w_uq.reshape(cfg.q_rank, cfg.mla_h, MLA_NOPE + MLA_ROPE), ((0, 0), (0, 0), (0, LANES - MLA_NOPE - MLA_ROPE)))
    wq = wq.reshape(cfg.q_rank, cfg.mla_h * LANES)
    kv = w_ukv.reshape(cfg.kv_rank, cfg.mla_h, MLA_NOPE + MLA_V)
    wk = jnp.pad(kv[:, :, :MLA_NOPE], ((0, 0), (0, 0), (0, LANES - MLA_NOPE))).reshape(cfg.kv_rank, cfg.mla_h * LANES)
    wv = kv[:, :, MLA_NOPE:].reshape(cfg.kv_rank, cfg.mla_h * MLA_V)
    del w
    return wa, gates, ws, wq, wk, wv


def _cumsum(xs):
    out, t = [], 0
    for v in xs:
        t += v
        out.append(t)
    return out


def _layer_weight_grads(cfg, dwa, dwg, dws, dwq, dwk, dwv):
    w, qr, kvr = cfg.width, cfg.q_rank, cfg.kv_rank
    off_kr = LANES + qr + kvr + MLA_NOPE
    dw_in = jnp.concatenate([
        dwa[:, :3 * w], dws[:, :cfg.fox_h], dws[:, LANES:LANES + qr], dws[:, LANES + qr:LANES + qr + kvr],
        dws[:, off_kr:off_kr + MLA_ROPE], dwa[:, 3 * w:], *dwg], axis=1)
    dw_uq = dwq.reshape(qr, cfg.mla_h, LANES)[:, :, :MLA_NOPE + MLA_ROPE].reshape(qr, cfg.mla_h * (MLA_NOPE + MLA_ROPE))
    dw_ukv = jnp.concatenate([dwk.reshape(kvr, cfg.mla_h, LANES)[:, :, :MLA_NOPE], dwv.reshape(kvr, cfg.mla_h, MLA_V)],
                             axis=2).reshape(kvr, cfg.mla_h * (MLA_NOPE + MLA_V))
    return dw_in, dw_uq, dw_ukv


def _attn_specs(cfg, batch):
    t = min(ATTN_TILE, cfg.seq)
    common = dict(batch=batch, sq=cfg.seq, chunk=cfg.chunk, tq=t)
    fox = Attn(sk=cfg.seq, groups=cfg.fox_h // 2, hq=2, hv=2, mode="fox", scale=FOX_DH ** -0.5, tk=t, **common)
    mla = Attn(sk=cfg.seq, groups=cfg.mla_h // 2, hq=1, hv=2, mode="chunk",
               scale=(MLA_NOPE + MLA_ROPE) ** -0.5, tk=t, **common)
    mem = Attn(sk=cfg.n_mem, groups=cfg.mem_h, hq=1, hv=1, mode="none", scale=MEM_DH ** -0.5, tk=cfg.n_mem,
               **dict(common, tq=min(MEM_ATTN_TILE, cfg.seq)))
    return fox, mla, mem


def _small_core(cfg, ps, bf, gq, gkv):
    qr, kvr = cfg.q_rank, cfg.kv_rank
    z = ps[:, :LANES] + bf
    logf = jnp.minimum(z, 0.0) - jnp.log1p(jnp.exp(-jnp.abs(z)))
    nq = _rms(ps[:, LANES:LANES + qr], gq)
    nkv = _rms(ps[:, LANES + qr:LANES + qr + kvr], gkv)
    return logf, nq, nkv


def _layer_fwd(cfg, l, batch, h, hb, mem_b, rope_c, rope_s, hosts, lw, bf_pad, g_cq, g_ckv, ln1, ln2):
    (wa, wg, ws, wq, wk, wv, wmkv), later_weights = lw
    w, d = cfg.width, cfg.d
    fox, mla, mem = _attn_specs(cfg, batch)
    nw = w // LANES
    pa = _mm(f"proj_a_{l}", "nn", [(hb, wa)], [BF16])
    gl = _mm(f"proj_gates_{l}", "nn", [(hb, wg)], [F32])
    ps = _mm(f"proj_small_{l}", "nn", [(hb, ws)], [F32], tn=cfg.small_w)

    def small_fwd(ps_, c_, s_, bf_, gq_, gkv_):
        logf, nq, nkv = _small_core(cfg, ps_, bf_, gq_, gkv_)
        kpe = _rope(ps_[:, cfg.small_w - LANES:], c_, s_)
        return logf, nq, nkv, kpe

    logf, nq, nkv, kpe = _rowwise(
        f"small_fwd_{l}", small_fwd, [ps, rope_c, rope_s], [bf_pad, g_cq, g_ckv],
        [(LANES, F32), (cfg.q_rank, BF16), (cfg.kv_rank, BF16), (LANES, F32)])
    cfk = _forget_cumsum(f"cum_forget_{l}", logf, batch, cfg.seq, cfg.fox_h, fox.tq)

    qf = _mm(f"mla_q_{l}", "nn", [(nq, wq)], [BF16], tn=wq.shape[1],
             epi=lambda acc, c_, s_: (_rope(acc, c_, s_),), row_extras=[rope_c, rope_s])
    kf = _mm(f"mla_k_{l}", "nn", [(nkv, wk)], [BF16], tn=wk.shape[1],
             epi=lambda acc, kp: (acc + jnp.tile(kp, (1, cfg.mla_h)),), row_extras=[kpe])
    vb = _mm(f"mla_v_{l}", "nn", [(nkv, wv)], [BF16])
    mkv = _mm(f"mem_kv_{l}", "nn", [(mem_b, wmkv)], [BF16])

    o_a, lse_a = _attn_fwd(f"fox_fwd_{l}", fox, (pa, 0), (pa, nw), (pa, 2 * nw), cfk,
                           hosts=hosts, host="fox_fwd")
    o_b, lse_b = _attn_fwd(f"mla_fwd_{l}", mla, (qf, 0), (kf, 0), (vb, 0), hosts=hosts, host="mla_fwd")
    o_c, lse_c = _attn_fwd(f"mem_fwd_{l}", mem, (pa, 3 * nw), (mkv, 0), (mkv, nw), hosts=hosts, host="mem_fwd")
    wbr, wout, wff1, wff2 = later_weights()

    def gated_sum(branches, gl_):
        g = jax.nn.sigmoid(gl_)
        return (sum(g[:, n * d:(n + 1) * d] * bp for n, bp in enumerate(branches)),)

    merged = _mm(f"merge_{l}", "nn", [(o, wbr[n]) for n, o in enumerate((o_a, o_b, o_c))], [BF16], tm=256, tn=d,
                 combine=gated_sum, row_extras=[gl])

    def post_ln(acc, res, g_, b_):
        z = cfg.alpha * res + acc
        y = _ln(z, g_, b_)
        return z, y, y

    z1, h1, h1b = _mm(f"out_ln1_{l}", "nn", [(merged, wout)], [F32, F32, BF16], tm=256, tn=d,
                      epi=post_ln, row_extras=[h], bc_extras=list(ln1))
    u, a = _mm(f"ff1_{l}", "nn", [(h1b, wff1)], [BF16, BF16],
               epi=lambda acc: (acc, jnp.square(jnp.maximum(acc, 0.0))))
    z2, h2, h2b = _mm(f"ff2_ln2_{l}", "nn", [(a, wff2)], [F32, F32, BF16], tm=256, tn=d,
                      epi=post_ln, row_extras=[h1], bc_extras=list(ln2))
    saved = dict(hb=hb, pa=pa, gl=gl, ps=ps, nq=nq, nkv=nkv, cfk=cfk, qf=qf, kf=kf, vb=vb, mkv=mkv,
                 o=(o_a, o_b, o_c), lse=(lse_a, lse_b, lse_c), merged=merged, z1=z1, h1b=h1b, u=u, a=a, z2=z2,
                 lw=(wa, wg, ws, wq, wk, wv, wmkv, wbr, wout, wff1, wff2))
    return h2, h2b, saved


def _ln_bwd(name, cfg, ga, gb, z, g, b):
    d = cfg.d

    def fn(*vals):
        if gb is None:
            ga_, z_, g_, b_ = vals
            dy = ga_
        else:
            ga_, gb_, z_, g_, b_ = vals
            dy = ga_ + cfg.alpha * gb_
        _, vjp = jax.vjp(_ln, z_, g_, b_)
        dz, dg, db = vjp(dy)
        return dz, dz, dg, db

    rows = [ga, z] if gb is None else [ga, gb, z]
    return _rowwise(name, fn, rows, [g, b], [(d, F32), (d, BF16)], accs=[(1, d), (1, d)])


def _layer_bwd(cfg, l, batch, ga, gb, sv, mem_b, rope_c, rope_s, hosts, lw, bf_pad, g_cq, g_ckv, ln1, ln2):
    wa, wg, ws, wq, wk, wv, wmkv, wbr, wout, wff1, wff2 = lw
    w, d = cfg.width, cfg.d
    fox, mla, mem = _attn_specs(cfg, batch)
    nw = w // LANES
    gdt = BF16

    dz2, dz2b, dg2, db2 = _ln_bwd(f"ln2_bwd_{l}", cfg, ga, gb, sv["z2"], *ln2)
    du = _mm(f"ff2_bwd_x_{l}", "nt", [(dz2b, wff2)], [BF16],
             epi=lambda acc, u_: (acc * (2.0 * jnp.maximum(u_.astype(F32), 0.0)),), row_extras=[sv["u"]],
             hosts=hosts, host="ff2_bwd_x")
    dwff2 = _mm(f"ff2_bwd_w_{l}", "tn", [(sv["a"], dz2b)], [gdt])
    dwff1 = _mm(f"ff1_bwd_w_{l}", "tn", [(sv["h1b"], du)], [gdt])
    dh1 = _mm(f"ff1_bwd_x_{l}", "nt", [(du, wff1)], [F32])
    dz1, dz1b, dg1, db1 = _ln_bwd(f"ln1_bwd_{l}", cfg, dh1, dz2, sv["z1"], *ln1)
    dmerged = _mm(f"out_bwd_x_{l}", "nt", [(dz1b, wout)], [F32])
    dwout = _mm(f"out_bwd_w_{l}", "tn", [(sv["merged"], dz1b)], [gdt])

    def gated_sum_bwd(branches, dm, gl_):
        g = jax.nn.sigmoid(gl_)
        gates = [g[:, n * d:(n + 1) * d] for n in range(3)]
        d_logits = [dm * bp * gn * (1.0 - gn) for bp, gn in zip(branches, gates)]
        return tuple(d_logits) + tuple(dm * gn for gn in gates)

    res = _mm(f"merge_bwd_{l}", "nn", [(sv["o"][n], wbr[n]) for n in range(3)], [BF16] * 6, tm=256, tn=d,
              combine=gated_sum_bwd, row_extras=[dmerged, sv["gl"]])
    dgls, dbps = res[:3], res[3:]
    dos = [_mm(f"branch_bwd_x_{n}_{l}", "nt", [(dbps[n], wbr[n])], [BF16]) for n in range(3)]
    dwbr = jnp.stack([_mm(f"branch_bwd_w_{n}_{l}", "tn", [(sv["o"][n], dbps[n])], [gdt]) for n in range(3)])
    if hosts is not None:
        hosts.early_grads(dict(w_br=dwbr, w_out=dwout, w_ff1=dwff1, w_ff2=dwff2))

    pa = sv["pa"]
    dq_a, dk_a, dv_a, dcfk = _attn_bwd(f"fox_bwd_{l}", fox, (pa, 0), (pa, nw), (pa, 2 * nw), sv["o"][0], sv["lse"][0],
                                       dos[0], sv["cfk"], hosts=hosts, host="fox_bwd")
    dqf, dkf, dvb = _attn_bwd(f"mla_bwd_{l}", mla, (sv["qf"], 0), (sv["kf"], 0), (sv["vb"], 0), sv["o"][1],
                              sv["lse"][1], dos[1], hosts=hosts, host="mla_bwd")
    dqm, dmk, dmv = _attn_bwd(f"mem_bwd_{l}", mem, (pa, 3 * nw), (sv["mkv"], 0), (sv["mkv"], nw), sv["o"][2],
                              sv["lse"][2], dos[2], hosts=hosts, host="mem_bwd")
    dwmkv = _mm(f"mem_kv_bwd_w_{l}", "tn", [(mem_b, jnp.concatenate([dmk, dmv], axis=1))], [gdt])

    (dq_raw,) = _rowwise(f"mla_q_rope_bwd_{l}", lambda dy, c_, s_: (_rope_t(dy, c_, s_),), [dqf, rope_c, rope_s], [],
                         [(wq.shape[1], BF16)])
    dwq = _mm(f"mla_q_bwd_w_{l}", "tn", [(sv["nq"], dq_raw)], [gdt])
    dnq = _mm(f"mla_q_bwd_x_{l}", "nt", [(dq_raw, wq)], [F32])
    dwk = _mm(f"mla_k_bwd_w_{l}", "tn", [(sv["nkv"], dkf)], [gdt])
    dwv = _mm(f"mla_v_bwd_w_{l}", "tn", [(sv["nkv"], dvb)], [gdt])
    dnkv = _mm(f"mla_kv_bwd_x_{l}", "nt", [(dkf, wk), (dvb, wv)], [F32])

    dlogf = _forget_cumsum_bwd(f"cum_forget_bwd_{l}", dcfk, batch, cfg.seq, cfg.fox_h, fox.tq)

    def small_bwd(ps_, dlogf_, dnq_, dnkv_, dkf_, c_, s_, bf_, gq_, gkv_):
        _, vjp = jax.vjp(functools.partial(_small_core, cfg), ps_, bf_, gq_, gkv_)
        dps, dbf, dgq, dgkv = vjp((dlogf_, dnq_, dnkv_))
        dkpe = dkf_[:, :LANES].astype(F32)
        for hh in range(1, cfg.mla_h):
            dkpe = dkpe + dkf_[:, hh * LANES:(hh + 1) * LANES].astype(F32)
        lane = lax.broadcasted_iota(jnp.int32, (1, LANES), 1)
        dkpe = jnp.where((lane >= MLA_NOPE) & (lane < MLA_NOPE + MLA_ROPE), dkpe, 0.0)
        dkr = _rope_t(dkpe, c_, s_)
        dps = jnp.concatenate([dps[:, :cfg.small_w - LANES], dkr], axis=1)
        return dps, dbf, dgq, dgkv

    dps, dbf, dgq, dgkv = _rowwise(
        f"small_bwd_{l}", small_bwd, [sv["ps"], dlogf, dnq, dnkv, dkf, rope_c, rope_s], [bf_pad, g_cq, g_ckv],
        [(cfg.small_w, BF16)], accs=[(1, LANES), (1, cfg.q_rank), (1, cfg.kv_rank)])

    dpa = jnp.concatenate([dq_a.astype(BF16), dk_a, dv_a, dqm.astype(BF16)], axis=1)
    hb = sv["hb"]
    gate_pairs = [(dgls[n], wg[:, n * d:(n + 1) * d]) for n in range(3)]
    dh = _mm(f"proj_bwd_x_{l}", "nt", [(dpa, wa)] + gate_pairs + [(dps, ws)], [F32], tn=d)
    dwa = _mm(f"proj_a_bwd_w_{l}", "tn", [(hb, dpa)], [gdt])
    dwg = [_mm(f"proj_gates_bwd_w_{n}_{l}", "tn", [(hb, dgls[n])], [gdt]) for n in range(3)]
    dws = _mm(f"proj_small_bwd_w_{l}", "tn", [(hb, dps)], [gdt], tn=cfg.small_w)
    dw_in, dw_uq, dw_ukv = _layer_weight_grads(cfg, dwa, dwg, dws, dwq, dwk, dwv)
    big = dict(w_in=dw_in, w_uq=dw_uq, w_ukv=dw_ukv, w_mem_kv=dwmkv, w_br=dwbr, w_out=dwout,
               w_ff1=dwff1, w_ff2=dwff2)
    small = dict(b_forget=dbf[0, :cfg.fox_h], g_cq=dgq[0], g_ckv=dgkv[0], ln1_g=dg1[0], ln1_b=db1[0],
                 ln2_g=dg2[0], ln2_b=db2[0])
    return dh, dz1, big, small


def _rope_tables(positions):
    inv_freq = ROPE_BASE ** (-jnp.arange(0, MLA_ROPE, 2, dtype=F32) / MLA_ROPE)
    ang = positions.astype(F32).reshape(-1)[:, None] * inv_freq
    cos, sin = jnp.cos(ang), jnp.sin(ang)
    t = ang.shape[0]
    rope_c = jnp.concatenate([jnp.ones((t, MLA_NOPE), F32), cos, cos, jnp.zeros((t, LANES - MLA_NOPE - MLA_ROPE), F32)], axis=1)
    rope_s = jnp.concatenate([jnp.zeros((t, MLA_NOPE), F32), -sin, sin, jnp.zeros((t, LANES - MLA_NOPE - MLA_ROPE), F32)], axis=1)
    return rope_c, rope_s


def _local_step(cfg, x, mem, positions, target, small_w, comm):
    batch = x.shape[0]
    d, depth = cfg.d, cfg.depth
    t = batch * cfg.seq
    x2, tgt = x.reshape(t, d), target.reshape(t, d)
    mem_b = mem.reshape(batch * cfg.n_mem, d).astype(BF16)
    rope_c, rope_s = _rope_tables(positions)
    row = lambda v: v.reshape(1, -1)
    ln_in = (row(small_w["ln_in_g"]), row(small_w["ln_in_b"]))

    h, hb = _rowwise("ln_in", lambda x_, g_, b_: (_ln(x_, g_, b_),) * 2, [x2], list(ln_in), [(d, F32), (d, BF16)])
    layers, saves = [], []
    for l in range(depth):
        first = comm.weights(l, LATE)
        lw = _layer_weights(cfg, first["w_in"], first["w_uq"], first["w_ukv"]) + (first["w_mem_kv"],)
        later = lambda l=l: tuple(comm.weights(l, EARLY).values())
        par = dict(
            bf_pad=jnp.pad(row(small_w["b_forget"][l]), ((0, 0), (0, LANES - cfg.fox_h))),
            g_cq=row(small_w["g_cq"][l]), g_ckv=row(small_w["g_ckv"][l]),
            ln1=(row(small_w["ln1_g"][l]), row(small_w["ln1_b"][l])),
            ln2=(row(small_w["ln2_g"][l]), row(small_w["ln2_b"][l])))
        h, hb, sv = _layer_fwd(cfg, l, batch, h, hb, mem_b, rope_c, rope_s, comm.forward_hosts(l), (lw, later), **par)
        layers.append(dict(par, lw=sv.pop("lw")))
        saves.append(sv)

    def loss_fn(y, tg):
        err = y - tg
        part = 0.5 * jnp.sum(jnp.mean(err * err, axis=-1, keepdims=True), axis=0, keepdims=True)
        return err * (1.0 / d), jnp.broadcast_to(part, (1, LANES))

    ga, loss_acc = _rowwise("loss", loss_fn, [h, tgt], [], [(d, F32)], accs=[(1, LANES)])
    gb = None
    small_g = {k: [None] * depth for k in ("b_forget", "g_cq", "g_ckv", "ln1_g", "ln1_b", "ln2_g", "ln2_b")}
    for l in reversed(range(depth)):
        ga, gb, big, small = _layer_bwd(cfg, l, batch, ga, gb, saves[l], mem_b, rope_c, rope_s,
                                        comm.backward_hosts(l), **layers[l])
        comm.grads(l, big)
        for k, v in small.items():
            small_g[k][l] = v
    dx, _, dg_in, db_in = _ln_bwd("ln_in_bwd", cfg, ga, gb, x2, *ln_in)
    small_g = {k: jnp.stack(v) for k, v in small_g.items()}
    small_g["ln_in_g"], small_g["ln_in_b"] = dg_in[0], db_in[0]
    return loss_acc[0, 0], dx.reshape(x.shape), small_g


BIG = ("w_in", "w_uq", "w_ukv", "w_mem_kv", "w_br", "w_out", "w_ff1", "w_ff2")
SMALL = ("ln_in_g", "ln_in_b", "b_forget", "g_cq", "g_ckv", "ln1_g", "ln1_b", "ln2_g", "ln2_b")
ROW_CUT = ("w_mem_kv", "w_out", "w_ff2")
LATE, EARLY = BIG[:4], BIG[4:]


def _shard_2d(a):
    cols = a.shape[-1]
    rows = a.size // cols
    return a.reshape(2, rows // 2, cols)


def _full_from_slots(name, slots, shard_shape):
    if name in ROW_CUT and len(shard_shape) == 2:
        return slots.reshape((N_CHIPS * shard_shape[0], shard_shape[1]))
    parts = slots.reshape((N_CHIPS,) + shard_shape)
    axis = len(shard_shape) - (2 if name in ROW_CUT else 1)
    return jnp.concatenate([parts[i] for i in range(N_CHIPS)], axis=axis)


def _slots_from_full(name, full, shard_shape):
    if name in ROW_CUT and len(shard_shape) == 2:
        return full.reshape(N_CHIPS, 2, shard_shape[0] // 2, shard_shape[1])
    axis = len(shard_shape) - (2 if name in ROW_CUT else 1)
    parts = jnp.stack(jnp.split(full, N_CHIPS, axis=axis))
    cols = shard_shape[-1]
    return parts.reshape(N_CHIPS, 2, -1, cols)


def _pack_small(cfg, vals):
    flat = jnp.concatenate([vals[k].reshape(-1).astype(F32) for k in SMALL])
    pad = (-flat.shape[0]) % (LANES * LANES)
    return jnp.pad(flat, (0, pad)).reshape(-1, LANES)


def _unpack_small(packed, like):
    flat, out, off = packed.reshape(-1), {}, 0
    for k in SMALL:
        n = like[k].size
        out[k] = flat[off:off + n].reshape(like[k].shape)
        off += n
    return out


class LayerComm:
    def __init__(self, cfg, w, m, v):
        self.cfg, self.w = cfg, w
        self.shards = [{k: _shard_2d(w[k][l].astype(BF16)) for k in BIG} for l in range(cfg.depth)]
        self.got = {LATE: _gather_now([self.shards[0][k] for k in LATE])}
        self.sets = {}
        self.pending = None
        rows = lambda a: a.reshape(a.shape[0], -1, a.shape[-1])
        self.state = {k: [rows(a[k]) for a in (w, m, v)] for k in BIG}
        self.outs = {k: None for k in BIG}

    def weights(self, l, names):
        if names in self.got:
            got = self.got.pop(names)
        elif names is EARLY:
            got = self.sets.pop((l, EARLY[:2])).got + self.sets.pop((l, EARLY[2:])).got
        else:
            got = self.sets.pop((l, names)).got
        mine = (_chip(), 0, 0, 0)
        return {k: _full_from_slots(k, lax.dynamic_update_slice(g, self.shards[l][k][None], mine), self.w[k].shape[1:])
                for k, g in zip(names, got)}

    def forward_hosts(self, l):
        for names, hosts in ((EARLY[2:], ("fox_fwd", "mla_fwd")), (EARLY[:2], ("mla_fwd", "mem_fwd"))):
            self.sets[l, names] = GatherSet([self.shards[l][k] for k in names], hosts)
        if l + 1 < self.cfg.depth:
            self.sets[l + 1, LATE] = GatherSet([self.shards[l + 1][k] for k in LATE], ("mla_fwd", "mem_fwd"))
        return Together([s for (layer, _), s in self.sets.items() if layer in (l, l + 1)])

    def _reduce(self, tag, names, big, hosts):
        return ReduceLayer(tag, [_slots_from_full(k, big[k], self.w[k].shape[1:]) for k in names], hosts)

    def backward_hosts(self, l):
        comm = self

        class Riders(Together):
            def early_grads(self, grads):
                comm.early = comm._reduce(f"{l}e", EARLY, grads, ("fox_bwd", "mla_bwd", "mem_bwd"))
                self.members.append(comm.early)

        return Riders([self.pending[1]] if self.pending else [])

    def _update(self, l, names, reduce):
        for k, (mine, theirs) in zip(names, reduce.result()):
            self.outs[k] = _adamw_layer(f"adamw_{k}_{l}", l, *self.state[k], mine, theirs, self.outs[k])

    def grads(self, l, big):
        self._update(l, EARLY, self.early)
        if self.pending:
            self._update(self.pending[0], LATE, self.pending[1])
        self.pending = (l, self._reduce(f"{l}l", LATE, big, ("ff2_bwd_x", "fox_bwd", "mla_bwd")))

    def finish(self):
        self.pending[1].run_now()
        self._update(self.pending[0], LATE, self.pending[1])
        return {k: tuple(a.reshape(self.w[k].shape) for a in self.outs[k]) for k in BIG}


def _step(cfg, x, mem, positions, target, w, m, v):
    comm = LayerComm(cfg, w, m, v)
    small_w = {k: w[k] for k in SMALL}
    loss_local, dx, small_g = _local_step(cfg, x, mem, positions, target, small_w, comm)
    loss = lax.psum(loss_local, ("x", "y", "c"))
    outs_big = comm.finish()

    g_small = _allreduce_small(_pack_small(cfg, small_g))
    packs = [_pack_small(cfg, {k: d_[k] for k in SMALL}) for d_ in (w, m, v)]
    dl, nm, nv = _slabwise("adamw_small", _adamw_math, [a[None] for a in (packs[0], g_small, packs[1], packs[2])],
                           [F32, F32, F32])
    outs_small = [_unpack_small(a[0] if a.ndim == 3 else a, w) for a in (g_small, dl, nm, nv)]

    names = SMALL[:2] + ("w_in", "b_forget", "w_uq", "g_cq", "w_ukv", "g_ckv", "w_mem_kv", "w_br", "w_out",
                         "ln1_g", "ln1_b", "w_ff1", "w_ff2", "ln2_g", "ln2_b")
    result = [loss, dx]
    for part in range(4):
        for k in names:
            result.append(outs_big[k][part] if k in outs_big else outs_small[part][k])
    return tuple(result)


def kernel(x, mem, positions, ln_in_g, ln_in_b, w_in, b_forget, w_uq, g_cq, w_ukv, g_ckv, w_mem_kv, w_br, w_out, ln1_g, ln1_b, w_ff1, w_ff2, ln2_g, ln2_b, loss_target, m_ln_in_g, m_ln_in_b, m_w_in, m_b_forget, m_w_uq, m_g_cq, m_w_ukv, m_g_ckv, m_w_mem_kv, m_w_br, m_w_out, m_ln1_g, m_ln1_b, m_w_ff1, m_w_ff2, m_ln2_g, m_ln2_b, v_ln_in_g, v_ln_in_b, v_w_in, v_b_forget, v_w_uq, v_g_cq, v_w_ukv, v_g_ckv, v_w_mem_kv, v_w_br, v_w_out, v_ln1_g, v_ln1_b, v_w_ff1, v_w_ff2, v_ln2_g, v_ln2_b):
    w = dict(ln_in_g=ln_in_g, ln_in_b=ln_in_b, w_in=w_in, b_forget=b_forget, w_uq=w_uq, g_cq=g_cq, w_ukv=w_ukv,
             g_ckv=g_ckv, w_mem_kv=w_mem_kv, w_br=w_br, w_out=w_out, ln1_g=ln1_g, ln1_b=ln1_b, w_ff1=w_ff1,
             w_ff2=w_ff2, ln2_g=ln2_g, ln2_b=ln2_b)
    m = dict(ln_in_g=m_ln_in_g, ln_in_b=m_ln_in_b, w_in=m_w_in, b_forget=m_b_forget, w_uq=m_w_uq, g_cq=m_g_cq,
             w_ukv=m_w_ukv, g_ckv=m_g_ckv, w_mem_kv=m_w_mem_kv, w_br=m_w_br, w_out=m_w_out, ln1_g=m_ln1_g,
             ln1_b=m_ln1_b, w_ff1=m_w_ff1, w_ff2=m_w_ff2, ln2_g=m_ln2_g, ln2_b=m_ln2_b)
    v = dict(ln_in_g=v_ln_in_g, ln_in_b=v_ln_in_b, w_in=v_w_in, b_forget=v_b_forget, w_uq=v_w_uq, g_cq=v_g_cq,
             w_ukv=v_w_ukv, g_ckv=v_g_ckv, w_mem_kv=v_w_mem_kv, w_br=v_w_br, w_out=v_w_out, ln1_g=v_ln1_g,
             ln1_b=v_ln1_b, w_ff1=v_w_ff1, w_ff2=v_w_ff2, ln2_g=v_ln2_g, ln2_b=v_ln2_b)
    return _step(Cfg(), x, mem, positions, loss_target, w, m, v)
```
